```python
import math
import jax, jax.numpy as jnp
from jax import lax
import numpy as np

D_MODEL = 1024
BATCH = 8
SEQ = 4096
DEPTH = 2

GRID_W = 64
CTX_LEN = 256
EPS = 1e-6

SG_WIDTH = 1024
SG_CHUNK = 128
SG_GROUPS = 8
SG_GROUP_DIM = SG_WIDTH // SG_GROUPS

DN_HEADS = 8
DN_DIM = 128
DN_WIDTH = DN_HEADS * DN_DIM
DN_CONV = 5
DN_CHUNK = 64

AT_Q_HEADS = 16
AT_KV_HEADS = 2
AT_DIM = 64
AT_Q_WIDTH = AT_Q_HEADS * AT_DIM
AT_KV_WIDTH = AT_KV_HEADS * AT_DIM
AT_WINDOW = 128
AT_BLOCK = 128
ROPE_BASE = 10000.0

N_EXPERTS = 32
TOP_K = 4
D_EXPERT = 1024
SWIGLU_ALPHA = 1.702
SWIGLU_LIMIT = 7.0
MOE_BLOCK = 128

N_BRANCH = 3

IN_COLS = (("dn_k", DN_WIDTH), ("dn_v", DN_WIDTH), ("dn_a", 2 * DN_HEADS), ("dn_b", 2 * DN_HEADS),
           ("at_k", AT_KV_WIDTH), ("at_v", AT_KV_WIDTH),
           ("dn_q", DN_WIDTH), ("dn_g", DN_WIDTH), ("at_q", AT_Q_WIDTH),
           ("sg_u", SG_WIDTH), ("sg_v", SG_WIDTH), ("gates", N_BRANCH * D_MODEL))
N_CTX_COLS = 2 * DN_WIDTH + 4 * DN_HEADS + 2 * AT_KV_WIDTH
D_IN = N_CTX_COLS + 2 * DN_WIDTH + AT_Q_WIDTH + 2 * SG_WIDTH + N_BRANCH * D_MODEL

kernel_name = "hybrid_gated_parallel_mixer_moe_dit"

F32 = jnp.float32


def _rms_norm(x, g):
    xf = x.astype(F32)
    y = xf * lax.rsqrt(jnp.mean(xf * xf, axis=-1, keepdims=True) + EPS)
    return (y * g.astype(F32)).astype(x.dtype)


def _layer_norm(x, g, b):
    xf = x.astype(F32)
    xc = xf - jnp.mean(xf, axis=-1, keepdims=True)
    var = jnp.mean(xc * xc, axis=-1, keepdims=True)
    return (xc * lax.rsqrt(var + EPS) * g.astype(F32) + b.astype(F32)).astype(x.dtype)


def _l2_norm(x):
    xf = x.astype(F32)
    return (xf * lax.rsqrt(jnp.sum(xf * xf, axis=-1, keepdims=True) + EPS)).astype(x.dtype)


def _split_cols(p):
    out, off = {}, 0
    for name, width in IN_COLS:
        if off + width > p.shape[-1]:
            break
        out[name] = p[..., off:off + width]
        off += width
    return out


def _short_conv(x, w):
    K, T, pad = w.shape[0], x.shape[1], w.shape[0] // 2
    xp = jnp.pad(x, ((0, 0), (pad, pad), (0, 0)))
    y = xp[:, 0:T] * w[0]
    for i in range(1, K):
        y = y + xp[:, i:i + T] * w[i]
    return y


def _axial_rope(x, row, col):
    half = AT_DIM // 2
    nf = half // 2
    inv_freq = ROPE_BASE ** (-jnp.arange(nf, dtype=F32) / nf)
    xf = x.astype(F32)

    def rotate(xs, pos):
        ang = pos.astype(F32)[:, None] * inv_freq
        cos, sin = jnp.cos(ang)[None, :, None, :], jnp.sin(ang)[None, :, None, :]
        x1, x2 = xs[..., :nf], xs[..., nf:]
        return jnp.concatenate([x1 * cos - x2 * sin, x2 * cos + x1 * sin], axis=-1)

    return jnp.concatenate([rotate(xf[..., :half], row), rotate(xf[..., half:], col)], axis=-1).astype(x.dtype)


def _softmax_with_sink(logits, sink):
    sink_col = jnp.broadcast_to(sink[None, :, :, None, None], logits.shape[:-1] + (1,))
    return jax.nn.softmax(jnp.concatenate([logits, sink_col], axis=-1), axis=-1)[..., :-1]


def _sgu_branch(cols, ln_g, ln_b, w_s, b_s):
    u = jax.nn.gelu(cols["sg_u"])
    v = _layer_norm(jax.nn.gelu(cols["sg_v"]), ln_g, ln_b)
    B, T, _ = v.shape
    vb = v.reshape(B, T // SG_CHUNK, SG_CHUNK, SG_GROUPS, SG_GROUP_DIM)
    mixed = jnp.einsum("gpq,bnqgc->bnpgc", w_s, vb) + b_s.T[:, :, None]
    return u * mixed.reshape(B, T, SG_WIDTH)


def _gated_delta_chunked(q, k, v, log_decay, beta, s0, with_output):
    B, T, H, dk = k.shape
    dv = v.shape[-1]
    C = DN_CHUNK
    n = T // C

    def chunks(t):
        t = t.astype(F32).reshape((B, n, C, H) + t.shape[3:])
        return jnp.moveaxis(jnp.swapaxes(t, 2, 3), 1, 0)

    kc, vc, bc = chunks(k), chunks(v), chunks(beta)
    gam = jnp.cumsum(chunks(log_decay), axis=-1)
    idx = jnp.arange(C)
    incl = idx[:, None] >= idx[None, :]
    decay = jnp.exp(jnp.where(incl, gam[..., :, None] - gam[..., None, :], -jnp.inf))
    a_mat = (jnp.where(idx[:, None] > idx[None, :], decay, 0.0) * bc[..., :, None]
             * jnp.einsum("nbhid,nbhjd->nbhij", kc, kc))
    rhs = jnp.concatenate([kc * (bc * jnp.exp(gam))[..., None], vc * bc[..., None]], axis=-1)
    sol = lax.linalg.triangular_solve(a_mat + jnp.eye(C, dtype=F32), rhs,
                                      left_side=True, lower=True, unit_diagonal=True)
    w, u0 = sol[..., :dk], sol[..., dk:]
    k_end = kc * jnp.exp(gam[..., -1:] - gam)[..., None]
    chunk_decay = jnp.exp(gam[..., -1])[..., None, None]
    if with_output:
        qc = chunks(q)
        qk = jnp.einsum("nbhid,nbhjd->nbhij", qc, kc) * decay
        q_start = qc * jnp.exp(gam)[..., None]
        xs = (w, u0, k_end, chunk_decay, qk, q_start)
    else:
        xs = (w, u0, k_end, chunk_decay)

    def step(s, blk):
        u = blk[1] - jnp.einsum("bhcd,bhde->bhce", blk[0], s)
        s_new = blk[3] * s + jnp.einsum("bhcd,bhce->bhde", blk[2], u)
        if not with_output:
            return s_new, None
        o = jnp.einsum("bhcd,bhde->bhce", blk[5], s) + jnp.einsum("bhij,bhje->bhie", blk[4], u)
        return s_new, o

    s_fin, o = lax.scan(step, s0, xs)
    if with_output:
        o = jnp.swapaxes(jnp.moveaxis(o, 0, 1), 2, 3).reshape(B, T, H, dv).astype(v.dtype)
    return o, s_fin


def _dn_inputs(cols, conv_w, a_log, dt_bias, with_q):
    B, T, _ = cols["dn_k"].shape

    def conv_heads(name, part):
        y = jax.nn.silu(_short_conv(cols[name], conv_w[:, part * DN_WIDTH:(part + 1) * DN_WIDTH]))
        return y.reshape(B, T, DN_HEADS, DN_DIM)

    k = _l2_norm(conv_heads("dn_k", 1))
    v = conv_heads("dn_v", 2)
    a = cols["dn_a"].reshape(B, T, 2, DN_HEADS).astype(F32)
    log_decay = -jnp.exp(a_log.astype(F32)) * jax.nn.softplus(a + dt_bias.astype(F32))
    beta = jax.nn.sigmoid(cols["dn_b"].reshape(B, T, 2, DN_HEADS).astype(F32))
    q = _l2_norm(conv_heads("dn_q", 0)) * DN_DIM ** -0.5 if with_q else None
    return q, k, v, log_decay, beta


def _flip(t):
    return None if t is None else jnp.flip(t, axis=1)


def _ident(t):
    return t


def _deltanet_branch(cols, cols_c, conv_w, a_log, dt_bias, norm_g, need_ctx_out):
    q, k, v, la, be = _dn_inputs(cols, conv_w, a_log, dt_bias, True)
    qc, kc, vc, lac, bec = _dn_inputs(cols_c, conv_w, a_log, dt_bias, need_ctx_out)
    B, T = k.shape[0], k.shape[1]
    s0 = jnp.zeros((B, DN_HEADS, DN_DIM, DN_DIM), F32)
    o_lat, o_ctx = None, None
    for d in range(2):
        tr = _ident if d == 0 else _flip
        oc, sc = _gated_delta_chunked(tr(qc), tr(kc), tr(vc), tr(lac[:, :, d]), tr(bec[:, :, d]), s0, need_ctx_out)
        ol, _ = _gated_delta_chunked(tr(q), tr(k), tr(v), tr(la[:, :, d]), tr(be[:, :, d]), sc, True)
        o_lat = tr(ol) if o_lat is None else o_lat + tr(ol)
        if need_ctx_out:
            o_ctx = tr(oc) if o_ctx is None else o_ctx + tr(oc)

    def finish(o, c):
        Bt, Tt = o.shape[0], o.shape[1]
        g = jax.nn.silu(c["dn_g"].reshape(Bt, Tt, DN_HEADS, DN_DIM))
        return (_rms_norm(o, norm_g) * g).reshape(Bt, Tt, DN_WIDTH)

    return finish(o_lat, cols), (finish(o_ctx, cols_c) if need_ctx_out else None)


def _window_attention(q, k, v, kc, vc, sink):
    B, S, Hkv, G, dh = q.shape
    P, W = AT_BLOCK, AT_WINDOW
    nb = S // P
    span = P + 2 * W
    qb = jnp.moveaxis(q.reshape(B, nb, P, Hkv, G, dh), 1, 0)
    key_idx = jnp.arange(nb)[:, None] * P + jnp.arange(span)[None, :]
    pad = ((0, 0), (W, W), (0, 0), (0, 0))
    kb = jnp.moveaxis(jnp.pad(k, pad)[:, key_idx], 1, 0)
    vb = jnp.moveaxis(jnp.pad(v, pad)[:, key_idx], 1, 0)
    q_pos = jnp.arange(nb)[:, None] * P + jnp.arange(P)[None, :]
    k_pos = key_idx - W
    mask = ((jnp.abs(k_pos[:, None, :] - q_pos[:, :, None]) <= W)
            & (k_pos[:, None, :] >= 0) & (k_pos[:, None, :] < S))

    def one_block(blk):
        qn, kn, vn, mn = blk
        s_loc = jnp.where(mn[None, None, None], jnp.einsum("bqhgd,bkhd->bhgqk", qn, kn).astype(F32), -jnp.inf)
        s_ctx = jnp.einsum("bqhgd,bkhd->bhgqk", qn, kc).astype(F32)
        pr = _softmax_with_sink(jnp.concatenate([s_loc, s_ctx], axis=-1), sink).astype(vn.dtype)
        return (jnp.einsum("bhgqk,bkhd->bqhgd", pr[..., :span], vn)
                + jnp.einsum("bhgqk,bkhd->bqhgd", pr[..., span:], vc))

    out = lax.map(one_block, (qb, kb, vb, mask))
    return jnp.moveaxis(out, 0, 1).reshape(B, S, Hkv * G * dh)


def _attention_branch(cols, cols_c, sinks, row, col, need_ctx_out):
    B, S, _ = cols["at_k"].shape
    L = cols_c["at_k"].shape[1]
    G = AT_Q_HEADS // AT_KV_HEADS
    scale = AT_DIM ** -0.5
    q = _axial_rope(cols["at_q"].reshape(B, S, AT_Q_HEADS, AT_DIM), row, col) * scale
    k = _axial_rope(cols["at_k"].reshape(B, S, AT_KV_HEADS, AT_DIM), row, col)
    v = cols["at_v"].reshape(B, S, AT_KV_HEADS, AT_DIM)
    kc = cols_c["at_k"].reshape(B, L, AT_KV_HEADS, AT_DIM)
    vc = cols_c["at_v"].reshape(B, L, AT_KV_HEADS, AT_DIM)
    sink = sinks.astype(F32).reshape(AT_KV_HEADS, G)
    y = _window_attention(q.reshape(B, S, AT_KV_HEADS, G, AT_DIM), k, v, kc, vc, sink)
    y_c = None
    if need_ctx_out:
        qc = cols_c["at_q"].reshape(B, L, AT_KV_HEADS, G, AT_DIM) * scale
        pr = _softmax_with_sink(jnp.einsum("bqhgd,bkhd->bhgqk", qc, kc).astype(F32), sink)
        y_c = jnp.einsum("bhgqk,bkhd->bqhgd", pr.astype(vc.dtype), vc).reshape(B, L, AT_Q_WIDTH)
    return y, y_c


def _merge(gate_cols, y_sg, y_dn, y_at, lp):
    g = jax.nn.sigmoid(gate_cols.reshape(gate_cols.shape[:-1] + (N_BRANCH, D_MODEL)))
    m = (g[..., 0, :] * (y_sg @ lp["w_proj_sg"]) + g[..., 1, :] * (y_dn @ lp["w_proj_dn"])
         + g[..., 2, :] * (y_at @ lp["w_proj_at"]))
    return m @ lp["w_out"]


def _mixer(h, hc, lp, row, col, need_ctx_out):
    cols = _split_cols(h @ lp["w_in"])
    cols_c = _split_cols(hc @ (lp["w_in"] if need_ctx_out else lp["w_in"][:, :N_CTX_COLS]))
    y_dn, y_dn_c = _deltanet_branch(cols, cols_c, lp["dn_conv_w"], lp["dn_a_log"], lp["dn_dt_bias"],
                                    lp["dn_norm_g"], need_ctx_out)
    y_at, y_at_c = _attention_branch(cols, cols_c, lp["at_sinks"], row, col, need_ctx_out)
    y_sg = _sgu_branch(cols, lp["sg_ln_g"], lp["sg_ln_b"], lp["sg_w"], lp["sg_b"])
    out = _merge(cols["gates"], y_sg, y_dn, y_at, lp)
    out_c = None
    if need_ctx_out:
        y_sg_c = _sgu_branch(cols_c, lp["sg_ln_g"], lp["sg_ln_b"], lp["sg_w"], lp["sg_b"])
        out_c = _merge(cols_c["gates"], y_sg_c, y_dn_c, y_at_c, lp)
    return out, out_c


def _clamped_swiglu(gu):
    g = jnp.minimum(gu[..., :D_EXPERT], SWIGLU_LIMIT)
    lin = jnp.clip(gu[..., D_EXPERT:], -SWIGLU_LIMIT, SWIGLU_LIMIT)
    return g * jax.nn.sigmoid(SWIGLU_ALPHA * g) * (lin + 1.0)


def _moe(h, lp):
    N, D = h.shape
    logits = (h @ lp["router_w"] + lp["router_b"]).astype(F32)
    top_val, top_idx = lax.top_k(logits, TOP_K)
    gate = jax.nn.softmax(top_val, axis=-1)
    M = N * TOP_K
    e_flat = top_idx.reshape(M)
    order = jnp.argsort(e_flat)
    e_sorted = e_flat[order]
    counts = jnp.bincount(e_flat, length=N_EXPERTS)
    padded = (counts + MOE_BLOCK - 1) // MOE_BLOCK * MOE_BLOCK
    pad_end = jnp.cumsum(padded)
    dest = (pad_end - padded)[e_sorted] + jnp.arange(M) - (jnp.cumsum(counts) - counts)[e_sorted]
    n_blocks = (M + MOE_BLOCK - 1) // MOE_BLOCK + N_EXPERTS
    n_slots = n_blocks * MOE_BLOCK
    slot_tok = jnp.full((n_slots,), N, jnp.int32).at[dest].set((order // TOP_K).astype(jnp.int32))
    slot_gate = jnp.zeros((n_slots,), F32).at[dest].set(gate.reshape(M)[order])
    block_expert = jnp.minimum(jnp.searchsorted(pad_end, jnp.arange(n_blocks) * MOE_BLOCK, side="right"),
                               N_EXPERTS - 1)
    h_pad = jnp.concatenate([h, jnp.zeros((1, D), h.dtype)], axis=0)

    def expert_block(acc, blk):
        tok, gw, e = blk
        gu = h_pad[tok] @ lp["exp_w_gu"][e] + lp["exp_b_gu"][e]
        y = _clamped_swiglu(gu) @ lp["exp_w_down"][e] + lp["exp_b_down"][e]
        return acc.at[tok].add((y * gw[:, None]).astype(acc.dtype)), None

    acc, _ = lax.scan(expert_block, jnp.zeros((N + 1, D), h.dtype),
                      (slot_tok.reshape(n_blocks, MOE_BLOCK), slot_gate.reshape(n_blocks, MOE_BLOCK), block_expert))
    return acc[:N]


def _layer(x, xc, mod, mod_c, lp, row, col, need_ctx_out):
    sh1, sc1, g1, sh2, sc2, g2 = jnp.split(mod, 6, axis=-1)
    sh1c, sc1c, g1c, sh2c, sc2c, g2c = jnp.split(mod_c, 6, axis=-1)
    h = _rms_norm(x, lp["norm_pre_mix"]) * (1.0 + sc1) + sh1
    hc = _rms_norm(xc, lp["norm_pre_mix"]) * (1.0 + sc1c) + sh1c
    y, yc = _mixer(h, hc, lp, row, col, need_ctx_out)
    x = x + g1 * _rms_norm(y, lp["norm_post_mix"])
    B, S, D = x.shape
    h2 = _rms_norm(x, lp["norm_pre_ffn"]) * (1.0 + sc2) + sh2
    if need_ctx_out:
        xc = xc + g1c * _rms_norm(yc, lp["norm_post_mix"])
        L = xc.shape[1]
        h2c = _rms_norm(xc, lp["norm_pre_ffn"]) * (1.0 + sc2c) + sh2c
        yf = _moe(jnp.concatenate([h2.reshape(B * S, D), h2c.reshape(B * L, D)], axis=0), lp)
        y2 = yf[:B * S].reshape(B, S, D)
        xc = xc + g2c * _rms_norm(yf[B * S:].reshape(B, L, D), lp["norm_post_ffn"])
    else:
        y2 = _moe(h2.reshape(B * S, D), lp).reshape(B, S, D)
    x = x + g2 * _rms_norm(y2, lp["norm_post_ffn"])
    return x, xc


def _normal(k, shape, scale):
    return jax.random.normal(k, shape, F32) * scale


def setup_inputs(seed: int = 0) -> dict:
    key = jax.random.key(seed)
    ks = jax.random.split(key, 32)
    Dd, H = DEPTH, DN_HEADS
    dt = jnp.exp(jax.random.uniform(ks[15], (Dd, 2, H), F32, math.log(1e-3), math.log(1e-1)))
    return {
        "x": _normal(ks[0], (BATCH, SEQ, D_MODEL), 1.0),
        "c": _normal(ks[1], (BATCH, D_MODEL), 1.0),
        "ctx": _normal(ks[2], (BATCH, CTX_LEN, D_MODEL), 1.0),
        "c_ctx": _normal(ks[3], (D_MODEL,), 1.0),
        "w_mod": _normal(ks[4], (Dd, D_MODEL, 6 * D_MODEL), 0.5 * D_MODEL ** -0.5),
        "b_mod": _normal(ks[5], (Dd, 6 * D_MODEL), 0.01),
        "norm_pre_mix": 1.0 + _normal(ks[6], (Dd, D_MODEL), 0.05),
        "norm_post_mix": 1.0 + _normal(ks[7], (Dd, D_MODEL), 0.05),
        "norm_pre_ffn": 1.0 + _normal(ks[8], (Dd, D_MODEL), 0.05),
        "norm_post_ffn": 1.0 + _normal(ks[9], (Dd, D_MODEL), 0.05),
        "w_in": _normal(ks[10], (Dd, D_MODEL, D_IN), D_MODEL ** -0.5),
        "sg_ln_g": 1.0 + _normal(ks[11], (Dd, SG_WIDTH), 0.05),
        "sg_ln_b": _normal(ks[12], (Dd, SG_WIDTH), 0.01),
        "sg_w": _normal(ks[13], (Dd, SG_GROUPS, SG_CHUNK, SG_CHUNK), SG_CHUNK ** -0.5),
        "sg_b": 1.0 + _normal(ks[14], (Dd, SG_GROUPS, SG_CHUNK), 0.1),
        "dn_conv_w": _normal(ks[16], (Dd, DN_CONV, 3 * DN_WIDTH), DN_CONV ** -0.5),
        "dn_a_log": jnp.log(jax.random.uniform(ks[17], (Dd, 2, H), F32, 1.0, 16.0)),
        "dn_dt_bias": dt + jnp.log(-jnp.expm1(-dt)),
        "dn_norm_g": 1.0 + _normal(ks[18], (Dd, DN_DIM), 0.05),
        "at_sinks": _normal(ks[19], (Dd, AT_Q_HEADS), 1.0),
        "w_proj_sg": _normal(ks[20], (Dd, SG_WIDTH, D_MODEL), SG_WIDTH ** -0.5),
        "w_proj_dn": _normal(ks[21], (Dd, DN_WIDTH, D_MODEL), DN_WIDTH ** -0.5),
        "w_proj_at": _normal(ks[22], (Dd, AT_Q_WIDTH, D_MODEL), AT_Q_WIDTH ** -0.5),
        "w_out": _normal(ks[23], (Dd, D_MODEL, D_MODEL), D_MODEL ** -0.5),
        "router_w": _normal(ks[24], (Dd, D_MODEL, N_EXPERTS), D_MODEL ** -0.5),
        "router_b": _normal(ks[25], (Dd, N_EXPERTS), 0.01),
        "exp_w_gu": _normal(ks[26], (Dd, N_EXPERTS, D_MODEL, 2 * D_EXPERT), D_MODEL ** -0.5),
        "exp_b_gu": _normal(ks[27], (Dd, N_EXPERTS, 2 * D_EXPERT), 0.01),
        "exp_w_down": _normal(ks[28], (Dd, N_EXPERTS, D_EXPERT, D_MODEL), D_EXPERT ** -0.5),
        "exp_b_down": _normal(ks[29], (Dd, N_EXPERTS, D_MODEL), 0.01),
    }


def reference(x, c, ctx, c_ctx, w_mod, b_mod, norm_pre_mix, norm_post_mix, norm_pre_ffn, norm_post_ffn,
              w_in, sg_ln_g, sg_ln_b, sg_w, sg_b, dn_conv_w, dn_a_log, dn_dt_bias, dn_norm_g, at_sinks,
              w_proj_sg, w_proj_dn, w_proj_at, w_out, router_w, router_b, exp_w_gu, exp_b_gu, exp_w_down,
              exp_b_down):
    S = x.shape[1]
    ROWS = S // GRID_W
    row = jnp.repeat(jnp.arange(ROWS, dtype=jnp.int32), GRID_W)
    col = jnp.tile(jnp.arange(GRID_W, dtype=jnp.int32), ROWS)
    params = {
        "w_mod": w_mod, "b_mod": b_mod, "norm_pre_mix": norm_pre_mix, "norm_post_mix": norm_post_mix,
        "norm_pre_ffn": norm_pre_ffn, "norm_post_ffn": norm_post_ffn, "w_in": w_in,
        "sg_ln_g": sg_ln_g, "sg_ln_b": sg_ln_b, "sg_w": sg_w, "sg_b": sg_b,
        "dn_conv_w": dn_conv_w, "dn_a_log": dn_a_log, "dn_dt_bias": dn_dt_bias, "dn_norm_g": dn_norm_g,
        "at_sinks": at_sinks, "w_proj_sg": w_proj_sg, "w_proj_dn": w_proj_dn, "w_proj_at": w_proj_at,
        "w_out": w_out, "router_w": router_w, "router_b": router_b, "exp_w_gu": exp_w_gu,
        "exp_b_gu": exp_b_gu, "exp_w_down": exp_w_down, "exp_b_down": exp_b_down,
    }
    xc = ctx
    for l in range(DEPTH):
        lp = {name: p[l] for name, p in params.items()}
        mod = (jax.nn.silu(c) @ lp["w_mod"] + lp["b_mod"])[:, None, :]
        mod_c = jax.nn.silu(c_ctx) @ lp["w_mod"] + lp["b_mod"]
        x, xc = _layer(x, xc, mod, mod_c, lp, row, col, l < DEPTH - 1)
    return x
```

```python
import functools
import math

import jax
import jax.numpy as jnp
from jax import lax
from jax.experimental import pallas as pl
from jax.experimental.pallas import tpu as pltpu

F32 = jnp.float32
BF16 = jnp.bfloat16
I32 = jnp.int32

EPS = 1e-6
D_MODEL = 1024
GRID_W = 64

SG_CHUNK = 128
SG_GROUPS = 8

DN_HEADS = 8
DN_DIM = 128
DN_CONV = 5
DN_CHUNK = 64
DN_HALO = 16

AT_Q_HEADS = 16
AT_KV_HEADS = 2
AT_DIM = 64
AT_BLOCK = 128
ROPE_BASE = 10000.0

N_EXPERTS = 32
TOP_K = 4
D_EXPERT = 1024
SWIGLU_ALPHA = 1.702
SWIGLU_LIMIT = 7.0
N_BRANCH = 3

LANES = 128
NEG_BIG = -1e30

COL_DN_K, COL_DN_V, COL_DN_Q, COL_DN_G, COL_AT_Q, COL_SG_U, COL_SG_V, COL_GATE0 = range(8)
N_MAIN_COLS = 10 * D_MODEL
N_CTX_MAIN_COLS = 2 * D_MODEL
N_SMALL_COLS = 3 * LANES

VMEM_LIMIT = 52 * 1024 * 1024


def _cparams(sem):
    return pltpu.CompilerParams(dimension_semantics=sem, vmem_limit_bytes=VMEM_LIMIT)


def _dot(a, b):
    return jnp.dot(a, b, preferred_element_type=F32)


def _dot_nt(a, b):
    return lax.dot_general(a, b, (((1,), (1,)), ((), ())), preferred_element_type=F32)


def _dot_tn(a, b):
    return lax.dot_general(a, b, (((0,), (0,)), ((), ())), preferred_element_type=F32)


def _sigmoid(x):
    return 1.0 / (1.0 + jnp.exp(-x))


def _silu(x):
    return x * _sigmoid(x)


def _gelu_tanh(x):
    return 0.5 * x * (1.0 + jnp.tanh(math.sqrt(2.0 / math.pi) * (x + 0.044715 * (x * x * x))))


def _softplus(x):
    return jnp.maximum(x, 0.0) + jnp.log(1.0 + jnp.exp(-jnp.abs(x)))


def _rms(x, g):
    return x * lax.rsqrt(jnp.mean(x * x, axis=-1, keepdims=True) + EPS) * g


def _mod_kernel(c_ref, w_ref, b_ref, o_ref):
    s = _silu(c_ref[...])
    o_ref[...] = jnp.dot(s, w_ref[...], preferred_element_type=F32,
                         precision=lax.Precision.HIGHEST) + b_ref[...]


def _modulation(cvec, w_mod, b_mod):
    depth = w_mod.shape[0]
    rows = cvec.shape[0]
    n_col = w_mod.shape[2] // D_MODEL
    return pl.pallas_call(
        _mod_kernel,
        grid=(depth, n_col),
        in_specs=[pl.BlockSpec((rows, D_MODEL), lambda l, j: (0, 0)),
                  pl.BlockSpec((None, D_MODEL, D_MODEL), lambda l, j: (l, 0, j)),
                  pl.BlockSpec((None, 1, D_MODEL), lambda l, j: (l, 0, j))],
        out_specs=pl.BlockSpec((None, rows, D_MODEL), lambda l, j: (l, 0, j)),
        out_shape=jax.ShapeDtypeStruct((depth, rows, w_mod.shape[2]), F32),
        compiler_params=_cparams(("arbitrary", "arbitrary")),
        name="modulation",
    )(cvec, w_mod, b_mod.reshape(depth, 1, -1))


def _inproj_kernel(x_ref, mod_ref, g_ref, wm_ref, ws_ref, main_ref, small_ref, h_ref):
    @pl.when(pl.program_id(1) == 0)
    def _():
        sh = mod_ref[:, 0 * D_MODEL:1 * D_MODEL]
        sc = mod_ref[:, 1 * D_MODEL:2 * D_MODEL]
        h = (_rms(x_ref[...], g_ref[...]) * (1.0 + sc) + sh).astype(BF16)
        h_ref[...] = h
        small_ref[...] = _dot(h, ws_ref[...])

    main_ref[...] = _dot(h_ref[...], wm_ref[...]).astype(BF16)


def _inproj(x, mod, mod_row, norm_g, w_main, w_small, tm, tn=1024):
    n_tok = x.shape[0]
    n_main = w_main.shape[1]
    return pl.pallas_call(
        _inproj_kernel,
        grid=(n_tok // tm, n_main // tn),
        in_specs=[pl.BlockSpec((tm, D_MODEL), lambda i, j: (i, 0)),
                  pl.BlockSpec((None, 1, 6 * D_MODEL), lambda i, j: (mod_row(i * tm), 0, 0)),
                  pl.BlockSpec((1, D_MODEL), lambda i, j: (0, 0)),
                  pl.BlockSpec((D_MODEL, tn), lambda i, j: (0, j)),
                  pl.BlockSpec((D_MODEL, N_SMALL_COLS), lambda i, j: (0, 0))],
        out_specs=[pl.BlockSpec((tm, tn), lambda i, j: (i, j)),
                   pl.BlockSpec((tm, N_SMALL_COLS), lambda i, j: (i, 0))],
        out_shape=[jax.ShapeDtypeStruct((n_tok, n_main), BF16),
                   jax.ShapeDtypeStruct((n_tok, N_SMALL_COLS), F32)],
        scratch_shapes=[pltpu.VMEM((tm, D_MODEL), BF16)],
        compiler_params=_cparams(("arbitrary", "arbitrary")),
        name="inproj",
    )(x, mod, norm_g, w_main, w_small)


def _sgu_kernel(u_ref, v_ref, lng_ref, lnb_ref, ws_ref, bs_ref, o_ref, *, n_chunk):
    u = _gelu_tanh(u_ref[...].astype(F32))
    v = _gelu_tanh(v_ref[...].astype(F32))
    vc = v - jnp.mean(v, axis=-1, keepdims=True)
    var = jnp.mean(vc * vc, axis=-1, keepdims=True)
    vn = (vc * lax.rsqrt(var + EPS) * lng_ref[...] + lnb_ref[...]).astype(BF16)
    for n in range(n_chunk):
        rows = slice(n * SG_CHUNK, (n + 1) * SG_CHUNK)
        for g in range(SG_GROUPS):
            cols = slice(g * LANES, (g + 1) * LANES)
            mixed = _dot(ws_ref[g], vn[rows, cols]) + bs_ref[:, g:g + 1]
            o_ref[rows, cols] = (u[rows, cols] * mixed).astype(BF16)


def _sgu(main, sg_ln_g, sg_ln_b, sg_w, sg_bt, n_chunk=2):
    n_tok = main.shape[0]
    tc = n_chunk * SG_CHUNK
    return pl.pallas_call(
        functools.partial(_sgu_kernel, n_chunk=n_chunk),
        grid=(n_tok // tc,),
        in_specs=[pl.BlockSpec((tc, D_MODEL), lambda i: (i, COL_SG_U)),
                  pl.BlockSpec((tc, D_MODEL), lambda i: (i, COL_SG_V)),
                  pl.BlockSpec((1, D_MODEL), lambda i: (0, 0)),
                  pl.BlockSpec((1, D_MODEL), lambda i: (0, 0)),
                  pl.BlockSpec((SG_GROUPS, SG_CHUNK, SG_CHUNK), lambda i: (0, 0, 0)),
                  pl.BlockSpec((SG_CHUNK, SG_GROUPS), lambda i: (0, 0))],
        out_specs=pl.BlockSpec((tc, D_MODEL), lambda i: (i, 0)),
        out_shape=jax.ShapeDtypeStruct((n_tok, D_MODEL), BF16),
        compiler_params=_cparams(("arbitrary",)),
        name="sgu",
    )(main, main, sg_ln_g, sg_ln_b, sg_w, sg_bt)


def _dn_kernel(*refs, with_q, n_chunks):
    if with_q:
        (qp_ref, qc_ref, qn_ref, kp_ref, kc_ref, kn_ref, vp_ref, vc_ref, vn_ref,
         gcol_ref, grow_ref, cw_ref, alog_r_ref, alog_c_ref, dtb_r_ref, dtb_c_ref, s0_ref,
         o_ref, sfin_ref, ext_ref, s_ref) = refs
    else:
        (kp_ref, kc_ref, kn_ref, vp_ref, vc_ref, vn_ref,
         gcol_ref, grow_ref, cw_ref, alog_r_ref, alog_c_ref, dtb_r_ref, dtb_c_ref, s0_ref,
         sfin_ref, ext_ref, s_ref) = refs
    d = pl.program_id(0)
    c = pl.program_id(2)
    is_fwd = d == 0
    cidx = jnp.where(is_fwd, c, n_chunks - 1 - c)
    C = DN_CHUNK

    @pl.when(c == 0)
    def _():
        s_ref[...] = s0_ref[...]

    has_prev = (cidx > 0).astype(F32)
    has_next = (cidx < n_chunks - 1).astype(F32)

    def conv_silu(p_ref, c_ref, n_ref, part):
        ext_ref[0:DN_HALO, :] = p_ref[...].astype(F32) * has_prev
        ext_ref[DN_HALO:DN_HALO + C, :] = c_ref[...].astype(F32)
        ext_ref[DN_HALO + C:2 * DN_HALO + C, :] = n_ref[...].astype(F32) * has_next
        base = DN_HALO - DN_CONV // 2
        y = None
        for i in range(DN_CONV):
            w = cw_ref[i:i + 1, part * D_MODEL:(part + 1) * D_MODEL]
            t = ext_ref[base + i:base + i + C, :] * w
            y = t if y is None else y + t
        return _silu(y)

    k_all = conv_silu(kp_ref, kc_ref, kn_ref, 1)
    v_all = conv_silu(vp_ref, vc_ref, vn_ref, 2)
    q_all = conv_silu(qp_ref, qc_ref, qn_ref, 0) if with_q else None

    gcol = gcol_ref[...]
    ld_col = -jnp.exp(alog_r_ref[...]) * _softplus(gcol[:, 0:DN_HEADS] + dtb_r_ref[...])
    beta_col = _sigmoid(gcol[:, DN_HEADS:2 * DN_HEADS])
    ld_row = -jnp.exp(alog_c_ref[...]) * _softplus(grow_ref[0:DN_HEADS, :] + dtb_c_ref[...])

    ri = lax.broadcasted_iota(I32, (C, C), 0)
    ci = lax.broadcasted_iota(I32, (C, C), 1)
    delta = (ri - ci) * (1 - 2 * d)
    incl = delta >= 0
    strict = delta > 0
    incl_t = delta <= 0
    gam_col = jnp.dot(incl.astype(F32), ld_col, preferred_element_type=F32,
                      precision=lax.Precision.HIGHEST)
    gam_row = jnp.dot(ld_row, incl_t.astype(F32), preferred_element_type=F32,
                      precision=lax.Precision.HIGHEST)
    gam_tot = jnp.sum(ld_col, axis=0, keepdims=True)
    eye = (ri == ci).astype(F32)

    for h in range(DN_HEADS):
        lanes = slice(h * DN_DIM, (h + 1) * DN_DIM)
        kh = k_all[:, lanes]
        kh = kh * lax.rsqrt(jnp.sum(kh * kh, axis=-1, keepdims=True) + EPS)
        vh = v_all[:, lanes]
        gc = gam_col[:, h:h + 1]
        gr = gam_row[h:h + 1, :]
        bc = beta_col[:, h:h + 1]
        decay = jnp.exp(jnp.where(incl, gc - gr, NEG_BIG))
        kb = kh.astype(BF16)
        kk = _dot_nt(kb, kb)
        a_mat = jnp.where(strict, decay, 0.0) * bc * kk
        x = -a_mat
        p = eye + x
        xb = x.astype(BF16)
        x = _dot(xb, xb)
        n_fac = int(math.log2(C)) - 1
        for j in range(n_fac):
            xb = x.astype(BF16)
            if j < n_fac - 1:
                r = _dot(xb, jnp.concatenate([xb, p.astype(BF16)], axis=1))
                x = r[:, :C]
                p = p + r[:, C:]
            else:
                p = p + _dot(xb, p.astype(BF16))
        rhs = jnp.concatenate([kh * (bc * jnp.exp(gc)), vh * bc], axis=1).astype(BF16)
        sol = _dot(p.astype(BF16), rhs)
        w = sol[:, :DN_DIM]
        u0 = sol[:, DN_DIM:]
        k_end = kh * jnp.exp(gam_tot[:, h:h + 1] - gc)
        s = s_ref[h]
        sb = s.astype(BF16)
        if with_q:
            qh = q_all[:, lanes]
            qh = qh * (lax.rsqrt(jnp.sum(qh * qh, axis=-1, keepdims=True) + EPS) * DN_DIM ** -0.5)
            qk = _dot_nt(qh.astype(BF16), kb) * decay
            q_start = qh * jnp.exp(gc)
            ws = _dot(jnp.concatenate([w, q_start], axis=0).astype(BF16), sb)
            u = u0 - ws[:C]
            ub = u.astype(BF16)
            o_ref[:, lanes] = (ws[C:] + _dot(qk.astype(BF16), ub)).astype(o_ref.dtype)
        else:
            u = u0 - _dot(w.astype(BF16), sb)
            ub = u.astype(BF16)
        s_ref[h] = jnp.exp(gam_tot[:, h:h + 1]) * s + _dot_tn(k_end.astype(BF16), ub)

    @pl.when(c == n_chunks - 1)
    def _():
        sfin_ref[...] = s_ref[...]


def _deltanet(main3, gate_col, gate_row, conv_w, alog, dtb, s0, with_q):
    B, T, _ = main3.shape
    C = DN_CHUNK
    n_chunks = T // C
    hpc = C // DN_HALO
    n_halo = T // DN_HALO

    def cix(d, c):
        return jnp.where(d == 0, c, n_chunks - 1 - c)

    def trio(col):
        return [pl.BlockSpec((None, DN_HALO, D_MODEL),
                             lambda d, b, c: (b, jnp.maximum(cix(d, c) * hpc - 1, 0), col)),
                pl.BlockSpec((None, C, D_MODEL), lambda d, b, c: (b, cix(d, c), col)),
                pl.BlockSpec((None, DN_HALO, D_MODEL),
                             lambda d, b, c: (b, jnp.minimum((cix(d, c) + 1) * hpc, n_halo - 1), col))]

    in_specs = (trio(COL_DN_Q) if with_q else []) + trio(COL_DN_K) + trio(COL_DN_V) + [
        pl.BlockSpec((None, None, C, 2 * DN_HEADS), lambda d, b, c: (d, b, cix(d, c), 0)),
        pl.BlockSpec((None, None, None, 2 * DN_HEADS, C), lambda d, b, c: (d, b, cix(d, c), 0, 0)),
        pl.BlockSpec((DN_CONV, 3 * D_MODEL), lambda d, b, c: (0, 0)),
        pl.BlockSpec((None, 1, DN_HEADS), lambda d, b, c: (d, 0, 0)),
        pl.BlockSpec((None, DN_HEADS, 1), lambda d, b, c: (d, 0, 0)),
        pl.BlockSpec((None, 1, DN_HEADS), lambda d, b, c: (d, 0, 0)),
        pl.BlockSpec((None, DN_HEADS, 1), lambda d, b, c: (d, 0, 0)),
        pl.BlockSpec((None, None, DN_HEADS, DN_DIM, DN_DIM), lambda d, b, c: (d, b, 0, 0, 0)),
    ]
    s_spec = pl.BlockSpec((None, None, DN_HEADS, DN_DIM, DN_DIM), lambda d, b, c: (d, b, 0, 0, 0))
    s_shape = jax.ShapeDtypeStruct((2, B, DN_HEADS, DN_DIM, DN_DIM), F32)
    if with_q:
        out_specs = [pl.BlockSpec((None, None, C, D_MODEL), lambda d, b, c: (d, b, cix(d, c), 0)), s_spec]
        out_shape = [jax.ShapeDtypeStruct((2, B, T, D_MODEL), BF16), s_shape]
    else:
        out_specs = [s_spec]
        out_shape = [s_shape]
    n_main = 3 if with_q else 2
    args = [main3] * (3 * n_main) + [
        gate_col, gate_row, conv_w,
        alog.reshape(2, 1, DN_HEADS), alog.reshape(2, DN_HEADS, 1),
        dtb.reshape(2, 1, DN_HEADS), dtb.reshape(2, DN_HEADS, 1), s0]
    out = pl.pallas_call(
        functools.partial(_dn_kernel, with_q=with_q, n_chunks=n_chunks),
        grid=(2, B, n_chunks),
        in_specs=in_specs, out_specs=out_specs, out_shape=out_shape,
        scratch_shapes=[pltpu.VMEM((C + 2 * DN_HALO, D_MODEL), F32),
                        pltpu.VMEM((DN_HEADS, DN_DIM, DN_DIM), F32)],
        compiler_params=_cparams(("arbitrary", "arbitrary", "arbitrary")),
        name="deltanet_q" if with_q else "deltanet_state",
    )(*args)
    return (out[0], out[1]) if with_q else (None, out[0])


def _rope_tables(S):
    half = AT_DIM // 2
    nf = half // 2
    inv_freq = ROPE_BASE ** (-jnp.arange(nf, dtype=F32) / nf)
    t = jnp.arange(S, dtype=jnp.int32)
    row = (t // GRID_W).astype(F32)
    col = (t % GRID_W).astype(F32)
    lane = jnp.arange(LANES)
    dd = lane % AT_DIM
    pos = jnp.where((dd < half)[None, :], row[:, None], col[:, None])
    ang = pos * inv_freq[lane % nf][None, :]
    first = ((lane % half) < nf)[None, :]
    sin = jnp.sin(ang)
    return jnp.cos(ang), jnp.where(first, -sin, 0.0), jnp.where(first, 0.0, sin)


def _rope_kernel(q_ref, k_ref, v_ref, cos_ref, sa_ref, sb_ref, qo_ref, ko_ref, vo_ref):
    cos, sa, sb = cos_ref[...], sa_ref[...], sb_ref[...]
    nf = AT_DIM // 4

    def rot(x):
        return x * cos + pltpu.roll(x, LANES - nf, 1) * sa + pltpu.roll(x, nf, 1) * sb

    for j in range(AT_Q_HEADS * AT_DIM // LANES):
        lanes = slice(j * LANES, (j + 1) * LANES)
        qo_ref[:, lanes] = (rot(q_ref[:, lanes].astype(F32)) * AT_DIM ** -0.5).astype(BF16)
    ko_ref[...] = rot(k_ref[...]).astype(BF16)
    vo_ref[...] = v_ref[...].astype(BF16)


def _rope(main, small, tables, S, tm=512):
    n_tok = main.shape[0]
    per_seq = S // tm
    tab_spec = pl.BlockSpec((tm, LANES), lambda i: (i % per_seq, 0))
    return pl.pallas_call(
        _rope_kernel,
        grid=(n_tok // tm,),
        in_specs=[pl.BlockSpec((tm, D_MODEL), lambda i: (i, COL_AT_Q)),
                  pl.BlockSpec((tm, LANES), lambda i: (i, 0)),
                  pl.BlockSpec((tm, LANES), lambda i: (i, 1)),
                  tab_spec, tab_spec, tab_spec],
        out_specs=[pl.BlockSpec((tm, D_MODEL), lambda i: (i, 0)),
                   pl.BlockSpec((tm, LANES), lambda i: (i, 0)),
                   pl.BlockSpec((tm, LANES), lambda i: (i, 0))],
        out_shape=[jax.ShapeDtypeStruct((n_tok, D_MODEL), BF16),
                   jax.ShapeDtypeStruct((n_tok, LANES), BF16),
                   jax.ShapeDtypeStruct((n_tok, LANES), BF16)],
        compiler_params=_cparams(("arbitrary",)),
        name="rope",
    )(main, small, small, *tables)


def _attn_kernel(*refs, local, n_blocks, q_scale):
    if local:
        (q_ref, kp_ref, kc_ref, kn_ref, vp_ref, vc_ref, vn_ref, kx_ref, vx_ref, sink_ref, o_ref) = refs
    else:
        (q_ref, kx_ref, vx_ref, sink_ref, o_ref) = refs
    P = AT_BLOCK
    G = AT_Q_HEADS // AT_KV_HEADS
    L = kx_ref.shape[0]
    kx = kx_ref[...].astype(BF16)
    vx = vx_ref[...].astype(BF16)
    if local:
        i = pl.program_id(1)
        k_all = jnp.concatenate([kp_ref[...], kc_ref[...], kn_ref[...], kx], axis=0)
        v_all = jnp.concatenate([vp_ref[...], vc_ref[...], vn_ref[...], vx], axis=0)
        qi = lax.broadcasted_iota(I32, (P, P), 0)
        kj = lax.broadcasted_iota(I32, (P, P), 1)
        b_prev = jnp.where(kj >= qi, 0.0, NEG_BIG) + jnp.where(i > 0, 0.0, NEG_BIG)
        b_next = jnp.where(kj <= qi, 0.0, NEG_BIG) + jnp.where(i < n_blocks - 1, 0.0, NEG_BIG)
        bias = jnp.concatenate([b_prev, jnp.zeros((P, P), F32), b_next, jnp.zeros((P, L), F32)], axis=1)
    else:
        k_all, v_all, bias = kx, vx, None
    for qh in range(AT_Q_HEADS):
        hk = qh // G
        q = q_ref[:, qh * AT_DIM:(qh + 1) * AT_DIM]
        if q_scale != 1.0:
            q = (q.astype(F32) * q_scale).astype(BF16)
        kh = k_all[:, hk * AT_DIM:(hk + 1) * AT_DIM]
        vh = v_all[:, hk * AT_DIM:(hk + 1) * AT_DIM]
        s = _dot_nt(q, kh)
        if bias is not None:
            s = s + bias
        sink = sink_ref[:, qh:qh + 1]
        m = jnp.maximum(jnp.max(s, axis=-1, keepdims=True), sink)
        p = jnp.exp(s - m)
        den = jnp.sum(p, axis=-1, keepdims=True) + jnp.exp(sink - m)
        o = _dot(p.astype(BF16), vh) / den
        o_ref[:, qh * AT_DIM:(qh + 1) * AT_DIM] = o.astype(BF16)


def _attention_local(q_r, k_r, v_r, small_c, sinks, B, S, L):
    P = AT_BLOCK
    nb = S // P

    def kv_trio():
        return [pl.BlockSpec((P, LANES), lambda b, i: (b * nb + jnp.maximum(i - 1, 0), 0)),
                pl.BlockSpec((P, LANES), lambda b, i: (b * nb + i, 0)),
                pl.BlockSpec((P, LANES), lambda b, i: (b * nb + jnp.minimum(i + 1, nb - 1), 0))]

    return pl.pallas_call(
        functools.partial(_attn_kernel, local=True, n_blocks=nb, q_scale=1.0),
        grid=(B, nb),
        in_specs=[pl.BlockSpec((P, D_MODEL), lambda b, i: (b * nb + i, 0))] + kv_trio() + kv_trio() + [
            pl.BlockSpec((L, LANES), lambda b, i: (b, 0)),
            pl.BlockSpec((L, LANES), lambda b, i: (b, 1)),
            pl.BlockSpec((1, AT_Q_HEADS), lambda b, i: (0, 0))],
        out_specs=pl.BlockSpec((P, D_MODEL), lambda b, i: (b * nb + i, 0)),
        out_shape=jax.ShapeDtypeStruct((B * S, D_MODEL), BF16),
        compiler_params=_cparams(("arbitrary", "arbitrary")),
        name="attn_local",
    )(q_r, k_r, k_r, k_r, v_r, v_r, v_r, small_c, small_c, sinks)


def _attention_ctx(main_c, small_c, sinks, B, L):
    P = AT_BLOCK
    nb = L // P
    return pl.pallas_call(
        functools.partial(_attn_kernel, local=False, n_blocks=nb, q_scale=AT_DIM ** -0.5),
        grid=(B, nb),
        in_specs=[pl.BlockSpec((P, D_MODEL), lambda b, i: (b * nb + i, COL_AT_Q)),
                  pl.BlockSpec((L, LANES), lambda b, i: (b, 0)),
                  pl.BlockSpec((L, LANES), lambda b, i: (b, 1)),
                  pl.BlockSpec((1, AT_Q_HEADS), lambda b, i: (0, 0))],
        out_specs=pl.BlockSpec((P, D_MODEL), lambda b, i: (b * nb + i, 0)),
        out_shape=jax.ShapeDtypeStruct((B * L, D_MODEL), BF16),
        compiler_params=_cparams(("arbitrary", "arbitrary")),
        name="attn_ctx",
    )(main_c, small_c, small_c, sinks)


def _merge_kernel(ysg_ref, odn_ref, dng_ref, yat_ref, g0_ref, g1_ref, g2_ref, x_ref, mod_ref,
                  dn_norm_ref, post_ref, pre_ref, wsg_ref, wdn_ref, wat_ref, wout_ref, rw_ref, rb_ref,
                  xo_ref, h2_ref, lg_ref):
    o = odn_ref[0].astype(F32) + odn_ref[1].astype(F32)
    dn_g = dn_norm_ref[...]
    parts = []
    for h in range(DN_HEADS):
        lanes = slice(h * DN_DIM, (h + 1) * DN_DIM)
        parts.append(_rms(o[:, lanes], dn_g) * _silu(dng_ref[:, lanes].astype(F32)))
    ydn = jnp.concatenate(parts, axis=1).astype(BF16)
    m = (_sigmoid(g0_ref[...].astype(F32)) * _dot(ysg_ref[...], wsg_ref[...])
         + _sigmoid(g1_ref[...].astype(F32)) * _dot(ydn, wdn_ref[...])
         + _sigmoid(g2_ref[...].astype(F32)) * _dot(yat_ref[...], wat_ref[...]))
    y = _dot(m.astype(BF16), wout_ref[...])
    gate1 = mod_ref[:, 2 * D_MODEL:3 * D_MODEL]
    sh2 = mod_ref[:, 3 * D_MODEL:4 * D_MODEL]
    sc2 = mod_ref[:, 4 * D_MODEL:5 * D_MODEL]
    xn = x_ref[...] + gate1 * _rms(y, post_ref[...])
    xo_ref[...] = xn
    h2 = _rms(xn, pre_ref[...]) * (1.0 + sc2) + sh2
    h2_ref[...] = h2
    lg_ref[...] = _dot(h2.astype(BF16), rw_ref[...]) + rb_ref[...]


def _merge(ysg, odn, main, yat, x, mod, mod_row, lw, tm=256):
    n_tok = x.shape[0]
    const = lambda i: (0, 0)
    wspec = pl.BlockSpec((D_MODEL, D_MODEL), const, pipeline_mode=pl.Buffered(1))
    vspec = pl.BlockSpec((1, D_MODEL), const)
    return pl.pallas_call(
        _merge_kernel,
        grid=(n_tok // tm,),
        in_specs=[pl.BlockSpec((tm, D_MODEL), lambda i: (i, 0)),
                  pl.BlockSpec((2, tm, D_MODEL), lambda i: (0, i, 0)),
                  pl.BlockSpec((tm, D_MODEL), lambda i: (i, COL_DN_G)),
                  pl.BlockSpec((tm, D_MODEL), lambda i: (i, 0)),
                  pl.BlockSpec((tm, D_MODEL), lambda i: (i, COL_GATE0)),
                  pl.BlockSpec((tm, D_MODEL), lambda i: (i, COL_GATE0 + 1)),
                  pl.BlockSpec((tm, D_MODEL), lambda i: (i, COL_GATE0 + 2)),
                  pl.BlockSpec((tm, D_MODEL), lambda i: (i, 0)),
                  pl.BlockSpec((None, 1, 6 * D_MODEL), lambda i: (mod_row(i * tm), 0, 0)),
                  pl.BlockSpec((1, DN_DIM), const), vspec, vspec,
                  wspec, wspec, wspec, wspec,
                  pl.BlockSpec((D_MODEL, LANES), const), pl.BlockSpec((1, LANES), const)],
        out_specs=[pl.BlockSpec((tm, D_MODEL), lambda i: (i, 0)),
                   pl.BlockSpec((tm, D_MODEL), lambda i: (i, 0)),
                   pl.BlockSpec((tm, LANES), lambda i: (i, 0))],
        out_shape=[jax.ShapeDtypeStruct((n_tok, D_MODEL), F32),
                   jax.ShapeDtypeStruct((n_tok, D_MODEL), F32),
                   jax.ShapeDtypeStruct((n_tok, LANES), F32)],
        compiler_params=_cparams(("arbitrary",)),
        name="merge",
    )(ysg, odn, main, yat, main, main, main, x, mod,
      lw["dn_norm_g"], lw["norm_post_mix"], lw["norm_pre_ffn"],
      lw["w_proj_sg"], lw["w_proj_dn"], lw["w_proj_at"], lw["w_out"], lw["router_w"], lw["router_b"])


def _route_kernel(lg_ref, idx_ref, gate_ref, rank_ref, cnt_ref, run_ref):
    tm = lg_ref.shape[0]

    @pl.when(pl.program_id(0) == 0)
    def _():
        run_ref[...] = jnp.zeros_like(run_ref)

    l = lg_ref[...]
    lane = lax.broadcasted_iota(I32, l.shape, 1).astype(F32)
    vals, onehots = [], []
    for k in range(TOP_K):
        m = jnp.max(l, axis=-1, keepdims=True)
        ik = jnp.min(jnp.where(l == m, lane, float(LANES)), axis=-1, keepdims=True)
        oh = lane == ik
        idx_ref[:, k:k + 1] = ik.astype(I32)
        vals.append(m)
        onehots.append(oh)
        l = jnp.where(oh, -jnp.inf, l)
    es = [jnp.exp(v - vals[0]) for v in vals]
    den = es[0] + es[1] + es[2] + es[3]
    sel = jnp.zeros(l.shape, F32)
    for k in range(TOP_K):
        gate_ref[:, k:k + 1] = es[k] / den
        sel = sel + onehots[k].astype(F32)
    ri = lax.broadcasted_iota(I32, (tm, tm), 0)
    ci = lax.broadcasted_iota(I32, (tm, tm), 1)
    before = _dot((ri > ci).astype(BF16), sel.astype(BF16)) + run_ref[...]
    for k in range(TOP_K):
        rank_ref[:, k:k + 1] = jnp.sum(jnp.where(onehots[k], before, 0.0), axis=-1,
                                       keepdims=True).astype(I32)
    run_ref[...] = run_ref[...] + jnp.sum(sel, axis=0, keepdims=True)
    cnt_ref[...] = run_ref[...]


def _route(logits, tm=256):
    n_tok = logits.shape[0]
    small = lambda dt: jax.ShapeDtypeStruct((n_tok, TOP_K), dt)
    kspec = pl.BlockSpec((tm, TOP_K), lambda i: (i, 0))
    return pl.pallas_call(
        _route_kernel,
        grid=(n_tok // tm,),
        in_specs=[pl.BlockSpec((tm, LANES), lambda i: (i, 0))],
        out_specs=[kspec, kspec, kspec, pl.BlockSpec((1, LANES), lambda i: (0, 0))],
        out_shape=[small(I32), small(F32), small(I32), jax.ShapeDtypeStruct((1, LANES), F32)],
        scratch_shapes=[pltpu.VMEM((1, LANES), F32)],
        compiler_params=_cparams(("arbitrary",)),
        name="route",
    )(logits)


def _dispatch_kernel(dest_ref, h_ref, xs_in_ref, xs_ref, sem):
    del xs_in_ref
    tm = h_ref.shape[0]

    def row_copy(r, k):
        return pltpu.make_async_copy(h_ref.at[pl.ds(r, 1)],
                                     xs_ref.at[pl.ds(dest_ref[r * TOP_K + k], 1)], sem)

    def issue(r, carry):
        for k in range(TOP_K):
            row_copy(r, k).start()
        return carry

    lax.fori_loop(0, tm, issue, 0)

    def drain(r, carry):
        for k in range(TOP_K):
            row_copy(r, k).wait()
        return carry

    lax.fori_loop(0, tm, drain, 0)


def _dispatch(dest_flat, h2, n_slots, tm=256):
    n_tok = h2.shape[0]
    xs0 = jnp.zeros((n_slots, D_MODEL), F32)
    return pl.pallas_call(
        _dispatch_kernel,
        grid=(n_tok // tm,),
        in_specs=[pl.BlockSpec((tm * TOP_K,), lambda i: (i,), memory_space=pltpu.SMEM),
                  pl.BlockSpec((tm, D_MODEL), lambda i: (i, 0)),
                  pl.BlockSpec(memory_space=pl.ANY)],
        out_specs=pl.BlockSpec(memory_space=pl.ANY),
        out_shape=jax.ShapeDtypeStruct((n_slots, D_MODEL), F32),
        scratch_shapes=[pltpu.SemaphoreType.DMA],
        input_output_aliases={2: 0},
        compiler_params=_cparams(("arbitrary",)),
        name="moe_dispatch",
    )(dest_flat, h2, xs0)


def _expert_kernel(te_ref, nu_ref, xs_ref, wgu_ref, bgu_ref, wd_ref, bd_ref, y_ref):
    del te_ref

    @pl.when(pl.program_id(0) < nu_ref[0])
    def _():
        gu = _dot(xs_ref[...].astype(BF16), wgu_ref[...]) + bgu_ref[...]
        g = jnp.minimum(gu[:, :D_EXPERT], SWIGLU_LIMIT)
        lin = jnp.clip(gu[:, D_EXPERT:], -SWIGLU_LIMIT, SWIGLU_LIMIT)
        act = g * _sigmoid(SWIGLU_ALPHA * g) * (lin + 1.0)
        y_ref[...] = _dot(act.astype(BF16), wd_ref[...]) + bd_ref[...]


def _experts(tile_expert, n_used, xs, wgu, bgu, wd, bd, tm):
    n_slots = xs.shape[0]
    n_tiles = n_slots // tm

    def row(i, te, nu):
        return (jnp.minimum(i, nu[0] - 1), 0)

    grid_spec = pltpu.PrefetchScalarGridSpec(
        num_scalar_prefetch=2,
        grid=(n_tiles,),
        in_specs=[pl.BlockSpec((tm, D_MODEL), row),
                  pl.BlockSpec((None, D_MODEL, 2 * D_EXPERT), lambda i, te, nu: (te[i], 0, 0)),
                  pl.BlockSpec((None, 1, 2 * D_EXPERT), lambda i, te, nu: (te[i], 0, 0)),
                  pl.BlockSpec((None, D_EXPERT, D_MODEL), lambda i, te, nu: (te[i], 0, 0)),
                  pl.BlockSpec((None, 1, D_MODEL), lambda i, te, nu: (te[i], 0, 0))],
        out_specs=pl.BlockSpec((tm, D_MODEL), row),
    )
    return pl.pallas_call(
        _expert_kernel,
        grid_spec=grid_spec,
        out_shape=jax.ShapeDtypeStruct((n_slots, D_MODEL), F32),
        compiler_params=_cparams(("arbitrary",)),
        name="moe_experts",
    )(tile_expert, n_used, xs, wgu, bgu, wd, bd)


def _combine_kernel(dest_ref, gate_ref, x_ref, mod_ref, post_ref, y_ref, xo_ref, buf_ref, sem):
    tm = x_ref.shape[0]

    def row_copy(r, k):
        return pltpu.make_async_copy(y_ref.at[pl.ds(dest_ref[r * TOP_K + k], 1)],
                                     buf_ref.at[k, pl.ds(r, 1)], sem)

    def issue(r, carry):
        for k in range(TOP_K):
            row_copy(r, k).start()
        return carry

    lax.fori_loop(0, tm, issue, 0)

    def drain(r, carry):
        for k in range(TOP_K):
            row_copy(r, k).wait()
        return carry

    lax.fori_loop(0, tm, drain, 0)

    y = None
    for k in range(TOP_K):
        t = buf_ref[k] * gate_ref[:, k:k + 1]
        y = t if y is None else y + t
    gate2 = mod_ref[:, 5 * D_MODEL:6 * D_MODEL]
    xo_ref[...] = x_ref[...] + gate2 * _rms(y, post_ref[...])


def _combine(dest_flat, gate, x_mid, mod, mod_row, post_g, y, tm=256):
    n_tok = x_mid.shape[0]
    return pl.pallas_call(
        _combine_kernel,
        grid=(n_tok // tm,),
        in_specs=[pl.BlockSpec((tm * TOP_K,), lambda i: (i,), memory_space=pltpu.SMEM),
                  pl.BlockSpec((tm, TOP_K), lambda i: (i, 0)),
                  pl.BlockSpec((tm, D_MODEL), lambda i: (i, 0)),
                  pl.BlockSpec((None, 1, 6 * D_MODEL), lambda i: (mod_row(i * tm), 0, 0)),
                  pl.BlockSpec((1, D_MODEL), lambda i: (0, 0)),
                  pl.BlockSpec(memory_space=pl.ANY)],
        out_specs=pl.BlockSpec((tm, D_MODEL), lambda i: (i, 0)),
        out_shape=jax.ShapeDtypeStruct((n_tok, D_MODEL), F32),
        scratch_shapes=[pltpu.VMEM((TOP_K, tm, D_MODEL), F32), pltpu.SemaphoreType.DMA],
        compiler_params=_cparams(("arbitrary",)),
        name="moe_combine",
    )(dest_flat, gate, x_mid, mod, post_g, y)


def _moe(h2, logits, x_mid, mod, mod_row, lw, tm_e=512):
    n_tok = h2.shape[0]
    idx, gate, rank, cnt = _route(logits)
    counts = cnt[0, :N_EXPERTS].astype(I32)
    padded = (counts + tm_e - 1) // tm_e * tm_e
    pad_end = jnp.cumsum(padded)
    offs = pad_end - padded
    dest = (jnp.take(offs, idx, axis=0) + rank).reshape(-1).astype(I32)
    n_tiles = (n_tok * TOP_K + tm_e - 1) // tm_e + N_EXPERTS
    tile_expert = jnp.minimum(
        jnp.searchsorted(pad_end, jnp.arange(n_tiles, dtype=I32) * tm_e, side="right"),
        N_EXPERTS - 1).astype(I32)
    n_used = (pad_end[-1:] // tm_e).astype(I32)
    xs = _dispatch(dest, h2, n_tiles * tm_e)
    y = _experts(tile_expert, n_used, xs, lw["exp_w_gu"], lw["exp_b_gu"], lw["exp_w_down"],
                 lw["exp_b_down"], tm_e)
    return _combine(dest, gate, x_mid, mod, mod_row, lw["norm_post_ffn"], y)


def _split_w_in(w_in):
    offs, o = {}, 0
    for name, width in (("dn_k", 1024), ("dn_v", 1024), ("dn_a", 16), ("dn_b", 16), ("at_k", 128),
                        ("at_v", 128), ("dn_q", 1024), ("dn_g", 1024), ("at_q", 1024),
                        ("sg_u", 1024), ("sg_v", 1024), ("gates", 3072)):
        offs[name] = (o, o + width)
        o += width
    sl = lambda n: w_in[:, offs[n][0]:offs[n][1]]
    w_main = jnp.concatenate([sl(n) for n in ("dn_k", "dn_v", "dn_q", "dn_g", "at_q", "sg_u", "sg_v",
                                              "gates")], axis=1).astype(BF16)
    pad = jnp.zeros((w_in.shape[0], N_SMALL_COLS - 2 * LANES - 4 * DN_HEADS), w_in.dtype)
    w_small = jnp.concatenate([sl("at_k"), sl("at_v"), sl("dn_a"), sl("dn_b"), pad], axis=1).astype(BF16)
    return w_main, w_small


def _dn_gates(small, B, T):
    ab = small[:, 2 * LANES:2 * LANES + 4 * DN_HEADS].reshape(B, T, 2, 2, DN_HEADS)
    col = jnp.transpose(ab, (3, 0, 1, 2, 4)).reshape(2, B, T, 2 * DN_HEADS)
    row = jnp.transpose(col.reshape(2, B, T // DN_CHUNK, DN_CHUNK, 2 * DN_HEADS), (0, 1, 2, 4, 3))
    return col, row


def kernel(x, c, ctx, c_ctx, w_mod, b_mod, norm_pre_mix, norm_post_mix, norm_pre_ffn, norm_post_ffn, w_in, sg_ln_g, sg_ln_b, sg_w, sg_b, dn_conv_w, dn_a_log, dn_dt_bias, dn_norm_g, at_sinks, w_proj_sg, w_proj_dn, w_proj_at, w_out, router_w, router_b, exp_w_gu, exp_b_gu, exp_w_down, exp_b_down):
    B, S, D = x.shape
    L = ctx.shape[1]
    depth = w_mod.shape[0]
    assert D == D_MODEL and S % GRID_W == 0
    n_lat, n_ctx = B * S, B * L

    rows = (B + 1 + 7) // 8 * 8
    cvec = jnp.zeros((rows, D), F32).at[:B].set(c).at[B].set(c_ctx)
    mod_all = _modulation(cvec, w_mod, b_mod)
    tables = _rope_tables(S)

    lat_row = lambda t: t // S
    ctx_row = lambda t: B
    all_row = lambda t: jnp.where(t < n_lat, t // S, B)

    xl = x.reshape(n_lat, D)
    xc = ctx.reshape(n_ctx, D)
    for l in range(depth):
        need_ctx_out = l < depth - 1
        mod = mod_all[l].reshape(rows, 1, 6 * D)
        w_main, w_small = _split_w_in(w_in[l])
        lw = {
            "dn_norm_g": dn_norm_g[l].reshape(1, -1),
            "norm_post_mix": norm_post_mix[l].reshape(1, -1),
            "norm_pre_ffn": norm_pre_ffn[l].reshape(1, -1),
            "norm_post_ffn": norm_post_ffn[l].reshape(1, -1),
            "w_proj_sg": w_proj_sg[l].astype(BF16), "w_proj_dn": w_proj_dn[l].astype(BF16),
            "w_proj_at": w_proj_at[l].astype(BF16), "w_out": w_out[l].astype(BF16),
            "router_w": jnp.pad(router_w[l], ((0, 0), (0, LANES - N_EXPERTS))).astype(BF16),
            "router_b": jnp.pad(router_b[l], (0, LANES - N_EXPERTS),
                                constant_values=NEG_BIG).reshape(1, -1),
            "exp_w_gu": exp_w_gu[l].astype(BF16), "exp_b_gu": exp_b_gu[l].reshape(N_EXPERTS, 1, -1),
            "exp_w_down": exp_w_down[l].astype(BF16), "exp_b_down": exp_b_down[l].reshape(N_EXPERTS, 1, -1),
        }
        pre_g = norm_pre_mix[l].reshape(1, -1)
        main, small = _inproj(xl, mod, lat_row, pre_g, w_main, w_small, min(1024, S))
        w_main_c = w_main if need_ctx_out else w_main[:, :N_CTX_MAIN_COLS]
        main_c, small_c = _inproj(xc, mod, ctx_row, pre_g, w_main_c, w_small, min(1024, n_ctx))

        sg_args = (sg_ln_g[l].reshape(1, -1), sg_ln_b[l].reshape(1, -1), sg_w[l].astype(BF16),
                   sg_b[l].T)
        ysg = _sgu(main, *sg_args)

        gcol_c, grow_c = _dn_gates(small_c, B, L)
        gcol, grow = _dn_gates(small, B, S)
        s0 = jnp.zeros((2, B, DN_HEADS, DN_DIM, DN_DIM), F32)
        odn_c, s_ctx = _deltanet(main_c.reshape(B, L, -1), gcol_c, grow_c, dn_conv_w[l], dn_a_log[l],
                                 dn_dt_bias[l], s0, need_ctx_out)
        odn, _ = _deltanet(main.reshape(B, S, -1), gcol, grow, dn_conv_w[l], dn_a_log[l],
                           dn_dt_bias[l], s_ctx, True)

        sinks = at_sinks[l].reshape(1, -1)
        q_r, k_r, v_r = _rope(main, small, tables, S)
        yat = _attention_local(q_r, k_r, v_r, small_c, sinks, B, S, L)

        x_mid, h2, logits = _merge(ysg, odn.reshape(2, n_lat, D), main, yat, xl, mod, lat_row, lw)
        if need_ctx_out:
            ysg_c = _sgu(main_c, *sg_args)
            yat_c = _attention_ctx(main_c, small_c, sinks, B, L)
            xc_mid, h2c, logits_c = _merge(ysg_c, odn_c.reshape(2, n_ctx, D), main_c, yat_c, xc, mod,
                                           ctx_row, lw)
            x_mid = jnp.concatenate([x_mid, xc_mid], axis=0)
            h2 = jnp.concatenate([h2, h2c], axis=0)
            logits = jnp.concatenate([logits, logits_c], axis=0)
            xo = _moe(h2, logits, x_mid, mod, all_row, lw)
            xl, xc = xo[:n_lat], xo[n_lat:]
        else:
            xl = _moe(h2, logits, x_mid, mod, lat_row, lw)
    return xl.reshape(B, S, D)
```

```python
import functools
import math

import jax
import jax.numpy as jnp
from jax import lax
from jax.experimental import pallas as pl
from jax.experimental.pallas import tpu as pltpu

F32 = jnp.float32
BF16 = jnp.bfloat16
I32 = jnp.int32

EPS = 1e-6
D_MODEL = 1024
GRID_W = 64

SG_CHUNK = 128
SG_GROUPS = 8

DN_HEADS = 8
DN_DIM = 128
DN_CONV = 5
DN_CHUNK = 64
DN_HALO = 16

AT_Q_HEADS = 16
AT_KV_HEADS = 2
AT_DIM = 64
AT_BLOCK = 128
ROPE_BASE = 10000.0

N_EXPERTS = 32
TOP_K = 4
D_EXPERT = 1024
SWIGLU_ALPHA = 1.702
SWIGLU_LIMIT = 7.0
N_BRANCH = 3

LANES = 128
NEG_BIG = -1e30

COL_DN_K, COL_DN_V, COL_DN_Q, COL_DN_G, COL_AT_Q, COL_SG_U, COL_SG_V, COL_GATE0 = range(8)
N_MAIN_COLS = 10 * D_MODEL
N_CTX_MAIN_COLS = 2 * D_MODEL
N_SMALL_COLS = 3 * LANES

VMEM_LIMIT = 52 * 1024 * 1024


def _cparams(sem):
    return pltpu.CompilerParams(dimension_semantics=sem, vmem_limit_bytes=VMEM_LIMIT)


def _dot(a, b):
    return jnp.dot(a, b, preferred_element_type=F32)


def _dot_nt(a, b):
    return lax.dot_general(a, b, (((1,), (1,)), ((), ())), preferred_element_type=F32)


def _dot_tn(a, b):
    return lax.dot_general(a, b, (((0,), (0,)), ((), ())), preferred_element_type=F32)


def _sigmoid(x):
    return 1.0 / (1.0 + jnp.exp(-x))


def _silu(x):
    return x * _sigmoid(x)


def _gelu_tanh(x):
    return 0.5 * x * (1.0 + jnp.tanh(math.sqrt(2.0 / math.pi) * (x + 0.044715 * (x * x * x))))


def _softplus(x):
    return jnp.maximum(x, 0.0) + jnp.log(1.0 + jnp.exp(-jnp.abs(x)))


def _rms(x, g):
    return x * lax.rsqrt(jnp.mean(x * x, axis=-1, keepdims=True) + EPS) * g


def _mod_kernel(c_ref, w_ref, b_ref, o_ref):
    s = _silu(c_ref[...])
    o_ref[...] = jnp.dot(s, w_ref[...], preferred_element_type=F32,
                         precision=lax.Precision.HIGHEST) + b_ref[...]


def _modulation(cvec, w_mod, b_mod):
    depth = w_mod.shape[0]
    rows = cvec.shape[0]
    n_col = w_mod.shape[2] // D_MODEL
    return pl.pallas_call(
        _mod_kernel,
        grid=(depth, n_col),
        in_specs=[pl.BlockSpec((rows, D_MODEL), lambda l, j: (0, 0)),
                  pl.BlockSpec((None, D_MODEL, D_MODEL), lambda l, j: (l, 0, j)),
                  pl.BlockSpec((None, 1, D_MODEL), lambda l, j: (l, 0, j))],
        out_specs=pl.BlockSpec((None, rows, D_MODEL), lambda l, j: (l, 0, j)),
        out_shape=jax.ShapeDtypeStruct((depth, rows, w_mod.shape[2]), F32),
        compiler_params=_cparams(("arbitrary", "arbitrary")),
        name="modulation",
    )(cvec, w_mod, b_mod.reshape(depth, 1, -1))


def _inproj_kernel(x_ref, mod_ref, g_ref, wm_ref, ws_ref, main_ref, small_ref, h_ref):
    @pl.when(pl.program_id(1) == 0)
    def _():
        sh = mod_ref[:, 0 * D_MODEL:1 * D_MODEL]
        sc = mod_ref[:, 1 * D_MODEL:2 * D_MODEL]
        h = (_rms(x_ref[...], g_ref[...]) * (1.0 + sc) + sh).astype(BF16)
        h_ref[...] = h
        small_ref[...] = _dot(h, ws_ref[...])

    main_ref[...] = _dot(h_ref[...], wm_ref[...]).astype(BF16)


def _inproj(x, mod, mod_row, norm_g, w_main, w_small, tm, tn=1024):
    n_tok = x.shape[0]
    n_main = w_main.shape[1]
    return pl.pallas_call(
        _inproj_kernel,
        grid=(n_tok // tm, n_main // tn),
        in_specs=[pl.BlockSpec((tm, D_MODEL), lambda i, j: (i, 0)),
                  pl.BlockSpec((None, 1, 6 * D_MODEL), lambda i, j: (mod_row(i * tm), 0, 0)),
                  pl.BlockSpec((1, D_MODEL), lambda i, j: (0, 0)),
                  pl.BlockSpec((D_MODEL, tn), lambda i, j: (0, j)),
                  pl.BlockSpec((D_MODEL, N_SMALL_COLS), lambda i, j: (0, 0))],
        out_specs=[pl.BlockSpec((tm, tn), lambda i, j: (i, j)),
                   pl.BlockSpec((tm, N_SMALL_COLS), lambda i, j: (i, 0))],
        out_shape=[jax.ShapeDtypeStruct((n_tok, n_main), BF16),
                   jax.ShapeDtypeStruct((n_tok, N_SMALL_COLS), F32)],
        scratch_shapes=[pltpu.VMEM((tm, D_MODEL), BF16)],
        compiler_params=_cparams(("arbitrary", "arbitrary")),
        name="inproj",
    )(x, mod, norm_g, w_main, w_small)


def _sgu_kernel(u_ref, v_ref, lng_ref, lnb_ref, ws_ref, bs_ref, o_ref, *, n_chunk):
    u = _gelu_tanh(u_ref[...].astype(F32))
    v = _gelu_tanh(v_ref[...].astype(F32))
    vc = v - jnp.mean(v, axis=-1, keepdims=True)
    var = jnp.mean(vc * vc, axis=-1, keepdims=True)
    vn = (vc * lax.rsqrt(var + EPS) * lng_ref[...] + lnb_ref[...]).astype(BF16)
    for n in range(n_chunk):
        rows = slice(n * SG_CHUNK, (n + 1) * SG_CHUNK)
        for g in range(SG_GROUPS):
            cols = slice(g * LANES, (g + 1) * LANES)
            mixed = _dot(ws_ref[g], vn[rows, cols]) + bs_ref[:, g:g + 1]
            o_ref[rows, cols] = (u[rows, cols] * mixed).astype(BF16)


def _sgu(main, sg_ln_g, sg_ln_b, sg_w, sg_bt, n_chunk=2):
    n_tok = main.shape[0]
    tc = n_chunk * SG_CHUNK
    return pl.pallas_call(
        functools.partial(_sgu_kernel, n_chunk=n_chunk),
        grid=(n_tok // tc,),
        in_specs=[pl.BlockSpec((tc, D_MODEL), lambda i: (i, COL_SG_U)),
                  pl.BlockSpec((tc, D_MODEL), lambda i: (i, COL_SG_V)),
                  pl.BlockSpec((1, D_MODEL), lambda i: (0, 0)),
                  pl.BlockSpec((1, D_MODEL), lambda i: (0, 0)),
                  pl.BlockSpec((SG_GROUPS, SG_CHUNK, SG_CHUNK), lambda i: (0, 0, 0)),
                  pl.BlockSpec((SG_CHUNK, SG_GROUPS), lambda i: (0, 0))],
        out_specs=pl.BlockSpec((tc, D_MODEL), lambda i: (i, 0)),
        out_shape=jax.ShapeDtypeStruct((n_tok, D_MODEL), BF16),
        compiler_params=_cparams(("arbitrary",)),
        name="sgu",
    )(main, main, sg_ln_g, sg_ln_b, sg_w, sg_bt)


def _dn_kernel(*refs, with_q, n_chunks):
    s_refs = refs[-DN_HEADS:]
    ext_refs = refs[-DN_HEADS - 3:-DN_HEADS]
    refs = refs[:-DN_HEADS - 3]
    if with_q:
        (qp_ref, qc_ref, qn_ref, kp_ref, kc_ref, kn_ref, vp_ref, vc_ref, vn_ref,
         gcol_ref, grow_ref, cw_ref, alog_r_ref, alog_c_ref, dtb_r_ref, dtb_c_ref, s0_ref,
         o_ref, sfin_ref) = refs
    else:
        (kp_ref, kc_ref, kn_ref, vp_ref, vc_ref, vn_ref,
         gcol_ref, grow_ref, cw_ref, alog_r_ref, alog_c_ref, dtb_r_ref, dtb_c_ref, s0_ref,
         sfin_ref) = refs
    d = pl.program_id(0)
    c = pl.program_id(2)
    is_fwd = d == 0
    cidx = jnp.where(is_fwd, c, n_chunks - 1 - c)
    C = DN_CHUNK

    @pl.when(c == 0)
    def _():
        for h in range(DN_HEADS):
            s_refs[h][...] = s0_ref[h]

    has_prev = (cidx > 0).astype(F32)
    has_next = (cidx < n_chunks - 1).astype(F32)

    def conv_silu(p_ref, c_ref, n_ref, part):
        ext_ref = ext_refs[part]
        ext_ref[0:DN_HALO, :] = p_ref[...].astype(F32) * has_prev
        ext_ref[DN_HALO:DN_HALO + C, :] = c_ref[...].astype(F32)
        ext_ref[DN_HALO + C:2 * DN_HALO + C, :] = n_ref[...].astype(F32) * has_next
        base = DN_HALO - DN_CONV // 2
        y = None
        for i in range(DN_CONV):
            w = cw_ref[i:i + 1, part * D_MODEL:(part + 1) * D_MODEL]
            t = ext_ref[base + i:base + i + C, :] * w
            y = t if y is None else y + t
        return _silu(y)

    k_all = conv_silu(kp_ref, kc_ref, kn_ref, 1)
    v_all = conv_silu(vp_ref, vc_ref, vn_ref, 2)
    q_all = conv_silu(qp_ref, qc_ref, qn_ref, 0) if with_q else None

    gcol = gcol_ref[...]
    ld_col = -jnp.exp(alog_r_ref[...]) * _softplus(gcol[:, 0:DN_HEADS] + dtb_r_ref[...])
    beta_col = _sigmoid(gcol[:, DN_HEADS:2 * DN_HEADS])
    ld_row = -jnp.exp(alog_c_ref[...]) * _softplus(grow_ref[0:DN_HEADS, :] + dtb_c_ref[...])

    ri = lax.broadcasted_iota(I32, (C, C), 0)
    ci = lax.broadcasted_iota(I32, (C, C), 1)
    delta = (ri - ci) * (1 - 2 * d)
    incl = delta >= 0
    strict = delta > 0
    incl_t = delta <= 0
    gam_col = jnp.dot(incl.astype(F32), ld_col, preferred_element_type=F32,
                      precision=lax.Precision.HIGHEST)
    gam_row = jnp.dot(ld_row, incl_t.astype(F32), preferred_element_type=F32,
                      precision=lax.Precision.HIGHEST)
    gam_tot = jnp.sum(ld_col, axis=0, keepdims=True)
    eye = (ri == ci).astype(F32)

    H = range(DN_HEADS)
    lanes = [slice(h * DN_DIM, (h + 1) * DN_DIM) for h in H]
    gc = [gam_col[:, h:h + 1] for h in H]
    bc = [beta_col[:, h:h + 1] for h in H]
    kh = [k_all[:, lanes[h]] for h in H]
    kh = [kh[h] * lax.rsqrt(jnp.sum(kh[h] * kh[h], axis=-1, keepdims=True) + EPS) for h in H]
    kb = [kh[h].astype(BF16) for h in H]
    decay = [jnp.exp(jnp.where(incl, gc[h] - gam_row[h:h + 1, :], NEG_BIG)) for h in H]
    kk = [_dot_nt(kb[h], kb[h]) for h in H]
    x = [-(jnp.where(strict, decay[h], 0.0) * bc[h] * kk[h]) for h in H]
    p = [eye + x[h] for h in H]
    xb = [x[h].astype(BF16) for h in H]
    x = [_dot(xb[h], xb[h]) for h in H]
    n_fac = int(math.log2(C)) - 1
    for j in range(n_fac):
        xb = [x[h].astype(BF16) for h in H]
        if j < n_fac - 1:
            r = [_dot(xb[h], jnp.concatenate([xb[h], p[h].astype(BF16)], axis=1)) for h in H]
            x = [r[h][:, :C] for h in H]
            p = [p[h] + r[h][:, C:] for h in H]
        else:
            p = [p[h] + _dot(xb[h], p[h].astype(BF16)) for h in H]
    rhs = [jnp.concatenate([kh[h] * (bc[h] * jnp.exp(gc[h])), v_all[:, lanes[h]] * bc[h]],
                           axis=1).astype(BF16) for h in H]
    sol = [_dot(p[h].astype(BF16), rhs[h]) for h in H]
    w = [sol[h][:, :DN_DIM] for h in H]
    u0 = [sol[h][:, DN_DIM:] for h in H]
    k_end = [(kh[h] * jnp.exp(gam_tot[:, h:h + 1] - gc[h])).astype(BF16) for h in H]
    s = [s_refs[h][...] for h in H]
    sb = [s[h].astype(BF16) for h in H]
    if with_q:
        qh = [q_all[:, lanes[h]] for h in H]
        qh = [qh[h] * (lax.rsqrt(jnp.sum(qh[h] * qh[h], axis=-1, keepdims=True) + EPS) * DN_DIM ** -0.5)
              for h in H]
        qk = [(_dot_nt(qh[h].astype(BF16), kb[h]) * decay[h]).astype(BF16) for h in H]
        wq = [jnp.concatenate([w[h], qh[h] * jnp.exp(gc[h])], axis=0).astype(BF16) for h in H]
        ws = [_dot(wq[h], sb[h]) for h in H]
        ub = [(u0[h] - ws[h][:C]).astype(BF16) for h in H]
        qu = [_dot(qk[h], ub[h]) for h in H]
        for h in H:
            o_ref[:, lanes[h]] = (ws[h][C:] + qu[h]).astype(o_ref.dtype)
    else:
        ws = [_dot(w[h].astype(BF16), sb[h]) for h in H]
        ub = [(u0[h] - ws[h]).astype(BF16) for h in H]
    ku = [_dot_tn(k_end[h], ub[h]) for h in H]
    for h in H:
        s_refs[h][...] = jnp.exp(gam_tot[:, h:h + 1]) * s[h] + ku[h]

    @pl.when(c == n_chunks - 1)
    def _():
        for h in range(DN_HEADS):
            sfin_ref[h] = s_refs[h][...]


def _deltanet(main3, gate_col, gate_row, conv_w, alog, dtb, s0, with_q):
    B, T, _ = main3.shape
    C = DN_CHUNK
    n_chunks = T // C
    hpc = C // DN_HALO
    n_halo = T // DN_HALO

    def cix(d, c):
        return jnp.where(d == 0, c, n_chunks - 1 - c)

    def trio(col):
        return [pl.BlockSpec((None, DN_HALO, D_MODEL),
                             lambda d, b, c: (b, jnp.maximum(cix(d, c) * hpc - 1, 0), col)),
                pl.BlockSpec((None, C, D_MODEL), lambda d, b, c: (b, cix(d, c), col)),
                pl.BlockSpec((None, DN_HALO, D_MODEL),
                             lambda d, b, c: (b, jnp.minimum((cix(d, c) + 1) * hpc, n_halo - 1), col))]

    in_specs = (trio(COL_DN_Q) if with_q else []) + trio(COL_DN_K) + trio(COL_DN_V) + [
        pl.BlockSpec((None, None, C, 2 * DN_HEADS), lambda d, b, c: (d, b, cix(d, c), 0)),
        pl.BlockSpec((None, None, None, 2 * DN_HEADS, C), lambda d, b, c: (d, b, cix(d, c), 0, 0)),
        pl.BlockSpec((DN_CONV, 3 * D_MODEL), lambda d, b, c: (0, 0)),
        pl.BlockSpec((None, 1, DN_HEADS), lambda d, b, c: (d, 0, 0)),
        pl.BlockSpec((None, DN_HEADS, 1), lambda d, b, c: (d, 0, 0)),
        pl.BlockSpec((None, 1, DN_HEADS), lambda d, b, c: (d, 0, 0)),
        pl.BlockSpec((None, DN_HEADS, 1), lambda d, b, c: (d, 0, 0)),
        pl.BlockSpec((None, None, DN_HEADS, DN_DIM, DN_DIM), lambda d, b, c: (d, b, 0, 0, 0)),
    ]
    s_spec = pl.BlockSpec((None, None, DN_HEADS, DN_DIM, DN_DIM), lambda d, b, c: (d, b, 0, 0, 0))
    s_shape = jax.ShapeDtypeStruct((2, B, DN_HEADS, DN_DIM, DN_DIM), F32)
    if with_q:
        out_specs = [pl.BlockSpec((None, None, C, D_MODEL), lambda d, b, c: (d, b, cix(d, c), 0)), s_spec]
        out_shape = [jax.ShapeDtypeStruct((2, B, T, D_MODEL), BF16), s_shape]
    else:
        out_specs = [s_spec]
        out_shape = [s_shape]
    n_main = 3 if with_q else 2
    args = [main3] * (3 * n_main) + [
        gate_col, gate_row, conv_w,
        alog.reshape(2, 1, DN_HEADS), alog.reshape(2, DN_HEADS, 1),
        dtb.reshape(2, 1, DN_HEADS), dtb.reshape(2, DN_HEADS, 1), s0]
    out = pl.pallas_call(
        functools.partial(_dn_kernel, with_q=with_q, n_chunks=n_chunks),
        grid=(2, B, n_chunks),
        in_specs=in_specs, out_specs=out_specs, out_shape=out_shape,
        scratch_shapes=[pltpu.VMEM((C + 2 * DN_HALO, D_MODEL), F32)] * 3
        + [pltpu.VMEM((DN_DIM, DN_DIM), F32)] * DN_HEADS,
        compiler_params=_cparams(("arbitrary", "arbitrary", "arbitrary")),
        name="deltanet_q" if with_q else "deltanet_state",
    )(*args)
    return (out[0], out[1]) if with_q else (None, out[0])


def _rope_tables(S):
    half = AT_DIM // 2
    nf = half // 2
    inv_freq = ROPE_BASE ** (-jnp.arange(nf, dtype=F32) / nf)
    t = jnp.arange(S, dtype=jnp.int32)
    row = (t // GRID_W).astype(F32)
    col = (t % GRID_W).astype(F32)
    lane = jnp.arange(LANES)
    dd = lane % AT_DIM
    pos = jnp.where((dd < half)[None, :], row[:, None], col[:, None])
    ang = pos * inv_freq[lane % nf][None, :]
    first = ((lane % half) < nf)[None, :]
    sin = jnp.sin(ang)
    return jnp.cos(ang), jnp.where(first, -sin, 0.0), jnp.where(first, 0.0, sin)


def _rope_kernel(q_ref, k_ref, v_ref, cos_ref, sa_ref, sb_ref, qo_ref, ko_ref, vo_ref):
    cos, sa, sb = cos_ref[...], sa_ref[...], sb_ref[...]
    nf = AT_DIM // 4

    def rot(x):
        return x * cos + pltpu.roll(x, LANES - nf, 1) * sa + pltpu.roll(x, nf, 1) * sb

    for j in range(AT_Q_HEADS * AT_DIM // LANES):
        lanes = slice(j * LANES, (j + 1) * LANES)
        qo_ref[:, lanes] = (rot(q_ref[:, lanes].astype(F32)) * AT_DIM ** -0.5).astype(BF16)
    ko_ref[...] = rot(k_ref[...]).astype(BF16)
    vo_ref[...] = v_ref[...].astype(BF16)


def _rope(main, small, tables, S, tm=512):
    n_tok = main.shape[0]
    per_seq = S // tm
    tab_spec = pl.BlockSpec((tm, LANES), lambda i: (i % per_seq, 0))
    return pl.pallas_call(
        _rope_kernel,
        grid=(n_tok // tm,),
        in_specs=[pl.BlockSpec((tm, D_MODEL), lambda i: (i, COL_AT_Q)),
                  pl.BlockSpec((tm, LANES), lambda i: (i, 0)),
                  pl.BlockSpec((tm, LANES), lambda i: (i, 1)),
                  tab_spec, tab_spec, tab_spec],
        out_specs=[pl.BlockSpec((tm, D_MODEL), lambda i: (i, 0)),
                   pl.BlockSpec((tm, LANES), lambda i: (i, 0)),
                   pl.BlockSpec((tm, LANES), lambda i: (i, 0))],
        out_shape=[jax.ShapeDtypeStruct((n_tok, D_MODEL), BF16),
                   jax.ShapeDtypeStruct((n_tok, LANES), BF16),
                   jax.ShapeDtypeStruct((n_tok, LANES), BF16)],
        compiler_params=_cparams(("arbitrary",)),
        name="rope",
    )(main, small, small, *tables)


def _attn_kernel(*refs, local, n_blocks, q_scale):
    if local:
        (q_ref, kp_ref, kc_ref, kn_ref, vp_ref, vc_ref, vn_ref, kx_ref, vx_ref, sink_ref, o_ref) = refs
    else:
        (q_ref, kx_ref, vx_ref, sink_ref, o_ref) = refs
    P = AT_BLOCK
    G = AT_Q_HEADS // AT_KV_HEADS
    L = kx_ref.shape[0]
    kx = kx_ref[...].astype(BF16)
    vx = vx_ref[...].astype(BF16)
    if local:
        i = pl.program_id(1)
        k_all = jnp.concatenate([kp_ref[...], kc_ref[...], kn_ref[...], kx], axis=0)
        v_all = jnp.concatenate([vp_ref[...], vc_ref[...], vn_ref[...], vx], axis=0)
        qi = lax.broadcasted_iota(I32, (P, P), 0)
        kj = lax.broadcasted_iota(I32, (P, P), 1)
        b_prev = jnp.where(kj >= qi, 0.0, NEG_BIG) + jnp.where(i > 0, 0.0, NEG_BIG)
        b_next = jnp.where(kj <= qi, 0.0, NEG_BIG) + jnp.where(i < n_blocks - 1, 0.0, NEG_BIG)
        bias = jnp.concatenate([b_prev, jnp.zeros((P, P), F32), b_next, jnp.zeros((P, L), F32)], axis=1)
    else:
        k_all, v_all, bias = kx, vx, None
    for qh in range(AT_Q_HEADS):
        hk = qh // G
        q = q_ref[:, qh * AT_DIM:(qh + 1) * AT_DIM]
        if q_scale != 1.0:
            q = (q.astype(F32) * q_scale).astype(BF16)
        kh = k_all[:, hk * AT_DIM:(hk + 1) * AT_DIM]
        vh = v_all[:, hk * AT_DIM:(hk + 1) * AT_DIM]
        s = _dot_nt(q, kh)
        if bias is not None:
            s = s + bias
        sink = sink_ref[:, qh:qh + 1]
        m = jnp.maximum(jnp.max(s, axis=-1, keepdims=True), sink)
        p = jnp.exp(s - m)
        den = jnp.sum(p, axis=-1, keepdims=True) + jnp.exp(sink - m)
        o = _dot(p.astype(BF16), vh) / den
        o_ref[:, qh * AT_DIM:(qh + 1) * AT_DIM] = o.astype(BF16)


def _attention_local(q_r, k_r, v_r, small_c, sinks, B, S, L):
    P = AT_BLOCK
    nb = S // P

    def kv_trio():
        return [pl.BlockSpec((P, LANES), lambda b, i: (b * nb + jnp.maximum(i - 1, 0), 0)),
                pl.BlockSpec((P, LANES), lambda b, i: (b * nb + i, 0)),
                pl.BlockSpec((P, LANES), lambda b, i: (b * nb + jnp.minimum(i + 1, nb - 1), 0))]

    return pl.pallas_call(
        functools.partial(_attn_kernel, local=True, n_blocks=nb, q_scale=1.0),
        grid=(B, nb),
        in_specs=[pl.BlockSpec((P, D_MODEL), lambda b, i: (b * nb + i, 0))] + kv_trio() + kv_trio() + [
            pl.BlockSpec((L, LANES), lambda b, i: (b, 0)),
            pl.BlockSpec((L, LANES), lambda b, i: (b, 1)),
            pl.BlockSpec((1, AT_Q_HEADS), lambda b, i: (0, 0))],
        out_specs=pl.BlockSpec((P, D_MODEL), lambda b, i: (b * nb + i, 0)),
        out_shape=jax.ShapeDtypeStruct((B * S, D_MODEL), BF16),
        compiler_params=_cparams(("arbitrary", "arbitrary")),
        name="attn_local",
    )(q_r, k_r, k_r, k_r, v_r, v_r, v_r, small_c, small_c, sinks)


def _attention_ctx(main_c, small_c, sinks, B, L):
    P = AT_BLOCK
    nb = L // P
    return pl.pallas_call(
        functools.partial(_attn_kernel, local=False, n_blocks=nb, q_scale=AT_DIM ** -0.5),
        grid=(B, nb),
        in_specs=[pl.BlockSpec((P, D_MODEL), lambda b, i: (b * nb + i, COL_AT_Q)),
                  pl.BlockSpec((L, LANES), lambda b, i: (b, 0)),
                  pl.BlockSpec((L, LANES), lambda b, i: (b, 1)),
                  pl.BlockSpec((1, AT_Q_HEADS), lambda b, i: (0, 0))],
        out_specs=pl.BlockSpec((P, D_MODEL), lambda b, i: (b * nb + i, 0)),
        out_shape=jax.ShapeDtypeStruct((B * L, D_MODEL), BF16),
        compiler_params=_cparams(("arbitrary", "arbitrary")),
        name="attn_ctx",
    )(main_c, small_c, small_c, sinks)


def _merge_kernel(ysg_ref, odn_ref, dng_ref, yat_ref, g0_ref, g1_ref, g2_ref, x_ref, mod_ref,
                  dn_norm_ref, post_ref, pre_ref, wsg_ref, wdn_ref, wat_ref, wout_ref, rw_ref, rb_ref,
                  xo_ref, h2_ref, lg_ref):
    o = odn_ref[0].astype(F32) + odn_ref[1].astype(F32)
    dn_g = dn_norm_ref[...]
    parts = []
    for h in range(DN_HEADS):
        lanes = slice(h * DN_DIM, (h + 1) * DN_DIM)
        parts.append(_rms(o[:, lanes], dn_g) * _silu(dng_ref[:, lanes].astype(F32)))
    ydn = jnp.concatenate(parts, axis=1).astype(BF16)
    m = (_sigmoid(g0_ref[...].astype(F32)) * _dot(ysg_ref[...], wsg_ref[...])
         + _sigmoid(g1_ref[...].astype(F32)) * _dot(ydn, wdn_ref[...])
         + _sigmoid(g2_ref[...].astype(F32)) * _dot(yat_ref[...], wat_ref[...]))
    y = _dot(m.astype(BF16), wout_ref[...])
    gate1 = mod_ref[:, 2 * D_MODEL:3 * D_MODEL]
    sh2 = mod_ref[:, 3 * D_MODEL:4 * D_MODEL]
    sc2 = mod_ref[:, 4 * D_MODEL:5 * D_MODEL]
    xn = x_ref[...] + gate1 * _rms(y, post_ref[...])
    xo_ref[...] = xn
    h2 = _rms(xn, pre_ref[...]) * (1.0 + sc2) + sh2
    h2_ref[...] = h2
    lg_ref[...] = _dot(h2.astype(BF16), rw_ref[...]) + rb_ref[...]


def _merge(ysg, odn, main, yat, x, mod, mod_row, lw, tm=256):
    n_tok = x.shape[0]
    const = lambda i: (0, 0)
    wspec = pl.BlockSpec((D_MODEL, D_MODEL), const, pipeline_mode=pl.Buffered(1))
    vspec = pl.BlockSpec((1, D_MODEL), const)
    return pl.pallas_call(
        _merge_kernel,
        grid=(n_tok // tm,),
        in_specs=[pl.BlockSpec((tm, D_MODEL), lambda i: (i, 0)),
                  pl.BlockSpec((2, tm, D_MODEL), lambda i: (0, i, 0)),
                  pl.BlockSpec((tm, D_MODEL), lambda i: (i, COL_DN_G)),
                  pl.BlockSpec((tm, D_MODEL), lambda i: (i, 0)),
                  pl.BlockSpec((tm, D_MODEL), lambda i: (i, COL_GATE0)),
                  pl.BlockSpec((tm, D_MODEL), lambda i: (i, COL_GATE0 + 1)),
                  pl.BlockSpec((tm, D_MODEL), lambda i: (i, COL_GATE0 + 2)),
                  pl.BlockSpec((tm, D_MODEL), lambda i: (i, 0)),
                  pl.BlockSpec((None, 1, 6 * D_MODEL), lambda i: (mod_row(i * tm), 0, 0)),
                  pl.BlockSpec((1, DN_DIM), const), vspec, vspec,
                  wspec, wspec, wspec, wspec,
                  pl.BlockSpec((D_MODEL, LANES), const), pl.BlockSpec((1, LANES), const)],
        out_specs=[pl.BlockSpec((tm, D_MODEL), lambda i: (i, 0)),
                   pl.BlockSpec((tm, D_MODEL), lambda i: (i, 0)),
                   pl.BlockSpec((tm, LANES), lambda i: (i, 0))],
        out_shape=[jax.ShapeDtypeStruct((n_tok, D_MODEL), F32),
                   jax.ShapeDtypeStruct((n_tok, D_MODEL), F32),
                   jax.ShapeDtypeStruct((n_tok, LANES), F32)],
        compiler_params=_cparams(("arbitrary",)),
        name="merge",
    )(ysg, odn, main, yat, main, main, main, x, mod,
      lw["dn_norm_g"], lw["norm_post_mix"], lw["norm_pre_ffn"],
      lw["w_proj_sg"], lw["w_proj_dn"], lw["w_proj_at"], lw["w_out"], lw["router_w"], lw["router_b"])


def _route_kernel(lg_ref, idx_ref, gate_ref, rank_ref, cnt_ref, run_ref):
    tm = lg_ref.shape[0]

    @pl.when(pl.program_id(0) == 0)
    def _():
        run_ref[...] = jnp.zeros_like(run_ref)

    l = lg_ref[...]
    lane = lax.broadcasted_iota(I32, l.shape, 1).astype(F32)
    vals, onehots = [], []
    for k in range(TOP_K):
        m = jnp.max(l, axis=-1, keepdims=True)
        ik = jnp.min(jnp.where(l == m, lane, float(LANES)), axis=-1, keepdims=True)
        oh = lane == ik
        idx_ref[:, k:k + 1] = ik.astype(I32)
        vals.append(m)
        onehots.append(oh)
        l = jnp.where(oh, -jnp.inf, l)
    es = [jnp.exp(v - vals[0]) for v in vals]
    den = es[0] + es[1] + es[2] + es[3]
    sel = jnp.zeros(l.shape, F32)
    for k in range(TOP_K):
        gate_ref[:, k:k + 1] = es[k] / den
        sel = sel + onehots[k].astype(F32)
    ri = lax.broadcasted_iota(I32, (tm, tm), 0)
    ci = lax.broadcasted_iota(I32, (tm, tm), 1)
    before = _dot((ri > ci).astype(BF16), sel.astype(BF16)) + run_ref[...]
    for k in range(TOP_K):
        rank_ref[:, k:k + 1] = jnp.sum(jnp.where(onehots[k], before, 0.0), axis=-1,
                                       keepdims=True).astype(I32)
    run_ref[...] = run_ref[...] + jnp.sum(sel, axis=0, keepdims=True)
    cnt_ref[...] = run_ref[...]


def _route(logits, tm=256):
    n_tok = logits.shape[0]
    small = lambda dt: jax.ShapeDtypeStruct((n_tok, TOP_K), dt)
    kspec = pl.BlockSpec((tm, TOP_K), lambda i: (i, 0))
    return pl.pallas_call(
        _route_kernel,
        grid=(n_tok // tm,),
        in_specs=[pl.BlockSpec((tm, LANES), lambda i: (i, 0))],
        out_specs=[kspec, kspec, kspec, pl.BlockSpec((1, LANES), lambda i: (0, 0))],
        out_shape=[small(I32), small(F32), small(I32), jax.ShapeDtypeStruct((1, LANES), F32)],
        scratch_shapes=[pltpu.VMEM((1, LANES), F32)],
        compiler_params=_cparams(("arbitrary",)),
        name="route",
    )(logits)


def _dispatch_kernel(dest_ref, h_ref, xs_in_ref, xs_ref, sem):
    del xs_in_ref
    tm = h_ref.shape[0]

    def row_copy(r, k):
        return pltpu.make_async_copy(h_ref.at[pl.ds(r, 1)],
                                     xs_ref.at[pl.ds(dest_ref[r * TOP_K + k], 1)], sem)

    def issue(r, carry):
        for k in range(TOP_K):
            row_copy(r, k).start()
        return carry

    lax.fori_loop(0, tm, issue, 0)

    def drain(r, carry):
        for k in range(TOP_K):
            row_copy(r, k).wait()
        return carry

    lax.fori_loop(0, tm, drain, 0)


def _dispatch(dest_flat, h2, n_slots, tm=256):
    n_tok = h2.shape[0]
    xs0 = jnp.zeros((n_slots, D_MODEL), F32)
    return pl.pallas_call(
        _dispatch_kernel,
        grid=(n_tok // tm,),
        in_specs=[pl.BlockSpec((tm * TOP_K,), lambda i: (i,), memory_space=pltpu.SMEM),
                  pl.BlockSpec((tm, D_MODEL), lambda i: (i, 0)),
                  pl.BlockSpec(memory_space=pl.ANY)],
        out_specs=pl.BlockSpec(memory_space=pl.ANY),
        out_shape=jax.ShapeDtypeStruct((n_slots, D_MODEL), F32),
        scratch_shapes=[pltpu.SemaphoreType.DMA],
        input_output_aliases={2: 0},
        compiler_params=_cparams(("arbitrary",)),
        name="moe_dispatch",
    )(dest_flat, h2, xs0)


def _expert_kernel(te_ref, nu_ref, xs_ref, wgu_ref, bgu_ref, wd_ref, bd_ref, y_ref):
    del te_ref

    @pl.when(pl.program_id(0) < nu_ref[0])
    def _():
        gu = _dot(xs_ref[...].astype(BF16), wgu_ref[...]) + bgu_ref[...]
        g = jnp.minimum(gu[:, :D_EXPERT], SWIGLU_LIMIT)
        lin = jnp.clip(gu[:, D_EXPERT:], -SWIGLU_LIMIT, SWIGLU_LIMIT)
        act = g * _sigmoid(SWIGLU_ALPHA * g) * (lin + 1.0)
        y_ref[...] = _dot(act.astype(BF16), wd_ref[...]) + bd_ref[...]


def _experts(tile_expert, n_used, xs, wgu, bgu, wd, bd, tm):
    n_slots = xs.shape[0]
    n_tiles = n_slots // tm

    def row(i, te, nu):
        return (jnp.minimum(i, nu[0] - 1), 0)

    grid_spec = pltpu.PrefetchScalarGridSpec(
        num_scalar_prefetch=2,
        grid=(n_tiles,),
        in_specs=[pl.BlockSpec((tm, D_MODEL), row),
                  pl.BlockSpec((None, D_MODEL, 2 * D_EXPERT), lambda i, te, nu: (te[i], 0, 0)),
                  pl.BlockSpec((None, 1, 2 * D_EXPERT), lambda i, te, nu: (te[i], 0, 0)),
                  pl.BlockSpec((None, D_EXPERT, D_MODEL), lambda i, te, nu: (te[i], 0, 0)),
                  pl.BlockSpec((None, 1, D_MODEL), lambda i, te, nu: (te[i], 0, 0))],
        out_specs=pl.BlockSpec((tm, D_MODEL), row),
    )
    return pl.pallas_call(
        _expert_kernel,
        grid_spec=grid_spec,
        out_shape=jax.ShapeDtypeStruct((n_slots, D_MODEL), F32),
        compiler_params=_cparams(("arbitrary",)),
        name="moe_experts",
    )(tile_expert, n_used, xs, wgu, bgu, wd, bd)


def _combine_kernel(dest_ref, gate_ref, x_ref, mod_ref, post_ref, y_ref, xo_ref, buf_ref, sem):
    tm = x_ref.shape[0]

    def row_copy(r, k):
        return pltpu.make_async_copy(y_ref.at[pl.ds(dest_ref[r * TOP_K + k], 1)],
                                     buf_ref.at[k, pl.ds(r, 1)], sem)

    def issue(r, carry):
        for k in range(TOP_K):
            row_copy(r, k).start()
        return carry

    lax.fori_loop(0, tm, issue, 0)

    def drain(r, carry):
        for k in range(TOP_K):
            row_copy(r, k).wait()
        return carry

    lax.fori_loop(0, tm, drain, 0)

    y = None
    for k in range(TOP_K):
        t = buf_ref[k] * gate_ref[:, k:k + 1]
        y = t if y is None else y + t
    gate2 = mod_ref[:, 5 * D_MODEL:6 * D_MODEL]
    xo_ref[...] = x_ref[...] + gate2 * _rms(y, post_ref[...])


def _combine(dest_flat, gate, x_mid, mod, mod_row, post_g, y, tm=256):
    n_tok = x_mid.shape[0]
    return pl.pallas_call(
        _combine_kernel,
        grid=(n_tok // tm,),
        in_specs=[pl.BlockSpec((tm * TOP_K,), lambda i: (i,), memory_space=pltpu.SMEM),
                  pl.BlockSpec((tm, TOP_K), lambda i: (i, 0)),
                  pl.BlockSpec((tm, D_MODEL), lambda i: (i, 0)),
                  pl.BlockSpec((None, 1, 6 * D_MODEL), lambda i: (mod_row(i * tm), 0, 0)),
                  pl.BlockSpec((1, D_MODEL), lambda i: (0, 0)),
                  pl.BlockSpec(memory_space=pl.ANY)],
        out_specs=pl.BlockSpec((tm, D_MODEL), lambda i: (i, 0)),
        out_shape=jax.ShapeDtypeStruct((n_tok, D_MODEL), F32),
        scratch_shapes=[pltpu.VMEM((TOP_K, tm, D_MODEL), F32), pltpu.SemaphoreType.DMA],
        compiler_params=_cparams(("arbitrary",)),
        name="moe_combine",
    )(dest_flat, gate, x_mid, mod, post_g, y)


def _moe(h2, logits, x_mid, mod, mod_row, lw, tm_e=512):
    n_tok = h2.shape[0]
    idx, gate, rank, cnt = _route(logits)
    counts = cnt[0, :N_EXPERTS].astype(I32)
    padded = (counts + tm_e - 1) // tm_e * tm_e
    pad_end = jnp.cumsum(padded)
    offs = pad_end - padded
    dest = (jnp.take(offs, idx, axis=0) + rank).reshape(-1).astype(I32)
    n_tiles = (n_tok * TOP_K + tm_e - 1) // tm_e + N_EXPERTS
    tile_expert = jnp.minimum(
        jnp.searchsorted(pad_end, jnp.arange(n_tiles, dtype=I32) * tm_e, side="right"),
        N_EXPERTS - 1).astype(I32)
    n_used = (pad_end[-1:] // tm_e).astype(I32)
    xs = _dispatch(dest, h2, n_tiles * tm_e)
    y = _experts(tile_expert, n_used, xs, lw["exp_w_gu"], lw["exp_b_gu"], lw["exp_w_down"],
                 lw["exp_b_down"], tm_e)
    return _combine(dest, gate, x_mid, mod, mod_row, lw["norm_post_ffn"], y)


def _split_w_in(w_in):
    offs, o = {}, 0
    for name, width in (("dn_k", 1024), ("dn_v", 1024), ("dn_a", 16), ("dn_b", 16), ("at_k", 128),
                        ("at_v", 128), ("dn_q", 1024), ("dn_g", 1024), ("at_q", 1024),
                        ("sg_u", 1024), ("sg_v", 1024), ("gates", 3072)):
        offs[name] = (o, o + width)
        o += width
    sl = lambda n: w_in[:, offs[n][0]:offs[n][1]]
    w_main = jnp.concatenate([sl(n) for n in ("dn_k", "dn_v", "dn_q", "dn_g", "at_q", "sg_u", "sg_v",
                                              "gates")], axis=1).astype(BF16)
    pad = jnp.zeros((w_in.shape[0], N_SMALL_COLS - 2 * LANES - 4 * DN_HEADS), w_in.dtype)
    w_small = jnp.concatenate([sl("at_k"), sl("at_v"), sl("dn_a"), sl("dn_b"), pad], axis=1).astype(BF16)
    return w_main, w_small


def _dn_gates(small, B, T):
    ab = small[:, 2 * LANES:2 * LANES + 4 * DN_HEADS].reshape(B, T, 2, 2, DN_HEADS)
    col = jnp.transpose(ab, (3, 0, 1, 2, 4)).reshape(2, B, T, 2 * DN_HEADS)
    row = jnp.transpose(col.reshape(2, B, T // DN_CHUNK, DN_CHUNK, 2 * DN_HEADS), (0, 1, 2, 4, 3))
    return col, row


def kernel(x, c, ctx, c_ctx, w_mod, b_mod, norm_pre_mix, norm_post_mix, norm_pre_ffn, norm_post_ffn, w_in, sg_ln_g, sg_ln_b, sg_w, sg_b, dn_conv_w, dn_a_log, dn_dt_bias, dn_norm_g, at_sinks, w_proj_sg, w_proj_dn, w_proj_at, w_out, router_w, router_b, exp_w_gu, exp_b_gu, exp_w_down, exp_b_down):
    B, S, D = x.shape
    L = ctx.shape[1]
    depth = w_mod.shape[0]
    assert D == D_MODEL and S % GRID_W == 0
    n_lat, n_ctx = B * S, B * L

    rows = (B + 1 + 7) // 8 * 8
    cvec = jnp.zeros((rows, D), F32).at[:B].set(c).at[B].set(c_ctx)
    mod_all = _modulation(cvec, w_mod, b_mod)
    tables = _rope_tables(S)

    lat_row = lambda t: t // S
    ctx_row = lambda t: B
    all_row = lambda t: jnp.where(t < n_lat, t // S, B)

    xl = x.reshape(n_lat, D)
    xc = ctx.reshape(n_ctx, D)
    for l in range(depth):
        need_ctx_out = l < depth - 1
        mod = mod_all[l].reshape(rows, 1, 6 * D)
        w_main, w_small = _split_w_in(w_in[l])
        lw = {
            "dn_norm_g": dn_norm_g[l].reshape(1, -1),
            "norm_post_mix": norm_post_mix[l].reshape(1, -1),
            "norm_pre_ffn": norm_pre_ffn[l].reshape(1, -1),
            "norm_post_ffn": norm_post_ffn[l].reshape(1, -1),
            "w_proj_sg": w_proj_sg[l].astype(BF16), "w_proj_dn": w_proj_dn[l].astype(BF16),
            "w_proj_at": w_proj_at[l].astype(BF16), "w_out": w_out[l].astype(BF16),
            "router_w": jnp.pad(router_w[l], ((0, 0), (0, LANES - N_EXPERTS))).astype(BF16),
            "router_b": jnp.pad(router_b[l], (0, LANES - N_EXPERTS),
                                constant_values=NEG_BIG).reshape(1, -1),
            "exp_w_gu": exp_w_gu[l].astype(BF16), "exp_b_gu": exp_b_gu[l].reshape(N_EXPERTS, 1, -1),
            "exp_w_down": exp_w_down[l].astype(BF16), "exp_b_down": exp_b_down[l].reshape(N_EXPERTS, 1, -1),
        }
        pre_g = norm_pre_mix[l].reshape(1, -1)
        main, small = _inproj(xl, mod, lat_row, pre_g, w_main, w_small, min(1024, S))
        w_main_c = w_main if need_ctx_out else w_main[:, :N_CTX_MAIN_COLS]
        main_c, small_c = _inproj(xc, mod, ctx_row, pre_g, w_main_c, w_small, min(1024, n_ctx))

        sg_args = (sg_ln_g[l].reshape(1, -1), sg_ln_b[l].reshape(1, -1), sg_w[l].astype(BF16),
                   sg_b[l].T)
        ysg = _sgu(main, *sg_args)

        gcol_c, grow_c = _dn_gates(small_c, B, L)
        gcol, grow = _dn_gates(small, B, S)
        s0 = jnp.zeros((2, B, DN_HEADS, DN_DIM, DN_DIM), F32)
        odn_c, s_ctx = _deltanet(main_c.reshape(B, L, -1), gcol_c, grow_c, dn_conv_w[l], dn_a_log[l],
                                 dn_dt_bias[l], s0, need_ctx_out)
        odn, _ = _deltanet(main.reshape(B, S, -1), gcol, grow, dn_conv_w[l], dn_a_log[l],
                           dn_dt_bias[l], s_ctx, True)

        sinks = at_sinks[l].reshape(1, -1)
        q_r, k_r, v_r = _rope(main, small, tables, S)
        yat = _attention_local(q_r, k_r, v_r, small_c, sinks, B, S, L)

        x_mid, h2, logits = _merge(ysg, odn.reshape(2, n_lat, D), main, yat, xl, mod, lat_row, lw)
        if need_ctx_out:
            ysg_c = _sgu(main_c, *sg_args)
            yat_c = _attention_ctx(main_c, small_c, sinks, B, L)
            xc_mid, h2c, logits_c = _merge(ysg_c, odn_c.reshape(2, n_ctx, D), main_c, yat_c, xc, mod,
                                           ctx_row, lw)
            x_mid = jnp.concatenate([x_mid, xc_mid], axis=0)
            h2 = jnp.concatenate([h2, h2c], axis=0)
            logits = jnp.concatenate([logits, logits_c], axis=0)
            xo = _moe(h2, logits, x_mid, mod, all_row, lw)
            xl, xc = xo[:n_lat], xo[n_lat:]
        else:
            xl = _moe(h2, logits, x_mid, mod, lat_row, lw)
    return xl.reshape(B, S, D)
```

```python
import functools
import math

import jax
import jax.numpy as jnp
from jax import lax
from jax.experimental import pallas as pl
from jax.experimental.pallas import tpu as pltpu

F32 = jnp.float32
BF16 = jnp.bfloat16
I32 = jnp.int32

EPS = 1e-6
D_MODEL = 1024
GRID_W = 64

SG_CHUNK = 128
SG_GROUPS = 8

DN_HEADS = 8
DN_DIM = 128
DN_CONV = 5
DN_CHUNK = 64
DN_HALO = 16

AT_Q_HEADS = 16
AT_KV_HEADS = 2
AT_DIM = 64
AT_BLOCK = 128
ROPE_BASE = 10000.0

N_EXPERTS = 32
TOP_K = 4
D_EXPERT = 1024
SWIGLU_ALPHA = 1.702
SWIGLU_LIMIT = 7.0
N_BRANCH = 3

LANES = 128
NEG_BIG = -1e30

COL_DN_K, COL_DN_V, COL_DN_Q, COL_DN_G, COL_AT_Q, COL_SG_U, COL_SG_V, COL_GATE0 = range(8)
N_MAIN_COLS = 10 * D_MODEL
N_CTX_MAIN_COLS = 2 * D_MODEL
N_SMALL_COLS = 3 * LANES

VMEM_LIMIT = 52 * 1024 * 1024


def _cparams(sem):
    return pltpu.CompilerParams(dimension_semantics=sem, vmem_limit_bytes=VMEM_LIMIT)


def _dot(a, b):
    return jnp.dot(a, b, preferred_element_type=F32)


def _dot_nt(a, b):
    return lax.dot_general(a, b, (((1,), (1,)), ((), ())), preferred_element_type=F32)


def _dot_tn(a, b):
    return lax.dot_general(a, b, (((0,), (0,)), ((), ())), preferred_element_type=F32)


def _sigmoid(x):
    return 1.0 / (1.0 + jnp.exp(-x))


def _silu(x):
    return x * _sigmoid(x)


def _gelu_tanh(x):
    return 0.5 * x * (1.0 + jnp.tanh(math.sqrt(2.0 / math.pi) * (x + 0.044715 * (x * x * x))))


def _softplus(x):
    return jnp.maximum(x, 0.0) + jnp.log(1.0 + jnp.exp(-jnp.abs(x)))


def _rms(x, g):
    return x * lax.rsqrt(jnp.mean(x * x, axis=-1, keepdims=True) + EPS) * g


def _mod_kernel(c_ref, w_ref, b_ref, o_ref):
    s = _silu(c_ref[...])
    o_ref[...] = jnp.dot(s, w_ref[...], preferred_element_type=F32,
                         precision=lax.Precision.HIGHEST) + b_ref[...]


def _modulation(cvec, w_mod, b_mod):
    depth = w_mod.shape[0]
    rows = cvec.shape[0]
    n_col = w_mod.shape[2] // D_MODEL
    return pl.pallas_call(
        _mod_kernel,
        grid=(depth, n_col),
        in_specs=[pl.BlockSpec((rows, D_MODEL), lambda l, j: (0, 0)),
                  pl.BlockSpec((None, D_MODEL, D_MODEL), lambda l, j: (l, 0, j)),
                  pl.BlockSpec((None, 1, D_MODEL), lambda l, j: (l, 0, j))],
        out_specs=pl.BlockSpec((None, rows, D_MODEL), lambda l, j: (l, 0, j)),
        out_shape=jax.ShapeDtypeStruct((depth, rows, w_mod.shape[2]), F32),
        compiler_params=_cparams(("arbitrary", "arbitrary")),
        name="modulation",
    )(cvec, w_mod, b_mod.reshape(depth, 1, -1))


def _inproj_kernel(x_ref, mod_ref, g_ref, wm_ref, ws_ref, main_ref, small_ref, h_ref):
    @pl.when(pl.program_id(1) == 0)
    def _():
        sh = mod_ref[:, 0 * D_MODEL:1 * D_MODEL]
        sc = mod_ref[:, 1 * D_MODEL:2 * D_MODEL]
        h = (_rms(x_ref[...], g_ref[...]) * (1.0 + sc) + sh).astype(BF16)
        h_ref[...] = h
        small_ref[...] = _dot(h, ws_ref[...])

    main_ref[...] = _dot(h_ref[...], wm_ref[...]).astype(BF16)


def _inproj(x, mod, mod_row, norm_g, w_main, w_small, tm, tn=1024):
    n_tok = x.shape[0]
    n_main = w_main.shape[1]
    return pl.pallas_call(
        _inproj_kernel,
        grid=(n_tok // tm, n_main // tn),
        in_specs=[pl.BlockSpec((tm, D_MODEL), lambda i, j: (i, 0)),
                  pl.BlockSpec((None, 1, 6 * D_MODEL), lambda i, j: (mod_row(i * tm), 0, 0)),
                  pl.BlockSpec((1, D_MODEL), lambda i, j: (0, 0)),
                  pl.BlockSpec((D_MODEL, tn), lambda i, j: (0, j)),
                  pl.BlockSpec((D_MODEL, N_SMALL_COLS), lambda i, j: (0, 0))],
        out_specs=[pl.BlockSpec((tm, tn), lambda i, j: (i, j)),
                   pl.BlockSpec((tm, N_SMALL_COLS), lambda i, j: (i, 0))],
        out_shape=[jax.ShapeDtypeStruct((n_tok, n_main), BF16),
                   jax.ShapeDtypeStruct((n_tok, N_SMALL_COLS), F32)],
        scratch_shapes=[pltpu.VMEM((tm, D_MODEL), BF16)],
        compiler_params=_cparams(("arbitrary", "arbitrary")),
        name="inproj",
    )(x, mod, norm_g, w_main, w_small)


def _sgu_kernel(u_ref, v_ref, lng_ref, lnb_ref, ws_ref, bs_ref, o_ref, *, n_chunk):
    u = _gelu_tanh(u_ref[...].astype(F32))
    v = _gelu_tanh(v_ref[...].astype(F32))
    vc = v - jnp.mean(v, axis=-1, keepdims=True)
    var = jnp.mean(vc * vc, axis=-1, keepdims=True)
    vn = (vc * lax.rsqrt(var + EPS) * lng_ref[...] + lnb_ref[...]).astype(BF16)
    for n in range(n_chunk):
        rows = slice(n * SG_CHUNK, (n + 1) * SG_CHUNK)
        for g in range(SG_GROUPS):
            cols = slice(g * LANES, (g + 1) * LANES)
            mixed = _dot(ws_ref[g], vn[rows, cols]) + bs_ref[:, g:g + 1]
            o_ref[rows, cols] = (u[rows, cols] * mixed).astype(BF16)


def _sgu(main, sg_ln_g, sg_ln_b, sg_w, sg_bt, n_chunk=2):
    n_tok = main.shape[0]
    tc = n_chunk * SG_CHUNK
    return pl.pallas_call(
        functools.partial(_sgu_kernel, n_chunk=n_chunk),
        grid=(n_tok // tc,),
        in_specs=[pl.BlockSpec((tc, D_MODEL), lambda i: (i, COL_SG_U)),
                  pl.BlockSpec((tc, D_MODEL), lambda i: (i, COL_SG_V)),
                  pl.BlockSpec((1, D_MODEL), lambda i: (0, 0)),
                  pl.BlockSpec((1, D_MODEL), lambda i: (0, 0)),
                  pl.BlockSpec((SG_GROUPS, SG_CHUNK, SG_CHUNK), lambda i: (0, 0, 0)),
                  pl.BlockSpec((SG_CHUNK, SG_GROUPS), lambda i: (0, 0))],
        out_specs=pl.BlockSpec((tc, D_MODEL), lambda i: (i, 0)),
        out_shape=jax.ShapeDtypeStruct((n_tok, D_MODEL), BF16),
        compiler_params=_cparams(("arbitrary",)),
        name="sgu",
    )(main, main, sg_ln_g, sg_ln_b, sg_w, sg_bt)


def _dn_kernel(*refs, with_q, n_chunks):
    s_refs = refs[-DN_HEADS:]
    ext_refs = refs[-DN_HEADS - 3:-DN_HEADS]
    refs = refs[:-DN_HEADS - 3]
    if with_q:
        (qp_ref, qc_ref, qn_ref, kp_ref, kc_ref, kn_ref, vp_ref, vc_ref, vn_ref,
         gcol_ref, grow_ref, cw_ref, alog_r_ref, alog_c_ref, dtb_r_ref, dtb_c_ref, s0_ref,
         o_ref, sfin_ref) = refs
    else:
        (kp_ref, kc_ref, kn_ref, vp_ref, vc_ref, vn_ref,
         gcol_ref, grow_ref, cw_ref, alog_r_ref, alog_c_ref, dtb_r_ref, dtb_c_ref, s0_ref,
         sfin_ref) = refs
    d = pl.program_id(0)
    c = pl.program_id(2)
    is_fwd = d == 0
    cidx = jnp.where(is_fwd, c, n_chunks - 1 - c)
    C = DN_CHUNK

    @pl.when(c == 0)
    def _():
        for h in range(DN_HEADS):
            s_refs[h][...] = s0_ref[h]

    has_prev = (cidx > 0).astype(F32)
    has_next = (cidx < n_chunks - 1).astype(F32)

    def conv_silu(p_ref, c_ref, n_ref, part):
        ext_ref = ext_refs[part]
        ext_ref[0:DN_HALO, :] = p_ref[...].astype(F32) * has_prev
        ext_ref[DN_HALO:DN_HALO + C, :] = c_ref[...].astype(F32)
        ext_ref[DN_HALO + C:2 * DN_HALO + C, :] = n_ref[...].astype(F32) * has_next
        base = DN_HALO - DN_CONV // 2
        y = None
        for i in range(DN_CONV):
            w = cw_ref[i:i + 1, part * D_MODEL:(part + 1) * D_MODEL]
            t = ext_ref[base + i:base + i + C, :] * w
            y = t if y is None else y + t
        return _silu(y)

    k_all = conv_silu(kp_ref, kc_ref, kn_ref, 1)
    v_all = conv_silu(vp_ref, vc_ref, vn_ref, 2)
    q_all = conv_silu(qp_ref, qc_ref, qn_ref, 0) if with_q else None

    gcol = gcol_ref[...]
    ld_col = -jnp.exp(alog_r_ref[...]) * _softplus(gcol[:, 0:DN_HEADS] + dtb_r_ref[...])
    beta_col = _sigmoid(gcol[:, DN_HEADS:2 * DN_HEADS])
    ld_row = -jnp.exp(alog_c_ref[...]) * _softplus(grow_ref[0:DN_HEADS, :] + dtb_c_ref[...])

    ri = lax.broadcasted_iota(I32, (C, C), 0)
    ci = lax.broadcasted_iota(I32, (C, C), 1)
    delta = (ri - ci) * (1 - 2 * d)
    incl = delta >= 0
    strict = delta > 0
    incl_t = delta <= 0
    gam_col = jnp.dot(incl.astype(F32), ld_col, preferred_element_type=F32,
                      precision=lax.Precision.HIGHEST)
    gam_row = jnp.dot(ld_row, incl_t.astype(F32), preferred_element_type=F32,
                      precision=lax.Precision.HIGHEST)
    gam_tot = jnp.sum(ld_col, axis=0, keepdims=True)
    eye = (ri == ci).astype(F32)

    H = range(DN_HEADS)
    lanes = [slice(h * DN_DIM, (h + 1) * DN_DIM) for h in H]
    gc = [gam_col[:, h:h + 1] for h in H]
    bc = [beta_col[:, h:h + 1] for h in H]
    kh = [k_all[:, lanes[h]] for h in H]
    kh = [kh[h] * lax.rsqrt(jnp.sum(kh[h] * kh[h], axis=-1, keepdims=True) + EPS) for h in H]
    kb = [kh[h].astype(BF16) for h in H]
    decay = [jnp.exp(jnp.where(incl, gc[h] - gam_row[h:h + 1, :], NEG_BIG)) for h in H]
    kk = [_dot_nt(kb[h], kb[h]) for h in H]
    x = [-(jnp.where(strict, decay[h], 0.0) * bc[h] * kk[h]) for h in H]
    p = [eye + x[h] for h in H]
    xb = [x[h].astype(BF16) for h in H]
    x = [_dot(xb[h], xb[h]) for h in H]
    n_fac = int(math.log2(C)) - 1
    for j in range(n_fac):
        xb = [x[h].astype(BF16) for h in H]
        if j < n_fac - 1:
            r = [_dot(xb[h], jnp.concatenate([xb[h], p[h].astype(BF16)], axis=1)) for h in H]
            x = [r[h][:, :C] for h in H]
            p = [p[h] + r[h][:, C:] for h in H]
        else:
            p = [p[h] + _dot(xb[h], p[h].astype(BF16)) for h in H]
    rhs = [jnp.concatenate([kh[h] * (bc[h] * jnp.exp(gc[h])), v_all[:, lanes[h]] * bc[h]],
                           axis=1).astype(BF16) for h in H]
    sol = [_dot(p[h].astype(BF16), rhs[h]) for h in H]
    w = [sol[h][:, :DN_DIM] for h in H]
    u0 = [sol[h][:, DN_DIM:] for h in H]
    k_end = [(kh[h] * jnp.exp(gam_tot[:, h:h + 1] - gc[h])).astype(BF16) for h in H]
    s = [s_refs[h][...] for h in H]
    sb = [s[h].astype(BF16) for h in H]
    if with_q:
        qh = [q_all[:, lanes[h]] for h in H]
        qh = [qh[h] * (lax.rsqrt(jnp.sum(qh[h] * qh[h], axis=-1, keepdims=True) + EPS) * DN_DIM ** -0.5)
              for h in H]
        qk = [(_dot_nt(qh[h].astype(BF16), kb[h]) * decay[h]).astype(BF16) for h in H]
        wq = [jnp.concatenate([w[h], qh[h] * jnp.exp(gc[h])], axis=0).astype(BF16) for h in H]
        ws = [_dot(wq[h], sb[h]) for h in H]
        ub = [(u0[h] - ws[h][:C]).astype(BF16) for h in H]
        qu = [_dot(qk[h], ub[h]) for h in H]
        for h in H:
            o_ref[:, lanes[h]] = (ws[h][C:] + qu[h]).astype(o_ref.dtype)
    else:
        ws = [_dot(w[h].astype(BF16), sb[h]) for h in H]
        ub = [(u0[h] - ws[h]).astype(BF16) for h in H]
    ku = [_dot_tn(k_end[h], ub[h]) for h in H]
    for h in H:
        s_refs[h][...] = jnp.exp(gam_tot[:, h:h + 1]) * s[h] + ku[h]

    @pl.when(c == n_chunks - 1)
    def _():
        for h in range(DN_HEADS):
            sfin_ref[h] = s_refs[h][...]


def _deltanet(main3, gate_col, gate_row, conv_w, alog, dtb, s0, with_q):
    B, T, _ = main3.shape
    C = DN_CHUNK
    n_chunks = T // C
    hpc = C // DN_HALO
    n_halo = T // DN_HALO

    def cix(d, c):
        return jnp.where(d == 0, c, n_chunks - 1 - c)

    def trio(col):
        return [pl.BlockSpec((None, DN_HALO, D_MODEL),
                             lambda d, b, c: (b, jnp.maximum(cix(d, c) * hpc - 1, 0), col)),
                pl.BlockSpec((None, C, D_MODEL), lambda d, b, c: (b, cix(d, c), col)),
                pl.BlockSpec((None, DN_HALO, D_MODEL),
                             lambda d, b, c: (b, jnp.minimum((cix(d, c) + 1) * hpc, n_halo - 1), col))]

    in_specs = (trio(COL_DN_Q) if with_q else []) + trio(COL_DN_K) + trio(COL_DN_V) + [
        pl.BlockSpec((None, None, C, 2 * DN_HEADS), lambda d, b, c: (d, b, cix(d, c), 0)),
        pl.BlockSpec((None, None, None, 2 * DN_HEADS, C), lambda d, b, c: (d, b, cix(d, c), 0, 0)),
        pl.BlockSpec((DN_CONV, 3 * D_MODEL), lambda d, b, c: (0, 0)),
        pl.BlockSpec((None, 1, DN_HEADS), lambda d, b, c: (d, 0, 0)),
        pl.BlockSpec((None, DN_HEADS, 1), lambda d, b, c: (d, 0, 0)),
        pl.BlockSpec((None, 1, DN_HEADS), lambda d, b, c: (d, 0, 0)),
        pl.BlockSpec((None, DN_HEADS, 1), lambda d, b, c: (d, 0, 0)),
        pl.BlockSpec((None, None, DN_HEADS, DN_DIM, DN_DIM), lambda d, b, c: (d, b, 0, 0, 0)),
    ]
    s_spec = pl.BlockSpec((None, None, DN_HEADS, DN_DIM, DN_DIM), lambda d, b, c: (d, b, 0, 0, 0))
    s_shape = jax.ShapeDtypeStruct((2, B, DN_HEADS, DN_DIM, DN_DIM), F32)
    if with_q:
        out_specs = [pl.BlockSpec((None, None, C, D_MODEL), lambda d, b, c: (d, b, cix(d, c), 0)), s_spec]
        out_shape = [jax.ShapeDtypeStruct((2, B, T, D_MODEL), BF16), s_shape]
    else:
        out_specs = [s_spec]
        out_shape = [s_shape]
    n_main = 3 if with_q else 2
    args = [main3] * (3 * n_main) + [
        gate_col, gate_row, conv_w,
        alog.reshape(2, 1, DN_HEADS), alog.reshape(2, DN_HEADS, 1),
        dtb.reshape(2, 1, DN_HEADS), dtb.reshape(2, DN_HEADS, 1), s0]
    out = pl.pallas_call(
        functools.partial(_dn_kernel, with_q=with_q, n_chunks=n_chunks),
        grid=(2, B, n_chunks),
        in_specs=in_specs, out_specs=out_specs, out_shape=out_shape,
        scratch_shapes=[pltpu.VMEM((C + 2 * DN_HALO, D_MODEL), F32)] * 3
        + [pltpu.VMEM((DN_DIM, DN_DIM), F32)] * DN_HEADS,
        compiler_params=_cparams(("arbitrary", "arbitrary", "arbitrary")),
        name="deltanet_q" if with_q else "deltanet_state",
    )(*args)
    return (out[0], out[1]) if with_q else (None, out[0])


def _rope_tables(S):
    half = AT_DIM // 2
    nf = half // 2
    inv_freq = ROPE_BASE ** (-jnp.arange(nf, dtype=F32) / nf)
    t = jnp.arange(S, dtype=jnp.int32)
    row = (t // GRID_W).astype(F32)
    col = (t % GRID_W).astype(F32)
    lane = jnp.arange(LANES)
    dd = lane % AT_DIM
    pos = jnp.where((dd < half)[None, :], row[:, None], col[:, None])
    ang = pos * inv_freq[lane % nf][None, :]
    first = ((lane % half) < nf)[None, :]
    sin = jnp.sin(ang)
    return jnp.cos(ang), jnp.where(first, -sin, 0.0), jnp.where(first, 0.0, sin)


def _rope_kernel(q_ref, k_ref, v_ref, cos_ref, sa_ref, sb_ref, qo_ref, ko_ref, vo_ref):
    cos, sa, sb = cos_ref[...], sa_ref[...], sb_ref[...]
    nf = AT_DIM // 4

    def rot(x):
        return x * cos + pltpu.roll(x, LANES - nf, 1) * sa + pltpu.roll(x, nf, 1) * sb

    for j in range(AT_Q_HEADS * AT_DIM // LANES):
        lanes = slice(j * LANES, (j + 1) * LANES)
        qo_ref[:, lanes] = (rot(q_ref[:, lanes].astype(F32)) * AT_DIM ** -0.5).astype(BF16)
    ko_ref[...] = rot(k_ref[...]).astype(BF16)
    vo_ref[...] = v_ref[...].astype(BF16)


def _rope(main, small, tables, S, tm=512):
    n_tok = main.shape[0]
    per_seq = S // tm
    tab_spec = pl.BlockSpec((tm, LANES), lambda i: (i % per_seq, 0))
    return pl.pallas_call(
        _rope_kernel,
        grid=(n_tok // tm,),
        in_specs=[pl.BlockSpec((tm, D_MODEL), lambda i: (i, COL_AT_Q)),
                  pl.BlockSpec((tm, LANES), lambda i: (i, 0)),
                  pl.BlockSpec((tm, LANES), lambda i: (i, 1)),
                  tab_spec, tab_spec, tab_spec],
        out_specs=[pl.BlockSpec((tm, D_MODEL), lambda i: (i, 0)),
                   pl.BlockSpec((tm, LANES), lambda i: (i, 0)),
                   pl.BlockSpec((tm, LANES), lambda i: (i, 0))],
        out_shape=[jax.ShapeDtypeStruct((n_tok, D_MODEL), BF16),
                   jax.ShapeDtypeStruct((n_tok, LANES), BF16),
                   jax.ShapeDtypeStruct((n_tok, LANES), BF16)],
        compiler_params=_cparams(("arbitrary",)),
        name="rope",
    )(main, small, small, *tables)


def _attn_kernel(*refs, local, n_blocks, q_scale):
    if local:
        (q_ref, kp_ref, kc_ref, kn_ref, vp_ref, vc_ref, vn_ref, kx_ref, vx_ref, sink_ref, o_ref) = refs
    else:
        (q_ref, kx_ref, vx_ref, sink_ref, o_ref) = refs
    P = AT_BLOCK
    G = AT_Q_HEADS // AT_KV_HEADS
    L = kx_ref.shape[0]
    kx = kx_ref[...].astype(BF16)
    vx = vx_ref[...].astype(BF16)
    if local:
        i = pl.program_id(1)
        k_all = jnp.concatenate([kp_ref[...], kc_ref[...], kn_ref[...], kx], axis=0)
        v_all = jnp.concatenate([vp_ref[...], vc_ref[...], vn_ref[...], vx], axis=0)
        qi = lax.broadcasted_iota(I32, (P, P), 0)
        kj = lax.broadcasted_iota(I32, (P, P), 1)
        b_prev = jnp.where(kj >= qi, 0.0, NEG_BIG) + jnp.where(i > 0, 0.0, NEG_BIG)
        b_next = jnp.where(kj <= qi, 0.0, NEG_BIG) + jnp.where(i < n_blocks - 1, 0.0, NEG_BIG)
        bias = jnp.concatenate([b_prev, jnp.zeros((P, P), F32), b_next, jnp.zeros((P, L), F32)], axis=1)
    else:
        k_all, v_all, bias = kx, vx, None
    n_keys = k_all.shape[0]
    lo = lax.broadcasted_iota(I32, (P, LANES), 1) < AT_DIM
    qf = q_ref[...].astype(F32) * q_scale
    pieces = []
    for qh in range(AT_Q_HEADS):
        blk = qf[:, (qh // 2) * LANES:(qh // 2 + 1) * LANES]
        want_lo = qh // G == 0
        if want_lo != (qh % 2 == 0):
            blk = pltpu.roll(blk, AT_DIM, 1)
        pieces.append(jnp.where(lo if want_lo else ~lo, blk, 0.0).astype(BF16))
    qs = jnp.concatenate(pieces, axis=0)
    s = _dot_nt(qs, k_all)
    if bias is not None:
        s = (s.reshape(AT_Q_HEADS, P, n_keys) + bias[None]).reshape(AT_Q_HEADS * P, n_keys)
    sink = sink_ref[...]
    m = jnp.maximum(jnp.max(s, axis=-1, keepdims=True), sink)
    p = jnp.exp(s - m)
    den = jnp.sum(p, axis=-1, keepdims=True) + jnp.exp(sink - m)
    o = _dot(p.astype(BF16), v_all) / den
    for j in range(AT_Q_HEADS // 2):
        a = o[(2 * j) * P:(2 * j + 1) * P]
        b = o[(2 * j + 1) * P:(2 * j + 2) * P]
        if (2 * j) // G == 0:
            out = jnp.where(lo, a, pltpu.roll(b, AT_DIM, 1))
        else:
            out = jnp.where(lo, pltpu.roll(a, AT_DIM, 1), b)
        o_ref[:, j * LANES:(j + 1) * LANES] = out.astype(BF16)


def _attention_local(q_r, k_r, v_r, small_c, sinks, B, S, L):
    P = AT_BLOCK
    nb = S // P

    def kv_trio():
        return [pl.BlockSpec((P, LANES), lambda b, i: (b * nb + jnp.maximum(i - 1, 0), 0)),
                pl.BlockSpec((P, LANES), lambda b, i: (b * nb + i, 0)),
                pl.BlockSpec((P, LANES), lambda b, i: (b * nb + jnp.minimum(i + 1, nb - 1), 0))]

    return pl.pallas_call(
        functools.partial(_attn_kernel, local=True, n_blocks=nb, q_scale=1.0),
        grid=(B, nb),
        in_specs=[pl.BlockSpec((P, D_MODEL), lambda b, i: (b * nb + i, 0))] + kv_trio() + kv_trio() + [
            pl.BlockSpec((L, LANES), lambda b, i: (b, 0)),
            pl.BlockSpec((L, LANES), lambda b, i: (b, 1)),
            pl.BlockSpec((AT_Q_HEADS * AT_BLOCK, 1), lambda b, i: (0, 0))],
        out_specs=pl.BlockSpec((P, D_MODEL), lambda b, i: (b * nb + i, 0)),
        out_shape=jax.ShapeDtypeStruct((B * S, D_MODEL), BF16),
        compiler_params=_cparams(("arbitrary", "arbitrary")),
        name="attn_local",
    )(q_r, k_r, k_r, k_r, v_r, v_r, v_r, small_c, small_c, sinks)


def _attention_ctx(main_c, small_c, sinks, B, L):
    P = AT_BLOCK
    nb = L // P
    return pl.pallas_call(
        functools.partial(_attn_kernel, local=False, n_blocks=nb, q_scale=AT_DIM ** -0.5),
        grid=(B, nb),
        in_specs=[pl.BlockSpec((P, D_MODEL), lambda b, i: (b * nb + i, COL_AT_Q)),
                  pl.BlockSpec((L, LANES), lambda b, i: (b, 0)),
                  pl.BlockSpec((L, LANES), lambda b, i: (b, 1)),
                  pl.BlockSpec((AT_Q_HEADS * AT_BLOCK, 1), lambda b, i: (0, 0))],
        out_specs=pl.BlockSpec((P, D_MODEL), lambda b, i: (b * nb + i, 0)),
        out_shape=jax.ShapeDtypeStruct((B * L, D_MODEL), BF16),
        compiler_params=_cparams(("arbitrary", "arbitrary")),
        name="attn_ctx",
    )(main_c, small_c, small_c, sinks)


def _merge_kernel(ysg_ref, odn_ref, dng_ref, yat_ref, g0_ref, g1_ref, g2_ref, x_ref, mod_ref,
                  dn_norm_ref, post_ref, pre_ref, wsg_ref, wdn_ref, wat_ref, wout_ref, rw_ref, rb_ref,
                  xo_ref, h2_ref, lg_ref):
    o = odn_ref[0].astype(F32) + odn_ref[1].astype(F32)
    dn_g = dn_norm_ref[...]
    parts = []
    for h in range(DN_HEADS):
        lanes = slice(h * DN_DIM, (h + 1) * DN_DIM)
        parts.append(_rms(o[:, lanes], dn_g) * _silu(dng_ref[:, lanes].astype(F32)))
    ydn = jnp.concatenate(parts, axis=1).astype(BF16)
    m = (_sigmoid(g0_ref[...].astype(F32)) * _dot(ysg_ref[...], wsg_ref[...])
         + _sigmoid(g1_ref[...].astype(F32)) * _dot(ydn, wdn_ref[...])
         + _sigmoid(g2_ref[...].astype(F32)) * _dot(yat_ref[...], wat_ref[...]))
    y = _dot(m.astype(BF16), wout_ref[...])
    gate1 = mod_ref[:, 2 * D_MODEL:3 * D_MODEL]
    sh2 = mod_ref[:, 3 * D_MODEL:4 * D_MODEL]
    sc2 = mod_ref[:, 4 * D_MODEL:5 * D_MODEL]
    xn = x_ref[...] + gate1 * _rms(y, post_ref[...])
    xo_ref[...] = xn
    h2 = _rms(xn, pre_ref[...]) * (1.0 + sc2) + sh2
    h2_ref[...] = h2
    lg_ref[...] = _dot(h2.astype(BF16), rw_ref[...]) + rb_ref[...]


def _merge(ysg, odn, main, yat, x, mod, mod_row, lw, tm=256):
    n_tok = x.shape[0]
    const = lambda i: (0, 0)
    wspec = pl.BlockSpec((D_MODEL, D_MODEL), const, pipeline_mode=pl.Buffered(1))
    vspec = pl.BlockSpec((1, D_MODEL), const)
    return pl.pallas_call(
        _merge_kernel,
        grid=(n_tok // tm,),
        in_specs=[pl.BlockSpec((tm, D_MODEL), lambda i: (i, 0)),
                  pl.BlockSpec((2, tm, D_MODEL), lambda i: (0, i, 0)),
                  pl.BlockSpec((tm, D_MODEL), lambda i: (i, COL_DN_G)),
                  pl.BlockSpec((tm, D_MODEL), lambda i: (i, 0)),
                  pl.BlockSpec((tm, D_MODEL), lambda i: (i, COL_GATE0)),
                  pl.BlockSpec((tm, D_MODEL), lambda i: (i, COL_GATE0 + 1)),
                  pl.BlockSpec((tm, D_MODEL), lambda i: (i, COL_GATE0 + 2)),
                  pl.BlockSpec((tm, D_MODEL), lambda i: (i, 0)),
                  pl.BlockSpec((None, 1, 6 * D_MODEL), lambda i: (mod_row(i * tm), 0, 0)),
                  pl.BlockSpec((1, DN_DIM), const), vspec, vspec,
                  wspec, wspec, wspec, wspec,
                  pl.BlockSpec((D_MODEL, LANES), const), pl.BlockSpec((1, LANES), const)],
        out_specs=[pl.BlockSpec((tm, D_MODEL), lambda i: (i, 0)),
                   pl.BlockSpec((tm, D_MODEL), lambda i: (i, 0)),
                   pl.BlockSpec((tm, LANES), lambda i: (i, 0))],
        out_shape=[jax.ShapeDtypeStruct((n_tok, D_MODEL), F32),
                   jax.ShapeDtypeStruct((n_tok, D_MODEL), F32),
                   jax.ShapeDtypeStruct((n_tok, LANES), F32)],
        compiler_params=_cparams(("arbitrary",)),
        name="merge",
    )(ysg, odn, main, yat, main, main, main, x, mod,
      lw["dn_norm_g"], lw["norm_post_mix"], lw["norm_pre_ffn"],
      lw["w_proj_sg"], lw["w_proj_dn"], lw["w_proj_at"], lw["w_out"], lw["router_w"], lw["router_b"])


def _route_kernel(lg_ref, idx_ref, gate_ref, rank_ref, cnt_ref, run_ref):
    tm = lg_ref.shape[0]

    @pl.when(pl.program_id(0) == 0)
    def _():
        run_ref[...] = jnp.zeros_like(run_ref)

    l = lg_ref[...]
    lane = lax.broadcasted_iota(I32, l.shape, 1).astype(F32)
    vals, onehots = [], []
    for k in range(TOP_K):
        m = jnp.max(l, axis=-1, keepdims=True)
        ik = jnp.min(jnp.where(l == m, lane, float(LANES)), axis=-1, keepdims=True)
        oh = lane == ik
        idx_ref[:, k:k + 1] = ik.astype(I32)
        vals.append(m)
        onehots.append(oh)
        l = jnp.where(oh, -jnp.inf, l)
    es = [jnp.exp(v - vals[0]) for v in vals]
    den = es[0] + es[1] + es[2] + es[3]
    sel = jnp.zeros(l.shape, F32)
    for k in range(TOP_K):
        gate_ref[:, k:k + 1] = es[k] / den
        sel = sel + onehots[k].astype(F32)
    ri = lax.broadcasted_iota(I32, (tm, tm), 0)
    ci = lax.broadcasted_iota(I32, (tm, tm), 1)
    before = _dot((ri > ci).astype(BF16), sel.astype(BF16)) + run_ref[...]
    for k in range(TOP_K):
        rank_ref[:, k:k + 1] = jnp.sum(jnp.where(onehots[k], before, 0.0), axis=-1,
                                       keepdims=True).astype(I32)
    run_ref[...] = run_ref[...] + jnp.sum(sel, axis=0, keepdims=True)
    cnt_ref[...] = run_ref[...]


def _route(logits, tm=256):
    n_tok = logits.shape[0]
    small = lambda dt: jax.ShapeDtypeStruct((n_tok, TOP_K), dt)
    kspec = pl.BlockSpec((tm, TOP_K), lambda i: (i, 0))
    return pl.pallas_call(
        _route_kernel,
        grid=(n_tok // tm,),
        in_specs=[pl.BlockSpec((tm, LANES), lambda i: (i, 0))],
        out_specs=[kspec, kspec, kspec, pl.BlockSpec((1, LANES), lambda i: (0, 0))],
        out_shape=[small(I32), small(F32), small(I32), jax.ShapeDtypeStruct((1, LANES), F32)],
        scratch_shapes=[pltpu.VMEM((1, LANES), F32)],
        compiler_params=_cparams(("arbitrary",)),
        name="route",
    )(logits)


def _dispatch_kernel(dest_ref, h_ref, xs_in_ref, xs_ref, sem):
    del xs_in_ref
    tm = h_ref.shape[0]

    def row_copy(r, k):
        return pltpu.make_async_copy(h_ref.at[pl.ds(r, 1)],
                                     xs_ref.at[pl.ds(dest_ref[r * TOP_K + k], 1)], sem)

    def issue(r, carry):
        for k in range(TOP_K):
            row_copy(r, k).start()
        return carry

    lax.fori_loop(0, tm, issue, 0)

    def drain(r, carry):
        for k in range(TOP_K):
            row_copy(r, k).wait()
        return carry

    lax.fori_loop(0, tm, drain, 0)


def _dispatch(dest_flat, h2, n_slots, tm=256):
    n_tok = h2.shape[0]
    xs0 = jnp.zeros((n_slots, D_MODEL), F32)
    return pl.pallas_call(
        _dispatch_kernel,
        grid=(n_tok // tm,),
        in_specs=[pl.BlockSpec((tm * TOP_K,), lambda i: (i,), memory_space=pltpu.SMEM),
                  pl.BlockSpec((tm, D_MODEL), lambda i: (i, 0)),
                  pl.BlockSpec(memory_space=pl.ANY)],
        out_specs=pl.BlockSpec(memory_space=pl.ANY),
        out_shape=jax.ShapeDtypeStruct((n_slots, D_MODEL), F32),
        scratch_shapes=[pltpu.SemaphoreType.DMA],
        input_output_aliases={2: 0},
        compiler_params=_cparams(("arbitrary",)),
        name="moe_dispatch",
    )(dest_flat, h2, xs0)


def _expert_kernel(te_ref, nu_ref, xs_ref, wgu_ref, bgu_ref, wd_ref, bd_ref, y_ref):
    del te_ref

    @pl.when(pl.program_id(0) < nu_ref[0])
    def _():
        gu = _dot(xs_ref[...].astype(BF16), wgu_ref[...]) + bgu_ref[...]
        g = jnp.minimum(gu[:, :D_EXPERT], SWIGLU_LIMIT)
        lin = jnp.clip(gu[:, D_EXPERT:], -SWIGLU_LIMIT, SWIGLU_LIMIT)
        act = g * _sigmoid(SWIGLU_ALPHA * g) * (lin + 1.0)
        y_ref[...] = _dot(act.astype(BF16), wd_ref[...]) + bd_ref[...]

    @pl.when(pl.program_id(0) >= nu_ref[0])
    def _():
        y_ref[...] = jnp.zeros_like(y_ref)


def _experts(tile_expert, n_used, xs, wgu, bgu, wd, bd, tm):
    n_slots = xs.shape[0]
    n_tiles = n_slots // tm

    def row(i, te, nu):
        return (jnp.minimum(i, nu[0] - 1), 0)

    grid_spec = pltpu.PrefetchScalarGridSpec(
        num_scalar_prefetch=2,
        grid=(n_tiles,),
        in_specs=[pl.BlockSpec((tm, D_MODEL), row),
                  pl.BlockSpec((None, D_MODEL, 2 * D_EXPERT), lambda i, te, nu: (te[i], 0, 0)),
                  pl.BlockSpec((None, 1, 2 * D_EXPERT), lambda i, te, nu: (te[i], 0, 0)),
                  pl.BlockSpec((None, D_EXPERT, D_MODEL), lambda i, te, nu: (te[i], 0, 0)),
                  pl.BlockSpec((None, 1, D_MODEL), lambda i, te, nu: (te[i], 0, 0))],
        out_specs=pl.BlockSpec((tm, D_MODEL), lambda i, te, nu: (i, 0)),
    )
    return pl.pallas_call(
        _expert_kernel,
        grid_spec=grid_spec,
        out_shape=jax.ShapeDtypeStruct((n_slots, D_MODEL), F32),
        compiler_params=_cparams(("arbitrary",)),
        name="moe_experts",
    )(tile_expert, n_used, xs, wgu, bgu, wd, bd)


def _combine_kernel(dest_ref, gate_ref, x_ref, mod_ref, post_ref, y_ref, xo_ref, buf_ref, sem):
    tm = x_ref.shape[0]

    def row_copy(r, k):
        return pltpu.make_async_copy(y_ref.at[pl.ds(dest_ref[r * TOP_K + k], 1)],
                                     buf_ref.at[k, pl.ds(r, 1)], sem)

    def issue(r, carry):
        for k in range(TOP_K):
            row_copy(r, k).start()
        return carry

    lax.fori_loop(0, tm, issue, 0)

    def drain(r, carry):
        for k in range(TOP_K):
            row_copy(r, k).wait()
        return carry

    lax.fori_loop(0, tm, drain, 0)

    y = None
    for k in range(TOP_K):
        t = buf_ref[k] * gate_ref[:, k:k + 1]
        y = t if y is None else y + t
    gate2 = mod_ref[:, 5 * D_MODEL:6 * D_MODEL]
    xo_ref[...] = x_ref[...] + gate2 * _rms(y, post_ref[...])


def _combine(dest_flat, gate, x_mid, mod, mod_row, post_g, y, tm=256):
    n_tok = x_mid.shape[0]
    return pl.pallas_call(
        _combine_kernel,
        grid=(n_tok // tm,),
        in_specs=[pl.BlockSpec((tm * TOP_K,), lambda i: (i,), memory_space=pltpu.SMEM),
                  pl.BlockSpec((tm, TOP_K), lambda i: (i, 0)),
                  pl.BlockSpec((tm, D_MODEL), lambda i: (i, 0)),
                  pl.BlockSpec((None, 1, 6 * D_MODEL), lambda i: (mod_row(i * tm), 0, 0)),
                  pl.BlockSpec((1, D_MODEL), lambda i: (0, 0)),
                  pl.BlockSpec(memory_space=pl.ANY)],
        out_specs=pl.BlockSpec((tm, D_MODEL), lambda i: (i, 0)),
        out_shape=jax.ShapeDtypeStruct((n_tok, D_MODEL), F32),
        scratch_shapes=[pltpu.VMEM((TOP_K, tm, D_MODEL), F32), pltpu.SemaphoreType.DMA],
        compiler_params=_cparams(("arbitrary",)),
        name="moe_combine",
    )(dest_flat, gate, x_mid, mod, post_g, y)


def _moe(h2, logits, x_mid, mod, mod_row, lw, tm_e=512):
    n_tok = h2.shape[0]
    idx, gate, rank, cnt = _route(logits)
    counts = cnt[0, :N_EXPERTS].astype(I32)
    padded = (counts + tm_e - 1) // tm_e * tm_e
    pad_end = jnp.cumsum(padded)
    offs = pad_end - padded
    dest = (jnp.take(offs, idx, axis=0) + rank).reshape(-1).astype(I32)
    n_tiles = (n_tok * TOP_K + tm_e - 1) // tm_e + N_EXPERTS
    tile_start = jnp.arange(n_tiles, dtype=I32) * tm_e
    tile_expert = jnp.minimum(jnp.sum(pad_end[None, :] <= tile_start[:, None], axis=1),
                              N_EXPERTS - 1).astype(I32)
    n_used = (pad_end[-1:] // tm_e).astype(I32)
    xs = _dispatch(dest, h2, n_tiles * tm_e)
    y = _experts(tile_expert, n_used, xs, lw["exp_w_gu"], lw["exp_b_gu"], lw["exp_w_down"],
                 lw["exp_b_down"], tm_e)
    return _combine(dest, gate, x_mid, mod, mod_row, lw["norm_post_ffn"], y)


def _split_w_in(w_in):
    offs, o = {}, 0
    for name, width in (("dn_k", 1024), ("dn_v", 1024), ("dn_a", 16), ("dn_b", 16), ("at_k", 128),
                        ("at_v", 128), ("dn_q", 1024), ("dn_g", 1024), ("at_q", 1024),
                        ("sg_u", 1024), ("sg_v", 1024), ("gates", 3072)):
        offs[name] = (o, o + width)
        o += width
    sl = lambda n: w_in[:, offs[n][0]:offs[n][1]]
    w_main = jnp.concatenate([sl(n) for n in ("dn_k", "dn_v", "dn_q", "dn_g", "at_q", "sg_u", "sg_v",
                                              "gates")], axis=1).astype(BF16)
    pad = jnp.zeros((w_in.shape[0], N_SMALL_COLS - 2 * LANES - 4 * DN_HEADS), w_in.dtype)
    w_small = jnp.concatenate([sl("at_k"), sl("at_v"), sl("dn_a"), sl("dn_b"), pad], axis=1).astype(BF16)
    return w_main, w_small


def _dn_gates(small, B, T):
    ab = small[:, 2 * LANES:2 * LANES + 4 * DN_HEADS].reshape(B, T, 2, 2, DN_HEADS)
    col = jnp.transpose(ab, (3, 0, 1, 2, 4)).reshape(2, B, T, 2 * DN_HEADS)
    row = jnp.transpose(col.reshape(2, B, T // DN_CHUNK, DN_CHUNK, 2 * DN_HEADS), (0, 1, 2, 4, 3))
    return col, row


def kernel(x, c, ctx, c_ctx, w_mod, b_mod, norm_pre_mix, norm_post_mix, norm_pre_ffn, norm_post_ffn, w_in, sg_ln_g, sg_ln_b, sg_w, sg_b, dn_conv_w, dn_a_log, dn_dt_bias, dn_norm_g, at_sinks, w_proj_sg, w_proj_dn, w_proj_at, w_out, router_w, router_b, exp_w_gu, exp_b_gu, exp_w_down, exp_b_down):
    B, S, D = x.shape
    L = ctx.shape[1]
    depth = w_mod.shape[0]
    assert D == D_MODEL and S % GRID_W == 0
    n_lat, n_ctx = B * S, B * L

    rows = (B + 1 + 7) // 8 * 8
    cvec = jnp.zeros((rows, D), F32).at[:B].set(c).at[B].set(c_ctx)
    mod_all = _modulation(cvec, w_mod, b_mod)
    tables = _rope_tables(S)

    lat_row = lambda t: t // S
    ctx_row = lambda t: B
    all_row = lambda t: jnp.where(t < n_lat, t // S, B)

    xl = x.reshape(n_lat, D)
    xc = ctx.reshape(n_ctx, D)
    for l in range(depth):
        need_ctx_out = l < depth - 1
        mod = mod_all[l].reshape(rows, 1, 6 * D)
        w_main, w_small = _split_w_in(w_in[l])
        lw = {
            "dn_norm_g": dn_norm_g[l].reshape(1, -1),
            "norm_post_mix": norm_post_mix[l].reshape(1, -1),
            "norm_pre_ffn": norm_pre_ffn[l].reshape(1, -1),
            "norm_post_ffn": norm_post_ffn[l].reshape(1, -1),
            "w_proj_sg": w_proj_sg[l].astype(BF16), "w_proj_dn": w_proj_dn[l].astype(BF16),
            "w_proj_at": w_proj_at[l].astype(BF16), "w_out": w_out[l].astype(BF16),
            "router_w": jnp.pad(router_w[l], ((0, 0), (0, LANES - N_EXPERTS))).astype(BF16),
            "router_b": jnp.pad(router_b[l], (0, LANES - N_EXPERTS),
                                constant_values=NEG_BIG).reshape(1, -1),
            "exp_w_gu": exp_w_gu[l].astype(BF16), "exp_b_gu": exp_b_gu[l].reshape(N_EXPERTS, 1, -1),
            "exp_w_down": exp_w_down[l].astype(BF16), "exp_b_down": exp_b_down[l].reshape(N_EXPERTS, 1, -1),
        }
        pre_g = norm_pre_mix[l].reshape(1, -1)
        main, small = _inproj(xl, mod, lat_row, pre_g, w_main, w_small, min(1024, S))
        w_main_c = w_main if need_ctx_out else w_main[:, :N_CTX_MAIN_COLS]
        main_c, small_c = _inproj(xc, mod, ctx_row, pre_g, w_main_c, w_small, min(1024, n_ctx))

        sg_args = (sg_ln_g[l].reshape(1, -1), sg_ln_b[l].reshape(1, -1), sg_w[l].astype(BF16),
                   sg_b[l].T)
        ysg = _sgu(main, *sg_args)

        gcol_c, grow_c = _dn_gates(small_c, B, L)
        gcol, grow = _dn_gates(small, B, S)
        s0 = jnp.zeros((2, B, DN_HEADS, DN_DIM, DN_DIM), F32)
        odn_c, s_ctx = _deltanet(main_c.reshape(B, L, -1), gcol_c, grow_c, dn_conv_w[l], dn_a_log[l],
                                 dn_dt_bias[l], s0, need_ctx_out)
        odn, _ = _deltanet(main.reshape(B, S, -1), gcol, grow, dn_conv_w[l], dn_a_log[l],
                           dn_dt_bias[l], s_ctx, True)

        sinks = jnp.repeat(at_sinks[l], AT_BLOCK).reshape(-1, 1)
        q_r, k_r, v_r = _rope(main, small, tables, S)
        yat = _attention_local(q_r, k_r, v_r, small_c, sinks, B, S, L)

        x_mid, h2, logits = _merge(ysg, odn.reshape(2, n_lat, D), main, yat, xl, mod, lat_row, lw)
        if need_ctx_out:
            ysg_c = _sgu(main_c, *sg_args)
            yat_c = _attention_ctx(main_c, small_c, sinks, B, L)
            xc_mid, h2c, logits_c = _merge(ysg_c, odn_c.reshape(2, n_ctx, D), main_c, yat_c, xc, mod,
                                           ctx_row, lw)
            x_mid = jnp.concatenate([x_mid, xc_mid], axis=0)
            h2 = jnp.concatenate([h2, h2c], axis=0)
            logits = jnp.concatenate([logits, logits_c], axis=0)
            xo = _moe(h2, logits, x_mid, mod, all_row, lw)
            xl, xc = xo[:n_lat], xo[n_lat:]
        else:
            xl = _moe(h2, logits, x_mid, mod, lat_row, lw)
    return xl.reshape(B, S, D)
```

```python
import functools
import math

import jax
import jax.numpy as jnp
from jax import lax
from jax.experimental import pallas as pl
from jax.experimental.pallas import tpu as pltpu

F32 = jnp.float32
BF16 = jnp.bfloat16
I32 = jnp.int32

EPS = 1e-6
D_MODEL = 1024
GRID_W = 64

SG_CHUNK = 128
SG_GROUPS = 8

DN_HEADS = 8
DN_DIM = 128
DN_CONV = 5
DN_CHUNK = 64
DN_HALO = 16

AT_Q_HEADS = 16
AT_KV_HEADS = 2
AT_DIM = 64
AT_BLOCK = 128
ROPE_BASE = 10000.0

N_EXPERTS = 32
TOP_K = 4
D_EXPERT = 1024
SWIGLU_ALPHA = 1.702
SWIGLU_LIMIT = 7.0
N_BRANCH = 3

LANES = 128
NEG_BIG = -1e30

COL_DN_K, COL_DN_V, COL_DN_Q, COL_DN_G, COL_AT_Q, COL_SG_U, COL_SG_V, COL_GATE0 = range(8)
N_MAIN_COLS = 10 * D_MODEL
N_CTX_MAIN_COLS = 2 * D_MODEL
N_SMALL_COLS = 3 * LANES

VMEM_LIMIT = 52 * 1024 * 1024


def _cparams(sem):
    return pltpu.CompilerParams(dimension_semantics=sem, vmem_limit_bytes=VMEM_LIMIT)


def _dot(a, b):
    return jnp.dot(a, b, preferred_element_type=F32)


def _dot_nt(a, b):
    return lax.dot_general(a, b, (((1,), (1,)), ((), ())), preferred_element_type=F32)


def _dot_tn(a, b):
    return lax.dot_general(a, b, (((0,), (0,)), ((), ())), preferred_element_type=F32)


def _sigmoid(x):
    return 1.0 / (1.0 + jnp.exp(-x))


def _silu(x):
    return x * _sigmoid(x)


def _gelu_tanh(x):
    return 0.5 * x * (1.0 + jnp.tanh(math.sqrt(2.0 / math.pi) * (x + 0.044715 * (x * x * x))))


def _softplus(x):
    return jnp.maximum(x, 0.0) + jnp.log(1.0 + jnp.exp(-jnp.abs(x)))


def _rms(x, g):
    return x * lax.rsqrt(jnp.mean(x * x, axis=-1, keepdims=True) + EPS) * g


def _mod_kernel(c_ref, w_ref, b_ref, o_ref):
    s = _silu(c_ref[...])
    o_ref[...] = jnp.dot(s, w_ref[...], preferred_element_type=F32,
                         precision=lax.Precision.HIGHEST) + b_ref[...]


def _modulation(cvec, w_mod, b_mod):
    depth = w_mod.shape[0]
    rows = cvec.shape[0]
    n_col = w_mod.shape[2] // D_MODEL
    return pl.pallas_call(
        _mod_kernel,
        grid=(depth, n_col),
        in_specs=[pl.BlockSpec((rows, D_MODEL), lambda l, j: (0, 0)),
                  pl.BlockSpec((None, D_MODEL, D_MODEL), lambda l, j: (l, 0, j)),
                  pl.BlockSpec((None, 1, D_MODEL), lambda l, j: (l, 0, j))],
        out_specs=pl.BlockSpec((None, rows, D_MODEL), lambda l, j: (l, 0, j)),
        out_shape=jax.ShapeDtypeStruct((depth, rows, w_mod.shape[2]), F32),
        compiler_params=_cparams(("arbitrary", "arbitrary")),
        name="modulation",
    )(cvec, w_mod, b_mod.reshape(depth, 1, -1))


def _inproj_kernel(x_ref, mod_ref, g_ref, wm_ref, ws_ref, main_ref, small_ref, h_ref):
    @pl.when(pl.program_id(1) == 0)
    def _():
        sh = mod_ref[:, 0 * D_MODEL:1 * D_MODEL]
        sc = mod_ref[:, 1 * D_MODEL:2 * D_MODEL]
        h = (_rms(x_ref[...], g_ref[...]) * (1.0 + sc) + sh).astype(BF16)
        h_ref[...] = h
        small_ref[...] = _dot(h, ws_ref[...])

    main_ref[...] = _dot(h_ref[...], wm_ref[...]).astype(BF16)


def _inproj(x, mod, mod_row, norm_g, w_main, w_small, tm, tn=1024):
    n_tok = x.shape[0]
    n_main = w_main.shape[1]
    return pl.pallas_call(
        _inproj_kernel,
        grid=(n_tok // tm, n_main // tn),
        in_specs=[pl.BlockSpec((tm, D_MODEL), lambda i, j: (i, 0)),
                  pl.BlockSpec((None, 1, 6 * D_MODEL), lambda i, j: (mod_row(i * tm), 0, 0)),
                  pl.BlockSpec((1, D_MODEL), lambda i, j: (0, 0)),
                  pl.BlockSpec((D_MODEL, tn), lambda i, j: (0, j)),
                  pl.BlockSpec((D_MODEL, N_SMALL_COLS), lambda i, j: (0, 0))],
        out_specs=[pl.BlockSpec((tm, tn), lambda i, j: (i, j)),
                   pl.BlockSpec((tm, N_SMALL_COLS), lambda i, j: (i, 0))],
        out_shape=[jax.ShapeDtypeStruct((n_tok, n_main), BF16),
                   jax.ShapeDtypeStruct((n_tok, N_SMALL_COLS), F32)],
        scratch_shapes=[pltpu.VMEM((tm, D_MODEL), BF16)],
        compiler_params=_cparams(("arbitrary", "arbitrary")),
        name="inproj",
    )(x, mod, norm_g, w_main, w_small)


def _sgu_kernel(u_ref, v_ref, lng_ref, lnb_ref, ws_ref, bs_ref, o_ref, *, n_chunk):
    u = _gelu_tanh(u_ref[...].astype(F32))
    v = _gelu_tanh(v_ref[...].astype(F32))
    vc = v - jnp.mean(v, axis=-1, keepdims=True)
    var = jnp.mean(vc * vc, axis=-1, keepdims=True)
    vn = (vc * lax.rsqrt(var + EPS) * lng_ref[...] + lnb_ref[...]).astype(BF16)
    for n in range(n_chunk):
        rows = slice(n * SG_CHUNK, (n + 1) * SG_CHUNK)
        for g in range(SG_GROUPS):
            cols = slice(g * LANES, (g + 1) * LANES)
            mixed = _dot(ws_ref[g], vn[rows, cols]) + bs_ref[:, g:g + 1]
            o_ref[rows, cols] = (u[rows, cols] * mixed).astype(BF16)


def _sgu(main, sg_ln_g, sg_ln_b, sg_w, sg_bt, n_chunk=2):
    n_tok = main.shape[0]
    tc = n_chunk * SG_CHUNK
    return pl.pallas_call(
        functools.partial(_sgu_kernel, n_chunk=n_chunk),
        grid=(n_tok // tc,),
        in_specs=[pl.BlockSpec((tc, D_MODEL), lambda i: (i, COL_SG_U)),
                  pl.BlockSpec((tc, D_MODEL), lambda i: (i, COL_SG_V)),
                  pl.BlockSpec((1, D_MODEL), lambda i: (0, 0)),
                  pl.BlockSpec((1, D_MODEL), lambda i: (0, 0)),
                  pl.BlockSpec((SG_GROUPS, SG_CHUNK, SG_CHUNK), lambda i: (0, 0, 0)),
                  pl.BlockSpec((SG_CHUNK, SG_GROUPS), lambda i: (0, 0))],
        out_specs=pl.BlockSpec((tc, D_MODEL), lambda i: (i, 0)),
        out_shape=jax.ShapeDtypeStruct((n_tok, D_MODEL), BF16),
        compiler_params=_cparams(("arbitrary",)),
        name="sgu",
    )(main, main, sg_ln_g, sg_ln_b, sg_w, sg_bt)


def _dn_kernel(*refs, with_q, n_chunks):
    s_refs = refs[-DN_HEADS:]
    ext_refs = refs[-DN_HEADS - 3:-DN_HEADS]
    refs = refs[:-DN_HEADS - 3]
    if with_q:
        (qp_ref, qc_ref, qn_ref, kp_ref, kc_ref, kn_ref, vp_ref, vc_ref, vn_ref,
         gcol_ref, grow_ref, cw_ref, alog_r_ref, alog_c_ref, dtb_r_ref, dtb_c_ref, s0_ref,
         o_ref, sfin_ref) = refs
    else:
        (kp_ref, kc_ref, kn_ref, vp_ref, vc_ref, vn_ref,
         gcol_ref, grow_ref, cw_ref, alog_r_ref, alog_c_ref, dtb_r_ref, dtb_c_ref, s0_ref,
         sfin_ref) = refs
    d = pl.program_id(0)
    c = pl.program_id(2)
    is_fwd = d == 0
    cidx = jnp.where(is_fwd, c, n_chunks - 1 - c)
    C = DN_CHUNK

    @pl.when(c == 0)
    def _():
        for h in range(DN_HEADS):
            s_refs[h][...] = s0_ref[h]

    has_prev = (cidx > 0).astype(F32)
    has_next = (cidx < n_chunks - 1).astype(F32)

    def conv_silu(p_ref, c_ref, n_ref, part):
        ext_ref = ext_refs[part]
        ext_ref[0:DN_HALO, :] = p_ref[...].astype(F32) * has_prev
        ext_ref[DN_HALO:DN_HALO + C, :] = c_ref[...].astype(F32)
        ext_ref[DN_HALO + C:2 * DN_HALO + C, :] = n_ref[...].astype(F32) * has_next
        base = DN_HALO - DN_CONV // 2
        y = None
        for i in range(DN_CONV):
            w = cw_ref[i:i + 1, part * D_MODEL:(part + 1) * D_MODEL]
            t = ext_ref[base + i:base + i + C, :] * w
            y = t if y is None else y + t
        return _silu(y)

    k_all = conv_silu(kp_ref, kc_ref, kn_ref, 1)
    v_all = conv_silu(vp_ref, vc_ref, vn_ref, 2)
    q_all = conv_silu(qp_ref, qc_ref, qn_ref, 0) if with_q else None

    gcol = gcol_ref[...]
    ld_col = -jnp.exp(alog_r_ref[...]) * _softplus(gcol[:, 0:DN_HEADS] + dtb_r_ref[...])
    beta_col = _sigmoid(gcol[:, DN_HEADS:2 * DN_HEADS])
    ld_row = -jnp.exp(alog_c_ref[...]) * _softplus(grow_ref[0:DN_HEADS, :] + dtb_c_ref[...])

    ri = lax.broadcasted_iota(I32, (C, C), 0)
    ci = lax.broadcasted_iota(I32, (C, C), 1)
    delta = (ri - ci) * (1 - 2 * d)
    incl = delta >= 0
    strict = delta > 0
    incl_t = delta <= 0
    gam_col = jnp.dot(incl.astype(F32), ld_col, preferred_element_type=F32,
                      precision=lax.Precision.HIGHEST)
    gam_row = jnp.dot(ld_row, incl_t.astype(F32), preferred_element_type=F32,
                      precision=lax.Precision.HIGHEST)
    gam_tot = jnp.sum(ld_col, axis=0, keepdims=True)
    eye = (ri == ci).astype(F32)

    H = range(DN_HEADS)
    lanes = [slice(h * DN_DIM, (h + 1) * DN_DIM) for h in H]
    gc = [gam_col[:, h:h + 1] for h in H]
    bc = [beta_col[:, h:h + 1] for h in H]
    kh = [k_all[:, lanes[h]] for h in H]
    kh = [kh[h] * lax.rsqrt(jnp.sum(kh[h] * kh[h], axis=-1, keepdims=True) + EPS) for h in H]
    kb = [kh[h].astype(BF16) for h in H]
    decay = [jnp.exp(jnp.where(incl, gc[h] - gam_row[h:h + 1, :], NEG_BIG)) for h in H]
    kk = [_dot_nt(kb[h], kb[h]) for h in H]
    x = [-(jnp.where(strict, decay[h], 0.0) * bc[h] * kk[h]) for h in H]
    p = [eye + x[h] for h in H]
    xb = [x[h].astype(BF16) for h in H]
    x = [_dot(xb[h], xb[h]) for h in H]
    n_fac = int(math.log2(C)) - 1
    for j in range(n_fac):
        xb = [x[h].astype(BF16) for h in H]
        if j < n_fac - 1:
            r = [_dot(xb[h], jnp.concatenate([xb[h], p[h].astype(BF16)], axis=1)) for h in H]
            x = [r[h][:, :C] for h in H]
            p = [p[h] + r[h][:, C:] for h in H]
        else:
            p = [p[h] + _dot(xb[h], p[h].astype(BF16)) for h in H]
    rhs = [jnp.concatenate([kh[h] * (bc[h] * jnp.exp(gc[h])), v_all[:, lanes[h]] * bc[h]],
                           axis=1).astype(BF16) for h in H]
    sol = [_dot(p[h].astype(BF16), rhs[h]) for h in H]
    w = [sol[h][:, :DN_DIM] for h in H]
    u0 = [sol[h][:, DN_DIM:] for h in H]
    k_end = [(kh[h] * jnp.exp(gam_tot[:, h:h + 1] - gc[h])).astype(BF16) for h in H]
    s = [s_refs[h][...] for h in H]
    sb = [s[h].astype(BF16) for h in H]
    if with_q:
        qh = [q_all[:, lanes[h]] for h in H]
        qh = [qh[h] * (lax.rsqrt(jnp.sum(qh[h] * qh[h], axis=-1, keepdims=True) + EPS) * DN_DIM ** -0.5)
              for h in H]
        qk = [(_dot_nt(qh[h].astype(BF16), kb[h]) * decay[h]).astype(BF16) for h in H]
        wq = [jnp.concatenate([w[h], qh[h] * jnp.exp(gc[h])], axis=0).astype(BF16) for h in H]
        ws = [_dot(wq[h], sb[h]) for h in H]
        ub = [(u0[h] - ws[h][:C]).astype(BF16) for h in H]
        qu = [_dot(qk[h], ub[h]) for h in H]
        for h in H:
            o_ref[:, lanes[h]] = (ws[h][C:] + qu[h]).astype(o_ref.dtype)
    else:
        ws = [_dot(w[h].astype(BF16), sb[h]) for h in H]
        ub = [(u0[h] - ws[h]).astype(BF16) for h in H]
    ku = [_dot_tn(k_end[h], ub[h]) for h in H]
    for h in H:
        s_refs[h][...] = jnp.exp(gam_tot[:, h:h + 1]) * s[h] + ku[h]

    @pl.when(c == n_chunks - 1)
    def _():
        for h in range(DN_HEADS):
            sfin_ref[h] = s_refs[h][...]


def _deltanet_single_pass(main3, gate_col, gate_row, conv_w, alog, dtb, s0, with_q):
    B, T, _ = main3.shape
    C = DN_CHUNK
    n_chunks = T // C
    hpc = C // DN_HALO
    n_halo = T // DN_HALO

    def cix(d, c):
        return jnp.where(d == 0, c, n_chunks - 1 - c)

    def trio(col):
        return [pl.BlockSpec((None, DN_HALO, D_MODEL),
                             lambda d, b, c: (b, jnp.maximum(cix(d, c) * hpc - 1, 0), col)),
                pl.BlockSpec((None, C, D_MODEL), lambda d, b, c: (b, cix(d, c), col)),
                pl.BlockSpec((None, DN_HALO, D_MODEL),
                             lambda d, b, c: (b, jnp.minimum((cix(d, c) + 1) * hpc, n_halo - 1), col))]

    in_specs = (trio(COL_DN_Q) if with_q else []) + trio(COL_DN_K) + trio(COL_DN_V) + [
        pl.BlockSpec((None, None, C, 2 * DN_HEADS), lambda d, b, c: (d, b, cix(d, c), 0)),
        pl.BlockSpec((None, None, None, 2 * DN_HEADS, C), lambda d, b, c: (d, b, cix(d, c), 0, 0)),
        pl.BlockSpec((DN_CONV, 3 * D_MODEL), lambda d, b, c: (0, 0)),
        pl.BlockSpec((None, 1, DN_HEADS), lambda d, b, c: (d, 0, 0)),
        pl.BlockSpec((None, DN_HEADS, 1), lambda d, b, c: (d, 0, 0)),
        pl.BlockSpec((None, 1, DN_HEADS), lambda d, b, c: (d, 0, 0)),
        pl.BlockSpec((None, DN_HEADS, 1), lambda d, b, c: (d, 0, 0)),
        pl.BlockSpec((None, None, DN_HEADS, DN_DIM, DN_DIM), lambda d, b, c: (d, b, 0, 0, 0)),
    ]
    s_spec = pl.BlockSpec((None, None, DN_HEADS, DN_DIM, DN_DIM), lambda d, b, c: (d, b, 0, 0, 0))
    s_shape = jax.ShapeDtypeStruct((2, B, DN_HEADS, DN_DIM, DN_DIM), F32)
    if with_q:
        out_specs = [pl.BlockSpec((None, None, C, D_MODEL), lambda d, b, c: (d, b, cix(d, c), 0)), s_spec]
        out_shape = [jax.ShapeDtypeStruct((2, B, T, D_MODEL), BF16), s_shape]
    else:
        out_specs = [s_spec]
        out_shape = [s_shape]
    n_main = 3 if with_q else 2
    args = [main3] * (3 * n_main) + [
        gate_col, gate_row, conv_w,
        alog.reshape(2, 1, DN_HEADS), alog.reshape(2, DN_HEADS, 1),
        dtb.reshape(2, 1, DN_HEADS), dtb.reshape(2, DN_HEADS, 1), s0]
    out = pl.pallas_call(
        functools.partial(_dn_kernel, with_q=with_q, n_chunks=n_chunks),
        grid=(2, B, n_chunks),
        in_specs=in_specs, out_specs=out_specs, out_shape=out_shape,
        scratch_shapes=[pltpu.VMEM((C + 2 * DN_HALO, D_MODEL), F32)] * 3
        + [pltpu.VMEM((DN_DIM, DN_DIM), F32)] * DN_HEADS,
        compiler_params=_cparams(("arbitrary", "arbitrary", "arbitrary")),
        name="deltanet_q" if with_q else "deltanet_state",
    )(*args)
    return (out[0], out[1]) if with_q else (None, out[0])


def _dn_prep_kernel(*refs, with_q, n_chunks):
    if with_q:
        (qp_ref, qc_ref, qn_ref, kp_ref, kc_ref, kn_ref, vp_ref, vc_ref, vn_ref,
         gcol_ref, grow_ref, cw_ref, alog_r_ref, alog_c_ref, dtb_r_ref, dtb_c_ref,
         w_ref, u0_ref, ke_ref, gt_ref, qs_ref, qk_ref) = refs
    else:
        (kp_ref, kc_ref, kn_ref, vp_ref, vc_ref, vn_ref,
         gcol_ref, grow_ref, cw_ref, alog_r_ref, alog_c_ref, dtb_r_ref, dtb_c_ref,
         w_ref, u0_ref, ke_ref, gt_ref) = refs
    c = pl.program_id(1)
    C = DN_CHUNK
    has_prev = (c > 0).astype(BF16)
    has_next = (c < n_chunks - 1).astype(BF16)

    pad = DN_CONV // 2
    n_sh = DN_CONV - 1
    sr = lax.broadcasted_iota(I32, (n_sh * C, C + 2 * DN_HALO), 0)
    sc = lax.broadcasted_iota(I32, (n_sh * C, C + 2 * DN_HALO), 1)
    blk = sr // C
    off = jnp.where(blk < pad, blk - pad, blk - pad + 1)
    shift_mat = (sc == DN_HALO + (sr - blk * C) + off).astype(BF16)

    def conv_silu(p_ref, c_ref, n_ref, part):
        cur = c_ref[...]
        ext = jnp.concatenate([p_ref[...] * has_prev, cur, n_ref[...] * has_next], axis=0)
        sh = _dot(shift_mat, ext)
        taps = [sh[j * C:(j + 1) * C] for j in range(pad)] + [cur.astype(F32)] + \
               [sh[j * C:(j + 1) * C] for j in range(pad, n_sh)]
        y = None
        for i in range(DN_CONV):
            t = taps[i] * cw_ref[i:i + 1, part * D_MODEL:(part + 1) * D_MODEL]
            y = t if y is None else y + t
        return _silu(y)

    k_all = conv_silu(kp_ref, kc_ref, kn_ref, 1)
    v_all = conv_silu(vp_ref, vc_ref, vn_ref, 2)
    q_all = conv_silu(qp_ref, qc_ref, qn_ref, 0) if with_q else None

    ri = lax.broadcasted_iota(I32, (C, C), 0)
    ci = lax.broadcasted_iota(I32, (C, C), 1)
    eye = (ri == ci).astype(F32)
    H = range(DN_HEADS)
    lanes = [slice(h * DN_DIM, (h + 1) * DN_DIM) for h in H]
    kh = [k_all[:, lanes[h]] for h in H]
    kh = [kh[h] * lax.rsqrt(jnp.sum(kh[h] * kh[h], axis=-1, keepdims=True) + EPS) for h in H]
    kb = [kh[h].astype(BF16) for h in H]
    if with_q:
        qh = [q_all[:, lanes[h]] for h in H]
        qh = [qh[h] * (lax.rsqrt(jnp.sum(qh[h] * qh[h], axis=-1, keepdims=True) + EPS) * DN_DIM ** -0.5)
              for h in H]
        gram = [_dot_nt(jnp.concatenate([kb[h], qh[h].astype(BF16)], axis=0), kb[h]) for h in H]
        kk = [gram[h][:C] for h in H]
        qk_raw = [gram[h][C:] for h in H]
    else:
        kk = [_dot_nt(kb[h], kb[h]) for h in H]

    D2 = range(2)
    DH = [(d, h) for d in D2 for h in H]
    incl, strict, gam_col, gam_row, gam_tot, beta_col = [], [], [], [], [], []
    for d in D2:
        inc = (ri >= ci) if d == 0 else (ri <= ci)
        inc_t = (ri <= ci) if d == 0 else (ri >= ci)
        gcol = gcol_ref[d]
        ld_col = -jnp.exp(alog_r_ref[d]) * _softplus(gcol[:, 0:DN_HEADS] + dtb_r_ref[d])
        ld_row = -jnp.exp(alog_c_ref[d]) * _softplus(grow_ref[d][0:DN_HEADS, :] + dtb_c_ref[d])
        incl.append(inc)
        strict.append((ri > ci) if d == 0 else (ri < ci))
        beta_col.append(_sigmoid(gcol[:, DN_HEADS:2 * DN_HEADS]))
        gam_col.append(jnp.dot(inc.astype(F32), ld_col, preferred_element_type=F32,
                               precision=lax.Precision.HIGHEST))
        gam_row.append(jnp.dot(ld_row, inc_t.astype(F32), preferred_element_type=F32,
                               precision=lax.Precision.HIGHEST))
        tot = jnp.sum(ld_col, axis=0, keepdims=True)
        gam_tot.append(tot)
        gt_ref[d] = tot
    gc = [gam_col[d][:, h:h + 1] for d, h in DH]
    bc = [beta_col[d][:, h:h + 1] for d, h in DH]
    decay = [jnp.exp(jnp.where(incl[d], gc[i] - gam_row[d][h:h + 1, :], NEG_BIG))
             for i, (d, h) in enumerate(DH)]
    x = [-(jnp.where(strict[d], decay[i], 0.0) * bc[i] * kk[h]) for i, (d, h) in enumerate(DH)]
    N = range(len(DH))
    p = [eye + x[i] for i in N]
    xb = [x[i].astype(BF16) for i in N]
    x = [_dot(xb[i], xb[i]) for i in N]
    n_fac = int(math.log2(C)) - 1
    for j in range(n_fac):
        xb = [x[i].astype(BF16) for i in N]
        if j < n_fac - 1:
            r = [_dot(xb[i], jnp.concatenate([xb[i], p[i].astype(BF16)], axis=1)) for i in N]
            x = [r[i][:, :C] for i in N]
            p = [p[i] + r[i][:, C:] for i in N]
        else:
            p = [p[i] + _dot(xb[i], p[i].astype(BF16)) for i in N]
    rhs = [jnp.concatenate([kh[h] * (bc[i] * jnp.exp(gc[i])), v_all[:, lanes[h]] * bc[i]],
                           axis=1).astype(BF16) for i, (d, h) in enumerate(DH)]
    sol = [_dot(p[i].astype(BF16), rhs[i]) for i in N]
    for i, (d, h) in enumerate(DH):
        w_ref[d, :, lanes[h]] = sol[i][:, :DN_DIM].astype(BF16)
        u0_ref[d, :, lanes[h]] = sol[i][:, DN_DIM:].astype(BF16)
        ke_ref[d, :, lanes[h]] = (kh[h] * jnp.exp(gam_tot[d][:, h:h + 1] - gc[i])).astype(BF16)
        if with_q:
            qs_ref[d, :, lanes[h]] = (qh[h] * jnp.exp(gc[i])).astype(BF16)
            qk_ref[d, :, h * C:(h + 1) * C] = (qk_raw[h] * decay[i]).astype(BF16)


def _dn_scan_kernel(*refs, with_q, n_chunks):
    n_state = 2 * DN_HEADS
    s_refs = refs[-n_state:]
    refs = refs[:-n_state]
    if with_q:
        (w0, w1, u0, u1, k0, k1, g0, g1, qs0, qs1, qk0, qk1, s0_ref, o0_ref, o1_ref, sfin_ref) = refs
        qs_r, qk_r, o_r = (qs0, qs1), (qk0, qk1), (o0_ref, o1_ref)
    else:
        (w0, w1, u0, u1, k0, k1, g0, g1, s0_ref, sfin_ref) = refs
    w_r, u_r, k_r, g_r = (w0, w1), (u0, u1), (k0, k1), (g0, g1)
    c = pl.program_id(1)
    C = DN_CHUNK
    DH = [(d, h) for d in range(2) for h in range(DN_HEADS)]
    N = range(len(DH))
    lanes = [slice(h * DN_DIM, (h + 1) * DN_DIM) for h in range(DN_HEADS)]

    @pl.when(c == 0)
    def _():
        for i, (d, h) in enumerate(DH):
            s_refs[i][...] = s0_ref[d, h]

    s = [s_refs[i][...] for i in N]
    sb = [s[i].astype(BF16) for i in N]
    w = [w_r[d][:, lanes[h]] for d, h in DH]
    if with_q:
        wq = [jnp.concatenate([w[i], qs_r[d][:, lanes[h]]], axis=0) for i, (d, h) in enumerate(DH)]
        ws = [_dot(wq[i], sb[i]) for i in N]
        ub = [(u_r[d][:, lanes[h]].astype(F32) - ws[i][:C]).astype(BF16) for i, (d, h) in enumerate(DH)]
        qu = [_dot(qk_r[d][:, h * C:(h + 1) * C], ub[i]) for i, (d, h) in enumerate(DH)]
        for i, (d, h) in enumerate(DH):
            o_r[d][:, lanes[h]] = (ws[i][C:] + qu[i]).astype(BF16)
    else:
        ws = [_dot(w[i], sb[i]) for i in N]
        ub = [(u_r[d][:, lanes[h]].astype(F32) - ws[i]).astype(BF16) for i, (d, h) in enumerate(DH)]
    ku = [_dot_tn(k_r[d][:, lanes[h]], ub[i]) for i, (d, h) in enumerate(DH)]
    for i, (d, h) in enumerate(DH):
        s_refs[i][...] = jnp.exp(g_r[d][:, h:h + 1]) * s[i] + ku[i]

    @pl.when(c == n_chunks - 1)
    def _():
        for i, (d, h) in enumerate(DH):
            sfin_ref[d, h] = s_refs[i][...]


def _deltanet(main3, gate_col, gate_row, conv_w, alog, dtb, s0, with_q):
    B, T, _ = main3.shape
    C = DN_CHUNK
    n_chunks = T // C
    hpc = C // DN_HALO
    n_halo = T // DN_HALO

    def trio(col):
        return [pl.BlockSpec((None, DN_HALO, D_MODEL), lambda b, c: (b, jnp.maximum(c * hpc - 1, 0), col)),
                pl.BlockSpec((None, C, D_MODEL), lambda b, c: (b, c, col)),
                pl.BlockSpec((None, DN_HALO, D_MODEL),
                             lambda b, c: (b, jnp.minimum((c + 1) * hpc, n_halo - 1), col))]

    vec = lambda shape: pl.BlockSpec(shape, lambda b, c: (0,) * len(shape))
    in_specs = (trio(COL_DN_Q) if with_q else []) + trio(COL_DN_K) + trio(COL_DN_V) + [
        pl.BlockSpec((2, None, C, 2 * DN_HEADS), lambda b, c: (0, b, c, 0)),
        pl.BlockSpec((2, None, None, 2 * DN_HEADS, C), lambda b, c: (0, b, c, 0, 0)),
        vec((DN_CONV, 3 * D_MODEL)),
        vec((2, 1, DN_HEADS)), vec((2, DN_HEADS, 1)), vec((2, 1, DN_HEADS)), vec((2, DN_HEADS, 1)),
    ]
    wide = lambda n: (pl.BlockSpec((2, None, C, n), lambda b, c: (0, b, c, 0)),
                      jax.ShapeDtypeStruct((2, B, T, n), BF16))
    outs = [wide(D_MODEL), wide(D_MODEL), wide(D_MODEL),
            (pl.BlockSpec((2, None, None, 1, DN_HEADS), lambda b, c: (0, b, c, 0, 0)),
             jax.ShapeDtypeStruct((2, B, n_chunks, 1, DN_HEADS), F32))]
    if with_q:
        outs += [wide(D_MODEL), wide(DN_HEADS * C)]
    n_main = 3 if with_q else 2
    prep = pl.pallas_call(
        functools.partial(_dn_prep_kernel, with_q=with_q, n_chunks=n_chunks),
        grid=(B, n_chunks),
        in_specs=in_specs, out_specs=[o[0] for o in outs], out_shape=[o[1] for o in outs],
        compiler_params=_cparams(("arbitrary", "arbitrary")),
        name="dn_prep_q" if with_q else "dn_prep",
    )(*([main3] * (3 * n_main)), gate_col, gate_row, conv_w,
      alog.reshape(2, 1, DN_HEADS), alog.reshape(2, DN_HEADS, 1),
      dtb.reshape(2, 1, DN_HEADS), dtb.reshape(2, DN_HEADS, 1))

    def both_dirs(arr, n):
        if n is None:
            return [pl.BlockSpec((None, None, None, 1, DN_HEADS), lambda b, c: (0, b, c, 0, 0)),
                    pl.BlockSpec((None, None, None, 1, DN_HEADS),
                                 lambda b, c: (1, b, n_chunks - 1 - c, 0, 0))], [arr, arr]
        return [pl.BlockSpec((None, None, C, n), lambda b, c: (0, b, c, 0)),
                pl.BlockSpec((None, None, C, n), lambda b, c: (1, b, n_chunks - 1 - c, 0))], [arr, arr]

    specs, args = [], []
    widths = [D_MODEL, D_MODEL, D_MODEL, None] + ([D_MODEL, DN_HEADS * C] if with_q else [])
    for arr, n in zip(prep, widths):
        sp, ar = both_dirs(arr, n)
        specs += sp
        args += ar
    s_spec = pl.BlockSpec((2, None, DN_HEADS, DN_DIM, DN_DIM), lambda b, c: (0, b, 0, 0, 0))
    s_shape = jax.ShapeDtypeStruct((2, B, DN_HEADS, DN_DIM, DN_DIM), F32)
    if with_q:
        out_specs = [pl.BlockSpec((None, C, D_MODEL), lambda b, c: (b, c, 0)),
                     pl.BlockSpec((None, C, D_MODEL), lambda b, c: (b, n_chunks - 1 - c, 0)), s_spec]
        out_shape = [jax.ShapeDtypeStruct((B, T, D_MODEL), BF16)] * 2 + [s_shape]
    else:
        out_specs, out_shape = [s_spec], [s_shape]
    out = pl.pallas_call(
        functools.partial(_dn_scan_kernel, with_q=with_q, n_chunks=n_chunks),
        grid=(B, n_chunks),
        in_specs=specs + [s_spec], out_specs=out_specs, out_shape=out_shape,
        scratch_shapes=[pltpu.VMEM((DN_DIM, DN_DIM), F32)] * (2 * DN_HEADS),
        compiler_params=_cparams(("arbitrary", "arbitrary")),
        name="dn_scan_q" if with_q else "dn_scan",
    )(*args, s0)
    return (out[0], out[1], out[2]) if with_q else (None, None, out[0])


def _rope_tables(S):
    half = AT_DIM // 2
    nf = half // 2
    inv_freq = ROPE_BASE ** (-jnp.arange(nf, dtype=F32) / nf)
    t = jnp.arange(S, dtype=jnp.int32)
    row = (t // GRID_W).astype(F32)
    col = (t % GRID_W).astype(F32)
    lane = jnp.arange(LANES)
    dd = lane % AT_DIM
    pos = jnp.where((dd < half)[None, :], row[:, None], col[:, None])
    ang = pos * inv_freq[lane % nf][None, :]
    first = ((lane % half) < nf)[None, :]
    sin = jnp.sin(ang)
    return jnp.cos(ang), jnp.where(first, -sin, 0.0), jnp.where(first, 0.0, sin)


def _rope_kernel(q_ref, k_ref, v_ref, cos_ref, sa_ref, sb_ref, qo_ref, ko_ref, vo_ref):
    cos, sa, sb = cos_ref[...], sa_ref[...], sb_ref[...]
    nf = AT_DIM // 4

    def rot(x):
        return x * cos + pltpu.roll(x, LANES - nf, 1) * sa + pltpu.roll(x, nf, 1) * sb

    for j in range(AT_Q_HEADS * AT_DIM // LANES):
        lanes = slice(j * LANES, (j + 1) * LANES)
        qo_ref[:, lanes] = (rot(q_ref[:, lanes].astype(F32)) * AT_DIM ** -0.5).astype(BF16)
    ko_ref[...] = rot(k_ref[...]).astype(BF16)
    vo_ref[...] = v_ref[...].astype(BF16)


def _rope(main, small, tables, S, tm=512):
    n_tok = main.shape[0]
    per_seq = S // tm
    tab_spec = pl.BlockSpec((tm, LANES), lambda i: (i % per_seq, 0))
    return pl.pallas_call(
        _rope_kernel,
        grid=(n_tok // tm,),
        in_specs=[pl.BlockSpec((tm, D_MODEL), lambda i: (i, COL_AT_Q)),
                  pl.BlockSpec((tm, LANES), lambda i: (i, 0)),
                  pl.BlockSpec((tm, LANES), lambda i: (i, 1)),
                  tab_spec, tab_spec, tab_spec],
        out_specs=[pl.BlockSpec((tm, D_MODEL), lambda i: (i, 0)),
                   pl.BlockSpec((tm, LANES), lambda i: (i, 0)),
                   pl.BlockSpec((tm, LANES), lambda i: (i, 0))],
        out_shape=[jax.ShapeDtypeStruct((n_tok, D_MODEL), BF16),
                   jax.ShapeDtypeStruct((n_tok, LANES), BF16),
                   jax.ShapeDtypeStruct((n_tok, LANES), BF16)],
        compiler_params=_cparams(("arbitrary",)),
        name="rope",
    )(main, small, small, *tables)


def _attn_kernel(*refs, local, n_blocks, q_scale):
    if local:
        (q_ref, kp_ref, kc_ref, kn_ref, vp_ref, vc_ref, vn_ref, kx_ref, vx_ref, sink_ref, o_ref) = refs
    else:
        (q_ref, kx_ref, vx_ref, sink_ref, o_ref) = refs
    P = AT_BLOCK
    G = AT_Q_HEADS // AT_KV_HEADS
    L = kx_ref.shape[0]
    kx = kx_ref[...].astype(BF16)
    vx = vx_ref[...].astype(BF16)
    if local:
        i = pl.program_id(1)
        k_all = jnp.concatenate([kp_ref[...], kc_ref[...], kn_ref[...], kx], axis=0)
        v_all = jnp.concatenate([vp_ref[...], vc_ref[...], vn_ref[...], vx], axis=0)
        qi = lax.broadcasted_iota(I32, (P, P), 0)
        kj = lax.broadcasted_iota(I32, (P, P), 1)
        b_prev = jnp.where(kj >= qi, 0.0, NEG_BIG) + jnp.where(i > 0, 0.0, NEG_BIG)
        b_next = jnp.where(kj <= qi, 0.0, NEG_BIG) + jnp.where(i < n_blocks - 1, 0.0, NEG_BIG)
        bias = jnp.concatenate([b_prev, jnp.zeros((P, P), F32), b_next, jnp.zeros((P, L), F32)], axis=1)
    else:
        k_all, v_all, bias = kx, vx, None
    n_keys = k_all.shape[0]
    lo = lax.broadcasted_iota(I32, (P, LANES), 1) < AT_DIM
    qf = q_ref[...].astype(F32) * q_scale
    pieces = []
    for qh in range(AT_Q_HEADS):
        blk = qf[:, (qh // 2) * LANES:(qh // 2 + 1) * LANES]
        want_lo = qh // G == 0
        if want_lo != (qh % 2 == 0):
            blk = pltpu.roll(blk, AT_DIM, 1)
        pieces.append(jnp.where(lo if want_lo else ~lo, blk, 0.0).astype(BF16))
    qs = jnp.concatenate(pieces, axis=0)
    s = _dot_nt(qs, k_all)
    if bias is not None:
        s = (s.reshape(AT_Q_HEADS, P, n_keys) + bias[None]).reshape(AT_Q_HEADS * P, n_keys)
    sink = sink_ref[...]
    m = jnp.maximum(jnp.max(s, axis=-1, keepdims=True), sink)
    p = jnp.exp(s - m)
    den = jnp.sum(p, axis=-1, keepdims=True) + jnp.exp(sink - m)
    o = _dot(p.astype(BF16), v_all) / den
    for j in range(AT_Q_HEADS // 2):
        a = o[(2 * j) * P:(2 * j + 1) * P]
        b = o[(2 * j + 1) * P:(2 * j + 2) * P]
        if (2 * j) // G == 0:
            out = jnp.where(lo, a, pltpu.roll(b, AT_DIM, 1))
        else:
            out = jnp.where(lo, pltpu.roll(a, AT_DIM, 1), b)
        o_ref[:, j * LANES:(j + 1) * LANES] = out.astype(BF16)


def _attention_local(q_r, k_r, v_r, small_c, sinks, B, S, L):
    P = AT_BLOCK
    nb = S // P

    def kv_trio():
        return [pl.BlockSpec((P, LANES), lambda b, i: (b * nb + jnp.maximum(i - 1, 0), 0)),
                pl.BlockSpec((P, LANES), lambda b, i: (b * nb + i, 0)),
                pl.BlockSpec((P, LANES), lambda b, i: (b * nb + jnp.minimum(i + 1, nb - 1), 0))]

    return pl.pallas_call(
        functools.partial(_attn_kernel, local=True, n_blocks=nb, q_scale=1.0),
        grid=(B, nb),
        in_specs=[pl.BlockSpec((P, D_MODEL), lambda b, i: (b * nb + i, 0))] + kv_trio() + kv_trio() + [
            pl.BlockSpec((L, LANES), lambda b, i: (b, 0)),
            pl.BlockSpec((L, LANES), lambda b, i: (b, 1)),
            pl.BlockSpec((AT_Q_HEADS * AT_BLOCK, 1), lambda b, i: (0, 0))],
        out_specs=pl.BlockSpec((P, D_MODEL), lambda b, i: (b * nb + i, 0)),
        out_shape=jax.ShapeDtypeStruct((B * S, D_MODEL), BF16),
        compiler_params=_cparams(("arbitrary", "arbitrary")),
        name="attn_local",
    )(q_r, k_r, k_r, k_r, v_r, v_r, v_r, small_c, small_c, sinks)


def _attention_ctx(main_c, small_c, sinks, B, L):
    P = AT_BLOCK
    nb = L // P
    return pl.pallas_call(
        functools.partial(_attn_kernel, local=False, n_blocks=nb, q_scale=AT_DIM ** -0.5),
        grid=(B, nb),
        in_specs=[pl.BlockSpec((P, D_MODEL), lambda b, i: (b * nb + i, COL_AT_Q)),
                  pl.BlockSpec((L, LANES), lambda b, i: (b, 0)),
                  pl.BlockSpec((L, LANES), lambda b, i: (b, 1)),
                  pl.BlockSpec((AT_Q_HEADS * AT_BLOCK, 1), lambda b, i: (0, 0))],
        out_specs=pl.BlockSpec((P, D_MODEL), lambda b, i: (b * nb + i, 0)),
        out_shape=jax.ShapeDtypeStruct((B * L, D_MODEL), BF16),
        compiler_params=_cparams(("arbitrary", "arbitrary")),
        name="attn_ctx",
    )(main_c, small_c, small_c, sinks)


def _merge_kernel(ysg_ref, of_ref, ob_ref, dng_ref, yat_ref, g0_ref, g1_ref, g2_ref, x_ref, mod_ref,
                  dn_norm_ref, post_ref, pre_ref, wsg_ref, wdn_ref, wat_ref, wout_ref, rw_ref, rb_ref,
                  xo_ref, h2_ref, lg_ref):
    o = of_ref[...].astype(F32) + ob_ref[...].astype(F32)
    dn_g = dn_norm_ref[...]
    parts = []
    for h in range(DN_HEADS):
        lanes = slice(h * DN_DIM, (h + 1) * DN_DIM)
        parts.append(_rms(o[:, lanes], dn_g) * _silu(dng_ref[:, lanes].astype(F32)))
    ydn = jnp.concatenate(parts, axis=1).astype(BF16)
    m = (_sigmoid(g0_ref[...].astype(F32)) * _dot(ysg_ref[...], wsg_ref[...])
         + _sigmoid(g1_ref[...].astype(F32)) * _dot(ydn, wdn_ref[...])
         + _sigmoid(g2_ref[...].astype(F32)) * _dot(yat_ref[...], wat_ref[...]))
    y = _dot(m.astype(BF16), wout_ref[...])
    gate1 = mod_ref[:, 2 * D_MODEL:3 * D_MODEL]
    sh2 = mod_ref[:, 3 * D_MODEL:4 * D_MODEL]
    sc2 = mod_ref[:, 4 * D_MODEL:5 * D_MODEL]
    xn = x_ref[...] + gate1 * _rms(y, post_ref[...])
    xo_ref[...] = xn
    h2 = _rms(xn, pre_ref[...]) * (1.0 + sc2) + sh2
    h2_ref[...] = h2
    lg_ref[...] = _dot(h2.astype(BF16), rw_ref[...]) + rb_ref[...]


def _merge(ysg, o_fwd, o_bwd, main, yat, x, mod, mod_row, lw, tm=256):
    n_tok = x.shape[0]
    const = lambda i: (0, 0)
    wspec = pl.BlockSpec((D_MODEL, D_MODEL), const, pipeline_mode=pl.Buffered(1))
    vspec = pl.BlockSpec((1, D_MODEL), const)
    return pl.pallas_call(
        _merge_kernel,
        grid=(n_tok // tm,),
        in_specs=[pl.BlockSpec((tm, D_MODEL), lambda i: (i, 0)),
                  pl.BlockSpec((tm, D_MODEL), lambda i: (i, 0)),
                  pl.BlockSpec((tm, D_MODEL), lambda i: (i, 0)),
                  pl.BlockSpec((tm, D_MODEL), lambda i: (i, COL_DN_G)),
                  pl.BlockSpec((tm, D_MODEL), lambda i: (i, 0)),
                  pl.BlockSpec((tm, D_MODEL), lambda i: (i, COL_GATE0)),
                  pl.BlockSpec((tm, D_MODEL), lambda i: (i, COL_GATE0 + 1)),
                  pl.BlockSpec((tm, D_MODEL), lambda i: (i, COL_GATE0 + 2)),
                  pl.BlockSpec((tm, D_MODEL), lambda i: (i, 0)),
                  pl.BlockSpec((None, 1, 6 * D_MODEL), lambda i: (mod_row(i * tm), 0, 0)),
                  pl.BlockSpec((1, DN_DIM), const), vspec, vspec,
                  wspec, wspec, wspec, wspec,
                  pl.BlockSpec((D_MODEL, LANES), const), pl.BlockSpec((1, LANES), const)],
        out_specs=[pl.BlockSpec((tm, D_MODEL), lambda i: (i, 0)),
                   pl.BlockSpec((tm, D_MODEL), lambda i: (i, 0)),
                   pl.BlockSpec((tm, LANES), lambda i: (i, 0))],
        out_shape=[jax.ShapeDtypeStruct((n_tok, D_MODEL), F32),
                   jax.ShapeDtypeStruct((n_tok, D_MODEL), F32),
                   jax.ShapeDtypeStruct((n_tok, LANES), F32)],
        compiler_params=_cparams(("arbitrary",)),
        name="merge",
    )(ysg, o_fwd, o_bwd, main, yat, main, main, main, x, mod,
      lw["dn_norm_g"], lw["norm_post_mix"], lw["norm_pre_ffn"],
      lw["w_proj_sg"], lw["w_proj_dn"], lw["w_proj_at"], lw["w_out"], lw["router_w"], lw["router_b"])


def _route_kernel(lg_ref, idx_ref, gate_ref, rank_ref, cnt_ref, run_ref):
    tm = lg_ref.shape[0]

    @pl.when(pl.program_id(0) == 0)
    def _():
        run_ref[...] = jnp.zeros_like(run_ref)

    l = lg_ref[...]
    lane = lax.broadcasted_iota(I32, l.shape, 1).astype(F32)
    vals, onehots = [], []
    for k in range(TOP_K):
        m = jnp.max(l, axis=-1, keepdims=True)
        ik = jnp.min(jnp.where(l == m, lane, float(LANES)), axis=-1, keepdims=True)
        oh = lane == ik
        idx_ref[:, k:k + 1] = ik.astype(I32)
        vals.append(m)
        onehots.append(oh)
        l = jnp.where(oh, -jnp.inf, l)
    es = [jnp.exp(v - vals[0]) for v in vals]
    den = es[0] + es[1] + es[2] + es[3]
    sel = jnp.zeros(l.shape, F32)
    for k in range(TOP_K):
        gate_ref[:, k:k + 1] = es[k] / den
        sel = sel + onehots[k].astype(F32)
    ri = lax.broadcasted_iota(I32, (tm, tm), 0)
    ci = lax.broadcasted_iota(I32, (tm, tm), 1)
    before = _dot((ri > ci).astype(BF16), sel.astype(BF16)) + run_ref[...]
    for k in range(TOP_K):
        rank_ref[:, k:k + 1] = jnp.sum(jnp.where(onehots[k], before, 0.0), axis=-1,
                                       keepdims=True).astype(I32)
    run_ref[...] = run_ref[...] + jnp.sum(sel, axis=0, keepdims=True)
    cnt_ref[...] = run_ref[...]


def _route(logits, tm=256):
    n_tok = logits.shape[0]
    small = lambda dt: jax.ShapeDtypeStruct((n_tok, TOP_K), dt)
    kspec = pl.BlockSpec((tm, TOP_K), lambda i: (i, 0))
    return pl.pallas_call(
        _route_kernel,
        grid=(n_tok // tm,),
        in_specs=[pl.BlockSpec((tm, LANES), lambda i: (i, 0))],
        out_specs=[kspec, kspec, kspec, pl.BlockSpec((1, LANES), lambda i: (0, 0))],
        out_shape=[small(I32), small(F32), small(I32), jax.ShapeDtypeStruct((1, LANES), F32)],
        scratch_shapes=[pltpu.VMEM((1, LANES), F32)],
        compiler_params=_cparams(("arbitrary",)),
        name="route",
    )(logits)


def _dispatch_kernel(dest_ref, h_ref, xs_in_ref, xs_ref, sem):
    del xs_in_ref
    tm = h_ref.shape[0]

    def row_copy(r, k):
        return pltpu.make_async_copy(h_ref.at[pl.ds(r, 1)],
                                     xs_ref.at[pl.ds(dest_ref[r * TOP_K + k], 1)], sem)

    def issue(r, carry):
        for k in range(TOP_K):
            row_copy(r, k).start()
        return carry

    lax.fori_loop(0, tm, issue, 0)

    def drain(r, carry):
        for k in range(TOP_K):
            row_copy(r, k).wait()
        return carry

    lax.fori_loop(0, tm, drain, 0)


def _dispatch(dest_flat, h2, n_slots, tm=256):
    n_tok = h2.shape[0]
    xs0 = jnp.zeros((n_slots, D_MODEL), F32)
    return pl.pallas_call(
        _dispatch_kernel,
        grid=(n_tok // tm,),
        in_specs=[pl.BlockSpec((tm * TOP_K,), lambda i: (i,), memory_space=pltpu.SMEM),
                  pl.BlockSpec((tm, D_MODEL), lambda i: (i, 0)),
                  pl.BlockSpec(memory_space=pl.ANY)],
        out_specs=pl.BlockSpec(memory_space=pl.ANY),
        out_shape=jax.ShapeDtypeStruct((n_slots, D_MODEL), F32),
        scratch_shapes=[pltpu.SemaphoreType.DMA],
        input_output_aliases={2: 0},
        compiler_params=_cparams(("arbitrary",)),
        name="moe_dispatch",
    )(dest_flat, h2, xs0)


def _expert_kernel(te_ref, nu_ref, xs_ref, wgu_ref, bgu_ref, wd_ref, bd_ref, y_ref):
    del te_ref

    @pl.when(pl.program_id(0) < nu_ref[0])
    def _():
        gu = _dot(xs_ref[...].astype(BF16), wgu_ref[...]) + bgu_ref[...]
        g = jnp.minimum(gu[:, :D_EXPERT], SWIGLU_LIMIT)
        lin = jnp.clip(gu[:, D_EXPERT:], -SWIGLU_LIMIT, SWIGLU_LIMIT)
        act = g * _sigmoid(SWIGLU_ALPHA * g) * (lin + 1.0)
        y_ref[...] = _dot(act.astype(BF16), wd_ref[...]) + bd_ref[...]

    @pl.when(pl.program_id(0) >= nu_ref[0])
    def _():
        y_ref[...] = jnp.zeros_like(y_ref)


def _experts(tile_expert, n_used, xs, wgu, bgu, wd, bd, tm):
    n_slots = xs.shape[0]
    n_tiles = n_slots // tm

    def row(i, te, nu):
        return (jnp.minimum(i, nu[0] - 1), 0)

    grid_spec = pltpu.PrefetchScalarGridSpec(
        num_scalar_prefetch=2,
        grid=(n_tiles,),
        in_specs=[pl.BlockSpec((tm, D_MODEL), row),
                  pl.BlockSpec((None, D_MODEL, 2 * D_EXPERT), lambda i, te, nu: (te[i], 0, 0)),
                  pl.BlockSpec((None, 1, 2 * D_EXPERT), lambda i, te, nu: (te[i], 0, 0)),
                  pl.BlockSpec((None, D_EXPERT, D_MODEL), lambda i, te, nu: (te[i], 0, 0)),
                  pl.BlockSpec((None, 1, D_MODEL), lambda i, te, nu: (te[i], 0, 0))],
        out_specs=pl.BlockSpec((tm, D_MODEL), lambda i, te, nu: (i, 0)),
    )
    return pl.pallas_call(
        _expert_kernel,
        grid_spec=grid_spec,
        out_shape=jax.ShapeDtypeStruct((n_slots, D_MODEL), F32),
        compiler_params=_cparams(("arbitrary",)),
        name="moe_experts",
    )(tile_expert, n_used, xs, wgu, bgu, wd, bd)


def _combine_kernel(dest_ref, gate_ref, x_ref, mod_ref, post_ref, y_ref, xo_ref, buf_ref, sem):
    tm = x_ref.shape[0]

    def row_copy(r, k):
        return pltpu.make_async_copy(y_ref.at[pl.ds(dest_ref[r * TOP_K + k], 1)],
                                     buf_ref.at[k, pl.ds(r, 1)], sem)

    def issue(r, carry):
        for k in range(TOP_K):
            row_copy(r, k).start()
        return carry

    lax.fori_loop(0, tm, issue, 0)

    def drain(r, carry):
        for k in range(TOP_K):
            row_copy(r, k).wait()
        return carry

    lax.fori_loop(0, tm, drain, 0)

    y = None
    for k in range(TOP_K):
        t = buf_ref[k] * gate_ref[:, k:k + 1]
        y = t if y is None else y + t
    gate2 = mod_ref[:, 5 * D_MODEL:6 * D_MODEL]
    xo_ref[...] = x_ref[...] + gate2 * _rms(y, post_ref[...])


def _combine(dest_flat, gate, x_mid, mod, mod_row, post_g, y, tm=256):
    n_tok = x_mid.shape[0]
    return pl.pallas_call(
        _combine_kernel,
        grid=(n_tok // tm,),
        in_specs=[pl.BlockSpec((tm * TOP_K,), lambda i: (i,), memory_space=pltpu.SMEM),
                  pl.BlockSpec((tm, TOP_K), lambda i: (i, 0)),
                  pl.BlockSpec((tm, D_MODEL), lambda i: (i, 0)),
                  pl.BlockSpec((None, 1, 6 * D_MODEL), lambda i: (mod_row(i * tm), 0, 0)),
                  pl.BlockSpec((1, D_MODEL), lambda i: (0, 0)),
                  pl.BlockSpec(memory_space=pl.ANY)],
        out_specs=pl.BlockSpec((tm, D_MODEL), lambda i: (i, 0)),
        out_shape=jax.ShapeDtypeStruct((n_tok, D_MODEL), F32),
        scratch_shapes=[pltpu.VMEM((TOP_K, tm, D_MODEL), F32), pltpu.SemaphoreType.DMA],
        compiler_params=_cparams(("arbitrary",)),
        name="moe_combine",
    )(dest_flat, gate, x_mid, mod, post_g, y)


def _moe(h2, logits, x_mid, mod, mod_row, lw, tm_e=512):
    n_tok = h2.shape[0]
    idx, gate, rank, cnt = _route(logits)
    counts = cnt[0, :N_EXPERTS].astype(I32)
    padded = (counts + tm_e - 1) // tm_e * tm_e
    pad_end = jnp.cumsum(padded)
    offs = pad_end - padded
    dest = (jnp.take(offs, idx, axis=0) + rank).reshape(-1).astype(I32)
    n_tiles = (n_tok * TOP_K + tm_e - 1) // tm_e + N_EXPERTS
    tile_start = jnp.arange(n_tiles, dtype=I32) * tm_e
    tile_expert = jnp.minimum(jnp.sum(pad_end[None, :] <= tile_start[:, None], axis=1),
                              N_EXPERTS - 1).astype(I32)
    n_used = (pad_end[-1:] // tm_e).astype(I32)
    xs = _dispatch(dest, h2, n_tiles * tm_e)
    y = _experts(tile_expert, n_used, xs, lw["exp_w_gu"], lw["exp_b_gu"], lw["exp_w_down"],
                 lw["exp_b_down"], tm_e)
    return _combine(dest, gate, x_mid, mod, mod_row, lw["norm_post_ffn"], y)


def _split_w_in(w_in):
    offs, o = {}, 0
    for name, width in (("dn_k", 1024), ("dn_v", 1024), ("dn_a", 16), ("dn_b", 16), ("at_k", 128),
                        ("at_v", 128), ("dn_q", 1024), ("dn_g", 1024), ("at_q", 1024),
                        ("sg_u", 1024), ("sg_v", 1024), ("gates", 3072)):
        offs[name] = (o, o + width)
        o += width
    sl = lambda n: w_in[:, offs[n][0]:offs[n][1]]
    w_main = jnp.concatenate([sl(n) for n in ("dn_k", "dn_v", "dn_q", "dn_g", "at_q", "sg_u", "sg_v",
                                              "gates")], axis=1).astype(BF16)
    pad = jnp.zeros((w_in.shape[0], N_SMALL_COLS - 2 * LANES - 4 * DN_HEADS), w_in.dtype)
    w_small = jnp.concatenate([sl("at_k"), sl("at_v"), sl("dn_a"), sl("dn_b"), pad], axis=1).astype(BF16)
    return w_main, w_small


def _dn_gates(small, B, T):
    ab = small[:, 2 * LANES:2 * LANES + 4 * DN_HEADS].reshape(B, T, 2, 2, DN_HEADS)
    col = jnp.transpose(ab, (3, 0, 1, 2, 4)).reshape(2, B, T, 2 * DN_HEADS)
    row = jnp.transpose(col.reshape(2, B, T // DN_CHUNK, DN_CHUNK, 2 * DN_HEADS), (0, 1, 2, 4, 3))
    return col, row


def kernel(x, c, ctx, c_ctx, w_mod, b_mod, norm_pre_mix, norm_post_mix, norm_pre_ffn, norm_post_ffn, w_in, sg_ln_g, sg_ln_b, sg_w, sg_b, dn_conv_w, dn_a_log, dn_dt_bias, dn_norm_g, at_sinks, w_proj_sg, w_proj_dn, w_proj_at, w_out, router_w, router_b, exp_w_gu, exp_b_gu, exp_w_down, exp_b_down):
    B, S, D = x.shape
    L = ctx.shape[1]
    depth = w_mod.shape[0]
    assert D == D_MODEL and S % GRID_W == 0
    n_lat, n_ctx = B * S, B * L

    rows = (B + 1 + 7) // 8 * 8
    cvec = jnp.zeros((rows, D), F32).at[:B].set(c).at[B].set(c_ctx)
    mod_all = _modulation(cvec, w_mod, b_mod)
    tables = _rope_tables(S)

    lat_row = lambda t: t // S
    ctx_row = lambda t: B
    all_row = lambda t: jnp.where(t < n_lat, t // S, B)

    xl = x.reshape(n_lat, D)
    xc = ctx.reshape(n_ctx, D)
    for l in range(depth):
        need_ctx_out = l < depth - 1
        mod = mod_all[l].reshape(rows, 1, 6 * D)
        w_main, w_small = _split_w_in(w_in[l])
        lw = {
            "dn_norm_g": dn_norm_g[l].reshape(1, -1),
            "norm_post_mix": norm_post_mix[l].reshape(1, -1),
            "norm_pre_ffn": norm_pre_ffn[l].reshape(1, -1),
            "norm_post_ffn": norm_post_ffn[l].reshape(1, -1),
            "w_proj_sg": w_proj_sg[l].astype(BF16), "w_proj_dn": w_proj_dn[l].astype(BF16),
            "w_proj_at": w_proj_at[l].astype(BF16), "w_out": w_out[l].astype(BF16),
            "router_w": jnp.pad(router_w[l], ((0, 0), (0, LANES - N_EXPERTS))).astype(BF16),
            "router_b": jnp.pad(router_b[l], (0, LANES - N_EXPERTS),
                                constant_values=NEG_BIG).reshape(1, -1),
            "exp_w_gu": exp_w_gu[l].astype(BF16), "exp_b_gu": exp_b_gu[l].reshape(N_EXPERTS, 1, -1),
            "exp_w_down": exp_w_down[l].astype(BF16), "exp_b_down": exp_b_down[l].reshape(N_EXPERTS, 1, -1),
        }
        pre_g = norm_pre_mix[l].reshape(1, -1)
        main, small = _inproj(xl, mod, lat_row, pre_g, w_main, w_small, min(1024, S))
        w_main_c = w_main if need_ctx_out else w_main[:, :N_CTX_MAIN_COLS]
        main_c, small_c = _inproj(xc, mod, ctx_row, pre_g, w_main_c, w_small, min(1024, n_ctx))

        sg_args = (sg_ln_g[l].reshape(1, -1), sg_ln_b[l].reshape(1, -1), sg_w[l].astype(BF16),
                   sg_b[l].T)
        ysg = _sgu(main, *sg_args)

        gcol_c, grow_c = _dn_gates(small_c, B, L)
        gcol, grow = _dn_gates(small, B, S)
        s0 = jnp.zeros((2, B, DN_HEADS, DN_DIM, DN_DIM), F32)
        of_c, ob_c, s_ctx = _deltanet(main_c.reshape(B, L, -1), gcol_c, grow_c, dn_conv_w[l], dn_a_log[l],
                                      dn_dt_bias[l], s0, need_ctx_out)
        of_l, ob_l, _ = _deltanet(main.reshape(B, S, -1), gcol, grow, dn_conv_w[l], dn_a_log[l],
                                  dn_dt_bias[l], s_ctx, True)

        sinks = jnp.repeat(at_sinks[l], AT_BLOCK).reshape(-1, 1)
        q_r, k_r, v_r = _rope(main, small, tables, S)
        yat = _attention_local(q_r, k_r, v_r, small_c, sinks, B, S, L)

        x_mid, h2, logits = _merge(ysg, of_l.reshape(n_lat, D), ob_l.reshape(n_lat, D), main, yat, xl, mod,
                                   lat_row, lw)
        if need_ctx_out:
            ysg_c = _sgu(main_c, *sg_args)
            yat_c = _attention_ctx(main_c, small_c, sinks, B, L)
            xc_mid, h2c, logits_c = _merge(ysg_c, of_c.reshape(n_ctx, D), ob_c.reshape(n_ctx, D), main_c,
                                           yat_c, xc, mod, ctx_row, lw)
            x_mid = jnp.concatenate([x_mid, xc_mid], axis=0)
            h2 = jnp.concatenate([h2, h2c], axis=0)
            logits = jnp.concatenate([logits, logits_c], axis=0)
            xo = _moe(h2, logits, x_mid, mod, all_row, lw)
            xl, xc = xo[:n_lat], xo[n_lat:]
        else:
            xl = _moe(h2, logits, x_mid, mod, lat_row, lw)
    return xl.reshape(B, S, D)
```

```python
import functools
import math

import jax
import jax.numpy as jnp
from jax import lax
from jax.experimental import pallas as pl
from jax.experimental.pallas import tpu as pltpu

F32 = jnp.float32
BF16 = jnp.bfloat16
I32 = jnp.int32

EPS = 1e-6
D_MODEL = 1024
GRID_W = 64

SG_CHUNK = 128
SG_GROUPS = 8

DN_HEADS = 8
DN_DIM = 128
DN_CONV = 5
DN_CHUNK = 64
DN_HALO = 16

AT_Q_HEADS = 16
AT_KV_HEADS = 2
AT_DIM = 64
AT_BLOCK = 128
ROPE_BASE = 10000.0

N_EXPERTS = 32
TOP_K = 4
D_EXPERT = 1024
SWIGLU_ALPHA = 1.702
SWIGLU_LIMIT = 7.0
N_BRANCH = 3

LANES = 128
NEG_BIG = -1e30

COL_DN_K, COL_DN_V, COL_DN_Q, COL_DN_G, COL_AT_Q, COL_SG_U, COL_SG_V, COL_GATE0 = range(8)
N_MAIN_COLS = 10 * D_MODEL
N_CTX_MAIN_COLS = 2 * D_MODEL
N_SMALL_COLS = 3 * LANES

VMEM_LIMIT = 52 * 1024 * 1024


def _cparams(sem):
    return pltpu.CompilerParams(dimension_semantics=sem, vmem_limit_bytes=VMEM_LIMIT)


def _dot(a, b):
    return jnp.dot(a, b, preferred_element_type=F32)


def _dot_nt(a, b):
    return lax.dot_general(a, b, (((1,), (1,)), ((), ())), preferred_element_type=F32)


def _dot_tn(a, b):
    return lax.dot_general(a, b, (((0,), (0,)), ((), ())), preferred_element_type=F32)


def _sigmoid(x):
    return 1.0 / (1.0 + jnp.exp(-x))


def _silu(x):
    return x * _sigmoid(x)


def _gelu_tanh(x):
    return 0.5 * x * (1.0 + jnp.tanh(math.sqrt(2.0 / math.pi) * (x + 0.044715 * (x * x * x))))


def _softplus(x):
    return jnp.maximum(x, 0.0) + jnp.log(1.0 + jnp.exp(-jnp.abs(x)))


def _rms(x, g):
    return x * lax.rsqrt(jnp.mean(x * x, axis=-1, keepdims=True) + EPS) * g


def _mod_kernel(c_ref, w_ref, b_ref, o_ref):
    s = _silu(c_ref[...])
    o_ref[...] = jnp.dot(s, w_ref[...], preferred_element_type=F32,
                         precision=lax.Precision.HIGHEST) + b_ref[...]


def _modulation(cvec, w_mod, b_mod):
    depth = w_mod.shape[0]
    rows = cvec.shape[0]
    n_col = w_mod.shape[2] // D_MODEL
    return pl.pallas_call(
        _mod_kernel,
        grid=(depth, n_col),
        in_specs=[pl.BlockSpec((rows, D_MODEL), lambda l, j: (0, 0)),
                  pl.BlockSpec((None, D_MODEL, D_MODEL), lambda l, j: (l, 0, j)),
                  pl.BlockSpec((None, 1, D_MODEL), lambda l, j: (l, 0, j))],
        out_specs=pl.BlockSpec((None, rows, D_MODEL), lambda l, j: (l, 0, j)),
        out_shape=jax.ShapeDtypeStruct((depth, rows, w_mod.shape[2]), F32),
        compiler_params=_cparams(("arbitrary", "arbitrary")),
        name="modulation",
    )(cvec, w_mod, b_mod.reshape(depth, 1, -1))


def _inproj_kernel(x_ref, mod_ref, g_ref, wm_ref, ws_ref, main_ref, small_ref, h_ref):
    @pl.when(pl.program_id(1) == 0)
    def _():
        sh = mod_ref[:, 0 * D_MODEL:1 * D_MODEL]
        sc = mod_ref[:, 1 * D_MODEL:2 * D_MODEL]
        h = (_rms(x_ref[...], g_ref[...]) * (1.0 + sc) + sh).astype(BF16)
        h_ref[...] = h
        small_ref[...] = _dot(h, ws_ref[...])

    main_ref[...] = _dot(h_ref[...], wm_ref[...]).astype(BF16)


def _inproj(x, mod, mod_row, norm_g, w_main, w_small, tm, tn=1024):
    n_tok = x.shape[0]
    n_main = w_main.shape[1]
    return pl.pallas_call(
        _inproj_kernel,
        grid=(n_tok // tm, n_main // tn),
        in_specs=[pl.BlockSpec((tm, D_MODEL), lambda i, j: (i, 0)),
                  pl.BlockSpec((None, 1, 6 * D_MODEL), lambda i, j: (mod_row(i * tm), 0, 0)),
                  pl.BlockSpec((1, D_MODEL), lambda i, j: (0, 0)),
                  pl.BlockSpec((D_MODEL, tn), lambda i, j: (0, j)),
                  pl.BlockSpec((D_MODEL, N_SMALL_COLS), lambda i, j: (0, 0))],
        out_specs=[pl.BlockSpec((tm, tn), lambda i, j: (i, j)),
                   pl.BlockSpec((tm, N_SMALL_COLS), lambda i, j: (i, 0))],
        out_shape=[jax.ShapeDtypeStruct((n_tok, n_main), BF16),
                   jax.ShapeDtypeStruct((n_tok, N_SMALL_COLS), F32)],
        scratch_shapes=[pltpu.VMEM((tm, D_MODEL), BF16)],
        compiler_params=_cparams(("arbitrary", "arbitrary")),
        name="inproj",
    )(x, mod, norm_g, w_main, w_small)


def _sgu_kernel(u_ref, v_ref, lng_ref, lnb_ref, ws_ref, bs_ref, o_ref, *, n_chunk):
    u = _gelu_tanh(u_ref[...].astype(F32))
    v = _gelu_tanh(v_ref[...].astype(F32))
    vc = v - jnp.mean(v, axis=-1, keepdims=True)
    var = jnp.mean(vc * vc, axis=-1, keepdims=True)
    vn = (vc * lax.rsqrt(var + EPS) * lng_ref[...] + lnb_ref[...]).astype(BF16)
    for n in range(n_chunk):
        rows = slice(n * SG_CHUNK, (n + 1) * SG_CHUNK)
        for g in range(SG_GROUPS):
            cols = slice(g * LANES, (g + 1) * LANES)
            mixed = _dot(ws_ref[g], vn[rows, cols]) + bs_ref[:, g:g + 1]
            o_ref[rows, cols] = (u[rows, cols] * mixed).astype(BF16)


def _sgu(main, sg_ln_g, sg_ln_b, sg_w, sg_bt, n_chunk=2):
    n_tok = main.shape[0]
    tc = n_chunk * SG_CHUNK
    return pl.pallas_call(
        functools.partial(_sgu_kernel, n_chunk=n_chunk),
        grid=(n_tok // tc,),
        in_specs=[pl.BlockSpec((tc, D_MODEL), lambda i: (i, COL_SG_U)),
                  pl.BlockSpec((tc, D_MODEL), lambda i: (i, COL_SG_V)),
                  pl.BlockSpec((1, D_MODEL), lambda i: (0, 0)),
                  pl.BlockSpec((1, D_MODEL), lambda i: (0, 0)),
                  pl.BlockSpec((SG_GROUPS, SG_CHUNK, SG_CHUNK), lambda i: (0, 0, 0)),
                  pl.BlockSpec((SG_CHUNK, SG_GROUPS), lambda i: (0, 0))],
        out_specs=pl.BlockSpec((tc, D_MODEL), lambda i: (i, 0)),
        out_shape=jax.ShapeDtypeStruct((n_tok, D_MODEL), BF16),
        compiler_params=_cparams(("arbitrary",)),
        name="sgu",
    )(main, main, sg_ln_g, sg_ln_b, sg_w, sg_bt)


def _dn_kernel(*refs, with_q, n_chunks):
    s_refs = refs[-DN_HEADS:]
    ext_refs = refs[-DN_HEADS - 3:-DN_HEADS]
    refs = refs[:-DN_HEADS - 3]
    if with_q:
        (qp_ref, qc_ref, qn_ref, kp_ref, kc_ref, kn_ref, vp_ref, vc_ref, vn_ref,
         gcol_ref, grow_ref, cw_ref, alog_r_ref, alog_c_ref, dtb_r_ref, dtb_c_ref, s0_ref,
         o_ref, sfin_ref) = refs
    else:
        (kp_ref, kc_ref, kn_ref, vp_ref, vc_ref, vn_ref,
         gcol_ref, grow_ref, cw_ref, alog_r_ref, alog_c_ref, dtb_r_ref, dtb_c_ref, s0_ref,
         sfin_ref) = refs
    d = pl.program_id(0)
    c = pl.program_id(2)
    is_fwd = d == 0
    cidx = jnp.where(is_fwd, c, n_chunks - 1 - c)
    C = DN_CHUNK

    @pl.when(c == 0)
    def _():
        for h in range(DN_HEADS):
            s_refs[h][...] = s0_ref[h]

    has_prev = (cidx > 0).astype(F32)
    has_next = (cidx < n_chunks - 1).astype(F32)

    def conv_silu(p_ref, c_ref, n_ref, part):
        ext_ref = ext_refs[part]
        ext_ref[0:DN_HALO, :] = p_ref[...].astype(F32) * has_prev
        ext_ref[DN_HALO:DN_HALO + C, :] = c_ref[...].astype(F32)
        ext_ref[DN_HALO + C:2 * DN_HALO + C, :] = n_ref[...].astype(F32) * has_next
        base = DN_HALO - DN_CONV // 2
        y = None
        for i in range(DN_CONV):
            w = cw_ref[i:i + 1, part * D_MODEL:(part + 1) * D_MODEL]
            t = ext_ref[base + i:base + i + C, :] * w
            y = t if y is None else y + t
        return _silu(y)

    k_all = conv_silu(kp_ref, kc_ref, kn_ref, 1)
    v_all = conv_silu(vp_ref, vc_ref, vn_ref, 2)
    q_all = conv_silu(qp_ref, qc_ref, qn_ref, 0) if with_q else None

    gcol = gcol_ref[...]
    ld_col = -jnp.exp(alog_r_ref[...]) * _softplus(gcol[:, 0:DN_HEADS] + dtb_r_ref[...])
    beta_col = _sigmoid(gcol[:, DN_HEADS:2 * DN_HEADS])
    ld_row = -jnp.exp(alog_c_ref[...]) * _softplus(grow_ref[0:DN_HEADS, :] + dtb_c_ref[...])

    ri = lax.broadcasted_iota(I32, (C, C), 0)
    ci = lax.broadcasted_iota(I32, (C, C), 1)
    delta = (ri - ci) * (1 - 2 * d)
    incl = delta >= 0
    strict = delta > 0
    incl_t = delta <= 0
    gam_col = jnp.dot(incl.astype(F32), ld_col, preferred_element_type=F32,
                      precision=lax.Precision.HIGHEST)
    gam_row = jnp.dot(ld_row, incl_t.astype(F32), preferred_element_type=F32,
                      precision=lax.Precision.HIGHEST)
    gam_tot = jnp.sum(ld_col, axis=0, keepdims=True)
    eye = (ri == ci).astype(F32)

    H = range(DN_HEADS)
    lanes = [slice(h * DN_DIM, (h + 1) * DN_DIM) for h in H]
    gc = [gam_col[:, h:h + 1] for h in H]
    bc = [beta_col[:, h:h + 1] for h in H]
    kh = [k_all[:, lanes[h]] for h in H]
    kh = [kh[h] * lax.rsqrt(jnp.sum(kh[h] * kh[h], axis=-1, keepdims=True) + EPS) for h in H]
    kb = [kh[h].astype(BF16) for h in H]
    decay = [jnp.exp(jnp.where(incl, gc[h] - gam_row[h:h + 1, :], NEG_BIG)) for h in H]
    kk = [_dot_nt(kb[h], kb[h]) for h in H]
    x = [-(jnp.where(strict, decay[h], 0.0) * bc[h] * kk[h]) for h in H]
    p = [eye + x[h] for h in H]
    xb = [x[h].astype(BF16) for h in H]
    x = [_dot(xb[h], xb[h]) for h in H]
    n_fac = int(math.log2(C)) - 1
    for j in range(n_fac):
        xb = [x[h].astype(BF16) for h in H]
        if j < n_fac - 1:
            r = [_dot(xb[h], jnp.concatenate([xb[h], p[h].astype(BF16)], axis=1)) for h in H]
            x = [r[h][:, :C] for h in H]
            p = [p[h] + r[h][:, C:] for h in H]
        else:
            p = [p[h] + _dot(xb[h], p[h].astype(BF16)) for h in H]
    rhs = [jnp.concatenate([kh[h] * (bc[h] * jnp.exp(gc[h])), v_all[:, lanes[h]] * bc[h]],
                           axis=1).astype(BF16) for h in H]
    sol = [_dot(p[h].astype(BF16), rhs[h]) for h in H]
    w = [sol[h][:, :DN_DIM] for h in H]
    u0 = [sol[h][:, DN_DIM:] for h in H]
    k_end = [(kh[h] * jnp.exp(gam_tot[:, h:h + 1] - gc[h])).astype(BF16) for h in H]
    s = [s_refs[h][...] for h in H]
    sb = [s[h].astype(BF16) for h in H]
    if with_q:
        qh = [q_all[:, lanes[h]] for h in H]
        qh = [qh[h] * (lax.rsqrt(jnp.sum(qh[h] * qh[h], axis=-1, keepdims=True) + EPS) * DN_DIM ** -0.5)
              for h in H]
        qk = [(_dot_nt(qh[h].astype(BF16), kb[h]) * decay[h]).astype(BF16) for h in H]
        wq = [jnp.concatenate([w[h], qh[h] * jnp.exp(gc[h])], axis=0).astype(BF16) for h in H]
        ws = [_dot(wq[h], sb[h]) for h in H]
        ub = [(u0[h] - ws[h][:C]).astype(BF16) for h in H]
        qu = [_dot(qk[h], ub[h]) for h in H]
        for h in H:
            o_ref[:, lanes[h]] = (ws[h][C:] + qu[h]).astype(o_ref.dtype)
    else:
        ws = [_dot(w[h].astype(BF16), sb[h]) for h in H]
        ub = [(u0[h] - ws[h]).astype(BF16) for h in H]
    ku = [_dot_tn(k_end[h], ub[h]) for h in H]
    for h in H:
        s_refs[h][...] = jnp.exp(gam_tot[:, h:h + 1]) * s[h] + ku[h]

    @pl.when(c == n_chunks - 1)
    def _():
        for h in range(DN_HEADS):
            sfin_ref[h] = s_refs[h][...]


def _deltanet_single_pass(main3, gate_col, gate_row, conv_w, alog, dtb, s0, with_q):
    B, T, _ = main3.shape
    C = DN_CHUNK
    n_chunks = T // C
    hpc = C // DN_HALO
    n_halo = T // DN_HALO

    def cix(d, c):
        return jnp.where(d == 0, c, n_chunks - 1 - c)

    def trio(col):
        return [pl.BlockSpec((None, DN_HALO, D_MODEL),
                             lambda d, b, c: (b, jnp.maximum(cix(d, c) * hpc - 1, 0), col)),
                pl.BlockSpec((None, C, D_MODEL), lambda d, b, c: (b, cix(d, c), col)),
                pl.BlockSpec((None, DN_HALO, D_MODEL),
                             lambda d, b, c: (b, jnp.minimum((cix(d, c) + 1) * hpc, n_halo - 1), col))]

    in_specs = (trio(COL_DN_Q) if with_q else []) + trio(COL_DN_K) + trio(COL_DN_V) + [
        pl.BlockSpec((None, None, C, 2 * DN_HEADS), lambda d, b, c: (d, b, cix(d, c), 0)),
        pl.BlockSpec((None, None, None, 2 * DN_HEADS, C), lambda d, b, c: (d, b, cix(d, c), 0, 0)),
        pl.BlockSpec((DN_CONV, 3 * D_MODEL), lambda d, b, c: (0, 0)),
        pl.BlockSpec((None, 1, DN_HEADS), lambda d, b, c: (d, 0, 0)),
        pl.BlockSpec((None, DN_HEADS, 1), lambda d, b, c: (d, 0, 0)),
        pl.BlockSpec((None, 1, DN_HEADS), lambda d, b, c: (d, 0, 0)),
        pl.BlockSpec((None, DN_HEADS, 1), lambda d, b, c: (d, 0, 0)),
        pl.BlockSpec((None, None, DN_HEADS, DN_DIM, DN_DIM), lambda d, b, c: (d, b, 0, 0, 0)),
    ]
    s_spec = pl.BlockSpec((None, None, DN_HEADS, DN_DIM, DN_DIM), lambda d, b, c: (d, b, 0, 0, 0))
    s_shape = jax.ShapeDtypeStruct((2, B, DN_HEADS, DN_DIM, DN_DIM), F32)
    if with_q:
        out_specs = [pl.BlockSpec((None, None, C, D_MODEL), lambda d, b, c: (d, b, cix(d, c), 0)), s_spec]
        out_shape = [jax.ShapeDtypeStruct((2, B, T, D_MODEL), BF16), s_shape]
    else:
        out_specs = [s_spec]
        out_shape = [s_shape]
    n_main = 3 if with_q else 2
    args = [main3] * (3 * n_main) + [
        gate_col, gate_row, conv_w,
        alog.reshape(2, 1, DN_HEADS), alog.reshape(2, DN_HEADS, 1),
        dtb.reshape(2, 1, DN_HEADS), dtb.reshape(2, DN_HEADS, 1), s0]
    out = pl.pallas_call(
        functools.partial(_dn_kernel, with_q=with_q, n_chunks=n_chunks),
        grid=(2, B, n_chunks),
        in_specs=in_specs, out_specs=out_specs, out_shape=out_shape,
        scratch_shapes=[pltpu.VMEM((C + 2 * DN_HALO, D_MODEL), F32)] * 3
        + [pltpu.VMEM((DN_DIM, DN_DIM), F32)] * DN_HEADS,
        compiler_params=_cparams(("arbitrary", "arbitrary", "arbitrary")),
        name="deltanet_q" if with_q else "deltanet_state",
    )(*args)
    return (out[0], out[1]) if with_q else (None, out[0])


def _dn_prep_kernel(*refs, with_q, n_chunks):
    if with_q:
        (qp_ref, qc_ref, qn_ref, kp_ref, kc_ref, kn_ref, vp_ref, vc_ref, vn_ref,
         gcol_ref, grow_ref, cw_ref, alog_r_ref, alog_c_ref, dtb_r_ref, dtb_c_ref,
         w_ref, u0_ref, ke_ref, gt_ref, qs_ref, qk_ref) = refs
    else:
        (kp_ref, kc_ref, kn_ref, vp_ref, vc_ref, vn_ref,
         gcol_ref, grow_ref, cw_ref, alog_r_ref, alog_c_ref, dtb_r_ref, dtb_c_ref,
         w_ref, u0_ref, ke_ref, gt_ref) = refs
    c = pl.program_id(1)
    C = DN_CHUNK
    has_prev = (c > 0).astype(BF16)
    has_next = (c < n_chunks - 1).astype(BF16)

    pad = DN_CONV // 2
    n_sh = DN_CONV - 1
    sr = lax.broadcasted_iota(I32, (n_sh * C, C + 2 * DN_HALO), 0)
    sc = lax.broadcasted_iota(I32, (n_sh * C, C + 2 * DN_HALO), 1)
    blk = sr // C
    off = jnp.where(blk < pad, blk - pad, blk - pad + 1)
    shift_mat = (sc == DN_HALO + (sr - blk * C) + off).astype(BF16)

    def conv_silu(p_ref, c_ref, n_ref, part):
        cur = c_ref[...]
        ext = jnp.concatenate([p_ref[...] * has_prev, cur, n_ref[...] * has_next], axis=0)
        sh = _dot(shift_mat, ext)
        taps = [sh[j * C:(j + 1) * C] for j in range(pad)] + [cur.astype(F32)] + \
               [sh[j * C:(j + 1) * C] for j in range(pad, n_sh)]
        y = None
        for i in range(DN_CONV):
            t = taps[i] * cw_ref[i:i + 1, part * D_MODEL:(part + 1) * D_MODEL]
            y = t if y is None else y + t
        return _silu(y)

    k_all = conv_silu(kp_ref, kc_ref, kn_ref, 1)
    v_all = conv_silu(vp_ref, vc_ref, vn_ref, 2)
    q_all = conv_silu(qp_ref, qc_ref, qn_ref, 0) if with_q else None

    ri = lax.broadcasted_iota(I32, (C, C), 0)
    ci = lax.broadcasted_iota(I32, (C, C), 1)
    eye = (ri == ci).astype(F32)
    H = range(DN_HEADS)
    lanes = [slice(h * DN_DIM, (h + 1) * DN_DIM) for h in H]
    kh = [k_all[:, lanes[h]] for h in H]
    kh = [kh[h] * lax.rsqrt(jnp.sum(kh[h] * kh[h], axis=-1, keepdims=True) + EPS) for h in H]
    kb = [kh[h].astype(BF16) for h in H]
    if with_q:
        qh = [q_all[:, lanes[h]] for h in H]
        qh = [qh[h] * (lax.rsqrt(jnp.sum(qh[h] * qh[h], axis=-1, keepdims=True) + EPS) * DN_DIM ** -0.5)
              for h in H]
        gram = [_dot_nt(jnp.concatenate([kb[h], qh[h].astype(BF16)], axis=0), kb[h]) for h in H]
        kk = [gram[h][:C] for h in H]
        qk_raw = [gram[h][C:] for h in H]
    else:
        kk = [_dot_nt(kb[h], kb[h]) for h in H]

    D2 = range(2)
    DH = [(d, h) for d in D2 for h in H]
    incl, strict, gam_col, gam_row, gam_tot, beta_col = [], [], [], [], [], []
    for d in D2:
        inc = (ri >= ci) if d == 0 else (ri <= ci)
        inc_t = (ri <= ci) if d == 0 else (ri >= ci)
        gcol = gcol_ref[d]
        ld_col = -jnp.exp(alog_r_ref[d]) * _softplus(gcol[:, 0:DN_HEADS] + dtb_r_ref[d])
        ld_row = -jnp.exp(alog_c_ref[d]) * _softplus(grow_ref[d][0:DN_HEADS, :] + dtb_c_ref[d])
        incl.append(inc)
        strict.append((ri > ci) if d == 0 else (ri < ci))
        beta_col.append(_sigmoid(gcol[:, DN_HEADS:2 * DN_HEADS]))
        gam_col.append(jnp.dot(inc.astype(F32), ld_col, preferred_element_type=F32,
                               precision=lax.Precision.HIGHEST))
        gam_row.append(jnp.dot(ld_row, inc_t.astype(F32), preferred_element_type=F32,
                               precision=lax.Precision.HIGHEST))
        tot = jnp.sum(ld_col, axis=0, keepdims=True)
        gam_tot.append(tot)
        gt_ref[d] = tot
    gc = [gam_col[d][:, h:h + 1] for d, h in DH]
    bc = [beta_col[d][:, h:h + 1] for d, h in DH]
    decay = [jnp.exp(jnp.where(incl[d], gc[i] - gam_row[d][h:h + 1, :], NEG_BIG))
             for i, (d, h) in enumerate(DH)]
    x = [-(jnp.where(strict[d], decay[i], 0.0) * bc[i] * kk[h]) for i, (d, h) in enumerate(DH)]
    N = range(len(DH))
    p = [eye + x[i] for i in N]
    xb = [x[i].astype(BF16) for i in N]
    x = [_dot(xb[i], xb[i]) for i in N]
    n_fac = int(math.log2(C)) - 1
    for j in range(n_fac):
        xb = [x[i].astype(BF16) for i in N]
        if j < n_fac - 1:
            r = [_dot(xb[i], jnp.concatenate([xb[i], p[i].astype(BF16)], axis=1)) for i in N]
            x = [r[i][:, :C] for i in N]
            p = [p[i] + r[i][:, C:] for i in N]
        else:
            p = [p[i] + _dot(xb[i], p[i].astype(BF16)) for i in N]
    rhs = [jnp.concatenate([kh[h] * (bc[i] * jnp.exp(gc[i])), v_all[:, lanes[h]] * bc[i]],
                           axis=1).astype(BF16) for i, (d, h) in enumerate(DH)]
    sol = [_dot(p[i].astype(BF16), rhs[i]) for i in N]
    for i, (d, h) in enumerate(DH):
        w_ref[d, :, lanes[h]] = sol[i][:, :DN_DIM].astype(BF16)
        u0_ref[d, :, lanes[h]] = sol[i][:, DN_DIM:].astype(BF16)
        ke_ref[d, :, lanes[h]] = (kh[h] * jnp.exp(gam_tot[d][:, h:h + 1] - gc[i])).astype(BF16)
        if with_q:
            qs_ref[d, :, lanes[h]] = (qh[h] * jnp.exp(gc[i])).astype(BF16)
            qk_ref[d, :, h * C:(h + 1) * C] = (qk_raw[h] * decay[i]).astype(BF16)


def _dn_scan_kernel(*refs, with_q, n_chunks):
    n_state = 2 * DN_HEADS
    s_refs = refs[-n_state:]
    refs = refs[:-n_state]
    if with_q:
        (w0, w1, u0, u1, k0, k1, g0, g1, qs0, qs1, qk0, qk1, s0_ref, o0_ref, o1_ref, sfin_ref) = refs
        qs_r, qk_r, o_r = (qs0, qs1), (qk0, qk1), (o0_ref, o1_ref)
    else:
        (w0, w1, u0, u1, k0, k1, g0, g1, s0_ref, sfin_ref) = refs
    w_r, u_r, k_r, g_r = (w0, w1), (u0, u1), (k0, k1), (g0, g1)
    c = pl.program_id(1)
    C = DN_CHUNK
    DH = [(d, h) for d in range(2) for h in range(DN_HEADS)]
    N = range(len(DH))
    lanes = [slice(h * DN_DIM, (h + 1) * DN_DIM) for h in range(DN_HEADS)]

    @pl.when(c == 0)
    def _():
        for i, (d, h) in enumerate(DH):
            s_refs[i][...] = s0_ref[d, h]

    s = [s_refs[i][...] for i in N]
    sb = [s[i].astype(BF16) for i in N]
    w = [w_r[d][:, lanes[h]] for d, h in DH]
    if with_q:
        wq = [jnp.concatenate([w[i], qs_r[d][:, lanes[h]]], axis=0) for i, (d, h) in enumerate(DH)]
        ws = [_dot(wq[i], sb[i]) for i in N]
        ub = [(u_r[d][:, lanes[h]].astype(F32) - ws[i][:C]).astype(BF16) for i, (d, h) in enumerate(DH)]
        qu = [_dot(qk_r[d][:, h * C:(h + 1) * C], ub[i]) for i, (d, h) in enumerate(DH)]
        for i, (d, h) in enumerate(DH):
            o_r[d][:, lanes[h]] = (ws[i][C:] + qu[i]).astype(BF16)
    else:
        ws = [_dot(w[i], sb[i]) for i in N]
        ub = [(u_r[d][:, lanes[h]].astype(F32) - ws[i]).astype(BF16) for i, (d, h) in enumerate(DH)]
    ku = [_dot_tn(k_r[d][:, lanes[h]], ub[i]) for i, (d, h) in enumerate(DH)]
    for i, (d, h) in enumerate(DH):
        s_refs[i][...] = jnp.exp(g_r[d][:, h:h + 1]) * s[i] + ku[i]

    @pl.when(c == n_chunks - 1)
    def _():
        for i, (d, h) in enumerate(DH):
            sfin_ref[d, h] = s_refs[i][...]


def _deltanet(main3, gate_col, gate_row, conv_w, alog, dtb, s0, with_q):
    B, T, _ = main3.shape
    C = DN_CHUNK
    n_chunks = T // C
    hpc = C // DN_HALO
    n_halo = T // DN_HALO

    def trio(col):
        return [pl.BlockSpec((None, DN_HALO, D_MODEL), lambda b, c: (b, jnp.maximum(c * hpc - 1, 0), col)),
                pl.BlockSpec((None, C, D_MODEL), lambda b, c: (b, c, col)),
                pl.BlockSpec((None, DN_HALO, D_MODEL),
                             lambda b, c: (b, jnp.minimum((c + 1) * hpc, n_halo - 1), col))]

    vec = lambda shape: pl.BlockSpec(shape, lambda b, c: (0,) * len(shape))
    in_specs = (trio(COL_DN_Q) if with_q else []) + trio(COL_DN_K) + trio(COL_DN_V) + [
        pl.BlockSpec((2, None, C, 2 * DN_HEADS), lambda b, c: (0, b, c, 0)),
        pl.BlockSpec((2, None, None, 2 * DN_HEADS, C), lambda b, c: (0, b, c, 0, 0)),
        vec((DN_CONV, 3 * D_MODEL)),
        vec((2, 1, DN_HEADS)), vec((2, DN_HEADS, 1)), vec((2, 1, DN_HEADS)), vec((2, DN_HEADS, 1)),
    ]
    wide = lambda n: (pl.BlockSpec((2, None, C, n), lambda b, c: (0, b, c, 0)),
                      jax.ShapeDtypeStruct((2, B, T, n), BF16))
    outs = [wide(D_MODEL), wide(D_MODEL), wide(D_MODEL),
            (pl.BlockSpec((2, None, None, 1, DN_HEADS), lambda b, c: (0, b, c, 0, 0)),
             jax.ShapeDtypeStruct((2, B, n_chunks, 1, DN_HEADS), F32))]
    if with_q:
        outs += [wide(D_MODEL), wide(DN_HEADS * C)]
    n_main = 3 if with_q else 2
    prep = pl.pallas_call(
        functools.partial(_dn_prep_kernel, with_q=with_q, n_chunks=n_chunks),
        grid=(B, n_chunks),
        in_specs=in_specs, out_specs=[o[0] for o in outs], out_shape=[o[1] for o in outs],
        compiler_params=_cparams(("arbitrary", "arbitrary")),
        name="dn_prep_q" if with_q else "dn_prep",
    )(*([main3] * (3 * n_main)), gate_col, gate_row, conv_w,
      alog.reshape(2, 1, DN_HEADS), alog.reshape(2, DN_HEADS, 1),
      dtb.reshape(2, 1, DN_HEADS), dtb.reshape(2, DN_HEADS, 1))

    def both_dirs(arr, n):
        if n is None:
            return [pl.BlockSpec((None, None, None, 1, DN_HEADS), lambda b, c: (0, b, c, 0, 0)),
                    pl.BlockSpec((None, None, None, 1, DN_HEADS),
                                 lambda b, c: (1, b, n_chunks - 1 - c, 0, 0))], [arr, arr]
        return [pl.BlockSpec((None, None, C, n), lambda b, c: (0, b, c, 0)),
                pl.BlockSpec((None, None, C, n), lambda b, c: (1, b, n_chunks - 1 - c, 0))], [arr, arr]

    specs, args = [], []
    widths = [D_MODEL, D_MODEL, D_MODEL, None] + ([D_MODEL, DN_HEADS * C] if with_q else [])
    for arr, n in zip(prep, widths):
        sp, ar = both_dirs(arr, n)
        specs += sp
        args += ar
    s_spec = pl.BlockSpec((2, None, DN_HEADS, DN_DIM, DN_DIM), lambda b, c: (0, b, 0, 0, 0))
    s_shape = jax.ShapeDtypeStruct((2, B, DN_HEADS, DN_DIM, DN_DIM), F32)
    if with_q:
        out_specs = [pl.BlockSpec((None, C, D_MODEL), lambda b, c: (b, c, 0)),
                     pl.BlockSpec((None, C, D_MODEL), lambda b, c: (b, n_chunks - 1 - c, 0)), s_spec]
        out_shape = [jax.ShapeDtypeStruct((B, T, D_MODEL), BF16)] * 2 + [s_shape]
    else:
        out_specs, out_shape = [s_spec], [s_shape]
    out = pl.pallas_call(
        functools.partial(_dn_scan_kernel, with_q=with_q, n_chunks=n_chunks),
        grid=(B, n_chunks),
        in_specs=specs + [s_spec], out_specs=out_specs, out_shape=out_shape,
        scratch_shapes=[pltpu.VMEM((DN_DIM, DN_DIM), F32)] * (2 * DN_HEADS),
        compiler_params=_cparams(("arbitrary", "arbitrary")),
        name="dn_scan_q" if with_q else "dn_scan",
    )(*args, s0)
    return (out[0], out[1], out[2]) if with_q else (None, None, out[0])


def _rope_tables(S):
    half = AT_DIM // 2
    nf = half // 2
    inv_freq = ROPE_BASE ** (-jnp.arange(nf, dtype=F32) / nf)
    t = jnp.arange(S, dtype=jnp.int32)
    row = (t // GRID_W).astype(F32)
    col = (t % GRID_W).astype(F32)
    lane = jnp.arange(LANES)
    dd = lane % AT_DIM
    pos = jnp.where((dd < half)[None, :], row[:, None], col[:, None])
    ang = pos * inv_freq[lane % nf][None, :]
    first = ((lane % half) < nf)[None, :]
    sin = jnp.sin(ang)
    return jnp.cos(ang), jnp.where(first, -sin, 0.0), jnp.where(first, 0.0, sin)


def _rope_kernel(q_ref, k_ref, v_ref, cos_ref, sa_ref, sb_ref, qo_ref, ko_ref, vo_ref):
    cos, sa, sb = cos_ref[...], sa_ref[...], sb_ref[...]
    nf = AT_DIM // 4

    def rot(x):
        return x * cos + pltpu.roll(x, LANES - nf, 1) * sa + pltpu.roll(x, nf, 1) * sb

    for j in range(AT_Q_HEADS * AT_DIM // LANES):
        lanes = slice(j * LANES, (j + 1) * LANES)
        qo_ref[:, lanes] = (rot(q_ref[:, lanes].astype(F32)) * AT_DIM ** -0.5).astype(BF16)
    ko_ref[...] = rot(k_ref[...]).astype(BF16)
    vo_ref[...] = v_ref[...].astype(BF16)


def _rope(main, small, tables, S, tm=512):
    n_tok = main.shape[0]
    per_seq = S // tm
    tab_spec = pl.BlockSpec((tm, LANES), lambda i: (i % per_seq, 0))
    return pl.pallas_call(
        _rope_kernel,
        grid=(n_tok // tm,),
        in_specs=[pl.BlockSpec((tm, D_MODEL), lambda i: (i, COL_AT_Q)),
                  pl.BlockSpec((tm, LANES), lambda i: (i, 0)),
                  pl.BlockSpec((tm, LANES), lambda i: (i, 1)),
                  tab_spec, tab_spec, tab_spec],
        out_specs=[pl.BlockSpec((tm, D_MODEL), lambda i: (i, 0)),
                   pl.BlockSpec((tm, LANES), lambda i: (i, 0)),
                   pl.BlockSpec((tm, LANES), lambda i: (i, 0))],
        out_shape=[jax.ShapeDtypeStruct((n_tok, D_MODEL), BF16),
                   jax.ShapeDtypeStruct((n_tok, LANES), BF16),
                   jax.ShapeDtypeStruct((n_tok, LANES), BF16)],
        compiler_params=_cparams(("arbitrary",)),
        name="rope",
    )(main, small, small, *tables)


def _attn_kernel(*refs, local, n_blocks, q_scale):
    if local:
        (q_ref, kp_ref, kc_ref, kn_ref, vp_ref, vc_ref, vn_ref, kx_ref, vx_ref, sink_ref, o_ref) = refs
    else:
        (q_ref, kx_ref, vx_ref, sink_ref, o_ref) = refs
    P = AT_BLOCK
    G = AT_Q_HEADS // AT_KV_HEADS
    L = kx_ref.shape[0]
    kx = kx_ref[...].astype(BF16)
    vx = vx_ref[...].astype(BF16)
    if local:
        i = pl.program_id(1)
        k_all = jnp.concatenate([kp_ref[...], kc_ref[...], kn_ref[...], kx], axis=0)
        v_all = jnp.concatenate([vp_ref[...], vc_ref[...], vn_ref[...], vx], axis=0)
        qi = lax.broadcasted_iota(I32, (P, P), 0)
        kj = lax.broadcasted_iota(I32, (P, P), 1)
        b_prev = jnp.where(kj >= qi, 0.0, NEG_BIG) + jnp.where(i > 0, 0.0, NEG_BIG)
        b_next = jnp.where(kj <= qi, 0.0, NEG_BIG) + jnp.where(i < n_blocks - 1, 0.0, NEG_BIG)
        bias = jnp.concatenate([b_prev, jnp.zeros((P, P), F32), b_next, jnp.zeros((P, L), F32)], axis=1)
    else:
        k_all, v_all, bias = kx, vx, None
    n_keys = k_all.shape[0]
    lo = lax.broadcasted_iota(I32, (P, LANES), 1) < AT_DIM
    qf = q_ref[...].astype(F32) * q_scale
    pieces = []
    for qh in range(AT_Q_HEADS):
        blk = qf[:, (qh // 2) * LANES:(qh // 2 + 1) * LANES]
        want_lo = qh // G == 0
        if want_lo != (qh % 2 == 0):
            blk = pltpu.roll(blk, AT_DIM, 1)
        pieces.append(jnp.where(lo if want_lo else ~lo, blk, 0.0).astype(BF16))
    qs = jnp.concatenate(pieces, axis=0)
    s = _dot_nt(qs, k_all)
    if bias is not None:
        s = (s.reshape(AT_Q_HEADS, P, n_keys) + bias[None]).reshape(AT_Q_HEADS * P, n_keys)
    sink = sink_ref[...]
    m = jnp.maximum(jnp.max(s, axis=-1, keepdims=True), sink)
    p = jnp.exp(s - m)
    den = jnp.sum(p, axis=-1, keepdims=True) + jnp.exp(sink - m)
    o = _dot(p.astype(BF16), v_all) / den
    for j in range(AT_Q_HEADS // 2):
        a = o[(2 * j) * P:(2 * j + 1) * P]
        b = o[(2 * j + 1) * P:(2 * j + 2) * P]
        if (2 * j) // G == 0:
            out = jnp.where(lo, a, pltpu.roll(b, AT_DIM, 1))
        else:
            out = jnp.where(lo, pltpu.roll(a, AT_DIM, 1), b)
        o_ref[:, j * LANES:(j + 1) * LANES] = out.astype(BF16)


def _attention_local(q_r, k_r, v_r, small_c, sinks, B, S, L):
    P = AT_BLOCK
    nb = S // P

    def kv_trio():
        return [pl.BlockSpec((P, LANES), lambda b, i: (b * nb + jnp.maximum(i - 1, 0), 0)),
                pl.BlockSpec((P, LANES), lambda b, i: (b * nb + i, 0)),
                pl.BlockSpec((P, LANES), lambda b, i: (b * nb + jnp.minimum(i + 1, nb - 1), 0))]

    return pl.pallas_call(
        functools.partial(_attn_kernel, local=True, n_blocks=nb, q_scale=1.0),
        grid=(B, nb),
        in_specs=[pl.BlockSpec((P, D_MODEL), lambda b, i: (b * nb + i, 0))] + kv_trio() + kv_trio() + [
            pl.BlockSpec((L, LANES), lambda b, i: (b, 0)),
            pl.BlockSpec((L, LANES), lambda b, i: (b, 1)),
            pl.BlockSpec((AT_Q_HEADS * AT_BLOCK, 1), lambda b, i: (0, 0))],
        out_specs=pl.BlockSpec((P, D_MODEL), lambda b, i: (b * nb + i, 0)),
        out_shape=jax.ShapeDtypeStruct((B * S, D_MODEL), BF16),
        compiler_params=_cparams(("arbitrary", "arbitrary")),
        name="attn_local",
    )(q_r, k_r, k_r, k_r, v_r, v_r, v_r, small_c, small_c, sinks)


def _attention_ctx(main_c, small_c, sinks, B, L):
    P = AT_BLOCK
    nb = L // P
    return pl.pallas_call(
        functools.partial(_attn_kernel, local=False, n_blocks=nb, q_scale=AT_DIM ** -0.5),
        grid=(B, nb),
        in_specs=[pl.BlockSpec((P, D_MODEL), lambda b, i: (b * nb + i, COL_AT_Q)),
                  pl.BlockSpec((L, LANES), lambda b, i: (b, 0)),
                  pl.BlockSpec((L, LANES), lambda b, i: (b, 1)),
                  pl.BlockSpec((AT_Q_HEADS * AT_BLOCK, 1), lambda b, i: (0, 0))],
        out_specs=pl.BlockSpec((P, D_MODEL), lambda b, i: (b * nb + i, 0)),
        out_shape=jax.ShapeDtypeStruct((B * L, D_MODEL), BF16),
        compiler_params=_cparams(("arbitrary", "arbitrary")),
        name="attn_ctx",
    )(main_c, small_c, small_c, sinks)


def _merge_kernel(ysg_ref, of_ref, ob_ref, dng_ref, yat_ref, g0_ref, g1_ref, g2_ref, x_ref, mod_ref,
                  dn_norm_ref, post_ref, pre_ref, wsg_ref, wdn_ref, wat_ref, wout_ref, rw_ref, rb_ref,
                  xo_ref, h2_ref, lg_ref):
    o = of_ref[...].astype(F32) + ob_ref[...].astype(F32)
    dn_g = dn_norm_ref[...]
    parts = []
    for h in range(DN_HEADS):
        lanes = slice(h * DN_DIM, (h + 1) * DN_DIM)
        parts.append(_rms(o[:, lanes], dn_g) * _silu(dng_ref[:, lanes].astype(F32)))
    ydn = jnp.concatenate(parts, axis=1).astype(BF16)
    m = (_sigmoid(g0_ref[...].astype(F32)) * _dot(ysg_ref[...], wsg_ref[...])
         + _sigmoid(g1_ref[...].astype(F32)) * _dot(ydn, wdn_ref[...])
         + _sigmoid(g2_ref[...].astype(F32)) * _dot(yat_ref[...], wat_ref[...]))
    y = _dot(m.astype(BF16), wout_ref[...])
    gate1 = mod_ref[:, 2 * D_MODEL:3 * D_MODEL]
    sh2 = mod_ref[:, 3 * D_MODEL:4 * D_MODEL]
    sc2 = mod_ref[:, 4 * D_MODEL:5 * D_MODEL]
    xn = x_ref[...] + gate1 * _rms(y, post_ref[...])
    xo_ref[...] = xn
    h2 = _rms(xn, pre_ref[...]) * (1.0 + sc2) + sh2
    h2_ref[...] = h2
    lg_ref[...] = _dot(h2.astype(BF16), rw_ref[...]) + rb_ref[...]


def _merge(ysg, o_fwd, o_bwd, main, yat, x, mod, mod_row, lw, tm=256):
    n_tok = x.shape[0]
    const = lambda i: (0, 0)
    wspec = pl.BlockSpec((D_MODEL, D_MODEL), const, pipeline_mode=pl.Buffered(1))
    vspec = pl.BlockSpec((1, D_MODEL), const)
    return pl.pallas_call(
        _merge_kernel,
        grid=(n_tok // tm,),
        in_specs=[pl.BlockSpec((tm, D_MODEL), lambda i: (i, 0)),
                  pl.BlockSpec((tm, D_MODEL), lambda i: (i, 0)),
                  pl.BlockSpec((tm, D_MODEL), lambda i: (i, 0)),
                  pl.BlockSpec((tm, D_MODEL), lambda i: (i, COL_DN_G)),
                  pl.BlockSpec((tm, D_MODEL), lambda i: (i, 0)),
                  pl.BlockSpec((tm, D_MODEL), lambda i: (i, COL_GATE0)),
                  pl.BlockSpec((tm, D_MODEL), lambda i: (i, COL_GATE0 + 1)),
                  pl.BlockSpec((tm, D_MODEL), lambda i: (i, COL_GATE0 + 2)),
                  pl.BlockSpec((tm, D_MODEL), lambda i: (i, 0)),
                  pl.BlockSpec((None, 1, 6 * D_MODEL), lambda i: (mod_row(i * tm), 0, 0)),
                  pl.BlockSpec((1, DN_DIM), const), vspec, vspec,
                  wspec, wspec, wspec, wspec,
                  pl.BlockSpec((D_MODEL, LANES), const), pl.BlockSpec((1, LANES), const)],
        out_specs=[pl.BlockSpec((tm, D_MODEL), lambda i: (i, 0)),
                   pl.BlockSpec((tm, D_MODEL), lambda i: (i, 0)),
                   pl.BlockSpec((tm, LANES), lambda i: (i, 0))],
        out_shape=[jax.ShapeDtypeStruct((n_tok, D_MODEL), F32),
                   jax.ShapeDtypeStruct((n_tok, D_MODEL), F32),
                   jax.ShapeDtypeStruct((n_tok, LANES), F32)],
        compiler_params=_cparams(("arbitrary",)),
        name="merge",
    )(ysg, o_fwd, o_bwd, main, yat, main, main, main, x, mod,
      lw["dn_norm_g"], lw["norm_post_mix"], lw["norm_pre_ffn"],
      lw["w_proj_sg"], lw["w_proj_dn"], lw["w_proj_at"], lw["w_out"], lw["router_w"], lw["router_b"])


MOE_TOK = 256
MOE_PIECE = 8
MOE_BUF = MOE_TOK * TOP_K + N_EXPERTS * MOE_PIECE
MOE_META = 4 * N_EXPERTS


def _route_kernel(lg_ref, idx_ref, gate_ref, lrank_ref, tcnt_ref):
    tm = lg_ref.shape[0]
    l = lg_ref[...]
    lane = lax.broadcasted_iota(I32, l.shape, 1).astype(F32)
    vals, onehots = [], []
    for k in range(TOP_K):
        m = jnp.max(l, axis=-1, keepdims=True)
        ik = jnp.min(jnp.where(l == m, lane, float(LANES)), axis=-1, keepdims=True)
        oh = lane == ik
        idx_ref[:, k:k + 1] = ik.astype(I32)
        vals.append(m)
        onehots.append(oh)
        l = jnp.where(oh, -jnp.inf, l)
    es = [jnp.exp(v - vals[0]) for v in vals]
    den = es[0] + es[1] + es[2] + es[3]
    sel = jnp.zeros(l.shape, F32)
    for k in range(TOP_K):
        gate_ref[:, k:k + 1] = es[k] / den
        sel = sel + onehots[k].astype(F32)
    ri = lax.broadcasted_iota(I32, (tm, tm), 0)
    ci = lax.broadcasted_iota(I32, (tm, tm), 1)
    before = _dot((ri > ci).astype(BF16), sel.astype(BF16))
    for k in range(TOP_K):
        lrank_ref[:, k:k + 1] = jnp.sum(jnp.where(onehots[k], before, 0.0), axis=-1,
                                        keepdims=True).astype(I32)
    tcnt_ref[...] = jnp.sum(sel, axis=0, keepdims=True)


def _route(logits):
    n_tok = logits.shape[0]
    tm = MOE_TOK
    n_t = n_tok // tm
    small = lambda dt: jax.ShapeDtypeStruct((n_tok, TOP_K), dt)
    kspec = pl.BlockSpec((tm, TOP_K), lambda i: (i, 0))
    tspec = pl.BlockSpec((None, 1, LANES), lambda i: (i, 0, 0))
    tshape = jax.ShapeDtypeStruct((n_t, 1, LANES), F32)
    return pl.pallas_call(
        _route_kernel,
        grid=(n_t,),
        in_specs=[pl.BlockSpec((tm, LANES), lambda i: (i, 0))],
        out_specs=[kspec, kspec, kspec, tspec],
        out_shape=[small(I32), small(F32), small(I32), tshape],
        compiler_params=_cparams(("arbitrary",)),
        name="route",
    )(logits)


def _run_copies(meta_ref, src_of, dst_of, sem, start):
    def per_expert(e, carry):
        local = pl.multiple_of(meta_ref[e], MOE_PIECE)
        slot = pl.multiple_of(meta_ref[N_EXPERTS + e], MOE_PIECE)

        def per_piece(p, c2):
            cp = pltpu.make_async_copy(src_of(local + p * MOE_PIECE, slot + p * MOE_PIECE),
                                       dst_of(local + p * MOE_PIECE, slot + p * MOE_PIECE), sem)
            if start:
                cp.start()
            else:
                cp.wait()
            return c2

        return lax.fori_loop(0, meta_ref[2 * N_EXPERTS + e], per_piece, carry)

    lax.fori_loop(0, N_EXPERTS, per_expert, 0)


def _dispatch_kernel(meta_ref, zmeta_ref, lpos_ref, h_ref, xs_ref, buf_ref, zero_ref, sem, *, tm_e):
    tm = h_ref.shape[0]
    rows = lambda ref, r: ref.at[pl.ds(r, MOE_PIECE)]

    @pl.when(pl.program_id(0) == 0)
    def _():
        zero_ref[...] = jnp.zeros_like(zero_ref)

        def zero_tail(start):
            def per_expert(e, carry):
                z0 = pl.multiple_of(zmeta_ref[e], MOE_PIECE)

                def per_piece(p, c2):
                    cp = pltpu.make_async_copy(zero_ref, rows(xs_ref, z0 + p * MOE_PIECE), sem)
                    if start:
                        cp.start()
                    else:
                        cp.wait()
                    return c2

                return lax.fori_loop(0, zmeta_ref[N_EXPERTS + e], per_piece, carry)

            lax.fori_loop(0, N_EXPERTS, per_expert, 0)

        zero_tail(True)
        zero_tail(False)

        buf_ref[0:tm_e, :] = jnp.zeros((tm_e, D_MODEL), F32)

        def zero_tiles(start):
            def per_tile(p, carry):
                t0 = pl.multiple_of((zmeta_ref[2 * N_EXPERTS] + p) * tm_e, tm_e)
                cp = pltpu.make_async_copy(buf_ref.at[pl.ds(0, tm_e)], xs_ref.at[pl.ds(t0, tm_e)], sem)
                if start:
                    cp.start()
                else:
                    cp.wait()
                return carry

            lax.fori_loop(0, zmeta_ref[2 * N_EXPERTS + 1], per_tile, 0)

        zero_tiles(True)
        zero_tiles(False)

    s_iota = lax.broadcasted_iota(I32, (MOE_BUF, tm), 0)
    perm = jnp.zeros((MOE_BUF, tm), F32)
    for k in range(TOP_K):
        perm = perm + (s_iota == lpos_ref[k:k + 1, :]).astype(F32)
    buf_ref[...] = _dot(perm.astype(BF16), h_ref[...].astype(BF16))

    src = lambda loc, slot: rows(buf_ref, loc)
    dst = lambda loc, slot: rows(xs_ref, slot)
    _run_copies(meta_ref, src, dst, sem, True)
    _run_copies(meta_ref, src, dst, sem, False)


def _dispatch(meta, zmeta, lpos_t, h2, n_slots, tm_e):
    n_tok = h2.shape[0]
    tm = MOE_TOK
    assert tm_e <= MOE_BUF
    return pl.pallas_call(
        functools.partial(_dispatch_kernel, tm_e=tm_e),
        grid=(n_tok // tm,),
        in_specs=[pl.BlockSpec((MOE_META,), lambda i: (i,), memory_space=pltpu.SMEM),
                  pl.BlockSpec((MOE_META,), lambda i: (0,), memory_space=pltpu.SMEM),
                  pl.BlockSpec((TOP_K, tm), lambda i: (0, i)),
                  pl.BlockSpec((tm, D_MODEL), lambda i: (i, 0))],
        out_specs=pl.BlockSpec(memory_space=pl.ANY),
        out_shape=jax.ShapeDtypeStruct((n_slots, D_MODEL), F32),
        scratch_shapes=[pltpu.VMEM((MOE_BUF, D_MODEL), F32), pltpu.VMEM((MOE_PIECE, D_MODEL), F32),
                        pltpu.SemaphoreType.DMA],
        compiler_params=_cparams(("arbitrary",)),
        name="moe_dispatch",
    )(meta, zmeta, lpos_t, h2)


def _expert_kernel(te_ref, nu_ref, xs_ref, wgu_ref, bgu_ref, wd_ref, bd_ref, y_ref):
    del te_ref

    @pl.when(pl.program_id(0) < nu_ref[0])
    def _():
        gu = _dot(xs_ref[...].astype(BF16), wgu_ref[...]) + bgu_ref[...]
        g = jnp.minimum(gu[:, :D_EXPERT], SWIGLU_LIMIT)
        lin = jnp.clip(gu[:, D_EXPERT:], -SWIGLU_LIMIT, SWIGLU_LIMIT)
        act = g * _sigmoid(SWIGLU_ALPHA * g) * (lin + 1.0)
        y_ref[...] = _dot(act.astype(BF16), wd_ref[...]) + bd_ref[...]

    @pl.when(pl.program_id(0) >= nu_ref[0])
    def _():
        y_ref[...] = jnp.zeros_like(y_ref)


def _experts(tile_expert, n_used, xs, wgu, bgu, wd, bd, tm):
    n_slots = xs.shape[0]
    n_tiles = n_slots // tm

    def row(i, te, nu):
        return (jnp.minimum(i, nu[0] - 1), 0)

    grid_spec = pltpu.PrefetchScalarGridSpec(
        num_scalar_prefetch=2,
        grid=(n_tiles,),
        in_specs=[pl.BlockSpec((tm, D_MODEL), row),
                  pl.BlockSpec((None, D_MODEL, 2 * D_EXPERT), lambda i, te, nu: (te[i], 0, 0)),
                  pl.BlockSpec((None, 1, 2 * D_EXPERT), lambda i, te, nu: (te[i], 0, 0)),
                  pl.BlockSpec((None, D_EXPERT, D_MODEL), lambda i, te, nu: (te[i], 0, 0)),
                  pl.BlockSpec((None, 1, D_MODEL), lambda i, te, nu: (te[i], 0, 0))],
        out_specs=pl.BlockSpec((tm, D_MODEL), lambda i, te, nu: (i, 0)),
    )
    return pl.pallas_call(
        _expert_kernel,
        grid_spec=grid_spec,
        out_shape=jax.ShapeDtypeStruct((n_slots, D_MODEL), F32),
        compiler_params=_cparams(("arbitrary",)),
        name="moe_experts",
    )(tile_expert, n_used, xs, wgu, bgu, wd, bd)


def _combine_kernel(meta_ref, lpos_ref, gate_ref, x_ref, mod_ref, post_ref, y_ref, xo_ref, buf_ref, sem):
    rows = lambda ref, r: ref.at[pl.ds(r, MOE_PIECE)]

    @pl.when(pl.program_id(0) == 0)
    def _():
        buf_ref[...] = jnp.zeros_like(buf_ref)

    src = lambda loc, slot: rows(y_ref, slot)
    dst = lambda loc, slot: rows(buf_ref, loc)
    _run_copies(meta_ref, src, dst, sem, True)
    _run_copies(meta_ref, src, dst, sem, False)

    tm = x_ref.shape[0]
    s_iota = lax.broadcasted_iota(I32, (tm, MOE_BUF), 1)
    sel = jnp.zeros((tm, MOE_BUF), F32)
    for k in range(TOP_K):
        sel = sel + jnp.where(s_iota == lpos_ref[:, k:k + 1], gate_ref[:, k:k + 1], 0.0)
    y = _dot(sel.astype(BF16), buf_ref[...].astype(BF16))
    gate2 = mod_ref[:, 5 * D_MODEL:6 * D_MODEL]
    xo_ref[...] = x_ref[...] + gate2 * _rms(y, post_ref[...])


def _combine(meta, lpos, gate, x_mid, mod, mod_row, post_g, y):
    n_tok = x_mid.shape[0]
    tm = MOE_TOK
    return pl.pallas_call(
        _combine_kernel,
        grid=(n_tok // tm,),
        in_specs=[pl.BlockSpec((MOE_META,), lambda i: (i,), memory_space=pltpu.SMEM),
                  pl.BlockSpec((tm, TOP_K), lambda i: (i, 0)),
                  pl.BlockSpec((tm, TOP_K), lambda i: (i, 0)),
                  pl.BlockSpec((tm, D_MODEL), lambda i: (i, 0)),
                  pl.BlockSpec((None, 1, 6 * D_MODEL), lambda i: (mod_row(i * tm), 0, 0)),
                  pl.BlockSpec((1, D_MODEL), lambda i: (0, 0)),
                  pl.BlockSpec(memory_space=pl.ANY)],
        out_specs=pl.BlockSpec((tm, D_MODEL), lambda i: (i, 0)),
        out_shape=jax.ShapeDtypeStruct((n_tok, D_MODEL), F32),
        scratch_shapes=[pltpu.VMEM((MOE_BUF, D_MODEL), F32), pltpu.SemaphoreType.DMA],
        compiler_params=_cparams(("arbitrary",)),
        name="moe_combine",
    )(meta, lpos, gate, x_mid, mod, post_g, y)


def _moe(h2, logits, x_mid, mod, mod_row, lw, tm_e=512):
    n_tok = h2.shape[0]
    n_t = n_tok // MOE_TOK
    idx, gate, lrank, tcnt = _route(logits)
    tcnt = tcnt[:, 0, :N_EXPERTS].astype(I32)
    pieces = (tcnt + MOE_PIECE - 1) // MOE_PIECE
    run_end = jnp.cumsum(pieces, axis=0) * MOE_PIECE
    used = run_end[-1]
    padded = (used + tm_e - 1) // tm_e * tm_e
    pad_end = jnp.cumsum(padded)
    offs = pad_end - padded
    n_tiles = (n_tok * TOP_K + n_t * N_EXPERTS * (MOE_PIECE - 1) + tm_e - 1) // tm_e + N_EXPERTS
    tile_start = jnp.arange(n_tiles, dtype=I32) * tm_e
    tile_expert = jnp.minimum(jnp.sum(pad_end[None, :] <= tile_start[:, None], axis=1),
                              N_EXPERTS - 1).astype(I32)
    n_used = (pad_end[-1:] // tm_e).astype(I32)
    lstart = (jnp.cumsum(pieces, axis=1) - pieces) * MOE_PIECE
    slot_start = offs[None, :] + run_end - pieces * MOE_PIECE
    meta = jnp.concatenate([lstart, slot_start, pieces, jnp.zeros_like(pieces)],
                           axis=1).reshape(-1).astype(I32)
    z0 = offs + used
    tail = jnp.stack([n_used[0], n_tiles - n_used[0]])
    zmeta = jnp.concatenate([z0, (pad_end - z0) // MOE_PIECE, tail,
                             jnp.zeros((MOE_META - 2 * N_EXPERTS - 2,), I32)]).astype(I32)
    tile_of = (jnp.arange(n_tok, dtype=I32) // MOE_TOK)[:, None]
    lpos = (jnp.take(lstart.reshape(-1), tile_of * N_EXPERTS + idx, axis=0) + lrank).astype(I32)
    xs = _dispatch(meta, zmeta, lpos.T, h2, n_tiles * tm_e, tm_e)
    y = _experts(tile_expert, n_used, xs, lw["exp_w_gu"], lw["exp_b_gu"], lw["exp_w_down"],
                 lw["exp_b_down"], tm_e)
    return _combine(meta, lpos, gate, x_mid, mod, mod_row, lw["norm_post_ffn"], y)


def _split_w_in(w_in):
    offs, o = {}, 0
    for name, width in (("dn_k", 1024), ("dn_v", 1024), ("dn_a", 16), ("dn_b", 16), ("at_k", 128),
                        ("at_v", 128), ("dn_q", 1024), ("dn_g", 1024), ("at_q", 1024),
                        ("sg_u", 1024), ("sg_v", 1024), ("gates", 3072)):
        offs[name] = (o, o + width)
        o += width
    sl = lambda n: w_in[:, offs[n][0]:offs[n][1]]
    w_main = jnp.concatenate([sl(n) for n in ("dn_k", "dn_v", "dn_q", "dn_g", "at_q", "sg_u", "sg_v",
                                              "gates")], axis=1).astype(BF16)
    pad = jnp.zeros((w_in.shape[0], N_SMALL_COLS - 2 * LANES - 4 * DN_HEADS), w_in.dtype)
    w_small = jnp.concatenate([sl("at_k"), sl("at_v"), sl("dn_a"), sl("dn_b"), pad], axis=1).astype(BF16)
    return w_main, w_small


def _dn_gates(small, B, T):
    ab = small[:, 2 * LANES:2 * LANES + 4 * DN_HEADS].reshape(B, T, 2, 2, DN_HEADS)
    col = jnp.transpose(ab, (3, 0, 1, 2, 4)).reshape(2, B, T, 2 * DN_HEADS)
    row = jnp.transpose(col.reshape(2, B, T // DN_CHUNK, DN_CHUNK, 2 * DN_HEADS), (0, 1, 2, 4, 3))
    return col, row


def kernel(x, c, ctx, c_ctx, w_mod, b_mod, norm_pre_mix, norm_post_mix, norm_pre_ffn, norm_post_ffn, w_in, sg_ln_g, sg_ln_b, sg_w, sg_b, dn_conv_w, dn_a_log, dn_dt_bias, dn_norm_g, at_sinks, w_proj_sg, w_proj_dn, w_proj_at, w_out, router_w, router_b, exp_w_gu, exp_b_gu, exp_w_down, exp_b_down):
    B, S, D = x.shape
    L = ctx.shape[1]
    depth = w_mod.shape[0]
    assert D == D_MODEL and S % GRID_W == 0
    n_lat, n_ctx = B * S, B * L

    rows = (B + 1 + 7) // 8 * 8
    cvec = jnp.zeros((rows, D), F32).at[:B].set(c).at[B].set(c_ctx)
    mod_all = _modulation(cvec, w_mod, b_mod)
    tables = _rope_tables(S)

    lat_row = lambda t: t // S
    ctx_row = lambda t: B
    all_row = lambda t: jnp.where(t < n_lat, t // S, B)

    xl = x.reshape(n_lat, D)
    xc = ctx.reshape(n_ctx, D)
    for l in range(depth):
        need_ctx_out = l < depth - 1
        mod = mod_all[l].reshape(rows, 1, 6 * D)
        w_main, w_small = _split_w_in(w_in[l])
        lw = {
            "dn_norm_g": dn_norm_g[l].reshape(1, -1),
            "norm_post_mix": norm_post_mix[l].reshape(1, -1),
            "norm_pre_ffn": norm_pre_ffn[l].reshape(1, -1),
            "norm_post_ffn": norm_post_ffn[l].reshape(1, -1),
            "w_proj_sg": w_proj_sg[l].astype(BF16), "w_proj_dn": w_proj_dn[l].astype(BF16),
            "w_proj_at": w_proj_at[l].astype(BF16), "w_out": w_out[l].astype(BF16),
            "router_w": jnp.pad(router_w[l], ((0, 0), (0, LANES - N_EXPERTS))).astype(BF16),
            "router_b": jnp.pad(router_b[l], (0, LANES - N_EXPERTS),
                                constant_values=NEG_BIG).reshape(1, -1),
            "exp_w_gu": exp_w_gu[l].astype(BF16), "exp_b_gu": exp_b_gu[l].reshape(N_EXPERTS, 1, -1),
            "exp_w_down": exp_w_down[l].astype(BF16), "exp_b_down": exp_b_down[l].reshape(N_EXPERTS, 1, -1),
        }
        pre_g = norm_pre_mix[l].reshape(1, -1)
        main, small = _inproj(xl, mod, lat_row, pre_g, w_main, w_small, min(1024, S))
        w_main_c = w_main if need_ctx_out else w_main[:, :N_CTX_MAIN_COLS]
        main_c, small_c = _inproj(xc, mod, ctx_row, pre_g, w_main_c, w_small, min(1024, n_ctx))

        sg_args = (sg_ln_g[l].reshape(1, -1), sg_ln_b[l].reshape(1, -1), sg_w[l].astype(BF16),
                   sg_b[l].T)
        ysg = _sgu(main, *sg_args)

        gcol_c, grow_c = _dn_gates(small_c, B, L)
        gcol, grow = _dn_gates(small, B, S)
        s0 = jnp.zeros((2, B, DN_HEADS, DN_DIM, DN_DIM), F32)
        of_c, ob_c, s_ctx = _deltanet(main_c.reshape(B, L, -1), gcol_c, grow_c, dn_conv_w[l], dn_a_log[l],
                                      dn_dt_bias[l], s0, need_ctx_out)
        of_l, ob_l, _ = _deltanet(main.reshape(B, S, -1), gcol, grow, dn_conv_w[l], dn_a_log[l],
                                  dn_dt_bias[l], s_ctx, True)

        sinks = jnp.repeat(at_sinks[l], AT_BLOCK).reshape(-1, 1)
        q_r, k_r, v_r = _rope(main, small, tables, S)
        yat = _attention_local(q_r, k_r, v_r, small_c, sinks, B, S, L)

        x_mid, h2, logits = _merge(ysg, of_l.reshape(n_lat, D), ob_l.reshape(n_lat, D), main, yat, xl, mod,
                                   lat_row, lw)
        if need_ctx_out:
            ysg_c = _sgu(main_c, *sg_args)
            yat_c = _attention_ctx(main_c, small_c, sinks, B, L)
            xc_mid, h2c, logits_c = _merge(ysg_c, of_c.reshape(n_ctx, D), ob_c.reshape(n_ctx, D), main_c,
                                           yat_c, xc, mod, ctx_row, lw)
            x_mid = jnp.concatenate([x_mid, xc_mid], axis=0)
            h2 = jnp.concatenate([h2, h2c], axis=0)
            logits = jnp.concatenate([logits, logits_c], axis=0)
            xo = _moe(h2, logits, x_mid, mod, all_row, lw)
            xl, xc = xo[:n_lat], xo[n_lat:]
        else:
            xl = _moe(h2, logits, x_mid, mod, lat_row, lw)
    return xl.reshape(B, S, D)
```

```python
import functools
import math

import jax
import jax.numpy as jnp
from jax import lax
from jax.experimental import pallas as pl
from jax.experimental.pallas import tpu as pltpu

F32 = jnp.float32
BF16 = jnp.bfloat16
I32 = jnp.int32

EPS = 1e-6
D_MODEL = 1024
GRID_W = 64

SG_CHUNK = 128
SG_GROUPS = 8

DN_HEADS = 8
DN_DIM = 128
DN_CONV = 5
DN_CHUNK = 64
DN_HALO = 16
DN_PREP_CHUNKS = 2

AT_Q_HEADS = 16
AT_KV_HEADS = 2
AT_DIM = 64
AT_BLOCK = 128
ROPE_BASE = 10000.0

N_EXPERTS = 32
TOP_K = 4
D_EXPERT = 1024
SWIGLU_ALPHA = 1.702
SWIGLU_LIMIT = 7.0
N_BRANCH = 3

LANES = 128
NEG_BIG = -1e30

COL_DN_K, COL_DN_V, COL_DN_Q, COL_DN_G, COL_AT_Q, COL_SG_U, COL_SG_V, COL_GATE0 = range(8)
N_MAIN_COLS = 10 * D_MODEL
N_CTX_MAIN_COLS = 2 * D_MODEL
N_SMALL_COLS = 3 * LANES

VMEM_LIMIT = 52 * 1024 * 1024


def _cparams(sem):
    return pltpu.CompilerParams(dimension_semantics=sem, vmem_limit_bytes=VMEM_LIMIT)


def _dot(a, b):
    return jnp.dot(a, b, preferred_element_type=F32)


def _dot_nt(a, b):
    return lax.dot_general(a, b, (((1,), (1,)), ((), ())), preferred_element_type=F32)


def _dot_tn(a, b):
    return lax.dot_general(a, b, (((0,), (0,)), ((), ())), preferred_element_type=F32)


def _sigmoid(x):
    return 1.0 / (1.0 + jnp.exp(-x))


def _silu(x):
    return x * _sigmoid(x)


def _gelu_tanh(x):
    return 0.5 * x * (1.0 + jnp.tanh(math.sqrt(2.0 / math.pi) * (x + 0.044715 * (x * x * x))))


def _softplus(x):
    return jnp.maximum(x, 0.0) + jnp.log(1.0 + jnp.exp(-jnp.abs(x)))


def _rms(x, g):
    return x * lax.rsqrt(jnp.mean(x * x, axis=-1, keepdims=True) + EPS) * g


def _mod_kernel(c_ref, w_ref, b_ref, o_ref):
    s = _silu(c_ref[...])
    o_ref[...] = jnp.dot(s, w_ref[...], preferred_element_type=F32,
                         precision=lax.Precision.HIGHEST) + b_ref[...]


def _modulation(cvec, w_mod, b_mod):
    depth = w_mod.shape[0]
    rows = cvec.shape[0]
    n_col = w_mod.shape[2] // D_MODEL
    return pl.pallas_call(
        _mod_kernel,
        grid=(depth, n_col),
        in_specs=[pl.BlockSpec((rows, D_MODEL), lambda l, j: (0, 0)),
                  pl.BlockSpec((None, D_MODEL, D_MODEL), lambda l, j: (l, 0, j)),
                  pl.BlockSpec((None, 1, D_MODEL), lambda l, j: (l, 0, j))],
        out_specs=pl.BlockSpec((None, rows, D_MODEL), lambda l, j: (l, 0, j)),
        out_shape=jax.ShapeDtypeStruct((depth, rows, w_mod.shape[2]), F32),
        compiler_params=_cparams(("arbitrary", "arbitrary")),
        name="modulation",
    )(cvec, w_mod, b_mod.reshape(depth, 1, -1))


def _inproj_kernel(x_ref, mod_ref, g_ref, wm_ref, ws_ref, main_ref, small_ref, h_ref):
    @pl.when(pl.program_id(1) == 0)
    def _():
        sh = mod_ref[:, 0 * D_MODEL:1 * D_MODEL]
        sc = mod_ref[:, 1 * D_MODEL:2 * D_MODEL]
        h = (_rms(x_ref[...], g_ref[...]) * (1.0 + sc) + sh).astype(BF16)
        h_ref[...] = h
        small_ref[...] = _dot(h, ws_ref[...])

    main_ref[...] = _dot(h_ref[...], wm_ref[...]).astype(BF16)


def _inproj(x, mod, mod_row, norm_g, w_main, w_small, tm, tn=2048):
    n_tok = x.shape[0]
    n_main = w_main.shape[1]
    return pl.pallas_call(
        _inproj_kernel,
        grid=(n_tok // tm, n_main // tn),
        in_specs=[pl.BlockSpec((tm, D_MODEL), lambda i, j: (i, 0)),
                  pl.BlockSpec((None, 1, 6 * D_MODEL), lambda i, j: (mod_row(i * tm), 0, 0)),
                  pl.BlockSpec((1, D_MODEL), lambda i, j: (0, 0)),
                  pl.BlockSpec((D_MODEL, tn), lambda i, j: (0, j)),
                  pl.BlockSpec((D_MODEL, N_SMALL_COLS), lambda i, j: (0, 0))],
        out_specs=[pl.BlockSpec((tm, tn), lambda i, j: (i, j)),
                   pl.BlockSpec((tm, N_SMALL_COLS), lambda i, j: (i, 0))],
        out_shape=[jax.ShapeDtypeStruct((n_tok, n_main), BF16),
                   jax.ShapeDtypeStruct((n_tok, N_SMALL_COLS), F32)],
        scratch_shapes=[pltpu.VMEM((tm, D_MODEL), BF16)],
        compiler_params=_cparams(("arbitrary", "arbitrary")),
        name="inproj",
    )(x, mod, norm_g, w_main, w_small)


def _sgu_kernel(u_ref, v_ref, lng_ref, lnb_ref, ws_ref, bs_ref, o_ref, *, n_chunk):
    u = _gelu_tanh(u_ref[...].astype(F32))
    v = _gelu_tanh(v_ref[...].astype(F32))
    vc = v - jnp.mean(v, axis=-1, keepdims=True)
    var = jnp.mean(vc * vc, axis=-1, keepdims=True)
    vn = (vc * lax.rsqrt(var + EPS) * lng_ref[...] + lnb_ref[...]).astype(BF16)
    for n in range(n_chunk):
        rows = slice(n * SG_CHUNK, (n + 1) * SG_CHUNK)
        for g in range(SG_GROUPS):
            cols = slice(g * LANES, (g + 1) * LANES)
            mixed = _dot(ws_ref[g], vn[rows, cols]) + bs_ref[:, g:g + 1]
            o_ref[rows, cols] = (u[rows, cols] * mixed).astype(BF16)


def _sgu(main, sg_ln_g, sg_ln_b, sg_w, sg_bt, n_chunk=2):
    n_tok = main.shape[0]
    tc = n_chunk * SG_CHUNK
    return pl.pallas_call(
        functools.partial(_sgu_kernel, n_chunk=n_chunk),
        grid=(n_tok // tc,),
        in_specs=[pl.BlockSpec((tc, D_MODEL), lambda i: (i, COL_SG_U)),
                  pl.BlockSpec((tc, D_MODEL), lambda i: (i, COL_SG_V)),
                  pl.BlockSpec((1, D_MODEL), lambda i: (0, 0)),
                  pl.BlockSpec((1, D_MODEL), lambda i: (0, 0)),
                  pl.BlockSpec((SG_GROUPS, SG_CHUNK, SG_CHUNK), lambda i: (0, 0, 0)),
                  pl.BlockSpec((SG_CHUNK, SG_GROUPS), lambda i: (0, 0))],
        out_specs=pl.BlockSpec((tc, D_MODEL), lambda i: (i, 0)),
        out_shape=jax.ShapeDtypeStruct((n_tok, D_MODEL), BF16),
        compiler_params=_cparams(("arbitrary",)),
        name="sgu",
    )(main, main, sg_ln_g, sg_ln_b, sg_w, sg_bt)


def _dn_kernel(*refs, with_q, n_chunks):
    s_refs = refs[-DN_HEADS:]
    ext_refs = refs[-DN_HEADS - 3:-DN_HEADS]
    refs = refs[:-DN_HEADS - 3]
    if with_q:
        (qp_ref, qc_ref, qn_ref, kp_ref, kc_ref, kn_ref, vp_ref, vc_ref, vn_ref,
         gcol_ref, grow_ref, cw_ref, alog_r_ref, alog_c_ref, dtb_r_ref, dtb_c_ref, s0_ref,
         o_ref, sfin_ref) = refs
    else:
        (kp_ref, kc_ref, kn_ref, vp_ref, vc_ref, vn_ref,
         gcol_ref, grow_ref, cw_ref, alog_r_ref, alog_c_ref, dtb_r_ref, dtb_c_ref, s0_ref,
         sfin_ref) = refs
    d = pl.program_id(0)
    c = pl.program_id(2)
    is_fwd = d == 0
    cidx = jnp.where(is_fwd, c, n_chunks - 1 - c)
    C = DN_CHUNK

    @pl.when(c == 0)
    def _():
        for h in range(DN_HEADS):
            s_refs[h][...] = s0_ref[h]

    has_prev = (cidx > 0).astype(F32)
    has_next = (cidx < n_chunks - 1).astype(F32)

    def conv_silu(p_ref, c_ref, n_ref, part):
        ext_ref = ext_refs[part]
        ext_ref[0:DN_HALO, :] = p_ref[...].astype(F32) * has_prev
        ext_ref[DN_HALO:DN_HALO + C, :] = c_ref[...].astype(F32)
        ext_ref[DN_HALO + C:2 * DN_HALO + C, :] = n_ref[...].astype(F32) * has_next
        base = DN_HALO - DN_CONV // 2
        y = None
        for i in range(DN_CONV):
            w = cw_ref[i:i + 1, part * D_MODEL:(part + 1) * D_MODEL]
            t = ext_ref[base + i:base + i + C, :] * w
            y = t if y is None else y + t
        return _silu(y)

    k_all = conv_silu(kp_ref, kc_ref, kn_ref, 1)
    v_all = conv_silu(vp_ref, vc_ref, vn_ref, 2)
    q_all = conv_silu(qp_ref, qc_ref, qn_ref, 0) if with_q else None

    gcol = gcol_ref[...]
    ld_col = -jnp.exp(alog_r_ref[...]) * _softplus(gcol[:, 0:DN_HEADS] + dtb_r_ref[...])
    beta_col = _sigmoid(gcol[:, DN_HEADS:2 * DN_HEADS])
    ld_row = -jnp.exp(alog_c_ref[...]) * _softplus(grow_ref[0:DN_HEADS, :] + dtb_c_ref[...])

    ri = lax.broadcasted_iota(I32, (C, C), 0)
    ci = lax.broadcasted_iota(I32, (C, C), 1)
    delta = (ri - ci) * (1 - 2 * d)
    incl = delta >= 0
    strict = delta > 0
    incl_t = delta <= 0
    gam_col = jnp.dot(incl.astype(F32), ld_col, preferred_element_type=F32,
                      precision=lax.Precision.HIGHEST)
    gam_row = jnp.dot(ld_row, incl_t.astype(F32), preferred_element_type=F32,
                      precision=lax.Precision.HIGHEST)
    gam_tot = jnp.sum(ld_col, axis=0, keepdims=True)
    eye = (ri == ci).astype(F32)

    H = range(DN_HEADS)
    lanes = [slice(h * DN_DIM, (h + 1) * DN_DIM) for h in H]
    gc = [gam_col[:, h:h + 1] for h in H]
    bc = [beta_col[:, h:h + 1] for h in H]
    kh = [k_all[:, lanes[h]] for h in H]
    kh = [kh[h] * lax.rsqrt(jnp.sum(kh[h] * kh[h], axis=-1, keepdims=True) + EPS) for h in H]
    kb = [kh[h].astype(BF16) for h in H]
    decay = [jnp.exp(jnp.where(incl, gc[h] - gam_row[h:h + 1, :], NEG_BIG)) for h in H]
    kk = [_dot_nt(kb[h], kb[h]) for h in H]
    x = [-(jnp.where(strict, decay[h], 0.0) * bc[h] * kk[h]) for h in H]
    p = [eye + x[h] for h in H]
    xb = [x[h].astype(BF16) for h in H]
    x = [_dot(xb[h], xb[h]) for h in H]
    n_fac = int(math.log2(C)) - 1
    for j in range(n_fac):
        xb = [x[h].astype(BF16) for h in H]
        if j < n_fac - 1:
            r = [_dot(xb[h], jnp.concatenate([xb[h], p[h].astype(BF16)], axis=1)) for h in H]
            x = [r[h][:, :C] for h in H]
            p = [p[h] + r[h][:, C:] for h in H]
        else:
            p = [p[h] + _dot(xb[h], p[h].astype(BF16)) for h in H]
    rhs = [jnp.concatenate([kh[h] * (bc[h] * jnp.exp(gc[h])), v_all[:, lanes[h]] * bc[h]],
                           axis=1).astype(BF16) for h in H]
    sol = [_dot(p[h].astype(BF16), rhs[h]) for h in H]
    w = [sol[h][:, :DN_DIM] for h in H]
    u0 = [sol[h][:, DN_DIM:] for h in H]
    k_end = [(kh[h] * jnp.exp(gam_tot[:, h:h + 1] - gc[h])).astype(BF16) for h in H]
    s = [s_refs[h][...] for h in H]
    sb = [s[h].astype(BF16) for h in H]
    if with_q:
        qh = [q_all[:, lanes[h]] for h in H]
        qh = [qh[h] * (lax.rsqrt(jnp.sum(qh[h] * qh[h], axis=-1, keepdims=True) + EPS) * DN_DIM ** -0.5)
              for h in H]
        qk = [(_dot_nt(qh[h].astype(BF16), kb[h]) * decay[h]).astype(BF16) for h in H]
        wq = [jnp.concatenate([w[h], qh[h] * jnp.exp(gc[h])], axis=0).astype(BF16) for h in H]
        ws = [_dot(wq[h], sb[h]) for h in H]
        ub = [(u0[h] - ws[h][:C]).astype(BF16) for h in H]
        qu = [_dot(qk[h], ub[h]) for h in H]
        for h in H:
            o_ref[:, lanes[h]] = (ws[h][C:] + qu[h]).astype(o_ref.dtype)
    else:
        ws = [_dot(w[h].astype(BF16), sb[h]) for h in H]
        ub = [(u0[h] - ws[h]).astype(BF16) for h in H]
    ku = [_dot_tn(k_end[h], ub[h]) for h in H]
    for h in H:
        s_refs[h][...] = jnp.exp(gam_tot[:, h:h + 1]) * s[h] + ku[h]

    @pl.when(c == n_chunks - 1)
    def _():
        for h in range(DN_HEADS):
            sfin_ref[h] = s_refs[h][...]


def _deltanet_single_pass(main3, gate_col, gate_row, conv_w, alog, dtb, s0, with_q):
    B, T, _ = main3.shape
    C = DN_CHUNK
    n_chunks = T // C
    hpc = C // DN_HALO
    n_halo = T // DN_HALO

    def cix(d, c):
        return jnp.where(d == 0, c, n_chunks - 1 - c)

    def trio(col):
        return [pl.BlockSpec((None, DN_HALO, D_MODEL),
                             lambda d, b, c: (b, jnp.maximum(cix(d, c) * hpc - 1, 0), col)),
                pl.BlockSpec((None, C, D_MODEL), lambda d, b, c: (b, cix(d, c), col)),
                pl.BlockSpec((None, DN_HALO, D_MODEL),
                             lambda d, b, c: (b, jnp.minimum((cix(d, c) + 1) * hpc, n_halo - 1), col))]

    in_specs = (trio(COL_DN_Q) if with_q else []) + trio(COL_DN_K) + trio(COL_DN_V) + [
        pl.BlockSpec((None, None, C, 2 * DN_HEADS), lambda d, b, c: (d, b, cix(d, c), 0)),
        pl.BlockSpec((None, None, None, 2 * DN_HEADS, C), lambda d, b, c: (d, b, cix(d, c), 0, 0)),
        pl.BlockSpec((DN_CONV, 3 * D_MODEL), lambda d, b, c: (0, 0)),
        pl.BlockSpec((None, 1, DN_HEADS), lambda d, b, c: (d, 0, 0)),
        pl.BlockSpec((None, DN_HEADS, 1), lambda d, b, c: (d, 0, 0)),
        pl.BlockSpec((None, 1, DN_HEADS), lambda d, b, c: (d, 0, 0)),
        pl.BlockSpec((None, DN_HEADS, 1), lambda d, b, c: (d, 0, 0)),
        pl.BlockSpec((None, None, DN_HEADS, DN_DIM, DN_DIM), lambda d, b, c: (d, b, 0, 0, 0)),
    ]
    s_spec = pl.BlockSpec((None, None, DN_HEADS, DN_DIM, DN_DIM), lambda d, b, c: (d, b, 0, 0, 0))
    s_shape = jax.ShapeDtypeStruct((2, B, DN_HEADS, DN_DIM, DN_DIM), F32)
    if with_q:
        out_specs = [pl.BlockSpec((None, None, C, D_MODEL), lambda d, b, c: (d, b, cix(d, c), 0)), s_spec]
        out_shape = [jax.ShapeDtypeStruct((2, B, T, D_MODEL), BF16), s_shape]
    else:
        out_specs = [s_spec]
        out_shape = [s_shape]
    n_main = 3 if with_q else 2
    args = [main3] * (3 * n_main) + [
        gate_col, gate_row, conv_w,
        alog.reshape(2, 1, DN_HEADS), alog.reshape(2, DN_HEADS, 1),
        dtb.reshape(2, 1, DN_HEADS), dtb.reshape(2, DN_HEADS, 1), s0]
    out = pl.pallas_call(
        functools.partial(_dn_kernel, with_q=with_q, n_chunks=n_chunks),
        grid=(2, B, n_chunks),
        in_specs=in_specs, out_specs=out_specs, out_shape=out_shape,
        scratch_shapes=[pltpu.VMEM((C + 2 * DN_HALO, D_MODEL), F32)] * 3
        + [pltpu.VMEM((DN_DIM, DN_DIM), F32)] * DN_HEADS,
        compiler_params=_cparams(("arbitrary", "arbitrary", "arbitrary")),
        name="deltanet_q" if with_q else "deltanet_state",
    )(*args)
    return (out[0], out[1]) if with_q else (None, out[0])


def _dn_prep_kernel(*refs, with_q, n_chunks):
    if with_q:
        (qp_ref, qc_ref, qn_ref, kp_ref, kc_ref, kn_ref, vp_ref, vc_ref, vn_ref,
         gcol_ref, grow_ref, cw_ref, alog_r_ref, alog_c_ref, dtb_r_ref, dtb_c_ref,
         w_ref, u0_ref, ke_ref, gt_ref, qs_ref, qk_ref) = refs
    else:
        (kp_ref, kc_ref, kn_ref, vp_ref, vc_ref, vn_ref,
         gcol_ref, grow_ref, cw_ref, alog_r_ref, alog_c_ref, dtb_r_ref, dtb_c_ref,
         w_ref, u0_ref, ke_ref, gt_ref) = refs
    c = pl.program_id(1)
    C = DN_CHUNK
    R = DN_PREP_CHUNKS * C
    has_prev = (c > 0).astype(BF16)
    has_next = (c < n_chunks // DN_PREP_CHUNKS - 1).astype(BF16)

    pad = DN_CONV // 2
    n_sh = DN_CONV - 1
    sr = lax.broadcasted_iota(I32, (n_sh * R, R + 2 * DN_HALO), 0)
    sc = lax.broadcasted_iota(I32, (n_sh * R, R + 2 * DN_HALO), 1)
    blk = sr // R
    off = jnp.where(blk < pad, blk - pad, blk - pad + 1)
    shift_mat = (sc == DN_HALO + (sr - blk * R) + off).astype(BF16)

    def conv_silu(p_ref, c_ref, n_ref, part):
        cur = c_ref[...]
        ext = jnp.concatenate([p_ref[...] * has_prev, cur, n_ref[...] * has_next], axis=0)
        sh = _dot(shift_mat, ext)
        taps = [sh[j * R:(j + 1) * R] for j in range(pad)] + [cur.astype(F32)] + \
               [sh[j * R:(j + 1) * R] for j in range(pad, n_sh)]
        y = None
        for i in range(DN_CONV):
            t = taps[i] * cw_ref[i:i + 1, part * D_MODEL:(part + 1) * D_MODEL]
            y = t if y is None else y + t
        return _silu(y)

    k_all = conv_silu(kp_ref, kc_ref, kn_ref, 1)
    v_all = conv_silu(vp_ref, vc_ref, vn_ref, 2)
    q_all = conv_silu(qp_ref, qc_ref, qn_ref, 0) if with_q else None

    ri = lax.broadcasted_iota(I32, (C, C), 0)
    ci = lax.broadcasted_iota(I32, (C, C), 1)
    eye = (ri == ci).astype(F32)
    CC = range(DN_PREP_CHUNKS)
    H = range(DN_HEADS)
    CH = [(cc, h) for cc in CC for h in H]
    ch = {key: i for i, key in enumerate(CH)}
    rows = [slice(cc * C, (cc + 1) * C) for cc in CC]
    lanes = [slice(h * DN_DIM, (h + 1) * DN_DIM) for h in H]
    kh = [k_all[rows[cc], lanes[h]] for cc, h in CH]
    kh = [k * lax.rsqrt(jnp.sum(k * k, axis=-1, keepdims=True) + EPS) for k in kh]
    kb = [k.astype(BF16) for k in kh]
    vh = [v_all[rows[cc], lanes[h]] for cc, h in CH]
    if with_q:
        qh = [q_all[rows[cc], lanes[h]] for cc, h in CH]
        qh = [q * (lax.rsqrt(jnp.sum(q * q, axis=-1, keepdims=True) + EPS) * DN_DIM ** -0.5) for q in qh]
        gram = [_dot_nt(jnp.concatenate([kb[j], qh[j].astype(BF16)], axis=0), kb[j]) for j in range(len(CH))]
        kk = [g[:C] for g in gram]
        qk_raw = [g[C:] for g in gram]
    else:
        kk = [_dot_nt(k, k) for k in kb]

    D2 = range(2)
    DH = [(cc, d, h) for cc in CC for d in D2 for h in H]
    incl = [(ri >= ci), (ri <= ci)]
    strict = [(ri > ci), (ri < ci)]
    gam_col, gam_row, gam_tot, beta_col = {}, {}, {}, {}
    for cc in CC:
        for d in D2:
            gcol = gcol_ref[d, rows[cc], :]
            ld_col = -jnp.exp(alog_r_ref[d]) * _softplus(gcol[:, 0:DN_HEADS] + dtb_r_ref[d])
            ld_row = -jnp.exp(alog_c_ref[d]) * _softplus(grow_ref[d, cc][0:DN_HEADS, :] + dtb_c_ref[d])
            beta_col[cc, d] = _sigmoid(gcol[:, DN_HEADS:2 * DN_HEADS])
            gam_col[cc, d] = jnp.dot(incl[d].astype(F32), ld_col, preferred_element_type=F32,
                                     precision=lax.Precision.HIGHEST)
            gam_row[cc, d] = jnp.dot(ld_row, incl[1 - d].astype(F32), preferred_element_type=F32,
                                     precision=lax.Precision.HIGHEST)
            tot = jnp.sum(ld_col, axis=0, keepdims=True)
            gam_tot[cc, d] = tot
            gt_ref[d, cc] = tot
    gc = [gam_col[cc, d][:, h:h + 1] for cc, d, h in DH]
    bc = [beta_col[cc, d][:, h:h + 1] for cc, d, h in DH]
    decay = [jnp.exp(jnp.where(incl[d], gc[i] - gam_row[cc, d][h:h + 1, :], NEG_BIG))
             for i, (cc, d, h) in enumerate(DH)]
    x = [-(jnp.where(strict[d], decay[i], 0.0) * bc[i] * kk[ch[cc, h]]) for i, (cc, d, h) in enumerate(DH)]
    N = range(len(DH))
    p = [eye + x[i] for i in N]
    xb = [x[i].astype(BF16) for i in N]
    x = [_dot(xb[i], xb[i]) for i in N]
    n_fac = int(math.log2(C)) - 1
    for j in range(n_fac):
        xb = [x[i].astype(BF16) for i in N]
        if j < n_fac - 1:
            r = [_dot(xb[i], jnp.concatenate([xb[i], p[i].astype(BF16)], axis=1)) for i in N]
            x = [r[i][:, :C] for i in N]
            p = [p[i] + r[i][:, C:] for i in N]
        else:
            p = [p[i] + _dot(xb[i], p[i].astype(BF16)) for i in N]
    rhs = [jnp.concatenate([kh[ch[cc, h]] * (bc[i] * jnp.exp(gc[i])), vh[ch[cc, h]] * bc[i]],
                           axis=1).astype(BF16) for i, (cc, d, h) in enumerate(DH)]
    sol = [_dot(p[i].astype(BF16), rhs[i]) for i in N]
    for i, (cc, d, h) in enumerate(DH):
        j = ch[cc, h]
        w_ref[d, rows[cc], lanes[h]] = sol[i][:, :DN_DIM].astype(BF16)
        u0_ref[d, rows[cc], lanes[h]] = sol[i][:, DN_DIM:].astype(BF16)
        ke_ref[d, rows[cc], lanes[h]] = (kh[j] * jnp.exp(gam_tot[cc, d][:, h:h + 1] - gc[i])).astype(BF16)
        if with_q:
            qs_ref[d, rows[cc], lanes[h]] = (qh[j] * jnp.exp(gc[i])).astype(BF16)
            qk_ref[d, rows[cc], h * C:(h + 1) * C] = (qk_raw[j] * decay[i]).astype(BF16)


def _dn_scan_kernel(*refs, with_q, n_chunks):
    n_state = 2 * DN_HEADS
    s_refs = refs[-n_state:]
    refs = refs[:-n_state]
    if with_q:
        (w0, w1, u0, u1, k0, k1, g0, g1, qs0, qs1, qk0, qk1, s0_ref, o0_ref, o1_ref, sfin_ref) = refs
        qs_r, qk_r, o_r = (qs0, qs1), (qk0, qk1), (o0_ref, o1_ref)
    else:
        (w0, w1, u0, u1, k0, k1, g0, g1, s0_ref, sfin_ref) = refs
    w_r, u_r, k_r, g_r = (w0, w1), (u0, u1), (k0, k1), (g0, g1)
    c = pl.program_id(1)
    C = DN_CHUNK
    DH = [(d, h) for d in range(2) for h in range(DN_HEADS)]
    N = range(len(DH))
    lanes = [slice(h * DN_DIM, (h + 1) * DN_DIM) for h in range(DN_HEADS)]

    @pl.when(c == 0)
    def _():
        for i, (d, h) in enumerate(DH):
            s_refs[i][...] = s0_ref[d, h]

    s = [s_refs[i][...] for i in N]
    sb = [s[i].astype(BF16) for i in N]
    w = [w_r[d][:, lanes[h]] for d, h in DH]
    if with_q:
        wq = [jnp.concatenate([w[i], qs_r[d][:, lanes[h]]], axis=0) for i, (d, h) in enumerate(DH)]
        ws = [_dot(wq[i], sb[i]) for i in N]
        ub = [(u_r[d][:, lanes[h]].astype(F32) - ws[i][:C]).astype(BF16) for i, (d, h) in enumerate(DH)]
        qu = [_dot(qk_r[d][:, h * C:(h + 1) * C], ub[i]) for i, (d, h) in enumerate(DH)]
        for i, (d, h) in enumerate(DH):
            o_r[d][:, lanes[h]] = (ws[i][C:] + qu[i]).astype(BF16)
    else:
        ws = [_dot(w[i], sb[i]) for i in N]
        ub = [(u_r[d][:, lanes[h]].astype(F32) - ws[i]).astype(BF16) for i, (d, h) in enumerate(DH)]
    ku = [_dot_tn(k_r[d][:, lanes[h]], ub[i]) for i, (d, h) in enumerate(DH)]
    for i, (d, h) in enumerate(DH):
        s_refs[i][...] = jnp.exp(g_r[d][:, h:h + 1]) * s[i] + ku[i]

    @pl.when(c == n_chunks - 1)
    def _():
        for i, (d, h) in enumerate(DH):
            sfin_ref[d, h] = s_refs[i][...]


def _deltanet(main3, gate_col, gate_row, conv_w, alog, dtb, s0, with_q):
    B, T, _ = main3.shape
    C = DN_CHUNK
    n_chunks = T // C
    NC = DN_PREP_CHUNKS
    R = NC * C
    assert T % R == 0
    hpc = R // DN_HALO
    n_halo = T // DN_HALO

    def trio(col):
        return [pl.BlockSpec((None, DN_HALO, D_MODEL), lambda b, c: (b, jnp.maximum(c * hpc - 1, 0), col)),
                pl.BlockSpec((None, R, D_MODEL), lambda b, c: (b, c, col)),
                pl.BlockSpec((None, DN_HALO, D_MODEL),
                             lambda b, c: (b, jnp.minimum((c + 1) * hpc, n_halo - 1), col))]

    vec = lambda shape: pl.BlockSpec(shape, lambda b, c: (0,) * len(shape))
    in_specs = (trio(COL_DN_Q) if with_q else []) + trio(COL_DN_K) + trio(COL_DN_V) + [
        pl.BlockSpec((2, None, R, 2 * DN_HEADS), lambda b, c: (0, b, c, 0)),
        pl.BlockSpec((2, None, NC, 2 * DN_HEADS, C), lambda b, c: (0, b, c, 0, 0)),
        vec((DN_CONV, 3 * D_MODEL)),
        vec((2, 1, DN_HEADS)), vec((2, DN_HEADS, 1)), vec((2, 1, DN_HEADS)), vec((2, DN_HEADS, 1)),
    ]
    wide = lambda n: (pl.BlockSpec((2, None, R, n), lambda b, c: (0, b, c, 0)),
                      jax.ShapeDtypeStruct((2, B, T, n), BF16))
    outs = [wide(D_MODEL), wide(D_MODEL), wide(D_MODEL),
            (pl.BlockSpec((2, None, NC, 1, DN_HEADS), lambda b, c: (0, b, c, 0, 0)),
             jax.ShapeDtypeStruct((2, B, n_chunks, 1, DN_HEADS), F32))]
    if with_q:
        outs += [wide(D_MODEL), wide(DN_HEADS * C)]
    n_main = 3 if with_q else 2
    prep = pl.pallas_call(
        functools.partial(_dn_prep_kernel, with_q=with_q, n_chunks=n_chunks),
        grid=(B, n_chunks // NC),
        in_specs=in_specs, out_specs=[o[0] for o in outs], out_shape=[o[1] for o in outs],
        compiler_params=_cparams(("arbitrary", "arbitrary")),
        name="dn_prep_q" if with_q else "dn_prep",
    )(*([main3] * (3 * n_main)), gate_col, gate_row, conv_w,
      alog.reshape(2, 1, DN_HEADS), alog.reshape(2, DN_HEADS, 1),
      dtb.reshape(2, 1, DN_HEADS), dtb.reshape(2, DN_HEADS, 1))

    def both_dirs(arr, n):
        if n is None:
            return [pl.BlockSpec((None, None, None, 1, DN_HEADS), lambda b, c: (0, b, c, 0, 0)),
                    pl.BlockSpec((None, None, None, 1, DN_HEADS),
                                 lambda b, c: (1, b, n_chunks - 1 - c, 0, 0))], [arr, arr]
        return [pl.BlockSpec((None, None, C, n), lambda b, c: (0, b, c, 0)),
                pl.BlockSpec((None, None, C, n), lambda b, c: (1, b, n_chunks - 1 - c, 0))], [arr, arr]

    specs, args = [], []
    widths = [D_MODEL, D_MODEL, D_MODEL, None] + ([D_MODEL, DN_HEADS * C] if with_q else [])
    for arr, n in zip(prep, widths):
        sp, ar = both_dirs(arr, n)
        specs += sp
        args += ar
    s_spec = pl.BlockSpec((2, None, DN_HEADS, DN_DIM, DN_DIM), lambda b, c: (0, b, 0, 0, 0))
    s_shape = jax.ShapeDtypeStruct((2, B, DN_HEADS, DN_DIM, DN_DIM), F32)
    if with_q:
        out_specs = [pl.BlockSpec((None, C, D_MODEL), lambda b, c: (b, c, 0)),
                     pl.BlockSpec((None, C, D_MODEL), lambda b, c: (b, n_chunks - 1 - c, 0)), s_spec]
        out_shape = [jax.ShapeDtypeStruct((B, T, D_MODEL), BF16)] * 2 + [s_shape]
    else:
        out_specs, out_shape = [s_spec], [s_shape]
    out = pl.pallas_call(
        functools.partial(_dn_scan_kernel, with_q=with_q, n_chunks=n_chunks),
        grid=(B, n_chunks),
        in_specs=specs + [s_spec], out_specs=out_specs, out_shape=out_shape,
        scratch_shapes=[pltpu.VMEM((DN_DIM, DN_DIM), F32)] * (2 * DN_HEADS),
        compiler_params=_cparams(("arbitrary", "arbitrary")),
        name="dn_scan_q" if with_q else "dn_scan",
    )(*args, s0)
    return (out[0], out[1], out[2]) if with_q else (None, None, out[0])


def _rope_tables(S):
    half = AT_DIM // 2
    nf = half // 2
    inv_freq = ROPE_BASE ** (-jnp.arange(nf, dtype=F32) / nf)
    t = jnp.arange(S, dtype=jnp.int32)
    row = (t // GRID_W).astype(F32)
    col = (t % GRID_W).astype(F32)
    lane = jnp.arange(LANES)
    dd = lane % AT_DIM
    pos = jnp.where((dd < half)[None, :], row[:, None], col[:, None])
    ang = pos * inv_freq[lane % nf][None, :]
    first = ((lane % half) < nf)[None, :]
    sin = jnp.sin(ang)
    return jnp.cos(ang), jnp.where(first, -sin, 0.0), jnp.where(first, 0.0, sin)


def _rope_kernel(q_ref, k_ref, v_ref, cos_ref, sa_ref, sb_ref, qo_ref, ko_ref, vo_ref):
    cos, sa, sb = cos_ref[...], sa_ref[...], sb_ref[...]
    nf = AT_DIM // 4

    def rot(x):
        return x * cos + pltpu.roll(x, LANES - nf, 1) * sa + pltpu.roll(x, nf, 1) * sb

    for j in range(AT_Q_HEADS * AT_DIM // LANES):
        lanes = slice(j * LANES, (j + 1) * LANES)
        qo_ref[:, lanes] = (rot(q_ref[:, lanes].astype(F32)) * AT_DIM ** -0.5).astype(BF16)
    ko_ref[...] = rot(k_ref[...]).astype(BF16)
    vo_ref[...] = v_ref[...].astype(BF16)


def _rope(main, small, tables, S, tm=512):
    n_tok = main.shape[0]
    per_seq = S // tm
    tab_spec = pl.BlockSpec((tm, LANES), lambda i: (i % per_seq, 0))
    return pl.pallas_call(
        _rope_kernel,
        grid=(n_tok // tm,),
        in_specs=[pl.BlockSpec((tm, D_MODEL), lambda i: (i, COL_AT_Q)),
                  pl.BlockSpec((tm, LANES), lambda i: (i, 0)),
                  pl.BlockSpec((tm, LANES), lambda i: (i, 1)),
                  tab_spec, tab_spec, tab_spec],
        out_specs=[pl.BlockSpec((tm, D_MODEL), lambda i: (i, 0)),
                   pl.BlockSpec((tm, LANES), lambda i: (i, 0)),
                   pl.BlockSpec((tm, LANES), lambda i: (i, 0))],
        out_shape=[jax.ShapeDtypeStruct((n_tok, D_MODEL), BF16),
                   jax.ShapeDtypeStruct((n_tok, LANES), BF16),
                   jax.ShapeDtypeStruct((n_tok, LANES), BF16)],
        compiler_params=_cparams(("arbitrary",)),
        name="rope",
    )(main, small, small, *tables)


def _attn_kernel(*refs, local, n_blocks, q_scale):
    if local:
        (q_ref, kp_ref, kc_ref, kn_ref, vp_ref, vc_ref, vn_ref, kx_ref, vx_ref, sink_ref, o_ref) = refs
    else:
        (q_ref, kx_ref, vx_ref, sink_ref, o_ref) = refs
    P = AT_BLOCK
    G = AT_Q_HEADS // AT_KV_HEADS
    L = kx_ref.shape[0]
    kx = kx_ref[...].astype(BF16)
    vx = vx_ref[...].astype(BF16)
    if local:
        i = pl.program_id(1)
        k_all = jnp.concatenate([kp_ref[...], kc_ref[...], kn_ref[...], kx], axis=0)
        v_all = jnp.concatenate([vp_ref[...], vc_ref[...], vn_ref[...], vx], axis=0)
        qi = lax.broadcasted_iota(I32, (P, P), 0)
        kj = lax.broadcasted_iota(I32, (P, P), 1)
        b_prev = jnp.where(kj >= qi, 0.0, NEG_BIG) + jnp.where(i > 0, 0.0, NEG_BIG)
        b_next = jnp.where(kj <= qi, 0.0, NEG_BIG) + jnp.where(i < n_blocks - 1, 0.0, NEG_BIG)
        bias = jnp.concatenate([b_prev, jnp.zeros((P, P), F32), b_next, jnp.zeros((P, L), F32)], axis=1)
    else:
        k_all, v_all, bias = kx, vx, None
    n_keys = k_all.shape[0]
    lo = lax.broadcasted_iota(I32, (P, LANES), 1) < AT_DIM
    qf = q_ref[...].astype(F32) * q_scale
    pieces = []
    for qh in range(AT_Q_HEADS):
        blk = qf[:, (qh // 2) * LANES:(qh // 2 + 1) * LANES]
        want_lo = qh // G == 0
        if want_lo != (qh % 2 == 0):
            blk = pltpu.roll(blk, AT_DIM, 1)
        pieces.append(jnp.where(lo if want_lo else ~lo, blk, 0.0).astype(BF16))
    qs = jnp.concatenate(pieces, axis=0)
    s = _dot_nt(qs, k_all)
    if bias is not None:
        s = (s.reshape(AT_Q_HEADS, P, n_keys) + bias[None]).reshape(AT_Q_HEADS * P, n_keys)
    sink = sink_ref[...]
    m = jnp.maximum(jnp.max(s, axis=-1, keepdims=True), sink)
    p = jnp.exp(s - m)
    den = jnp.sum(p, axis=-1, keepdims=True) + jnp.exp(sink - m)
    o = _dot(p.astype(BF16), v_all) / den
    for j in range(AT_Q_HEADS // 2):
        a = o[(2 * j) * P:(2 * j + 1) * P]
        b = o[(2 * j + 1) * P:(2 * j + 2) * P]
        if (2 * j) // G == 0:
            out = jnp.where(lo, a, pltpu.roll(b, AT_DIM, 1))
        else:
            out = jnp.where(lo, pltpu.roll(a, AT_DIM, 1), b)
        o_ref[:, j * LANES:(j + 1) * LANES] = out.astype(BF16)


def _attention_local(q_r, k_r, v_r, small_c, sinks, B, S, L):
    P = AT_BLOCK
    nb = S // P

    def kv_trio():
        return [pl.BlockSpec((P, LANES), lambda b, i: (b * nb + jnp.maximum(i - 1, 0), 0)),
                pl.BlockSpec((P, LANES), lambda b, i: (b * nb + i, 0)),
                pl.BlockSpec((P, LANES), lambda b, i: (b * nb + jnp.minimum(i + 1, nb - 1), 0))]

    return pl.pallas_call(
        functools.partial(_attn_kernel, local=True, n_blocks=nb, q_scale=1.0),
        grid=(B, nb),
        in_specs=[pl.BlockSpec((P, D_MODEL), lambda b, i: (b * nb + i, 0))] + kv_trio() + kv_trio() + [
            pl.BlockSpec((L, LANES), lambda b, i: (b, 0)),
            pl.BlockSpec((L, LANES), lambda b, i: (b, 1)),
            pl.BlockSpec((AT_Q_HEADS * AT_BLOCK, 1), lambda b, i: (0, 0))],
        out_specs=pl.BlockSpec((P, D_MODEL), lambda b, i: (b * nb + i, 0)),
        out_shape=jax.ShapeDtypeStruct((B * S, D_MODEL), BF16),
        compiler_params=_cparams(("arbitrary", "arbitrary")),
        name="attn_local",
    )(q_r, k_r, k_r, k_r, v_r, v_r, v_r, small_c, small_c, sinks)


def _attention_ctx(main_c, small_c, sinks, B, L):
    P = AT_BLOCK
    nb = L // P
    return pl.pallas_call(
        functools.partial(_attn_kernel, local=False, n_blocks=nb, q_scale=AT_DIM ** -0.5),
        grid=(B, nb),
        in_specs=[pl.BlockSpec((P, D_MODEL), lambda b, i: (b * nb + i, COL_AT_Q)),
                  pl.BlockSpec((L, LANES), lambda b, i: (b, 0)),
                  pl.BlockSpec((L, LANES), lambda b, i: (b, 1)),
                  pl.BlockSpec((AT_Q_HEADS * AT_BLOCK, 1), lambda b, i: (0, 0))],
        out_specs=pl.BlockSpec((P, D_MODEL), lambda b, i: (b * nb + i, 0)),
        out_shape=jax.ShapeDtypeStruct((B * L, D_MODEL), BF16),
        compiler_params=_cparams(("arbitrary", "arbitrary")),
        name="attn_ctx",
    )(main_c, small_c, small_c, sinks)


def _merge_kernel(ysg_ref, of_ref, ob_ref, dng_ref, yat_ref, g0_ref, g1_ref, g2_ref, x_ref, mod_ref,
                  dn_norm_ref, post_ref, pre_ref, wsg_ref, wdn_ref, wat_ref, wout_ref, rw_ref, rb_ref,
                  xo_ref, h2_ref, lg_ref):
    o = of_ref[...].astype(F32) + ob_ref[...].astype(F32)
    dn_g = dn_norm_ref[...]
    parts = []
    for h in range(DN_HEADS):
        lanes = slice(h * DN_DIM, (h + 1) * DN_DIM)
        parts.append(_rms(o[:, lanes], dn_g) * _silu(dng_ref[:, lanes].astype(F32)))
    ydn = jnp.concatenate(parts, axis=1).astype(BF16)
    m = (_sigmoid(g0_ref[...].astype(F32)) * _dot(ysg_ref[...], wsg_ref[...])
         + _sigmoid(g1_ref[...].astype(F32)) * _dot(ydn, wdn_ref[...])
         + _sigmoid(g2_ref[...].astype(F32)) * _dot(yat_ref[...], wat_ref[...]))
    y = _dot(m.astype(BF16), wout_ref[...])
    gate1 = mod_ref[:, 2 * D_MODEL:3 * D_MODEL]
    sh2 = mod_ref[:, 3 * D_MODEL:4 * D_MODEL]
    sc2 = mod_ref[:, 4 * D_MODEL:5 * D_MODEL]
    xn = x_ref[...] + gate1 * _rms(y, post_ref[...])
    xo_ref[...] = xn
    h2 = _rms(xn, pre_ref[...]) * (1.0 + sc2) + sh2
    h2_ref[...] = h2
    lg_ref[...] = _dot(h2.astype(BF16), rw_ref[...]) + rb_ref[...]


def _merge(ysg, o_fwd, o_bwd, main, yat, x, mod, mod_row, lw, tm=512):
    n_tok = x.shape[0]
    const = lambda i: (0, 0)
    wspec = pl.BlockSpec((D_MODEL, D_MODEL), const, pipeline_mode=pl.Buffered(1))
    vspec = pl.BlockSpec((1, D_MODEL), const)
    return pl.pallas_call(
        _merge_kernel,
        grid=(n_tok // tm,),
        in_specs=[pl.BlockSpec((tm, D_MODEL), lambda i: (i, 0)),
                  pl.BlockSpec((tm, D_MODEL), lambda i: (i, 0)),
                  pl.BlockSpec((tm, D_MODEL), lambda i: (i, 0)),
                  pl.BlockSpec((tm, D_MODEL), lambda i: (i, COL_DN_G)),
                  pl.BlockSpec((tm, D_MODEL), lambda i: (i, 0)),
                  pl.BlockSpec((tm, D_MODEL), lambda i: (i, COL_GATE0)),
                  pl.BlockSpec((tm, D_MODEL), lambda i: (i, COL_GATE0 + 1)),
                  pl.BlockSpec((tm, D_MODEL), lambda i: (i, COL_GATE0 + 2)),
                  pl.BlockSpec((tm, D_MODEL), lambda i: (i, 0)),
                  pl.BlockSpec((None, 1, 6 * D_MODEL), lambda i: (mod_row(i * tm), 0, 0)),
                  pl.BlockSpec((1, DN_DIM), const), vspec, vspec,
                  wspec, wspec, wspec, wspec,
                  pl.BlockSpec((D_MODEL, LANES), const), pl.BlockSpec((1, LANES), const)],
        out_specs=[pl.BlockSpec((tm, D_MODEL), lambda i: (i, 0)),
                   pl.BlockSpec((tm, D_MODEL), lambda i: (i, 0)),
                   pl.BlockSpec((tm, LANES), lambda i: (i, 0))],
        out_shape=[jax.ShapeDtypeStruct((n_tok, D_MODEL), F32),
                   jax.ShapeDtypeStruct((n_tok, D_MODEL), F32),
                   jax.ShapeDtypeStruct((n_tok, LANES), F32)],
        compiler_params=_cparams(("arbitrary",)),
        name="merge",
    )(ysg, o_fwd, o_bwd, main, yat, main, main, main, x, mod,
      lw["dn_norm_g"], lw["norm_post_mix"], lw["norm_pre_ffn"],
      lw["w_proj_sg"], lw["w_proj_dn"], lw["w_proj_at"], lw["w_out"], lw["router_w"], lw["router_b"])


MOE_TOK = 256
MOE_PIECE = 8
MOE_BUF = MOE_TOK * TOP_K + N_EXPERTS * MOE_PIECE
MOE_META = 4 * N_EXPERTS


def _route_kernel(lg_ref, gate_ref, lpos_ref, tcnt_ref):
    tm = lg_ref.shape[0]
    l = lg_ref[...]
    lane = lax.broadcasted_iota(I32, l.shape, 1).astype(F32)
    vals, onehots = [], []
    for k in range(TOP_K):
        m = jnp.max(l, axis=-1, keepdims=True)
        ik = jnp.min(jnp.where(l == m, lane, float(LANES)), axis=-1, keepdims=True)
        oh = lane == ik
        vals.append(m)
        onehots.append(oh)
        l = jnp.where(oh, -jnp.inf, l)
    es = [jnp.exp(v - vals[0]) for v in vals]
    den = es[0] + es[1] + es[2] + es[3]
    sel = jnp.zeros(l.shape, F32)
    for k in range(TOP_K):
        gate_ref[:, k:k + 1] = es[k] / den
        sel = sel + onehots[k].astype(F32)
    ri = lax.broadcasted_iota(I32, (tm, tm), 0)
    ci = lax.broadcasted_iota(I32, (tm, tm), 1)
    before = _dot((ri > ci).astype(BF16), sel.astype(BF16))
    tcnt = jnp.sum(sel, axis=0, keepdims=True)
    tcnt_ref[...] = tcnt
    n_piece = jnp.floor((tcnt + (MOE_PIECE - 1)) * (1.0 / MOE_PIECE))
    ei = lax.broadcasted_iota(I32, (LANES, LANES), 0)
    ej = lax.broadcasted_iota(I32, (LANES, LANES), 1)
    run_start = _dot(jnp.broadcast_to(n_piece, (8, LANES)).astype(BF16),
                     (ei < ej).astype(BF16))[0:1] * float(MOE_PIECE)
    pos = before + run_start
    for k in range(TOP_K):
        lpos_ref[:, k:k + 1] = jnp.sum(jnp.where(onehots[k], pos, 0.0), axis=-1,
                                       keepdims=True).astype(I32)


def _route(logits):
    n_tok = logits.shape[0]
    tm = MOE_TOK
    n_t = n_tok // tm
    small = lambda dt: jax.ShapeDtypeStruct((n_tok, TOP_K), dt)
    kspec = pl.BlockSpec((tm, TOP_K), lambda i: (i, 0))
    tspec = pl.BlockSpec((None, 1, LANES), lambda i: (i, 0, 0))
    tshape = jax.ShapeDtypeStruct((n_t, 1, LANES), F32)
    return pl.pallas_call(
        _route_kernel,
        grid=(n_t,),
        in_specs=[pl.BlockSpec((tm, LANES), lambda i: (i, 0))],
        out_specs=[kspec, kspec, tspec],
        out_shape=[small(F32), small(I32), tshape],
        compiler_params=_cparams(("arbitrary",)),
        name="route",
    )(logits)


def _run_copies(meta_ref, src_of, dst_of, sem, start):
    def per_expert(e, carry):
        local = pl.multiple_of(meta_ref[e], MOE_PIECE)
        slot = pl.multiple_of(meta_ref[N_EXPERTS + e], MOE_PIECE)

        def per_piece(p, c2):
            cp = pltpu.make_async_copy(src_of(local + p * MOE_PIECE, slot + p * MOE_PIECE),
                                       dst_of(local + p * MOE_PIECE, slot + p * MOE_PIECE), sem)
            if start:
                cp.start()
            else:
                cp.wait()
            return c2

        return lax.fori_loop(0, meta_ref[2 * N_EXPERTS + e], per_piece, carry)

    lax.fori_loop(0, N_EXPERTS, per_expert, 0)


def _dispatch_kernel(meta_ref, zmeta_ref, lpos_ref, h_ref, xs_ref, buf_ref, zero_ref, sem, *, tm_e):
    tm = h_ref.shape[0]
    rows = lambda ref, r: ref.at[pl.ds(r, MOE_PIECE)]

    @pl.when(pl.program_id(0) == 0)
    def _():
        zero_ref[...] = jnp.zeros_like(zero_ref)

        def zero_tail(start):
            def per_expert(e, carry):
                z0 = pl.multiple_of(zmeta_ref[e], MOE_PIECE)

                def per_piece(p, c2):
                    cp = pltpu.make_async_copy(zero_ref, rows(xs_ref, z0 + p * MOE_PIECE), sem)
                    if start:
                        cp.start()
                    else:
                        cp.wait()
                    return c2

                return lax.fori_loop(0, zmeta_ref[N_EXPERTS + e], per_piece, carry)

            lax.fori_loop(0, N_EXPERTS, per_expert, 0)

        zero_tail(True)
        zero_tail(False)

        buf_ref[0:tm_e, :] = jnp.zeros((tm_e, D_MODEL), F32)

        def zero_tiles(start):
            def per_tile(p, carry):
                t0 = pl.multiple_of((zmeta_ref[2 * N_EXPERTS] + p) * tm_e, tm_e)
                cp = pltpu.make_async_copy(buf_ref.at[pl.ds(0, tm_e)], xs_ref.at[pl.ds(t0, tm_e)], sem)
                if start:
                    cp.start()
                else:
                    cp.wait()
                return carry

            lax.fori_loop(0, zmeta_ref[2 * N_EXPERTS + 1], per_tile, 0)

        zero_tiles(True)
        zero_tiles(False)

    s_iota = lax.broadcasted_iota(I32, (MOE_BUF, tm), 0)
    perm = jnp.zeros((MOE_BUF, tm), F32)
    for k in range(TOP_K):
        perm = perm + (s_iota == lpos_ref[k:k + 1, :]).astype(F32)
    buf_ref[...] = _dot(perm.astype(BF16), h_ref[...].astype(BF16))

    src = lambda loc, slot: rows(buf_ref, loc)
    dst = lambda loc, slot: rows(xs_ref, slot)
    _run_copies(meta_ref, src, dst, sem, True)
    _run_copies(meta_ref, src, dst, sem, False)


def _dispatch(meta, zmeta, lpos_t, h2, n_slots, tm_e):
    n_tok = h2.shape[0]
    tm = MOE_TOK
    assert tm_e <= MOE_BUF
    return pl.pallas_call(
        functools.partial(_dispatch_kernel, tm_e=tm_e),
        grid=(n_tok // tm,),
        in_specs=[pl.BlockSpec((MOE_META,), lambda i: (i,), memory_space=pltpu.SMEM),
                  pl.BlockSpec((MOE_META,), lambda i: (0,), memory_space=pltpu.SMEM),
                  pl.BlockSpec((TOP_K, tm), lambda i: (0, i)),
                  pl.BlockSpec((tm, D_MODEL), lambda i: (i, 0))],
        out_specs=pl.BlockSpec(memory_space=pl.ANY),
        out_shape=jax.ShapeDtypeStruct((n_slots, D_MODEL), F32),
        scratch_shapes=[pltpu.VMEM((MOE_BUF, D_MODEL), F32), pltpu.VMEM((MOE_PIECE, D_MODEL), F32),
                        pltpu.SemaphoreType.DMA],
        compiler_params=_cparams(("arbitrary",)),
        name="moe_dispatch",
    )(meta, zmeta, lpos_t, h2)


def _expert_kernel(te_ref, nu_ref, xs_ref, wgu_ref, bgu_ref, wd_ref, bd_ref, y_ref):
    del te_ref

    @pl.when(pl.program_id(0) < nu_ref[0])
    def _():
        gu = _dot(xs_ref[...].astype(BF16), wgu_ref[...]) + bgu_ref[...]
        g = jnp.minimum(gu[:, :D_EXPERT], SWIGLU_LIMIT)
        lin = jnp.clip(gu[:, D_EXPERT:], -SWIGLU_LIMIT, SWIGLU_LIMIT)
        act = g * _sigmoid(SWIGLU_ALPHA * g) * (lin + 1.0)
        y_ref[...] = _dot(act.astype(BF16), wd_ref[...]) + bd_ref[...]

    @pl.when(pl.program_id(0) >= nu_ref[0])
    def _():
        y_ref[...] = jnp.zeros_like(y_ref)


def _experts(tile_expert, n_used, xs, wgu, bgu, wd, bd, tm):
    n_slots = xs.shape[0]
    n_tiles = n_slots // tm

    def row(i, te, nu):
        return (jnp.minimum(i, nu[0] - 1), 0)

    grid_spec = pltpu.PrefetchScalarGridSpec(
        num_scalar_prefetch=2,
        grid=(n_tiles,),
        in_specs=[pl.BlockSpec((tm, D_MODEL), row),
                  pl.BlockSpec((None, D_MODEL, 2 * D_EXPERT), lambda i, te, nu: (te[i], 0, 0)),
                  pl.BlockSpec((None, 1, 2 * D_EXPERT), lambda i, te, nu: (te[i], 0, 0)),
                  pl.BlockSpec((None, D_EXPERT, D_MODEL), lambda i, te, nu: (te[i], 0, 0)),
                  pl.BlockSpec((None, 1, D_MODEL), lambda i, te, nu: (te[i], 0, 0))],
        out_specs=pl.BlockSpec((tm, D_MODEL), lambda i, te, nu: (i, 0)),
    )
    return pl.pallas_call(
        _expert_kernel,
        grid_spec=grid_spec,
        out_shape=jax.ShapeDtypeStruct((n_slots, D_MODEL), F32),
        compiler_params=_cparams(("arbitrary",)),
        name="moe_experts",
    )(tile_expert, n_used, xs, wgu, bgu, wd, bd)


def _combine_kernel(meta_ref, lpos_ref, gate_ref, x_ref, mod_ref, post_ref, y_ref, xo_ref, buf_ref, sem):
    rows = lambda ref, r: ref.at[pl.ds(r, MOE_PIECE)]

    @pl.when(pl.program_id(0) == 0)
    def _():
        buf_ref[...] = jnp.zeros_like(buf_ref)

    src = lambda loc, slot: rows(y_ref, slot)
    dst = lambda loc, slot: rows(buf_ref, loc)
    _run_copies(meta_ref, src, dst, sem, True)
    _run_copies(meta_ref, src, dst, sem, False)

    tm = x_ref.shape[0]
    s_iota = lax.broadcasted_iota(I32, (tm, MOE_BUF), 1)
    sel = jnp.zeros((tm, MOE_BUF), F32)
    for k in range(TOP_K):
        sel = sel + jnp.where(s_iota == lpos_ref[:, k:k + 1], gate_ref[:, k:k + 1], 0.0)
    y = _dot(sel.astype(BF16), buf_ref[...].astype(BF16))
    gate2 = mod_ref[:, 5 * D_MODEL:6 * D_MODEL]
    xo_ref[...] = x_ref[...] + gate2 * _rms(y, post_ref[...])


def _combine(meta, lpos, gate, x_mid, mod, mod_row, post_g, y):
    n_tok = x_mid.shape[0]
    tm = MOE_TOK
    return pl.pallas_call(
        _combine_kernel,
        grid=(n_tok // tm,),
        in_specs=[pl.BlockSpec((MOE_META,), lambda i: (i,), memory_space=pltpu.SMEM),
                  pl.BlockSpec((tm, TOP_K), lambda i: (i, 0)),
                  pl.BlockSpec((tm, TOP_K), lambda i: (i, 0)),
                  pl.BlockSpec((tm, D_MODEL), lambda i: (i, 0)),
                  pl.BlockSpec((None, 1, 6 * D_MODEL), lambda i: (mod_row(i * tm), 0, 0)),
                  pl.BlockSpec((1, D_MODEL), lambda i: (0, 0)),
                  pl.BlockSpec(memory_space=pl.ANY)],
        out_specs=pl.BlockSpec((tm, D_MODEL), lambda i: (i, 0)),
        out_shape=jax.ShapeDtypeStruct((n_tok, D_MODEL), F32),
        scratch_shapes=[pltpu.VMEM((MOE_BUF, D_MODEL), F32), pltpu.SemaphoreType.DMA],
        compiler_params=_cparams(("arbitrary",)),
        name="moe_combine",
    )(meta, lpos, gate, x_mid, mod, post_g, y)


def _moe(h2, logits, x_mid, mod, mod_row, lw, tm_e=512):
    n_tok = h2.shape[0]
    n_t = n_tok // MOE_TOK
    gate, lpos, tcnt = _route(logits)
    tcnt = tcnt[:, 0, :N_EXPERTS].astype(I32)
    pieces = (tcnt + MOE_PIECE - 1) // MOE_PIECE
    run_end = jnp.cumsum(pieces, axis=0) * MOE_PIECE
    used = run_end[-1]
    padded = (used + tm_e - 1) // tm_e * tm_e
    pad_end = jnp.cumsum(padded)
    offs = pad_end - padded
    n_tiles = (n_tok * TOP_K + n_t * N_EXPERTS * (MOE_PIECE - 1) + tm_e - 1) // tm_e + N_EXPERTS
    tile_start = jnp.arange(n_tiles, dtype=I32) * tm_e
    tile_expert = jnp.minimum(jnp.sum(pad_end[None, :] <= tile_start[:, None], axis=1),
                              N_EXPERTS - 1).astype(I32)
    n_used = (pad_end[-1:] // tm_e).astype(I32)
    lstart = (jnp.cumsum(pieces, axis=1) - pieces) * MOE_PIECE
    slot_start = offs[None, :] + run_end - pieces * MOE_PIECE
    meta = jnp.concatenate([lstart, slot_start, pieces, jnp.zeros_like(pieces)],
                           axis=1).reshape(-1).astype(I32)
    z0 = offs + used
    tail = jnp.stack([n_used[0], n_tiles - n_used[0]])
    zmeta = jnp.concatenate([z0, (pad_end - z0) // MOE_PIECE, tail,
                             jnp.zeros((MOE_META - 2 * N_EXPERTS - 2,), I32)]).astype(I32)
    xs = _dispatch(meta, zmeta, lpos.T, h2, n_tiles * tm_e, tm_e)
    y = _experts(tile_expert, n_used, xs, lw["exp_w_gu"], lw["exp_b_gu"], lw["exp_w_down"],
                 lw["exp_b_down"], tm_e)
    return _combine(meta, lpos, gate, x_mid, mod, mod_row, lw["norm_post_ffn"], y)


def _split_w_in(w_in):
    offs, o = {}, 0
    for name, width in (("dn_k", 1024), ("dn_v", 1024), ("dn_a", 16), ("dn_b", 16), ("at_k", 128),
                        ("at_v", 128), ("dn_q", 1024), ("dn_g", 1024), ("at_q", 1024),
                        ("sg_u", 1024), ("sg_v", 1024), ("gates", 3072)):
        offs[name] = (o, o + width)
        o += width
    sl = lambda n: w_in[:, offs[n][0]:offs[n][1]]
    w_main = jnp.concatenate([sl(n) for n in ("dn_k", "dn_v", "dn_q", "dn_g", "at_q", "sg_u", "sg_v",
                                              "gates")], axis=1).astype(BF16)
    pad = jnp.zeros((w_in.shape[0], N_SMALL_COLS - 2 * LANES - 4 * DN_HEADS), w_in.dtype)
    w_small = jnp.concatenate([sl("at_k"), sl("at_v"), sl("dn_a"), sl("dn_b"), pad], axis=1).astype(BF16)
    return w_main, w_small


def _dn_gates(small, B, T):
    ab = small[:, 2 * LANES:2 * LANES + 4 * DN_HEADS].reshape(B, T, 2, 2, DN_HEADS)
    col = jnp.transpose(ab, (3, 0, 1, 2, 4)).reshape(2, B, T, 2 * DN_HEADS)
    row = jnp.transpose(col.reshape(2, B, T // DN_CHUNK, DN_CHUNK, 2 * DN_HEADS), (0, 1, 2, 4, 3))
    return col, row


def kernel(x, c, ctx, c_ctx, w_mod, b_mod, norm_pre_mix, norm_post_mix, norm_pre_ffn, norm_post_ffn, w_in, sg_ln_g, sg_ln_b, sg_w, sg_b, dn_conv_w, dn_a_log, dn_dt_bias, dn_norm_g, at_sinks, w_proj_sg, w_proj_dn, w_proj_at, w_out, router_w, router_b, exp_w_gu, exp_b_gu, exp_w_down, exp_b_down):
    B, S, D = x.shape
    L = ctx.shape[1]
    depth = w_mod.shape[0]
    assert D == D_MODEL and S % GRID_W == 0
    n_lat, n_ctx = B * S, B * L

    rows = (B + 1 + 7) // 8 * 8
    cvec = jnp.zeros((rows, D), F32).at[:B].set(c).at[B].set(c_ctx)
    mod_all = _modulation(cvec, w_mod, b_mod)
    tables = _rope_tables(S)

    lat_row = lambda t: t // S
    ctx_row = lambda t: B
    all_row = lambda t: jnp.where(t < n_lat, t // S, B)

    xl = x.reshape(n_lat, D)
    xc = ctx.reshape(n_ctx, D)
    for l in range(depth):
        need_ctx_out = l < depth - 1
        mod = mod_all[l].reshape(rows, 1, 6 * D)
        w_main, w_small = _split_w_in(w_in[l])
        lw = {
            "dn_norm_g": dn_norm_g[l].reshape(1, -1),
            "norm_post_mix": norm_post_mix[l].reshape(1, -1),
            "norm_pre_ffn": norm_pre_ffn[l].reshape(1, -1),
            "norm_post_ffn": norm_post_ffn[l].reshape(1, -1),
            "w_proj_sg": w_proj_sg[l].astype(BF16), "w_proj_dn": w_proj_dn[l].astype(BF16),
            "w_proj_at": w_proj_at[l].astype(BF16), "w_out": w_out[l].astype(BF16),
            "router_w": jnp.pad(router_w[l], ((0, 0), (0, LANES - N_EXPERTS))).astype(BF16),
            "router_b": jnp.pad(router_b[l], (0, LANES - N_EXPERTS),
                                constant_values=NEG_BIG).reshape(1, -1),
            "exp_w_gu": exp_w_gu[l].astype(BF16), "exp_b_gu": exp_b_gu[l].reshape(N_EXPERTS, 1, -1),
            "exp_w_down": exp_w_down[l].astype(BF16), "exp_b_down": exp_b_down[l].reshape(N_EXPERTS, 1, -1),
        }
        pre_g = norm_pre_mix[l].reshape(1, -1)
        main, small = _inproj(xl, mod, lat_row, pre_g, w_main, w_small, min(1024, S))
        w_main_c = w_main if need_ctx_out else w_main[:, :N_CTX_MAIN_COLS]
        main_c, small_c = _inproj(xc, mod, ctx_row, pre_g, w_main_c, w_small, min(1024, n_ctx))

        sg_args = (sg_ln_g[l].reshape(1, -1), sg_ln_b[l].reshape(1, -1), sg_w[l].astype(BF16),
                   sg_b[l].T)
        ysg = _sgu(main, *sg_args)

        gcol_c, grow_c = _dn_gates(small_c, B, L)
        gcol, grow = _dn_gates(small, B, S)
        s0 = jnp.zeros((2, B, DN_HEADS, DN_DIM, DN_DIM), F32)
        of_c, ob_c, s_ctx = _deltanet(main_c.reshape(B, L, -1), gcol_c, grow_c, dn_conv_w[l], dn_a_log[l],
                                      dn_dt_bias[l], s0, need_ctx_out)
        of_l, ob_l, _ = _deltanet(main.reshape(B, S, -1), gcol, grow, dn_conv_w[l], dn_a_log[l],
                                  dn_dt_bias[l], s_ctx, True)

        sinks = jnp.repeat(at_sinks[l], AT_BLOCK).reshape(-1, 1)
        q_r, k_r, v_r = _rope(main, small, tables, S)
        yat = _attention_local(q_r, k_r, v_r, small_c, sinks, B, S, L)

        x_mid, h2, logits = _merge(ysg, of_l.reshape(n_lat, D), ob_l.reshape(n_lat, D), main, yat, xl, mod,
                                   lat_row, lw)
        if need_ctx_out:
            ysg_c = _sgu(main_c, *sg_args)
            yat_c = _attention_ctx(main_c, small_c, sinks, B, L)
            xc_mid, h2c, logits_c = _merge(ysg_c, of_c.reshape(n_ctx, D), ob_c.reshape(n_ctx, D), main_c,
                                           yat_c, xc, mod, ctx_row, lw)
            x_mid = jnp.concatenate([x_mid, xc_mid], axis=0)
            h2 = jnp.concatenate([h2, h2c], axis=0)
            logits = jnp.concatenate([logits, logits_c], axis=0)
            xo = _moe(h2, logits, x_mid, mod, all_row, lw)
            xl, xc = xo[:n_lat], xo[n_lat:]
        else:
            xl = _moe(h2, logits, x_mid, mod, lat_row, lw)
    return xl.reshape(B, S, D)
```

```python
import functools
import math

import jax
import jax.numpy as jnp
from jax import lax
from jax.experimental import pallas as pl
from jax.experimental.pallas import tpu as pltpu

F32 = jnp.float32
BF16 = jnp.bfloat16
I32 = jnp.int32

EPS = 1e-6
D_MODEL = 1024
GRID_W = 64

SG_CHUNK = 128
SG_GROUPS = 8

DN_HEADS = 8
DN_DIM = 128
DN_CONV = 5
DN_CHUNK = 64
DN_HALO = 16
DN_PREP_CHUNKS = 2
DN_SCAN_CHUNKS = 4

AT_Q_HEADS = 16
AT_KV_HEADS = 2
AT_DIM = 64
AT_BLOCK = 128
ROPE_BASE = 10000.0

N_EXPERTS = 32
TOP_K = 4
D_EXPERT = 1024
SWIGLU_ALPHA = 1.702
SWIGLU_LIMIT = 7.0
N_BRANCH = 3

LANES = 128
NEG_BIG = -1e30

COL_DN_K, COL_DN_V, COL_DN_Q, COL_DN_G, COL_AT_Q, COL_SG_U, COL_SG_V, COL_GATE0 = range(8)
N_MAIN_COLS = 10 * D_MODEL
N_CTX_MAIN_COLS = 2 * D_MODEL
N_SMALL_COLS = 3 * LANES

VMEM_LIMIT = 52 * 1024 * 1024


def _cparams(sem):
    return pltpu.CompilerParams(dimension_semantics=sem, vmem_limit_bytes=VMEM_LIMIT)


def _dot(a, b):
    return jnp.dot(a, b, preferred_element_type=F32)


def _dot_nt(a, b):
    return lax.dot_general(a, b, (((1,), (1,)), ((), ())), preferred_element_type=F32)


def _dot_tn(a, b):
    return lax.dot_general(a, b, (((0,), (0,)), ((), ())), preferred_element_type=F32)


def _sigmoid(x):
    return 1.0 / (1.0 + jnp.exp(-x))


def _silu(x):
    return x * _sigmoid(x)


def _gelu_tanh(x):
    return 0.5 * x * (1.0 + jnp.tanh(math.sqrt(2.0 / math.pi) * (x + 0.044715 * (x * x * x))))


def _softplus(x):
    return jnp.maximum(x, 0.0) + jnp.log(1.0 + jnp.exp(-jnp.abs(x)))


def _rms(x, g):
    return x * lax.rsqrt(jnp.mean(x * x, axis=-1, keepdims=True) + EPS) * g


def _mod_kernel(c_ref, w_ref, b_ref, o_ref):
    s = _silu(c_ref[...])
    o_ref[...] = jnp.dot(s, w_ref[...], preferred_element_type=F32,
                         precision=lax.Precision.HIGHEST) + b_ref[...]


def _modulation(cvec, w_mod, b_mod):
    depth = w_mod.shape[0]
    rows = cvec.shape[0]
    n_col = w_mod.shape[2] // D_MODEL
    return pl.pallas_call(
        _mod_kernel,
        grid=(depth, n_col),
        in_specs=[pl.BlockSpec((rows, D_MODEL), lambda l, j: (0, 0)),
                  pl.BlockSpec((None, D_MODEL, D_MODEL), lambda l, j: (l, 0, j)),
                  pl.BlockSpec((None, 1, D_MODEL), lambda l, j: (l, 0, j))],
        out_specs=pl.BlockSpec((None, rows, D_MODEL), lambda l, j: (l, 0, j)),
        out_shape=jax.ShapeDtypeStruct((depth, rows, w_mod.shape[2]), F32),
        compiler_params=_cparams(("arbitrary", "arbitrary")),
        name="modulation",
    )(cvec, w_mod, b_mod.reshape(depth, 1, -1))


def _inproj_kernel(x_ref, mod_ref, g_ref, wm_ref, ws_ref, main_ref, small_ref, h_ref):
    @pl.when(pl.program_id(1) == 0)
    def _():
        sh = mod_ref[:, 0 * D_MODEL:1 * D_MODEL]
        sc = mod_ref[:, 1 * D_MODEL:2 * D_MODEL]
        h = (_rms(x_ref[...], g_ref[...]) * (1.0 + sc) + sh).astype(BF16)
        h_ref[...] = h
        small_ref[...] = _dot(h, ws_ref[...])

    main_ref[...] = _dot(h_ref[...], wm_ref[...]).astype(BF16)


def _inproj(x, mod, mod_row, norm_g, w_main, w_small, tm, tn=2048):
    n_tok = x.shape[0]
    n_main = w_main.shape[1]
    return pl.pallas_call(
        _inproj_kernel,
        grid=(n_tok // tm, n_main // tn),
        in_specs=[pl.BlockSpec((tm, D_MODEL), lambda i, j: (i, 0)),
                  pl.BlockSpec((None, 1, 6 * D_MODEL), lambda i, j: (mod_row(i * tm), 0, 0)),
                  pl.BlockSpec((1, D_MODEL), lambda i, j: (0, 0)),
                  pl.BlockSpec((D_MODEL, tn), lambda i, j: (0, j)),
                  pl.BlockSpec((D_MODEL, N_SMALL_COLS), lambda i, j: (0, 0))],
        out_specs=[pl.BlockSpec((tm, tn), lambda i, j: (i, j)),
                   pl.BlockSpec((tm, N_SMALL_COLS), lambda i, j: (i, 0))],
        out_shape=[jax.ShapeDtypeStruct((n_tok, n_main), BF16),
                   jax.ShapeDtypeStruct((n_tok, N_SMALL_COLS), F32)],
        scratch_shapes=[pltpu.VMEM((tm, D_MODEL), BF16)],
        compiler_params=_cparams(("arbitrary", "arbitrary")),
        name="inproj",
    )(x, mod, norm_g, w_main, w_small)


def _sgu_kernel(u_ref, v_ref, lng_ref, lnb_ref, ws_ref, bs_ref, o_ref, *, n_chunk):
    u = _gelu_tanh(u_ref[...].astype(F32))
    v = _gelu_tanh(v_ref[...].astype(F32))
    vc = v - jnp.mean(v, axis=-1, keepdims=True)
    var = jnp.mean(vc * vc, axis=-1, keepdims=True)
    vn = (vc * lax.rsqrt(var + EPS) * lng_ref[...] + lnb_ref[...]).astype(BF16)
    for n in range(n_chunk):
        rows = slice(n * SG_CHUNK, (n + 1) * SG_CHUNK)
        for g in range(SG_GROUPS):
            cols = slice(g * LANES, (g + 1) * LANES)
            mixed = _dot(ws_ref[g], vn[rows, cols]) + bs_ref[:, g:g + 1]
            o_ref[rows, cols] = (u[rows, cols] * mixed).astype(BF16)


def _sgu(main, sg_ln_g, sg_ln_b, sg_w, sg_bt, n_chunk=2):
    n_tok = main.shape[0]
    tc = n_chunk * SG_CHUNK
    return pl.pallas_call(
        functools.partial(_sgu_kernel, n_chunk=n_chunk),
        grid=(n_tok // tc,),
        in_specs=[pl.BlockSpec((tc, D_MODEL), lambda i: (i, COL_SG_U)),
                  pl.BlockSpec((tc, D_MODEL), lambda i: (i, COL_SG_V)),
                  pl.BlockSpec((1, D_MODEL), lambda i: (0, 0)),
                  pl.BlockSpec((1, D_MODEL), lambda i: (0, 0)),
                  pl.BlockSpec((SG_GROUPS, SG_CHUNK, SG_CHUNK), lambda i: (0, 0, 0)),
                  pl.BlockSpec((SG_CHUNK, SG_GROUPS), lambda i: (0, 0))],
        out_specs=pl.BlockSpec((tc, D_MODEL), lambda i: (i, 0)),
        out_shape=jax.ShapeDtypeStruct((n_tok, D_MODEL), BF16),
        compiler_params=_cparams(("arbitrary",)),
        name="sgu",
    )(main, main, sg_ln_g, sg_ln_b, sg_w, sg_bt)


def _dn_kernel(*refs, with_q, n_chunks):
    s_refs = refs[-DN_HEADS:]
    ext_refs = refs[-DN_HEADS - 3:-DN_HEADS]
    refs = refs[:-DN_HEADS - 3]
    if with_q:
        (qp_ref, qc_ref, qn_ref, kp_ref, kc_ref, kn_ref, vp_ref, vc_ref, vn_ref,
         gcol_ref, grow_ref, cw_ref, alog_r_ref, alog_c_ref, dtb_r_ref, dtb_c_ref, s0_ref,
         o_ref, sfin_ref) = refs
    else:
        (kp_ref, kc_ref, kn_ref, vp_ref, vc_ref, vn_ref,
         gcol_ref, grow_ref, cw_ref, alog_r_ref, alog_c_ref, dtb_r_ref, dtb_c_ref, s0_ref,
         sfin_ref) = refs
    d = pl.program_id(0)
    c = pl.program_id(2)
    is_fwd = d == 0
    cidx = jnp.where(is_fwd, c, n_chunks - 1 - c)
    C = DN_CHUNK

    @pl.when(c == 0)
    def _():
        for h in range(DN_HEADS):
            s_refs[h][...] = s0_ref[h]

    has_prev = (cidx > 0).astype(F32)
    has_next = (cidx < n_chunks - 1).astype(F32)

    def conv_silu(p_ref, c_ref, n_ref, part):
        ext_ref = ext_refs[part]
        ext_ref[0:DN_HALO, :] = p_ref[...].astype(F32) * has_prev
        ext_ref[DN_HALO:DN_HALO + C, :] = c_ref[...].astype(F32)
        ext_ref[DN_HALO + C:2 * DN_HALO + C, :] = n_ref[...].astype(F32) * has_next
        base = DN_HALO - DN_CONV // 2
        y = None
        for i in range(DN_CONV):
            w = cw_ref[i:i + 1, part * D_MODEL:(part + 1) * D_MODEL]
            t = ext_ref[base + i:base + i + C, :] * w
            y = t if y is None else y + t
        return _silu(y)

    k_all = conv_silu(kp_ref, kc_ref, kn_ref, 1)
    v_all = conv_silu(vp_ref, vc_ref, vn_ref, 2)
    q_all = conv_silu(qp_ref, qc_ref, qn_ref, 0) if with_q else None

    gcol = gcol_ref[...]
    ld_col = -jnp.exp(alog_r_ref[...]) * _softplus(gcol[:, 0:DN_HEADS] + dtb_r_ref[...])
    beta_col = _sigmoid(gcol[:, DN_HEADS:2 * DN_HEADS])
    ld_row = -jnp.exp(alog_c_ref[...]) * _softplus(grow_ref[0:DN_HEADS, :] + dtb_c_ref[...])

    ri = lax.broadcasted_iota(I32, (C, C), 0)
    ci = lax.broadcasted_iota(I32, (C, C), 1)
    delta = (ri - ci) * (1 - 2 * d)
    incl = delta >= 0
    strict = delta > 0
    incl_t = delta <= 0
    gam_col = jnp.dot(incl.astype(F32), ld_col, preferred_element_type=F32,
                      precision=lax.Precision.HIGHEST)
    gam_row = jnp.dot(ld_row, incl_t.astype(F32), preferred_element_type=F32,
                      precision=lax.Precision.HIGHEST)
    gam_tot = jnp.sum(ld_col, axis=0, keepdims=True)
    eye = (ri == ci).astype(F32)

    H = range(DN_HEADS)
    lanes = [slice(h * DN_DIM, (h + 1) * DN_DIM) for h in H]
    gc = [gam_col[:, h:h + 1] for h in H]
    bc = [beta_col[:, h:h + 1] for h in H]
    kh = [k_all[:, lanes[h]] for h in H]
    kh = [kh[h] * lax.rsqrt(jnp.sum(kh[h] * kh[h], axis=-1, keepdims=True) + EPS) for h in H]
    kb = [kh[h].astype(BF16) for h in H]
    decay = [jnp.exp(jnp.where(incl, gc[h] - gam_row[h:h + 1, :], NEG_BIG)) for h in H]
    kk = [_dot_nt(kb[h], kb[h]) for h in H]
    x = [-(jnp.where(strict, decay[h], 0.0) * bc[h] * kk[h]) for h in H]
    p = [eye + x[h] for h in H]
    xb = [x[h].astype(BF16) for h in H]
    x = [_dot(xb[h], xb[h]) for h in H]
    n_fac = int(math.log2(C)) - 1
    for j in range(n_fac):
        xb = [x[h].astype(BF16) for h in H]
        if j < n_fac - 1:
            r = [_dot(xb[h], jnp.concatenate([xb[h], p[h].astype(BF16)], axis=1)) for h in H]
            x = [r[h][:, :C] for h in H]
            p = [p[h] + r[h][:, C:] for h in H]
        else:
            p = [p[h] + _dot(xb[h], p[h].astype(BF16)) for h in H]
    rhs = [jnp.concatenate([kh[h] * (bc[h] * jnp.exp(gc[h])), v_all[:, lanes[h]] * bc[h]],
                           axis=1).astype(BF16) for h in H]
    sol = [_dot(p[h].astype(BF16), rhs[h]) for h in H]
    w = [sol[h][:, :DN_DIM] for h in H]
    u0 = [sol[h][:, DN_DIM:] for h in H]
    k_end = [(kh[h] * jnp.exp(gam_tot[:, h:h + 1] - gc[h])).astype(BF16) for h in H]
    s = [s_refs[h][...] for h in H]
    sb = [s[h].astype(BF16) for h in H]
    if with_q:
        qh = [q_all[:, lanes[h]] for h in H]
        qh = [qh[h] * (lax.rsqrt(jnp.sum(qh[h] * qh[h], axis=-1, keepdims=True) + EPS) * DN_DIM ** -0.5)
              for h in H]
        qk = [(_dot_nt(qh[h].astype(BF16), kb[h]) * decay[h]).astype(BF16) for h in H]
        wq = [jnp.concatenate([w[h], qh[h] * jnp.exp(gc[h])], axis=0).astype(BF16) for h in H]
        ws = [_dot(wq[h], sb[h]) for h in H]
        ub = [(u0[h] - ws[h][:C]).astype(BF16) for h in H]
        qu = [_dot(qk[h], ub[h]) for h in H]
        for h in H:
            o_ref[:, lanes[h]] = (ws[h][C:] + qu[h]).astype(o_ref.dtype)
    else:
        ws = [_dot(w[h].astype(BF16), sb[h]) for h in H]
        ub = [(u0[h] - ws[h]).astype(BF16) for h in H]
    ku = [_dot_tn(k_end[h], ub[h]) for h in H]
    for h in H:
        s_refs[h][...] = jnp.exp(gam_tot[:, h:h + 1]) * s[h] + ku[h]

    @pl.when(c == n_chunks - 1)
    def _():
        for h in range(DN_HEADS):
            sfin_ref[h] = s_refs[h][...]


def _deltanet_single_pass(main3, gate_col, gate_row, conv_w, alog, dtb, s0, with_q):
    B, T, _ = main3.shape
    C = DN_CHUNK
    n_chunks = T // C
    hpc = C // DN_HALO
    n_halo = T // DN_HALO

    def cix(d, c):
        return jnp.where(d == 0, c, n_chunks - 1 - c)

    def trio(col):
        return [pl.BlockSpec((None, DN_HALO, D_MODEL),
                             lambda d, b, c: (b, jnp.maximum(cix(d, c) * hpc - 1, 0), col)),
                pl.BlockSpec((None, C, D_MODEL), lambda d, b, c: (b, cix(d, c), col)),
                pl.BlockSpec((None, DN_HALO, D_MODEL),
                             lambda d, b, c: (b, jnp.minimum((cix(d, c) + 1) * hpc, n_halo - 1), col))]

    in_specs = (trio(COL_DN_Q) if with_q else []) + trio(COL_DN_K) + trio(COL_DN_V) + [
        pl.BlockSpec((None, None, C, 2 * DN_HEADS), lambda d, b, c: (d, b, cix(d, c), 0)),
        pl.BlockSpec((None, None, None, 2 * DN_HEADS, C), lambda d, b, c: (d, b, cix(d, c), 0, 0)),
        pl.BlockSpec((DN_CONV, 3 * D_MODEL), lambda d, b, c: (0, 0)),
        pl.BlockSpec((None, 1, DN_HEADS), lambda d, b, c: (d, 0, 0)),
        pl.BlockSpec((None, DN_HEADS, 1), lambda d, b, c: (d, 0, 0)),
        pl.BlockSpec((None, 1, DN_HEADS), lambda d, b, c: (d, 0, 0)),
        pl.BlockSpec((None, DN_HEADS, 1), lambda d, b, c: (d, 0, 0)),
        pl.BlockSpec((None, None, DN_HEADS, DN_DIM, DN_DIM), lambda d, b, c: (d, b, 0, 0, 0)),
    ]
    s_spec = pl.BlockSpec((None, None, DN_HEADS, DN_DIM, DN_DIM), lambda d, b, c: (d, b, 0, 0, 0))
    s_shape = jax.ShapeDtypeStruct((2, B, DN_HEADS, DN_DIM, DN_DIM), F32)
    if with_q:
        out_specs = [pl.BlockSpec((None, None, C, D_MODEL), lambda d, b, c: (d, b, cix(d, c), 0)), s_spec]
        out_shape = [jax.ShapeDtypeStruct((2, B, T, D_MODEL), BF16), s_shape]
    else:
        out_specs = [s_spec]
        out_shape = [s_shape]
    n_main = 3 if with_q else 2
    args = [main3] * (3 * n_main) + [
        gate_col, gate_row, conv_w,
        alog.reshape(2, 1, DN_HEADS), alog.reshape(2, DN_HEADS, 1),
        dtb.reshape(2, 1, DN_HEADS), dtb.reshape(2, DN_HEADS, 1), s0]
    out = pl.pallas_call(
        functools.partial(_dn_kernel, with_q=with_q, n_chunks=n_chunks),
        grid=(2, B, n_chunks),
        in_specs=in_specs, out_specs=out_specs, out_shape=out_shape,
        scratch_shapes=[pltpu.VMEM((C + 2 * DN_HALO, D_MODEL), F32)] * 3
        + [pltpu.VMEM((DN_DIM, DN_DIM), F32)] * DN_HEADS,
        compiler_params=_cparams(("arbitrary", "arbitrary", "arbitrary")),
        name="deltanet_q" if with_q else "deltanet_state",
    )(*args)
    return (out[0], out[1]) if with_q else (None, out[0])


def _dn_prep_kernel(*refs, with_q, n_chunks):
    if with_q:
        (qp_ref, qc_ref, qn_ref, kp_ref, kc_ref, kn_ref, vp_ref, vc_ref, vn_ref,
         gcol_ref, grow_ref, cw_ref, alog_r_ref, alog_c_ref, dtb_r_ref, dtb_c_ref,
         w_ref, u0_ref, ke_ref, gt_ref, qs_ref, qk_ref) = refs
    else:
        (kp_ref, kc_ref, kn_ref, vp_ref, vc_ref, vn_ref,
         gcol_ref, grow_ref, cw_ref, alog_r_ref, alog_c_ref, dtb_r_ref, dtb_c_ref,
         w_ref, u0_ref, ke_ref, gt_ref) = refs
    c = pl.program_id(1)
    C = DN_CHUNK
    R = DN_PREP_CHUNKS * C
    has_prev = (c > 0).astype(BF16)
    has_next = (c < n_chunks // DN_PREP_CHUNKS - 1).astype(BF16)

    pad = DN_CONV // 2
    n_sh = DN_CONV - 1
    sr = lax.broadcasted_iota(I32, (n_sh * R, R + 2 * DN_HALO), 0)
    sc = lax.broadcasted_iota(I32, (n_sh * R, R + 2 * DN_HALO), 1)
    blk = sr // R
    off = jnp.where(blk < pad, blk - pad, blk - pad + 1)
    shift_mat = (sc == DN_HALO + (sr - blk * R) + off).astype(BF16)

    def conv_silu(p_ref, c_ref, n_ref, part):
        cur = c_ref[...]
        ext = jnp.concatenate([p_ref[...] * has_prev, cur, n_ref[...] * has_next], axis=0)
        sh = _dot(shift_mat, ext)
        taps = [sh[j * R:(j + 1) * R] for j in range(pad)] + [cur.astype(F32)] + \
               [sh[j * R:(j + 1) * R] for j in range(pad, n_sh)]
        y = None
        for i in range(DN_CONV):
            t = taps[i] * cw_ref[i:i + 1, part * D_MODEL:(part + 1) * D_MODEL]
            y = t if y is None else y + t
        return _silu(y)

    k_all = conv_silu(kp_ref, kc_ref, kn_ref, 1)
    v_all = conv_silu(vp_ref, vc_ref, vn_ref, 2)
    q_all = conv_silu(qp_ref, qc_ref, qn_ref, 0) if with_q else None

    ri = lax.broadcasted_iota(I32, (C, C), 0)
    ci = lax.broadcasted_iota(I32, (C, C), 1)
    eye = (ri == ci).astype(F32)
    CC = range(DN_PREP_CHUNKS)
    H = range(DN_HEADS)
    CH = [(cc, h) for cc in CC for h in H]
    ch = {key: i for i, key in enumerate(CH)}
    rows = [slice(cc * C, (cc + 1) * C) for cc in CC]
    lanes = [slice(h * DN_DIM, (h + 1) * DN_DIM) for h in H]
    kh = [k_all[rows[cc], lanes[h]] for cc, h in CH]
    kh = [k * lax.rsqrt(jnp.sum(k * k, axis=-1, keepdims=True) + EPS) for k in kh]
    kb = [k.astype(BF16) for k in kh]
    vh = [v_all[rows[cc], lanes[h]] for cc, h in CH]
    if with_q:
        qh = [q_all[rows[cc], lanes[h]] for cc, h in CH]
        qh = [q * (lax.rsqrt(jnp.sum(q * q, axis=-1, keepdims=True) + EPS) * DN_DIM ** -0.5) for q in qh]
        gram = [_dot_nt(jnp.concatenate([kb[j], qh[j].astype(BF16)], axis=0), kb[j]) for j in range(len(CH))]
        kk = [g[:C] for g in gram]
        qk_raw = [g[C:] for g in gram]
    else:
        kk = [_dot_nt(k, k) for k in kb]

    D2 = range(2)
    DH = [(cc, d, h) for cc in CC for d in D2 for h in H]
    incl = [(ri >= ci), (ri <= ci)]
    strict = [(ri > ci), (ri < ci)]
    gam_col, gam_row, gam_tot, beta_col = {}, {}, {}, {}
    for cc in CC:
        for d in D2:
            gcol = gcol_ref[d, rows[cc], :]
            ld_col = -jnp.exp(alog_r_ref[d]) * _softplus(gcol[:, 0:DN_HEADS] + dtb_r_ref[d])
            ld_row = -jnp.exp(alog_c_ref[d]) * _softplus(grow_ref[d, cc][0:DN_HEADS, :] + dtb_c_ref[d])
            beta_col[cc, d] = _sigmoid(gcol[:, DN_HEADS:2 * DN_HEADS])
            gam_col[cc, d] = jnp.dot(incl[d].astype(F32), ld_col, preferred_element_type=F32,
                                     precision=lax.Precision.HIGHEST)
            gam_row[cc, d] = jnp.dot(ld_row, incl[1 - d].astype(F32), preferred_element_type=F32,
                                     precision=lax.Precision.HIGHEST)
            tot = jnp.sum(ld_col, axis=0, keepdims=True)
            gam_tot[cc, d] = tot
            gt_ref[d, cc] = tot
    gc = [gam_col[cc, d][:, h:h + 1] for cc, d, h in DH]
    bc = [beta_col[cc, d][:, h:h + 1] for cc, d, h in DH]
    decay = [jnp.exp(jnp.where(incl[d], gc[i] - gam_row[cc, d][h:h + 1, :], NEG_BIG))
             for i, (cc, d, h) in enumerate(DH)]
    x = [-(jnp.where(strict[d], decay[i], 0.0) * bc[i] * kk[ch[cc, h]]) for i, (cc, d, h) in enumerate(DH)]
    N = range(len(DH))
    p = [eye + x[i] for i in N]
    xb = [x[i].astype(BF16) for i in N]
    x = [_dot(xb[i], xb[i]) for i in N]
    n_fac = int(math.log2(C)) - 1
    for j in range(n_fac):
        xb = [x[i].astype(BF16) for i in N]
        if j < n_fac - 1:
            r = [_dot(xb[i], jnp.concatenate([xb[i], p[i].astype(BF16)], axis=1)) for i in N]
            x = [r[i][:, :C] for i in N]
            p = [p[i] + r[i][:, C:] for i in N]
        else:
            p = [p[i] + _dot(xb[i], p[i].astype(BF16)) for i in N]
    rhs = [jnp.concatenate([kh[ch[cc, h]] * (bc[i] * jnp.exp(gc[i])), vh[ch[cc, h]] * bc[i]],
                           axis=1).astype(BF16) for i, (cc, d, h) in enumerate(DH)]
    sol = [_dot(p[i].astype(BF16), rhs[i]) for i in N]
    for i, (cc, d, h) in enumerate(DH):
        j = ch[cc, h]
        w_ref[d, rows[cc], lanes[h]] = sol[i][:, :DN_DIM].astype(BF16)
        u0_ref[d, rows[cc], lanes[h]] = sol[i][:, DN_DIM:].astype(BF16)
        ke_ref[d, rows[cc], lanes[h]] = (kh[j] * jnp.exp(gam_tot[cc, d][:, h:h + 1] - gc[i])).astype(BF16)
        if with_q:
            qs_ref[d, rows[cc], lanes[h]] = (qh[j] * jnp.exp(gc[i])).astype(BF16)
            qk_ref[d, rows[cc], h * C:(h + 1) * C] = (qk_raw[j] * decay[i]).astype(BF16)


def _dn_scan_kernel(*refs, with_q, n_chunks):
    n_state = 2 * DN_HEADS
    s_refs = refs[-n_state:]
    refs = refs[:-n_state]
    if with_q:
        (w0, w1, u0, u1, k0, k1, g0, g1, qs0, qs1, qk0, qk1, s0_ref, o0_ref, o1_ref, sfin_ref) = refs
        qs_r, qk_r, o_r = (qs0, qs1), (qk0, qk1), (o0_ref, o1_ref)
    else:
        (w0, w1, u0, u1, k0, k1, g0, g1, s0_ref, sfin_ref) = refs
    w_r, u_r, k_r, g_r = (w0, w1), (u0, u1), (k0, k1), (g0, g1)
    c = pl.program_id(1)
    C = DN_CHUNK
    DH = [(d, h) for d in range(2) for h in range(DN_HEADS)]
    N = range(len(DH))
    lanes = [slice(h * DN_DIM, (h + 1) * DN_DIM) for h in range(DN_HEADS)]

    @pl.when(c == 0)
    def _():
        for i, (d, h) in enumerate(DH):
            s_refs[i][...] = s0_ref[d, h]

    s = [s_refs[i][...] for i in N]
    for sub in range(DN_SCAN_CHUNKS):
        cix = (sub, DN_SCAN_CHUNKS - 1 - sub)
        rows = [slice(cix[d] * C, (cix[d] + 1) * C) for d in range(2)]
        sb = [s[i].astype(BF16) for i in N]
        w = [w_r[d][rows[d], lanes[h]] for d, h in DH]
        if with_q:
            wq = [jnp.concatenate([w[i], qs_r[d][rows[d], lanes[h]]], axis=0) for i, (d, h) in enumerate(DH)]
            ws = [_dot(wq[i], sb[i]) for i in N]
            ub = [(u_r[d][rows[d], lanes[h]].astype(F32) - ws[i][:C]).astype(BF16)
                  for i, (d, h) in enumerate(DH)]
            qu = [_dot(qk_r[d][rows[d], h * C:(h + 1) * C], ub[i]) for i, (d, h) in enumerate(DH)]
            for i, (d, h) in enumerate(DH):
                o_r[d][rows[d], lanes[h]] = (ws[i][C:] + qu[i]).astype(BF16)
        else:
            ws = [_dot(w[i], sb[i]) for i in N]
            ub = [(u_r[d][rows[d], lanes[h]].astype(F32) - ws[i]).astype(BF16) for i, (d, h) in enumerate(DH)]
        ku = [_dot_tn(k_r[d][rows[d], lanes[h]], ub[i]) for i, (d, h) in enumerate(DH)]
        s = [jnp.exp(g_r[d][cix[d]][:, h:h + 1]) * s[i] + ku[i] for i, (d, h) in enumerate(DH)]
    for i in N:
        s_refs[i][...] = s[i]

    @pl.when(c == n_chunks // DN_SCAN_CHUNKS - 1)
    def _():
        for i, (d, h) in enumerate(DH):
            sfin_ref[d, h] = s_refs[i][...]


def _deltanet(main3, gate_col, gate_row, conv_w, alog, dtb, s0, with_q):
    B, T, _ = main3.shape
    C = DN_CHUNK
    n_chunks = T // C
    NC = DN_PREP_CHUNKS
    R = NC * C
    assert T % R == 0
    hpc = R // DN_HALO
    n_halo = T // DN_HALO

    def trio(col):
        return [pl.BlockSpec((None, DN_HALO, D_MODEL), lambda b, c: (b, jnp.maximum(c * hpc - 1, 0), col)),
                pl.BlockSpec((None, R, D_MODEL), lambda b, c: (b, c, col)),
                pl.BlockSpec((None, DN_HALO, D_MODEL),
                             lambda b, c: (b, jnp.minimum((c + 1) * hpc, n_halo - 1), col))]

    vec = lambda shape: pl.BlockSpec(shape, lambda b, c: (0,) * len(shape))
    in_specs = (trio(COL_DN_Q) if with_q else []) + trio(COL_DN_K) + trio(COL_DN_V) + [
        pl.BlockSpec((2, None, R, 2 * DN_HEADS), lambda b, c: (0, b, c, 0)),
        pl.BlockSpec((2, None, NC, 2 * DN_HEADS, C), lambda b, c: (0, b, c, 0, 0)),
        vec((DN_CONV, 3 * D_MODEL)),
        vec((2, 1, DN_HEADS)), vec((2, DN_HEADS, 1)), vec((2, 1, DN_HEADS)), vec((2, DN_HEADS, 1)),
    ]
    wide = lambda n: (pl.BlockSpec((2, None, R, n), lambda b, c: (0, b, c, 0)),
                      jax.ShapeDtypeStruct((2, B, T, n), BF16))
    outs = [wide(D_MODEL), wide(D_MODEL), wide(D_MODEL),
            (pl.BlockSpec((2, None, NC, 1, DN_HEADS), lambda b, c: (0, b, c, 0, 0)),
             jax.ShapeDtypeStruct((2, B, n_chunks, 1, DN_HEADS), F32))]
    if with_q:
        outs += [wide(D_MODEL), wide(DN_HEADS * C)]
    n_main = 3 if with_q else 2
    prep = pl.pallas_call(
        functools.partial(_dn_prep_kernel, with_q=with_q, n_chunks=n_chunks),
        grid=(B, n_chunks // NC),
        in_specs=in_specs, out_specs=[o[0] for o in outs], out_shape=[o[1] for o in outs],
        compiler_params=_cparams(("arbitrary", "arbitrary")),
        name="dn_prep_q" if with_q else "dn_prep",
    )(*([main3] * (3 * n_main)), gate_col, gate_row, conv_w,
      alog.reshape(2, 1, DN_HEADS), alog.reshape(2, DN_HEADS, 1),
      dtb.reshape(2, 1, DN_HEADS), dtb.reshape(2, DN_HEADS, 1))

    NS = DN_SCAN_CHUNKS
    RS = NS * C
    n_steps = n_chunks // NS
    assert n_chunks % NS == 0

    def both_dirs(arr, n):
        if n is None:
            return [pl.BlockSpec((None, None, NS, 1, DN_HEADS), lambda b, c: (0, b, c, 0, 0)),
                    pl.BlockSpec((None, None, NS, 1, DN_HEADS),
                                 lambda b, c: (1, b, n_steps - 1 - c, 0, 0))], [arr, arr]
        return [pl.BlockSpec((None, None, RS, n), lambda b, c: (0, b, c, 0)),
                pl.BlockSpec((None, None, RS, n), lambda b, c: (1, b, n_steps - 1 - c, 0))], [arr, arr]

    specs, args = [], []
    widths = [D_MODEL, D_MODEL, D_MODEL, None] + ([D_MODEL, DN_HEADS * C] if with_q else [])
    for arr, n in zip(prep, widths):
        sp, ar = both_dirs(arr, n)
        specs += sp
        args += ar
    s_spec = pl.BlockSpec((2, None, DN_HEADS, DN_DIM, DN_DIM), lambda b, c: (0, b, 0, 0, 0))
    s_shape = jax.ShapeDtypeStruct((2, B, DN_HEADS, DN_DIM, DN_DIM), F32)
    if with_q:
        out_specs = [pl.BlockSpec((None, RS, D_MODEL), lambda b, c: (b, c, 0)),
                     pl.BlockSpec((None, RS, D_MODEL), lambda b, c: (b, n_steps - 1 - c, 0)), s_spec]
        out_shape = [jax.ShapeDtypeStruct((B, T, D_MODEL), BF16)] * 2 + [s_shape]
    else:
        out_specs, out_shape = [s_spec], [s_shape]
    out = pl.pallas_call(
        functools.partial(_dn_scan_kernel, with_q=with_q, n_chunks=n_chunks),
        grid=(B, n_steps),
        in_specs=specs + [s_spec], out_specs=out_specs, out_shape=out_shape,
        scratch_shapes=[pltpu.VMEM((DN_DIM, DN_DIM), F32)] * (2 * DN_HEADS),
        compiler_params=_cparams(("arbitrary", "arbitrary")),
        name="dn_scan_q" if with_q else "dn_scan",
    )(*args, s0)
    return (out[0], out[1], out[2]) if with_q else (None, None, out[0])


def _rope_tables(S):
    half = AT_DIM // 2
    nf = half // 2
    inv_freq = ROPE_BASE ** (-jnp.arange(nf, dtype=F32) / nf)
    t = jnp.arange(S, dtype=jnp.int32)
    row = (t // GRID_W).astype(F32)
    col = (t % GRID_W).astype(F32)
    lane = jnp.arange(LANES)
    dd = lane % AT_DIM
    pos = jnp.where((dd < half)[None, :], row[:, None], col[:, None])
    ang = pos * inv_freq[lane % nf][None, :]
    first = ((lane % half) < nf)[None, :]
    sin = jnp.sin(ang)
    return jnp.cos(ang), jnp.where(first, -sin, 0.0), jnp.where(first, 0.0, sin)


def _rope_kernel(q_ref, k_ref, v_ref, cos_ref, sa_ref, sb_ref, qo_ref, ko_ref, vo_ref):
    cos, sa, sb = cos_ref[...], sa_ref[...], sb_ref[...]
    nf = AT_DIM // 4

    def rot(x):
        return x * cos + pltpu.roll(x, LANES - nf, 1) * sa + pltpu.roll(x, nf, 1) * sb

    for j in range(AT_Q_HEADS * AT_DIM // LANES):
        lanes = slice(j * LANES, (j + 1) * LANES)
        qo_ref[:, lanes] = (rot(q_ref[:, lanes].astype(F32)) * AT_DIM ** -0.5).astype(BF16)
    ko_ref[...] = rot(k_ref[...]).astype(BF16)
    vo_ref[...] = v_ref[...].astype(BF16)


def _rope(main, small, tables, S, tm=512):
    n_tok = main.shape[0]
    per_seq = S // tm
    tab_spec = pl.BlockSpec((tm, LANES), lambda i: (i % per_seq, 0))
    return pl.pallas_call(
        _rope_kernel,
        grid=(n_tok // tm,),
        in_specs=[pl.BlockSpec((tm, D_MODEL), lambda i: (i, COL_AT_Q)),
                  pl.BlockSpec((tm, LANES), lambda i: (i, 0)),
                  pl.BlockSpec((tm, LANES), lambda i: (i, 1)),
                  tab_spec, tab_spec, tab_spec],
        out_specs=[pl.BlockSpec((tm, D_MODEL), lambda i: (i, 0)),
                   pl.BlockSpec((tm, LANES), lambda i: (i, 0)),
                   pl.BlockSpec((tm, LANES), lambda i: (i, 0))],
        out_shape=[jax.ShapeDtypeStruct((n_tok, D_MODEL), BF16),
                   jax.ShapeDtypeStruct((n_tok, LANES), BF16),
                   jax.ShapeDtypeStruct((n_tok, LANES), BF16)],
        compiler_params=_cparams(("arbitrary",)),
        name="rope",
    )(main, small, small, *tables)


def _attn_kernel(*refs, local, n_blocks, q_scale):
    if local:
        (q_ref, kp_ref, kc_ref, kn_ref, vp_ref, vc_ref, vn_ref, kx_ref, vx_ref, sink_ref, o_ref) = refs
    else:
        (q_ref, kx_ref, vx_ref, sink_ref, o_ref) = refs
    P = AT_BLOCK
    G = AT_Q_HEADS // AT_KV_HEADS
    L = kx_ref.shape[0]
    kx = kx_ref[...].astype(BF16)
    vx = vx_ref[...].astype(BF16)
    if local:
        i = pl.program_id(1)
        k_all = jnp.concatenate([kp_ref[...], kc_ref[...], kn_ref[...], kx], axis=0)
        v_all = jnp.concatenate([vp_ref[...], vc_ref[...], vn_ref[...], vx], axis=0)
        qi = lax.broadcasted_iota(I32, (P, P), 0)
        kj = lax.broadcasted_iota(I32, (P, P), 1)
        b_prev = jnp.where(kj >= qi, 0.0, NEG_BIG) + jnp.where(i > 0, 0.0, NEG_BIG)
        b_next = jnp.where(kj <= qi, 0.0, NEG_BIG) + jnp.where(i < n_blocks - 1, 0.0, NEG_BIG)
        bias = jnp.concatenate([b_prev, jnp.zeros((P, P), F32), b_next, jnp.zeros((P, L), F32)], axis=1)
    else:
        k_all, v_all, bias = kx, vx, None
    n_keys = k_all.shape[0]
    lo = lax.broadcasted_iota(I32, (P, LANES), 1) < AT_DIM
    qf = q_ref[...].astype(F32) * q_scale
    pieces = []
    for qh in range(AT_Q_HEADS):
        blk = qf[:, (qh // 2) * LANES:(qh // 2 + 1) * LANES]
        want_lo = qh // G == 0
        if want_lo != (qh % 2 == 0):
            blk = pltpu.roll(blk, AT_DIM, 1)
        pieces.append(jnp.where(lo if want_lo else ~lo, blk, 0.0).astype(BF16))
    qs = jnp.concatenate(pieces, axis=0)
    s = _dot_nt(qs, k_all)
    if bias is not None:
        s = (s.reshape(AT_Q_HEADS, P, n_keys) + bias[None]).reshape(AT_Q_HEADS * P, n_keys)
    sink = sink_ref[...]
    m = jnp.maximum(jnp.max(s, axis=-1, keepdims=True), sink)
    p = jnp.exp(s - m)
    den = jnp.sum(p, axis=-1, keepdims=True) + jnp.exp(sink - m)
    o = _dot(p.astype(BF16), v_all) / den
    for j in range(AT_Q_HEADS // 2):
        a = o[(2 * j) * P:(2 * j + 1) * P]
        b = o[(2 * j + 1) * P:(2 * j + 2) * P]
        if (2 * j) // G == 0:
            out = jnp.where(lo, a, pltpu.roll(b, AT_DIM, 1))
        else:
            out = jnp.where(lo, pltpu.roll(a, AT_DIM, 1), b)
        o_ref[:, j * LANES:(j + 1) * LANES] = out.astype(BF16)


def _attention_local(q_r, k_r, v_r, small_c, sinks, B, S, L):
    P = AT_BLOCK
    nb = S // P

    def kv_trio():
        return [pl.BlockSpec((P, LANES), lambda b, i: (b * nb + jnp.maximum(i - 1, 0), 0)),
                pl.BlockSpec((P, LANES), lambda b, i: (b * nb + i, 0)),
                pl.BlockSpec((P, LANES), lambda b, i: (b * nb + jnp.minimum(i + 1, nb - 1), 0))]

    return pl.pallas_call(
        functools.partial(_attn_kernel, local=True, n_blocks=nb, q_scale=1.0),
        grid=(B, nb),
        in_specs=[pl.BlockSpec((P, D_MODEL), lambda b, i: (b * nb + i, 0))] + kv_trio() + kv_trio() + [
            pl.BlockSpec((L, LANES), lambda b, i: (b, 0)),
            pl.BlockSpec((L, LANES), lambda b, i: (b, 1)),
            pl.BlockSpec((AT_Q_HEADS * AT_BLOCK, 1), lambda b, i: (0, 0))],
        out_specs=pl.BlockSpec((P, D_MODEL), lambda b, i: (b * nb + i, 0)),
        out_shape=jax.ShapeDtypeStruct((B * S, D_MODEL), BF16),
        compiler_params=_cparams(("arbitrary", "arbitrary")),
        name="attn_local",
    )(q_r, k_r, k_r, k_r, v_r, v_r, v_r, small_c, small_c, sinks)


def _attention_ctx(main_c, small_c, sinks, B, L):
    P = AT_BLOCK
    nb = L // P
    return pl.pallas_call(
        functools.partial(_attn_kernel, local=False, n_blocks=nb, q_scale=AT_DIM ** -0.5),
        grid=(B, nb),
        in_specs=[pl.BlockSpec((P, D_MODEL), lambda b, i: (b * nb + i, COL_AT_Q)),
                  pl.BlockSpec((L, LANES), lambda b, i: (b, 0)),
                  pl.BlockSpec((L, LANES), lambda b, i: (b, 1)),
                  pl.BlockSpec((AT_Q_HEADS * AT_BLOCK, 1), lambda b, i: (0, 0))],
        out_specs=pl.BlockSpec((P, D_MODEL), lambda b, i: (b * nb + i, 0)),
        out_shape=jax.ShapeDtypeStruct((B * L, D_MODEL), BF16),
        compiler_params=_cparams(("arbitrary", "arbitrary")),
        name="attn_ctx",
    )(main_c, small_c, small_c, sinks)


def _merge_kernel(ysg_ref, of_ref, ob_ref, dng_ref, yat_ref, g0_ref, g1_ref, g2_ref, x_ref, mod_ref,
                  dn_norm_ref, post_ref, pre_ref, wsg_ref, wdn_ref, wat_ref, wout_ref, rw_ref, rb_ref,
                  xo_ref, h2_ref, lg_ref):
    o = of_ref[...].astype(F32) + ob_ref[...].astype(F32)
    dn_g = dn_norm_ref[...]
    parts = []
    for h in range(DN_HEADS):
        lanes = slice(h * DN_DIM, (h + 1) * DN_DIM)
        parts.append(_rms(o[:, lanes], dn_g) * _silu(dng_ref[:, lanes].astype(F32)))
    ydn = jnp.concatenate(parts, axis=1).astype(BF16)
    m = (_sigmoid(g0_ref[...].astype(F32)) * _dot(ysg_ref[...], wsg_ref[...])
         + _sigmoid(g1_ref[...].astype(F32)) * _dot(ydn, wdn_ref[...])
         + _sigmoid(g2_ref[...].astype(F32)) * _dot(yat_ref[...], wat_ref[...]))
    y = _dot(m.astype(BF16), wout_ref[...])
    gate1 = mod_ref[:, 2 * D_MODEL:3 * D_MODEL]
    sh2 = mod_ref[:, 3 * D_MODEL:4 * D_MODEL]
    sc2 = mod_ref[:, 4 * D_MODEL:5 * D_MODEL]
    xn = x_ref[...] + gate1 * _rms(y, post_ref[...])
    xo_ref[...] = xn
    h2 = _rms(xn, pre_ref[...]) * (1.0 + sc2) + sh2
    h2_ref[...] = h2
    lg_ref[...] = _dot(h2.astype(BF16), rw_ref[...]) + rb_ref[...]


def _merge(ysg, o_fwd, o_bwd, main, yat, x, mod, mod_row, lw, tm=512):
    n_tok = x.shape[0]
    const = lambda i: (0, 0)
    wspec = pl.BlockSpec((D_MODEL, D_MODEL), const, pipeline_mode=pl.Buffered(1))
    vspec = pl.BlockSpec((1, D_MODEL), const)
    return pl.pallas_call(
        _merge_kernel,
        grid=(n_tok // tm,),
        in_specs=[pl.BlockSpec((tm, D_MODEL), lambda i: (i, 0)),
                  pl.BlockSpec((tm, D_MODEL), lambda i: (i, 0)),
                  pl.BlockSpec((tm, D_MODEL), lambda i: (i, 0)),
                  pl.BlockSpec((tm, D_MODEL), lambda i: (i, COL_DN_G)),
                  pl.BlockSpec((tm, D_MODEL), lambda i: (i, 0)),
                  pl.BlockSpec((tm, D_MODEL), lambda i: (i, COL_GATE0)),
                  pl.BlockSpec((tm, D_MODEL), lambda i: (i, COL_GATE0 + 1)),
                  pl.BlockSpec((tm, D_MODEL), lambda i: (i, COL_GATE0 + 2)),
                  pl.BlockSpec((tm, D_MODEL), lambda i: (i, 0)),
                  pl.BlockSpec((None, 1, 6 * D_MODEL), lambda i: (mod_row(i * tm), 0, 0)),
                  pl.BlockSpec((1, DN_DIM), const), vspec, vspec,
                  wspec, wspec, wspec, wspec,
                  pl.BlockSpec((D_MODEL, LANES), const), pl.BlockSpec((1, LANES), const)],
        out_specs=[pl.BlockSpec((tm, D_MODEL), lambda i: (i, 0)),
                   pl.BlockSpec((tm, D_MODEL), lambda i: (i, 0)),
                   pl.BlockSpec((tm, LANES), lambda i: (i, 0))],
        out_shape=[jax.ShapeDtypeStruct((n_tok, D_MODEL), F32),
                   jax.ShapeDtypeStruct((n_tok, D_MODEL), F32),
                   jax.ShapeDtypeStruct((n_tok, LANES), F32)],
        compiler_params=_cparams(("arbitrary",)),
        name="merge",
    )(ysg, o_fwd, o_bwd, main, yat, main, main, main, x, mod,
      lw["dn_norm_g"], lw["norm_post_mix"], lw["norm_pre_ffn"],
      lw["w_proj_sg"], lw["w_proj_dn"], lw["w_proj_at"], lw["w_out"], lw["router_w"], lw["router_b"])


MOE_TOK = 256
MOE_PIECE = 8
MOE_BUF = MOE_TOK * TOP_K + N_EXPERTS * MOE_PIECE
MOE_META = 4 * N_EXPERTS


def _route_kernel(lg_ref, gate_ref, lpos_ref, tcnt_ref):
    tm = lg_ref.shape[0]
    l = lg_ref[...]
    lane = lax.broadcasted_iota(I32, l.shape, 1).astype(F32)
    vals, onehots = [], []
    for k in range(TOP_K):
        m = jnp.max(l, axis=-1, keepdims=True)
        ik = jnp.min(jnp.where(l == m, lane, float(LANES)), axis=-1, keepdims=True)
        oh = lane == ik
        vals.append(m)
        onehots.append(oh)
        l = jnp.where(oh, -jnp.inf, l)
    es = [jnp.exp(v - vals[0]) for v in vals]
    den = es[0] + es[1] + es[2] + es[3]
    sel = jnp.zeros(l.shape, F32)
    for k in range(TOP_K):
        gate_ref[:, k:k + 1] = es[k] / den
        sel = sel + onehots[k].astype(F32)
    ri = lax.broadcasted_iota(I32, (tm, tm), 0)
    ci = lax.broadcasted_iota(I32, (tm, tm), 1)
    before = _dot((ri > ci).astype(BF16), sel.astype(BF16))
    tcnt = jnp.sum(sel, axis=0, keepdims=True)
    tcnt_ref[...] = tcnt
    n_piece = jnp.floor((tcnt + (MOE_PIECE - 1)) * (1.0 / MOE_PIECE))
    ei = lax.broadcasted_iota(I32, (LANES, LANES), 0)
    ej = lax.broadcasted_iota(I32, (LANES, LANES), 1)
    run_start = _dot(jnp.broadcast_to(n_piece, (8, LANES)).astype(BF16),
                     (ei < ej).astype(BF16))[0:1] * float(MOE_PIECE)
    pos = before + run_start
    for k in range(TOP_K):
        lpos_ref[:, k:k + 1] = jnp.sum(jnp.where(onehots[k], pos, 0.0), axis=-1,
                                       keepdims=True).astype(I32)


def _route(logits):
    n_tok = logits.shape[0]
    tm = MOE_TOK
    n_t = n_tok // tm
    small = lambda dt: jax.ShapeDtypeStruct((n_tok, TOP_K), dt)
    kspec = pl.BlockSpec((tm, TOP_K), lambda i: (i, 0))
    tspec = pl.BlockSpec((None, 1, LANES), lambda i: (i, 0, 0))
    tshape = jax.ShapeDtypeStruct((n_t, 1, LANES), F32)
    return pl.pallas_call(
        _route_kernel,
        grid=(n_t,),
        in_specs=[pl.BlockSpec((tm, LANES), lambda i: (i, 0))],
        out_specs=[kspec, kspec, tspec],
        out_shape=[small(F32), small(I32), tshape],
        compiler_params=_cparams(("arbitrary",)),
        name="route",
    )(logits)


def _run_copies(meta_ref, base, src_of, dst_of, sem, start):
    def per_expert(e, carry):
        local = pl.multiple_of(meta_ref[base + e], MOE_PIECE)
        slot = pl.multiple_of(meta_ref[base + N_EXPERTS + e], MOE_PIECE)

        def per_piece(p, c2):
            cp = pltpu.make_async_copy(src_of(local + p * MOE_PIECE, slot + p * MOE_PIECE),
                                       dst_of(local + p * MOE_PIECE, slot + p * MOE_PIECE), sem)
            if start:
                cp.start()
            else:
                cp.wait()
            return c2

        return lax.fori_loop(0, meta_ref[base + 2 * N_EXPERTS + e], per_piece, carry)

    lax.fori_loop(0, N_EXPERTS, per_expert, 0)


def _dispatch_kernel(meta_ref, prev_ref, zmeta_ref, lpos_ref, h_ref, xs_ref, buf_ref, buf1_ref, zero_ref,
                     sem, sem1, *, tm_e):
    tm = MOE_TOK
    step = pl.program_id(0)
    bufs, sems = (buf_ref, buf1_ref), (sem, sem1)
    rows = lambda ref, r: ref.at[pl.ds(r, MOE_PIECE)]

    @pl.when(step == 0)
    def _():
        zero_ref[...] = jnp.zeros_like(zero_ref)

        def zero_tail(start):
            def per_expert(e, carry):
                z0 = pl.multiple_of(zmeta_ref[e], MOE_PIECE)

                def per_piece(p, c2):
                    cp = pltpu.make_async_copy(zero_ref, rows(xs_ref, z0 + p * MOE_PIECE), sem)
                    if start:
                        cp.start()
                    else:
                        cp.wait()
                    return c2

                return lax.fori_loop(0, zmeta_ref[N_EXPERTS + e], per_piece, carry)

            lax.fori_loop(0, N_EXPERTS, per_expert, 0)

        zero_tail(True)
        zero_tail(False)

        buf_ref[0:tm_e, :] = jnp.zeros((tm_e, D_MODEL), F32)

        def zero_tiles(start):
            def per_tile(p, carry):
                t0 = pl.multiple_of((zmeta_ref[2 * N_EXPERTS] + p) * tm_e, tm_e)
                cp = pltpu.make_async_copy(buf_ref.at[pl.ds(0, tm_e)], xs_ref.at[pl.ds(t0, tm_e)], sem)
                if start:
                    cp.start()
                else:
                    cp.wait()
                return carry

            lax.fori_loop(0, zmeta_ref[2 * N_EXPERTS + 1], per_tile, 0)

        zero_tiles(True)
        zero_tiles(False)

    def group(j):
        s_iota = lax.broadcasted_iota(I32, (MOE_BUF, tm), 0)
        perm = jnp.zeros((MOE_BUF, tm), F32)
        for k in range(TOP_K):
            perm = perm + (s_iota == lpos_ref[k:k + 1, j * tm:(j + 1) * tm]).astype(F32)
        bufs[j][...] = _dot(perm.astype(BF16), h_ref[j * tm:(j + 1) * tm, :].astype(BF16))

    def copies(mref, j, start):
        _run_copies(mref, j * MOE_META, lambda loc, slot: rows(bufs[j], loc),
                    lambda loc, slot: rows(xs_ref, slot), sems[j], start)

    group(0)
    copies(meta_ref, 0, True)

    @pl.when(step > 0)
    def _():
        copies(prev_ref, 1, False)

    group(1)
    copies(meta_ref, 1, True)
    copies(meta_ref, 0, False)

    @pl.when(step == pl.num_programs(0) - 1)
    def _():
        copies(meta_ref, 1, False)


def _dispatch(meta, zmeta, lpos_t, h2, n_slots, tm_e):
    n_tok = h2.shape[0]
    tm = 2 * MOE_TOK
    assert tm_e <= MOE_BUF and n_tok % tm == 0
    return pl.pallas_call(
        functools.partial(_dispatch_kernel, tm_e=tm_e),
        grid=(n_tok // tm,),
        in_specs=[pl.BlockSpec((2 * MOE_META,), lambda i: (i,), memory_space=pltpu.SMEM),
                  pl.BlockSpec((2 * MOE_META,), lambda i: (jnp.maximum(i - 1, 0),), memory_space=pltpu.SMEM),
                  pl.BlockSpec((MOE_META,), lambda i: (0,), memory_space=pltpu.SMEM),
                  pl.BlockSpec((TOP_K, tm), lambda i: (0, i)),
                  pl.BlockSpec((tm, D_MODEL), lambda i: (i, 0))],
        out_specs=pl.BlockSpec(memory_space=pl.ANY),
        out_shape=jax.ShapeDtypeStruct((n_slots, D_MODEL), F32),
        scratch_shapes=[pltpu.VMEM((MOE_BUF, D_MODEL), F32), pltpu.VMEM((MOE_BUF, D_MODEL), F32),
                        pltpu.VMEM((MOE_PIECE, D_MODEL), F32),
                        pltpu.SemaphoreType.DMA, pltpu.SemaphoreType.DMA],
        compiler_params=_cparams(("arbitrary",)),
        name="moe_dispatch",
    )(meta, meta, zmeta, lpos_t, h2)


def _expert_kernel(te_ref, nu_ref, xs_ref, wgu_ref, bgu_ref, wd_ref, bd_ref, y_ref):
    del te_ref

    @pl.when(pl.program_id(0) < nu_ref[0])
    def _():
        gu = _dot(xs_ref[...].astype(BF16), wgu_ref[...]) + bgu_ref[...]
        g = jnp.minimum(gu[:, :D_EXPERT], SWIGLU_LIMIT)
        lin = jnp.clip(gu[:, D_EXPERT:], -SWIGLU_LIMIT, SWIGLU_LIMIT)
        act = g * _sigmoid(SWIGLU_ALPHA * g) * (lin + 1.0)
        y_ref[...] = _dot(act.astype(BF16), wd_ref[...]) + bd_ref[...]

    @pl.when(pl.program_id(0) >= nu_ref[0])
    def _():
        y_ref[...] = jnp.zeros_like(y_ref)


def _experts(tile_expert, n_used, xs, wgu, bgu, wd, bd, tm):
    n_slots = xs.shape[0]
    n_tiles = n_slots // tm

    def row(i, te, nu):
        return (jnp.minimum(i, nu[0] - 1), 0)

    grid_spec = pltpu.PrefetchScalarGridSpec(
        num_scalar_prefetch=2,
        grid=(n_tiles,),
        in_specs=[pl.BlockSpec((tm, D_MODEL), row),
                  pl.BlockSpec((None, D_MODEL, 2 * D_EXPERT), lambda i, te, nu: (te[i], 0, 0)),
                  pl.BlockSpec((None, 1, 2 * D_EXPERT), lambda i, te, nu: (te[i], 0, 0)),
                  pl.BlockSpec((None, D_EXPERT, D_MODEL), lambda i, te, nu: (te[i], 0, 0)),
                  pl.BlockSpec((None, 1, D_MODEL), lambda i, te, nu: (te[i], 0, 0))],
        out_specs=pl.BlockSpec((tm, D_MODEL), lambda i, te, nu: (i, 0)),
    )
    return pl.pallas_call(
        _expert_kernel,
        grid_spec=grid_spec,
        out_shape=jax.ShapeDtypeStruct((n_slots, D_MODEL), F32),
        compiler_params=_cparams(("arbitrary",)),
        name="moe_experts",
    )(tile_expert, n_used, xs, wgu, bgu, wd, bd)


def _combine_kernel(meta_ref, next_ref, lpos_ref, gate_ref, x_ref, mod_ref, post_ref, y_ref, xo_ref,
                    buf_ref, buf1_ref, sem, sem1):
    tm = MOE_TOK
    step = pl.program_id(0)
    bufs, sems = (buf_ref, buf1_ref), (sem, sem1)
    rows = lambda ref, r: ref.at[pl.ds(r, MOE_PIECE)]

    def copies(mref, j, base, start):
        _run_copies(mref, base, lambda loc, slot: rows(y_ref, slot),
                    lambda loc, slot: rows(bufs[j], loc), sems[j], start)

    def reduce_tile(j):
        tok = slice(j * tm, (j + 1) * tm)
        s_iota = lax.broadcasted_iota(I32, (tm, MOE_BUF), 1)
        sel = jnp.zeros((tm, MOE_BUF), F32)
        for k in range(TOP_K):
            sel = sel + jnp.where(s_iota == lpos_ref[tok, k:k + 1], gate_ref[tok, k:k + 1], 0.0)
        y = _dot(sel.astype(BF16), bufs[j][...].astype(BF16))
        gate2 = mod_ref[:, 5 * D_MODEL:6 * D_MODEL]
        xo_ref[tok, :] = x_ref[tok, :] + gate2 * _rms(y, post_ref[...])

    @pl.when(step == 0)
    def _():
        buf_ref[...] = jnp.zeros_like(buf_ref)
        buf1_ref[...] = jnp.zeros_like(buf1_ref)
        copies(meta_ref, 0, 0, True)

    copies(meta_ref, 1, MOE_META, True)
    copies(meta_ref, 0, 0, False)
    reduce_tile(0)

    @pl.when(step < pl.num_programs(0) - 1)
    def _():
        copies(next_ref, 0, 0, True)

    copies(meta_ref, 1, MOE_META, False)
    reduce_tile(1)


def _combine(meta, lpos, gate, x_mid, mod, mod_row, post_g, y):
    n_tok = x_mid.shape[0]
    tm = 2 * MOE_TOK
    n_steps = n_tok // tm
    return pl.pallas_call(
        _combine_kernel,
        grid=(n_steps,),
        in_specs=[pl.BlockSpec((2 * MOE_META,), lambda i: (i,), memory_space=pltpu.SMEM),
                  pl.BlockSpec((2 * MOE_META,), lambda i: (jnp.minimum(i + 1, n_steps - 1),),
                               memory_space=pltpu.SMEM),
                  pl.BlockSpec((tm, TOP_K), lambda i: (i, 0)),
                  pl.BlockSpec((tm, TOP_K), lambda i: (i, 0)),
                  pl.BlockSpec((tm, D_MODEL), lambda i: (i, 0)),
                  pl.BlockSpec((None, 1, 6 * D_MODEL), lambda i: (mod_row(i * tm), 0, 0)),
                  pl.BlockSpec((1, D_MODEL), lambda i: (0, 0)),
                  pl.BlockSpec(memory_space=pl.ANY)],
        out_specs=pl.BlockSpec((tm, D_MODEL), lambda i: (i, 0)),
        out_shape=jax.ShapeDtypeStruct((n_tok, D_MODEL), F32),
        scratch_shapes=[pltpu.VMEM((MOE_BUF, D_MODEL), F32), pltpu.VMEM((MOE_BUF, D_MODEL), F32),
                        pltpu.SemaphoreType.DMA, pltpu.SemaphoreType.DMA],
        compiler_params=_cparams(("arbitrary",)),
        name="moe_combine",
    )(meta, meta, lpos, gate, x_mid, mod, post_g, y)


def _moe(h2, logits, x_mid, mod, mod_row, lw, tm_e=512):
    n_tok = h2.shape[0]
    n_t = n_tok // MOE_TOK
    gate, lpos, tcnt = _route(logits)
    tcnt = tcnt[:, 0, :N_EXPERTS].astype(I32)
    pieces = (tcnt + MOE_PIECE - 1) // MOE_PIECE
    run_end = jnp.cumsum(pieces, axis=0) * MOE_PIECE
    used = run_end[-1]
    padded = (used + tm_e - 1) // tm_e * tm_e
    pad_end = jnp.cumsum(padded)
    offs = pad_end - padded
    n_tiles = (n_tok * TOP_K + n_t * N_EXPERTS * (MOE_PIECE - 1) + tm_e - 1) // tm_e + N_EXPERTS
    tile_start = jnp.arange(n_tiles, dtype=I32) * tm_e
    tile_expert = jnp.minimum(jnp.sum(pad_end[None, :] <= tile_start[:, None], axis=1),
                              N_EXPERTS - 1).astype(I32)
    n_used = (pad_end[-1:] // tm_e).astype(I32)
    lstart = (jnp.cumsum(pieces, axis=1) - pieces) * MOE_PIECE
    slot_start = offs[None, :] + run_end - pieces * MOE_PIECE
    meta = jnp.concatenate([lstart, slot_start, pieces, jnp.zeros_like(pieces)],
                           axis=1).reshape(-1).astype(I32)
    z0 = offs + used
    tail = jnp.stack([n_used[0], n_tiles - n_used[0]])
    zmeta = jnp.concatenate([z0, (pad_end - z0) // MOE_PIECE, tail,
                             jnp.zeros((MOE_META - 2 * N_EXPERTS - 2,), I32)]).astype(I32)
    xs = _dispatch(meta, zmeta, lpos.T, h2, n_tiles * tm_e, tm_e)
    y = _experts(tile_expert, n_used, xs, lw["exp_w_gu"], lw["exp_b_gu"], lw["exp_w_down"],
                 lw["exp_b_down"], tm_e)
    return _combine(meta, lpos, gate, x_mid, mod, mod_row, lw["norm_post_ffn"], y)


def _split_w_in(w_in):
    offs, o = {}, 0
    for name, width in (("dn_k", 1024), ("dn_v", 1024), ("dn_a", 16), ("dn_b", 16), ("at_k", 128),
                        ("at_v", 128), ("dn_q", 1024), ("dn_g", 1024), ("at_q", 1024),
                        ("sg_u", 1024), ("sg_v", 1024), ("gates", 3072)):
        offs[name] = (o, o + width)
        o += width
    sl = lambda n: w_in[:, offs[n][0]:offs[n][1]]
    w_main = jnp.concatenate([sl(n) for n in ("dn_k", "dn_v", "dn_q", "dn_g", "at_q", "sg_u", "sg_v",
                                              "gates")], axis=1).astype(BF16)
    pad = jnp.zeros((w_in.shape[0], N_SMALL_COLS - 2 * LANES - 4 * DN_HEADS), w_in.dtype)
    w_small = jnp.concatenate([sl("at_k"), sl("at_v"), sl("dn_a"), sl("dn_b"), pad], axis=1).astype(BF16)
    return w_main, w_small


def _dn_gates(small, B, T):
    ab = small[:, 2 * LANES:2 * LANES + 4 * DN_HEADS].reshape(B, T, 2, 2, DN_HEADS)
    col = jnp.transpose(ab, (3, 0, 1, 2, 4)).reshape(2, B, T, 2 * DN_HEADS)
    row = jnp.transpose(col.reshape(2, B, T // DN_CHUNK, DN_CHUNK, 2 * DN_HEADS), (0, 1, 2, 4, 3))
    return col, row


def kernel(x, c, ctx, c_ctx, w_mod, b_mod, norm_pre_mix, norm_post_mix, norm_pre_ffn, norm_post_ffn, w_in, sg_ln_g, sg_ln_b, sg_w, sg_b, dn_conv_w, dn_a_log, dn_dt_bias, dn_norm_g, at_sinks, w_proj_sg, w_proj_dn, w_proj_at, w_out, router_w, router_b, exp_w_gu, exp_b_gu, exp_w_down, exp_b_down):
    B, S, D = x.shape
    L = ctx.shape[1]
    depth = w_mod.shape[0]
    assert D == D_MODEL and S % GRID_W == 0
    n_lat, n_ctx = B * S, B * L

    rows = (B + 1 + 7) // 8 * 8
    cvec = jnp.zeros((rows, D), F32).at[:B].set(c).at[B].set(c_ctx)
    mod_all = _modulation(cvec, w_mod, b_mod)
    tables = _rope_tables(S)

    lat_row = lambda t: t // S
    ctx_row = lambda t: B
    all_row = lambda t: jnp.where(t < n_lat, t // S, B)

    xl = x.reshape(n_lat, D)
    xc = ctx.reshape(n_ctx, D)
    for l in range(depth):
        need_ctx_out = l < depth - 1
        mod = mod_all[l].reshape(rows, 1, 6 * D)
        w_main, w_small = _split_w_in(w_in[l])
        lw = {
            "dn_norm_g": dn_norm_g[l].reshape(1, -1),
            "norm_post_mix": norm_post_mix[l].reshape(1, -1),
            "norm_pre_ffn": norm_pre_ffn[l].reshape(1, -1),
            "norm_post_ffn": norm_post_ffn[l].reshape(1, -1),
            "w_proj_sg": w_proj_sg[l].astype(BF16), "w_proj_dn": w_proj_dn[l].astype(BF16),
            "w_proj_at": w_proj_at[l].astype(BF16), "w_out": w_out[l].astype(BF16),
            "router_w": jnp.pad(router_w[l], ((0, 0), (0, LANES - N_EXPERTS))).astype(BF16),
            "router_b": jnp.pad(router_b[l], (0, LANES - N_EXPERTS),
                                constant_values=NEG_BIG).reshape(1, -1),
            "exp_w_gu": exp_w_gu[l].astype(BF16), "exp_b_gu": exp_b_gu[l].reshape(N_EXPERTS, 1, -1),
            "exp_w_down": exp_w_down[l].astype(BF16), "exp_b_down": exp_b_down[l].reshape(N_EXPERTS, 1, -1),
        }
        pre_g = norm_pre_mix[l].reshape(1, -1)
        main, small = _inproj(xl, mod, lat_row, pre_g, w_main, w_small, min(1024, S))
        w_main_c = w_main if need_ctx_out else w_main[:, :N_CTX_MAIN_COLS]
        main_c, small_c = _inproj(xc, mod, ctx_row, pre_g, w_main_c, w_small, min(1024, n_ctx))

        sg_args = (sg_ln_g[l].reshape(1, -1), sg_ln_b[l].reshape(1, -1), sg_w[l].astype(BF16),
                   sg_b[l].T)
        ysg = _sgu(main, *sg_args)

        gcol_c, grow_c = _dn_gates(small_c, B, L)
        gcol, grow = _dn_gates(small, B, S)
        s0 = jnp.zeros((2, B, DN_HEADS, DN_DIM, DN_DIM), F32)
        of_c, ob_c, s_ctx = _deltanet(main_c.reshape(B, L, -1), gcol_c, grow_c, dn_conv_w[l], dn_a_log[l],
                                      dn_dt_bias[l], s0, need_ctx_out)
        of_l, ob_l, _ = _deltanet(main.reshape(B, S, -1), gcol, grow, dn_conv_w[l], dn_a_log[l],
                                  dn_dt_bias[l], s_ctx, True)

        sinks = jnp.repeat(at_sinks[l], AT_BLOCK).reshape(-1, 1)
        q_r, k_r, v_r = _rope(main, small, tables, S)
        yat = _attention_local(q_r, k_r, v_r, small_c, sinks, B, S, L)

        x_mid, h2, logits = _merge(ysg, of_l.reshape(n_lat, D), ob_l.reshape(n_lat, D), main, yat, xl, mod,
                                   lat_row, lw)
        if need_ctx_out:
            ysg_c = _sgu(main_c, *sg_args)
            yat_c = _attention_ctx(main_c, small_c, sinks, B, L)
            xc_mid, h2c, logits_c = _merge(ysg_c, of_c.reshape(n_ctx, D), ob_c.reshape(n_ctx, D), main_c,
                                           yat_c, xc, mod, ctx_row, lw)
            x_mid = jnp.concatenate([x_mid, xc_mid], axis=0)
            h2 = jnp.concatenate([h2, h2c], axis=0)
            logits = jnp.concatenate([logits, logits_c], axis=0)
            xo = _moe(h2, logits, x_mid, mod, all_row, lw)
            xl, xc = xo[:n_lat], xo[n_lat:]
        else:
            xl = _moe(h2, logits, x_mid, mod, lat_row, lw)
    return xl.reshape(B, S, D)
```

```python
import functools
import math

import jax
import jax.numpy as jnp
from jax import lax
from jax.experimental import pallas as pl
from jax.experimental.pallas import tpu as pltpu

F32 = jnp.float32
BF16 = jnp.bfloat16
I32 = jnp.int32
U32 = jnp.uint32

EPS = 1e-6
D_MODEL = 1024
GRID_W = 64

SG_CHUNK = 128
SG_GROUPS = 8

DN_HEADS = 8
DN_DIM = 128
DN_CONV = 5
DN_CHUNK = 64
DN_HALO = 16
DN_PREP_CHUNKS = 2
DN_SCAN_CHUNKS = 4

AT_Q_HEADS = 16
AT_KV_HEADS = 2
AT_DIM = 64
AT_BLOCK = 128
ROPE_BASE = 10000.0

N_EXPERTS = 32
TOP_K = 4
D_EXPERT = 1024
SWIGLU_ALPHA = 1.702
SWIGLU_LIMIT = 7.0
N_BRANCH = 3

LANES = 128
NEG_BIG = -1e30

COL_DN_K, COL_DN_V, COL_DN_Q, COL_DN_G, COL_AT_Q, COL_SG_U, COL_SG_V, COL_GATE0 = range(8)
N_MAIN_COLS = 10 * D_MODEL
N_CTX_MAIN_COLS = 2 * D_MODEL
N_SMALL_COLS = 3 * LANES

VMEM_LIMIT = 52 * 1024 * 1024


def _cparams(sem):
    return pltpu.CompilerParams(dimension_semantics=sem, vmem_limit_bytes=VMEM_LIMIT)


def _dot(a, b):
    return jnp.dot(a, b, preferred_element_type=F32)


def _dot_nt(a, b):
    return lax.dot_general(a, b, (((1,), (1,)), ((), ())), preferred_element_type=F32)


def _dot_tn(a, b):
    return lax.dot_general(a, b, (((0,), (0,)), ((), ())), preferred_element_type=F32)


def _sigmoid(x):
    return 1.0 / (1.0 + jnp.exp(-x))


def _silu(x):
    return x * _sigmoid(x)


def _gelu_tanh(x):
    return 0.5 * x * (1.0 + jnp.tanh(math.sqrt(2.0 / math.pi) * (x + 0.044715 * (x * x * x))))


def _softplus(x):
    return jnp.maximum(x, 0.0) + jnp.log(1.0 + jnp.exp(-jnp.abs(x)))


def _rms(x, g):
    return x * lax.rsqrt(jnp.mean(x * x, axis=-1, keepdims=True) + EPS) * g


def _mod_kernel(c_ref, w_ref, b_ref, o_ref):
    s = _silu(c_ref[...])
    o_ref[...] = jnp.dot(s, w_ref[...], preferred_element_type=F32,
                         precision=lax.Precision.HIGHEST) + b_ref[...]


def _modulation(cvec, w_mod, b_mod):
    depth = w_mod.shape[0]
    rows = cvec.shape[0]
    n_col = w_mod.shape[2] // D_MODEL
    return pl.pallas_call(
        _mod_kernel,
        grid=(depth, n_col),
        in_specs=[pl.BlockSpec((rows, D_MODEL), lambda l, j: (0, 0)),
                  pl.BlockSpec((None, D_MODEL, D_MODEL), lambda l, j: (l, 0, j)),
                  pl.BlockSpec((None, 1, D_MODEL), lambda l, j: (l, 0, j))],
        out_specs=pl.BlockSpec((None, rows, D_MODEL), lambda l, j: (l, 0, j)),
        out_shape=jax.ShapeDtypeStruct((depth, rows, w_mod.shape[2]), F32),
        compiler_params=_cparams(("arbitrary", "arbitrary")),
        name="modulation",
    )(cvec, w_mod, b_mod.reshape(depth, 1, -1))


def _inproj_kernel(x_ref, mod_ref, g_ref, wm_ref, ws_ref, main_ref, small_ref, h_ref):
    @pl.when(pl.program_id(1) == 0)
    def _():
        sh = mod_ref[:, 0 * D_MODEL:1 * D_MODEL]
        sc = mod_ref[:, 1 * D_MODEL:2 * D_MODEL]
        h = (_rms(x_ref[...], g_ref[...]) * (1.0 + sc) + sh).astype(BF16)
        h_ref[...] = h
        small_ref[...] = _dot(h, ws_ref[...])

    main_ref[...] = _dot(h_ref[...], wm_ref[...]).astype(BF16)


def _inproj(x, mod, mod_row, norm_g, w_main, w_small, tm, tn=2048):
    n_tok = x.shape[0]
    n_main = w_main.shape[1]
    return pl.pallas_call(
        _inproj_kernel,
        grid=(n_tok // tm, n_main // tn),
        in_specs=[pl.BlockSpec((tm, D_MODEL), lambda i, j: (i, 0)),
                  pl.BlockSpec((None, 1, 6 * D_MODEL), lambda i, j: (mod_row(i * tm), 0, 0)),
                  pl.BlockSpec((1, D_MODEL), lambda i, j: (0, 0)),
                  pl.BlockSpec((D_MODEL, tn), lambda i, j: (0, j)),
                  pl.BlockSpec((D_MODEL, N_SMALL_COLS), lambda i, j: (0, 0))],
        out_specs=[pl.BlockSpec((tm, tn), lambda i, j: (i, j)),
                   pl.BlockSpec((tm, N_SMALL_COLS), lambda i, j: (i, 0))],
        out_shape=[jax.ShapeDtypeStruct((n_tok, n_main), BF16),
                   jax.ShapeDtypeStruct((n_tok, N_SMALL_COLS), F32)],
        scratch_shapes=[pltpu.VMEM((tm, D_MODEL), BF16)],
        compiler_params=_cparams(("arbitrary", "arbitrary")),
        name="inproj",
    )(x, mod, norm_g, w_main, w_small)


def _sgu_kernel(u_ref, v_ref, lng_ref, lnb_ref, ws_ref, bs_ref, o_ref, *, n_chunk):
    u = _gelu_tanh(u_ref[...].astype(F32))
    v = _gelu_tanh(v_ref[...].astype(F32))
    vc = v - jnp.mean(v, axis=-1, keepdims=True)
    var = jnp.mean(vc * vc, axis=-1, keepdims=True)
    vn = (vc * lax.rsqrt(var + EPS) * lng_ref[...] + lnb_ref[...]).astype(BF16)
    for n in range(n_chunk):
        rows = slice(n * SG_CHUNK, (n + 1) * SG_CHUNK)
        for g in range(SG_GROUPS):
            cols = slice(g * LANES, (g + 1) * LANES)
            mixed = _dot(ws_ref[g], vn[rows, cols]) + bs_ref[:, g:g + 1]
            o_ref[rows, cols] = (u[rows, cols] * mixed).astype(BF16)


def _sgu(main, sg_ln_g, sg_ln_b, sg_w, sg_bt, n_chunk=2):
    n_tok = main.shape[0]
    tc = n_chunk * SG_CHUNK
    return pl.pallas_call(
        functools.partial(_sgu_kernel, n_chunk=n_chunk),
        grid=(n_tok // tc,),
        in_specs=[pl.BlockSpec((tc, D_MODEL), lambda i: (i, COL_SG_U)),
                  pl.BlockSpec((tc, D_MODEL), lambda i: (i, COL_SG_V)),
                  pl.BlockSpec((1, D_MODEL), lambda i: (0, 0)),
                  pl.BlockSpec((1, D_MODEL), lambda i: (0, 0)),
                  pl.BlockSpec((SG_GROUPS, SG_CHUNK, SG_CHUNK), lambda i: (0, 0, 0)),
                  pl.BlockSpec((SG_CHUNK, SG_GROUPS), lambda i: (0, 0))],
        out_specs=pl.BlockSpec((tc, D_MODEL), lambda i: (i, 0)),
        out_shape=jax.ShapeDtypeStruct((n_tok, D_MODEL), BF16),
        compiler_params=_cparams(("arbitrary",)),
        name="sgu",
    )(main, main, sg_ln_g, sg_ln_b, sg_w, sg_bt)


def _dn_kernel(*refs, with_q, n_chunks):
    s_refs = refs[-DN_HEADS:]
    ext_refs = refs[-DN_HEADS - 3:-DN_HEADS]
    refs = refs[:-DN_HEADS - 3]
    if with_q:
        (qp_ref, qc_ref, qn_ref, kp_ref, kc_ref, kn_ref, vp_ref, vc_ref, vn_ref,
         gcol_ref, grow_ref, cw_ref, alog_r_ref, alog_c_ref, dtb_r_ref, dtb_c_ref, s0_ref,
         o_ref, sfin_ref) = refs
    else:
        (kp_ref, kc_ref, kn_ref, vp_ref, vc_ref, vn_ref,
         gcol_ref, grow_ref, cw_ref, alog_r_ref, alog_c_ref, dtb_r_ref, dtb_c_ref, s0_ref,
         sfin_ref) = refs
    d = pl.program_id(0)
    c = pl.program_id(2)
    is_fwd = d == 0
    cidx = jnp.where(is_fwd, c, n_chunks - 1 - c)
    C = DN_CHUNK

    @pl.when(c == 0)
    def _():
        for h in range(DN_HEADS):
            s_refs[h][...] = s0_ref[h]

    has_prev = (cidx > 0).astype(F32)
    has_next = (cidx < n_chunks - 1).astype(F32)

    def conv_silu(p_ref, c_ref, n_ref, part):
        ext_ref = ext_refs[part]
        ext_ref[0:DN_HALO, :] = p_ref[...].astype(F32) * has_prev
        ext_ref[DN_HALO:DN_HALO + C, :] = c_ref[...].astype(F32)
        ext_ref[DN_HALO + C:2 * DN_HALO + C, :] = n_ref[...].astype(F32) * has_next
        base = DN_HALO - DN_CONV // 2
        y = None
        for i in range(DN_CONV):
            w = cw_ref[i:i + 1, part * D_MODEL:(part + 1) * D_MODEL]
            t = ext_ref[base + i:base + i + C, :] * w
            y = t if y is None else y + t
        return _silu(y)

    k_all = conv_silu(kp_ref, kc_ref, kn_ref, 1)
    v_all = conv_silu(vp_ref, vc_ref, vn_ref, 2)
    q_all = conv_silu(qp_ref, qc_ref, qn_ref, 0) if with_q else None

    gcol = gcol_ref[...]
    ld_col = -jnp.exp(alog_r_ref[...]) * _softplus(gcol[:, 0:DN_HEADS] + dtb_r_ref[...])
    beta_col = _sigmoid(gcol[:, DN_HEADS:2 * DN_HEADS])
    ld_row = -jnp.exp(alog_c_ref[...]) * _softplus(grow_ref[0:DN_HEADS, :] + dtb_c_ref[...])

    ri = lax.broadcasted_iota(I32, (C, C), 0)
    ci = lax.broadcasted_iota(I32, (C, C), 1)
    delta = (ri - ci) * (1 - 2 * d)
    incl = delta >= 0
    strict = delta > 0
    incl_t = delta <= 0
    gam_col = jnp.dot(incl.astype(F32), ld_col, preferred_element_type=F32,
                      precision=lax.Precision.HIGHEST)
    gam_row = jnp.dot(ld_row, incl_t.astype(F32), preferred_element_type=F32,
                      precision=lax.Precision.HIGHEST)
    gam_tot = jnp.sum(ld_col, axis=0, keepdims=True)
    eye = (ri == ci).astype(F32)

    H = range(DN_HEADS)
    lanes = [slice(h * DN_DIM, (h + 1) * DN_DIM) for h in H]
    gc = [gam_col[:, h:h + 1] for h in H]
    bc = [beta_col[:, h:h + 1] for h in H]
    kh = [k_all[:, lanes[h]] for h in H]
    kh = [kh[h] * lax.rsqrt(jnp.sum(kh[h] * kh[h], axis=-1, keepdims=True) + EPS) for h in H]
    kb = [kh[h].astype(BF16) for h in H]
    decay = [jnp.exp(jnp.where(incl, gc[h] - gam_row[h:h + 1, :], NEG_BIG)) for h in H]
    kk = [_dot_nt(kb[h], kb[h]) for h in H]
    x = [-(jnp.where(strict, decay[h], 0.0) * bc[h] * kk[h]) for h in H]
    p = [eye + x[h] for h in H]
    xb = [x[h].astype(BF16) for h in H]
    x = [_dot(xb[h], xb[h]) for h in H]
    n_fac = int(math.log2(C)) - 1
    for j in range(n_fac):
        xb = [x[h].astype(BF16) for h in H]
        if j < n_fac - 1:
            r = [_dot(xb[h], jnp.concatenate([xb[h], p[h].astype(BF16)], axis=1)) for h in H]
            x = [r[h][:, :C] for h in H]
            p = [p[h] + r[h][:, C:] for h in H]
        else:
            p = [p[h] + _dot(xb[h], p[h].astype(BF16)) for h in H]
    rhs = [jnp.concatenate([kh[h] * (bc[h] * jnp.exp(gc[h])), v_all[:, lanes[h]] * bc[h]],
                           axis=1).astype(BF16) for h in H]
    sol = [_dot(p[h].astype(BF16), rhs[h]) for h in H]
    w = [sol[h][:, :DN_DIM] for h in H]
    u0 = [sol[h][:, DN_DIM:] for h in H]
    k_end = [(kh[h] * jnp.exp(gam_tot[:, h:h + 1] - gc[h])).astype(BF16) for h in H]
    s = [s_refs[h][...] for h in H]
    sb = [s[h].astype(BF16) for h in H]
    if with_q:
        qh = [q_all[:, lanes[h]] for h in H]
        qh = [qh[h] * (lax.rsqrt(jnp.sum(qh[h] * qh[h], axis=-1, keepdims=True) + EPS) * DN_DIM ** -0.5)
              for h in H]
        qk = [(_dot_nt(qh[h].astype(BF16), kb[h]) * decay[h]).astype(BF16) for h in H]
        wq = [jnp.concatenate([w[h], qh[h] * jnp.exp(gc[h])], axis=0).astype(BF16) for h in H]
        ws = [_dot(wq[h], sb[h]) for h in H]
        ub = [(u0[h] - ws[h][:C]).astype(BF16) for h in H]
        qu = [_dot(qk[h], ub[h]) for h in H]
        for h in H:
            o_ref[:, lanes[h]] = (ws[h][C:] + qu[h]).astype(o_ref.dtype)
    else:
        ws = [_dot(w[h].astype(BF16), sb[h]) for h in H]
        ub = [(u0[h] - ws[h]).astype(BF16) for h in H]
    ku = [_dot_tn(k_end[h], ub[h]) for h in H]
    for h in H:
        s_refs[h][...] = jnp.exp(gam_tot[:, h:h + 1]) * s[h] + ku[h]

    @pl.when(c == n_chunks - 1)
    def _():
        for h in range(DN_HEADS):
            sfin_ref[h] = s_refs[h][...]


def _deltanet_single_pass(main3, gate_col, gate_row, conv_w, alog, dtb, s0, with_q):
    B, T, _ = main3.shape
    C = DN_CHUNK
    n_chunks = T // C
    hpc = C // DN_HALO
    n_halo = T // DN_HALO

    def cix(d, c):
        return jnp.where(d == 0, c, n_chunks - 1 - c)

    def trio(col):
        return [pl.BlockSpec((None, DN_HALO, D_MODEL),
                             lambda d, b, c: (b, jnp.maximum(cix(d, c) * hpc - 1, 0), col)),
                pl.BlockSpec((None, C, D_MODEL), lambda d, b, c: (b, cix(d, c), col)),
                pl.BlockSpec((None, DN_HALO, D_MODEL),
                             lambda d, b, c: (b, jnp.minimum((cix(d, c) + 1) * hpc, n_halo - 1), col))]

    in_specs = (trio(COL_DN_Q) if with_q else []) + trio(COL_DN_K) + trio(COL_DN_V) + [
        pl.BlockSpec((None, None, C, 2 * DN_HEADS), lambda d, b, c: (d, b, cix(d, c), 0)),
        pl.BlockSpec((None, None, None, 2 * DN_HEADS, C), lambda d, b, c: (d, b, cix(d, c), 0, 0)),
        pl.BlockSpec((DN_CONV, 3 * D_MODEL), lambda d, b, c: (0, 0)),
        pl.BlockSpec((None, 1, DN_HEADS), lambda d, b, c: (d, 0, 0)),
        pl.BlockSpec((None, DN_HEADS, 1), lambda d, b, c: (d, 0, 0)),
        pl.BlockSpec((None, 1, DN_HEADS), lambda d, b, c: (d, 0, 0)),
        pl.BlockSpec((None, DN_HEADS, 1), lambda d, b, c: (d, 0, 0)),
        pl.BlockSpec((None, None, DN_HEADS, DN_DIM, DN_DIM), lambda d, b, c: (d, b, 0, 0, 0)),
    ]
    s_spec = pl.BlockSpec((None, None, DN_HEADS, DN_DIM, DN_DIM), lambda d, b, c: (d, b, 0, 0, 0))
    s_shape = jax.ShapeDtypeStruct((2, B, DN_HEADS, DN_DIM, DN_DIM), F32)
    if with_q:
        out_specs = [pl.BlockSpec((None, None, C, D_MODEL), lambda d, b, c: (d, b, cix(d, c), 0)), s_spec]
        out_shape = [jax.ShapeDtypeStruct((2, B, T, D_MODEL), BF16), s_shape]
    else:
        out_specs = [s_spec]
        out_shape = [s_shape]
    n_main = 3 if with_q else 2
    args = [main3] * (3 * n_main) + [
        gate_col, gate_row, conv_w,
        alog.reshape(2, 1, DN_HEADS), alog.reshape(2, DN_HEADS, 1),
        dtb.reshape(2, 1, DN_HEADS), dtb.reshape(2, DN_HEADS, 1), s0]
    out = pl.pallas_call(
        functools.partial(_dn_kernel, with_q=with_q, n_chunks=n_chunks),
        grid=(2, B, n_chunks),
        in_specs=in_specs, out_specs=out_specs, out_shape=out_shape,
        scratch_shapes=[pltpu.VMEM((C + 2 * DN_HALO, D_MODEL), F32)] * 3
        + [pltpu.VMEM((DN_DIM, DN_DIM), F32)] * DN_HEADS,
        compiler_params=_cparams(("arbitrary", "arbitrary", "arbitrary")),
        name="deltanet_q" if with_q else "deltanet_state",
    )(*args)
    return (out[0], out[1]) if with_q else (None, out[0])


def _dn_prep_kernel(*refs, with_q, n_chunks):
    if with_q:
        (qp_ref, qc_ref, qn_ref, kp_ref, kc_ref, kn_ref, vp_ref, vc_ref, vn_ref,
         gcol_ref, grow_ref, cw_ref, alog_r_ref, alog_c_ref, dtb_r_ref, dtb_c_ref,
         w_ref, u0_ref, ke_ref, gt_ref, qs_ref, qk_ref) = refs
    else:
        (kp_ref, kc_ref, kn_ref, vp_ref, vc_ref, vn_ref,
         gcol_ref, grow_ref, cw_ref, alog_r_ref, alog_c_ref, dtb_r_ref, dtb_c_ref,
         w_ref, u0_ref, ke_ref, gt_ref) = refs
    c = pl.program_id(1)
    C = DN_CHUNK
    R = DN_PREP_CHUNKS * C
    has_prev = (c > 0).astype(BF16)
    has_next = (c < n_chunks // DN_PREP_CHUNKS - 1).astype(BF16)

    pad = DN_CONV // 2
    n_sh = DN_CONV - 1
    sr = lax.broadcasted_iota(I32, (n_sh * R, R + 2 * DN_HALO), 0)
    sc = lax.broadcasted_iota(I32, (n_sh * R, R + 2 * DN_HALO), 1)
    blk = sr // R
    off = jnp.where(blk < pad, blk - pad, blk - pad + 1)
    shift_mat = (sc == DN_HALO + (sr - blk * R) + off).astype(BF16)

    def conv_silu(p_ref, c_ref, n_ref, part):
        cur = c_ref[...]
        ext = jnp.concatenate([p_ref[...] * has_prev, cur, n_ref[...] * has_next], axis=0)
        sh = _dot(shift_mat, ext)
        taps = [sh[j * R:(j + 1) * R] for j in range(pad)] + [cur.astype(F32)] + \
               [sh[j * R:(j + 1) * R] for j in range(pad, n_sh)]
        y = None
        for i in range(DN_CONV):
            t = taps[i] * cw_ref[i:i + 1, part * D_MODEL:(part + 1) * D_MODEL]
            y = t if y is None else y + t
        return _silu(y)

    k_all = conv_silu(kp_ref, kc_ref, kn_ref, 1)
    v_all = conv_silu(vp_ref, vc_ref, vn_ref, 2)
    q_all = conv_silu(qp_ref, qc_ref, qn_ref, 0) if with_q else None

    ri = lax.broadcasted_iota(I32, (C, C), 0)
    ci = lax.broadcasted_iota(I32, (C, C), 1)
    eye = (ri == ci).astype(F32)
    CC = range(DN_PREP_CHUNKS)
    H = range(DN_HEADS)
    CH = [(cc, h) for cc in CC for h in H]
    ch = {key: i for i, key in enumerate(CH)}
    rows = [slice(cc * C, (cc + 1) * C) for cc in CC]
    lanes = [slice(h * DN_DIM, (h + 1) * DN_DIM) for h in H]
    kh = [k_all[rows[cc], lanes[h]] for cc, h in CH]
    kh = [k * lax.rsqrt(jnp.sum(k * k, axis=-1, keepdims=True) + EPS) for k in kh]
    kb = [k.astype(BF16) for k in kh]
    vh = [v_all[rows[cc], lanes[h]] for cc, h in CH]
    if with_q:
        qh = [q_all[rows[cc], lanes[h]] for cc, h in CH]
        qh = [q * (lax.rsqrt(jnp.sum(q * q, axis=-1, keepdims=True) + EPS) * DN_DIM ** -0.5) for q in qh]
        gram = [_dot_nt(jnp.concatenate([kb[j], qh[j].astype(BF16)], axis=0), kb[j]) for j in range(len(CH))]
        kk = [g[:C] for g in gram]
        qk_raw = [g[C:] for g in gram]
    else:
        kk = [_dot_nt(k, k) for k in kb]

    D2 = range(2)
    DH = [(cc, d, h) for cc in CC for d in D2 for h in H]
    incl = [(ri >= ci), (ri <= ci)]
    strict = [(ri > ci), (ri < ci)]
    gam_col, gam_row, gam_tot, beta_col = {}, {}, {}, {}
    for cc in CC:
        for d in D2:
            gcol = gcol_ref[d, rows[cc], :]
            ld_col = -jnp.exp(alog_r_ref[d]) * _softplus(gcol[:, 0:DN_HEADS] + dtb_r_ref[d])
            ld_row = -jnp.exp(alog_c_ref[d]) * _softplus(grow_ref[d, cc][0:DN_HEADS, :] + dtb_c_ref[d])
            beta_col[cc, d] = _sigmoid(gcol[:, DN_HEADS:2 * DN_HEADS])
            gam_col[cc, d] = jnp.dot(incl[d].astype(F32), ld_col, preferred_element_type=F32,
                                     precision=lax.Precision.HIGHEST)
            gam_row[cc, d] = jnp.dot(ld_row, incl[1 - d].astype(F32), preferred_element_type=F32,
                                     precision=lax.Precision.HIGHEST)
            tot = jnp.sum(ld_col, axis=0, keepdims=True)
            gam_tot[cc, d] = tot
            gt_ref[d, cc] = tot
    gc = [gam_col[cc, d][:, h:h + 1] for cc, d, h in DH]
    bc = [beta_col[cc, d][:, h:h + 1] for cc, d, h in DH]
    decay = [jnp.exp(jnp.where(incl[d], gc[i] - gam_row[cc, d][h:h + 1, :], NEG_BIG))
             for i, (cc, d, h) in enumerate(DH)]
    x = [-(jnp.where(strict[d], decay[i], 0.0) * bc[i] * kk[ch[cc, h]]) for i, (cc, d, h) in enumerate(DH)]
    N = range(len(DH))
    p = [eye + x[i] for i in N]
    xb = [x[i].astype(BF16) for i in N]
    x = [_dot(xb[i], xb[i]) for i in N]
    n_fac = int(math.log2(C)) - 1
    for j in range(n_fac):
        xb = [x[i].astype(BF16) for i in N]
        if j < n_fac - 1:
            r = [_dot(xb[i], jnp.concatenate([xb[i], p[i].astype(BF16)], axis=1)) for i in N]
            x = [r[i][:, :C] for i in N]
            p = [p[i] + r[i][:, C:] for i in N]
        else:
            p = [p[i] + _dot(xb[i], p[i].astype(BF16)) for i in N]
    rhs = [jnp.concatenate([kh[ch[cc, h]] * (bc[i] * jnp.exp(gc[i])), vh[ch[cc, h]] * bc[i]],
                           axis=1).astype(BF16) for i, (cc, d, h) in enumerate(DH)]
    sol = [_dot(p[i].astype(BF16), rhs[i]) for i in N]
    for i, (cc, d, h) in enumerate(DH):
        j = ch[cc, h]
        w_ref[d, rows[cc], lanes[h]] = sol[i][:, :DN_DIM].astype(BF16)
        u0_ref[d, rows[cc], lanes[h]] = sol[i][:, DN_DIM:].astype(BF16)
        ke_ref[d, rows[cc], lanes[h]] = (kh[j] * jnp.exp(gam_tot[cc, d][:, h:h + 1] - gc[i])).astype(BF16)
        if with_q:
            qs_ref[d, rows[cc], lanes[h]] = (qh[j] * jnp.exp(gc[i])).astype(BF16)
            qk_ref[d, rows[cc], h * C:(h + 1) * C] = (qk_raw[j] * decay[i]).astype(BF16)


def _dn_scan_kernel(*refs, with_q, n_chunks):
    n_state = 2 * DN_HEADS
    s_refs = refs[-n_state:]
    refs = refs[:-n_state]
    if with_q:
        (w0, w1, u0, u1, k0, k1, g0, g1, qs0, qs1, qk0, qk1, s0_ref, o0_ref, o1_ref, sfin_ref) = refs
        qs_r, qk_r, o_r = (qs0, qs1), (qk0, qk1), (o0_ref, o1_ref)
    else:
        (w0, w1, u0, u1, k0, k1, g0, g1, s0_ref, sfin_ref) = refs
    w_r, u_r, k_r, g_r = (w0, w1), (u0, u1), (k0, k1), (g0, g1)
    c = pl.program_id(1)
    C = DN_CHUNK
    DH = [(d, h) for d in range(2) for h in range(DN_HEADS)]
    N = range(len(DH))
    lanes = [slice(h * DN_DIM, (h + 1) * DN_DIM) for h in range(DN_HEADS)]

    @pl.when(c == 0)
    def _():
        for i, (d, h) in enumerate(DH):
            s_refs[i][...] = s0_ref[d, h]

    s = [s_refs[i][...] for i in N]
    for sub in range(DN_SCAN_CHUNKS):
        cix = (sub, DN_SCAN_CHUNKS - 1 - sub)
        rows = [slice(cix[d] * C, (cix[d] + 1) * C) for d in range(2)]
        sb = [s[i].astype(BF16) for i in N]
        w = [w_r[d][rows[d], lanes[h]] for d, h in DH]
        if with_q:
            wq = [jnp.concatenate([w[i], qs_r[d][rows[d], lanes[h]]], axis=0) for i, (d, h) in enumerate(DH)]
            ws = [_dot(wq[i], sb[i]) for i in N]
            ub = [(u_r[d][rows[d], lanes[h]].astype(F32) - ws[i][:C]).astype(BF16)
                  for i, (d, h) in enumerate(DH)]
            qu = [_dot(qk_r[d][rows[d], h * C:(h + 1) * C], ub[i]) for i, (d, h) in enumerate(DH)]
            for i, (d, h) in enumerate(DH):
                o_r[d][rows[d], lanes[h]] = (ws[i][C:] + qu[i]).astype(BF16)
        else:
            ws = [_dot(w[i], sb[i]) for i in N]
            ub = [(u_r[d][rows[d], lanes[h]].astype(F32) - ws[i]).astype(BF16) for i, (d, h) in enumerate(DH)]
        ku = [_dot_tn(k_r[d][rows[d], lanes[h]], ub[i]) for i, (d, h) in enumerate(DH)]
        s = [jnp.exp(g_r[d][cix[d]][:, h:h + 1]) * s[i] + ku[i] for i, (d, h) in enumerate(DH)]
    for i in N:
        s_refs[i][...] = s[i]

    @pl.when(c == n_chunks // DN_SCAN_CHUNKS - 1)
    def _():
        for i, (d, h) in enumerate(DH):
            sfin_ref[d, h] = s_refs[i][...]


def _deltanet(main3, gate_col, gate_row, conv_w, alog, dtb, s0, with_q):
    B, T, _ = main3.shape
    C = DN_CHUNK
    n_chunks = T // C
    NC = DN_PREP_CHUNKS
    R = NC * C
    assert T % R == 0
    hpc = R // DN_HALO
    n_halo = T // DN_HALO

    def trio(col):
        return [pl.BlockSpec((None, DN_HALO, D_MODEL), lambda b, c: (b, jnp.maximum(c * hpc - 1, 0), col)),
                pl.BlockSpec((None, R, D_MODEL), lambda b, c: (b, c, col)),
                pl.BlockSpec((None, DN_HALO, D_MODEL),
                             lambda b, c: (b, jnp.minimum((c + 1) * hpc, n_halo - 1), col))]

    vec = lambda shape: pl.BlockSpec(shape, lambda b, c: (0,) * len(shape))
    in_specs = (trio(COL_DN_Q) if with_q else []) + trio(COL_DN_K) + trio(COL_DN_V) + [
        pl.BlockSpec((2, None, R, 2 * DN_HEADS), lambda b, c: (0, b, c, 0)),
        pl.BlockSpec((2, None, NC, 2 * DN_HEADS, C), lambda b, c: (0, b, c, 0, 0)),
        vec((DN_CONV, 3 * D_MODEL)),
        vec((2, 1, DN_HEADS)), vec((2, DN_HEADS, 1)), vec((2, 1, DN_HEADS)), vec((2, DN_HEADS, 1)),
    ]
    wide = lambda n: (pl.BlockSpec((2, None, R, n), lambda b, c: (0, b, c, 0)),
                      jax.ShapeDtypeStruct((2, B, T, n), BF16))
    outs = [wide(D_MODEL), wide(D_MODEL), wide(D_MODEL),
            (pl.BlockSpec((2, None, NC, 1, DN_HEADS), lambda b, c: (0, b, c, 0, 0)),
             jax.ShapeDtypeStruct((2, B, n_chunks, 1, DN_HEADS), F32))]
    if with_q:
        outs += [wide(D_MODEL), wide(DN_HEADS * C)]
    n_main = 3 if with_q else 2
    prep = pl.pallas_call(
        functools.partial(_dn_prep_kernel, with_q=with_q, n_chunks=n_chunks),
        grid=(B, n_chunks // NC),
        in_specs=in_specs, out_specs=[o[0] for o in outs], out_shape=[o[1] for o in outs],
        compiler_params=_cparams(("arbitrary", "arbitrary")),
        name="dn_prep_q" if with_q else "dn_prep",
    )(*([main3] * (3 * n_main)), gate_col, gate_row, conv_w,
      alog.reshape(2, 1, DN_HEADS), alog.reshape(2, DN_HEADS, 1),
      dtb.reshape(2, 1, DN_HEADS), dtb.reshape(2, DN_HEADS, 1))

    NS = DN_SCAN_CHUNKS
    RS = NS * C
    n_steps = n_chunks // NS
    assert n_chunks % NS == 0

    def both_dirs(arr, n):
        if n is None:
            return [pl.BlockSpec((None, None, NS, 1, DN_HEADS), lambda b, c: (0, b, c, 0, 0)),
                    pl.BlockSpec((None, None, NS, 1, DN_HEADS),
                                 lambda b, c: (1, b, n_steps - 1 - c, 0, 0))], [arr, arr]
        return [pl.BlockSpec((None, None, RS, n), lambda b, c: (0, b, c, 0)),
                pl.BlockSpec((None, None, RS, n), lambda b, c: (1, b, n_steps - 1 - c, 0))], [arr, arr]

    specs, args = [], []
    widths = [D_MODEL, D_MODEL, D_MODEL, None] + ([D_MODEL, DN_HEADS * C] if with_q else [])
    for arr, n in zip(prep, widths):
        sp, ar = both_dirs(arr, n)
        specs += sp
        args += ar
    s_spec = pl.BlockSpec((2, None, DN_HEADS, DN_DIM, DN_DIM), lambda b, c: (0, b, 0, 0, 0))
    s_shape = jax.ShapeDtypeStruct((2, B, DN_HEADS, DN_DIM, DN_DIM), F32)
    if with_q:
        out_specs = [pl.BlockSpec((None, RS, D_MODEL), lambda b, c: (b, c, 0)),
                     pl.BlockSpec((None, RS, D_MODEL), lambda b, c: (b, n_steps - 1 - c, 0)), s_spec]
        out_shape = [jax.ShapeDtypeStruct((B, T, D_MODEL), BF16)] * 2 + [s_shape]
    else:
        out_specs, out_shape = [s_spec], [s_shape]
    out = pl.pallas_call(
        functools.partial(_dn_scan_kernel, with_q=with_q, n_chunks=n_chunks),
        grid=(B, n_steps),
        in_specs=specs + [s_spec], out_specs=out_specs, out_shape=out_shape,
        scratch_shapes=[pltpu.VMEM((DN_DIM, DN_DIM), F32)] * (2 * DN_HEADS),
        compiler_params=_cparams(("arbitrary", "arbitrary")),
        name="dn_scan_q" if with_q else "dn_scan",
    )(*args, s0)
    return (out[0], out[1], out[2]) if with_q else (None, None, out[0])


def _rope_tables(S):
    half = AT_DIM // 2
    nf = half // 2
    inv_freq = ROPE_BASE ** (-jnp.arange(nf, dtype=F32) / nf)
    t = jnp.arange(S, dtype=jnp.int32)
    row = (t // GRID_W).astype(F32)
    col = (t % GRID_W).astype(F32)
    lane = jnp.arange(LANES)
    dd = lane % AT_DIM
    pos = jnp.where((dd < half)[None, :], row[:, None], col[:, None])
    ang = pos * inv_freq[lane % nf][None, :]
    first = ((lane % half) < nf)[None, :]
    sin = jnp.sin(ang)
    return jnp.cos(ang), jnp.where(first, -sin, 0.0), jnp.where(first, 0.0, sin)


def _rope_kernel(q_ref, k_ref, v_ref, cos_ref, sa_ref, sb_ref, qo_ref, ko_ref, vo_ref):
    cos, sa, sb = cos_ref[...], sa_ref[...], sb_ref[...]
    nf = AT_DIM // 4

    def rot(x):
        return x * cos + pltpu.roll(x, LANES - nf, 1) * sa + pltpu.roll(x, nf, 1) * sb

    for j in range(AT_Q_HEADS * AT_DIM // LANES):
        lanes = slice(j * LANES, (j + 1) * LANES)
        qo_ref[:, lanes] = (rot(q_ref[:, lanes].astype(F32)) * AT_DIM ** -0.5).astype(BF16)
    ko_ref[...] = rot(k_ref[...]).astype(BF16)
    vo_ref[...] = v_ref[...].astype(BF16)


def _rope(main, small, tables, S, tm=512):
    n_tok = main.shape[0]
    per_seq = S // tm
    tab_spec = pl.BlockSpec((tm, LANES), lambda i: (i % per_seq, 0))
    return pl.pallas_call(
        _rope_kernel,
        grid=(n_tok // tm,),
        in_specs=[pl.BlockSpec((tm, D_MODEL), lambda i: (i, COL_AT_Q)),
                  pl.BlockSpec((tm, LANES), lambda i: (i, 0)),
                  pl.BlockSpec((tm, LANES), lambda i: (i, 1)),
                  tab_spec, tab_spec, tab_spec],
        out_specs=[pl.BlockSpec((tm, D_MODEL), lambda i: (i, 0)),
                   pl.BlockSpec((tm, LANES), lambda i: (i, 0)),
                   pl.BlockSpec((tm, LANES), lambda i: (i, 0))],
        out_shape=[jax.ShapeDtypeStruct((n_tok, D_MODEL), BF16),
                   jax.ShapeDtypeStruct((n_tok, LANES), BF16),
                   jax.ShapeDtypeStruct((n_tok, LANES), BF16)],
        compiler_params=_cparams(("arbitrary",)),
        name="rope",
    )(main, small, small, *tables)


def _attn_kernel(*refs, local, n_blocks, q_scale):
    if local:
        (q_ref, kp_ref, kc_ref, kn_ref, vp_ref, vc_ref, vn_ref, kx_ref, vx_ref, sink_ref, o_ref) = refs
    else:
        (q_ref, kx_ref, vx_ref, sink_ref, o_ref) = refs
    P = AT_BLOCK
    G = AT_Q_HEADS // AT_KV_HEADS
    L = kx_ref.shape[0]
    kx = kx_ref[...].astype(BF16)
    vx = vx_ref[...].astype(BF16)
    if local:
        i = pl.program_id(1)
        k_all = jnp.concatenate([kp_ref[...], kc_ref[...], kn_ref[...], kx], axis=0)
        v_all = jnp.concatenate([vp_ref[...], vc_ref[...], vn_ref[...], vx], axis=0)
        qi = lax.broadcasted_iota(I32, (P, P), 0)
        kj = lax.broadcasted_iota(I32, (P, P), 1)
        b_prev = jnp.where(kj >= qi, 0.0, NEG_BIG) + jnp.where(i > 0, 0.0, NEG_BIG)
        b_next = jnp.where(kj <= qi, 0.0, NEG_BIG) + jnp.where(i < n_blocks - 1, 0.0, NEG_BIG)
        bias = jnp.concatenate([b_prev, jnp.zeros((P, P), F32), b_next, jnp.zeros((P, L), F32)], axis=1)
    else:
        k_all, v_all, bias = kx, vx, None
    n_keys = k_all.shape[0]
    lo = lax.broadcasted_iota(I32, (P, LANES), 1) < AT_DIM
    qf = q_ref[...].astype(F32) * q_scale
    pieces = []
    for qh in range(AT_Q_HEADS):
        blk = qf[:, (qh // 2) * LANES:(qh // 2 + 1) * LANES]
        want_lo = qh // G == 0
        if want_lo != (qh % 2 == 0):
            blk = pltpu.roll(blk, AT_DIM, 1)
        pieces.append(jnp.where(lo if want_lo else ~lo, blk, 0.0).astype(BF16))
    qs = jnp.concatenate(pieces, axis=0)
    s = _dot_nt(qs, k_all)
    if bias is not None:
        s = (s.reshape(AT_Q_HEADS, P, n_keys) + bias[None]).reshape(AT_Q_HEADS * P, n_keys)
    sink = sink_ref[...]
    m = jnp.maximum(jnp.max(s, axis=-1, keepdims=True), sink)
    p = jnp.exp(s - m)
    den = jnp.sum(p, axis=-1, keepdims=True) + jnp.exp(sink - m)
    o = _dot(p.astype(BF16), v_all) / den
    for j in range(AT_Q_HEADS // 2):
        a = o[(2 * j) * P:(2 * j + 1) * P]
        b = o[(2 * j + 1) * P:(2 * j + 2) * P]
        if (2 * j) // G == 0:
            out = jnp.where(lo, a, pltpu.roll(b, AT_DIM, 1))
        else:
            out = jnp.where(lo, pltpu.roll(a, AT_DIM, 1), b)
        o_ref[:, j * LANES:(j + 1) * LANES] = out.astype(BF16)


def _attention_local(q_r, k_r, v_r, small_c, sinks, B, S, L):
    P = AT_BLOCK
    nb = S // P

    def kv_trio():
        return [pl.BlockSpec((P, LANES), lambda b, i: (b * nb + jnp.maximum(i - 1, 0), 0)),
                pl.BlockSpec((P, LANES), lambda b, i: (b * nb + i, 0)),
                pl.BlockSpec((P, LANES), lambda b, i: (b * nb + jnp.minimum(i + 1, nb - 1), 0))]

    return pl.pallas_call(
        functools.partial(_attn_kernel, local=True, n_blocks=nb, q_scale=1.0),
        grid=(B, nb),
        in_specs=[pl.BlockSpec((P, D_MODEL), lambda b, i: (b * nb + i, 0))] + kv_trio() + kv_trio() + [
            pl.BlockSpec((L, LANES), lambda b, i: (b, 0)),
            pl.BlockSpec((L, LANES), lambda b, i: (b, 1)),
            pl.BlockSpec((AT_Q_HEADS * AT_BLOCK, 1), lambda b, i: (0, 0))],
        out_specs=pl.BlockSpec((P, D_MODEL), lambda b, i: (b * nb + i, 0)),
        out_shape=jax.ShapeDtypeStruct((B * S, D_MODEL), BF16),
        compiler_params=_cparams(("arbitrary", "arbitrary")),
        name="attn_local",
    )(q_r, k_r, k_r, k_r, v_r, v_r, v_r, small_c, small_c, sinks)


def _attention_ctx(main_c, small_c, sinks, B, L):
    P = AT_BLOCK
    nb = L // P
    return pl.pallas_call(
        functools.partial(_attn_kernel, local=False, n_blocks=nb, q_scale=AT_DIM ** -0.5),
        grid=(B, nb),
        in_specs=[pl.BlockSpec((P, D_MODEL), lambda b, i: (b * nb + i, COL_AT_Q)),
                  pl.BlockSpec((L, LANES), lambda b, i: (b, 0)),
                  pl.BlockSpec((L, LANES), lambda b, i: (b, 1)),
                  pl.BlockSpec((AT_Q_HEADS * AT_BLOCK, 1), lambda b, i: (0, 0))],
        out_specs=pl.BlockSpec((P, D_MODEL), lambda b, i: (b * nb + i, 0)),
        out_shape=jax.ShapeDtypeStruct((B * L, D_MODEL), BF16),
        compiler_params=_cparams(("arbitrary", "arbitrary")),
        name="attn_ctx",
    )(main_c, small_c, small_c, sinks)


def _merge_kernel(ysg_ref, of_ref, ob_ref, dng_ref, yat_ref, g0_ref, g1_ref, g2_ref, x_ref, mod_ref,
                  dn_norm_ref, post_ref, pre_ref, wsg_ref, wdn_ref, wat_ref, wout_ref, rw_ref, rb_ref,
                  xo_ref, h2_ref, lg_ref):
    o = of_ref[...].astype(F32) + ob_ref[...].astype(F32)
    dn_g = dn_norm_ref[...]
    parts = []
    for h in range(DN_HEADS):
        lanes = slice(h * DN_DIM, (h + 1) * DN_DIM)
        parts.append(_rms(o[:, lanes], dn_g) * _silu(dng_ref[:, lanes].astype(F32)))
    ydn = jnp.concatenate(parts, axis=1).astype(BF16)
    m = (_sigmoid(g0_ref[...].astype(F32)) * _dot(ysg_ref[...], wsg_ref[...])
         + _sigmoid(g1_ref[...].astype(F32)) * _dot(ydn, wdn_ref[...])
         + _sigmoid(g2_ref[...].astype(F32)) * _dot(yat_ref[...], wat_ref[...]))
    y = _dot(m.astype(BF16), wout_ref[...])
    gate1 = mod_ref[:, 2 * D_MODEL:3 * D_MODEL]
    sh2 = mod_ref[:, 3 * D_MODEL:4 * D_MODEL]
    sc2 = mod_ref[:, 4 * D_MODEL:5 * D_MODEL]
    xn = x_ref[...] + gate1 * _rms(y, post_ref[...])
    xo_ref[...] = xn
    h2 = _rms(xn, pre_ref[...]) * (1.0 + sc2) + sh2
    h2_ref[...] = h2
    lg_ref[...] = _dot(h2.astype(BF16), rw_ref[...]) + rb_ref[...]


def _merge(ysg, o_fwd, o_bwd, main, yat, x, mod, mod_row, lw, tm=512):
    n_tok = x.shape[0]
    const = lambda i: (0, 0)
    wspec = pl.BlockSpec((D_MODEL, D_MODEL), const, pipeline_mode=pl.Buffered(1))
    vspec = pl.BlockSpec((1, D_MODEL), const)
    return pl.pallas_call(
        _merge_kernel,
        grid=(n_tok // tm,),
        in_specs=[pl.BlockSpec((tm, D_MODEL), lambda i: (i, 0)),
                  pl.BlockSpec((tm, D_MODEL), lambda i: (i, 0)),
                  pl.BlockSpec((tm, D_MODEL), lambda i: (i, 0)),
                  pl.BlockSpec((tm, D_MODEL), lambda i: (i, COL_DN_G)),
                  pl.BlockSpec((tm, D_MODEL), lambda i: (i, 0)),
                  pl.BlockSpec((tm, D_MODEL), lambda i: (i, COL_GATE0)),
                  pl.BlockSpec((tm, D_MODEL), lambda i: (i, COL_GATE0 + 1)),
                  pl.BlockSpec((tm, D_MODEL), lambda i: (i, COL_GATE0 + 2)),
                  pl.BlockSpec((tm, D_MODEL), lambda i: (i, 0)),
                  pl.BlockSpec((None, 1, 6 * D_MODEL), lambda i: (mod_row(i * tm), 0, 0)),
                  pl.BlockSpec((1, DN_DIM), const), vspec, vspec,
                  wspec, wspec, wspec, wspec,
                  pl.BlockSpec((D_MODEL, LANES), const), pl.BlockSpec((1, LANES), const)],
        out_specs=[pl.BlockSpec((tm, D_MODEL), lambda i: (i, 0)),
                   pl.BlockSpec((tm, D_MODEL), lambda i: (i, 0)),
                   pl.BlockSpec((tm, LANES), lambda i: (i, 0))],
        out_shape=[jax.ShapeDtypeStruct((n_tok, D_MODEL), F32),
                   jax.ShapeDtypeStruct((n_tok, D_MODEL), F32),
                   jax.ShapeDtypeStruct((n_tok, LANES), F32)],
        compiler_params=_cparams(("arbitrary",)),
        name="merge",
    )(ysg, o_fwd, o_bwd, main, yat, main, main, main, x, mod,
      lw["dn_norm_g"], lw["norm_post_mix"], lw["norm_pre_ffn"],
      lw["w_proj_sg"], lw["w_proj_dn"], lw["w_proj_at"], lw["w_out"], lw["router_w"], lw["router_b"])


MOE_TOK = 256
MOE_PIECE = 8
MOE_BUF = MOE_TOK * TOP_K + N_EXPERTS * MOE_PIECE
MOE_META = 4 * N_EXPERTS


def _route_kernel(lg_ref, gate_ref, lpos_ref, tcnt_ref):
    tm = lg_ref.shape[0]
    l = lg_ref[...]
    lane = lax.broadcasted_iota(I32, l.shape, 1).astype(F32)
    vals, onehots = [], []
    for k in range(TOP_K):
        m = jnp.max(l, axis=-1, keepdims=True)
        ik = jnp.min(jnp.where(l == m, lane, float(LANES)), axis=-1, keepdims=True)
        oh = lane == ik
        vals.append(m)
        onehots.append(oh)
        l = jnp.where(oh, -jnp.inf, l)
    es = [jnp.exp(v - vals[0]) for v in vals]
    den = es[0] + es[1] + es[2] + es[3]
    sel = jnp.zeros(l.shape, F32)
    for k in range(TOP_K):
        gate_ref[:, k:k + 1] = es[k] / den
        sel = sel + onehots[k].astype(F32)
    ri = lax.broadcasted_iota(I32, (tm, tm), 0)
    ci = lax.broadcasted_iota(I32, (tm, tm), 1)
    before = _dot((ri > ci).astype(BF16), sel.astype(BF16))
    tcnt = jnp.sum(sel, axis=0, keepdims=True)
    tcnt_ref[...] = tcnt
    n_piece = jnp.floor((tcnt + (MOE_PIECE - 1)) * (1.0 / MOE_PIECE))
    ei = lax.broadcasted_iota(I32, (LANES, LANES), 0)
    ej = lax.broadcasted_iota(I32, (LANES, LANES), 1)
    run_start = _dot(jnp.broadcast_to(n_piece, (8, LANES)).astype(BF16),
                     (ei < ej).astype(BF16))[0:1] * float(MOE_PIECE)
    pos = before + run_start
    for k in range(TOP_K):
        lpos_ref[:, k:k + 1] = jnp.sum(jnp.where(onehots[k], pos, 0.0), axis=-1,
                                       keepdims=True).astype(I32)


def _route(logits):
    n_tok = logits.shape[0]
    tm = MOE_TOK
    n_t = n_tok // tm
    small = lambda dt: jax.ShapeDtypeStruct((n_tok, TOP_K), dt)
    kspec = pl.BlockSpec((tm, TOP_K), lambda i: (i, 0))
    tspec = pl.BlockSpec((None, 1, LANES), lambda i: (i, 0, 0))
    tshape = jax.ShapeDtypeStruct((n_t, 1, LANES), F32)
    return pl.pallas_call(
        _route_kernel,
        grid=(n_t,),
        in_specs=[pl.BlockSpec((tm, LANES), lambda i: (i, 0))],
        out_specs=[kspec, kspec, tspec],
        out_shape=[small(F32), small(I32), tshape],
        compiler_params=_cparams(("arbitrary",)),
        name="route",
    )(logits)


def _run_copies(meta_ref, base, src_of, dst_of, sem, start):
    def per_expert(e, carry):
        local = pl.multiple_of(meta_ref[base + e], MOE_PIECE)
        slot = pl.multiple_of(meta_ref[base + N_EXPERTS + e], MOE_PIECE)

        def per_piece(p, c2):
            cp = pltpu.make_async_copy(src_of(local + p * MOE_PIECE, slot + p * MOE_PIECE),
                                       dst_of(local + p * MOE_PIECE, slot + p * MOE_PIECE), sem)
            if start:
                cp.start()
            else:
                cp.wait()
            return c2

        return lax.fori_loop(0, meta_ref[base + 2 * N_EXPERTS + e], per_piece, carry)

    lax.fori_loop(0, N_EXPERTS, per_expert, 0)


def _dispatch_kernel(meta_ref, prev_ref, zmeta_ref, lpos_ref, h_ref, xs_ref, buf_ref, buf1_ref, zero_ref,
                     sem, sem1, *, tm_e):
    tm = MOE_TOK
    step = pl.program_id(0)
    bufs, sems = (buf_ref, buf1_ref), (sem, sem1)
    rows = lambda ref, r: ref.at[pl.ds(r, MOE_PIECE)]

    @pl.when(step == 0)
    def _():
        zero_ref[...] = jnp.zeros_like(zero_ref)

        def zero_tail(start):
            def per_expert(e, carry):
                z0 = pl.multiple_of(zmeta_ref[e], MOE_PIECE)

                def per_piece(p, c2):
                    cp = pltpu.make_async_copy(zero_ref, rows(xs_ref, z0 + p * MOE_PIECE), sem)
                    if start:
                        cp.start()
                    else:
                        cp.wait()
                    return c2

                return lax.fori_loop(0, zmeta_ref[N_EXPERTS + e], per_piece, carry)

            lax.fori_loop(0, N_EXPERTS, per_expert, 0)

        zero_tail(True)
        zero_tail(False)

        buf_ref[0:tm_e, :] = jnp.zeros((tm_e, D_MODEL // 2), U32)

        def zero_tiles(start):
            def per_tile(p, carry):
                t0 = pl.multiple_of((zmeta_ref[2 * N_EXPERTS] + p) * tm_e, tm_e)
                cp = pltpu.make_async_copy(buf_ref.at[pl.ds(0, tm_e)], xs_ref.at[pl.ds(t0, tm_e)], sem)
                if start:
                    cp.start()
                else:
                    cp.wait()
                return carry

            lax.fori_loop(0, zmeta_ref[2 * N_EXPERTS + 1], per_tile, 0)

        zero_tiles(True)
        zero_tiles(False)

    def group(j):
        s_iota = lax.broadcasted_iota(I32, (MOE_BUF, tm), 0)
        perm = jnp.zeros((MOE_BUF, tm), F32)
        for k in range(TOP_K):
            perm = perm + (s_iota == lpos_ref[k:k + 1, j * tm:(j + 1) * tm]).astype(F32)
        bufs[j][...] = _pack_bf16_pairs(_dot(perm.astype(BF16), h_ref[j * tm:(j + 1) * tm, :].astype(BF16)))

    def copies(mref, j, start):
        _run_copies(mref, j * MOE_META, lambda loc, slot: rows(bufs[j], loc),
                    lambda loc, slot: rows(xs_ref, slot), sems[j], start)

    group(0)
    copies(meta_ref, 0, True)

    @pl.when(step > 0)
    def _():
        copies(prev_ref, 1, False)

    group(1)
    copies(meta_ref, 1, True)
    copies(meta_ref, 0, False)

    @pl.when(step == pl.num_programs(0) - 1)
    def _():
        copies(meta_ref, 1, False)


def _dispatch(meta, zmeta, lpos_t, h2, n_slots, tm_e):
    n_tok = h2.shape[0]
    tm = 2 * MOE_TOK
    assert tm_e <= MOE_BUF and n_tok % tm == 0
    return pl.pallas_call(
        functools.partial(_dispatch_kernel, tm_e=tm_e),
        grid=(n_tok // tm,),
        in_specs=[pl.BlockSpec((2 * MOE_META,), lambda i: (i,), memory_space=pltpu.SMEM),
                  pl.BlockSpec((2 * MOE_META,), lambda i: (jnp.maximum(i - 1, 0),), memory_space=pltpu.SMEM),
                  pl.BlockSpec((MOE_META,), lambda i: (0,), memory_space=pltpu.SMEM),
                  pl.BlockSpec((TOP_K, tm), lambda i: (0, i)),
                  pl.BlockSpec((tm, D_MODEL), lambda i: (i, 0))],
        out_specs=pl.BlockSpec(memory_space=pl.ANY),
        out_shape=jax.ShapeDtypeStruct((n_slots, D_MODEL // 2), U32),
        scratch_shapes=[pltpu.VMEM((MOE_BUF, D_MODEL // 2), U32), pltpu.VMEM((MOE_BUF, D_MODEL // 2), U32),
                        pltpu.VMEM((MOE_PIECE, D_MODEL // 2), U32),
                        pltpu.SemaphoreType.DMA, pltpu.SemaphoreType.DMA],
        compiler_params=_cparams(("arbitrary",)),
        name="moe_dispatch",
    )(meta, meta, zmeta, lpos_t, h2)


def _pack_bf16_pairs(x):
    w = x.shape[1] // 2
    xb = x.astype(BF16).astype(F32)
    lo = lax.shift_right_logical(lax.bitcast_convert_type(xb[:, :w], U32), jnp.uint32(16))
    hi = lax.bitcast_convert_type(xb[:, w:], U32) & jnp.uint32(0xFFFF0000)
    return hi | lo


def _unpack_bf16_pairs(p):
    lo = lax.bitcast_convert_type(lax.shift_left(p, jnp.uint32(16)), F32)
    hi = lax.bitcast_convert_type(p & jnp.uint32(0xFFFF0000), F32)
    return jnp.concatenate([lo, hi], axis=1).astype(BF16)


def _expert_kernel(te_ref, first_ref, nu_ref, xs_ref, wgu_ref, bgu_ref, wd_ref, bd_ref, y_ref,
                   wgu_b_ref, wd_b_ref):
    del te_ref
    i = pl.program_id(0)

    @pl.when(first_ref[i] == 1)
    def _():
        wgu_b_ref[...] = wgu_ref[...].astype(BF16)
        wd_b_ref[...] = wd_ref[...].astype(BF16)

    @pl.when(i < nu_ref[0])
    def _():
        gu = _dot(_unpack_bf16_pairs(xs_ref[...]), wgu_b_ref[...]) + bgu_ref[...]
        g = jnp.minimum(gu[:, :D_EXPERT], SWIGLU_LIMIT)
        lin = jnp.clip(gu[:, D_EXPERT:], -SWIGLU_LIMIT, SWIGLU_LIMIT)
        act = g * _sigmoid(SWIGLU_ALPHA * g) * (lin + 1.0)
        y_ref[...] = _pack_bf16_pairs(_dot(act.astype(BF16), wd_b_ref[...]) + bd_ref[...])

    @pl.when(i >= nu_ref[0])
    def _():
        y_ref[...] = jnp.zeros_like(y_ref)


def _experts(tile_expert, n_used, xs, wgu, bgu, wd, bd, tm):
    n_slots = xs.shape[0]
    n_tiles = n_slots // tm
    first = jnp.concatenate([jnp.ones((1,), I32),
                             (tile_expert[1:] != tile_expert[:-1]).astype(I32)])

    def row(i, te, fi, nu):
        return (jnp.minimum(i, nu[0] - 1), 0)

    grid_spec = pltpu.PrefetchScalarGridSpec(
        num_scalar_prefetch=3,
        grid=(n_tiles,),
        in_specs=[pl.BlockSpec((tm, D_MODEL // 2), row),
                  pl.BlockSpec((None, D_MODEL, 2 * D_EXPERT), lambda i, te, fi, nu: (te[i], 0, 0)),
                  pl.BlockSpec((None, 1, 2 * D_EXPERT), lambda i, te, fi, nu: (te[i], 0, 0)),
                  pl.BlockSpec((None, D_EXPERT, D_MODEL), lambda i, te, fi, nu: (te[i], 0, 0)),
                  pl.BlockSpec((None, 1, D_MODEL), lambda i, te, fi, nu: (te[i], 0, 0))],
        out_specs=pl.BlockSpec((tm, D_MODEL // 2), lambda i, te, fi, nu: (i, 0)),
        scratch_shapes=[pltpu.VMEM((D_MODEL, 2 * D_EXPERT), BF16), pltpu.VMEM((D_EXPERT, D_MODEL), BF16)],
    )
    return pl.pallas_call(
        _expert_kernel,
        grid_spec=grid_spec,
        out_shape=jax.ShapeDtypeStruct((n_slots, D_MODEL // 2), U32),
        compiler_params=_cparams(("arbitrary",)),
        name="moe_experts",
    )(tile_expert, first, n_used, xs, wgu, bgu, wd, bd)


def _combine_kernel(meta_ref, next_ref, lpos_ref, gate_ref, x_ref, mod_ref, post_ref, y_ref, xo_ref,
                    buf_ref, buf1_ref, sem, sem1):
    tm = MOE_TOK
    step = pl.program_id(0)
    bufs, sems = (buf_ref, buf1_ref), (sem, sem1)
    rows = lambda ref, r: ref.at[pl.ds(r, MOE_PIECE)]

    def copies(mref, j, base, start):
        _run_copies(mref, base, lambda loc, slot: rows(y_ref, slot),
                    lambda loc, slot: rows(bufs[j], loc), sems[j], start)

    def reduce_tile(j):
        tok = slice(j * tm, (j + 1) * tm)
        s_iota = lax.broadcasted_iota(I32, (tm, MOE_BUF), 1)
        sel = jnp.zeros((tm, MOE_BUF), F32)
        for k in range(TOP_K):
            sel = sel + jnp.where(s_iota == lpos_ref[tok, k:k + 1], gate_ref[tok, k:k + 1], 0.0)
        y = _dot(sel.astype(BF16), _unpack_bf16_pairs(bufs[j][...]))
        gate2 = mod_ref[:, 5 * D_MODEL:6 * D_MODEL]
        xo_ref[tok, :] = x_ref[tok, :] + gate2 * _rms(y, post_ref[...])

    @pl.when(step == 0)
    def _():
        buf_ref[...] = jnp.zeros_like(buf_ref)
        buf1_ref[...] = jnp.zeros_like(buf1_ref)
        copies(meta_ref, 0, 0, True)

    copies(meta_ref, 1, MOE_META, True)
    copies(meta_ref, 0, 0, False)
    reduce_tile(0)

    @pl.when(step < pl.num_programs(0) - 1)
    def _():
        copies(next_ref, 0, 0, True)

    copies(meta_ref, 1, MOE_META, False)
    reduce_tile(1)


def _combine(meta, lpos, gate, x_mid, mod, mod_row, post_g, y):
    n_tok = x_mid.shape[0]
    tm = 2 * MOE_TOK
    n_steps = n_tok // tm
    return pl.pallas_call(
        _combine_kernel,
        grid=(n_steps,),
        in_specs=[pl.BlockSpec((2 * MOE_META,), lambda i: (i,), memory_space=pltpu.SMEM),
                  pl.BlockSpec((2 * MOE_META,), lambda i: (jnp.minimum(i + 1, n_steps - 1),),
                               memory_space=pltpu.SMEM),
                  pl.BlockSpec((tm, TOP_K), lambda i: (i, 0)),
                  pl.BlockSpec((tm, TOP_K), lambda i: (i, 0)),
                  pl.BlockSpec((tm, D_MODEL), lambda i: (i, 0)),
                  pl.BlockSpec((None, 1, 6 * D_MODEL), lambda i: (mod_row(i * tm), 0, 0)),
                  pl.BlockSpec((1, D_MODEL), lambda i: (0, 0)),
                  pl.BlockSpec(memory_space=pl.ANY)],
        out_specs=pl.BlockSpec((tm, D_MODEL), lambda i: (i, 0)),
        out_shape=jax.ShapeDtypeStruct((n_tok, D_MODEL), F32),
        scratch_shapes=[pltpu.VMEM((MOE_BUF, D_MODEL // 2), U32), pltpu.VMEM((MOE_BUF, D_MODEL // 2), U32),
                        pltpu.SemaphoreType.DMA, pltpu.SemaphoreType.DMA],
        compiler_params=_cparams(("arbitrary",)),
        name="moe_combine",
    )(meta, meta, lpos, gate, x_mid, mod, post_g, y)


def _moe(h2, logits, x_mid, mod, mod_row, lw, tm_e=512):
    n_tok = h2.shape[0]
    n_t = n_tok // MOE_TOK
    gate, lpos, tcnt = _route(logits)
    tcnt = tcnt[:, 0, :N_EXPERTS].astype(I32)
    pieces = (tcnt + MOE_PIECE - 1) // MOE_PIECE
    run_end = jnp.cumsum(pieces, axis=0) * MOE_PIECE
    used = run_end[-1]
    padded = (used + tm_e - 1) // tm_e * tm_e
    pad_end = jnp.cumsum(padded)
    offs = pad_end - padded
    n_tiles = (n_tok * TOP_K + n_t * N_EXPERTS * (MOE_PIECE - 1) + tm_e - 1) // tm_e + N_EXPERTS
    tile_start = jnp.arange(n_tiles, dtype=I32) * tm_e
    tile_expert = jnp.minimum(jnp.sum(pad_end[None, :] <= tile_start[:, None], axis=1),
                              N_EXPERTS - 1).astype(I32)
    n_used = (pad_end[-1:] // tm_e).astype(I32)
    lstart = (jnp.cumsum(pieces, axis=1) - pieces) * MOE_PIECE
    slot_start = offs[None, :] + run_end - pieces * MOE_PIECE
    meta = jnp.concatenate([lstart, slot_start, pieces, jnp.zeros_like(pieces)],
                           axis=1).reshape(-1).astype(I32)
    z0 = offs + used
    tail = jnp.stack([n_used[0], n_tiles - n_used[0]])
    zmeta = jnp.concatenate([z0, (pad_end - z0) // MOE_PIECE, tail,
                             jnp.zeros((MOE_META - 2 * N_EXPERTS - 2,), I32)]).astype(I32)
    xs = _dispatch(meta, zmeta, lpos.T, h2, n_tiles * tm_e, tm_e)
    y = _experts(tile_expert, n_used, xs, lw["exp_w_gu"], lw["exp_b_gu"], lw["exp_w_down"],
                 lw["exp_b_down"], tm_e)
    return _combine(meta, lpos, gate, x_mid, mod, mod_row, lw["norm_post_ffn"], y)


def _split_w_in(w_in):
    offs, o = {}, 0
    for name, width in (("dn_k", 1024), ("dn_v", 1024), ("dn_a", 16), ("dn_b", 16), ("at_k", 128),
                        ("at_v", 128), ("dn_q", 1024), ("dn_g", 1024), ("at_q", 1024),
                        ("sg_u", 1024), ("sg_v", 1024), ("gates", 3072)):
        offs[name] = (o, o + width)
        o += width
    sl = lambda n: w_in[:, offs[n][0]:offs[n][1]]
    w_main = jnp.concatenate([sl(n) for n in ("dn_k", "dn_v", "dn_q", "dn_g", "at_q", "sg_u", "sg_v",
                                              "gates")], axis=1).astype(BF16)
    pad = jnp.zeros((w_in.shape[0], N_SMALL_COLS - 2 * LANES - 4 * DN_HEADS), w_in.dtype)
    w_small = jnp.concatenate([sl("at_k"), sl("at_v"), sl("dn_a"), sl("dn_b"), pad], axis=1).astype(BF16)
    return w_main, w_small


def _dn_gates(small, B, T):
    ab = small[:, 2 * LANES:2 * LANES + 4 * DN_HEADS].reshape(B, T, 2, 2, DN_HEADS)
    col = jnp.transpose(ab, (3, 0, 1, 2, 4)).reshape(2, B, T, 2 * DN_HEADS)
    row = jnp.transpose(col.reshape(2, B, T // DN_CHUNK, DN_CHUNK, 2 * DN_HEADS), (0, 1, 2, 4, 3))
    return col, row


def kernel(x, c, ctx, c_ctx, w_mod, b_mod, norm_pre_mix, norm_post_mix, norm_pre_ffn, norm_post_ffn, w_in, sg_ln_g, sg_ln_b, sg_w, sg_b, dn_conv_w, dn_a_log, dn_dt_bias, dn_norm_g, at_sinks, w_proj_sg, w_proj_dn, w_proj_at, w_out, router_w, router_b, exp_w_gu, exp_b_gu, exp_w_down, exp_b_down):
    B, S, D = x.shape
    L = ctx.shape[1]
    depth = w_mod.shape[0]
    assert D == D_MODEL and S % GRID_W == 0
    n_lat, n_ctx = B * S, B * L

    rows = (B + 1 + 7) // 8 * 8
    cvec = jnp.zeros((rows, D), F32).at[:B].set(c).at[B].set(c_ctx)
    mod_all = _modulation(cvec, w_mod, b_mod)
    tables = _rope_tables(S)

    lat_row = lambda t: t // S
    ctx_row = lambda t: B
    all_row = lambda t: jnp.where(t < n_lat, t // S, B)

    xl = x.reshape(n_lat, D)
    xc = ctx.reshape(n_ctx, D)
    for l in range(depth):
        need_ctx_out = l < depth - 1
        mod = mod_all[l].reshape(rows, 1, 6 * D)
        w_main, w_small = _split_w_in(w_in[l])
        lw = {
            "dn_norm_g": dn_norm_g[l].reshape(1, -1),
            "norm_post_mix": norm_post_mix[l].reshape(1, -1),
            "norm_pre_ffn": norm_pre_ffn[l].reshape(1, -1),
            "norm_post_ffn": norm_post_ffn[l].reshape(1, -1),
            "w_proj_sg": w_proj_sg[l].astype(BF16), "w_proj_dn": w_proj_dn[l].astype(BF16),
            "w_proj_at": w_proj_at[l].astype(BF16), "w_out": w_out[l].astype(BF16),
            "router_w": jnp.pad(router_w[l], ((0, 0), (0, LANES - N_EXPERTS))).astype(BF16),
            "router_b": jnp.pad(router_b[l], (0, LANES - N_EXPERTS),
                                constant_values=NEG_BIG).reshape(1, -1),
            "exp_w_gu": exp_w_gu[l], "exp_b_gu": exp_b_gu[l].reshape(N_EXPERTS, 1, -1),
            "exp_w_down": exp_w_down[l], "exp_b_down": exp_b_down[l].reshape(N_EXPERTS, 1, -1),
        }
        pre_g = norm_pre_mix[l].reshape(1, -1)
        main, small = _inproj(xl, mod, lat_row, pre_g, w_main, w_small, min(1024, S))
        w_main_c = w_main if need_ctx_out else w_main[:, :N_CTX_MAIN_COLS]
        main_c, small_c = _inproj(xc, mod, ctx_row, pre_g, w_main_c, w_small, min(1024, n_ctx))

        sg_args = (sg_ln_g[l].reshape(1, -1), sg_ln_b[l].reshape(1, -1), sg_w[l].astype(BF16),
                   sg_b[l].T)
        ysg = _sgu(main, *sg_args)

        gcol_c, grow_c = _dn_gates(small_c, B, L)
        gcol, grow = _dn_gates(small, B, S)
        s0 = jnp.zeros((2, B, DN_HEADS, DN_DIM, DN_DIM), F32)
        of_c, ob_c, s_ctx = _deltanet(main_c.reshape(B, L, -1), gcol_c, grow_c, dn_conv_w[l], dn_a_log[l],
                                      dn_dt_bias[l], s0, need_ctx_out)
        of_l, ob_l, _ = _deltanet(main.reshape(B, S, -1), gcol, grow, dn_conv_w[l], dn_a_log[l],
                                  dn_dt_bias[l], s_ctx, True)

        sinks = jnp.repeat(at_sinks[l], AT_BLOCK).reshape(-1, 1)
        q_r, k_r, v_r = _rope(main, small, tables, S)
        yat = _attention_local(q_r, k_r, v_r, small_c, sinks, B, S, L)

        x_mid, h2, logits = _merge(ysg, of_l.reshape(n_lat, D), ob_l.reshape(n_lat, D), main, yat, xl, mod,
                                   lat_row, lw)
        if need_ctx_out:
            ysg_c = _sgu(main_c, *sg_args)
            yat_c = _attention_ctx(main_c, small_c, sinks, B, L)
            xc_mid, h2c, logits_c = _merge(ysg_c, of_c.reshape(n_ctx, D), ob_c.reshape(n_ctx, D), main_c,
                                           yat_c, xc, mod, ctx_row, lw)
            x_mid = jnp.concatenate([x_mid, xc_mid], axis=0)
            h2 = jnp.concatenate([h2, h2c], axis=0)
            logits = jnp.concatenate([logits, logits_c], axis=0)
            xo = _moe(h2, logits, x_mid, mod, all_row, lw)
            xl, xc = xo[:n_lat], xo[n_lat:]
        else:
            xl = _moe(h2, logits, x_mid, mod, lat_row, lw)
    return xl.reshape(B, S, D)
```

```python
import functools
import math

import jax
import jax.numpy as jnp
from jax import lax
from jax.experimental import pallas as pl
from jax.experimental.pallas import tpu as pltpu

F32 = jnp.float32
BF16 = jnp.bfloat16
I32 = jnp.int32
U32 = jnp.uint32

EPS = 1e-6
D_MODEL = 1024
GRID_W = 64

SG_CHUNK = 128
SG_GROUPS = 8

DN_HEADS = 8
DN_DIM = 128
DN_CONV = 5
DN_CHUNK = 64
DN_HALO = 16
DN_PREP_CHUNKS = 2
DN_SCAN_CHUNKS = 4

AT_Q_HEADS = 16
AT_KV_HEADS = 2
AT_DIM = 64
AT_BLOCK = 128
ROPE_BASE = 10000.0

N_EXPERTS = 32
TOP_K = 4
D_EXPERT = 1024
SWIGLU_ALPHA = 1.702
SWIGLU_LIMIT = 7.0
N_BRANCH = 3

LANES = 128
NEG_BIG = -1e30

COL_DN_K, COL_DN_V, COL_DN_Q, COL_DN_G, COL_AT_Q, COL_SG_U, COL_SG_V, COL_GATE0 = range(8)
N_MAIN_COLS = 10 * D_MODEL
N_CTX_MAIN_COLS = 2 * D_MODEL
N_SMALL_COLS = 3 * LANES

VMEM_LIMIT = 52 * 1024 * 1024


def _cparams(sem):
    return pltpu.CompilerParams(dimension_semantics=sem, vmem_limit_bytes=VMEM_LIMIT)


def _dot(a, b):
    return jnp.dot(a, b, preferred_element_type=F32)


def _dot_nt(a, b):
    return lax.dot_general(a, b, (((1,), (1,)), ((), ())), preferred_element_type=F32)


def _dot_tn(a, b):
    return lax.dot_general(a, b, (((0,), (0,)), ((), ())), preferred_element_type=F32)


def _sigmoid(x):
    return 1.0 / (1.0 + jnp.exp(-x))


def _silu(x):
    return x * _sigmoid(x)


def _gelu_tanh(x):
    return 0.5 * x * (1.0 + jnp.tanh(math.sqrt(2.0 / math.pi) * (x + 0.044715 * (x * x * x))))


def _softplus(x):
    return jnp.maximum(x, 0.0) + jnp.log(1.0 + jnp.exp(-jnp.abs(x)))


def _rms(x, g):
    return x * lax.rsqrt(jnp.mean(x * x, axis=-1, keepdims=True) + EPS) * g


def _mod_kernel(c_ref, w_ref, b_ref, o_ref):
    s = _silu(c_ref[...])
    o_ref[...] = jnp.dot(s, w_ref[...], preferred_element_type=F32,
                         precision=lax.Precision.HIGHEST) + b_ref[...]


def _modulation(cvec, w_mod, b_mod):
    depth = w_mod.shape[0]
    rows = cvec.shape[0]
    n_col = w_mod.shape[2] // D_MODEL
    return pl.pallas_call(
        _mod_kernel,
        grid=(depth, n_col),
        in_specs=[pl.BlockSpec((rows, D_MODEL), lambda l, j: (0, 0)),
                  pl.BlockSpec((None, D_MODEL, D_MODEL), lambda l, j: (l, 0, j)),
                  pl.BlockSpec((None, 1, D_MODEL), lambda l, j: (l, 0, j))],
        out_specs=pl.BlockSpec((None, rows, D_MODEL), lambda l, j: (l, 0, j)),
        out_shape=jax.ShapeDtypeStruct((depth, rows, w_mod.shape[2]), F32),
        compiler_params=_cparams(("arbitrary", "arbitrary")),
        name="modulation",
    )(cvec, w_mod, b_mod.reshape(depth, 1, -1))


def _inproj_kernel(x_ref, mod_ref, g_ref, wm_ref, ws_ref, main_ref, small_ref, h_ref):
    @pl.when(pl.program_id(1) == 0)
    def _():
        sh = mod_ref[:, 0 * D_MODEL:1 * D_MODEL]
        sc = mod_ref[:, 1 * D_MODEL:2 * D_MODEL]
        h = (_rms(x_ref[...], g_ref[...]) * (1.0 + sc) + sh).astype(BF16)
        h_ref[...] = h
        small_ref[...] = _dot(h, ws_ref[...])

    main_ref[...] = _dot(h_ref[...], wm_ref[...]).astype(BF16)


def _inproj(x, mod, mod_row, norm_g, w_main, w_small, tm, tn=2048):
    n_tok = x.shape[0]
    n_main = w_main.shape[1]
    return pl.pallas_call(
        _inproj_kernel,
        grid=(n_tok // tm, n_main // tn),
        in_specs=[pl.BlockSpec((tm, D_MODEL), lambda i, j: (i, 0)),
                  pl.BlockSpec((None, 1, 6 * D_MODEL), lambda i, j: (mod_row(i * tm), 0, 0)),
                  pl.BlockSpec((1, D_MODEL), lambda i, j: (0, 0)),
                  pl.BlockSpec((D_MODEL, tn), lambda i, j: (0, j)),
                  pl.BlockSpec((D_MODEL, N_SMALL_COLS), lambda i, j: (0, 0))],
        out_specs=[pl.BlockSpec((tm, tn), lambda i, j: (i, j)),
                   pl.BlockSpec((tm, N_SMALL_COLS), lambda i, j: (i, 0))],
        out_shape=[jax.ShapeDtypeStruct((n_tok, n_main), BF16),
                   jax.ShapeDtypeStruct((n_tok, N_SMALL_COLS), F32)],
        scratch_shapes=[pltpu.VMEM((tm, D_MODEL), BF16)],
        compiler_params=_cparams(("arbitrary", "arbitrary")),
        name="inproj",
    )(x, mod, norm_g, w_main, w_small)


def _sgu_kernel(u_ref, v_ref, lng_ref, lnb_ref, ws_ref, bs_ref, o_ref, *, n_chunk):
    u = _gelu_tanh(u_ref[...].astype(F32))
    v = _gelu_tanh(v_ref[...].astype(F32))
    vc = v - jnp.mean(v, axis=-1, keepdims=True)
    var = jnp.mean(vc * vc, axis=-1, keepdims=True)
    vn = (vc * lax.rsqrt(var + EPS) * lng_ref[...] + lnb_ref[...]).astype(BF16)
    for n in range(n_chunk):
        rows = slice(n * SG_CHUNK, (n + 1) * SG_CHUNK)
        for g in range(SG_GROUPS):
            cols = slice(g * LANES, (g + 1) * LANES)
            mixed = _dot(ws_ref[g], vn[rows, cols]) + bs_ref[:, g:g + 1]
            o_ref[rows, cols] = (u[rows, cols] * mixed).astype(BF16)


def _sgu(main, sg_ln_g, sg_ln_b, sg_w, sg_bt, n_chunk=2):
    n_tok = main.shape[0]
    tc = n_chunk * SG_CHUNK
    return pl.pallas_call(
        functools.partial(_sgu_kernel, n_chunk=n_chunk),
        grid=(n_tok // tc,),
        in_specs=[pl.BlockSpec((tc, D_MODEL), lambda i: (i, COL_SG_U)),
                  pl.BlockSpec((tc, D_MODEL), lambda i: (i, COL_SG_V)),
                  pl.BlockSpec((1, D_MODEL), lambda i: (0, 0)),
                  pl.BlockSpec((1, D_MODEL), lambda i: (0, 0)),
                  pl.BlockSpec((SG_GROUPS, SG_CHUNK, SG_CHUNK), lambda i: (0, 0, 0)),
                  pl.BlockSpec((SG_CHUNK, SG_GROUPS), lambda i: (0, 0))],
        out_specs=pl.BlockSpec((tc, D_MODEL), lambda i: (i, 0)),
        out_shape=jax.ShapeDtypeStruct((n_tok, D_MODEL), BF16),
        compiler_params=_cparams(("arbitrary",)),
        name="sgu",
    )(main, main, sg_ln_g, sg_ln_b, sg_w, sg_bt)


def _dn_kernel(*refs, with_q, n_chunks):
    s_refs = refs[-DN_HEADS:]
    ext_refs = refs[-DN_HEADS - 3:-DN_HEADS]
    refs = refs[:-DN_HEADS - 3]
    if with_q:
        (qp_ref, qc_ref, qn_ref, kp_ref, kc_ref, kn_ref, vp_ref, vc_ref, vn_ref,
         gcol_ref, grow_ref, cw_ref, alog_r_ref, alog_c_ref, dtb_r_ref, dtb_c_ref, s0_ref,
         o_ref, sfin_ref) = refs
    else:
        (kp_ref, kc_ref, kn_ref, vp_ref, vc_ref, vn_ref,
         gcol_ref, grow_ref, cw_ref, alog_r_ref, alog_c_ref, dtb_r_ref, dtb_c_ref, s0_ref,
         sfin_ref) = refs
    d = pl.program_id(0)
    c = pl.program_id(2)
    is_fwd = d == 0
    cidx = jnp.where(is_fwd, c, n_chunks - 1 - c)
    C = DN_CHUNK

    @pl.when(c == 0)
    def _():
        for h in range(DN_HEADS):
            s_refs[h][...] = s0_ref[h]

    has_prev = (cidx > 0).astype(F32)
    has_next = (cidx < n_chunks - 1).astype(F32)

    def conv_silu(p_ref, c_ref, n_ref, part):
        ext_ref = ext_refs[part]
        ext_ref[0:DN_HALO, :] = p_ref[...].astype(F32) * has_prev
        ext_ref[DN_HALO:DN_HALO + C, :] = c_ref[...].astype(F32)
        ext_ref[DN_HALO + C:2 * DN_HALO + C, :] = n_ref[...].astype(F32) * has_next
        base = DN_HALO - DN_CONV // 2
        y = None
        for i in range(DN_CONV):
            w = cw_ref[i:i + 1, part * D_MODEL:(part + 1) * D_MODEL]
            t = ext_ref[base + i:base + i + C, :] * w
            y = t if y is None else y + t
        return _silu(y)

    k_all = conv_silu(kp_ref, kc_ref, kn_ref, 1)
    v_all = conv_silu(vp_ref, vc_ref, vn_ref, 2)
    q_all = conv_silu(qp_ref, qc_ref, qn_ref, 0) if with_q else None

    gcol = gcol_ref[...]
    ld_col = -jnp.exp(alog_r_ref[...]) * _softplus(gcol[:, 0:DN_HEADS] + dtb_r_ref[...])
    beta_col = _sigmoid(gcol[:, DN_HEADS:2 * DN_HEADS])
    ld_row = -jnp.exp(alog_c_ref[...]) * _softplus(grow_ref[0:DN_HEADS, :] + dtb_c_ref[...])

    ri = lax.broadcasted_iota(I32, (C, C), 0)
    ci = lax.broadcasted_iota(I32, (C, C), 1)
    delta = (ri - ci) * (1 - 2 * d)
    incl = delta >= 0
    strict = delta > 0
    incl_t = delta <= 0
    gam_col = jnp.dot(incl.astype(F32), ld_col, preferred_element_type=F32,
                      precision=lax.Precision.HIGHEST)
    gam_row = jnp.dot(ld_row, incl_t.astype(F32), preferred_element_type=F32,
                      precision=lax.Precision.HIGHEST)
    gam_tot = jnp.sum(ld_col, axis=0, keepdims=True)
    eye = (ri == ci).astype(F32)

    H = range(DN_HEADS)
    lanes = [slice(h * DN_DIM, (h + 1) * DN_DIM) for h in H]
    gc = [gam_col[:, h:h + 1] for h in H]
    bc = [beta_col[:, h:h + 1] for h in H]
    kh = [k_all[:, lanes[h]] for h in H]
    kh = [kh[h] * lax.rsqrt(jnp.sum(kh[h] * kh[h], axis=-1, keepdims=True) + EPS) for h in H]
    kb = [kh[h].astype(BF16) for h in H]
    decay = [jnp.exp(jnp.where(incl, gc[h] - gam_row[h:h + 1, :], NEG_BIG)) for h in H]
    kk = [_dot_nt(kb[h], kb[h]) for h in H]
    x = [-(jnp.where(strict, decay[h], 0.0) * bc[h] * kk[h]) for h in H]
    p = [eye + x[h] for h in H]
    xb = [x[h].astype(BF16) for h in H]
    x = [_dot(xb[h], xb[h]) for h in H]
    n_fac = int(math.log2(C)) - 1
    for j in range(n_fac):
        xb = [x[h].astype(BF16) for h in H]
        if j < n_fac - 1:
            r = [_dot(xb[h], jnp.concatenate([xb[h], p[h].astype(BF16)], axis=1)) for h in H]
            x = [r[h][:, :C] for h in H]
            p = [p[h] + r[h][:, C:] for h in H]
        else:
            p = [p[h] + _dot(xb[h], p[h].astype(BF16)) for h in H]
    rhs = [jnp.concatenate([kh[h] * (bc[h] * jnp.exp(gc[h])), v_all[:, lanes[h]] * bc[h]],
                           axis=1).astype(BF16) for h in H]
    sol = [_dot(p[h].astype(BF16), rhs[h]) for h in H]
    w = [sol[h][:, :DN_DIM] for h in H]
    u0 = [sol[h][:, DN_DIM:] for h in H]
    k_end = [(kh[h] * jnp.exp(gam_tot[:, h:h + 1] - gc[h])).astype(BF16) for h in H]
    s = [s_refs[h][...] for h in H]
    sb = [s[h].astype(BF16) for h in H]
    if with_q:
        qh = [q_all[:, lanes[h]] for h in H]
        qh = [qh[h] * (lax.rsqrt(jnp.sum(qh[h] * qh[h], axis=-1, keepdims=True) + EPS) * DN_DIM ** -0.5)
              for h in H]
        qk = [(_dot_nt(qh[h].astype(BF16), kb[h]) * decay[h]).astype(BF16) for h in H]
        wq = [jnp.concatenate([w[h], qh[h] * jnp.exp(gc[h])], axis=0).astype(BF16) for h in H]
        ws = [_dot(wq[h], sb[h]) for h in H]
        ub = [(u0[h] - ws[h][:C]).astype(BF16) for h in H]
        qu = [_dot(qk[h], ub[h]) for h in H]
        for h in H:
            o_ref[:, lanes[h]] = (ws[h][C:] + qu[h]).astype(o_ref.dtype)
    else:
        ws = [_dot(w[h].astype(BF16), sb[h]) for h in H]
        ub = [(u0[h] - ws[h]).astype(BF16) for h in H]
    ku = [_dot_tn(k_end[h], ub[h]) for h in H]
    for h in H:
        s_refs[h][...] = jnp.exp(gam_tot[:, h:h + 1]) * s[h] + ku[h]

    @pl.when(c == n_chunks - 1)
    def _():
        for h in range(DN_HEADS):
            sfin_ref[h] = s_refs[h][...]


def _deltanet_single_pass(main3, gate_col, gate_row, conv_w, alog, dtb, s0, with_q):
    B, T, _ = main3.shape
    C = DN_CHUNK
    n_chunks = T // C
    hpc = C // DN_HALO
    n_halo = T // DN_HALO

    def cix(d, c):
        return jnp.where(d == 0, c, n_chunks - 1 - c)

    def trio(col):
        return [pl.BlockSpec((None, DN_HALO, D_MODEL),
                             lambda d, b, c: (b, jnp.maximum(cix(d, c) * hpc - 1, 0), col)),
                pl.BlockSpec((None, C, D_MODEL), lambda d, b, c: (b, cix(d, c), col)),
                pl.BlockSpec((None, DN_HALO, D_MODEL),
                             lambda d, b, c: (b, jnp.minimum((cix(d, c) + 1) * hpc, n_halo - 1), col))]

    in_specs = (trio(COL_DN_Q) if with_q else []) + trio(COL_DN_K) + trio(COL_DN_V) + [
        pl.BlockSpec((None, None, C, 2 * DN_HEADS), lambda d, b, c: (d, b, cix(d, c), 0)),
        pl.BlockSpec((None, None, None, 2 * DN_HEADS, C), lambda d, b, c: (d, b, cix(d, c), 0, 0)),
        pl.BlockSpec((DN_CONV, 3 * D_MODEL), lambda d, b, c: (0, 0)),
        pl.BlockSpec((None, 1, DN_HEADS), lambda d, b, c: (d, 0, 0)),
        pl.BlockSpec((None, DN_HEADS, 1), lambda d, b, c: (d, 0, 0)),
        pl.BlockSpec((None, 1, DN_HEADS), lambda d, b, c: (d, 0, 0)),
        pl.BlockSpec((None, DN_HEADS, 1), lambda d, b, c: (d, 0, 0)),
        pl.BlockSpec((None, None, DN_HEADS, DN_DIM, DN_DIM), lambda d, b, c: (d, b, 0, 0, 0)),
    ]
    s_spec = pl.BlockSpec((None, None, DN_HEADS, DN_DIM, DN_DIM), lambda d, b, c: (d, b, 0, 0, 0))
    s_shape = jax.ShapeDtypeStruct((2, B, DN_HEADS, DN_DIM, DN_DIM), F32)
    if with_q:
        out_specs = [pl.BlockSpec((None, None, C, D_MODEL), lambda d, b, c: (d, b, cix(d, c), 0)), s_spec]
        out_shape = [jax.ShapeDtypeStruct((2, B, T, D_MODEL), BF16), s_shape]
    else:
        out_specs = [s_spec]
        out_shape = [s_shape]
    n_main = 3 if with_q else 2
    args = [main3] * (3 * n_main) + [
        gate_col, gate_row, conv_w,
        alog.reshape(2, 1, DN_HEADS), alog.reshape(2, DN_HEADS, 1),
        dtb.reshape(2, 1, DN_HEADS), dtb.reshape(2, DN_HEADS, 1), s0]
    out = pl.pallas_call(
        functools.partial(_dn_kernel, with_q=with_q, n_chunks=n_chunks),
        grid=(2, B, n_chunks),
        in_specs=in_specs, out_specs=out_specs, out_shape=out_shape,
        scratch_shapes=[pltpu.VMEM((C + 2 * DN_HALO, D_MODEL), F32)] * 3
        + [pltpu.VMEM((DN_DIM, DN_DIM), F32)] * DN_HEADS,
        compiler_params=_cparams(("arbitrary", "arbitrary", "arbitrary")),
        name="deltanet_q" if with_q else "deltanet_state",
    )(*args)
    return (out[0], out[1]) if with_q else (None, out[0])


def _dn_prep_kernel(*refs, with_q, n_chunks):
    if with_q:
        (qp_ref, qc_ref, qn_ref, kp_ref, kc_ref, kn_ref, vp_ref, vc_ref, vn_ref,
         gcol_ref, grow_ref, cw_ref, alog_r_ref, alog_c_ref, dtb_r_ref, dtb_c_ref,
         w_ref, u0_ref, ke_ref, gt_ref, qs_ref, qk_ref) = refs
    else:
        (kp_ref, kc_ref, kn_ref, vp_ref, vc_ref, vn_ref,
         gcol_ref, grow_ref, cw_ref, alog_r_ref, alog_c_ref, dtb_r_ref, dtb_c_ref,
         w_ref, u0_ref, ke_ref, gt_ref) = refs
    c = pl.program_id(1)
    C = DN_CHUNK
    R = DN_PREP_CHUNKS * C
    has_prev = (c > 0).astype(BF16)
    has_next = (c < n_chunks // DN_PREP_CHUNKS - 1).astype(BF16)

    pad = DN_CONV // 2
    n_sh = DN_CONV - 1
    sr = lax.broadcasted_iota(I32, (n_sh * R, R + 2 * DN_HALO), 0)
    sc = lax.broadcasted_iota(I32, (n_sh * R, R + 2 * DN_HALO), 1)
    blk = sr // R
    off = jnp.where(blk < pad, blk - pad, blk - pad + 1)
    shift_mat = (sc == DN_HALO + (sr - blk * R) + off).astype(BF16)

    def conv_silu(p_ref, c_ref, n_ref, part):
        cur = c_ref[...]
        ext = jnp.concatenate([p_ref[...] * has_prev, cur, n_ref[...] * has_next], axis=0)
        sh = _dot(shift_mat, ext)
        taps = [sh[j * R:(j + 1) * R] for j in range(pad)] + [cur.astype(F32)] + \
               [sh[j * R:(j + 1) * R] for j in range(pad, n_sh)]
        y = None
        for i in range(DN_CONV):
            t = taps[i] * cw_ref[i:i + 1, part * D_MODEL:(part + 1) * D_MODEL]
            y = t if y is None else y + t
        return _silu(y)

    k_all = conv_silu(kp_ref, kc_ref, kn_ref, 1)
    v_all = conv_silu(vp_ref, vc_ref, vn_ref, 2)
    q_all = conv_silu(qp_ref, qc_ref, qn_ref, 0) if with_q else None

    ri = lax.broadcasted_iota(I32, (C, C), 0)
    ci = lax.broadcasted_iota(I32, (C, C), 1)
    eye = (ri == ci).astype(F32)
    CC = range(DN_PREP_CHUNKS)
    H = range(DN_HEADS)
    CH = [(cc, h) for cc in CC for h in H]
    ch = {key: i for i, key in enumerate(CH)}
    rows = [slice(cc * C, (cc + 1) * C) for cc in CC]
    lanes = [slice(h * DN_DIM, (h + 1) * DN_DIM) for h in H]
    kh = [k_all[rows[cc], lanes[h]] for cc, h in CH]
    kh = [k * lax.rsqrt(jnp.sum(k * k, axis=-1, keepdims=True) + EPS) for k in kh]
    kb = [k.astype(BF16) for k in kh]
    vh = [v_all[rows[cc], lanes[h]] for cc, h in CH]
    if with_q:
        qh = [q_all[rows[cc], lanes[h]] for cc, h in CH]
        qh = [q * (lax.rsqrt(jnp.sum(q * q, axis=-1, keepdims=True) + EPS) * DN_DIM ** -0.5) for q in qh]
        gram = [_dot_nt(jnp.concatenate([kb[j], qh[j].astype(BF16)], axis=0), kb[j]) for j in range(len(CH))]
        kk = [g[:C] for g in gram]
        qk_raw = [g[C:] for g in gram]
    else:
        kk = [_dot_nt(k, k) for k in kb]

    D2 = range(2)
    DH = [(cc, d, h) for cc in CC for d in D2 for h in H]
    incl = [(ri >= ci), (ri <= ci)]
    strict = [(ri > ci), (ri < ci)]
    gam_col, gam_row, gam_tot, beta_col = {}, {}, {}, {}
    for cc in CC:
        for d in D2:
            gcol = gcol_ref[d, rows[cc], :]
            ld_col = -jnp.exp(alog_r_ref[d]) * _softplus(gcol[:, 0:DN_HEADS] + dtb_r_ref[d])
            ld_row = -jnp.exp(alog_c_ref[d]) * _softplus(grow_ref[d, cc][0:DN_HEADS, :] + dtb_c_ref[d])
            beta_col[cc, d] = _sigmoid(gcol[:, DN_HEADS:2 * DN_HEADS])
            gam_col[cc, d] = jnp.dot(incl[d].astype(F32), ld_col, preferred_element_type=F32,
                                     precision=lax.Precision.HIGHEST)
            gam_row[cc, d] = jnp.dot(ld_row, incl[1 - d].astype(F32), preferred_element_type=F32,
                                     precision=lax.Precision.HIGHEST)
            tot = jnp.sum(ld_col, axis=0, keepdims=True)
            gam_tot[cc, d] = tot
            gt_ref[d, cc] = tot
    gc = [gam_col[cc, d][:, h:h + 1] for cc, d, h in DH]
    bc = [beta_col[cc, d][:, h:h + 1] for cc, d, h in DH]
    decay = [jnp.exp(jnp.where(incl[d], gc[i] - gam_row[cc, d][h:h + 1, :], NEG_BIG))
             for i, (cc, d, h) in enumerate(DH)]
    x = [-(jnp.where(strict[d], decay[i], 0.0) * bc[i] * kk[ch[cc, h]]) for i, (cc, d, h) in enumerate(DH)]
    N = range(len(DH))
    p = [eye + x[i] for i in N]
    xb = [x[i].astype(BF16) for i in N]
    x = [_dot(xb[i], xb[i]) for i in N]
    n_fac = int(math.log2(C)) - 1
    for j in range(n_fac):
        xb = [x[i].astype(BF16) for i in N]
        if j < n_fac - 1:
            r = [_dot(xb[i], jnp.concatenate([xb[i], p[i].astype(BF16)], axis=1)) for i in N]
            x = [r[i][:, :C] for i in N]
            p = [p[i] + r[i][:, C:] for i in N]
        else:
            p = [p[i] + _dot(xb[i], p[i].astype(BF16)) for i in N]
    rhs = [jnp.concatenate([kh[ch[cc, h]] * (bc[i] * jnp.exp(gc[i])), vh[ch[cc, h]] * bc[i]],
                           axis=1).astype(BF16) for i, (cc, d, h) in enumerate(DH)]
    sol = [_dot(p[i].astype(BF16), rhs[i]) for i in N]
    for i, (cc, d, h) in enumerate(DH):
        j = ch[cc, h]
        w_ref[d, rows[cc], lanes[h]] = sol[i][:, :DN_DIM].astype(BF16)
        u0_ref[d, rows[cc], lanes[h]] = sol[i][:, DN_DIM:].astype(BF16)
        ke_ref[d, rows[cc], lanes[h]] = (kh[j] * jnp.exp(gam_tot[cc, d][:, h:h + 1] - gc[i])).astype(BF16)
        if with_q:
            qs_ref[d, rows[cc], lanes[h]] = (qh[j] * jnp.exp(gc[i])).astype(BF16)
            qk_ref[d, rows[cc], h * C:(h + 1) * C] = (qk_raw[j] * decay[i]).astype(BF16)


def _dn_scan_kernel(*refs, with_q, n_chunks):
    n_state = 2 * DN_HEADS
    s_refs = refs[-n_state:]
    refs = refs[:-n_state]
    if with_q:
        (w0, w1, u0, u1, k0, k1, g0, g1, qs0, qs1, qk0, qk1, s0_ref, o0_ref, o1_ref, sfin_ref) = refs
        qs_r, qk_r, o_r = (qs0, qs1), (qk0, qk1), (o0_ref, o1_ref)
    else:
        (w0, w1, u0, u1, k0, k1, g0, g1, s0_ref, sfin_ref) = refs
    w_r, u_r, k_r, g_r = (w0, w1), (u0, u1), (k0, k1), (g0, g1)
    c = pl.program_id(1)
    C = DN_CHUNK
    DH = [(d, h) for d in range(2) for h in range(DN_HEADS)]
    N = range(len(DH))
    lanes = [slice(h * DN_DIM, (h + 1) * DN_DIM) for h in range(DN_HEADS)]

    @pl.when(c == 0)
    def _():
        for i, (d, h) in enumerate(DH):
            s_refs[i][...] = s0_ref[d, h]

    s = [s_refs[i][...] for i in N]
    for sub in range(DN_SCAN_CHUNKS):
        cix = (sub, DN_SCAN_CHUNKS - 1 - sub)
        rows = [slice(cix[d] * C, (cix[d] + 1) * C) for d in range(2)]
        sb = [s[i].astype(BF16) for i in N]
        w = [w_r[d][rows[d], lanes[h]] for d, h in DH]
        if with_q:
            wq = [jnp.concatenate([w[i], qs_r[d][rows[d], lanes[h]]], axis=0) for i, (d, h) in enumerate(DH)]
            ws = [_dot(wq[i], sb[i]) for i in N]
            ub = [(u_r[d][rows[d], lanes[h]].astype(F32) - ws[i][:C]).astype(BF16)
                  for i, (d, h) in enumerate(DH)]
            qu = [_dot(qk_r[d][rows[d], h * C:(h + 1) * C], ub[i]) for i, (d, h) in enumerate(DH)]
            for i, (d, h) in enumerate(DH):
                o_r[d][rows[d], lanes[h]] = (ws[i][C:] + qu[i]).astype(BF16)
        else:
            ws = [_dot(w[i], sb[i]) for i in N]
            ub = [(u_r[d][rows[d], lanes[h]].astype(F32) - ws[i]).astype(BF16) for i, (d, h) in enumerate(DH)]
        ku = [_dot_tn(k_r[d][rows[d], lanes[h]], ub[i]) for i, (d, h) in enumerate(DH)]
        s = [jnp.exp(g_r[d][cix[d]][:, h:h + 1]) * s[i] + ku[i] for i, (d, h) in enumerate(DH)]
    for i in N:
        s_refs[i][...] = s[i]

    @pl.when(c == n_chunks // DN_SCAN_CHUNKS - 1)
    def _():
        for i, (d, h) in enumerate(DH):
            sfin_ref[d, h] = s_refs[i][...]


def _deltanet(main3, gate_col, gate_row, conv_w, alog, dtb, s0, with_q):
    B, T, _ = main3.shape
    C = DN_CHUNK
    n_chunks = T // C
    NC = DN_PREP_CHUNKS
    R = NC * C
    assert T % R == 0
    hpc = R // DN_HALO
    n_halo = T // DN_HALO

    def trio(col):
        return [pl.BlockSpec((None, DN_HALO, D_MODEL), lambda b, c: (b, jnp.maximum(c * hpc - 1, 0), col)),
                pl.BlockSpec((None, R, D_MODEL), lambda b, c: (b, c, col)),
                pl.BlockSpec((None, DN_HALO, D_MODEL),
                             lambda b, c: (b, jnp.minimum((c + 1) * hpc, n_halo - 1), col))]

    vec = lambda shape: pl.BlockSpec(shape, lambda b, c: (0,) * len(shape))
    in_specs = (trio(COL_DN_Q) if with_q else []) + trio(COL_DN_K) + trio(COL_DN_V) + [
        pl.BlockSpec((2, None, R, 2 * DN_HEADS), lambda b, c: (0, b, c, 0)),
        pl.BlockSpec((2, None, NC, 2 * DN_HEADS, C), lambda b, c: (0, b, c, 0, 0)),
        vec((DN_CONV, 3 * D_MODEL)),
        vec((2, 1, DN_HEADS)), vec((2, DN_HEADS, 1)), vec((2, 1, DN_HEADS)), vec((2, DN_HEADS, 1)),
    ]
    wide = lambda n: (pl.BlockSpec((2, None, R, n), lambda b, c: (0, b, c, 0)),
                      jax.ShapeDtypeStruct((2, B, T, n), BF16))
    outs = [wide(D_MODEL), wide(D_MODEL), wide(D_MODEL),
            (pl.BlockSpec((2, None, NC, 1, DN_HEADS), lambda b, c: (0, b, c, 0, 0)),
             jax.ShapeDtypeStruct((2, B, n_chunks, 1, DN_HEADS), F32))]
    if with_q:
        outs += [wide(D_MODEL), wide(DN_HEADS * C)]
    n_main = 3 if with_q else 2
    prep = pl.pallas_call(
        functools.partial(_dn_prep_kernel, with_q=with_q, n_chunks=n_chunks),
        grid=(B, n_chunks // NC),
        in_specs=in_specs, out_specs=[o[0] for o in outs], out_shape=[o[1] for o in outs],
        compiler_params=_cparams(("arbitrary", "arbitrary")),
        name="dn_prep_q" if with_q else "dn_prep",
    )(*([main3] * (3 * n_main)), gate_col, gate_row, conv_w,
      alog.reshape(2, 1, DN_HEADS), alog.reshape(2, DN_HEADS, 1),
      dtb.reshape(2, 1, DN_HEADS), dtb.reshape(2, DN_HEADS, 1))

    NS = DN_SCAN_CHUNKS
    RS = NS * C
    n_steps = n_chunks // NS
    assert n_chunks % NS == 0

    def both_dirs(arr, n):
        if n is None:
            return [pl.BlockSpec((None, None, NS, 1, DN_HEADS), lambda b, c: (0, b, c, 0, 0)),
                    pl.BlockSpec((None, None, NS, 1, DN_HEADS),
                                 lambda b, c: (1, b, n_steps - 1 - c, 0, 0))], [arr, arr]
        return [pl.BlockSpec((None, None, RS, n), lambda b, c: (0, b, c, 0)),
                pl.BlockSpec((None, None, RS, n), lambda b, c: (1, b, n_steps - 1 - c, 0))], [arr, arr]

    specs, args = [], []
    widths = [D_MODEL, D_MODEL, D_MODEL, None] + ([D_MODEL, DN_HEADS * C] if with_q else [])
    for arr, n in zip(prep, widths):
        sp, ar = both_dirs(arr, n)
        specs += sp
        args += ar
    s_spec = pl.BlockSpec((2, None, DN_HEADS, DN_DIM, DN_DIM), lambda b, c: (0, b, 0, 0, 0))
    s_shape = jax.ShapeDtypeStruct((2, B, DN_HEADS, DN_DIM, DN_DIM), F32)
    if with_q:
        out_specs = [pl.BlockSpec((None, RS, D_MODEL), lambda b, c: (b, c, 0)),
                     pl.BlockSpec((None, RS, D_MODEL), lambda b, c: (b, n_steps - 1 - c, 0)), s_spec]
        out_shape = [jax.ShapeDtypeStruct((B, T, D_MODEL), BF16)] * 2 + [s_shape]
    else:
        out_specs, out_shape = [s_spec], [s_shape]
    out = pl.pallas_call(
        functools.partial(_dn_scan_kernel, with_q=with_q, n_chunks=n_chunks),
        grid=(B, n_steps),
        in_specs=specs + [s_spec], out_specs=out_specs, out_shape=out_shape,
        scratch_shapes=[pltpu.VMEM((DN_DIM, DN_DIM), F32)] * (2 * DN_HEADS),
        compiler_params=_cparams(("arbitrary", "arbitrary")),
        name="dn_scan_q" if with_q else "dn_scan",
    )(*args, s0)
    return (out[0], out[1], out[2]) if with_q else (None, None, out[0])


def _rope_tables(S):
    half = AT_DIM // 2
    nf = half // 2
    inv_freq = ROPE_BASE ** (-jnp.arange(nf, dtype=F32) / nf)
    t = jnp.arange(S, dtype=jnp.int32)
    row = (t // GRID_W).astype(F32)
    col = (t % GRID_W).astype(F32)
    lane = jnp.arange(LANES)
    dd = lane % AT_DIM
    pos = jnp.where((dd < half)[None, :], row[:, None], col[:, None])
    ang = pos * inv_freq[lane % nf][None, :]
    first = ((lane % half) < nf)[None, :]
    sin = jnp.sin(ang)
    return jnp.cos(ang), jnp.where(first, -sin, 0.0), jnp.where(first, 0.0, sin)


def _rope_kernel(q_ref, k_ref, v_ref, cos_ref, sa_ref, sb_ref, qo_ref, ko_ref, vo_ref):
    cos, sa, sb = cos_ref[...], sa_ref[...], sb_ref[...]
    nf = AT_DIM // 4

    def rot(x):
        return x * cos + pltpu.roll(x, LANES - nf, 1) * sa + pltpu.roll(x, nf, 1) * sb

    for j in range(AT_Q_HEADS * AT_DIM // LANES):
        lanes = slice(j * LANES, (j + 1) * LANES)
        qo_ref[:, lanes] = (rot(q_ref[:, lanes].astype(F32)) * AT_DIM ** -0.5).astype(BF16)
    ko_ref[...] = rot(k_ref[...]).astype(BF16)
    vo_ref[...] = v_ref[...].astype(BF16)


def _rope(main, small, tables, S, tm=512):
    n_tok = main.shape[0]
    per_seq = S // tm
    tab_spec = pl.BlockSpec((tm, LANES), lambda i: (i % per_seq, 0))
    return pl.pallas_call(
        _rope_kernel,
        grid=(n_tok // tm,),
        in_specs=[pl.BlockSpec((tm, D_MODEL), lambda i: (i, COL_AT_Q)),
                  pl.BlockSpec((tm, LANES), lambda i: (i, 0)),
                  pl.BlockSpec((tm, LANES), lambda i: (i, 1)),
                  tab_spec, tab_spec, tab_spec],
        out_specs=[pl.BlockSpec((tm, D_MODEL), lambda i: (i, 0)),
                   pl.BlockSpec((tm, LANES), lambda i: (i, 0)),
                   pl.BlockSpec((tm, LANES), lambda i: (i, 0))],
        out_shape=[jax.ShapeDtypeStruct((n_tok, D_MODEL), BF16),
                   jax.ShapeDtypeStruct((n_tok, LANES), BF16),
                   jax.ShapeDtypeStruct((n_tok, LANES), BF16)],
        compiler_params=_cparams(("arbitrary",)),
        name="rope",
    )(main, small, small, *tables)


def _attn_kernel(*refs, local, n_blocks, q_scale):
    if local:
        (q_ref, kp_ref, kc_ref, kn_ref, vp_ref, vc_ref, vn_ref, kx_ref, vx_ref, sink_ref, o_ref) = refs
    else:
        (q_ref, kx_ref, vx_ref, sink_ref, o_ref) = refs
    P = AT_BLOCK
    G = AT_Q_HEADS // AT_KV_HEADS
    L = kx_ref.shape[0]
    kx = kx_ref[...].astype(BF16)
    vx = vx_ref[...].astype(BF16)
    if local:
        i = pl.program_id(1)
        k_all = jnp.concatenate([kp_ref[...], kc_ref[...], kn_ref[...], kx], axis=0)
        v_all = jnp.concatenate([vp_ref[...], vc_ref[...], vn_ref[...], vx], axis=0)
        qi = lax.broadcasted_iota(I32, (P, P), 0)
        kj = lax.broadcasted_iota(I32, (P, P), 1)
        b_prev = jnp.where(kj >= qi, 0.0, NEG_BIG) + jnp.where(i > 0, 0.0, NEG_BIG)
        b_next = jnp.where(kj <= qi, 0.0, NEG_BIG) + jnp.where(i < n_blocks - 1, 0.0, NEG_BIG)
        bias = jnp.concatenate([b_prev, jnp.zeros((P, P), F32), b_next, jnp.zeros((P, L), F32)], axis=1)
    else:
        k_all, v_all, bias = kx, vx, None
    n_keys = k_all.shape[0]
    lo = lax.broadcasted_iota(I32, (P, LANES), 1) < AT_DIM
    qf = q_ref[...].astype(F32) * q_scale
    pieces = []
    for qh in range(AT_Q_HEADS):
        blk = qf[:, (qh // 2) * LANES:(qh // 2 + 1) * LANES]
        want_lo = qh // G == 0
        if want_lo != (qh % 2 == 0):
            blk = pltpu.roll(blk, AT_DIM, 1)
        pieces.append(jnp.where(lo if want_lo else ~lo, blk, 0.0).astype(BF16))
    qs = jnp.concatenate(pieces, axis=0)
    s = _dot_nt(qs, k_all)
    if bias is not None:
        s = (s.reshape(AT_Q_HEADS, P, n_keys) + bias[None]).reshape(AT_Q_HEADS * P, n_keys)
    sink = sink_ref[...]
    m = jnp.maximum(jnp.max(s, axis=-1, keepdims=True), sink)
    p = jnp.exp(s - m)
    den = jnp.sum(p, axis=-1, keepdims=True) + jnp.exp(sink - m)
    o = _dot(p.astype(BF16), v_all) / den
    for j in range(AT_Q_HEADS // 2):
        a = o[(2 * j) * P:(2 * j + 1) * P]
        b = o[(2 * j + 1) * P:(2 * j + 2) * P]
        if (2 * j) // G == 0:
            out = jnp.where(lo, a, pltpu.roll(b, AT_DIM, 1))
        else:
            out = jnp.where(lo, pltpu.roll(a, AT_DIM, 1), b)
        o_ref[:, j * LANES:(j + 1) * LANES] = out.astype(BF16)


def _attention_local(q_r, k_r, v_r, small_c, sinks, B, S, L):
    P = AT_BLOCK
    nb = S // P

    def kv_trio():
        return [pl.BlockSpec((P, LANES), lambda b, i: (b * nb + jnp.maximum(i - 1, 0), 0)),
                pl.BlockSpec((P, LANES), lambda b, i: (b * nb + i, 0)),
                pl.BlockSpec((P, LANES), lambda b, i: (b * nb + jnp.minimum(i + 1, nb - 1), 0))]

    return pl.pallas_call(
        functools.partial(_attn_kernel, local=True, n_blocks=nb, q_scale=1.0),
        grid=(B, nb),
        in_specs=[pl.BlockSpec((P, D_MODEL), lambda b, i: (b * nb + i, 0))] + kv_trio() + kv_trio() + [
            pl.BlockSpec((L, LANES), lambda b, i: (b, 0)),
            pl.BlockSpec((L, LANES), lambda b, i: (b, 1)),
            pl.BlockSpec((AT_Q_HEADS * AT_BLOCK, 1), lambda b, i: (0, 0))],
        out_specs=pl.BlockSpec((P, D_MODEL), lambda b, i: (b * nb + i, 0)),
        out_shape=jax.ShapeDtypeStruct((B * S, D_MODEL), BF16),
        compiler_params=_cparams(("arbitrary", "arbitrary")),
        name="attn_local",
    )(q_r, k_r, k_r, k_r, v_r, v_r, v_r, small_c, small_c, sinks)


def _attention_ctx(main_c, small_c, sinks, B, L):
    P = AT_BLOCK
    nb = L // P
    return pl.pallas_call(
        functools.partial(_attn_kernel, local=False, n_blocks=nb, q_scale=AT_DIM ** -0.5),
        grid=(B, nb),
        in_specs=[pl.BlockSpec((P, D_MODEL), lambda b, i: (b * nb + i, COL_AT_Q)),
                  pl.BlockSpec((L, LANES), lambda b, i: (b, 0)),
                  pl.BlockSpec((L, LANES), lambda b, i: (b, 1)),
                  pl.BlockSpec((AT_Q_HEADS * AT_BLOCK, 1), lambda b, i: (0, 0))],
        out_specs=pl.BlockSpec((P, D_MODEL), lambda b, i: (b * nb + i, 0)),
        out_shape=jax.ShapeDtypeStruct((B * L, D_MODEL), BF16),
        compiler_params=_cparams(("arbitrary", "arbitrary")),
        name="attn_ctx",
    )(main_c, small_c, small_c, sinks)


def _merge_kernel(ysg_ref, of_ref, ob_ref, dng_ref, yat_ref, g0_ref, g1_ref, g2_ref, x_ref, mod_ref,
                  dn_norm_ref, post_ref, pre_ref, wsg_ref, wdn_ref, wat_ref, wout_ref, rw_ref, rb_ref,
                  xo_ref, h2_ref, lg_ref):
    o = of_ref[...].astype(F32) + ob_ref[...].astype(F32)
    dn_g = dn_norm_ref[...]
    parts = []
    for h in range(DN_HEADS):
        lanes = slice(h * DN_DIM, (h + 1) * DN_DIM)
        parts.append(_rms(o[:, lanes], dn_g) * _silu(dng_ref[:, lanes].astype(F32)))
    ydn = jnp.concatenate(parts, axis=1).astype(BF16)
    m = (_sigmoid(g0_ref[...].astype(F32)) * _dot(ysg_ref[...], wsg_ref[...])
         + _sigmoid(g1_ref[...].astype(F32)) * _dot(ydn, wdn_ref[...])
         + _sigmoid(g2_ref[...].astype(F32)) * _dot(yat_ref[...], wat_ref[...]))
    y = _dot(m.astype(BF16), wout_ref[...])
    gate1 = mod_ref[:, 2 * D_MODEL:3 * D_MODEL]
    sh2 = mod_ref[:, 3 * D_MODEL:4 * D_MODEL]
    sc2 = mod_ref[:, 4 * D_MODEL:5 * D_MODEL]
    xn = x_ref[...] + gate1 * _rms(y, post_ref[...])
    xo_ref[...] = xn
    h2 = _rms(xn, pre_ref[...]) * (1.0 + sc2) + sh2
    h2_ref[...] = h2
    lg_ref[...] = _dot(h2.astype(BF16), rw_ref[...]) + rb_ref[...]


def _merge(ysg, o_fwd, o_bwd, main, yat, x, mod, mod_row, lw, tm=512):
    n_tok = x.shape[0]
    const = lambda i: (0, 0)
    wspec = pl.BlockSpec((D_MODEL, D_MODEL), const, pipeline_mode=pl.Buffered(1))
    vspec = pl.BlockSpec((1, D_MODEL), const)
    return pl.pallas_call(
        _merge_kernel,
        grid=(n_tok // tm,),
        in_specs=[pl.BlockSpec((tm, D_MODEL), lambda i: (i, 0)),
                  pl.BlockSpec((tm, D_MODEL), lambda i: (i, 0)),
                  pl.BlockSpec((tm, D_MODEL), lambda i: (i, 0)),
                  pl.BlockSpec((tm, D_MODEL), lambda i: (i, COL_DN_G)),
                  pl.BlockSpec((tm, D_MODEL), lambda i: (i, 0)),
                  pl.BlockSpec((tm, D_MODEL), lambda i: (i, COL_GATE0)),
                  pl.BlockSpec((tm, D_MODEL), lambda i: (i, COL_GATE0 + 1)),
                  pl.BlockSpec((tm, D_MODEL), lambda i: (i, COL_GATE0 + 2)),
                  pl.BlockSpec((tm, D_MODEL), lambda i: (i, 0)),
                  pl.BlockSpec((None, 1, 6 * D_MODEL), lambda i: (mod_row(i * tm), 0, 0)),
                  pl.BlockSpec((1, DN_DIM), const), vspec, vspec,
                  wspec, wspec, wspec, wspec,
                  pl.BlockSpec((D_MODEL, LANES), const), pl.BlockSpec((1, LANES), const)],
        out_specs=[pl.BlockSpec((tm, D_MODEL), lambda i: (i, 0)),
                   pl.BlockSpec((tm, D_MODEL), lambda i: (i, 0)),
                   pl.BlockSpec((tm, LANES), lambda i: (i, 0))],
        out_shape=[jax.ShapeDtypeStruct((n_tok, D_MODEL), F32),
                   jax.ShapeDtypeStruct((n_tok, D_MODEL), F32),
                   jax.ShapeDtypeStruct((n_tok, LANES), F32)],
        compiler_params=_cparams(("arbitrary",)),
        name="merge",
    )(ysg, o_fwd, o_bwd, main, yat, main, main, main, x, mod,
      lw["dn_norm_g"], lw["norm_post_mix"], lw["norm_pre_ffn"],
      lw["w_proj_sg"], lw["w_proj_dn"], lw["w_proj_at"], lw["w_out"], lw["router_w"], lw["router_b"])


MOE_TOK = 256
MOE_PIECE = 8
MOE_BUF = MOE_TOK * TOP_K + N_EXPERTS * MOE_PIECE
MOE_META = 256
assert MOE_META > MOE_BUF // MOE_PIECE


def _route_kernel(lg_ref, gate_ref, lpos_ref, tcnt_ref):
    tm = lg_ref.shape[0]
    l = lg_ref[...]
    lane = lax.broadcasted_iota(I32, l.shape, 1).astype(F32)
    vals, onehots = [], []
    for k in range(TOP_K):
        m = jnp.max(l, axis=-1, keepdims=True)
        ik = jnp.min(jnp.where(l == m, lane, float(LANES)), axis=-1, keepdims=True)
        oh = lane == ik
        vals.append(m)
        onehots.append(oh)
        l = jnp.where(oh, -jnp.inf, l)
    es = [jnp.exp(v - vals[0]) for v in vals]
    den = es[0] + es[1] + es[2] + es[3]
    sel = jnp.zeros(l.shape, F32)
    for k in range(TOP_K):
        gate_ref[:, k:k + 1] = es[k] / den
        sel = sel + onehots[k].astype(F32)
    ri = lax.broadcasted_iota(I32, (tm, tm), 0)
    ci = lax.broadcasted_iota(I32, (tm, tm), 1)
    before = _dot((ri > ci).astype(BF16), sel.astype(BF16))
    tcnt = jnp.sum(sel, axis=0, keepdims=True)
    tcnt_ref[...] = tcnt
    n_piece = jnp.floor((tcnt + (MOE_PIECE - 1)) * (1.0 / MOE_PIECE))
    ei = lax.broadcasted_iota(I32, (LANES, LANES), 0)
    ej = lax.broadcasted_iota(I32, (LANES, LANES), 1)
    run_start = _dot(jnp.broadcast_to(n_piece, (8, LANES)).astype(BF16),
                     (ei < ej).astype(BF16))[0:1] * float(MOE_PIECE)
    pos = before + run_start
    for k in range(TOP_K):
        lpos_ref[:, k:k + 1] = jnp.sum(jnp.where(onehots[k], pos, 0.0), axis=-1,
                                       keepdims=True).astype(I32)


def _route(logits):
    n_tok = logits.shape[0]
    tm = MOE_TOK
    n_t = n_tok // tm
    small = lambda dt: jax.ShapeDtypeStruct((n_tok, TOP_K), dt)
    kspec = pl.BlockSpec((tm, TOP_K), lambda i: (i, 0))
    tspec = pl.BlockSpec((None, 1, LANES), lambda i: (i, 0, 0))
    tshape = jax.ShapeDtypeStruct((n_t, 1, LANES), F32)
    return pl.pallas_call(
        _route_kernel,
        grid=(n_t,),
        in_specs=[pl.BlockSpec((tm, LANES), lambda i: (i, 0))],
        out_specs=[kspec, kspec, tspec],
        out_shape=[small(F32), small(I32), tshape],
        compiler_params=_cparams(("arbitrary",)),
        name="route",
    )(logits)


def _run_copies(meta_ref, base, src_of, dst_of, sem, start):
    def per_piece(q, carry):
        local = pl.multiple_of(q * MOE_PIECE, MOE_PIECE)
        slot = pl.multiple_of(meta_ref[base + 1 + q], MOE_PIECE)
        cp = pltpu.make_async_copy(src_of(local, slot), dst_of(local, slot), sem)
        if start:
            cp.start()
        else:
            cp.wait()
        return carry

    lax.fori_loop(0, meta_ref[base], per_piece, 0)


def _dispatch_kernel(meta_ref, prev_ref, zmeta_ref, lpos_ref, h_ref, xs_ref, buf_ref, buf1_ref, zero_ref,
                     sem, sem1, *, tm_e):
    tm = MOE_TOK
    step = pl.program_id(0)
    bufs, sems = (buf_ref, buf1_ref), (sem, sem1)
    rows = lambda ref, r: ref.at[pl.ds(r, MOE_PIECE)]

    @pl.when(step == 0)
    def _():
        zero_ref[...] = jnp.zeros_like(zero_ref)

        def zero_tail(start):
            def per_expert(e, carry):
                z0 = pl.multiple_of(zmeta_ref[e], MOE_PIECE)

                def per_piece(p, c2):
                    cp = pltpu.make_async_copy(zero_ref, rows(xs_ref, z0 + p * MOE_PIECE), sem)
                    if start:
                        cp.start()
                    else:
                        cp.wait()
                    return c2

                return lax.fori_loop(0, zmeta_ref[N_EXPERTS + e], per_piece, carry)

            lax.fori_loop(0, N_EXPERTS, per_expert, 0)

        zero_tail(True)
        zero_tail(False)

        buf_ref[0:tm_e, :] = jnp.zeros((tm_e, D_MODEL // 2), U32)

        def zero_tiles(start):
            def per_tile(p, carry):
                t0 = pl.multiple_of((zmeta_ref[2 * N_EXPERTS] + p) * tm_e, tm_e)
                cp = pltpu.make_async_copy(buf_ref.at[pl.ds(0, tm_e)], xs_ref.at[pl.ds(t0, tm_e)], sem)
                if start:
                    cp.start()
                else:
                    cp.wait()
                return carry

            lax.fori_loop(0, zmeta_ref[2 * N_EXPERTS + 1], per_tile, 0)

        zero_tiles(True)
        zero_tiles(False)

    def group(j):
        s_iota = lax.broadcasted_iota(I32, (MOE_BUF, tm), 0)
        perm = jnp.zeros((MOE_BUF, tm), F32)
        for k in range(TOP_K):
            perm = perm + (s_iota == lpos_ref[k:k + 1, j * tm:(j + 1) * tm]).astype(F32)
        bufs[j][...] = _pack_bf16_pairs(_dot(perm.astype(BF16), h_ref[j * tm:(j + 1) * tm, :].astype(BF16)))

    def copies(mref, j, start):
        _run_copies(mref, j * MOE_META, lambda loc, slot: rows(bufs[j], loc),
                    lambda loc, slot: rows(xs_ref, slot), sems[j], start)

    group(0)
    copies(meta_ref, 0, True)

    @pl.when(step > 0)
    def _():
        copies(prev_ref, 1, False)

    group(1)
    copies(meta_ref, 1, True)
    copies(meta_ref, 0, False)

    @pl.when(step == pl.num_programs(0) - 1)
    def _():
        copies(meta_ref, 1, False)


def _dispatch(meta, zmeta, lpos_t, h2, n_slots, tm_e):
    n_tok = h2.shape[0]
    tm = 2 * MOE_TOK
    assert tm_e <= MOE_BUF and n_tok % tm == 0
    return pl.pallas_call(
        functools.partial(_dispatch_kernel, tm_e=tm_e),
        grid=(n_tok // tm,),
        in_specs=[pl.BlockSpec((2 * MOE_META,), lambda i: (i,), memory_space=pltpu.SMEM),
                  pl.BlockSpec((2 * MOE_META,), lambda i: (jnp.maximum(i - 1, 0),), memory_space=pltpu.SMEM),
                  pl.BlockSpec((MOE_META,), lambda i: (0,), memory_space=pltpu.SMEM),
                  pl.BlockSpec((TOP_K, tm), lambda i: (0, i)),
                  pl.BlockSpec((tm, D_MODEL), lambda i: (i, 0))],
        out_specs=pl.BlockSpec(memory_space=pl.ANY),
        out_shape=jax.ShapeDtypeStruct((n_slots, D_MODEL // 2), U32),
        scratch_shapes=[pltpu.VMEM((MOE_BUF, D_MODEL // 2), U32), pltpu.VMEM((MOE_BUF, D_MODEL // 2), U32),
                        pltpu.VMEM((MOE_PIECE, D_MODEL // 2), U32),
                        pltpu.SemaphoreType.DMA, pltpu.SemaphoreType.DMA],
        compiler_params=_cparams(("arbitrary",)),
        name="moe_dispatch",
    )(meta, meta, zmeta, lpos_t, h2)


def _pack_bf16_pairs(x):
    w = x.shape[1] // 2
    xb = x.astype(BF16).astype(F32)
    lo = lax.shift_right_logical(lax.bitcast_convert_type(xb[:, :w], U32), jnp.uint32(16))
    hi = lax.bitcast_convert_type(xb[:, w:], U32) & jnp.uint32(0xFFFF0000)
    return hi | lo


def _unpack_bf16_pairs(p):
    lo = lax.bitcast_convert_type(lax.shift_left(p, jnp.uint32(16)), F32)
    hi = lax.bitcast_convert_type(p & jnp.uint32(0xFFFF0000), F32)
    return jnp.concatenate([lo, hi], axis=1).astype(BF16)


def _expert_kernel(te_ref, first_ref, nu_ref, xs_ref, wgu_ref, bgu_ref, wd_ref, bd_ref, y_ref,
                   wgu_b_ref, wd_b_ref):
    del te_ref
    i = pl.program_id(0)

    @pl.when(first_ref[i] == 1)
    def _():
        wgu_b_ref[...] = wgu_ref[...].astype(BF16)
        wd_b_ref[...] = wd_ref[...].astype(BF16)

    @pl.when(i < nu_ref[0])
    def _():
        gu = _dot(_unpack_bf16_pairs(xs_ref[...]), wgu_b_ref[...]) + bgu_ref[...]
        g = jnp.minimum(gu[:, :D_EXPERT], SWIGLU_LIMIT)
        lin = jnp.clip(gu[:, D_EXPERT:], -SWIGLU_LIMIT, SWIGLU_LIMIT)
        act = g * _sigmoid(SWIGLU_ALPHA * g) * (lin + 1.0)
        y_ref[...] = _pack_bf16_pairs(_dot(act.astype(BF16), wd_b_ref[...]) + bd_ref[...])

    @pl.when(i >= nu_ref[0])
    def _():
        y_ref[...] = jnp.zeros_like(y_ref)


def _experts(tile_expert, n_used, xs, wgu, bgu, wd, bd, layer, tm):
    n_slots = xs.shape[0]
    n_tiles = n_slots // tm
    first = jnp.concatenate([jnp.ones((1,), I32),
                             (tile_expert[1:] != tile_expert[:-1]).astype(I32)])

    def row(i, te, fi, nu):
        return (jnp.minimum(i, nu[0] - 1), 0)

    grid_spec = pltpu.PrefetchScalarGridSpec(
        num_scalar_prefetch=3,
        grid=(n_tiles,),
        in_specs=[pl.BlockSpec((tm, D_MODEL // 2), row),
                  pl.BlockSpec((None, None, D_MODEL, 2 * D_EXPERT), lambda i, te, fi, nu: (layer, te[i], 0, 0)),
                  pl.BlockSpec((None, None, 1, 2 * D_EXPERT), lambda i, te, fi, nu: (layer, te[i], 0, 0)),
                  pl.BlockSpec((None, None, D_EXPERT, D_MODEL), lambda i, te, fi, nu: (layer, te[i], 0, 0)),
                  pl.BlockSpec((None, None, 1, D_MODEL), lambda i, te, fi, nu: (layer, te[i], 0, 0))],
        out_specs=pl.BlockSpec((tm, D_MODEL // 2), lambda i, te, fi, nu: (i, 0)),
        scratch_shapes=[pltpu.VMEM((D_MODEL, 2 * D_EXPERT), BF16), pltpu.VMEM((D_EXPERT, D_MODEL), BF16)],
    )
    return pl.pallas_call(
        _expert_kernel,
        grid_spec=grid_spec,
        out_shape=jax.ShapeDtypeStruct((n_slots, D_MODEL // 2), U32),
        compiler_params=_cparams(("arbitrary",)),
        name="moe_experts",
    )(tile_expert, first, n_used, xs, wgu, bgu, wd, bd)


def _combine_kernel(meta_ref, next_ref, lpos_ref, gate_ref, x_ref, mod_ref, post_ref, y_ref, xo_ref,
                    buf_ref, buf1_ref, sem, sem1):
    tm = MOE_TOK
    step = pl.program_id(0)
    bufs, sems = (buf_ref, buf1_ref), (sem, sem1)
    rows = lambda ref, r: ref.at[pl.ds(r, MOE_PIECE)]

    def copies(mref, j, base, start):
        _run_copies(mref, base, lambda loc, slot: rows(y_ref, slot),
                    lambda loc, slot: rows(bufs[j], loc), sems[j], start)

    def reduce_tile(j):
        tok = slice(j * tm, (j + 1) * tm)
        s_iota = lax.broadcasted_iota(I32, (tm, MOE_BUF), 1)
        sel = jnp.zeros((tm, MOE_BUF), F32)
        for k in range(TOP_K):
            sel = sel + jnp.where(s_iota == lpos_ref[tok, k:k + 1], gate_ref[tok, k:k + 1], 0.0)
        y = _dot(sel.astype(BF16), _unpack_bf16_pairs(bufs[j][...]))
        gate2 = mod_ref[:, 5 * D_MODEL:6 * D_MODEL]
        xo_ref[tok, :] = x_ref[tok, :] + gate2 * _rms(y, post_ref[...])

    @pl.when(step == 0)
    def _():
        buf_ref[...] = jnp.zeros_like(buf_ref)
        buf1_ref[...] = jnp.zeros_like(buf1_ref)
        copies(meta_ref, 0, 0, True)

    copies(meta_ref, 1, MOE_META, True)
    copies(meta_ref, 0, 0, False)
    reduce_tile(0)

    @pl.when(step < pl.num_programs(0) - 1)
    def _():
        copies(next_ref, 0, 0, True)

    copies(meta_ref, 1, MOE_META, False)
    reduce_tile(1)


def _combine(meta, lpos, gate, x_mid, mod, mod_row, post_g, y):
    n_tok = x_mid.shape[0]
    tm = 2 * MOE_TOK
    n_steps = n_tok // tm
    return pl.pallas_call(
        _combine_kernel,
        grid=(n_steps,),
        in_specs=[pl.BlockSpec((2 * MOE_META,), lambda i: (i,), memory_space=pltpu.SMEM),
                  pl.BlockSpec((2 * MOE_META,), lambda i: (jnp.minimum(i + 1, n_steps - 1),),
                               memory_space=pltpu.SMEM),
                  pl.BlockSpec((tm, TOP_K), lambda i: (i, 0)),
                  pl.BlockSpec((tm, TOP_K), lambda i: (i, 0)),
                  pl.BlockSpec((tm, D_MODEL), lambda i: (i, 0)),
                  pl.BlockSpec((None, 1, 6 * D_MODEL), lambda i: (mod_row(i * tm), 0, 0)),
                  pl.BlockSpec((1, D_MODEL), lambda i: (0, 0)),
                  pl.BlockSpec(memory_space=pl.ANY)],
        out_specs=pl.BlockSpec((tm, D_MODEL), lambda i: (i, 0)),
        out_shape=jax.ShapeDtypeStruct((n_tok, D_MODEL), F32),
        scratch_shapes=[pltpu.VMEM((MOE_BUF, D_MODEL // 2), U32), pltpu.VMEM((MOE_BUF, D_MODEL // 2), U32),
                        pltpu.SemaphoreType.DMA, pltpu.SemaphoreType.DMA],
        compiler_params=_cparams(("arbitrary",)),
        name="moe_combine",
    )(meta, meta, lpos, gate, x_mid, mod, post_g, y)


def _moe(h2, logits, x_mid, mod, mod_row, lw, tm_e=512):
    n_tok = h2.shape[0]
    n_t = n_tok // MOE_TOK
    gate, lpos, tcnt = _route(logits)
    tcnt = tcnt[:, 0, :N_EXPERTS].astype(I32)
    pieces = (tcnt + MOE_PIECE - 1) // MOE_PIECE
    run_end = jnp.cumsum(pieces, axis=0) * MOE_PIECE
    used = run_end[-1]
    padded = (used + tm_e - 1) // tm_e * tm_e
    pad_end = jnp.cumsum(padded)
    offs = pad_end - padded
    n_tiles = (n_tok * TOP_K + n_t * N_EXPERTS * (MOE_PIECE - 1) + tm_e - 1) // tm_e + N_EXPERTS
    tile_start = jnp.arange(n_tiles, dtype=I32) * tm_e
    tile_expert = jnp.minimum(jnp.sum(pad_end[None, :] <= tile_start[:, None], axis=1),
                              N_EXPERTS - 1).astype(I32)
    n_used = (pad_end[-1:] // tm_e).astype(I32)
    piece_end = jnp.cumsum(pieces, axis=1)
    slot_start = offs[None, :] + run_end - pieces * MOE_PIECE
    q = jnp.arange(MOE_META - 1, dtype=I32)
    owner = q[None, :, None] >= piece_end[:, None, :]
    is_owner = owner != jnp.concatenate([jnp.ones_like(owner[..., :1]), owner[..., :-1]], axis=-1)
    run_base = slot_start - (piece_end - pieces) * MOE_PIECE
    piece_slot = jnp.sum(jnp.where(is_owner, run_base[:, None, :], 0), axis=-1) + q[None, :] * MOE_PIECE
    meta = jnp.concatenate([piece_end[:, -1:], piece_slot], axis=1).reshape(-1).astype(I32)
    z0 = offs + used
    tail = jnp.stack([n_used[0], n_tiles - n_used[0]])
    zmeta = jnp.concatenate([z0, (pad_end - z0) // MOE_PIECE, tail,
                             jnp.zeros((MOE_META - 2 * N_EXPERTS - 2,), I32)]).astype(I32)
    xs = _dispatch(meta, zmeta, lpos.T, h2, n_tiles * tm_e, tm_e)
    y = _experts(tile_expert, n_used, xs, lw["exp_w_gu"], lw["exp_b_gu"], lw["exp_w_down"],
                 lw["exp_b_down"], lw["layer"], tm_e)
    return _combine(meta, lpos, gate, x_mid, mod, mod_row, lw["norm_post_ffn"], y)


def _split_w_in(w_in):
    offs, o = {}, 0
    for name, width in (("dn_k", 1024), ("dn_v", 1024), ("dn_a", 16), ("dn_b", 16), ("at_k", 128),
                        ("at_v", 128), ("dn_q", 1024), ("dn_g", 1024), ("at_q", 1024),
                        ("sg_u", 1024), ("sg_v", 1024), ("gates", 3072)):
        offs[name] = (o, o + width)
        o += width
    sl = lambda n: w_in[:, offs[n][0]:offs[n][1]]
    w_main = jnp.concatenate([sl(n) for n in ("dn_k", "dn_v", "dn_q", "dn_g", "at_q", "sg_u", "sg_v",
                                              "gates")], axis=1).astype(BF16)
    pad = jnp.zeros((w_in.shape[0], N_SMALL_COLS - 2 * LANES - 4 * DN_HEADS), w_in.dtype)
    w_small = jnp.concatenate([sl("at_k"), sl("at_v"), sl("dn_a"), sl("dn_b"), pad], axis=1).astype(BF16)
    return w_main, w_small


def _dn_gates(small, B, T):
    ab = small[:, 2 * LANES:2 * LANES + 4 * DN_HEADS].reshape(B, T, 2, 2, DN_HEADS)
    col = jnp.transpose(ab, (3, 0, 1, 2, 4)).reshape(2, B, T, 2 * DN_HEADS)
    row = jnp.transpose(col.reshape(2, B, T // DN_CHUNK, DN_CHUNK, 2 * DN_HEADS), (0, 1, 2, 4, 3))
    return col, row


def kernel(x, c, ctx, c_ctx, w_mod, b_mod, norm_pre_mix, norm_post_mix, norm_pre_ffn, norm_post_ffn, w_in, sg_ln_g, sg_ln_b, sg_w, sg_b, dn_conv_w, dn_a_log, dn_dt_bias, dn_norm_g, at_sinks, w_proj_sg, w_proj_dn, w_proj_at, w_out, router_w, router_b, exp_w_gu, exp_b_gu, exp_w_down, exp_b_down):
    B, S, D = x.shape
    L = ctx.shape[1]
    depth = w_mod.shape[0]
    assert D == D_MODEL and S % GRID_W == 0
    n_lat, n_ctx = B * S, B * L

    rows = (B + 1 + 7) // 8 * 8
    cvec = jnp.zeros((rows, D), F32).at[:B].set(c).at[B].set(c_ctx)
    mod_all = _modulation(cvec, w_mod, b_mod)
    tables = _rope_tables(S)

    lat_row = lambda t: t // S
    ctx_row = lambda t: B
    all_row = lambda t: jnp.where(t < n_lat, t // S, B)

    xl = x.reshape(n_lat, D)
    xc = ctx.reshape(n_ctx, D)
    for l in range(depth):
        need_ctx_out = l < depth - 1
        mod = mod_all[l].reshape(rows, 1, 6 * D)
        w_main, w_small = _split_w_in(w_in[l])
        lw = {
            "dn_norm_g": dn_norm_g[l].reshape(1, -1),
            "norm_post_mix": norm_post_mix[l].reshape(1, -1),
            "norm_pre_ffn": norm_pre_ffn[l].reshape(1, -1),
            "norm_post_ffn": norm_post_ffn[l].reshape(1, -1),
            "w_proj_sg": w_proj_sg[l].astype(BF16), "w_proj_dn": w_proj_dn[l].astype(BF16),
            "w_proj_at": w_proj_at[l].astype(BF16), "w_out": w_out[l].astype(BF16),
            "router_w": jnp.pad(router_w[l], ((0, 0), (0, LANES - N_EXPERTS))).astype(BF16),
            "router_b": jnp.pad(router_b[l], (0, LANES - N_EXPERTS),
                                constant_values=NEG_BIG).reshape(1, -1),
            "layer": l,
            "exp_w_gu": exp_w_gu, "exp_b_gu": exp_b_gu.reshape(depth, N_EXPERTS, 1, -1),
            "exp_w_down": exp_w_down, "exp_b_down": exp_b_down.reshape(depth, N_EXPERTS, 1, -1),
        }
        pre_g = norm_pre_mix[l].reshape(1, -1)
        main, small = _inproj(xl, mod, lat_row, pre_g, w_main, w_small, min(1024, S))
        w_main_c = w_main if need_ctx_out else w_main[:, :N_CTX_MAIN_COLS]
        main_c, small_c = _inproj(xc, mod, ctx_row, pre_g, w_main_c, w_small, min(1024, n_ctx))

        sg_args = (sg_ln_g[l].reshape(1, -1), sg_ln_b[l].reshape(1, -1), sg_w[l].astype(BF16),
                   sg_b[l].T)
        ysg = _sgu(main, *sg_args)

        gcol_c, grow_c = _dn_gates(small_c, B, L)
        gcol, grow = _dn_gates(small, B, S)
        s0 = jnp.zeros((2, B, DN_HEADS, DN_DIM, DN_DIM), F32)
        of_c, ob_c, s_ctx = _deltanet(main_c.reshape(B, L, -1), gcol_c, grow_c, dn_conv_w[l], dn_a_log[l],
                                      dn_dt_bias[l], s0, need_ctx_out)
        of_l, ob_l, _ = _deltanet(main.reshape(B, S, -1), gcol, grow, dn_conv_w[l], dn_a_log[l],
                                  dn_dt_bias[l], s_ctx, True)

        sinks = jnp.repeat(at_sinks[l], AT_BLOCK).reshape(-1, 1)
        q_r, k_r, v_r = _rope(main, small, tables, S)
        yat = _attention_local(q_r, k_r, v_r, small_c, sinks, B, S, L)

        x_mid, h2, logits = _merge(ysg, of_l.reshape(n_lat, D), ob_l.reshape(n_lat, D), main, yat, xl, mod,
                                   lat_row, lw)
        if need_ctx_out:
            ysg_c = _sgu(main_c, *sg_args)
            yat_c = _attention_ctx(main_c, small_c, sinks, B, L)
            xc_mid, h2c, logits_c = _merge(ysg_c, of_c.reshape(n_ctx, D), ob_c.reshape(n_ctx, D), main_c,
                                           yat_c, xc, mod, ctx_row, lw)
            x_mid = jnp.concatenate([x_mid, xc_mid], axis=0)
            h2 = jnp.concatenate([h2, h2c], axis=0)
            logits = jnp.concatenate([logits, logits_c], axis=0)
            xo = _moe(h2, logits, x_mid, mod, all_row, lw)
            xl, xc = xo[:n_lat], xo[n_lat:]
        else:
            xl = _moe(h2, logits, x_mid, mod, lat_row, lw)
    return xl.reshape(B, S, D)
```

```python
import functools
import math

import jax
import jax.numpy as jnp
from jax import lax
from jax.experimental import pallas as pl
from jax.experimental.pallas import tpu as pltpu

F32 = jnp.float32
BF16 = jnp.bfloat16
I32 = jnp.int32
U32 = jnp.uint32

EPS = 1e-6
D_MODEL = 1024
GRID_W = 64

SG_CHUNK = 128
SG_GROUPS = 8

DN_HEADS = 8
DN_DIM = 128
DN_CONV = 5
DN_CHUNK = 64
DN_HALO = 16
DN_PREP_CHUNKS = 2
DN_SCAN_CHUNKS = 4

AT_Q_HEADS = 16
AT_KV_HEADS = 2
AT_DIM = 64
AT_BLOCK = 128
ROPE_BASE = 10000.0

N_EXPERTS = 32
TOP_K = 4
D_EXPERT = 1024
SWIGLU_ALPHA = 1.702
SWIGLU_LIMIT = 7.0
N_BRANCH = 3

LANES = 128
NEG_BIG = -1e30

COL_DN_K, COL_DN_V, COL_DN_Q, COL_DN_G, COL_AT_Q, COL_SG_U, COL_SG_V, COL_GATE0 = range(8)
N_MAIN_COLS = 10 * D_MODEL
N_CTX_MAIN_COLS = 2 * D_MODEL
N_SMALL_COLS = 3 * LANES

VMEM_LIMIT = 52 * 1024 * 1024


def _cparams(sem):
    return pltpu.CompilerParams(dimension_semantics=sem, vmem_limit_bytes=VMEM_LIMIT)


def _dot(a, b):
    return jnp.dot(a, b, preferred_element_type=F32)


def _dot_nt(a, b):
    return lax.dot_general(a, b, (((1,), (1,)), ((), ())), preferred_element_type=F32)


def _dot_tn(a, b):
    return lax.dot_general(a, b, (((0,), (0,)), ((), ())), preferred_element_type=F32)


def _sigmoid(x):
    return 0.5 * (1.0 + jnp.tanh(0.5 * x))


def _silu(x):
    return x * _sigmoid(x)


def _gelu_tanh(x):
    return 0.5 * x * (1.0 + jnp.tanh(math.sqrt(2.0 / math.pi) * (x + 0.044715 * (x * x * x))))


def _softplus(x):
    return jnp.maximum(x, 0.0) + jnp.log(1.0 + jnp.exp(-jnp.abs(x)))


def _rms(x, g):
    return x * lax.rsqrt(jnp.mean(x * x, axis=-1, keepdims=True) + EPS) * g


def _mod_kernel(c_ref, w_ref, b_ref, o_ref):
    s = _silu(c_ref[...])
    o_ref[...] = jnp.dot(s, w_ref[...], preferred_element_type=F32,
                         precision=lax.Precision.HIGHEST) + b_ref[...]


def _modulation(cvec, w_mod, b_mod):
    depth = w_mod.shape[0]
    rows = cvec.shape[0]
    n_col = w_mod.shape[2] // D_MODEL
    return pl.pallas_call(
        _mod_kernel,
        grid=(depth, n_col),
        in_specs=[pl.BlockSpec((rows, D_MODEL), lambda l, j: (0, 0)),
                  pl.BlockSpec((None, D_MODEL, D_MODEL), lambda l, j: (l, 0, j)),
                  pl.BlockSpec((None, 1, D_MODEL), lambda l, j: (l, 0, j))],
        out_specs=pl.BlockSpec((None, rows, D_MODEL), lambda l, j: (l, 0, j)),
        out_shape=jax.ShapeDtypeStruct((depth, rows, w_mod.shape[2]), F32),
        compiler_params=_cparams(("arbitrary", "arbitrary")),
        name="modulation",
    )(cvec, w_mod, b_mod.reshape(depth, 1, -1))


def _inproj_kernel(x_ref, mod_ref, g_ref, wm_ref, ws_ref, main_ref, small_ref, h_ref):
    @pl.when(pl.program_id(1) == 0)
    def _():
        sh = mod_ref[:, 0 * D_MODEL:1 * D_MODEL]
        sc = mod_ref[:, 1 * D_MODEL:2 * D_MODEL]
        h = (_rms(x_ref[...], g_ref[...]) * (1.0 + sc) + sh).astype(BF16)
        h_ref[...] = h
        small_ref[...] = _dot(h, ws_ref[...])

    main_ref[...] = _dot(h_ref[...], wm_ref[...]).astype(BF16)


def _inproj(x, mod, mod_row, norm_g, w_main, w_small, tm, tn=2048):
    n_tok = x.shape[0]
    n_main = w_main.shape[1]
    return pl.pallas_call(
        _inproj_kernel,
        grid=(n_tok // tm, n_main // tn),
        in_specs=[pl.BlockSpec((tm, D_MODEL), lambda i, j: (i, 0)),
                  pl.BlockSpec((None, 1, 6 * D_MODEL), lambda i, j: (mod_row(i * tm), 0, 0)),
                  pl.BlockSpec((1, D_MODEL), lambda i, j: (0, 0)),
                  pl.BlockSpec((D_MODEL, tn), lambda i, j: (0, j)),
                  pl.BlockSpec((D_MODEL, N_SMALL_COLS), lambda i, j: (0, 0))],
        out_specs=[pl.BlockSpec((tm, tn), lambda i, j: (i, j)),
                   pl.BlockSpec((tm, N_SMALL_COLS), lambda i, j: (i, 0))],
        out_shape=[jax.ShapeDtypeStruct((n_tok, n_main), BF16),
                   jax.ShapeDtypeStruct((n_tok, N_SMALL_COLS), F32)],
        scratch_shapes=[pltpu.VMEM((tm, D_MODEL), BF16)],
        compiler_params=_cparams(("arbitrary", "arbitrary")),
        name="inproj",
    )(x, mod, norm_g, w_main, w_small)


def _sgu_kernel(u_ref, v_ref, lng_ref, lnb_ref, ws_ref, bs_ref, o_ref, *, n_chunk):
    u = _gelu_tanh(u_ref[...].astype(F32))
    v = _gelu_tanh(v_ref[...].astype(F32))
    vc = v - jnp.mean(v, axis=-1, keepdims=True)
    var = jnp.mean(vc * vc, axis=-1, keepdims=True)
    vn = (vc * lax.rsqrt(var + EPS) * lng_ref[...] + lnb_ref[...]).astype(BF16)
    for n in range(n_chunk):
        rows = slice(n * SG_CHUNK, (n + 1) * SG_CHUNK)
        for g in range(SG_GROUPS):
            cols = slice(g * LANES, (g + 1) * LANES)
            mixed = _dot(ws_ref[g], vn[rows, cols]) + bs_ref[:, g:g + 1]
            o_ref[rows, cols] = (u[rows, cols] * mixed).astype(BF16)


def _sgu(main, sg_ln_g, sg_ln_b, sg_w, sg_bt, n_chunk=2):
    n_tok = main.shape[0]
    tc = n_chunk * SG_CHUNK
    return pl.pallas_call(
        functools.partial(_sgu_kernel, n_chunk=n_chunk),
        grid=(n_tok // tc,),
        in_specs=[pl.BlockSpec((tc, D_MODEL), lambda i: (i, COL_SG_U)),
                  pl.BlockSpec((tc, D_MODEL), lambda i: (i, COL_SG_V)),
                  pl.BlockSpec((1, D_MODEL), lambda i: (0, 0)),
                  pl.BlockSpec((1, D_MODEL), lambda i: (0, 0)),
                  pl.BlockSpec((SG_GROUPS, SG_CHUNK, SG_CHUNK), lambda i: (0, 0, 0)),
                  pl.BlockSpec((SG_CHUNK, SG_GROUPS), lambda i: (0, 0))],
        out_specs=pl.BlockSpec((tc, D_MODEL), lambda i: (i, 0)),
        out_shape=jax.ShapeDtypeStruct((n_tok, D_MODEL), BF16),
        compiler_params=_cparams(("arbitrary",)),
        name="sgu",
    )(main, main, sg_ln_g, sg_ln_b, sg_w, sg_bt)


def _dn_prep_kernel(*refs, with_q, n_chunks):
    if with_q:
        (qp_ref, qc_ref, qn_ref, kp_ref, kc_ref, kn_ref, vp_ref, vc_ref, vn_ref,
         gcol_ref, grow_ref, cw_ref, alog_r_ref, alog_c_ref, dtb_r_ref, dtb_c_ref,
         w_ref, u0_ref, ke_ref, gt_ref, qs_ref, qk_ref) = refs
    else:
        (kp_ref, kc_ref, kn_ref, vp_ref, vc_ref, vn_ref,
         gcol_ref, grow_ref, cw_ref, alog_r_ref, alog_c_ref, dtb_r_ref, dtb_c_ref,
         w_ref, u0_ref, ke_ref, gt_ref) = refs
    c = pl.program_id(1)
    C = DN_CHUNK
    R = DN_PREP_CHUNKS * C
    has_prev = (c > 0).astype(BF16)
    has_next = (c < n_chunks // DN_PREP_CHUNKS - 1).astype(BF16)

    pad = DN_CONV // 2
    n_sh = DN_CONV - 1
    sr = lax.broadcasted_iota(I32, (n_sh * R, R + 2 * DN_HALO), 0)
    sc = lax.broadcasted_iota(I32, (n_sh * R, R + 2 * DN_HALO), 1)
    blk = sr // R
    off = jnp.where(blk < pad, blk - pad, blk - pad + 1)
    shift_mat = (sc == DN_HALO + (sr - blk * R) + off).astype(BF16)

    def conv_silu(p_ref, c_ref, n_ref, part):
        cur = c_ref[...]
        ext = jnp.concatenate([p_ref[...] * has_prev, cur, n_ref[...] * has_next], axis=0)
        sh = _dot(shift_mat, ext)
        taps = [sh[j * R:(j + 1) * R] for j in range(pad)] + [cur.astype(F32)] + \
               [sh[j * R:(j + 1) * R] for j in range(pad, n_sh)]
        y = None
        for i in range(DN_CONV):
            t = taps[i] * cw_ref[i:i + 1, part * D_MODEL:(part + 1) * D_MODEL]
            y = t if y is None else y + t
        return _silu(y)

    k_all = conv_silu(kp_ref, kc_ref, kn_ref, 1)
    v_all = conv_silu(vp_ref, vc_ref, vn_ref, 2)
    q_all = conv_silu(qp_ref, qc_ref, qn_ref, 0) if with_q else None

    ri = lax.broadcasted_iota(I32, (C, C), 0)
    ci = lax.broadcasted_iota(I32, (C, C), 1)
    eye = (ri == ci).astype(F32)
    CC = range(DN_PREP_CHUNKS)
    H = range(DN_HEADS)
    CH = [(cc, h) for cc in CC for h in H]
    ch = {key: i for i, key in enumerate(CH)}
    rows = [slice(cc * C, (cc + 1) * C) for cc in CC]
    lanes = [slice(h * DN_DIM, (h + 1) * DN_DIM) for h in H]
    kh = [k_all[rows[cc], lanes[h]] for cc, h in CH]
    kh = [k * lax.rsqrt(jnp.sum(k * k, axis=-1, keepdims=True) + EPS) for k in kh]
    kb = [k.astype(BF16) for k in kh]
    vh = [v_all[rows[cc], lanes[h]] for cc, h in CH]
    if with_q:
        qh = [q_all[rows[cc], lanes[h]] for cc, h in CH]
        qh = [q * (lax.rsqrt(jnp.sum(q * q, axis=-1, keepdims=True) + EPS) * DN_DIM ** -0.5) for q in qh]
        gram = [_dot_nt(jnp.concatenate([kb[j], qh[j].astype(BF16)], axis=0), kb[j]) for j in range(len(CH))]
        kk = [g[:C] for g in gram]
        qk_raw = [g[C:] for g in gram]
    else:
        kk = [_dot_nt(k, k) for k in kb]

    D2 = range(2)
    DH = [(cc, d, h) for cc in CC for d in D2 for h in H]
    incl = [(ri >= ci), (ri <= ci)]
    strict = [(ri > ci), (ri < ci)]
    gam_col, gam_row, gam_tot, beta_col = {}, {}, {}, {}
    for cc in CC:
        for d in D2:
            gcol = gcol_ref[d, rows[cc], :]
            ld_col = -jnp.exp(alog_r_ref[d]) * _softplus(gcol[:, 0:DN_HEADS] + dtb_r_ref[d])
            ld_row = -jnp.exp(alog_c_ref[d]) * _softplus(grow_ref[d, cc][0:DN_HEADS, :] + dtb_c_ref[d])
            beta_col[cc, d] = _sigmoid(gcol[:, DN_HEADS:2 * DN_HEADS])
            gam_col[cc, d] = jnp.dot(incl[d].astype(F32), ld_col, preferred_element_type=F32,
                                     precision=lax.Precision.HIGHEST)
            gam_row[cc, d] = jnp.dot(ld_row, incl[1 - d].astype(F32), preferred_element_type=F32,
                                     precision=lax.Precision.HIGHEST)
            tot = jnp.sum(ld_col, axis=0, keepdims=True)
            gam_tot[cc, d] = tot
            gt_ref[d, cc] = tot
    gc = [gam_col[cc, d][:, h:h + 1] for cc, d, h in DH]
    bc = [beta_col[cc, d][:, h:h + 1] for cc, d, h in DH]
    decay = [jnp.exp(jnp.where(incl[d], gc[i] - gam_row[cc, d][h:h + 1, :], NEG_BIG))
             for i, (cc, d, h) in enumerate(DH)]
    x = [-(jnp.where(strict[d], decay[i], 0.0) * bc[i] * kk[ch[cc, h]]) for i, (cc, d, h) in enumerate(DH)]
    N = range(len(DH))
    p = [eye + x[i] for i in N]
    xb = [x[i].astype(BF16) for i in N]
    x = [_dot(xb[i], xb[i]) for i in N]
    n_fac = int(math.log2(C)) - 1
    for j in range(n_fac):
        xb = [x[i].astype(BF16) for i in N]
        if j < n_fac - 1:
            r = [_dot(xb[i], jnp.concatenate([xb[i], p[i].astype(BF16)], axis=1)) for i in N]
            x = [r[i][:, :C] for i in N]
            p = [p[i] + r[i][:, C:] for i in N]
        else:
            p = [p[i] + _dot(xb[i], p[i].astype(BF16)) for i in N]
    rhs = [jnp.concatenate([kh[ch[cc, h]] * (bc[i] * jnp.exp(gc[i])), vh[ch[cc, h]] * bc[i]],
                           axis=1).astype(BF16) for i, (cc, d, h) in enumerate(DH)]
    sol = [_dot(p[i].astype(BF16), rhs[i]) for i in N]
    for i, (cc, d, h) in enumerate(DH):
        j = ch[cc, h]
        w_ref[d, rows[cc], lanes[h]] = sol[i][:, :DN_DIM].astype(BF16)
        u0_ref[d, rows[cc], lanes[h]] = sol[i][:, DN_DIM:].astype(BF16)
        ke_ref[d, rows[cc], lanes[h]] = (kh[j] * jnp.exp(gam_tot[cc, d][:, h:h + 1] - gc[i])).astype(BF16)
        if with_q:
            qs_ref[d, rows[cc], lanes[h]] = (qh[j] * jnp.exp(gc[i])).astype(BF16)
            qk_ref[d, rows[cc], h * C:(h + 1) * C] = (qk_raw[j] * decay[i]).astype(BF16)


def _dn_scan_kernel(*refs, with_q, n_chunks):
    n_state = 2 * DN_HEADS
    s_refs = refs[-n_state:]
    refs = refs[:-n_state]
    if with_q:
        (w0, w1, u0, u1, k0, k1, g0, g1, qs0, qs1, qk0, qk1, s0_ref, o0_ref, o1_ref, sfin_ref) = refs
        qs_r, qk_r, o_r = (qs0, qs1), (qk0, qk1), (o0_ref, o1_ref)
    else:
        (w0, w1, u0, u1, k0, k1, g0, g1, s0_ref, sfin_ref) = refs
    w_r, u_r, k_r, g_r = (w0, w1), (u0, u1), (k0, k1), (g0, g1)
    c = pl.program_id(1)
    C = DN_CHUNK
    DH = [(d, h) for d in range(2) for h in range(DN_HEADS)]
    N = range(len(DH))
    lanes = [slice(h * DN_DIM, (h + 1) * DN_DIM) for h in range(DN_HEADS)]

    @pl.when(c == 0)
    def _():
        for i, (d, h) in enumerate(DH):
            s_refs[i][...] = s0_ref[d, h]

    s = [s_refs[i][...] for i in N]
    for sub in range(DN_SCAN_CHUNKS):
        cix = (sub, DN_SCAN_CHUNKS - 1 - sub)
        rows = [slice(cix[d] * C, (cix[d] + 1) * C) for d in range(2)]
        sb = [s[i].astype(BF16) for i in N]
        w = [w_r[d][rows[d], lanes[h]] for d, h in DH]
        if with_q:
            wq = [jnp.concatenate([w[i], qs_r[d][rows[d], lanes[h]]], axis=0) for i, (d, h) in enumerate(DH)]
            ws = [_dot(wq[i], sb[i]) for i in N]
            ub = [(u_r[d][rows[d], lanes[h]].astype(F32) - ws[i][:C]).astype(BF16)
                  for i, (d, h) in enumerate(DH)]
            qu = [_dot(qk_r[d][rows[d], h * C:(h + 1) * C], ub[i]) for i, (d, h) in enumerate(DH)]
            for i, (d, h) in enumerate(DH):
                o_r[d][rows[d], lanes[h]] = (ws[i][C:] + qu[i]).astype(BF16)
        else:
            ws = [_dot(w[i], sb[i]) for i in N]
            ub = [(u_r[d][rows[d], lanes[h]].astype(F32) - ws[i]).astype(BF16) for i, (d, h) in enumerate(DH)]
        ku = [_dot_tn(k_r[d][rows[d], lanes[h]], ub[i]) for i, (d, h) in enumerate(DH)]
        s = [jnp.exp(g_r[d][cix[d]][:, h:h + 1]) * s[i] + ku[i] for i, (d, h) in enumerate(DH)]
    for i in N:
        s_refs[i][...] = s[i]

    @pl.when(c == n_chunks // DN_SCAN_CHUNKS - 1)
    def _():
        for i, (d, h) in enumerate(DH):
            sfin_ref[d, h] = s_refs[i][...]


def _deltanet(main3, gate_col, gate_row, conv_w, alog, dtb, s0, with_q):
    B, T, _ = main3.shape
    C = DN_CHUNK
    n_chunks = T // C
    NC = DN_PREP_CHUNKS
    R = NC * C
    assert T % R == 0
    hpc = R // DN_HALO
    n_halo = T // DN_HALO

    def trio(col):
        return [pl.BlockSpec((None, DN_HALO, D_MODEL), lambda b, c: (b, jnp.maximum(c * hpc - 1, 0), col)),
                pl.BlockSpec((None, R, D_MODEL), lambda b, c: (b, c, col)),
                pl.BlockSpec((None, DN_HALO, D_MODEL),
                             lambda b, c: (b, jnp.minimum((c + 1) * hpc, n_halo - 1), col))]

    vec = lambda shape: pl.BlockSpec(shape, lambda b, c: (0,) * len(shape))
    in_specs = (trio(COL_DN_Q) if with_q else []) + trio(COL_DN_K) + trio(COL_DN_V) + [
        pl.BlockSpec((2, None, R, 2 * DN_HEADS), lambda b, c: (0, b, c, 0)),
        pl.BlockSpec((2, None, NC, 2 * DN_HEADS, C), lambda b, c: (0, b, c, 0, 0)),
        vec((DN_CONV, 3 * D_MODEL)),
        vec((2, 1, DN_HEADS)), vec((2, DN_HEADS, 1)), vec((2, 1, DN_HEADS)), vec((2, DN_HEADS, 1)),
    ]
    wide = lambda n: (pl.BlockSpec((2, None, R, n), lambda b, c: (0, b, c, 0)),
                      jax.ShapeDtypeStruct((2, B, T, n), BF16))
    outs = [wide(D_MODEL), wide(D_MODEL), wide(D_MODEL),
            (pl.BlockSpec((2, None, NC, 1, DN_HEADS), lambda b, c: (0, b, c, 0, 0)),
             jax.ShapeDtypeStruct((2, B, n_chunks, 1, DN_HEADS), F32))]
    if with_q:
        outs += [wide(D_MODEL), wide(DN_HEADS * C)]
    n_main = 3 if with_q else 2
    prep = pl.pallas_call(
        functools.partial(_dn_prep_kernel, with_q=with_q, n_chunks=n_chunks),
        grid=(B, n_chunks // NC),
        in_specs=in_specs, out_specs=[o[0] for o in outs], out_shape=[o[1] for o in outs],
        compiler_params=_cparams(("arbitrary", "arbitrary")),
        name="dn_prep_q" if with_q else "dn_prep",
    )(*([main3] * (3 * n_main)), gate_col, gate_row, conv_w,
      alog.reshape(2, 1, DN_HEADS), alog.reshape(2, DN_HEADS, 1),
      dtb.reshape(2, 1, DN_HEADS), dtb.reshape(2, DN_HEADS, 1))

    NS = DN_SCAN_CHUNKS
    RS = NS * C
    n_steps = n_chunks // NS
    assert n_chunks % NS == 0

    def both_dirs(arr, n):
        if n is None:
            return [pl.BlockSpec((None, None, NS, 1, DN_HEADS), lambda b, c: (0, b, c, 0, 0)),
                    pl.BlockSpec((None, None, NS, 1, DN_HEADS),
                                 lambda b, c: (1, b, n_steps - 1 - c, 0, 0))], [arr, arr]
        return [pl.BlockSpec((None, None, RS, n), lambda b, c: (0, b, c, 0)),
                pl.BlockSpec((None, None, RS, n), lambda b, c: (1, b, n_steps - 1 - c, 0))], [arr, arr]

    specs, args = [], []
    widths = [D_MODEL, D_MODEL, D_MODEL, None] + ([D_MODEL, DN_HEADS * C] if with_q else [])
    for arr, n in zip(prep, widths):
        sp, ar = both_dirs(arr, n)
        specs += sp
        args += ar
    s_spec = pl.BlockSpec((2, None, DN_HEADS, DN_DIM, DN_DIM), lambda b, c: (0, b, 0, 0, 0))
    s_shape = jax.ShapeDtypeStruct((2, B, DN_HEADS, DN_DIM, DN_DIM), F32)
    if with_q:
        out_specs = [pl.BlockSpec((None, RS, D_MODEL), lambda b, c: (b, c, 0)),
                     pl.BlockSpec((None, RS, D_MODEL), lambda b, c: (b, n_steps - 1 - c, 0)), s_spec]
        out_shape = [jax.ShapeDtypeStruct((B, T, D_MODEL), BF16)] * 2 + [s_shape]
    else:
        out_specs, out_shape = [s_spec], [s_shape]
    out = pl.pallas_call(
        functools.partial(_dn_scan_kernel, with_q=with_q, n_chunks=n_chunks),
        grid=(B, n_steps),
        in_specs=specs + [s_spec], out_specs=out_specs, out_shape=out_shape,
        scratch_shapes=[pltpu.VMEM((DN_DIM, DN_DIM), F32)] * (2 * DN_HEADS),
        compiler_params=_cparams(("arbitrary", "arbitrary")),
        name="dn_scan_q" if with_q else "dn_scan",
    )(*args, s0)
    return (out[0], out[1], out[2]) if with_q else (None, None, out[0])


def _rope_tables(S):
    half = AT_DIM // 2
    nf = half // 2
    inv_freq = ROPE_BASE ** (-jnp.arange(nf, dtype=F32) / nf)
    t = jnp.arange(S, dtype=jnp.int32)
    row = (t // GRID_W).astype(F32)
    col = (t % GRID_W).astype(F32)
    lane = jnp.arange(LANES)
    dd = lane % AT_DIM
    pos = jnp.where((dd < half)[None, :], row[:, None], col[:, None])
    ang = pos * inv_freq[lane % nf][None, :]
    first = ((lane % half) < nf)[None, :]
    sin = jnp.sin(ang)
    return jnp.cos(ang), jnp.where(first, -sin, 0.0), jnp.where(first, 0.0, sin)


def _rope_kernel(q_ref, k_ref, v_ref, cos_ref, sa_ref, sb_ref, qo_ref, ko_ref, vo_ref):
    cos, sa, sb = cos_ref[...], sa_ref[...], sb_ref[...]
    nf = AT_DIM // 4

    def rot(x):
        return x * cos + pltpu.roll(x, LANES - nf, 1) * sa + pltpu.roll(x, nf, 1) * sb

    for j in range(AT_Q_HEADS * AT_DIM // LANES):
        lanes = slice(j * LANES, (j + 1) * LANES)
        qo_ref[:, lanes] = (rot(q_ref[:, lanes].astype(F32)) * AT_DIM ** -0.5).astype(BF16)
    ko_ref[...] = rot(k_ref[...]).astype(BF16)
    vo_ref[...] = v_ref[...].astype(BF16)


def _rope(main, small, tables, S, tm=512):
    n_tok = main.shape[0]
    per_seq = S // tm
    tab_spec = pl.BlockSpec((tm, LANES), lambda i: (i % per_seq, 0))
    return pl.pallas_call(
        _rope_kernel,
        grid=(n_tok // tm,),
        in_specs=[pl.BlockSpec((tm, D_MODEL), lambda i: (i, COL_AT_Q)),
                  pl.BlockSpec((tm, LANES), lambda i: (i, 0)),
                  pl.BlockSpec((tm, LANES), lambda i: (i, 1)),
                  tab_spec, tab_spec, tab_spec],
        out_specs=[pl.BlockSpec((tm, D_MODEL), lambda i: (i, 0)),
                   pl.BlockSpec((tm, LANES), lambda i: (i, 0)),
                   pl.BlockSpec((tm, LANES), lambda i: (i, 0))],
        out_shape=[jax.ShapeDtypeStruct((n_tok, D_MODEL), BF16),
                   jax.ShapeDtypeStruct((n_tok, LANES), BF16),
                   jax.ShapeDtypeStruct((n_tok, LANES), BF16)],
        compiler_params=_cparams(("arbitrary",)),
        name="rope",
    )(main, small, small, *tables)


def _attn_kernel(*refs, local, n_blocks, q_scale):
    if local:
        (q_ref, kp_ref, kc_ref, kn_ref, vp_ref, vc_ref, vn_ref, kx_ref, vx_ref, sink_ref, o_ref) = refs
    else:
        (q_ref, kx_ref, vx_ref, sink_ref, o_ref) = refs
    P = AT_BLOCK
    G = AT_Q_HEADS // AT_KV_HEADS
    L = kx_ref.shape[0]
    kx = kx_ref[...].astype(BF16)
    vx = vx_ref[...].astype(BF16)
    if local:
        i = pl.program_id(1)
        k_all = jnp.concatenate([kp_ref[...], kc_ref[...], kn_ref[...], kx], axis=0)
        v_all = jnp.concatenate([vp_ref[...], vc_ref[...], vn_ref[...], vx], axis=0)
        qi = lax.broadcasted_iota(I32, (P, P), 0)
        kj = lax.broadcasted_iota(I32, (P, P), 1)
        b_prev = jnp.where(kj >= qi, 0.0, NEG_BIG) + jnp.where(i > 0, 0.0, NEG_BIG)
        b_next = jnp.where(kj <= qi, 0.0, NEG_BIG) + jnp.where(i < n_blocks - 1, 0.0, NEG_BIG)
        bias = jnp.concatenate([b_prev, jnp.zeros((P, P), F32), b_next, jnp.zeros((P, L), F32)], axis=1)
    else:
        k_all, v_all, bias = kx, vx, None
    n_keys = k_all.shape[0]
    lo = lax.broadcasted_iota(I32, (P, LANES), 1) < AT_DIM
    qf = q_ref[...].astype(F32) * q_scale
    pieces = []
    for qh in range(AT_Q_HEADS):
        blk = qf[:, (qh // 2) * LANES:(qh // 2 + 1) * LANES]
        want_lo = qh // G == 0
        if want_lo != (qh % 2 == 0):
            blk = pltpu.roll(blk, AT_DIM, 1)
        pieces.append(jnp.where(lo if want_lo else ~lo, blk, 0.0).astype(BF16))
    qs = jnp.concatenate(pieces, axis=0)
    s = _dot_nt(qs, k_all)
    if bias is not None:
        s = (s.reshape(AT_Q_HEADS, P, n_keys) + bias[None]).reshape(AT_Q_HEADS * P, n_keys)
    sink = sink_ref[...]
    m = jnp.maximum(jnp.max(s, axis=-1, keepdims=True), sink)
    p = jnp.exp(s - m)
    den = jnp.sum(p, axis=-1, keepdims=True) + jnp.exp(sink - m)
    o = _dot(p.astype(BF16), v_all) / den
    for j in range(AT_Q_HEADS // 2):
        a = o[(2 * j) * P:(2 * j + 1) * P]
        b = o[(2 * j + 1) * P:(2 * j + 2) * P]
        if (2 * j) // G == 0:
            out = jnp.where(lo, a, pltpu.roll(b, AT_DIM, 1))
        else:
            out = jnp.where(lo, pltpu.roll(a, AT_DIM, 1), b)
        o_ref[:, j * LANES:(j + 1) * LANES] = out.astype(BF16)


def _attention_local(q_r, k_r, v_r, small_c, sinks, B, S, L):
    P = AT_BLOCK
    nb = S // P

    def kv_trio():
        return [pl.BlockSpec((P, LANES), lambda b, i: (b * nb + jnp.maximum(i - 1, 0), 0)),
                pl.BlockSpec((P, LANES), lambda b, i: (b * nb + i, 0)),
                pl.BlockSpec((P, LANES), lambda b, i: (b * nb + jnp.minimum(i + 1, nb - 1), 0))]

    return pl.pallas_call(
        functools.partial(_attn_kernel, local=True, n_blocks=nb, q_scale=1.0),
        grid=(B, nb),
        in_specs=[pl.BlockSpec((P, D_MODEL), lambda b, i: (b * nb + i, 0))] + kv_trio() + kv_trio() + [
            pl.BlockSpec((L, LANES), lambda b, i: (b, 0)),
            pl.BlockSpec((L, LANES), lambda b, i: (b, 1)),
            pl.BlockSpec((AT_Q_HEADS * AT_BLOCK, 1), lambda b, i: (0, 0))],
        out_specs=pl.BlockSpec((P, D_MODEL), lambda b, i: (b * nb + i, 0)),
        out_shape=jax.ShapeDtypeStruct((B * S, D_MODEL), BF16),
        compiler_params=_cparams(("arbitrary", "arbitrary")),
        name="attn_local",
    )(q_r, k_r, k_r, k_r, v_r, v_r, v_r, small_c, small_c, sinks)


def _attention_ctx(main_c, small_c, sinks, B, L):
    P = AT_BLOCK
    nb = L // P
    return pl.pallas_call(
        functools.partial(_attn_kernel, local=False, n_blocks=nb, q_scale=AT_DIM ** -0.5),
        grid=(B, nb),
        in_specs=[pl.BlockSpec((P, D_MODEL), lambda b, i: (b * nb + i, COL_AT_Q)),
                  pl.BlockSpec((L, LANES), lambda b, i: (b, 0)),
                  pl.BlockSpec((L, LANES), lambda b, i: (b, 1)),
                  pl.BlockSpec((AT_Q_HEADS * AT_BLOCK, 1), lambda b, i: (0, 0))],
        out_specs=pl.BlockSpec((P, D_MODEL), lambda b, i: (b * nb + i, 0)),
        out_shape=jax.ShapeDtypeStruct((B * L, D_MODEL), BF16),
        compiler_params=_cparams(("arbitrary", "arbitrary")),
        name="attn_ctx",
    )(main_c, small_c, small_c, sinks)


def _merge_kernel(ysg_ref, of_ref, ob_ref, dng_ref, yat_ref, g0_ref, g1_ref, g2_ref, x_ref, mod_ref,
                  dn_norm_ref, post_ref, pre_ref, wsg_ref, wdn_ref, wat_ref, wout_ref, rw_ref, rb_ref,
                  xo_ref, h2_ref, lg_ref):
    o = of_ref[...].astype(F32) + ob_ref[...].astype(F32)
    dn_g = dn_norm_ref[...]
    parts = []
    for h in range(DN_HEADS):
        lanes = slice(h * DN_DIM, (h + 1) * DN_DIM)
        parts.append(_rms(o[:, lanes], dn_g) * _silu(dng_ref[:, lanes].astype(F32)))
    ydn = jnp.concatenate(parts, axis=1).astype(BF16)
    m = (_sigmoid(g0_ref[...].astype(F32)) * _dot(ysg_ref[...], wsg_ref[...])
         + _sigmoid(g1_ref[...].astype(F32)) * _dot(ydn, wdn_ref[...])
         + _sigmoid(g2_ref[...].astype(F32)) * _dot(yat_ref[...], wat_ref[...]))
    y = _dot(m.astype(BF16), wout_ref[...])
    gate1 = mod_ref[:, 2 * D_MODEL:3 * D_MODEL]
    sh2 = mod_ref[:, 3 * D_MODEL:4 * D_MODEL]
    sc2 = mod_ref[:, 4 * D_MODEL:5 * D_MODEL]
    xn = x_ref[...] + gate1 * _rms(y, post_ref[...])
    xo_ref[...] = xn
    h2 = _rms(xn, pre_ref[...]) * (1.0 + sc2) + sh2
    h2_ref[...] = h2
    lg_ref[...] = _dot(h2.astype(BF16), rw_ref[...]) + rb_ref[...]


def _merge(ysg, o_fwd, o_bwd, main, yat, x, mod, mod_row, lw, tm=512):
    n_tok = x.shape[0]
    const = lambda i: (0, 0)
    wspec = pl.BlockSpec((D_MODEL, D_MODEL), const, pipeline_mode=pl.Buffered(1))
    vspec = pl.BlockSpec((1, D_MODEL), const)
    return pl.pallas_call(
        _merge_kernel,
        grid=(n_tok // tm,),
        in_specs=[pl.BlockSpec((tm, D_MODEL), lambda i: (i, 0)),
                  pl.BlockSpec((tm, D_MODEL), lambda i: (i, 0)),
                  pl.BlockSpec((tm, D_MODEL), lambda i: (i, 0)),
                  pl.BlockSpec((tm, D_MODEL), lambda i: (i, COL_DN_G)),
                  pl.BlockSpec((tm, D_MODEL), lambda i: (i, 0)),
                  pl.BlockSpec((tm, D_MODEL), lambda i: (i, COL_GATE0)),
                  pl.BlockSpec((tm, D_MODEL), lambda i: (i, COL_GATE0 + 1)),
                  pl.BlockSpec((tm, D_MODEL), lambda i: (i, COL_GATE0 + 2)),
                  pl.BlockSpec((tm, D_MODEL), lambda i: (i, 0)),
                  pl.BlockSpec((None, 1, 6 * D_MODEL), lambda i: (mod_row(i * tm), 0, 0)),
                  pl.BlockSpec((1, DN_DIM), const), vspec, vspec,
                  wspec, wspec, wspec, wspec,
                  pl.BlockSpec((D_MODEL, LANES), const), pl.BlockSpec((1, LANES), const)],
        out_specs=[pl.BlockSpec((tm, D_MODEL), lambda i: (i, 0)),
                   pl.BlockSpec((tm, D_MODEL), lambda i: (i, 0)),
                   pl.BlockSpec((tm, LANES), lambda i: (i, 0))],
        out_shape=[jax.ShapeDtypeStruct((n_tok, D_MODEL), F32),
                   jax.ShapeDtypeStruct((n_tok, D_MODEL), F32),
                   jax.ShapeDtypeStruct((n_tok, LANES), F32)],
        compiler_params=_cparams(("arbitrary",)),
        name="merge",
    )(ysg, o_fwd, o_bwd, main, yat, main, main, main, x, mod,
      lw["dn_norm_g"], lw["norm_post_mix"], lw["norm_pre_ffn"],
      lw["w_proj_sg"], lw["w_proj_dn"], lw["w_proj_at"], lw["w_out"], lw["router_w"], lw["router_b"])


MOE_TOK = 256
MOE_PIECE = 8
MOE_BUF = MOE_TOK * TOP_K + N_EXPERTS * MOE_PIECE
MOE_META = 256
assert MOE_META > MOE_BUF // MOE_PIECE


def _route_kernel(lg_ref, gate_ref, lpos_ref, tcnt_ref):
    tm = lg_ref.shape[0]
    l = lg_ref[...]
    lane = lax.broadcasted_iota(I32, l.shape, 1).astype(F32)
    vals, onehots = [], []
    for k in range(TOP_K):
        m = jnp.max(l, axis=-1, keepdims=True)
        ik = jnp.min(jnp.where(l == m, lane, float(LANES)), axis=-1, keepdims=True)
        oh = lane == ik
        vals.append(m)
        onehots.append(oh)
        l = jnp.where(oh, -jnp.inf, l)
    es = [jnp.exp(v - vals[0]) for v in vals]
    den = es[0] + es[1] + es[2] + es[3]
    sel = jnp.zeros(l.shape, F32)
    for k in range(TOP_K):
        gate_ref[:, k:k + 1] = es[k] / den
        sel = sel + onehots[k].astype(F32)
    ri = lax.broadcasted_iota(I32, (tm, tm), 0)
    ci = lax.broadcasted_iota(I32, (tm, tm), 1)
    before = _dot((ri > ci).astype(BF16), sel.astype(BF16))
    tcnt = jnp.sum(sel, axis=0, keepdims=True)
    tcnt_ref[...] = tcnt
    n_piece = jnp.floor((tcnt + (MOE_PIECE - 1)) * (1.0 / MOE_PIECE))
    ei = lax.broadcasted_iota(I32, (LANES, LANES), 0)
    ej = lax.broadcasted_iota(I32, (LANES, LANES), 1)
    run_start = _dot(jnp.broadcast_to(n_piece, (8, LANES)).astype(BF16),
                     (ei < ej).astype(BF16))[0:1] * float(MOE_PIECE)
    pos = before + run_start
    for k in range(TOP_K):
        lpos_ref[:, k:k + 1] = jnp.sum(jnp.where(onehots[k], pos, 0.0), axis=-1,
                                       keepdims=True).astype(I32)


def _route(logits):
    n_tok = logits.shape[0]
    tm = MOE_TOK
    n_t = n_tok // tm
    small = lambda dt: jax.ShapeDtypeStruct((n_tok, TOP_K), dt)
    kspec = pl.BlockSpec((tm, TOP_K), lambda i: (i, 0))
    tspec = pl.BlockSpec((None, 1, LANES), lambda i: (i, 0, 0))
    tshape = jax.ShapeDtypeStruct((n_t, 1, LANES), F32)
    return pl.pallas_call(
        _route_kernel,
        grid=(n_t,),
        in_specs=[pl.BlockSpec((tm, LANES), lambda i: (i, 0))],
        out_specs=[kspec, kspec, tspec],
        out_shape=[small(F32), small(I32), tshape],
        compiler_params=_cparams(("arbitrary",)),
        name="route",
    )(logits)


def _run_copies(meta_ref, base, src_of, dst_of, sem, start):
    def per_piece(q, carry):
        local = pl.multiple_of(q * MOE_PIECE, MOE_PIECE)
        slot = pl.multiple_of(meta_ref[base + 1 + q], MOE_PIECE)
        cp = pltpu.make_async_copy(src_of(local, slot), dst_of(local, slot), sem)
        if start:
            cp.start()
        else:
            cp.wait()
        return carry

    lax.fori_loop(0, meta_ref[base], per_piece, 0)


def _dispatch_kernel(meta_ref, prev_ref, zmeta_ref, lpos_ref, h_ref, xs_ref, buf_ref, buf1_ref, zero_ref,
                     sem, sem1, *, tm_e):
    tm = MOE_TOK
    step = pl.program_id(0)
    bufs, sems = (buf_ref, buf1_ref), (sem, sem1)
    rows = lambda ref, r: ref.at[pl.ds(r, MOE_PIECE)]

    @pl.when(step == 0)
    def _():
        zero_ref[...] = jnp.zeros_like(zero_ref)

        def zero_tail(start):
            def per_expert(e, carry):
                z0 = pl.multiple_of(zmeta_ref[e], MOE_PIECE)

                def per_piece(p, c2):
                    cp = pltpu.make_async_copy(zero_ref, rows(xs_ref, z0 + p * MOE_PIECE), sem)
                    if start:
                        cp.start()
                    else:
                        cp.wait()
                    return c2

                return lax.fori_loop(0, zmeta_ref[N_EXPERTS + e], per_piece, carry)

            lax.fori_loop(0, N_EXPERTS, per_expert, 0)

        zero_tail(True)
        zero_tail(False)

        buf_ref[0:tm_e, :] = jnp.zeros((tm_e, D_MODEL // 2), U32)

        def zero_tiles(start):
            def per_tile(p, carry):
                t0 = pl.multiple_of((zmeta_ref[2 * N_EXPERTS] + p) * tm_e, tm_e)
                cp = pltpu.make_async_copy(buf_ref.at[pl.ds(0, tm_e)], xs_ref.at[pl.ds(t0, tm_e)], sem)
                if start:
                    cp.start()
                else:
                    cp.wait()
                return carry

            lax.fori_loop(0, zmeta_ref[2 * N_EXPERTS + 1], per_tile, 0)

        zero_tiles(True)
        zero_tiles(False)

    def group(j):
        s_iota = lax.broadcasted_iota(I32, (MOE_BUF, tm), 0)
        perm = jnp.zeros((MOE_BUF, tm), F32)
        for k in range(TOP_K):
            perm = jnp.where(s_iota == lpos_ref[k:k + 1, j * tm:(j + 1) * tm], 1.0, perm)
        bufs[j][...] = _pack_bf16_pairs(_dot(perm.astype(BF16), h_ref[j * tm:(j + 1) * tm, :].astype(BF16)))

    def copies(mref, j, start):
        _run_copies(mref, j * MOE_META, lambda loc, slot: rows(bufs[j], loc),
                    lambda loc, slot: rows(xs_ref, slot), sems[j], start)

    group(0)
    copies(meta_ref, 0, True)

    @pl.when(step > 0)
    def _():
        copies(prev_ref, 1, False)

    group(1)
    copies(meta_ref, 1, True)
    copies(meta_ref, 0, False)

    @pl.when(step == pl.num_programs(0) - 1)
    def _():
        copies(meta_ref, 1, False)


def _dispatch(meta, zmeta, lpos_t, h2, n_slots, tm_e):
    n_tok = h2.shape[0]
    tm = 2 * MOE_TOK
    assert tm_e <= MOE_BUF and n_tok % tm == 0
    return pl.pallas_call(
        functools.partial(_dispatch_kernel, tm_e=tm_e),
        grid=(n_tok // tm,),
        in_specs=[pl.BlockSpec((2 * MOE_META,), lambda i: (i,), memory_space=pltpu.SMEM),
                  pl.BlockSpec((2 * MOE_META,), lambda i: (jnp.maximum(i - 1, 0),), memory_space=pltpu.SMEM),
                  pl.BlockSpec((MOE_META,), lambda i: (0,), memory_space=pltpu.SMEM),
                  pl.BlockSpec((TOP_K, tm), lambda i: (0, i)),
                  pl.BlockSpec((tm, D_MODEL), lambda i: (i, 0))],
        out_specs=pl.BlockSpec(memory_space=pl.ANY),
        out_shape=jax.ShapeDtypeStruct((n_slots, D_MODEL // 2), U32),
        scratch_shapes=[pltpu.VMEM((MOE_BUF, D_MODEL // 2), U32), pltpu.VMEM((MOE_BUF, D_MODEL // 2), U32),
                        pltpu.VMEM((MOE_PIECE, D_MODEL // 2), U32),
                        pltpu.SemaphoreType.DMA, pltpu.SemaphoreType.DMA],
        compiler_params=_cparams(("arbitrary",)),
        name="moe_dispatch",
    )(meta, meta, zmeta, lpos_t, h2)


def _pack_bf16_pairs(x):
    w = x.shape[1] // 2
    xb = x.astype(BF16).astype(F32)
    lo = lax.shift_right_logical(lax.bitcast_convert_type(xb[:, :w], U32), jnp.uint32(16))
    hi = lax.bitcast_convert_type(xb[:, w:], U32) & jnp.uint32(0xFFFF0000)
    return hi | lo


def _unpack_bf16_pairs(p):
    lo = lax.bitcast_convert_type(lax.shift_left(p, jnp.uint32(16)), F32)
    hi = lax.bitcast_convert_type(p & jnp.uint32(0xFFFF0000), F32)
    return jnp.concatenate([lo, hi], axis=1).astype(BF16)


def _expert_kernel(te_ref, first_ref, nu_ref, xs_ref, wgu_ref, bgu_ref, wd_ref, bd_ref, y_ref,
                   wgu_b_ref, wd_b_ref):
    del te_ref
    i = pl.program_id(0)

    @pl.when(first_ref[i] == 1)
    def _():
        wgu_b_ref[...] = wgu_ref[...].astype(BF16)
        wd_b_ref[...] = wd_ref[...].astype(BF16)

    @pl.when(i < nu_ref[0])
    def _():
        gu = _dot(_unpack_bf16_pairs(xs_ref[...]), wgu_b_ref[...]) + bgu_ref[...]
        g = jnp.minimum(gu[:, :D_EXPERT], SWIGLU_LIMIT)
        lin = jnp.clip(gu[:, D_EXPERT:], -SWIGLU_LIMIT, SWIGLU_LIMIT)
        act = g * _sigmoid(SWIGLU_ALPHA * g) * (lin + 1.0)
        y_ref[...] = _pack_bf16_pairs(_dot(act.astype(BF16), wd_b_ref[...]) + bd_ref[...])

    @pl.when(i >= nu_ref[0])
    def _():
        y_ref[...] = jnp.zeros_like(y_ref)


def _experts(tile_expert, n_used, xs, wgu, bgu, wd, bd, layer, tm):
    n_slots = xs.shape[0]
    n_tiles = n_slots // tm
    first = jnp.concatenate([jnp.ones((1,), I32),
                             (tile_expert[1:] != tile_expert[:-1]).astype(I32)])

    def row(i, te, fi, nu):
        return (jnp.minimum(i, nu[0] - 1), 0)

    grid_spec = pltpu.PrefetchScalarGridSpec(
        num_scalar_prefetch=3,
        grid=(n_tiles,),
        in_specs=[pl.BlockSpec((tm, D_MODEL // 2), row),
                  pl.BlockSpec((None, None, D_MODEL, 2 * D_EXPERT), lambda i, te, fi, nu: (layer, te[i], 0, 0)),
                  pl.BlockSpec((None, None, 1, 2 * D_EXPERT), lambda i, te, fi, nu: (layer, te[i], 0, 0)),
                  pl.BlockSpec((None, None, D_EXPERT, D_MODEL), lambda i, te, fi, nu: (layer, te[i], 0, 0)),
                  pl.BlockSpec((None, None, 1, D_MODEL), lambda i, te, fi, nu: (layer, te[i], 0, 0))],
        out_specs=pl.BlockSpec((tm, D_MODEL // 2), lambda i, te, fi, nu: (i, 0)),
        scratch_shapes=[pltpu.VMEM((D_MODEL, 2 * D_EXPERT), BF16), pltpu.VMEM((D_EXPERT, D_MODEL), BF16)],
    )
    return pl.pallas_call(
        _expert_kernel,
        grid_spec=grid_spec,
        out_shape=jax.ShapeDtypeStruct((n_slots, D_MODEL // 2), U32),
        compiler_params=_cparams(("arbitrary",)),
        name="moe_experts",
    )(tile_expert, first, n_used, xs, wgu, bgu, wd, bd)


def _combine_kernel(meta_ref, next_ref, lpos_ref, gate_ref, x_ref, mod_ref, post_ref, y_ref, xo_ref,
                    buf_ref, buf1_ref, sem, sem1):
    tm = MOE_TOK
    step = pl.program_id(0)
    bufs, sems = (buf_ref, buf1_ref), (sem, sem1)
    rows = lambda ref, r: ref.at[pl.ds(r, MOE_PIECE)]

    def copies(mref, j, base, start):
        _run_copies(mref, base, lambda loc, slot: rows(y_ref, slot),
                    lambda loc, slot: rows(bufs[j], loc), sems[j], start)

    def reduce_tile(j):
        tok = slice(j * tm, (j + 1) * tm)
        s_iota = lax.broadcasted_iota(I32, (tm, MOE_BUF), 1)
        sel = jnp.zeros((tm, MOE_BUF), F32)
        for k in range(TOP_K):
            sel = jnp.where(s_iota == lpos_ref[tok, k:k + 1], gate_ref[tok, k:k + 1], sel)
        y = _dot(sel.astype(BF16), _unpack_bf16_pairs(bufs[j][...]))
        gate2 = mod_ref[:, 5 * D_MODEL:6 * D_MODEL]
        xo_ref[tok, :] = x_ref[tok, :] + gate2 * _rms(y, post_ref[...])

    @pl.when(step == 0)
    def _():
        buf_ref[...] = jnp.zeros_like(buf_ref)
        buf1_ref[...] = jnp.zeros_like(buf1_ref)
        copies(meta_ref, 0, 0, True)

    copies(meta_ref, 1, MOE_META, True)
    copies(meta_ref, 0, 0, False)
    reduce_tile(0)

    @pl.when(step < pl.num_programs(0) - 1)
    def _():
        copies(next_ref, 0, 0, True)

    copies(meta_ref, 1, MOE_META, False)
    reduce_tile(1)


def _combine(meta, lpos, gate, x_mid, mod, mod_row, post_g, y):
    n_tok = x_mid.shape[0]
    tm = 2 * MOE_TOK
    n_steps = n_tok // tm
    return pl.pallas_call(
        _combine_kernel,
        grid=(n_steps,),
        in_specs=[pl.BlockSpec((2 * MOE_META,), lambda i: (i,), memory_space=pltpu.SMEM),
                  pl.BlockSpec((2 * MOE_META,), lambda i: (jnp.minimum(i + 1, n_steps - 1),),
                               memory_space=pltpu.SMEM),
                  pl.BlockSpec((tm, TOP_K), lambda i: (i, 0)),
                  pl.BlockSpec((tm, TOP_K), lambda i: (i, 0)),
                  pl.BlockSpec((tm, D_MODEL), lambda i: (i, 0)),
                  pl.BlockSpec((None, 1, 6 * D_MODEL), lambda i: (mod_row(i * tm), 0, 0)),
                  pl.BlockSpec((1, D_MODEL), lambda i: (0, 0)),
                  pl.BlockSpec(memory_space=pl.ANY)],
        out_specs=pl.BlockSpec((tm, D_MODEL), lambda i: (i, 0)),
        out_shape=jax.ShapeDtypeStruct((n_tok, D_MODEL), F32),
        scratch_shapes=[pltpu.VMEM((MOE_BUF, D_MODEL // 2), U32), pltpu.VMEM((MOE_BUF, D_MODEL // 2), U32),
                        pltpu.SemaphoreType.DMA, pltpu.SemaphoreType.DMA],
        compiler_params=_cparams(("arbitrary",)),
        name="moe_combine",
    )(meta, meta, lpos, gate, x_mid, mod, post_g, y)


def _moe(h2, logits, x_mid, mod, mod_row, lw, tm_e=512):
    n_tok = h2.shape[0]
    n_t = n_tok // MOE_TOK
    gate, lpos, tcnt = _route(logits)
    tcnt = tcnt[:, 0, :N_EXPERTS].astype(I32)
    pieces = (tcnt + MOE_PIECE - 1) // MOE_PIECE
    run_end = jnp.cumsum(pieces, axis=0) * MOE_PIECE
    used = run_end[-1]
    padded = (used + tm_e - 1) // tm_e * tm_e
    pad_end = jnp.cumsum(padded)
    offs = pad_end - padded
    n_tiles = (n_tok * TOP_K + n_t * N_EXPERTS * (MOE_PIECE - 1) + tm_e - 1) // tm_e + N_EXPERTS
    tile_start = jnp.arange(n_tiles, dtype=I32) * tm_e
    tile_expert = jnp.minimum(jnp.sum(pad_end[None, :] <= tile_start[:, None], axis=1),
                              N_EXPERTS - 1).astype(I32)
    n_used = (pad_end[-1:] // tm_e).astype(I32)
    piece_end = jnp.cumsum(pieces, axis=1)
    slot_start = offs[None, :] + run_end - pieces * MOE_PIECE
    q = jnp.arange(MOE_META - 1, dtype=I32)
    owner = q[None, :, None] >= piece_end[:, None, :]
    is_owner = owner != jnp.concatenate([jnp.ones_like(owner[..., :1]), owner[..., :-1]], axis=-1)
    run_base = slot_start - (piece_end - pieces) * MOE_PIECE
    piece_slot = jnp.sum(jnp.where(is_owner, run_base[:, None, :], 0), axis=-1) + q[None, :] * MOE_PIECE
    meta = jnp.concatenate([piece_end[:, -1:], piece_slot], axis=1).reshape(-1).astype(I32)
    z0 = offs + used
    tail = jnp.stack([n_used[0], n_tiles - n_used[0]])
    zmeta = jnp.concatenate([z0, (pad_end - z0) // MOE_PIECE, tail,
                             jnp.zeros((MOE_META - 2 * N_EXPERTS - 2,), I32)]).astype(I32)
    xs = _dispatch(meta, zmeta, lpos.T, h2, n_tiles * tm_e, tm_e)
    y = _experts(tile_expert, n_used, xs, lw["exp_w_gu"], lw["exp_b_gu"], lw["exp_w_down"],
                 lw["exp_b_down"], lw["layer"], tm_e)
    return _combine(meta, lpos, gate, x_mid, mod, mod_row, lw["norm_post_ffn"], y)


def _split_w_in(w_in):
    offs, o = {}, 0
    for name, width in (("dn_k", 1024), ("dn_v", 1024), ("dn_a", 16), ("dn_b", 16), ("at_k", 128),
                        ("at_v", 128), ("dn_q", 1024), ("dn_g", 1024), ("at_q", 1024),
                        ("sg_u", 1024), ("sg_v", 1024), ("gates", 3072)):
        offs[name] = (o, o + width)
        o += width
    sl = lambda n: w_in[:, offs[n][0]:offs[n][1]]
    w_main = jnp.concatenate([sl(n) for n in ("dn_k", "dn_v", "dn_q", "dn_g", "at_q", "sg_u", "sg_v",
                                              "gates")], axis=1).astype(BF16)
    pad = jnp.zeros((w_in.shape[0], N_SMALL_COLS - 2 * LANES - 4 * DN_HEADS), w_in.dtype)
    w_small = jnp.concatenate([sl("at_k"), sl("at_v"), sl("dn_a"), sl("dn_b"), pad], axis=1).astype(BF16)
    return w_main, w_small


def _dn_gates(small, B, T):
    ab = small[:, 2 * LANES:2 * LANES + 4 * DN_HEADS].reshape(B, T, 2, 2, DN_HEADS)
    col = jnp.transpose(ab, (3, 0, 1, 2, 4)).reshape(2, B, T, 2 * DN_HEADS)
    row = jnp.transpose(col.reshape(2, B, T // DN_CHUNK, DN_CHUNK, 2 * DN_HEADS), (0, 1, 2, 4, 3))
    return col, row


def kernel(x, c, ctx, c_ctx, w_mod, b_mod, norm_pre_mix, norm_post_mix, norm_pre_ffn, norm_post_ffn, w_in, sg_ln_g, sg_ln_b, sg_w, sg_b, dn_conv_w, dn_a_log, dn_dt_bias, dn_norm_g, at_sinks, w_proj_sg, w_proj_dn, w_proj_at, w_out, router_w, router_b, exp_w_gu, exp_b_gu, exp_w_down, exp_b_down):
    B, S, D = x.shape
    L = ctx.shape[1]
    depth = w_mod.shape[0]
    assert D == D_MODEL and S % GRID_W == 0
    n_lat, n_ctx = B * S, B * L

    rows = (B + 1 + 7) // 8 * 8
    cvec = jnp.zeros((rows, D), F32).at[:B].set(c).at[B].set(c_ctx)
    mod_all = _modulation(cvec, w_mod, b_mod)
    tables = _rope_tables(S)

    lat_row = lambda t: t // S
    ctx_row = lambda t: B
    all_row = lambda t: jnp.where(t < n_lat, t // S, B)

    xl = x.reshape(n_lat, D)
    xc = ctx.reshape(n_ctx, D)
    for l in range(depth):
        need_ctx_out = l < depth - 1
        mod = mod_all[l].reshape(rows, 1, 6 * D)
        w_main, w_small = _split_w_in(w_in[l])
        lw = {
            "dn_norm_g": dn_norm_g[l].reshape(1, -1),
            "norm_post_mix": norm_post_mix[l].reshape(1, -1),
            "norm_pre_ffn": norm_pre_ffn[l].reshape(1, -1),
            "norm_post_ffn": norm_post_ffn[l].reshape(1, -1),
            "w_proj_sg": w_proj_sg[l].astype(BF16), "w_proj_dn": w_proj_dn[l].astype(BF16),
            "w_proj_at": w_proj_at[l].astype(BF16), "w_out": w_out[l].astype(BF16),
            "router_w": jnp.pad(router_w[l], ((0, 0), (0, LANES - N_EXPERTS))).astype(BF16),
            "router_b": jnp.pad(router_b[l], (0, LANES - N_EXPERTS),
                                constant_values=NEG_BIG).reshape(1, -1),
            "layer": l,
            "exp_w_gu": exp_w_gu, "exp_b_gu": exp_b_gu.reshape(depth, N_EXPERTS, 1, -1),
            "exp_w_down": exp_w_down, "exp_b_down": exp_b_down.reshape(depth, N_EXPERTS, 1, -1),
        }
        pre_g = norm_pre_mix[l].reshape(1, -1)
        main, small = _inproj(xl, mod, lat_row, pre_g, w_main, w_small, min(1024, S))
        w_main_c = w_main if need_ctx_out else w_main[:, :N_CTX_MAIN_COLS]
        main_c, small_c = _inproj(xc, mod, ctx_row, pre_g, w_main_c, w_small, min(1024, n_ctx))

        sg_args = (sg_ln_g[l].reshape(1, -1), sg_ln_b[l].reshape(1, -1), sg_w[l].astype(BF16),
                   sg_b[l].T)
        ysg = _sgu(main, *sg_args)

        gcol_c, grow_c = _dn_gates(small_c, B, L)
        gcol, grow = _dn_gates(small, B, S)
        s0 = jnp.zeros((2, B, DN_HEADS, DN_DIM, DN_DIM), F32)
        of_c, ob_c, s_ctx = _deltanet(main_c.reshape(B, L, -1), gcol_c, grow_c, dn_conv_w[l], dn_a_log[l],
                                      dn_dt_bias[l], s0, need_ctx_out)
        of_l, ob_l, _ = _deltanet(main.reshape(B, S, -1), gcol, grow, dn_conv_w[l], dn_a_log[l],
                                  dn_dt_bias[l], s_ctx, True)

        sinks = jnp.repeat(at_sinks[l], AT_BLOCK).reshape(-1, 1)
        q_r, k_r, v_r = _rope(main, small, tables, S)
        yat = _attention_local(q_r, k_r, v_r, small_c, sinks, B, S, L)

        x_mid, h2, logits = _merge(ysg, of_l.reshape(n_lat, D), ob_l.reshape(n_lat, D), main, yat, xl, mod,
                                   lat_row, lw)
        if need_ctx_out:
            ysg_c = _sgu(main_c, *sg_args)
            yat_c = _attention_ctx(main_c, small_c, sinks, B, L)
            xc_mid, h2c, logits_c = _merge(ysg_c, of_c.reshape(n_ctx, D), ob_c.reshape(n_ctx, D), main_c,
                                           yat_c, xc, mod, ctx_row, lw)
            x_mid = jnp.concatenate([x_mid, xc_mid], axis=0)
            h2 = jnp.concatenate([h2, h2c], axis=0)
            logits = jnp.concatenate([logits, logits_c], axis=0)
            xo = _moe(h2, logits, x_mid, mod, all_row, lw)
            xl, xc = xo[:n_lat], xo[n_lat:]
        else:
            xl = _moe(h2, logits, x_mid, mod, lat_row, lw)
    return xl.reshape(B, S, D)
```

```python
import functools
import math

import jax
import jax.numpy as jnp
from jax import lax
from jax.experimental import pallas as pl
from jax.experimental.pallas import tpu as pltpu

F32 = jnp.float32
BF16 = jnp.bfloat16
I32 = jnp.int32
U32 = jnp.uint32

EPS = 1e-6
D_MODEL = 1024
GRID_W = 64

SG_CHUNK = 128
SG_GROUPS = 8

DN_HEADS = 8
DN_DIM = 128
DN_CONV = 5
DN_CHUNK = 64
DN_HALO = 16
DN_PREP_CHUNKS = 2
DN_SCAN_CHUNKS = 4

AT_Q_HEADS = 16
AT_KV_HEADS = 2
AT_DIM = 64
AT_BLOCK = 128
ROPE_BASE = 10000.0

N_EXPERTS = 32
TOP_K = 4
D_EXPERT = 1024
SWIGLU_ALPHA = 1.702
SWIGLU_LIMIT = 7.0
N_BRANCH = 3

LANES = 128
NEG_BIG = -1e30

COL_DN_K, COL_DN_V, COL_DN_Q, COL_DN_G, COL_AT_Q, COL_SG_U, COL_SG_V, COL_GATE0 = range(8)
N_MAIN_COLS = 10 * D_MODEL
N_CTX_MAIN_COLS = 2 * D_MODEL
N_SMALL_COLS = 3 * LANES

VMEM_LIMIT = 52 * 1024 * 1024


def _cparams(sem):
    return pltpu.CompilerParams(dimension_semantics=sem, vmem_limit_bytes=VMEM_LIMIT)


def _dot(a, b):
    return jnp.dot(a, b, preferred_element_type=F32)


def _dot_nt(a, b):
    return lax.dot_general(a, b, (((1,), (1,)), ((), ())), preferred_element_type=F32)


def _dot_tn(a, b):
    return lax.dot_general(a, b, (((0,), (0,)), ((), ())), preferred_element_type=F32)


def _sigmoid(x):
    return 0.5 * (1.0 + jnp.tanh(0.5 * x))


def _silu(x):
    return x * _sigmoid(x)


def _gelu_tanh(x):
    return 0.5 * x * (1.0 + jnp.tanh(math.sqrt(2.0 / math.pi) * (x + 0.044715 * (x * x * x))))


def _softplus(x):
    return jnp.maximum(x, 0.0) + jnp.log(1.0 + jnp.exp(-jnp.abs(x)))


def _rms(x, g):
    return x * lax.rsqrt(jnp.mean(x * x, axis=-1, keepdims=True) + EPS) * g


def _mod_kernel(c_ref, w_ref, b_ref, o_ref):
    s = _silu(c_ref[...])
    o_ref[...] = jnp.dot(s, w_ref[...], preferred_element_type=F32,
                         precision=lax.Precision.HIGHEST) + b_ref[...]


def _modulation(cvec, w_mod, b_mod):
    depth = w_mod.shape[0]
    rows = cvec.shape[0]
    n_col = w_mod.shape[2] // D_MODEL
    return pl.pallas_call(
        _mod_kernel,
        grid=(depth, n_col),
        in_specs=[pl.BlockSpec((rows, D_MODEL), lambda l, j: (0, 0)),
                  pl.BlockSpec((None, D_MODEL, D_MODEL), lambda l, j: (l, 0, j)),
                  pl.BlockSpec((None, 1, D_MODEL), lambda l, j: (l, 0, j))],
        out_specs=pl.BlockSpec((None, rows, D_MODEL), lambda l, j: (l, 0, j)),
        out_shape=jax.ShapeDtypeStruct((depth, rows, w_mod.shape[2]), F32),
        compiler_params=_cparams(("arbitrary", "arbitrary")),
        name="modulation",
    )(cvec, w_mod, b_mod.reshape(depth, 1, -1))


def _inproj_kernel(x_ref, mod_ref, g_ref, wm_ref, ws_ref, main_ref, small_ref, h_ref):
    @pl.when(pl.program_id(1) == 0)
    def _():
        sh = mod_ref[:, 0 * D_MODEL:1 * D_MODEL]
        sc = mod_ref[:, 1 * D_MODEL:2 * D_MODEL]
        h = (_rms(x_ref[...], g_ref[...]) * (1.0 + sc) + sh).astype(BF16)
        h_ref[...] = h
        small_ref[...] = _dot(h, ws_ref[...])

    main_ref[...] = _dot(h_ref[...], wm_ref[...]).astype(BF16)


def _inproj(x, mod, mod_row, norm_g, w_main, w_small, tm, tn=2048):
    n_tok = x.shape[0]
    n_main = w_main.shape[1]
    return pl.pallas_call(
        _inproj_kernel,
        grid=(n_tok // tm, n_main // tn),
        in_specs=[pl.BlockSpec((tm, D_MODEL), lambda i, j: (i, 0)),
                  pl.BlockSpec((None, 1, 6 * D_MODEL), lambda i, j: (mod_row(i * tm), 0, 0)),
                  pl.BlockSpec((1, D_MODEL), lambda i, j: (0, 0)),
                  pl.BlockSpec((D_MODEL, tn), lambda i, j: (0, j)),
                  pl.BlockSpec((D_MODEL, N_SMALL_COLS), lambda i, j: (0, 0))],
        out_specs=[pl.BlockSpec((tm, tn), lambda i, j: (i, j)),
                   pl.BlockSpec((tm, N_SMALL_COLS), lambda i, j: (i, 0))],
        out_shape=[jax.ShapeDtypeStruct((n_tok, n_main), BF16),
                   jax.ShapeDtypeStruct((n_tok, N_SMALL_COLS), F32)],
        scratch_shapes=[pltpu.VMEM((tm, D_MODEL), BF16)],
        compiler_params=_cparams(("arbitrary", "arbitrary")),
        name="inproj",
    )(x, mod, norm_g, w_main, w_small)


def _sgu_kernel(u_ref, v_ref, lng_ref, lnb_ref, ws_ref, bs_ref, o_ref, *, n_chunk):
    u = _gelu_tanh(u_ref[...].astype(F32))
    v = _gelu_tanh(v_ref[...].astype(F32))
    vc = v - jnp.mean(v, axis=-1, keepdims=True)
    var = jnp.mean(vc * vc, axis=-1, keepdims=True)
    vn = (vc * lax.rsqrt(var + EPS) * lng_ref[...] + lnb_ref[...]).astype(BF16)
    for n in range(n_chunk):
        rows = slice(n * SG_CHUNK, (n + 1) * SG_CHUNK)
        for g in range(SG_GROUPS):
            cols = slice(g * LANES, (g + 1) * LANES)
            mixed = _dot(ws_ref[g], vn[rows, cols]) + bs_ref[:, g:g + 1]
            o_ref[rows, cols] = (u[rows, cols] * mixed).astype(BF16)


def _sgu(main, sg_ln_g, sg_ln_b, sg_w, sg_bt, n_chunk=2):
    n_tok = main.shape[0]
    tc = n_chunk * SG_CHUNK
    return pl.pallas_call(
        functools.partial(_sgu_kernel, n_chunk=n_chunk),
        grid=(n_tok // tc,),
        in_specs=[pl.BlockSpec((tc, D_MODEL), lambda i: (i, COL_SG_U)),
                  pl.BlockSpec((tc, D_MODEL), lambda i: (i, COL_SG_V)),
                  pl.BlockSpec((1, D_MODEL), lambda i: (0, 0)),
                  pl.BlockSpec((1, D_MODEL), lambda i: (0, 0)),
                  pl.BlockSpec((SG_GROUPS, SG_CHUNK, SG_CHUNK), lambda i: (0, 0, 0)),
                  pl.BlockSpec((SG_CHUNK, SG_GROUPS), lambda i: (0, 0))],
        out_specs=pl.BlockSpec((tc, D_MODEL), lambda i: (i, 0)),
        out_shape=jax.ShapeDtypeStruct((n_tok, D_MODEL), BF16),
        compiler_params=_cparams(("arbitrary",)),
        name="sgu",
    )(main, main, sg_ln_g, sg_ln_b, sg_w, sg_bt)


def _dn_prep_kernel(*refs, with_q, n_chunks):
    if with_q:
        (qp_ref, qc_ref, qn_ref, kp_ref, kc_ref, kn_ref, vp_ref, vc_ref, vn_ref,
         gcol_ref, grow_ref, cw_ref, alog_r_ref, alog_c_ref, dtb_r_ref, dtb_c_ref,
         w_ref, u0_ref, ke_ref, gt_ref, qs_ref, qk_ref) = refs
    else:
        (kp_ref, kc_ref, kn_ref, vp_ref, vc_ref, vn_ref,
         gcol_ref, grow_ref, cw_ref, alog_r_ref, alog_c_ref, dtb_r_ref, dtb_c_ref,
         w_ref, u0_ref, ke_ref, gt_ref) = refs
    c = pl.program_id(1)
    C = DN_CHUNK
    R = DN_PREP_CHUNKS * C
    has_prev = (c > 0).astype(BF16)
    has_next = (c < n_chunks // DN_PREP_CHUNKS - 1).astype(BF16)

    pad = DN_CONV // 2
    n_sh = DN_CONV - 1
    sr = lax.broadcasted_iota(I32, (n_sh * R, R + 2 * DN_HALO), 0)
    sc = lax.broadcasted_iota(I32, (n_sh * R, R + 2 * DN_HALO), 1)
    blk = sr // R
    off = jnp.where(blk < pad, blk - pad, blk - pad + 1)
    shift_mat = (sc == DN_HALO + (sr - blk * R) + off).astype(BF16)

    def conv_silu(p_ref, c_ref, n_ref, part):
        cur = c_ref[...]
        ext = jnp.concatenate([p_ref[...] * has_prev, cur, n_ref[...] * has_next], axis=0)
        sh = _dot(shift_mat, ext)
        taps = [sh[j * R:(j + 1) * R] for j in range(pad)] + [cur.astype(F32)] + \
               [sh[j * R:(j + 1) * R] for j in range(pad, n_sh)]
        y = None
        for i in range(DN_CONV):
            t = taps[i] * cw_ref[i:i + 1, part * D_MODEL:(part + 1) * D_MODEL]
            y = t if y is None else y + t
        return _silu(y)

    k_all = conv_silu(kp_ref, kc_ref, kn_ref, 1)
    v_all = conv_silu(vp_ref, vc_ref, vn_ref, 2)
    q_all = conv_silu(qp_ref, qc_ref, qn_ref, 0) if with_q else None

    ri = lax.broadcasted_iota(I32, (C, C), 0)
    ci = lax.broadcasted_iota(I32, (C, C), 1)
    eye = (ri == ci).astype(F32)
    CC = range(DN_PREP_CHUNKS)
    H = range(DN_HEADS)
    CH = [(cc, h) for cc in CC for h in H]
    ch = {key: i for i, key in enumerate(CH)}
    rows = [slice(cc * C, (cc + 1) * C) for cc in CC]
    lanes = [slice(h * DN_DIM, (h + 1) * DN_DIM) for h in H]
    kh = [k_all[rows[cc], lanes[h]] for cc, h in CH]
    kh = [k * lax.rsqrt(jnp.sum(k * k, axis=-1, keepdims=True) + EPS) for k in kh]
    kb = [k.astype(BF16) for k in kh]
    vh = [v_all[rows[cc], lanes[h]] for cc, h in CH]
    if with_q:
        qh = [q_all[rows[cc], lanes[h]] for cc, h in CH]
        qh = [q * (lax.rsqrt(jnp.sum(q * q, axis=-1, keepdims=True) + EPS) * DN_DIM ** -0.5) for q in qh]
        gram = [_dot_nt(jnp.concatenate([kb[j], qh[j].astype(BF16)], axis=0), kb[j]) for j in range(len(CH))]
        kk = [g[:C] for g in gram]
        qk_raw = [g[C:] for g in gram]
    else:
        kk = [_dot_nt(k, k) for k in kb]

    D2 = range(2)
    DH = [(cc, d, h) for cc in CC for d in D2 for h in H]
    incl = [(ri >= ci), (ri <= ci)]
    strict = [(ri > ci), (ri < ci)]
    gam_col, gam_row, gam_tot, beta_col = {}, {}, {}, {}
    for cc in CC:
        for d in D2:
            gcol = gcol_ref[d, rows[cc], :]
            ld_col = -jnp.exp(alog_r_ref[d]) * _softplus(gcol[:, 0:DN_HEADS] + dtb_r_ref[d])
            ld_row = -jnp.exp(alog_c_ref[d]) * _softplus(grow_ref[d, cc][0:DN_HEADS, :] + dtb_c_ref[d])
            beta_col[cc, d] = _sigmoid(gcol[:, DN_HEADS:2 * DN_HEADS])
            gam_col[cc, d] = jnp.dot(incl[d].astype(F32), ld_col, preferred_element_type=F32,
                                     precision=lax.Precision.HIGHEST)
            gam_row[cc, d] = jnp.dot(ld_row, incl[1 - d].astype(F32), preferred_element_type=F32,
                                     precision=lax.Precision.HIGHEST)
            tot = jnp.sum(ld_col, axis=0, keepdims=True)
            gam_tot[cc, d] = tot
            gt_ref[d, cc] = tot
    gc = [gam_col[cc, d][:, h:h + 1] for cc, d, h in DH]
    bc = [beta_col[cc, d][:, h:h + 1] for cc, d, h in DH]
    decay = [jnp.exp(jnp.where(incl[d], gc[i] - gam_row[cc, d][h:h + 1, :], NEG_BIG))
             for i, (cc, d, h) in enumerate(DH)]
    x = [-(jnp.where(strict[d], decay[i], 0.0) * bc[i] * kk[ch[cc, h]]) for i, (cc, d, h) in enumerate(DH)]
    N = range(len(DH))
    p = [eye + x[i] for i in N]
    xb = [x[i].astype(BF16) for i in N]
    x = [_dot(xb[i], xb[i]) for i in N]
    n_fac = int(math.log2(C)) - 1
    for j in range(n_fac):
        xb = [x[i].astype(BF16) for i in N]
        if j < n_fac - 1:
            r = [_dot(xb[i], jnp.concatenate([xb[i], p[i].astype(BF16)], axis=1)) for i in N]
            x = [r[i][:, :C] for i in N]
            p = [p[i] + r[i][:, C:] for i in N]
        else:
            p = [p[i] + _dot(xb[i], p[i].astype(BF16)) for i in N]
    rhs = [jnp.concatenate([kh[ch[cc, h]] * (bc[i] * jnp.exp(gc[i])), vh[ch[cc, h]] * bc[i]],
                           axis=1).astype(BF16) for i, (cc, d, h) in enumerate(DH)]
    sol = [_dot(p[i].astype(BF16), rhs[i]) for i in N]
    for i, (cc, d, h) in enumerate(DH):
        j = ch[cc, h]
        w_ref[d, rows[cc], lanes[h]] = sol[i][:, :DN_DIM].astype(BF16)
        u0_ref[d, rows[cc], lanes[h]] = sol[i][:, DN_DIM:].astype(BF16)
        ke_ref[d, rows[cc], lanes[h]] = (kh[j] * jnp.exp(gam_tot[cc, d][:, h:h + 1] - gc[i])).astype(BF16)
        if with_q:
            qs_ref[d, rows[cc], lanes[h]] = (qh[j] * jnp.exp(gc[i])).astype(BF16)
            qk_ref[d, rows[cc], h * C:(h + 1) * C] = (qk_raw[j] * decay[i]).astype(BF16)


def _dn_scan_kernel(*refs, with_q, n_chunks):
    n_state = 2 * DN_HEADS
    s_refs = refs[-n_state:]
    refs = refs[:-n_state]
    if with_q:
        (w0, w1, u0, u1, k0, k1, g0, g1, qs0, qs1, qk0, qk1, s0_ref, o0_ref, o1_ref, sfin_ref) = refs
        qs_r, qk_r, o_r = (qs0, qs1), (qk0, qk1), (o0_ref, o1_ref)
    else:
        (w0, w1, u0, u1, k0, k1, g0, g1, s0_ref, sfin_ref) = refs
    w_r, u_r, k_r, g_r = (w0, w1), (u0, u1), (k0, k1), (g0, g1)
    c = pl.program_id(1)
    C = DN_CHUNK
    DH = [(d, h) for d in range(2) for h in range(DN_HEADS)]
    N = range(len(DH))
    lanes = [slice(h * DN_DIM, (h + 1) * DN_DIM) for h in range(DN_HEADS)]

    @pl.when(c == 0)
    def _():
        for i, (d, h) in enumerate(DH):
            s_refs[i][...] = s0_ref[d, h]

    s = [s_refs[i][...] for i in N]
    for sub in range(DN_SCAN_CHUNKS):
        cix = (sub, DN_SCAN_CHUNKS - 1 - sub)
        rows = [slice(cix[d] * C, (cix[d] + 1) * C) for d in range(2)]
        sb = [s[i].astype(BF16) for i in N]
        w = [w_r[d][rows[d], lanes[h]] for d, h in DH]
        if with_q:
            wq = [jnp.concatenate([w[i], qs_r[d][rows[d], lanes[h]]], axis=0) for i, (d, h) in enumerate(DH)]
            ws = [_dot(wq[i], sb[i]) for i in N]
            ub = [(u_r[d][rows[d], lanes[h]].astype(F32) - ws[i][:C]).astype(BF16)
                  for i, (d, h) in enumerate(DH)]
            qu = [_dot(qk_r[d][rows[d], h * C:(h + 1) * C], ub[i]) for i, (d, h) in enumerate(DH)]
            for i, (d, h) in enumerate(DH):
                o_r[d][rows[d], lanes[h]] = (ws[i][C:] + qu[i]).astype(BF16)
        else:
            ws = [_dot(w[i], sb[i]) for i in N]
            ub = [(u_r[d][rows[d], lanes[h]].astype(F32) - ws[i]).astype(BF16) for i, (d, h) in enumerate(DH)]
        ku = [_dot_tn(k_r[d][rows[d], lanes[h]], ub[i]) for i, (d, h) in enumerate(DH)]
        s = [jnp.exp(g_r[d][cix[d]][:, h:h + 1]) * s[i] + ku[i] for i, (d, h) in enumerate(DH)]
    for i in N:
        s_refs[i][...] = s[i]

    @pl.when(c == n_chunks // DN_SCAN_CHUNKS - 1)
    def _():
        for i, (d, h) in enumerate(DH):
            sfin_ref[d, h] = s_refs[i][...]


def _deltanet(main3, gate_col, gate_row, conv_w, alog, dtb, s0, with_q):
    B, T, _ = main3.shape
    C = DN_CHUNK
    n_chunks = T // C
    NC = DN_PREP_CHUNKS
    R = NC * C
    assert T % R == 0
    hpc = R // DN_HALO
    n_halo = T // DN_HALO

    def trio(col):
        return [pl.BlockSpec((None, DN_HALO, D_MODEL), lambda b, c: (b, jnp.maximum(c * hpc - 1, 0), col)),
                pl.BlockSpec((None, R, D_MODEL), lambda b, c: (b, c, col)),
                pl.BlockSpec((None, DN_HALO, D_MODEL),
                             lambda b, c: (b, jnp.minimum((c + 1) * hpc, n_halo - 1), col))]

    vec = lambda shape: pl.BlockSpec(shape, lambda b, c: (0,) * len(shape))
    in_specs = (trio(COL_DN_Q) if with_q else []) + trio(COL_DN_K) + trio(COL_DN_V) + [
        pl.BlockSpec((2, None, R, 2 * DN_HEADS), lambda b, c: (0, b, c, 0)),
        pl.BlockSpec((2, None, NC, 2 * DN_HEADS, C), lambda b, c: (0, b, c, 0, 0)),
        vec((DN_CONV, 3 * D_MODEL)),
        vec((2, 1, DN_HEADS)), vec((2, DN_HEADS, 1)), vec((2, 1, DN_HEADS)), vec((2, DN_HEADS, 1)),
    ]
    wide = lambda n: (pl.BlockSpec((2, None, R, n), lambda b, c: (0, b, c, 0)),
                      jax.ShapeDtypeStruct((2, B, T, n), BF16))
    outs = [wide(D_MODEL), wide(D_MODEL), wide(D_MODEL),
            (pl.BlockSpec((2, None, NC, 1, DN_HEADS), lambda b, c: (0, b, c, 0, 0)),
             jax.ShapeDtypeStruct((2, B, n_chunks, 1, DN_HEADS), F32))]
    if with_q:
        outs += [wide(D_MODEL), wide(DN_HEADS * C)]
    n_main = 3 if with_q else 2
    prep = pl.pallas_call(
        functools.partial(_dn_prep_kernel, with_q=with_q, n_chunks=n_chunks),
        grid=(B, n_chunks // NC),
        in_specs=in_specs, out_specs=[o[0] for o in outs], out_shape=[o[1] for o in outs],
        compiler_params=_cparams(("arbitrary", "arbitrary")),
        name="dn_prep_q" if with_q else "dn_prep",
    )(*([main3] * (3 * n_main)), gate_col, gate_row, conv_w,
      alog.reshape(2, 1, DN_HEADS), alog.reshape(2, DN_HEADS, 1),
      dtb.reshape(2, 1, DN_HEADS), dtb.reshape(2, DN_HEADS, 1))

    NS = DN_SCAN_CHUNKS
    RS = NS * C
    n_steps = n_chunks // NS
    assert n_chunks % NS == 0

    def both_dirs(arr, n):
        if n is None:
            return [pl.BlockSpec((None, None, NS, 1, DN_HEADS), lambda b, c: (0, b, c, 0, 0)),
                    pl.BlockSpec((None, None, NS, 1, DN_HEADS),
                                 lambda b, c: (1, b, n_steps - 1 - c, 0, 0))], [arr, arr]
        return [pl.BlockSpec((None, None, RS, n), lambda b, c: (0, b, c, 0)),
                pl.BlockSpec((None, None, RS, n), lambda b, c: (1, b, n_steps - 1 - c, 0))], [arr, arr]

    specs, args = [], []
    widths = [D_MODEL, D_MODEL, D_MODEL, None] + ([D_MODEL, DN_HEADS * C] if with_q else [])
    for arr, n in zip(prep, widths):
        sp, ar = both_dirs(arr, n)
        specs += sp
        args += ar
    s_spec = pl.BlockSpec((2, None, DN_HEADS, DN_DIM, DN_DIM), lambda b, c: (0, b, 0, 0, 0))
    s_shape = jax.ShapeDtypeStruct((2, B, DN_HEADS, DN_DIM, DN_DIM), F32)
    if with_q:
        out_specs = [pl.BlockSpec((None, RS, D_MODEL), lambda b, c: (b, c, 0)),
                     pl.BlockSpec((None, RS, D_MODEL), lambda b, c: (b, n_steps - 1 - c, 0)), s_spec]
        out_shape = [jax.ShapeDtypeStruct((B, T, D_MODEL), BF16)] * 2 + [s_shape]
    else:
        out_specs, out_shape = [s_spec], [s_shape]
    out = pl.pallas_call(
        functools.partial(_dn_scan_kernel, with_q=with_q, n_chunks=n_chunks),
        grid=(B, n_steps),
        in_specs=specs + [s_spec], out_specs=out_specs, out_shape=out_shape,
        scratch_shapes=[pltpu.VMEM((DN_DIM, DN_DIM), F32)] * (2 * DN_HEADS),
        compiler_params=_cparams(("arbitrary", "arbitrary")),
        name="dn_scan_q" if with_q else "dn_scan",
    )(*args, s0)
    return (out[0], out[1], out[2]) if with_q else (None, None, out[0])


def _rope_tables(S):
    half = AT_DIM // 2
    nf = half // 2
    inv_freq = ROPE_BASE ** (-jnp.arange(nf, dtype=F32) / nf)
    t = jnp.arange(S, dtype=jnp.int32)
    row = (t // GRID_W).astype(F32)
    col = (t % GRID_W).astype(F32)
    lane = jnp.arange(LANES)
    dd = lane % AT_DIM
    pos = jnp.where((dd < half)[None, :], row[:, None], col[:, None])
    ang = pos * inv_freq[lane % nf][None, :]
    first = ((lane % half) < nf)[None, :]
    sin = jnp.sin(ang)
    return jnp.cos(ang), jnp.where(first, -sin, 0.0), jnp.where(first, 0.0, sin)


def _rope_kernel(q_ref, k_ref, v_ref, cos_ref, sa_ref, sb_ref, qo_ref, ko_ref, vo_ref):
    cos, sa, sb = cos_ref[...], sa_ref[...], sb_ref[...]
    nf = AT_DIM // 4

    def rot(x):
        return x * cos + pltpu.roll(x, LANES - nf, 1) * sa + pltpu.roll(x, nf, 1) * sb

    for j in range(AT_Q_HEADS * AT_DIM // LANES):
        lanes = slice(j * LANES, (j + 1) * LANES)
        qo_ref[:, lanes] = (rot(q_ref[:, lanes].astype(F32)) * AT_DIM ** -0.5).astype(BF16)
    ko_ref[...] = rot(k_ref[...]).astype(BF16)
    vo_ref[...] = v_ref[...].astype(BF16)


def _rope(main, small, tables, S, tm=512):
    n_tok = main.shape[0]
    per_seq = S // tm
    tab_spec = pl.BlockSpec((tm, LANES), lambda i: (i % per_seq, 0))
    return pl.pallas_call(
        _rope_kernel,
        grid=(n_tok // tm,),
        in_specs=[pl.BlockSpec((tm, D_MODEL), lambda i: (i, COL_AT_Q)),
                  pl.BlockSpec((tm, LANES), lambda i: (i, 0)),
                  pl.BlockSpec((tm, LANES), lambda i: (i, 1)),
                  tab_spec, tab_spec, tab_spec],
        out_specs=[pl.BlockSpec((tm, D_MODEL), lambda i: (i, 0)),
                   pl.BlockSpec((tm, LANES), lambda i: (i, 0)),
                   pl.BlockSpec((tm, LANES), lambda i: (i, 0))],
        out_shape=[jax.ShapeDtypeStruct((n_tok, D_MODEL), BF16),
                   jax.ShapeDtypeStruct((n_tok, LANES), BF16),
                   jax.ShapeDtypeStruct((n_tok, LANES), BF16)],
        compiler_params=_cparams(("arbitrary",)),
        name="rope",
    )(main, small, small, *tables)


def _attn_kernel(*refs, local, n_blocks, q_scale):
    if local:
        (q_ref, kp_ref, kc_ref, kn_ref, vp_ref, vc_ref, vn_ref, kx_ref, vx_ref, sink_ref, o_ref) = refs
    else:
        (q_ref, kx_ref, vx_ref, sink_ref, o_ref) = refs
    P = AT_BLOCK
    G = AT_Q_HEADS // AT_KV_HEADS
    kx = kx_ref[...].astype(BF16)
    vx = vx_ref[...].astype(BF16)
    if local:
        i = pl.program_id(1)
        k_all = jnp.concatenate([kp_ref[...], kc_ref[...], kn_ref[...], kx], axis=0)
        v_all = jnp.concatenate([vp_ref[...], vc_ref[...], vn_ref[...], vx], axis=0)
        qi = lax.broadcasted_iota(I32, (P, P), 0)
        kj = lax.broadcasted_iota(I32, (P, P), 1)
        b_prev = jnp.where(kj >= qi, 0.0, NEG_BIG) + jnp.where(i > 0, 0.0, NEG_BIG)
        b_next = jnp.where(kj <= qi, 0.0, NEG_BIG) + jnp.where(i < n_blocks - 1, 0.0, NEG_BIG)
        b_prev2 = jnp.concatenate([b_prev, b_prev], axis=0)
        b_next2 = jnp.concatenate([b_next, b_next], axis=0)
    else:
        k_all, v_all = kx, vx
    lo = lax.broadcasted_iota(I32, (P, LANES), 1) < AT_DIM
    qf = q_ref[...].astype(F32) * q_scale
    pieces = []
    for qh in range(AT_Q_HEADS):
        blk = qf[:, (qh // 2) * LANES:(qh // 2 + 1) * LANES]
        want_lo = qh // G == 0
        if want_lo != (qh % 2 == 0):
            blk = pltpu.roll(blk, AT_DIM, 1)
        pieces.append(jnp.where(lo if want_lo else ~lo, blk, 0.0).astype(BF16))
    n_pair = AT_Q_HEADS // 2

    def logits(j):
        s = _dot_nt(jnp.concatenate([pieces[2 * j], pieces[2 * j + 1]], axis=0), k_all)
        if not local:
            return s
        return jnp.concatenate([s[:, 0:P] + b_prev2, s[:, P:2 * P], s[:, 2 * P:3 * P] + b_next2,
                                s[:, 3 * P:]], axis=1)

    def softmax(j, s):
        sink = sink_ref[2 * j * P:(2 * j + 2) * P, :]
        m = jnp.maximum(jnp.max(s, axis=-1, keepdims=True), sink)
        p = jnp.exp(s - m)
        den = jnp.sum(p, axis=-1, keepdims=True) + jnp.exp(sink - m)
        return p.astype(BF16), den

    def values(j, p, den):
        o = _dot(p, v_all) / den
        a, b = o[:P], o[P:]
        if (2 * j) // G == 0:
            out = jnp.where(lo, a, pltpu.roll(b, AT_DIM, 1))
        else:
            out = jnp.where(lo, pltpu.roll(a, AT_DIM, 1), b)
        o_ref[:, j * LANES:(j + 1) * LANES] = out.astype(BF16)

    s_next = logits(0)
    prob = None
    for j in range(n_pair):
        s_cur = s_next
        if j + 1 < n_pair:
            s_next = logits(j + 1)
        done = prob
        prob = softmax(j, s_cur)
        if done is not None:
            values(j - 1, *done)
    values(n_pair - 1, *prob)


def _attention_local(q_r, k_r, v_r, small_c, sinks, B, S, L):
    P = AT_BLOCK
    nb = S // P

    def kv_trio():
        return [pl.BlockSpec((P, LANES), lambda b, i: (b * nb + jnp.maximum(i - 1, 0), 0)),
                pl.BlockSpec((P, LANES), lambda b, i: (b * nb + i, 0)),
                pl.BlockSpec((P, LANES), lambda b, i: (b * nb + jnp.minimum(i + 1, nb - 1), 0))]

    return pl.pallas_call(
        functools.partial(_attn_kernel, local=True, n_blocks=nb, q_scale=1.0),
        grid=(B, nb),
        in_specs=[pl.BlockSpec((P, D_MODEL), lambda b, i: (b * nb + i, 0))] + kv_trio() + kv_trio() + [
            pl.BlockSpec((L, LANES), lambda b, i: (b, 0)),
            pl.BlockSpec((L, LANES), lambda b, i: (b, 1)),
            pl.BlockSpec((AT_Q_HEADS * AT_BLOCK, 1), lambda b, i: (0, 0))],
        out_specs=pl.BlockSpec((P, D_MODEL), lambda b, i: (b * nb + i, 0)),
        out_shape=jax.ShapeDtypeStruct((B * S, D_MODEL), BF16),
        compiler_params=_cparams(("arbitrary", "arbitrary")),
        name="attn_local",
    )(q_r, k_r, k_r, k_r, v_r, v_r, v_r, small_c, small_c, sinks)


def _attention_ctx(main_c, small_c, sinks, B, L):
    P = AT_BLOCK
    nb = L // P
    return pl.pallas_call(
        functools.partial(_attn_kernel, local=False, n_blocks=nb, q_scale=AT_DIM ** -0.5),
        grid=(B, nb),
        in_specs=[pl.BlockSpec((P, D_MODEL), lambda b, i: (b * nb + i, COL_AT_Q)),
                  pl.BlockSpec((L, LANES), lambda b, i: (b, 0)),
                  pl.BlockSpec((L, LANES), lambda b, i: (b, 1)),
                  pl.BlockSpec((AT_Q_HEADS * AT_BLOCK, 1), lambda b, i: (0, 0))],
        out_specs=pl.BlockSpec((P, D_MODEL), lambda b, i: (b * nb + i, 0)),
        out_shape=jax.ShapeDtypeStruct((B * L, D_MODEL), BF16),
        compiler_params=_cparams(("arbitrary", "arbitrary")),
        name="attn_ctx",
    )(main_c, small_c, small_c, sinks)


def _merge_kernel(ysg_ref, of_ref, ob_ref, dng_ref, yat_ref, g0_ref, g1_ref, g2_ref, x_ref, mod_ref,
                  dn_norm_ref, post_ref, pre_ref, wsg_ref, wdn_ref, wat_ref, wout_ref, rw_ref, rb_ref,
                  xo_ref, h2_ref, lg_ref):
    o = of_ref[...].astype(F32) + ob_ref[...].astype(F32)
    dn_g = dn_norm_ref[...]
    parts = []
    for h in range(DN_HEADS):
        lanes = slice(h * DN_DIM, (h + 1) * DN_DIM)
        parts.append(_rms(o[:, lanes], dn_g) * _silu(dng_ref[:, lanes].astype(F32)))
    ydn = jnp.concatenate(parts, axis=1).astype(BF16)
    m = (_sigmoid(g0_ref[...].astype(F32)) * _dot(ysg_ref[...], wsg_ref[...])
         + _sigmoid(g1_ref[...].astype(F32)) * _dot(ydn, wdn_ref[...])
         + _sigmoid(g2_ref[...].astype(F32)) * _dot(yat_ref[...], wat_ref[...]))
    y = _dot(m.astype(BF16), wout_ref[...])
    gate1 = mod_ref[:, 2 * D_MODEL:3 * D_MODEL]
    sh2 = mod_ref[:, 3 * D_MODEL:4 * D_MODEL]
    sc2 = mod_ref[:, 4 * D_MODEL:5 * D_MODEL]
    xn = x_ref[...] + gate1 * _rms(y, post_ref[...])
    xo_ref[...] = xn
    h2 = _rms(xn, pre_ref[...]) * (1.0 + sc2) + sh2
    h2_ref[...] = h2
    lg_ref[...] = _dot(h2.astype(BF16), rw_ref[...]) + rb_ref[...]


def _merge(ysg, o_fwd, o_bwd, main, yat, x, mod, mod_row, lw, tm=512):
    n_tok = x.shape[0]
    const = lambda i: (0, 0)
    wspec = pl.BlockSpec((D_MODEL, D_MODEL), const, pipeline_mode=pl.Buffered(1))
    vspec = pl.BlockSpec((1, D_MODEL), const)
    return pl.pallas_call(
        _merge_kernel,
        grid=(n_tok // tm,),
        in_specs=[pl.BlockSpec((tm, D_MODEL), lambda i: (i, 0)),
                  pl.BlockSpec((tm, D_MODEL), lambda i: (i, 0)),
                  pl.BlockSpec((tm, D_MODEL), lambda i: (i, 0)),
                  pl.BlockSpec((tm, D_MODEL), lambda i: (i, COL_DN_G)),
                  pl.BlockSpec((tm, D_MODEL), lambda i: (i, 0)),
                  pl.BlockSpec((tm, D_MODEL), lambda i: (i, COL_GATE0)),
                  pl.BlockSpec((tm, D_MODEL), lambda i: (i, COL_GATE0 + 1)),
                  pl.BlockSpec((tm, D_MODEL), lambda i: (i, COL_GATE0 + 2)),
                  pl.BlockSpec((tm, D_MODEL), lambda i: (i, 0)),
                  pl.BlockSpec((None, 1, 6 * D_MODEL), lambda i: (mod_row(i * tm), 0, 0)),
                  pl.BlockSpec((1, DN_DIM), const), vspec, vspec,
                  wspec, wspec, wspec, wspec,
                  pl.BlockSpec((D_MODEL, LANES), const), pl.BlockSpec((1, LANES), const)],
        out_specs=[pl.BlockSpec((tm, D_MODEL), lambda i: (i, 0)),
                   pl.BlockSpec((tm, D_MODEL), lambda i: (i, 0)),
                   pl.BlockSpec((tm, LANES), lambda i: (i, 0))],
        out_shape=[jax.ShapeDtypeStruct((n_tok, D_MODEL), F32),
                   jax.ShapeDtypeStruct((n_tok, D_MODEL), F32),
                   jax.ShapeDtypeStruct((n_tok, LANES), F32)],
        compiler_params=_cparams(("arbitrary",)),
        name="merge",
    )(ysg, o_fwd, o_bwd, main, yat, main, main, main, x, mod,
      lw["dn_norm_g"], lw["norm_post_mix"], lw["norm_pre_ffn"],
      lw["w_proj_sg"], lw["w_proj_dn"], lw["w_proj_at"], lw["w_out"], lw["router_w"], lw["router_b"])


MOE_TOK = 256
MOE_PIECE = 8
MOE_BUF = MOE_TOK * TOP_K + N_EXPERTS * MOE_PIECE
MOE_META = 256
assert MOE_META > MOE_BUF // MOE_PIECE


def _route_kernel(lg_ref, gate_ref, lpos_ref, tcnt_ref):
    tm = lg_ref.shape[0]
    l = lg_ref[...]
    lane = lax.broadcasted_iota(I32, l.shape, 1).astype(F32)
    vals, onehots = [], []
    for k in range(TOP_K):
        m = jnp.max(l, axis=-1, keepdims=True)
        ik = jnp.min(jnp.where(l == m, lane, float(LANES)), axis=-1, keepdims=True)
        oh = lane == ik
        vals.append(m)
        onehots.append(oh)
        l = jnp.where(oh, -jnp.inf, l)
    es = [jnp.exp(v - vals[0]) for v in vals]
    den = es[0] + es[1] + es[2] + es[3]
    sel = jnp.zeros(l.shape, F32)
    for k in range(TOP_K):
        gate_ref[:, k:k + 1] = es[k] / den
        sel = sel + onehots[k].astype(F32)
    ri = lax.broadcasted_iota(I32, (tm, tm), 0)
    ci = lax.broadcasted_iota(I32, (tm, tm), 1)
    before = _dot((ri > ci).astype(BF16), sel.astype(BF16))
    tcnt = jnp.sum(sel, axis=0, keepdims=True)
    tcnt_ref[...] = tcnt
    n_piece = jnp.floor((tcnt + (MOE_PIECE - 1)) * (1.0 / MOE_PIECE))
    ei = lax.broadcasted_iota(I32, (LANES, LANES), 0)
    ej = lax.broadcasted_iota(I32, (LANES, LANES), 1)
    run_start = _dot(jnp.broadcast_to(n_piece, (8, LANES)).astype(BF16),
                     (ei < ej).astype(BF16))[0:1] * float(MOE_PIECE)
    pos = before + run_start
    for k in range(TOP_K):
        lpos_ref[:, k:k + 1] = jnp.sum(jnp.where(onehots[k], pos, 0.0), axis=-1,
                                       keepdims=True).astype(I32)


def _route(logits):
    n_tok = logits.shape[0]
    tm = MOE_TOK
    n_t = n_tok // tm
    small = lambda dt: jax.ShapeDtypeStruct((n_tok, TOP_K), dt)
    kspec = pl.BlockSpec((tm, TOP_K), lambda i: (i, 0))
    tspec = pl.BlockSpec((None, 1, LANES), lambda i: (i, 0, 0))
    tshape = jax.ShapeDtypeStruct((n_t, 1, LANES), F32)
    return pl.pallas_call(
        _route_kernel,
        grid=(n_t,),
        in_specs=[pl.BlockSpec((tm, LANES), lambda i: (i, 0))],
        out_specs=[kspec, kspec, tspec],
        out_shape=[small(F32), small(I32), tshape],
        compiler_params=_cparams(("arbitrary",)),
        name="route",
    )(logits)


def _run_copies(meta_ref, base, src_of, dst_of, sem, start):
    def per_piece(q, carry):
        local = pl.multiple_of(q * MOE_PIECE, MOE_PIECE)
        slot = pl.multiple_of(meta_ref[base + 1 + q], MOE_PIECE)
        cp = pltpu.make_async_copy(src_of(local, slot), dst_of(local, slot), sem)
        if start:
            cp.start()
        else:
            cp.wait()
        return carry

    lax.fori_loop(0, meta_ref[base], per_piece, 0)


def _dispatch_kernel(meta_ref, prev_ref, zmeta_ref, lpos_ref, h_ref, xs_ref, buf_ref, buf1_ref, zero_ref,
                     sem, sem1, *, tm_e):
    tm = MOE_TOK
    step = pl.program_id(0)
    bufs, sems = (buf_ref, buf1_ref), (sem, sem1)
    rows = lambda ref, r: ref.at[pl.ds(r, MOE_PIECE)]

    @pl.when(step == 0)
    def _():
        zero_ref[...] = jnp.zeros_like(zero_ref)

        def zero_tail(start):
            def per_expert(e, carry):
                z0 = pl.multiple_of(zmeta_ref[e], MOE_PIECE)

                def per_piece(p, c2):
                    cp = pltpu.make_async_copy(zero_ref, rows(xs_ref, z0 + p * MOE_PIECE), sem)
                    if start:
                        cp.start()
                    else:
                        cp.wait()
                    return c2

                return lax.fori_loop(0, zmeta_ref[N_EXPERTS + e], per_piece, carry)

            lax.fori_loop(0, N_EXPERTS, per_expert, 0)

        zero_tail(True)
        zero_tail(False)

        buf_ref[0:tm_e, :] = jnp.zeros((tm_e, D_MODEL // 2), U32)

        def zero_tiles(start):
            def per_tile(p, carry):
                t0 = pl.multiple_of((zmeta_ref[2 * N_EXPERTS] + p) * tm_e, tm_e)
                cp = pltpu.make_async_copy(buf_ref.at[pl.ds(0, tm_e)], xs_ref.at[pl.ds(t0, tm_e)], sem)
                if start:
                    cp.start()
                else:
                    cp.wait()
                return carry

            lax.fori_loop(0, zmeta_ref[2 * N_EXPERTS + 1], per_tile, 0)

        zero_tiles(True)
        zero_tiles(False)

    def group(j):
        s_iota = lax.broadcasted_iota(I32, (MOE_BUF, tm), 0)
        perm = jnp.zeros((MOE_BUF, tm), F32)
        for k in range(TOP_K):
            perm = jnp.where(s_iota == lpos_ref[k:k + 1, j * tm:(j + 1) * tm], 1.0, perm)
        bufs[j][...] = _pack_bf16_pairs(_dot(perm.astype(BF16), h_ref[j * tm:(j + 1) * tm, :].astype(BF16)))

    def copies(mref, j, start):
        _run_copies(mref, j * MOE_META, lambda loc, slot: rows(bufs[j], loc),
                    lambda loc, slot: rows(xs_ref, slot), sems[j], start)

    group(0)
    copies(meta_ref, 0, True)

    @pl.when(step > 0)
    def _():
        copies(prev_ref, 1, False)

    group(1)
    copies(meta_ref, 1, True)
    copies(meta_ref, 0, False)

    @pl.when(step == pl.num_programs(0) - 1)
    def _():
        copies(meta_ref, 1, False)


def _dispatch(meta, zmeta, lpos_t, h2, n_slots, tm_e):
    n_tok = h2.shape[0]
    tm = 2 * MOE_TOK
    assert tm_e <= MOE_BUF and n_tok % tm == 0
    return pl.pallas_call(
        functools.partial(_dispatch_kernel, tm_e=tm_e),
        grid=(n_tok // tm,),
        in_specs=[pl.BlockSpec((2 * MOE_META,), lambda i: (i,), memory_space=pltpu.SMEM),
                  pl.BlockSpec((2 * MOE_META,), lambda i: (jnp.maximum(i - 1, 0),), memory_space=pltpu.SMEM),
                  pl.BlockSpec((MOE_META,), lambda i: (0,), memory_space=pltpu.SMEM),
                  pl.BlockSpec((TOP_K, tm), lambda i: (0, i)),
                  pl.BlockSpec((tm, D_MODEL), lambda i: (i, 0))],
        out_specs=pl.BlockSpec(memory_space=pl.ANY),
        out_shape=jax.ShapeDtypeStruct((n_slots, D_MODEL // 2), U32),
        scratch_shapes=[pltpu.VMEM((MOE_BUF, D_MODEL // 2), U32), pltpu.VMEM((MOE_BUF, D_MODEL // 2), U32),
                        pltpu.VMEM((MOE_PIECE, D_MODEL // 2), U32),
                        pltpu.SemaphoreType.DMA, pltpu.SemaphoreType.DMA],
        compiler_params=_cparams(("arbitrary",)),
        name="moe_dispatch",
    )(meta, meta, zmeta, lpos_t, h2)


def _pack_bf16_pairs(x):
    w = x.shape[1] // 2
    xb = x.astype(BF16).astype(F32)
    lo = lax.shift_right_logical(lax.bitcast_convert_type(xb[:, :w], U32), jnp.uint32(16))
    hi = lax.bitcast_convert_type(xb[:, w:], U32) & jnp.uint32(0xFFFF0000)
    return hi | lo


def _unpack_bf16_pairs(p):
    lo = lax.bitcast_convert_type(lax.shift_left(p, jnp.uint32(16)), F32)
    hi = lax.bitcast_convert_type(p & jnp.uint32(0xFFFF0000), F32)
    return jnp.concatenate([lo, hi], axis=1).astype(BF16)


def _expert_kernel(te_ref, first_ref, nu_ref, xs_ref, wgu_ref, bgu_ref, wd_ref, bd_ref, y_ref,
                   wgu_b_ref, wd_b_ref):
    del te_ref
    i = pl.program_id(0)

    @pl.when(first_ref[i] == 1)
    def _():
        wgu_b_ref[...] = wgu_ref[...].astype(BF16)
        wd_b_ref[...] = wd_ref[...].astype(BF16)

    @pl.when(i < nu_ref[0])
    def _():
        gu = _dot(_unpack_bf16_pairs(xs_ref[...]), wgu_b_ref[...]) + bgu_ref[...]
        g = jnp.minimum(gu[:, :D_EXPERT], SWIGLU_LIMIT)
        lin = jnp.clip(gu[:, D_EXPERT:], -SWIGLU_LIMIT, SWIGLU_LIMIT)
        act = g * _sigmoid(SWIGLU_ALPHA * g) * (lin + 1.0)
        y_ref[...] = _pack_bf16_pairs(_dot(act.astype(BF16), wd_b_ref[...]) + bd_ref[...])

    @pl.when(i >= nu_ref[0])
    def _():
        y_ref[...] = jnp.zeros_like(y_ref)


def _experts(tile_expert, n_used, xs, wgu, bgu, wd, bd, layer, tm):
    n_slots = xs.shape[0]
    n_tiles = n_slots // tm
    first = jnp.concatenate([jnp.ones((1,), I32),
                             (tile_expert[1:] != tile_expert[:-1]).astype(I32)])

    def row(i, te, fi, nu):
        return (jnp.minimum(i, nu[0] - 1), 0)

    grid_spec = pltpu.PrefetchScalarGridSpec(
        num_scalar_prefetch=3,
        grid=(n_tiles,),
        in_specs=[pl.BlockSpec((tm, D_MODEL // 2), row),
                  pl.BlockSpec((None, None, D_MODEL, 2 * D_EXPERT), lambda i, te, fi, nu: (layer, te[i], 0, 0)),
                  pl.BlockSpec((None, None, 1, 2 * D_EXPERT), lambda i, te, fi, nu: (layer, te[i], 0, 0)),
                  pl.BlockSpec((None, None, D_EXPERT, D_MODEL), lambda i, te, fi, nu: (layer, te[i], 0, 0)),
                  pl.BlockSpec((None, None, 1, D_MODEL), lambda i, te, fi, nu: (layer, te[i], 0, 0))],
        out_specs=pl.BlockSpec((tm, D_MODEL // 2), lambda i, te, fi, nu: (i, 0)),
        scratch_shapes=[pltpu.VMEM((D_MODEL, 2 * D_EXPERT), BF16), pltpu.VMEM((D_EXPERT, D_MODEL), BF16)],
    )
    return pl.pallas_call(
        _expert_kernel,
        grid_spec=grid_spec,
        out_shape=jax.ShapeDtypeStruct((n_slots, D_MODEL // 2), U32),
        compiler_params=_cparams(("arbitrary",)),
        name="moe_experts",
    )(tile_expert, first, n_used, xs, wgu, bgu, wd, bd)


def _combine_kernel(meta_ref, next_ref, lpos_ref, gate_ref, x_ref, mod_ref, post_ref, y_ref, xo_ref,
                    buf_ref, buf1_ref, sem, sem1):
    tm = MOE_TOK
    step = pl.program_id(0)
    bufs, sems = (buf_ref, buf1_ref), (sem, sem1)
    rows = lambda ref, r: ref.at[pl.ds(r, MOE_PIECE)]

    def copies(mref, j, base, start):
        _run_copies(mref, base, lambda loc, slot: rows(y_ref, slot),
                    lambda loc, slot: rows(bufs[j], loc), sems[j], start)

    def reduce_tile(j):
        tok = slice(j * tm, (j + 1) * tm)
        s_iota = lax.broadcasted_iota(I32, (tm, MOE_BUF), 1)
        sel = jnp.zeros((tm, MOE_BUF), F32)
        for k in range(TOP_K):
            sel = jnp.where(s_iota == lpos_ref[tok, k:k + 1], gate_ref[tok, k:k + 1], sel)
        y = _dot(sel.astype(BF16), _unpack_bf16_pairs(bufs[j][...]))
        gate2 = mod_ref[:, 5 * D_MODEL:6 * D_MODEL]
        xo_ref[tok, :] = x_ref[tok, :] + gate2 * _rms(y, post_ref[...])

    @pl.when(step == 0)
    def _():
        buf_ref[...] = jnp.zeros_like(buf_ref)
        buf1_ref[...] = jnp.zeros_like(buf1_ref)
        copies(meta_ref, 0, 0, True)

    copies(meta_ref, 1, MOE_META, True)
    copies(meta_ref, 0, 0, False)
    reduce_tile(0)

    @pl.when(step < pl.num_programs(0) - 1)
    def _():
        copies(next_ref, 0, 0, True)

    copies(meta_ref, 1, MOE_META, False)
    reduce_tile(1)


def _combine(meta, lpos, gate, x_mid, mod, mod_row, post_g, y):
    n_tok = x_mid.shape[0]
    tm = 2 * MOE_TOK
    n_steps = n_tok // tm
    return pl.pallas_call(
        _combine_kernel,
        grid=(n_steps,),
        in_specs=[pl.BlockSpec((2 * MOE_META,), lambda i: (i,), memory_space=pltpu.SMEM),
                  pl.BlockSpec((2 * MOE_META,), lambda i: (jnp.minimum(i + 1, n_steps - 1),),
                               memory_space=pltpu.SMEM),
                  pl.BlockSpec((tm, TOP_K), lambda i: (i, 0)),
                  pl.BlockSpec((tm, TOP_K), lambda i: (i, 0)),
                  pl.BlockSpec((tm, D_MODEL), lambda i: (i, 0)),
                  pl.BlockSpec((None, 1, 6 * D_MODEL), lambda i: (mod_row(i * tm), 0, 0)),
                  pl.BlockSpec((1, D_MODEL), lambda i: (0, 0)),
                  pl.BlockSpec(memory_space=pl.ANY)],
        out_specs=pl.BlockSpec((tm, D_MODEL), lambda i: (i, 0)),
        out_shape=jax.ShapeDtypeStruct((n_tok, D_MODEL), F32),
        scratch_shapes=[pltpu.VMEM((MOE_BUF, D_MODEL // 2), U32), pltpu.VMEM((MOE_BUF, D_MODEL // 2), U32),
                        pltpu.SemaphoreType.DMA, pltpu.SemaphoreType.DMA],
        compiler_params=_cparams(("arbitrary",)),
        name="moe_combine",
    )(meta, meta, lpos, gate, x_mid, mod, post_g, y)


def _moe(h2, logits, x_mid, mod, mod_row, lw, tm_e=512):
    n_tok = h2.shape[0]
    n_t = n_tok // MOE_TOK
    gate, lpos, tcnt = _route(logits)
    tcnt = tcnt[:, 0, :N_EXPERTS].astype(I32)
    pieces = (tcnt + MOE_PIECE - 1) // MOE_PIECE
    run_end = jnp.cumsum(pieces, axis=0) * MOE_PIECE
    used = run_end[-1]
    padded = (used + tm_e - 1) // tm_e * tm_e
    pad_end = jnp.cumsum(padded)
    offs = pad_end - padded
    n_tiles = (n_tok * TOP_K + n_t * N_EXPERTS * (MOE_PIECE - 1) + tm_e - 1) // tm_e + N_EXPERTS
    tile_start = jnp.arange(n_tiles, dtype=I32) * tm_e
    tile_expert = jnp.minimum(jnp.sum(pad_end[None, :] <= tile_start[:, None], axis=1),
                              N_EXPERTS - 1).astype(I32)
    n_used = (pad_end[-1:] // tm_e).astype(I32)
    piece_end = jnp.cumsum(pieces, axis=1)
    slot_start = offs[None, :] + run_end - pieces * MOE_PIECE
    q = jnp.arange(MOE_META - 1, dtype=I32)
    owner = q[None, :, None] >= piece_end[:, None, :]
    is_owner = owner != jnp.concatenate([jnp.ones_like(owner[..., :1]), owner[..., :-1]], axis=-1)
    run_base = slot_start - (piece_end - pieces) * MOE_PIECE
    piece_slot = jnp.sum(jnp.where(is_owner, run_base[:, None, :], 0), axis=-1) + q[None, :] * MOE_PIECE
    meta = jnp.concatenate([piece_end[:, -1:], piece_slot], axis=1).reshape(-1).astype(I32)
    z0 = offs + used
    tail = jnp.stack([n_used[0], n_tiles - n_used[0]])
    zmeta = jnp.concatenate([z0, (pad_end - z0) // MOE_PIECE, tail,
                             jnp.zeros((MOE_META - 2 * N_EXPERTS - 2,), I32)]).astype(I32)
    xs = _dispatch(meta, zmeta, lpos.T, h2, n_tiles * tm_e, tm_e)
    y = _experts(tile_expert, n_used, xs, lw["exp_w_gu"], lw["exp_b_gu"], lw["exp_w_down"],
                 lw["exp_b_down"], lw["layer"], tm_e)
    return _combine(meta, lpos, gate, x_mid, mod, mod_row, lw["norm_post_ffn"], y)


def _split_w_in(w_in):
    offs, o = {}, 0
    for name, width in (("dn_k", 1024), ("dn_v", 1024), ("dn_a", 16), ("dn_b", 16), ("at_k", 128),
                        ("at_v", 128), ("dn_q", 1024), ("dn_g", 1024), ("at_q", 1024),
                        ("sg_u", 1024), ("sg_v", 1024), ("gates", 3072)):
        offs[name] = (o, o + width)
        o += width
    sl = lambda n: w_in[:, offs[n][0]:offs[n][1]]
    w_main = jnp.concatenate([sl(n) for n in ("dn_k", "dn_v", "dn_q", "dn_g", "at_q", "sg_u", "sg_v",
                                              "gates")], axis=1).astype(BF16)
    pad = jnp.zeros((w_in.shape[0], N_SMALL_COLS - 2 * LANES - 4 * DN_HEADS), w_in.dtype)
    w_small = jnp.concatenate([sl("at_k"), sl("at_v"), sl("dn_a"), sl("dn_b"), pad], axis=1).astype(BF16)
    return w_main, w_small


def _dn_gates(small, B, T):
    ab = small[:, 2 * LANES:2 * LANES + 4 * DN_HEADS].reshape(B, T, 2, 2, DN_HEADS)
    col = jnp.transpose(ab, (3, 0, 1, 2, 4)).reshape(2, B, T, 2 * DN_HEADS)
    row = jnp.transpose(col.reshape(2, B, T // DN_CHUNK, DN_CHUNK, 2 * DN_HEADS), (0, 1, 2, 4, 3))
    return col, row


def kernel(x, c, ctx, c_ctx, w_mod, b_mod, norm_pre_mix, norm_post_mix, norm_pre_ffn, norm_post_ffn, w_in, sg_ln_g, sg_ln_b, sg_w, sg_b, dn_conv_w, dn_a_log, dn_dt_bias, dn_norm_g, at_sinks, w_proj_sg, w_proj_dn, w_proj_at, w_out, router_w, router_b, exp_w_gu, exp_b_gu, exp_w_down, exp_b_down):
    B, S, D = x.shape
    L = ctx.shape[1]
    depth = w_mod.shape[0]
    assert D == D_MODEL and S % GRID_W == 0
    n_lat, n_ctx = B * S, B * L

    rows = (B + 1 + 7) // 8 * 8
    cvec = jnp.zeros((rows, D), F32).at[:B].set(c).at[B].set(c_ctx)
    mod_all = _modulation(cvec, w_mod, b_mod)
    tables = _rope_tables(S)

    lat_row = lambda t: t // S
    ctx_row = lambda t: B
    all_row = lambda t: jnp.where(t < n_lat, t // S, B)

    xl = x.reshape(n_lat, D)
    xc = ctx.reshape(n_ctx, D)
    for l in range(depth):
        need_ctx_out = l < depth - 1
        mod = mod_all[l].reshape(rows, 1, 6 * D)
        w_main, w_small = _split_w_in(w_in[l])
        lw = {
            "dn_norm_g": dn_norm_g[l].reshape(1, -1),
            "norm_post_mix": norm_post_mix[l].reshape(1, -1),
            "norm_pre_ffn": norm_pre_ffn[l].reshape(1, -1),
            "norm_post_ffn": norm_post_ffn[l].reshape(1, -1),
            "w_proj_sg": w_proj_sg[l].astype(BF16), "w_proj_dn": w_proj_dn[l].astype(BF16),
            "w_proj_at": w_proj_at[l].astype(BF16), "w_out": w_out[l].astype(BF16),
            "router_w": jnp.pad(router_w[l], ((0, 0), (0, LANES - N_EXPERTS))).astype(BF16),
            "router_b": jnp.pad(router_b[l], (0, LANES - N_EXPERTS),
                                constant_values=NEG_BIG).reshape(1, -1),
            "layer": l,
            "exp_w_gu": exp_w_gu, "exp_b_gu": exp_b_gu.reshape(depth, N_EXPERTS, 1, -1),
            "exp_w_down": exp_w_down, "exp_b_down": exp_b_down.reshape(depth, N_EXPERTS, 1, -1),
        }
        pre_g = norm_pre_mix[l].reshape(1, -1)
        main, small = _inproj(xl, mod, lat_row, pre_g, w_main, w_small, min(1024, S))
        w_main_c = w_main if need_ctx_out else w_main[:, :N_CTX_MAIN_COLS]
        main_c, small_c = _inproj(xc, mod, ctx_row, pre_g, w_main_c, w_small, min(1024, n_ctx))

        sg_args = (sg_ln_g[l].reshape(1, -1), sg_ln_b[l].reshape(1, -1), sg_w[l].astype(BF16),
                   sg_b[l].T)
        ysg = _sgu(main, *sg_args)

        gcol_c, grow_c = _dn_gates(small_c, B, L)
        gcol, grow = _dn_gates(small, B, S)
        s0 = jnp.zeros((2, B, DN_HEADS, DN_DIM, DN_DIM), F32)
        of_c, ob_c, s_ctx = _deltanet(main_c.reshape(B, L, -1), gcol_c, grow_c, dn_conv_w[l], dn_a_log[l],
                                      dn_dt_bias[l], s0, need_ctx_out)
        of_l, ob_l, _ = _deltanet(main.reshape(B, S, -1), gcol, grow, dn_conv_w[l], dn_a_log[l],
                                  dn_dt_bias[l], s_ctx, True)

        sinks = jnp.repeat(at_sinks[l], AT_BLOCK).reshape(-1, 1)
        q_r, k_r, v_r = _rope(main, small, tables, S)
        yat = _attention_local(q_r, k_r, v_r, small_c, sinks, B, S, L)

        x_mid, h2, logits = _merge(ysg, of_l.reshape(n_lat, D), ob_l.reshape(n_lat, D), main, yat, xl, mod,
                                   lat_row, lw)
        if need_ctx_out:
            ysg_c = _sgu(main_c, *sg_args)
            yat_c = _attention_ctx(main_c, small_c, sinks, B, L)
            xc_mid, h2c, logits_c = _merge(ysg_c, of_c.reshape(n_ctx, D), ob_c.reshape(n_ctx, D), main_c,
                                           yat_c, xc, mod, ctx_row, lw)
            x_mid = jnp.concatenate([x_mid, xc_mid], axis=0)
            h2 = jnp.concatenate([h2, h2c], axis=0)
            logits = jnp.concatenate([logits, logits_c], axis=0)
            xo = _moe(h2, logits, x_mid, mod, all_row, lw)
            xl, xc = xo[:n_lat], xo[n_lat:]
        else:
            xl = _moe(h2, logits, x_mid, mod, lat_row, lw)
    return xl.reshape(B, S, D)
```

```python
import functools
import math

import jax
import jax.numpy as jnp
from jax import lax
from jax.experimental import pallas as pl
from jax.experimental.pallas import tpu as pltpu

F32 = jnp.float32
BF16 = jnp.bfloat16
I32 = jnp.int32
U32 = jnp.uint32

EPS = 1e-6
D_MODEL = 1024
GRID_W = 64

SG_CHUNK = 128
SG_GROUPS = 8

DN_HEADS = 8
DN_DIM = 128
DN_CONV = 5
DN_CHUNK = 64
DN_HALO = 16
DN_PREP_CHUNKS = 2
DN_SCAN_CHUNKS = 4

AT_Q_HEADS = 16
AT_KV_HEADS = 2
AT_DIM = 64
AT_BLOCK = 128
MERGE_ROW_GROUPS = 2
AT_STAGE_HEADS = 2
ROPE_BASE = 10000.0

N_EXPERTS = 32
TOP_K = 4
D_EXPERT = 1024
SWIGLU_ALPHA = 1.702
SWIGLU_LIMIT = 7.0
N_BRANCH = 3

LANES = 128
NEG_BIG = -1e30

COL_DN_K, COL_DN_V, COL_DN_Q, COL_DN_G, COL_AT_Q, COL_SG_U, COL_SG_V, COL_GATE0 = range(8)
N_MAIN_COLS = 10 * D_MODEL
N_CTX_MAIN_COLS = 2 * D_MODEL
N_SMALL_COLS = 3 * LANES

VMEM_LIMIT = 52 * 1024 * 1024


def _cparams(sem):
    return pltpu.CompilerParams(dimension_semantics=sem, vmem_limit_bytes=VMEM_LIMIT)


def _dot(a, b):
    return jnp.dot(a, b, preferred_element_type=F32)


def _dot_nt(a, b):
    return lax.dot_general(a, b, (((1,), (1,)), ((), ())), preferred_element_type=F32)


def _dot_tn(a, b):
    return lax.dot_general(a, b, (((0,), (0,)), ((), ())), preferred_element_type=F32)


def _sigmoid(x):
    return 0.5 * (1.0 + jnp.tanh(0.5 * x))


def _silu(x):
    return x * _sigmoid(x)


def _gelu_tanh(x):
    return 0.5 * x * (1.0 + jnp.tanh(math.sqrt(2.0 / math.pi) * (x + 0.044715 * (x * x * x))))


def _softplus(x):
    return jnp.maximum(x, 0.0) + jnp.log(1.0 + jnp.exp(-jnp.abs(x)))


def _rms(x, g):
    return x * lax.rsqrt(jnp.mean(x * x, axis=-1, keepdims=True) + EPS) * g


def _mod_kernel(c_ref, w_ref, b_ref, o_ref):
    s = _silu(c_ref[...])
    o_ref[...] = jnp.dot(s, w_ref[...], preferred_element_type=F32,
                         precision=lax.Precision.HIGHEST) + b_ref[...]


def _modulation(cvec, w_mod, b_mod):
    depth = w_mod.shape[0]
    rows = cvec.shape[0]
    n_col = w_mod.shape[2] // D_MODEL
    return pl.pallas_call(
        _mod_kernel,
        grid=(depth, n_col),
        in_specs=[pl.BlockSpec((rows, D_MODEL), lambda l, j: (0, 0)),
                  pl.BlockSpec((None, D_MODEL, D_MODEL), lambda l, j: (l, 0, j)),
                  pl.BlockSpec((None, 1, D_MODEL), lambda l, j: (l, 0, j))],
        out_specs=pl.BlockSpec((None, rows, D_MODEL), lambda l, j: (l, 0, j)),
        out_shape=jax.ShapeDtypeStruct((depth, rows, w_mod.shape[2]), F32),
        compiler_params=_cparams(("arbitrary", "arbitrary")),
        name="modulation",
    )(cvec, w_mod, b_mod.reshape(depth, 1, -1))


def _inproj_kernel(x_ref, mod_ref, g_ref, wm_ref, ws_ref, main_ref, small_ref, h_ref):
    @pl.when(pl.program_id(1) == 0)
    def _():
        sh = mod_ref[:, 0 * D_MODEL:1 * D_MODEL]
        sc = mod_ref[:, 1 * D_MODEL:2 * D_MODEL]
        h = (_rms(x_ref[...], g_ref[...]) * (1.0 + sc) + sh).astype(BF16)
        h_ref[...] = h
        small_ref[...] = _dot(h, ws_ref[...])

    main_ref[...] = _dot(h_ref[...], wm_ref[...]).astype(BF16)


def _inproj(x, mod, mod_row, norm_g, w_main, w_small, tm, tn=2048):
    n_tok = x.shape[0]
    n_main = w_main.shape[1]
    return pl.pallas_call(
        _inproj_kernel,
        grid=(n_tok // tm, n_main // tn),
        in_specs=[pl.BlockSpec((tm, D_MODEL), lambda i, j: (i, 0)),
                  pl.BlockSpec((None, 1, 6 * D_MODEL), lambda i, j: (mod_row(i * tm), 0, 0)),
                  pl.BlockSpec((1, D_MODEL), lambda i, j: (0, 0)),
                  pl.BlockSpec((D_MODEL, tn), lambda i, j: (0, j)),
                  pl.BlockSpec((D_MODEL, N_SMALL_COLS), lambda i, j: (0, 0))],
        out_specs=[pl.BlockSpec((tm, tn), lambda i, j: (i, j)),
                   pl.BlockSpec((tm, N_SMALL_COLS), lambda i, j: (i, 0))],
        out_shape=[jax.ShapeDtypeStruct((n_tok, n_main), BF16),
                   jax.ShapeDtypeStruct((n_tok, N_SMALL_COLS), F32)],
        scratch_shapes=[pltpu.VMEM((tm, D_MODEL), BF16)],
        compiler_params=_cparams(("arbitrary", "arbitrary")),
        name="inproj",
    )(x, mod, norm_g, w_main, w_small)


def _sgu_kernel(u_ref, v_ref, lng_ref, lnb_ref, ws_ref, bs_ref, o_ref, *, n_chunk):
    u = _gelu_tanh(u_ref[...].astype(F32))
    v = _gelu_tanh(v_ref[...].astype(F32))
    vc = v - jnp.mean(v, axis=-1, keepdims=True)
    var = jnp.mean(vc * vc, axis=-1, keepdims=True)
    vn = (vc * lax.rsqrt(var + EPS) * lng_ref[...] + lnb_ref[...]).astype(BF16)
    for n in range(n_chunk):
        rows = slice(n * SG_CHUNK, (n + 1) * SG_CHUNK)
        for g in range(SG_GROUPS):
            cols = slice(g * LANES, (g + 1) * LANES)
            mixed = _dot(ws_ref[g], vn[rows, cols]) + bs_ref[:, g:g + 1]
            o_ref[rows, cols] = (u[rows, cols] * mixed).astype(BF16)


def _sgu(main, sg_ln_g, sg_ln_b, sg_w, sg_bt, n_chunk=2):
    n_tok = main.shape[0]
    tc = n_chunk * SG_CHUNK
    return pl.pallas_call(
        functools.partial(_sgu_kernel, n_chunk=n_chunk),
        grid=(n_tok // tc,),
        in_specs=[pl.BlockSpec((tc, D_MODEL), lambda i: (i, COL_SG_U)),
                  pl.BlockSpec((tc, D_MODEL), lambda i: (i, COL_SG_V)),
                  pl.BlockSpec((1, D_MODEL), lambda i: (0, 0)),
                  pl.BlockSpec((1, D_MODEL), lambda i: (0, 0)),
                  pl.BlockSpec((SG_GROUPS, SG_CHUNK, SG_CHUNK), lambda i: (0, 0, 0)),
                  pl.BlockSpec((SG_CHUNK, SG_GROUPS), lambda i: (0, 0))],
        out_specs=pl.BlockSpec((tc, D_MODEL), lambda i: (i, 0)),
        out_shape=jax.ShapeDtypeStruct((n_tok, D_MODEL), BF16),
        compiler_params=_cparams(("arbitrary",)),
        name="sgu",
    )(main, main, sg_ln_g, sg_ln_b, sg_w, sg_bt)


def _dn_prep_kernel(*refs, with_q, n_chunks):
    if with_q:
        (qp_ref, qc_ref, qn_ref, kp_ref, kc_ref, kn_ref, vp_ref, vc_ref, vn_ref,
         gcol_ref, grow_ref, cw_ref, alog_r_ref, alog_c_ref, dtb_r_ref, dtb_c_ref,
         w_ref, u0_ref, ke_ref, gt_ref, qs_ref, qk_ref) = refs
    else:
        (kp_ref, kc_ref, kn_ref, vp_ref, vc_ref, vn_ref,
         gcol_ref, grow_ref, cw_ref, alog_r_ref, alog_c_ref, dtb_r_ref, dtb_c_ref,
         w_ref, u0_ref, ke_ref, gt_ref) = refs
    c = pl.program_id(1)
    C = DN_CHUNK
    R = DN_PREP_CHUNKS * C
    has_prev = (c > 0).astype(BF16)
    has_next = (c < n_chunks // DN_PREP_CHUNKS - 1).astype(BF16)

    pad = DN_CONV // 2
    n_sh = DN_CONV - 1
    sr = lax.broadcasted_iota(I32, (n_sh * R, R + 2 * DN_HALO), 0)
    sc = lax.broadcasted_iota(I32, (n_sh * R, R + 2 * DN_HALO), 1)
    blk = sr // R
    off = jnp.where(blk < pad, blk - pad, blk - pad + 1)
    shift_mat = (sc == DN_HALO + (sr - blk * R) + off).astype(BF16)

    def conv_silu(p_ref, c_ref, n_ref, part):
        cur = c_ref[...]
        ext = jnp.concatenate([p_ref[...] * has_prev, cur, n_ref[...] * has_next], axis=0)
        sh = _dot(shift_mat, ext)
        taps = [sh[j * R:(j + 1) * R] for j in range(pad)] + [cur.astype(F32)] + \
               [sh[j * R:(j + 1) * R] for j in range(pad, n_sh)]
        y = None
        for i in range(DN_CONV):
            t = taps[i] * cw_ref[i:i + 1, part * D_MODEL:(part + 1) * D_MODEL]
            y = t if y is None else y + t
        return _silu(y)

    k_all = conv_silu(kp_ref, kc_ref, kn_ref, 1)
    v_all = conv_silu(vp_ref, vc_ref, vn_ref, 2)
    q_all = conv_silu(qp_ref, qc_ref, qn_ref, 0) if with_q else None

    ri = lax.broadcasted_iota(I32, (C, C), 0)
    ci = lax.broadcasted_iota(I32, (C, C), 1)
    eye = (ri == ci).astype(F32)
    CC = range(DN_PREP_CHUNKS)
    H = range(DN_HEADS)
    CH = [(cc, h) for cc in CC for h in H]
    ch = {key: i for i, key in enumerate(CH)}
    rows = [slice(cc * C, (cc + 1) * C) for cc in CC]
    lanes = [slice(h * DN_DIM, (h + 1) * DN_DIM) for h in H]
    kh = [k_all[rows[cc], lanes[h]] for cc, h in CH]
    kh = [k * lax.rsqrt(jnp.sum(k * k, axis=-1, keepdims=True) + EPS) for k in kh]
    kb = [k.astype(BF16) for k in kh]
    vh = [v_all[rows[cc], lanes[h]] for cc, h in CH]
    if with_q:
        qh = [q_all[rows[cc], lanes[h]] for cc, h in CH]
        qh = [q * (lax.rsqrt(jnp.sum(q * q, axis=-1, keepdims=True) + EPS) * DN_DIM ** -0.5) for q in qh]
        gram = [_dot_nt(jnp.concatenate([kb[j], qh[j].astype(BF16)], axis=0), kb[j]) for j in range(len(CH))]
        kk = [g[:C] for g in gram]
        qk_raw = [g[C:] for g in gram]
    else:
        kk = [_dot_nt(k, k) for k in kb]

    D2 = range(2)
    DH = [(cc, d, h) for cc in CC for d in D2 for h in H]
    incl = [(ri >= ci), (ri <= ci)]
    strict = [(ri > ci), (ri < ci)]
    gam_col, gam_row, gam_tot, beta_col = {}, {}, {}, {}
    for cc in CC:
        for d in D2:
            gcol = gcol_ref[d, rows[cc], :]
            ld_col = -jnp.exp(alog_r_ref[d]) * _softplus(gcol[:, 0:DN_HEADS] + dtb_r_ref[d])
            ld_row = -jnp.exp(alog_c_ref[d]) * _softplus(grow_ref[d, cc][0:DN_HEADS, :] + dtb_c_ref[d])
            beta_col[cc, d] = _sigmoid(gcol[:, DN_HEADS:2 * DN_HEADS])
            gam_col[cc, d] = jnp.dot(incl[d].astype(F32), ld_col, preferred_element_type=F32,
                                     precision=lax.Precision.HIGHEST)
            gam_row[cc, d] = jnp.dot(ld_row, incl[1 - d].astype(F32), preferred_element_type=F32,
                                     precision=lax.Precision.HIGHEST)
            tot = jnp.sum(ld_col, axis=0, keepdims=True)
            gam_tot[cc, d] = tot
            gt_ref[d, cc] = tot
    gc = [gam_col[cc, d][:, h:h + 1] for cc, d, h in DH]
    bc = [beta_col[cc, d][:, h:h + 1] for cc, d, h in DH]
    decay = [jnp.exp(jnp.where(incl[d], gc[i] - gam_row[cc, d][h:h + 1, :], NEG_BIG))
             for i, (cc, d, h) in enumerate(DH)]
    x = [-(jnp.where(strict[d], decay[i], 0.0) * bc[i] * kk[ch[cc, h]]) for i, (cc, d, h) in enumerate(DH)]
    N = range(len(DH))
    p = [eye + x[i] for i in N]
    xb = [x[i].astype(BF16) for i in N]
    x = [_dot(xb[i], xb[i]) for i in N]
    n_fac = int(math.log2(C)) - 1
    for j in range(n_fac):
        xb = [x[i].astype(BF16) for i in N]
        if j < n_fac - 1:
            r = [_dot(xb[i], jnp.concatenate([xb[i], p[i].astype(BF16)], axis=1)) for i in N]
            x = [r[i][:, :C] for i in N]
            p = [p[i] + r[i][:, C:] for i in N]
        else:
            p = [p[i] + _dot(xb[i], p[i].astype(BF16)) for i in N]
    rhs = [jnp.concatenate([kh[ch[cc, h]] * (bc[i] * jnp.exp(gc[i])), vh[ch[cc, h]] * bc[i]],
                           axis=1).astype(BF16) for i, (cc, d, h) in enumerate(DH)]
    sol = [_dot(p[i].astype(BF16), rhs[i]) for i in N]
    for i, (cc, d, h) in enumerate(DH):
        j = ch[cc, h]
        w_ref[d, rows[cc], lanes[h]] = sol[i][:, :DN_DIM].astype(BF16)
        u0_ref[d, rows[cc], lanes[h]] = sol[i][:, DN_DIM:].astype(BF16)
        ke_ref[d, rows[cc], lanes[h]] = (kh[j] * jnp.exp(gam_tot[cc, d][:, h:h + 1] - gc[i])).astype(BF16)
        if with_q:
            qs_ref[d, rows[cc], lanes[h]] = (qh[j] * jnp.exp(gc[i])).astype(BF16)
            qk_ref[d, rows[cc], h * C:(h + 1) * C] = (qk_raw[j] * decay[i]).astype(BF16)


def _dn_scan_kernel(*refs, with_q, n_chunks):
    n_state = 2 * DN_HEADS
    s_refs = refs[-n_state:]
    refs = refs[:-n_state]
    if with_q:
        (w0, w1, u0, u1, k0, k1, g0, g1, qs0, qs1, qk0, qk1, s0_ref, o0_ref, o1_ref, sfin_ref) = refs
        qs_r, qk_r, o_r = (qs0, qs1), (qk0, qk1), (o0_ref, o1_ref)
    else:
        (w0, w1, u0, u1, k0, k1, g0, g1, s0_ref, sfin_ref) = refs
    w_r, u_r, k_r, g_r = (w0, w1), (u0, u1), (k0, k1), (g0, g1)
    c = pl.program_id(1)
    C = DN_CHUNK
    DH = [(d, h) for d in range(2) for h in range(DN_HEADS)]
    N = range(len(DH))
    lanes = [slice(h * DN_DIM, (h + 1) * DN_DIM) for h in range(DN_HEADS)]

    @pl.when(c == 0)
    def _():
        for i, (d, h) in enumerate(DH):
            s_refs[i][...] = s0_ref[d, h]

    s = [s_refs[i][...] for i in N]
    for sub in range(DN_SCAN_CHUNKS):
        cix = (sub, DN_SCAN_CHUNKS - 1 - sub)
        rows = [slice(cix[d] * C, (cix[d] + 1) * C) for d in range(2)]
        sb = [s[i].astype(BF16) for i in N]
        w = [w_r[d][rows[d], lanes[h]] for d, h in DH]
        if with_q:
            wq = [jnp.concatenate([w[i], qs_r[d][rows[d], lanes[h]]], axis=0) for i, (d, h) in enumerate(DH)]
            ws = [_dot(wq[i], sb[i]) for i in N]
            ub = [(u_r[d][rows[d], lanes[h]].astype(F32) - ws[i][:C]).astype(BF16)
                  for i, (d, h) in enumerate(DH)]
            qu = [_dot(qk_r[d][rows[d], h * C:(h + 1) * C], ub[i]) for i, (d, h) in enumerate(DH)]
            for i, (d, h) in enumerate(DH):
                o_r[d][rows[d], lanes[h]] = (ws[i][C:] + qu[i]).astype(BF16)
        else:
            ws = [_dot(w[i], sb[i]) for i in N]
            ub = [(u_r[d][rows[d], lanes[h]].astype(F32) - ws[i]).astype(BF16) for i, (d, h) in enumerate(DH)]
        ku = [_dot_tn(k_r[d][rows[d], lanes[h]], ub[i]) for i, (d, h) in enumerate(DH)]
        s = [jnp.exp(g_r[d][cix[d]][:, h:h + 1]) * s[i] + ku[i] for i, (d, h) in enumerate(DH)]
    for i in N:
        s_refs[i][...] = s[i]

    @pl.when(c == n_chunks // DN_SCAN_CHUNKS - 1)
    def _():
        for i, (d, h) in enumerate(DH):
            sfin_ref[d, h] = s_refs[i][...]


def _deltanet(main3, gate_col, gate_row, conv_w, alog, dtb, s0, with_q):
    B, T, _ = main3.shape
    C = DN_CHUNK
    n_chunks = T // C
    NC = DN_PREP_CHUNKS
    R = NC * C
    assert T % R == 0
    hpc = R // DN_HALO
    n_halo = T // DN_HALO

    def trio(col):
        return [pl.BlockSpec((None, DN_HALO, D_MODEL), lambda b, c: (b, jnp.maximum(c * hpc - 1, 0), col)),
                pl.BlockSpec((None, R, D_MODEL), lambda b, c: (b, c, col)),
                pl.BlockSpec((None, DN_HALO, D_MODEL),
                             lambda b, c: (b, jnp.minimum((c + 1) * hpc, n_halo - 1), col))]

    vec = lambda shape: pl.BlockSpec(shape, lambda b, c: (0,) * len(shape))
    in_specs = (trio(COL_DN_Q) if with_q else []) + trio(COL_DN_K) + trio(COL_DN_V) + [
        pl.BlockSpec((2, None, R, 2 * DN_HEADS), lambda b, c: (0, b, c, 0)),
        pl.BlockSpec((2, None, NC, 2 * DN_HEADS, C), lambda b, c: (0, b, c, 0, 0)),
        vec((DN_CONV, 3 * D_MODEL)),
        vec((2, 1, DN_HEADS)), vec((2, DN_HEADS, 1)), vec((2, 1, DN_HEADS)), vec((2, DN_HEADS, 1)),
    ]
    wide = lambda n: (pl.BlockSpec((2, None, R, n), lambda b, c: (0, b, c, 0)),
                      jax.ShapeDtypeStruct((2, B, T, n), BF16))
    outs = [wide(D_MODEL), wide(D_MODEL), wide(D_MODEL),
            (pl.BlockSpec((2, None, NC, 1, DN_HEADS), lambda b, c: (0, b, c, 0, 0)),
             jax.ShapeDtypeStruct((2, B, n_chunks, 1, DN_HEADS), F32))]
    if with_q:
        outs += [wide(D_MODEL), wide(DN_HEADS * C)]
    n_main = 3 if with_q else 2
    prep = pl.pallas_call(
        functools.partial(_dn_prep_kernel, with_q=with_q, n_chunks=n_chunks),
        grid=(B, n_chunks // NC),
        in_specs=in_specs, out_specs=[o[0] for o in outs], out_shape=[o[1] for o in outs],
        compiler_params=_cparams(("arbitrary", "arbitrary")),
        name="dn_prep_q" if with_q else "dn_prep",
    )(*([main3] * (3 * n_main)), gate_col, gate_row, conv_w,
      alog.reshape(2, 1, DN_HEADS), alog.reshape(2, DN_HEADS, 1),
      dtb.reshape(2, 1, DN_HEADS), dtb.reshape(2, DN_HEADS, 1))

    NS = DN_SCAN_CHUNKS
    RS = NS * C
    n_steps = n_chunks // NS
    assert n_chunks % NS == 0

    def both_dirs(arr, n):
        if n is None:
            return [pl.BlockSpec((None, None, NS, 1, DN_HEADS), lambda b, c: (0, b, c, 0, 0)),
                    pl.BlockSpec((None, None, NS, 1, DN_HEADS),
                                 lambda b, c: (1, b, n_steps - 1 - c, 0, 0))], [arr, arr]
        return [pl.BlockSpec((None, None, RS, n), lambda b, c: (0, b, c, 0)),
                pl.BlockSpec((None, None, RS, n), lambda b, c: (1, b, n_steps - 1 - c, 0))], [arr, arr]

    specs, args = [], []
    widths = [D_MODEL, D_MODEL, D_MODEL, None] + ([D_MODEL, DN_HEADS * C] if with_q else [])
    for arr, n in zip(prep, widths):
        sp, ar = both_dirs(arr, n)
        specs += sp
        args += ar
    s_spec = pl.BlockSpec((2, None, DN_HEADS, DN_DIM, DN_DIM), lambda b, c: (0, b, 0, 0, 0))
    s_shape = jax.ShapeDtypeStruct((2, B, DN_HEADS, DN_DIM, DN_DIM), F32)
    if with_q:
        out_specs = [pl.BlockSpec((None, RS, D_MODEL), lambda b, c: (b, c, 0)),
                     pl.BlockSpec((None, RS, D_MODEL), lambda b, c: (b, n_steps - 1 - c, 0)), s_spec]
        out_shape = [jax.ShapeDtypeStruct((B, T, D_MODEL), BF16)] * 2 + [s_shape]
    else:
        out_specs, out_shape = [s_spec], [s_shape]
    out = pl.pallas_call(
        functools.partial(_dn_scan_kernel, with_q=with_q, n_chunks=n_chunks),
        grid=(B, n_steps),
        in_specs=specs + [s_spec], out_specs=out_specs, out_shape=out_shape,
        scratch_shapes=[pltpu.VMEM((DN_DIM, DN_DIM), F32)] * (2 * DN_HEADS),
        compiler_params=_cparams(("arbitrary", "arbitrary")),
        name="dn_scan_q" if with_q else "dn_scan",
    )(*args, s0)
    return (out[0], out[1], out[2]) if with_q else (None, None, out[0])


def _rope_tables(S):
    half = AT_DIM // 2
    nf = half // 2
    inv_freq = ROPE_BASE ** (-jnp.arange(nf, dtype=F32) / nf)
    t = jnp.arange(S, dtype=jnp.int32)
    row = (t // GRID_W).astype(F32)
    col = (t % GRID_W).astype(F32)
    lane = jnp.arange(LANES)
    dd = lane % AT_DIM
    pos = jnp.where((dd < half)[None, :], row[:, None], col[:, None])
    ang = pos * inv_freq[lane % nf][None, :]
    first = ((lane % half) < nf)[None, :]
    sin = jnp.sin(ang)
    return jnp.cos(ang), jnp.where(first, -sin, 0.0), jnp.where(first, 0.0, sin)


def _rope_kernel(q_ref, k_ref, v_ref, cos_ref, sa_ref, sb_ref, qo_ref, ko_ref, vo_ref):
    cos, sa, sb = cos_ref[...], sa_ref[...], sb_ref[...]
    nf = AT_DIM // 4

    def rot(x):
        return x * cos + pltpu.roll(x, LANES - nf, 1) * sa + pltpu.roll(x, nf, 1) * sb

    for j in range(AT_Q_HEADS * AT_DIM // LANES):
        lanes = slice(j * LANES, (j + 1) * LANES)
        qo_ref[:, lanes] = (rot(q_ref[:, lanes].astype(F32)) * AT_DIM ** -0.5).astype(BF16)
    ko_ref[...] = rot(k_ref[...]).astype(BF16)
    vo_ref[...] = v_ref[...].astype(BF16)


def _rope(main, small, tables, S, tm=512):
    n_tok = main.shape[0]
    per_seq = S // tm
    tab_spec = pl.BlockSpec((tm, LANES), lambda i: (i % per_seq, 0))
    return pl.pallas_call(
        _rope_kernel,
        grid=(n_tok // tm,),
        in_specs=[pl.BlockSpec((tm, D_MODEL), lambda i: (i, COL_AT_Q)),
                  pl.BlockSpec((tm, LANES), lambda i: (i, 0)),
                  pl.BlockSpec((tm, LANES), lambda i: (i, 1)),
                  tab_spec, tab_spec, tab_spec],
        out_specs=[pl.BlockSpec((tm, D_MODEL), lambda i: (i, 0)),
                   pl.BlockSpec((tm, LANES), lambda i: (i, 0)),
                   pl.BlockSpec((tm, LANES), lambda i: (i, 0))],
        out_shape=[jax.ShapeDtypeStruct((n_tok, D_MODEL), BF16),
                   jax.ShapeDtypeStruct((n_tok, LANES), BF16),
                   jax.ShapeDtypeStruct((n_tok, LANES), BF16)],
        compiler_params=_cparams(("arbitrary",)),
        name="rope",
    )(main, small, small, *tables)


def _attn_kernel(*refs, local, n_blocks, q_scale):
    if local:
        (q_ref, kp_ref, kc_ref, kn_ref, vp_ref, vc_ref, vn_ref, kx_ref, vx_ref, sink_ref, o_ref) = refs
    else:
        (q_ref, kx_ref, vx_ref, sink_ref, o_ref) = refs
    P = AT_BLOCK
    G = AT_Q_HEADS // AT_KV_HEADS
    kx = kx_ref[...].astype(BF16)
    vx = vx_ref[...].astype(BF16)
    if local:
        i = pl.program_id(1)
        k_all = jnp.concatenate([kp_ref[...], kc_ref[...], kn_ref[...], kx], axis=0)
        v_all = jnp.concatenate([vp_ref[...], vc_ref[...], vn_ref[...], vx], axis=0)
        qi = lax.broadcasted_iota(I32, (P, P), 0)
        kj = lax.broadcasted_iota(I32, (P, P), 1)
        b_prev = jnp.where(kj >= qi, 0.0, NEG_BIG) + jnp.where(i > 0, 0.0, NEG_BIG)
        b_next = jnp.where(kj <= qi, 0.0, NEG_BIG) + jnp.where(i < n_blocks - 1, 0.0, NEG_BIG)
        b_prev2 = jnp.concatenate([b_prev] * AT_STAGE_HEADS, axis=0)
        b_next2 = jnp.concatenate([b_next] * AT_STAGE_HEADS, axis=0)
    else:
        k_all, v_all = kx, vx
    lo = lax.broadcasted_iota(I32, (P, LANES), 1) < AT_DIM
    qf = q_ref[...].astype(F32) * q_scale
    pieces = []
    for qh in range(AT_Q_HEADS):
        blk = qf[:, (qh // 2) * LANES:(qh // 2 + 1) * LANES]
        want_lo = qh // G == 0
        if want_lo != (qh % 2 == 0):
            blk = pltpu.roll(blk, AT_DIM, 1)
        pieces.append(jnp.where(lo if want_lo else ~lo, blk, 0.0).astype(BF16))
    HS = AT_STAGE_HEADS
    n_pair = AT_Q_HEADS // HS

    def logits(j):
        s = _dot_nt(jnp.concatenate(pieces[HS * j:HS * (j + 1)], axis=0), k_all)
        if not local:
            return s
        return jnp.concatenate([s[:, 0:P] + b_prev2, s[:, P:2 * P], s[:, 2 * P:3 * P] + b_next2,
                                s[:, 3 * P:]], axis=1)

    def softmax(j, s):
        sink = sink_ref[HS * j * P:HS * (j + 1) * P, :]
        m = jnp.maximum(jnp.max(s, axis=-1, keepdims=True), sink)
        p = jnp.exp(s - m)
        den = jnp.sum(p, axis=-1, keepdims=True) + jnp.exp(sink - m)
        return p.astype(BF16), den

    def values(j, p, den):
        o = _dot(p, v_all) / den
        for t in range(HS // 2):
            blk = (HS * j) // 2 + t
            a, b = o[2 * t * P:(2 * t + 1) * P], o[(2 * t + 1) * P:(2 * t + 2) * P]
            if (2 * blk) // G == 0:
                out = jnp.where(lo, a, pltpu.roll(b, AT_DIM, 1))
            else:
                out = jnp.where(lo, pltpu.roll(a, AT_DIM, 1), b)
            o_ref[:, blk * LANES:(blk + 1) * LANES] = out.astype(BF16)

    s_next = logits(0)
    prob = None
    for j in range(n_pair):
        s_cur = s_next
        if j + 1 < n_pair:
            s_next = logits(j + 1)
        done = prob
        prob = softmax(j, s_cur)
        if done is not None:
            values(j - 1, *done)
    values(n_pair - 1, *prob)


def _attention_local(q_r, k_r, v_r, small_c, sinks, B, S, L):
    P = AT_BLOCK
    nb = S // P

    def kv_trio():
        return [pl.BlockSpec((P, LANES), lambda b, i: (b * nb + jnp.maximum(i - 1, 0), 0)),
                pl.BlockSpec((P, LANES), lambda b, i: (b * nb + i, 0)),
                pl.BlockSpec((P, LANES), lambda b, i: (b * nb + jnp.minimum(i + 1, nb - 1), 0))]

    return pl.pallas_call(
        functools.partial(_attn_kernel, local=True, n_blocks=nb, q_scale=1.0),
        grid=(B, nb),
        in_specs=[pl.BlockSpec((P, D_MODEL), lambda b, i: (b * nb + i, 0))] + kv_trio() + kv_trio() + [
            pl.BlockSpec((L, LANES), lambda b, i: (b, 0)),
            pl.BlockSpec((L, LANES), lambda b, i: (b, 1)),
            pl.BlockSpec((AT_Q_HEADS * AT_BLOCK, 1), lambda b, i: (0, 0))],
        out_specs=pl.BlockSpec((P, D_MODEL), lambda b, i: (b * nb + i, 0)),
        out_shape=jax.ShapeDtypeStruct((B * S, D_MODEL), BF16),
        compiler_params=_cparams(("arbitrary", "arbitrary")),
        name="attn_local",
    )(q_r, k_r, k_r, k_r, v_r, v_r, v_r, small_c, small_c, sinks)


def _attention_ctx(main_c, small_c, sinks, B, L):
    P = AT_BLOCK
    nb = L // P
    return pl.pallas_call(
        functools.partial(_attn_kernel, local=False, n_blocks=nb, q_scale=AT_DIM ** -0.5),
        grid=(B, nb),
        in_specs=[pl.BlockSpec((P, D_MODEL), lambda b, i: (b * nb + i, COL_AT_Q)),
                  pl.BlockSpec((L, LANES), lambda b, i: (b, 0)),
                  pl.BlockSpec((L, LANES), lambda b, i: (b, 1)),
                  pl.BlockSpec((AT_Q_HEADS * AT_BLOCK, 1), lambda b, i: (0, 0))],
        out_specs=pl.BlockSpec((P, D_MODEL), lambda b, i: (b * nb + i, 0)),
        out_shape=jax.ShapeDtypeStruct((B * L, D_MODEL), BF16),
        compiler_params=_cparams(("arbitrary", "arbitrary")),
        name="attn_ctx",
    )(main_c, small_c, small_c, sinks)


def _merge_kernel(ysg_ref, of_ref, ob_ref, dng_ref, yat_ref, g0_ref, g1_ref, g2_ref, x_ref, mod_ref,
                  dn_norm_ref, post_ref, pre_ref, wsg_ref, wdn_ref, wat_ref, wout_ref, rw_ref, rb_ref,
                  xo_ref, h2_ref, lg_ref):
    dn_g = dn_norm_ref[...]
    gate1 = mod_ref[:, 2 * D_MODEL:3 * D_MODEL]
    sh2 = mod_ref[:, 3 * D_MODEL:4 * D_MODEL]
    sc2 = mod_ref[:, 4 * D_MODEL:5 * D_MODEL]
    n_grp = MERGE_ROW_GROUPS
    grp = x_ref.shape[0] // n_grp

    def dn_out(r):
        rows = slice(r * grp, (r + 1) * grp)
        o = of_ref[rows, :].astype(F32) + ob_ref[rows, :].astype(F32)
        parts = []
        for h in range(DN_HEADS):
            lanes = slice(h * DN_DIM, (h + 1) * DN_DIM)
            parts.append(_rms(o[:, lanes], dn_g) * _silu(dng_ref[rows, lanes].astype(F32)))
        return jnp.concatenate(parts, axis=1).astype(BF16)

    def branches(r, ydn):
        rows = slice(r * grp, (r + 1) * grp)
        m = (_sigmoid(g0_ref[rows, :].astype(F32)) * _dot(ysg_ref[rows, :], wsg_ref[...])
             + _sigmoid(g1_ref[rows, :].astype(F32)) * _dot(ydn, wdn_ref[...])
             + _sigmoid(g2_ref[rows, :].astype(F32)) * _dot(yat_ref[rows, :], wat_ref[...]))
        return m.astype(BF16)

    def out_proj(r, m):
        return _dot(m, wout_ref[...])

    def residual(r, y):
        rows = slice(r * grp, (r + 1) * grp)
        xn = x_ref[rows, :] + gate1 * _rms(y, post_ref[...])
        xo_ref[rows, :] = xn
        h2 = _rms(xn, pre_ref[...]) * (1.0 + sc2) + sh2
        h2_ref[rows, :] = h2
        lg_ref[rows, :] = _dot(h2.astype(BF16), rw_ref[...]) + rb_ref[...]

    stages = (dn_out, branches, out_proj, residual)
    carried = {}
    for t in range(n_grp + len(stages) - 1):
        for k in reversed(range(len(stages))):
            r = t - k
            if 0 <= r < n_grp:
                carried[r] = stages[k](r) if k == 0 else stages[k](r, carried[r])


def _merge(ysg, o_fwd, o_bwd, main, yat, x, mod, mod_row, lw, tm=512):
    n_tok = x.shape[0]
    const = lambda i: (0, 0)
    wspec = pl.BlockSpec((D_MODEL, D_MODEL), const, pipeline_mode=pl.Buffered(1))
    vspec = pl.BlockSpec((1, D_MODEL), const)
    return pl.pallas_call(
        _merge_kernel,
        grid=(n_tok // tm,),
        in_specs=[pl.BlockSpec((tm, D_MODEL), lambda i: (i, 0)),
                  pl.BlockSpec((tm, D_MODEL), lambda i: (i, 0)),
                  pl.BlockSpec((tm, D_MODEL), lambda i: (i, 0)),
                  pl.BlockSpec((tm, D_MODEL), lambda i: (i, COL_DN_G)),
                  pl.BlockSpec((tm, D_MODEL), lambda i: (i, 0)),
                  pl.BlockSpec((tm, D_MODEL), lambda i: (i, COL_GATE0)),
                  pl.BlockSpec((tm, D_MODEL), lambda i: (i, COL_GATE0 + 1)),
                  pl.BlockSpec((tm, D_MODEL), lambda i: (i, COL_GATE0 + 2)),
                  pl.BlockSpec((tm, D_MODEL), lambda i: (i, 0)),
                  pl.BlockSpec((None, 1, 6 * D_MODEL), lambda i: (mod_row(i * tm), 0, 0)),
                  pl.BlockSpec((1, DN_DIM), const), vspec, vspec,
                  wspec, wspec, wspec, wspec,
                  pl.BlockSpec((D_MODEL, LANES), const), pl.BlockSpec((1, LANES), const)],
        out_specs=[pl.BlockSpec((tm, D_MODEL), lambda i: (i, 0)),
                   pl.BlockSpec((tm, D_MODEL), lambda i: (i, 0)),
                   pl.BlockSpec((tm, LANES), lambda i: (i, 0))],
        out_shape=[jax.ShapeDtypeStruct((n_tok, D_MODEL), F32),
                   jax.ShapeDtypeStruct((n_tok, D_MODEL), F32),
                   jax.ShapeDtypeStruct((n_tok, LANES), F32)],
        compiler_params=_cparams(("arbitrary",)),
        name="merge",
    )(ysg, o_fwd, o_bwd, main, yat, main, main, main, x, mod,
      lw["dn_norm_g"], lw["norm_post_mix"], lw["norm_pre_ffn"],
      lw["w_proj_sg"], lw["w_proj_dn"], lw["w_proj_at"], lw["w_out"], lw["router_w"], lw["router_b"])


MOE_TOK = 256
MOE_PIECE = 8
MOE_BUF = MOE_TOK * TOP_K + N_EXPERTS * MOE_PIECE
MOE_META = 256
assert MOE_META > MOE_BUF // MOE_PIECE


def _route_kernel(lg_ref, gate_ref, lpos_ref, tcnt_ref):
    tm = lg_ref.shape[0]
    l = lg_ref[...]
    lane = lax.broadcasted_iota(I32, l.shape, 1).astype(F32)
    vals, onehots = [], []
    for k in range(TOP_K):
        m = jnp.max(l, axis=-1, keepdims=True)
        ik = jnp.min(jnp.where(l == m, lane, float(LANES)), axis=-1, keepdims=True)
        oh = lane == ik
        vals.append(m)
        onehots.append(oh)
        l = jnp.where(oh, -jnp.inf, l)
    es = [jnp.exp(v - vals[0]) for v in vals]
    den = es[0] + es[1] + es[2] + es[3]
    sel = jnp.zeros(l.shape, F32)
    for k in range(TOP_K):
        gate_ref[:, k:k + 1] = es[k] / den
        sel = sel + onehots[k].astype(F32)
    ri = lax.broadcasted_iota(I32, (tm, tm), 0)
    ci = lax.broadcasted_iota(I32, (tm, tm), 1)
    before = _dot((ri > ci).astype(BF16), sel.astype(BF16))
    tcnt = jnp.sum(sel, axis=0, keepdims=True)
    tcnt_ref[...] = tcnt
    n_piece = jnp.floor((tcnt + (MOE_PIECE - 1)) * (1.0 / MOE_PIECE))
    ei = lax.broadcasted_iota(I32, (LANES, LANES), 0)
    ej = lax.broadcasted_iota(I32, (LANES, LANES), 1)
    run_start = _dot(jnp.broadcast_to(n_piece, (8, LANES)).astype(BF16),
                     (ei < ej).astype(BF16))[0:1] * float(MOE_PIECE)
    pos = before + run_start
    for k in range(TOP_K):
        lpos_ref[:, k:k + 1] = jnp.sum(jnp.where(onehots[k], pos, 0.0), axis=-1,
                                       keepdims=True).astype(I32)


def _route(logits):
    n_tok = logits.shape[0]
    tm = MOE_TOK
    n_t = n_tok // tm
    small = lambda dt: jax.ShapeDtypeStruct((n_tok, TOP_K), dt)
    kspec = pl.BlockSpec((tm, TOP_K), lambda i: (i, 0))
    tspec = pl.BlockSpec((None, 1, LANES), lambda i: (i, 0, 0))
    tshape = jax.ShapeDtypeStruct((n_t, 1, LANES), F32)
    return pl.pallas_call(
        _route_kernel,
        grid=(n_t,),
        in_specs=[pl.BlockSpec((tm, LANES), lambda i: (i, 0))],
        out_specs=[kspec, kspec, tspec],
        out_shape=[small(F32), small(I32), tshape],
        compiler_params=_cparams(("arbitrary",)),
        name="route",
    )(logits)


def _run_copies(meta_ref, base, src_of, dst_of, sem, start):
    def per_piece(q, carry):
        local = pl.multiple_of(q * MOE_PIECE, MOE_PIECE)
        slot = pl.multiple_of(meta_ref[base + 1 + q], MOE_PIECE)
        cp = pltpu.make_async_copy(src_of(local, slot), dst_of(local, slot), sem)
        if start:
            cp.start()
        else:
            cp.wait()
        return carry

    lax.fori_loop(0, meta_ref[base], per_piece, 0)


def _dispatch_kernel(meta_ref, prev_ref, zmeta_ref, lpos_ref, h_ref, xs_ref, buf_ref, buf1_ref, zero_ref,
                     sem, sem1, *, tm_e):
    tm = MOE_TOK
    step = pl.program_id(0)
    bufs, sems = (buf_ref, buf1_ref), (sem, sem1)
    rows = lambda ref, r: ref.at[pl.ds(r, MOE_PIECE)]

    @pl.when(step == 0)
    def _():
        zero_ref[...] = jnp.zeros_like(zero_ref)

        def zero_tail(start):
            def per_expert(e, carry):
                z0 = pl.multiple_of(zmeta_ref[e], MOE_PIECE)

                def per_piece(p, c2):
                    cp = pltpu.make_async_copy(zero_ref, rows(xs_ref, z0 + p * MOE_PIECE), sem)
                    if start:
                        cp.start()
                    else:
                        cp.wait()
                    return c2

                return lax.fori_loop(0, zmeta_ref[N_EXPERTS + e], per_piece, carry)

            lax.fori_loop(0, N_EXPERTS, per_expert, 0)

        zero_tail(True)
        zero_tail(False)

        buf_ref[0:tm_e, :] = jnp.zeros((tm_e, D_MODEL // 2), U32)

        def zero_tiles(start):
            def per_tile(p, carry):
                t0 = pl.multiple_of((zmeta_ref[2 * N_EXPERTS] + p) * tm_e, tm_e)
                cp = pltpu.make_async_copy(buf_ref.at[pl.ds(0, tm_e)], xs_ref.at[pl.ds(t0, tm_e)], sem)
                if start:
                    cp.start()
                else:
                    cp.wait()
                return carry

            lax.fori_loop(0, zmeta_ref[2 * N_EXPERTS + 1], per_tile, 0)

        zero_tiles(True)
        zero_tiles(False)

    def group(j):
        s_iota = lax.broadcasted_iota(I32, (MOE_BUF, tm), 0)
        perm = jnp.zeros((MOE_BUF, tm), F32)
        for k in range(TOP_K):
            perm = jnp.where(s_iota == lpos_ref[k:k + 1, j * tm:(j + 1) * tm], 1.0, perm)
        bufs[j][...] = _pack_bf16_pairs(_dot(perm.astype(BF16), h_ref[j * tm:(j + 1) * tm, :].astype(BF16)))

    def copies(mref, j, start):
        _run_copies(mref, j * MOE_META, lambda loc, slot: rows(bufs[j], loc),
                    lambda loc, slot: rows(xs_ref, slot), sems[j], start)

    group(0)
    copies(meta_ref, 0, True)

    @pl.when(step > 0)
    def _():
        copies(prev_ref, 1, False)

    group(1)
    copies(meta_ref, 1, True)
    copies(meta_ref, 0, False)

    @pl.when(step == pl.num_programs(0) - 1)
    def _():
        copies(meta_ref, 1, False)


def _dispatch(meta, zmeta, lpos_t, h2, n_slots, tm_e):
    n_tok = h2.shape[0]
    tm = 2 * MOE_TOK
    assert tm_e <= MOE_BUF and n_tok % tm == 0
    return pl.pallas_call(
        functools.partial(_dispatch_kernel, tm_e=tm_e),
        grid=(n_tok // tm,),
        in_specs=[pl.BlockSpec((2 * MOE_META,), lambda i: (i,), memory_space=pltpu.SMEM),
                  pl.BlockSpec((2 * MOE_META,), lambda i: (jnp.maximum(i - 1, 0),), memory_space=pltpu.SMEM),
                  pl.BlockSpec((MOE_META,), lambda i: (0,), memory_space=pltpu.SMEM),
                  pl.BlockSpec((TOP_K, tm), lambda i: (0, i)),
                  pl.BlockSpec((tm, D_MODEL), lambda i: (i, 0))],
        out_specs=pl.BlockSpec(memory_space=pl.ANY),
        out_shape=jax.ShapeDtypeStruct((n_slots, D_MODEL // 2), U32),
        scratch_shapes=[pltpu.VMEM((MOE_BUF, D_MODEL // 2), U32), pltpu.VMEM((MOE_BUF, D_MODEL // 2), U32),
                        pltpu.VMEM((MOE_PIECE, D_MODEL // 2), U32),
                        pltpu.SemaphoreType.DMA, pltpu.SemaphoreType.DMA],
        compiler_params=_cparams(("arbitrary",)),
        name="moe_dispatch",
    )(meta, meta, zmeta, lpos_t, h2)


def _pack_bf16_pairs(x):
    w = x.shape[1] // 2
    xb = x.astype(BF16).astype(F32)
    lo = lax.shift_right_logical(lax.bitcast_convert_type(xb[:, :w], U32), jnp.uint32(16))
    hi = lax.bitcast_convert_type(xb[:, w:], U32) & jnp.uint32(0xFFFF0000)
    return hi | lo


def _unpack_bf16_pairs(p):
    lo = lax.bitcast_convert_type(lax.shift_left(p, jnp.uint32(16)), F32)
    hi = lax.bitcast_convert_type(p & jnp.uint32(0xFFFF0000), F32)
    return jnp.concatenate([lo, hi], axis=1).astype(BF16)


def _expert_kernel(te_ref, first_ref, nu_ref, xs_ref, wgu_ref, bgu_ref, wd_ref, bd_ref, y_ref,
                   wgu_b_ref, wd_b_ref):
    del te_ref
    i = pl.program_id(0)

    @pl.when(first_ref[i] == 1)
    def _():
        wgu_b_ref[...] = wgu_ref[...].astype(BF16)
        wd_b_ref[...] = wd_ref[...].astype(BF16)

    @pl.when(i < nu_ref[0])
    def _():
        gu = _dot(_unpack_bf16_pairs(xs_ref[...]), wgu_b_ref[...]) + bgu_ref[...]
        g = jnp.minimum(gu[:, :D_EXPERT], SWIGLU_LIMIT)
        lin = jnp.clip(gu[:, D_EXPERT:], -SWIGLU_LIMIT, SWIGLU_LIMIT)
        act = g * _sigmoid(SWIGLU_ALPHA * g) * (lin + 1.0)
        y_ref[...] = _pack_bf16_pairs(_dot(act.astype(BF16), wd_b_ref[...]) + bd_ref[...])

    @pl.when(i >= nu_ref[0])
    def _():
        y_ref[...] = jnp.zeros_like(y_ref)


def _experts(tile_expert, n_used, xs, wgu, bgu, wd, bd, layer, tm):
    n_slots = xs.shape[0]
    n_tiles = n_slots // tm
    first = jnp.concatenate([jnp.ones((1,), I32),
                             (tile_expert[1:] != tile_expert[:-1]).astype(I32)])

    def row(i, te, fi, nu):
        return (jnp.minimum(i, nu[0] - 1), 0)

    grid_spec = pltpu.PrefetchScalarGridSpec(
        num_scalar_prefetch=3,
        grid=(n_tiles,),
        in_specs=[pl.BlockSpec((tm, D_MODEL // 2), row),
                  pl.BlockSpec((None, None, D_MODEL, 2 * D_EXPERT), lambda i, te, fi, nu: (layer, te[i], 0, 0)),
                  pl.BlockSpec((None, None, 1, 2 * D_EXPERT), lambda i, te, fi, nu: (layer, te[i], 0, 0)),
                  pl.BlockSpec((None, None, D_EXPERT, D_MODEL), lambda i, te, fi, nu: (layer, te[i], 0, 0)),
                  pl.BlockSpec((None, None, 1, D_MODEL), lambda i, te, fi, nu: (layer, te[i], 0, 0))],
        out_specs=pl.BlockSpec((tm, D_MODEL // 2), lambda i, te, fi, nu: (i, 0)),
        scratch_shapes=[pltpu.VMEM((D_MODEL, 2 * D_EXPERT), BF16), pltpu.VMEM((D_EXPERT, D_MODEL), BF16)],
    )
    return pl.pallas_call(
        _expert_kernel,
        grid_spec=grid_spec,
        out_shape=jax.ShapeDtypeStruct((n_slots, D_MODEL // 2), U32),
        compiler_params=_cparams(("arbitrary",)),
        name="moe_experts",
    )(tile_expert, first, n_used, xs, wgu, bgu, wd, bd)


def _combine_kernel(meta_ref, next_ref, lpos_ref, gate_ref, x_ref, mod_ref, post_ref, y_ref, xo_ref,
                    buf_ref, buf1_ref, sem, sem1):
    tm = MOE_TOK
    step = pl.program_id(0)
    bufs, sems = (buf_ref, buf1_ref), (sem, sem1)
    rows = lambda ref, r: ref.at[pl.ds(r, MOE_PIECE)]

    def copies(mref, j, base, start):
        _run_copies(mref, base, lambda loc, slot: rows(y_ref, slot),
                    lambda loc, slot: rows(bufs[j], loc), sems[j], start)

    def reduce_tile(j):
        tok = slice(j * tm, (j + 1) * tm)
        s_iota = lax.broadcasted_iota(I32, (tm, MOE_BUF), 1)
        sel = jnp.zeros((tm, MOE_BUF), F32)
        for k in range(TOP_K):
            sel = jnp.where(s_iota == lpos_ref[tok, k:k + 1], gate_ref[tok, k:k + 1], sel)
        y = _dot(sel.astype(BF16), _unpack_bf16_pairs(bufs[j][...]))
        gate2 = mod_ref[:, 5 * D_MODEL:6 * D_MODEL]
        xo_ref[tok, :] = x_ref[tok, :] + gate2 * _rms(y, post_ref[...])

    @pl.when(step == 0)
    def _():
        buf_ref[...] = jnp.zeros_like(buf_ref)
        buf1_ref[...] = jnp.zeros_like(buf1_ref)
        copies(meta_ref, 0, 0, True)

    copies(meta_ref, 1, MOE_META, True)
    copies(meta_ref, 0, 0, False)
    reduce_tile(0)

    @pl.when(step < pl.num_programs(0) - 1)
    def _():
        copies(next_ref, 0, 0, True)

    copies(meta_ref, 1, MOE_META, False)
    reduce_tile(1)


def _combine(meta, lpos, gate, x_mid, mod, mod_row, post_g, y):
    n_tok = x_mid.shape[0]
    tm = 2 * MOE_TOK
    n_steps = n_tok // tm
    return pl.pallas_call(
        _combine_kernel,
        grid=(n_steps,),
        in_specs=[pl.BlockSpec((2 * MOE_META,), lambda i: (i,), memory_space=pltpu.SMEM),
                  pl.BlockSpec((2 * MOE_META,), lambda i: (jnp.minimum(i + 1, n_steps - 1),),
                               memory_space=pltpu.SMEM),
                  pl.BlockSpec((tm, TOP_K), lambda i: (i, 0)),
                  pl.BlockSpec((tm, TOP_K), lambda i: (i, 0)),
                  pl.BlockSpec((tm, D_MODEL), lambda i: (i, 0)),
                  pl.BlockSpec((None, 1, 6 * D_MODEL), lambda i: (mod_row(i * tm), 0, 0)),
                  pl.BlockSpec((1, D_MODEL), lambda i: (0, 0)),
                  pl.BlockSpec(memory_space=pl.ANY)],
        out_specs=pl.BlockSpec((tm, D_MODEL), lambda i: (i, 0)),
        out_shape=jax.ShapeDtypeStruct((n_tok, D_MODEL), F32),
        scratch_shapes=[pltpu.VMEM((MOE_BUF, D_MODEL // 2), U32), pltpu.VMEM((MOE_BUF, D_MODEL // 2), U32),
                        pltpu.SemaphoreType.DMA, pltpu.SemaphoreType.DMA],
        compiler_params=_cparams(("arbitrary",)),
        name="moe_combine",
    )(meta, meta, lpos, gate, x_mid, mod, post_g, y)


def _moe(h2, logits, x_mid, mod, mod_row, lw, tm_e=512):
    n_tok = h2.shape[0]
    n_t = n_tok // MOE_TOK
    gate, lpos, tcnt = _route(logits)
    tcnt = tcnt[:, 0, :N_EXPERTS].astype(I32)
    pieces = (tcnt + MOE_PIECE - 1) // MOE_PIECE
    run_end = jnp.cumsum(pieces, axis=0) * MOE_PIECE
    used = run_end[-1]
    padded = (used + tm_e - 1) // tm_e * tm_e
    pad_end = jnp.cumsum(padded)
    offs = pad_end - padded
    n_tiles = (n_tok * TOP_K + n_t * N_EXPERTS * (MOE_PIECE - 1) + tm_e - 1) // tm_e + N_EXPERTS
    tile_start = jnp.arange(n_tiles, dtype=I32) * tm_e
    tile_expert = jnp.minimum(jnp.sum(pad_end[None, :] <= tile_start[:, None], axis=1),
                              N_EXPERTS - 1).astype(I32)
    n_used = (pad_end[-1:] // tm_e).astype(I32)
    piece_end = jnp.cumsum(pieces, axis=1)
    slot_start = offs[None, :] + run_end - pieces * MOE_PIECE
    q = jnp.arange(MOE_META - 1, dtype=I32)
    owner = q[None, :, None] >= piece_end[:, None, :]
    is_owner = owner != jnp.concatenate([jnp.ones_like(owner[..., :1]), owner[..., :-1]], axis=-1)
    run_base = slot_start - (piece_end - pieces) * MOE_PIECE
    piece_slot = jnp.sum(jnp.where(is_owner, run_base[:, None, :], 0), axis=-1) + q[None, :] * MOE_PIECE
    meta = jnp.concatenate([piece_end[:, -1:], piece_slot], axis=1).reshape(-1).astype(I32)
    z0 = offs + used
    tail = jnp.stack([n_used[0], n_tiles - n_used[0]])
    zmeta = jnp.concatenate([z0, (pad_end - z0) // MOE_PIECE, tail,
                             jnp.zeros((MOE_META - 2 * N_EXPERTS - 2,), I32)]).astype(I32)
    xs = _dispatch(meta, zmeta, lpos.T, h2, n_tiles * tm_e, tm_e)
    y = _experts(tile_expert, n_used, xs, lw["exp_w_gu"], lw["exp_b_gu"], lw["exp_w_down"],
                 lw["exp_b_down"], lw["layer"], tm_e)
    return _combine(meta, lpos, gate, x_mid, mod, mod_row, lw["norm_post_ffn"], y)


def _split_w_in(w_in):
    offs, o = {}, 0
    for name, width in (("dn_k", 1024), ("dn_v", 1024), ("dn_a", 16), ("dn_b", 16), ("at_k", 128),
                        ("at_v", 128), ("dn_q", 1024), ("dn_g", 1024), ("at_q", 1024),
                        ("sg_u", 1024), ("sg_v", 1024), ("gates", 3072)):
        offs[name] = (o, o + width)
        o += width
    sl = lambda n: w_in[:, offs[n][0]:offs[n][1]]
    w_main = jnp.concatenate([sl(n) for n in ("dn_k", "dn_v", "dn_q", "dn_g", "at_q", "sg_u", "sg_v",
                                              "gates")], axis=1).astype(BF16)
    pad = jnp.zeros((w_in.shape[0], N_SMALL_COLS - 2 * LANES - 4 * DN_HEADS), w_in.dtype)
    w_small = jnp.concatenate([sl("at_k"), sl("at_v"), sl("dn_a"), sl("dn_b"), pad], axis=1).astype(BF16)
    return w_main, w_small


def _dn_gates(small, B, T):
    ab = small[:, 2 * LANES:2 * LANES + 4 * DN_HEADS].reshape(B, T, 2, 2, DN_HEADS)
    col = jnp.transpose(ab, (3, 0, 1, 2, 4)).reshape(2, B, T, 2 * DN_HEADS)
    row = jnp.transpose(col.reshape(2, B, T // DN_CHUNK, DN_CHUNK, 2 * DN_HEADS), (0, 1, 2, 4, 3))
    return col, row


def kernel(x, c, ctx, c_ctx, w_mod, b_mod, norm_pre_mix, norm_post_mix, norm_pre_ffn, norm_post_ffn, w_in, sg_ln_g, sg_ln_b, sg_w, sg_b, dn_conv_w, dn_a_log, dn_dt_bias, dn_norm_g, at_sinks, w_proj_sg, w_proj_dn, w_proj_at, w_out, router_w, router_b, exp_w_gu, exp_b_gu, exp_w_down, exp_b_down):
    B, S, D = x.shape
    L = ctx.shape[1]
    depth = w_mod.shape[0]
    assert D == D_MODEL and S % GRID_W == 0
    n_lat, n_ctx = B * S, B * L

    rows = (B + 1 + 7) // 8 * 8
    cvec = jnp.zeros((rows, D), F32).at[:B].set(c).at[B].set(c_ctx)
    mod_all = _modulation(cvec, w_mod, b_mod)
    tables = _rope_tables(S)

    lat_row = lambda t: t // S
    ctx_row = lambda t: B
    all_row = lambda t: jnp.where(t < n_lat, t // S, B)

    xl = x.reshape(n_lat, D)
    xc = ctx.reshape(n_ctx, D)
    for l in range(depth):
        need_ctx_out = l < depth - 1
        mod = mod_all[l].reshape(rows, 1, 6 * D)
        w_main, w_small = _split_w_in(w_in[l])
        lw = {
            "dn_norm_g": dn_norm_g[l].reshape(1, -1),
            "norm_post_mix": norm_post_mix[l].reshape(1, -1),
            "norm_pre_ffn": norm_pre_ffn[l].reshape(1, -1),
            "norm_post_ffn": norm_post_ffn[l].reshape(1, -1),
            "w_proj_sg": w_proj_sg[l].astype(BF16), "w_proj_dn": w_proj_dn[l].astype(BF16),
            "w_proj_at": w_proj_at[l].astype(BF16), "w_out": w_out[l].astype(BF16),
            "router_w": jnp.pad(router_w[l], ((0, 0), (0, LANES - N_EXPERTS))).astype(BF16),
            "router_b": jnp.pad(router_b[l], (0, LANES - N_EXPERTS),
                                constant_values=NEG_BIG).reshape(1, -1),
            "layer": l,
            "exp_w_gu": exp_w_gu, "exp_b_gu": exp_b_gu.reshape(depth, N_EXPERTS, 1, -1),
            "exp_w_down": exp_w_down, "exp_b_down": exp_b_down.reshape(depth, N_EXPERTS, 1, -1),
        }
        pre_g = norm_pre_mix[l].reshape(1, -1)
        main, small = _inproj(xl, mod, lat_row, pre_g, w_main, w_small, min(1024, S))
        w_main_c = w_main if need_ctx_out else w_main[:, :N_CTX_MAIN_COLS]
        main_c, small_c = _inproj(xc, mod, ctx_row, pre_g, w_main_c, w_small, min(1024, n_ctx))

        sg_args = (sg_ln_g[l].reshape(1, -1), sg_ln_b[l].reshape(1, -1), sg_w[l].astype(BF16),
                   sg_b[l].T)
        ysg = _sgu(main, *sg_args)

        gcol_c, grow_c = _dn_gates(small_c, B, L)
        gcol, grow = _dn_gates(small, B, S)
        s0 = jnp.zeros((2, B, DN_HEADS, DN_DIM, DN_DIM), F32)
        of_c, ob_c, s_ctx = _deltanet(main_c.reshape(B, L, -1), gcol_c, grow_c, dn_conv_w[l], dn_a_log[l],
                                      dn_dt_bias[l], s0, need_ctx_out)
        of_l, ob_l, _ = _deltanet(main.reshape(B, S, -1), gcol, grow, dn_conv_w[l], dn_a_log[l],
                                  dn_dt_bias[l], s_ctx, True)

        sinks = jnp.repeat(at_sinks[l], AT_BLOCK).reshape(-1, 1)
        q_r, k_r, v_r = _rope(main, small, tables, S)
        yat = _attention_local(q_r, k_r, v_r, small_c, sinks, B, S, L)

        x_mid, h2, logits = _merge(ysg, of_l.reshape(n_lat, D), ob_l.reshape(n_lat, D), main, yat, xl, mod,
                                   lat_row, lw)
        if need_ctx_out:
            ysg_c = _sgu(main_c, *sg_args)
            yat_c = _attention_ctx(main_c, small_c, sinks, B, L)
            xc_mid, h2c, logits_c = _merge(ysg_c, of_c.reshape(n_ctx, D), ob_c.reshape(n_ctx, D), main_c,
                                           yat_c, xc, mod, ctx_row, lw)
            x_mid = jnp.concatenate([x_mid, xc_mid], axis=0)
            h2 = jnp.concatenate([h2, h2c], axis=0)
            logits = jnp.concatenate([logits, logits_c], axis=0)
            xo = _moe(h2, logits, x_mid, mod, all_row, lw)
            xl, xc = xo[:n_lat], xo[n_lat:]
        else:
            xl = _moe(h2, logits, x_mid, mod, lat_row, lw)
    return xl.reshape(B, S, D)
```

```python
import functools
import math

import jax
import jax.numpy as jnp
from jax import lax
from jax.experimental import pallas as pl
from jax.experimental.pallas import tpu as pltpu

F32 = jnp.float32
BF16 = jnp.bfloat16
I32 = jnp.int32
U32 = jnp.uint32

EPS = 1e-6
D_MODEL = 1024
GRID_W = 64

SG_CHUNK = 128
SG_GROUPS = 8

DN_HEADS = 8
DN_DIM = 128
DN_CONV = 5
DN_CHUNK = 64
DN_HALO = 16
DN_PREP_CHUNKS = 2
DN_SCAN_CHUNKS = 4

AT_Q_HEADS = 16
AT_KV_HEADS = 2
AT_DIM = 64
AT_BLOCK = 128
MERGE_ROW_GROUPS = 2
AT_STAGE_HEADS = 2
ROPE_BASE = 10000.0

N_EXPERTS = 32
TOP_K = 4
D_EXPERT = 1024
SWIGLU_ALPHA = 1.702
SWIGLU_LIMIT = 7.0
N_BRANCH = 3

LANES = 128
NEG_BIG = -1e30

COL_DN_K, COL_DN_V, COL_DN_Q, COL_DN_G, COL_AT_Q, COL_SG_U, COL_SG_V, COL_GATE0 = range(8)
N_MAIN_COLS = 10 * D_MODEL
N_CTX_MAIN_COLS = 2 * D_MODEL
N_SMALL_COLS = 3 * LANES

VMEM_LIMIT = 52 * 1024 * 1024


def _cparams(sem):
    return pltpu.CompilerParams(dimension_semantics=sem, vmem_limit_bytes=VMEM_LIMIT)


def _dot(a, b):
    return jnp.dot(a, b, preferred_element_type=F32)


def _dot_nt(a, b):
    return lax.dot_general(a, b, (((1,), (1,)), ((), ())), preferred_element_type=F32)


def _dot_tn(a, b):
    return lax.dot_general(a, b, (((0,), (0,)), ((), ())), preferred_element_type=F32)


def _sigmoid(x):
    return 0.5 * (1.0 + jnp.tanh(0.5 * x))


def _silu(x):
    return x * _sigmoid(x)


def _gelu_tanh(x):
    return 0.5 * x * (1.0 + jnp.tanh(math.sqrt(2.0 / math.pi) * (x + 0.044715 * (x * x * x))))


def _softplus(x):
    return jnp.maximum(x, 0.0) + jnp.log(1.0 + jnp.exp(-jnp.abs(x)))


def _rms(x, g):
    return x * lax.rsqrt(jnp.mean(x * x, axis=-1, keepdims=True) + EPS) * g


def _mod_kernel(c_ref, w_ref, b_ref, o_ref):
    s = _silu(c_ref[...])
    o_ref[...] = jnp.dot(s, w_ref[...], preferred_element_type=F32,
                         precision=lax.Precision.HIGHEST) + b_ref[...]


def _modulation(cvec, w_mod, b_mod):
    depth = w_mod.shape[0]
    rows = cvec.shape[0]
    n_col = w_mod.shape[2] // D_MODEL
    return pl.pallas_call(
        _mod_kernel,
        grid=(depth, n_col),
        in_specs=[pl.BlockSpec((rows, D_MODEL), lambda l, j: (0, 0)),
                  pl.BlockSpec((None, D_MODEL, D_MODEL), lambda l, j: (l, 0, j)),
                  pl.BlockSpec((None, 1, D_MODEL), lambda l, j: (l, 0, j))],
        out_specs=pl.BlockSpec((None, rows, D_MODEL), lambda l, j: (l, 0, j)),
        out_shape=jax.ShapeDtypeStruct((depth, rows, w_mod.shape[2]), F32),
        compiler_params=_cparams(("arbitrary", "arbitrary")),
        name="modulation",
    )(cvec, w_mod, b_mod.reshape(depth, 1, -1))


def _inproj_kernel(x_ref, mod_ref, g_ref, wm_ref, ws_ref, main_ref, small_ref, h_ref):
    @pl.when(pl.program_id(1) == 0)
    def _():
        sh = mod_ref[:, 0 * D_MODEL:1 * D_MODEL]
        sc = mod_ref[:, 1 * D_MODEL:2 * D_MODEL]
        h = (_rms(x_ref[...], g_ref[...]) * (1.0 + sc) + sh).astype(BF16)
        h_ref[...] = h
        small_ref[...] = _dot(h, ws_ref[...])

    main_ref[...] = _dot(h_ref[...], wm_ref[...]).astype(BF16)


def _inproj(x, mod, mod_row, norm_g, w_main, w_small, tm, tn=2048):
    n_tok = x.shape[0]
    n_main = w_main.shape[1]
    return pl.pallas_call(
        _inproj_kernel,
        grid=(n_tok // tm, n_main // tn),
        in_specs=[pl.BlockSpec((tm, D_MODEL), lambda i, j: (i, 0)),
                  pl.BlockSpec((None, 1, 6 * D_MODEL), lambda i, j: (mod_row(i * tm), 0, 0)),
                  pl.BlockSpec((1, D_MODEL), lambda i, j: (0, 0)),
                  pl.BlockSpec((D_MODEL, tn), lambda i, j: (0, j)),
                  pl.BlockSpec((D_MODEL, N_SMALL_COLS), lambda i, j: (0, 0))],
        out_specs=[pl.BlockSpec((tm, tn), lambda i, j: (i, j)),
                   pl.BlockSpec((tm, N_SMALL_COLS), lambda i, j: (i, 0))],
        out_shape=[jax.ShapeDtypeStruct((n_tok, n_main), BF16),
                   jax.ShapeDtypeStruct((n_tok, N_SMALL_COLS), F32)],
        scratch_shapes=[pltpu.VMEM((tm, D_MODEL), BF16)],
        compiler_params=_cparams(("arbitrary", "arbitrary")),
        name="inproj",
    )(x, mod, norm_g, w_main, w_small)


def _sgu_kernel(u_ref, v_ref, lng_ref, lnb_ref, ws_ref, bs_ref, o_ref, *, n_chunk):
    u = _gelu_tanh(u_ref[...].astype(F32))
    v = _gelu_tanh(v_ref[...].astype(F32))
    vc = v - jnp.mean(v, axis=-1, keepdims=True)
    var = jnp.mean(vc * vc, axis=-1, keepdims=True)
    vn = (vc * lax.rsqrt(var + EPS) * lng_ref[...] + lnb_ref[...]).astype(BF16)
    for n in range(n_chunk):
        rows = slice(n * SG_CHUNK, (n + 1) * SG_CHUNK)
        for g in range(SG_GROUPS):
            cols = slice(g * LANES, (g + 1) * LANES)
            mixed = _dot(ws_ref[g], vn[rows, cols]) + bs_ref[:, g:g + 1]
            o_ref[rows, cols] = (u[rows, cols] * mixed).astype(BF16)


def _sgu(main, sg_ln_g, sg_ln_b, sg_w, sg_bt, n_chunk=2):
    n_tok = main.shape[0]
    tc = n_chunk * SG_CHUNK
    return pl.pallas_call(
        functools.partial(_sgu_kernel, n_chunk=n_chunk),
        grid=(n_tok // tc,),
        in_specs=[pl.BlockSpec((tc, D_MODEL), lambda i: (i, COL_SG_U)),
                  pl.BlockSpec((tc, D_MODEL), lambda i: (i, COL_SG_V)),
                  pl.BlockSpec((1, D_MODEL), lambda i: (0, 0)),
                  pl.BlockSpec((1, D_MODEL), lambda i: (0, 0)),
                  pl.BlockSpec((SG_GROUPS, SG_CHUNK, SG_CHUNK), lambda i: (0, 0, 0)),
                  pl.BlockSpec((SG_CHUNK, SG_GROUPS), lambda i: (0, 0))],
        out_specs=pl.BlockSpec((tc, D_MODEL), lambda i: (i, 0)),
        out_shape=jax.ShapeDtypeStruct((n_tok, D_MODEL), BF16),
        compiler_params=_cparams(("arbitrary",)),
        name="sgu",
    )(main, main, sg_ln_g, sg_ln_b, sg_w, sg_bt)


def _dn_prep_kernel(*refs, with_q, n_chunks):
    if with_q:
        (qp_ref, qc_ref, qn_ref, kp_ref, kc_ref, kn_ref, vp_ref, vc_ref, vn_ref,
         gcol_ref, grow_ref, cw_ref, alog_r_ref, alog_c_ref, dtb_r_ref, dtb_c_ref,
         w_ref, u0_ref, ke_ref, gt_ref, qs_ref, qk_ref) = refs
    else:
        (kp_ref, kc_ref, kn_ref, vp_ref, vc_ref, vn_ref,
         gcol_ref, grow_ref, cw_ref, alog_r_ref, alog_c_ref, dtb_r_ref, dtb_c_ref,
         w_ref, u0_ref, ke_ref, gt_ref) = refs
    c = pl.program_id(1)
    C = DN_CHUNK
    R = DN_PREP_CHUNKS * C
    has_prev = (c > 0).astype(BF16)
    has_next = (c < n_chunks // DN_PREP_CHUNKS - 1).astype(BF16)

    pad = DN_CONV // 2
    n_sh = DN_CONV - 1
    sr = lax.broadcasted_iota(I32, (n_sh * R, R + 2 * DN_HALO), 0)
    sc = lax.broadcasted_iota(I32, (n_sh * R, R + 2 * DN_HALO), 1)
    blk = sr // R
    off = jnp.where(blk < pad, blk - pad, blk - pad + 1)
    shift_mat = (sc == DN_HALO + (sr - blk * R) + off).astype(BF16)

    def conv_silu(p_ref, c_ref, n_ref, part):
        cur = c_ref[...]
        ext = jnp.concatenate([p_ref[...] * has_prev, cur, n_ref[...] * has_next], axis=0)
        sh = _dot(shift_mat, ext)
        taps = [sh[j * R:(j + 1) * R] for j in range(pad)] + [cur.astype(F32)] + \
               [sh[j * R:(j + 1) * R] for j in range(pad, n_sh)]
        y = None
        for i in range(DN_CONV):
            t = taps[i] * cw_ref[i:i + 1, part * D_MODEL:(part + 1) * D_MODEL]
            y = t if y is None else y + t
        return _silu(y)

    k_all = conv_silu(kp_ref, kc_ref, kn_ref, 1)
    v_all = conv_silu(vp_ref, vc_ref, vn_ref, 2)
    q_all = conv_silu(qp_ref, qc_ref, qn_ref, 0) if with_q else None

    ri = lax.broadcasted_iota(I32, (C, C), 0)
    ci = lax.broadcasted_iota(I32, (C, C), 1)
    eye = (ri == ci).astype(F32)
    CC = range(DN_PREP_CHUNKS)
    H = range(DN_HEADS)
    CH = [(cc, h) for cc in CC for h in H]
    ch = {key: i for i, key in enumerate(CH)}
    rows = [slice(cc * C, (cc + 1) * C) for cc in CC]
    lanes = [slice(h * DN_DIM, (h + 1) * DN_DIM) for h in H]
    kh = [k_all[rows[cc], lanes[h]] for cc, h in CH]
    kh = [k * lax.rsqrt(jnp.sum(k * k, axis=-1, keepdims=True) + EPS) for k in kh]
    kb = [k.astype(BF16) for k in kh]
    vh = [v_all[rows[cc], lanes[h]] for cc, h in CH]
    if with_q:
        qh = [q_all[rows[cc], lanes[h]] for cc, h in CH]
        qh = [q * (lax.rsqrt(jnp.sum(q * q, axis=-1, keepdims=True) + EPS) * DN_DIM ** -0.5) for q in qh]
        gram = [_dot_nt(jnp.concatenate([kb[j], qh[j].astype(BF16)], axis=0), kb[j]) for j in range(len(CH))]
        kk = [g[:C] for g in gram]
        qk_raw = [g[C:] for g in gram]
    else:
        kk = [_dot_nt(k, k) for k in kb]

    D2 = range(2)
    DH = [(cc, d, h) for cc in CC for d in D2 for h in H]
    incl = [(ri >= ci), (ri <= ci)]
    strict = [(ri > ci), (ri < ci)]
    gam_col, gam_row, gam_tot, beta_col = {}, {}, {}, {}
    for cc in CC:
        for d in D2:
            gcol = gcol_ref[d, rows[cc], :]
            ld_col = -jnp.exp(alog_r_ref[d]) * _softplus(gcol[:, 0:DN_HEADS] + dtb_r_ref[d])
            ld_row = -jnp.exp(alog_c_ref[d]) * _softplus(grow_ref[d, cc][0:DN_HEADS, :] + dtb_c_ref[d])
            beta_col[cc, d] = _sigmoid(gcol[:, DN_HEADS:2 * DN_HEADS])
            gam_col[cc, d] = jnp.dot(incl[d].astype(F32), ld_col, preferred_element_type=F32,
                                     precision=lax.Precision.HIGHEST)
            gam_row[cc, d] = jnp.dot(ld_row, incl[1 - d].astype(F32), preferred_element_type=F32,
                                     precision=lax.Precision.HIGHEST)
            tot = jnp.sum(ld_col, axis=0, keepdims=True)
            gam_tot[cc, d] = tot
            gt_ref[d, cc] = tot
    gc = [gam_col[cc, d][:, h:h + 1] for cc, d, h in DH]
    bc = [beta_col[cc, d][:, h:h + 1] for cc, d, h in DH]
    decay = [jnp.exp(jnp.where(incl[d], gc[i] - gam_row[cc, d][h:h + 1, :], NEG_BIG))
             for i, (cc, d, h) in enumerate(DH)]
    x = [-(jnp.where(strict[d], decay[i], 0.0) * bc[i] * kk[ch[cc, h]]) for i, (cc, d, h) in enumerate(DH)]
    N = range(len(DH))
    p = [eye + x[i] for i in N]
    xb = [x[i].astype(BF16) for i in N]
    x = [_dot(xb[i], xb[i]) for i in N]
    n_fac = int(math.log2(C)) - 1
    for j in range(n_fac):
        xb = [x[i].astype(BF16) for i in N]
        if j < n_fac - 1:
            r = [_dot(xb[i], jnp.concatenate([xb[i], p[i].astype(BF16)], axis=1)) for i in N]
            x = [r[i][:, :C] for i in N]
            p = [p[i] + r[i][:, C:] for i in N]
        else:
            p = [p[i] + _dot(xb[i], p[i].astype(BF16)) for i in N]
    rhs = [jnp.concatenate([kh[ch[cc, h]] * (bc[i] * jnp.exp(gc[i])), vh[ch[cc, h]] * bc[i]],
                           axis=1).astype(BF16) for i, (cc, d, h) in enumerate(DH)]
    sol = [_dot(p[i].astype(BF16), rhs[i]) for i in N]
    for i, (cc, d, h) in enumerate(DH):
        j = ch[cc, h]
        w_ref[d, rows[cc], lanes[h]] = sol[i][:, :DN_DIM].astype(BF16)
        u0_ref[d, rows[cc], lanes[h]] = sol[i][:, DN_DIM:].astype(BF16)
        ke_ref[d, rows[cc], lanes[h]] = (kh[j] * jnp.exp(gam_tot[cc, d][:, h:h + 1] - gc[i])).astype(BF16)
        if with_q:
            qs_ref[d, rows[cc], lanes[h]] = (qh[j] * jnp.exp(gc[i])).astype(BF16)
            qk_ref[d, rows[cc], h * C:(h + 1) * C] = (qk_raw[j] * decay[i]).astype(BF16)


def _dn_scan_kernel(*refs, with_q, n_chunks):
    n_state = 2 * DN_HEADS
    s_refs = refs[-n_state:]
    refs = refs[:-n_state]
    if with_q:
        (w0, w1, u0, u1, k0, k1, g0, g1, qs0, qs1, qk0, qk1, s0_ref, o0_ref, o1_ref, sfin_ref) = refs
        qs_r, qk_r, o_r = (qs0, qs1), (qk0, qk1), (o0_ref, o1_ref)
    else:
        (w0, w1, u0, u1, k0, k1, g0, g1, s0_ref, sfin_ref) = refs
    w_r, u_r, k_r, g_r = (w0, w1), (u0, u1), (k0, k1), (g0, g1)
    c = pl.program_id(1)
    C = DN_CHUNK
    DH = [(d, h) for d in range(2) for h in range(DN_HEADS)]
    N = range(len(DH))
    lanes = [slice(h * DN_DIM, (h + 1) * DN_DIM) for h in range(DN_HEADS)]

    @pl.when(c == 0)
    def _():
        for i, (d, h) in enumerate(DH):
            s_refs[i][...] = s0_ref[d, h]

    s = [s_refs[i][...] for i in N]
    for sub in range(DN_SCAN_CHUNKS):
        cix = (sub, DN_SCAN_CHUNKS - 1 - sub)
        rows = [slice(cix[d] * C, (cix[d] + 1) * C) for d in range(2)]
        sb = [s[i].astype(BF16) for i in N]
        w = [w_r[d][rows[d], lanes[h]] for d, h in DH]
        if with_q:
            wq = [jnp.concatenate([w[i], qs_r[d][rows[d], lanes[h]]], axis=0) for i, (d, h) in enumerate(DH)]
            ws = [_dot(wq[i], sb[i]) for i in N]
            ub = [(u_r[d][rows[d], lanes[h]].astype(F32) - ws[i][:C]).astype(BF16)
                  for i, (d, h) in enumerate(DH)]
            qu = [_dot(qk_r[d][rows[d], h * C:(h + 1) * C], ub[i]) for i, (d, h) in enumerate(DH)]
            for i, (d, h) in enumerate(DH):
                o_r[d][rows[d], lanes[h]] = (ws[i][C:] + qu[i]).astype(BF16)
        else:
            ws = [_dot(w[i], sb[i]) for i in N]
            ub = [(u_r[d][rows[d], lanes[h]].astype(F32) - ws[i]).astype(BF16) for i, (d, h) in enumerate(DH)]
        ku = [_dot_tn(k_r[d][rows[d], lanes[h]], ub[i]) for i, (d, h) in enumerate(DH)]
        s = [jnp.exp(g_r[d][cix[d]][:, h:h + 1]) * s[i] + ku[i] for i, (d, h) in enumerate(DH)]
    for i in N:
        s_refs[i][...] = s[i]

    @pl.when(c == n_chunks // DN_SCAN_CHUNKS - 1)
    def _():
        for i, (d, h) in enumerate(DH):
            sfin_ref[d, h] = s_refs[i][...]


def _deltanet(main3, gate_col, gate_row, conv_w, alog, dtb, s0, with_q):
    B, T, _ = main3.shape
    C = DN_CHUNK
    n_chunks = T // C
    NC = DN_PREP_CHUNKS
    R = NC * C
    assert T % R == 0
    hpc = R // DN_HALO
    n_halo = T // DN_HALO

    def trio(col):
        return [pl.BlockSpec((None, DN_HALO, D_MODEL), lambda b, c: (b, jnp.maximum(c * hpc - 1, 0), col)),
                pl.BlockSpec((None, R, D_MODEL), lambda b, c: (b, c, col)),
                pl.BlockSpec((None, DN_HALO, D_MODEL),
                             lambda b, c: (b, jnp.minimum((c + 1) * hpc, n_halo - 1), col))]

    vec = lambda shape: pl.BlockSpec(shape, lambda b, c: (0,) * len(shape))
    in_specs = (trio(COL_DN_Q) if with_q else []) + trio(COL_DN_K) + trio(COL_DN_V) + [
        pl.BlockSpec((2, None, R, 2 * DN_HEADS), lambda b, c: (0, b, c, 0)),
        pl.BlockSpec((2, None, NC, 2 * DN_HEADS, C), lambda b, c: (0, b, c, 0, 0)),
        vec((DN_CONV, 3 * D_MODEL)),
        vec((2, 1, DN_HEADS)), vec((2, DN_HEADS, 1)), vec((2, 1, DN_HEADS)), vec((2, DN_HEADS, 1)),
    ]
    wide = lambda n: (pl.BlockSpec((2, None, R, n), lambda b, c: (0, b, c, 0)),
                      jax.ShapeDtypeStruct((2, B, T, n), BF16))
    outs = [wide(D_MODEL), wide(D_MODEL), wide(D_MODEL),
            (pl.BlockSpec((2, None, NC, 1, DN_HEADS), lambda b, c: (0, b, c, 0, 0)),
             jax.ShapeDtypeStruct((2, B, n_chunks, 1, DN_HEADS), F32))]
    if with_q:
        outs += [wide(D_MODEL), wide(DN_HEADS * C)]
    n_main = 3 if with_q else 2
    prep = pl.pallas_call(
        functools.partial(_dn_prep_kernel, with_q=with_q, n_chunks=n_chunks),
        grid=(B, n_chunks // NC),
        in_specs=in_specs, out_specs=[o[0] for o in outs], out_shape=[o[1] for o in outs],
        compiler_params=_cparams(("arbitrary", "arbitrary")),
        name="dn_prep_q" if with_q else "dn_prep",
    )(*([main3] * (3 * n_main)), gate_col, gate_row, conv_w,
      alog.reshape(2, 1, DN_HEADS), alog.reshape(2, DN_HEADS, 1),
      dtb.reshape(2, 1, DN_HEADS), dtb.reshape(2, DN_HEADS, 1))

    NS = DN_SCAN_CHUNKS
    RS = NS * C
    n_steps = n_chunks // NS
    assert n_chunks % NS == 0

    def both_dirs(arr, n):
        if n is None:
            return [pl.BlockSpec((None, None, NS, 1, DN_HEADS), lambda b, c: (0, b, c, 0, 0)),
                    pl.BlockSpec((None, None, NS, 1, DN_HEADS),
                                 lambda b, c: (1, b, n_steps - 1 - c, 0, 0))], [arr, arr]
        return [pl.BlockSpec((None, None, RS, n), lambda b, c: (0, b, c, 0)),
                pl.BlockSpec((None, None, RS, n), lambda b, c: (1, b, n_steps - 1 - c, 0))], [arr, arr]

    specs, args = [], []
    widths = [D_MODEL, D_MODEL, D_MODEL, None] + ([D_MODEL, DN_HEADS * C] if with_q else [])
    for arr, n in zip(prep, widths):
        sp, ar = both_dirs(arr, n)
        specs += sp
        args += ar
    s_spec = pl.BlockSpec((2, None, DN_HEADS, DN_DIM, DN_DIM), lambda b, c: (0, b, 0, 0, 0))
    s_shape = jax.ShapeDtypeStruct((2, B, DN_HEADS, DN_DIM, DN_DIM), F32)
    if with_q:
        out_specs = [pl.BlockSpec((None, RS, D_MODEL), lambda b, c: (b, c, 0)),
                     pl.BlockSpec((None, RS, D_MODEL), lambda b, c: (b, n_steps - 1 - c, 0)), s_spec]
        out_shape = [jax.ShapeDtypeStruct((B, T, D_MODEL), BF16)] * 2 + [s_shape]
    else:
        out_specs, out_shape = [s_spec], [s_shape]
    out = pl.pallas_call(
        functools.partial(_dn_scan_kernel, with_q=with_q, n_chunks=n_chunks),
        grid=(B, n_steps),
        in_specs=specs + [s_spec], out_specs=out_specs, out_shape=out_shape,
        scratch_shapes=[pltpu.VMEM((DN_DIM, DN_DIM), F32)] * (2 * DN_HEADS),
        compiler_params=_cparams(("arbitrary", "arbitrary")),
        name="dn_scan_q" if with_q else "dn_scan",
    )(*args, s0)
    return (out[0], out[1], out[2]) if with_q else (None, None, out[0])


def _rope_tables(S):
    half = AT_DIM // 2
    nf = half // 2
    inv_freq = ROPE_BASE ** (-jnp.arange(nf, dtype=F32) / nf)
    t = jnp.arange(S, dtype=jnp.int32)
    row = (t // GRID_W).astype(F32)
    col = (t % GRID_W).astype(F32)
    lane = jnp.arange(LANES)
    dd = lane % AT_DIM
    pos = jnp.where((dd < half)[None, :], row[:, None], col[:, None])
    ang = pos * inv_freq[lane % nf][None, :]
    first = ((lane % half) < nf)[None, :]
    sin = jnp.sin(ang)
    return jnp.cos(ang), jnp.where(first, -sin, 0.0), jnp.where(first, 0.0, sin)


def _rope_kernel(q_ref, k_ref, v_ref, cos_ref, sa_ref, sb_ref, qo_ref, ko_ref, vo_ref):
    cos, sa, sb = cos_ref[...], sa_ref[...], sb_ref[...]
    nf = AT_DIM // 4

    def rot(x):
        return x * cos + pltpu.roll(x, LANES - nf, 1) * sa + pltpu.roll(x, nf, 1) * sb

    for j in range(AT_Q_HEADS * AT_DIM // LANES):
        lanes = slice(j * LANES, (j + 1) * LANES)
        qo_ref[:, lanes] = (rot(q_ref[:, lanes].astype(F32)) * AT_DIM ** -0.5).astype(BF16)
    ko_ref[...] = rot(k_ref[...]).astype(BF16)
    vo_ref[...] = v_ref[...].astype(BF16)


def _rope(main, small, tables, S, tm=512):
    n_tok = main.shape[0]
    per_seq = S // tm
    tab_spec = pl.BlockSpec((tm, LANES), lambda i: (i % per_seq, 0))
    return pl.pallas_call(
        _rope_kernel,
        grid=(n_tok // tm,),
        in_specs=[pl.BlockSpec((tm, D_MODEL), lambda i: (i, COL_AT_Q)),
                  pl.BlockSpec((tm, LANES), lambda i: (i, 0)),
                  pl.BlockSpec((tm, LANES), lambda i: (i, 1)),
                  tab_spec, tab_spec, tab_spec],
        out_specs=[pl.BlockSpec((tm, D_MODEL), lambda i: (i, 0)),
                   pl.BlockSpec((tm, LANES), lambda i: (i, 0)),
                   pl.BlockSpec((tm, LANES), lambda i: (i, 0))],
        out_shape=[jax.ShapeDtypeStruct((n_tok, D_MODEL), BF16),
                   jax.ShapeDtypeStruct((n_tok, LANES), BF16),
                   jax.ShapeDtypeStruct((n_tok, LANES), BF16)],
        compiler_params=_cparams(("arbitrary",)),
        name="rope",
    )(main, small, small, *tables)


def _attn_kernel(*refs, local, n_blocks, q_scale):
    if local:
        (q_ref, kp_ref, kc_ref, kn_ref, vp_ref, vc_ref, vn_ref, kx_ref, vx_ref, sink_ref, o_ref) = refs
    else:
        (q_ref, kx_ref, vx_ref, sink_ref, o_ref) = refs
    P = AT_BLOCK
    G = AT_Q_HEADS // AT_KV_HEADS
    kx = kx_ref[...].astype(BF16)
    vx = vx_ref[...].astype(BF16)
    if local:
        i = pl.program_id(1)
        k_all = jnp.concatenate([kp_ref[...], kc_ref[...], kn_ref[...], kx], axis=0)
        v_all = jnp.concatenate([vp_ref[...], vc_ref[...], vn_ref[...], vx], axis=0)
        qi = lax.broadcasted_iota(I32, (P, P), 0)
        kj = lax.broadcasted_iota(I32, (P, P), 1)
        b_prev = jnp.where(kj >= qi, 0.0, NEG_BIG) + jnp.where(i > 0, 0.0, NEG_BIG)
        b_next = jnp.where(kj <= qi, 0.0, NEG_BIG) + jnp.where(i < n_blocks - 1, 0.0, NEG_BIG)
        b_prev2 = jnp.concatenate([b_prev] * AT_STAGE_HEADS, axis=0)
        b_next2 = jnp.concatenate([b_next] * AT_STAGE_HEADS, axis=0)
    else:
        k_all, v_all = kx, vx
    lo = lax.broadcasted_iota(I32, (P, LANES), 1) < AT_DIM
    qf = q_ref[...].astype(F32) * q_scale
    pieces = []
    for qh in range(AT_Q_HEADS):
        blk = qf[:, (qh // 2) * LANES:(qh // 2 + 1) * LANES]
        want_lo = qh // G == 0
        if want_lo != (qh % 2 == 0):
            blk = pltpu.roll(blk, AT_DIM, 1)
        pieces.append(jnp.where(lo if want_lo else ~lo, blk, 0.0).astype(BF16))
    HS = AT_STAGE_HEADS
    n_pair = AT_Q_HEADS // HS

    def logits(j):
        s = _dot_nt(jnp.concatenate(pieces[HS * j:HS * (j + 1)], axis=0), k_all)
        if not local:
            return s
        return jnp.concatenate([s[:, 0:P] + b_prev2, s[:, P:2 * P], s[:, 2 * P:3 * P] + b_next2,
                                s[:, 3 * P:]], axis=1)

    def softmax(j, s):
        sink = sink_ref[HS * j * P:HS * (j + 1) * P, :]
        m = jnp.maximum(jnp.max(s, axis=-1, keepdims=True), sink)
        p = jnp.exp(s - m)
        den = jnp.sum(p, axis=-1, keepdims=True) + jnp.exp(sink - m)
        return p.astype(BF16), den

    def values(j, p, den):
        o = _dot(p, v_all) / den
        for t in range(HS // 2):
            blk = (HS * j) // 2 + t
            a, b = o[2 * t * P:(2 * t + 1) * P], o[(2 * t + 1) * P:(2 * t + 2) * P]
            if (2 * blk) // G == 0:
                out = jnp.where(lo, a, pltpu.roll(b, AT_DIM, 1))
            else:
                out = jnp.where(lo, pltpu.roll(a, AT_DIM, 1), b)
            o_ref[:, blk * LANES:(blk + 1) * LANES] = out.astype(BF16)

    s_next = logits(0)
    prob = None
    for j in range(n_pair):
        s_cur = s_next
        if j + 1 < n_pair:
            s_next = logits(j + 1)
        done = prob
        prob = softmax(j, s_cur)
        if done is not None:
            values(j - 1, *done)
    values(n_pair - 1, *prob)


def _attention_local(q_r, k_r, v_r, small_c, sinks, B, S, L):
    P = AT_BLOCK
    nb = S // P

    def kv_trio():
        return [pl.BlockSpec((P, LANES), lambda b, i: (b * nb + jnp.maximum(i - 1, 0), 0)),
                pl.BlockSpec((P, LANES), lambda b, i: (b * nb + i, 0)),
                pl.BlockSpec((P, LANES), lambda b, i: (b * nb + jnp.minimum(i + 1, nb - 1), 0))]

    return pl.pallas_call(
        functools.partial(_attn_kernel, local=True, n_blocks=nb, q_scale=1.0),
        grid=(B, nb),
        in_specs=[pl.BlockSpec((P, D_MODEL), lambda b, i: (b * nb + i, 0))] + kv_trio() + kv_trio() + [
            pl.BlockSpec((L, LANES), lambda b, i: (b, 0)),
            pl.BlockSpec((L, LANES), lambda b, i: (b, 1)),
            pl.BlockSpec((AT_Q_HEADS * AT_BLOCK, 1), lambda b, i: (0, 0))],
        out_specs=pl.BlockSpec((P, D_MODEL), lambda b, i: (b * nb + i, 0)),
        out_shape=jax.ShapeDtypeStruct((B * S, D_MODEL), BF16),
        compiler_params=_cparams(("arbitrary", "arbitrary")),
        name="attn_local",
    )(q_r, k_r, k_r, k_r, v_r, v_r, v_r, small_c, small_c, sinks)


def _attention_ctx(main_c, small_c, sinks, B, L):
    P = AT_BLOCK
    nb = L // P
    return pl.pallas_call(
        functools.partial(_attn_kernel, local=False, n_blocks=nb, q_scale=AT_DIM ** -0.5),
        grid=(B, nb),
        in_specs=[pl.BlockSpec((P, D_MODEL), lambda b, i: (b * nb + i, COL_AT_Q)),
                  pl.BlockSpec((L, LANES), lambda b, i: (b, 0)),
                  pl.BlockSpec((L, LANES), lambda b, i: (b, 1)),
                  pl.BlockSpec((AT_Q_HEADS * AT_BLOCK, 1), lambda b, i: (0, 0))],
        out_specs=pl.BlockSpec((P, D_MODEL), lambda b, i: (b * nb + i, 0)),
        out_shape=jax.ShapeDtypeStruct((B * L, D_MODEL), BF16),
        compiler_params=_cparams(("arbitrary", "arbitrary")),
        name="attn_ctx",
    )(main_c, small_c, small_c, sinks)


def _merge_kernel(ysg_ref, of_ref, ob_ref, dng_ref, yat_ref, g0_ref, g1_ref, g2_ref, x_ref, mod_ref,
                  dn_norm_ref, post_ref, pre_ref, wsg_ref, wdn_ref, wat_ref, wout_ref, rw_ref, rb_ref,
                  xo_ref, h2_ref, lg_ref):
    dn_g = dn_norm_ref[...]
    gate1 = mod_ref[:, 2 * D_MODEL:3 * D_MODEL]
    sh2 = mod_ref[:, 3 * D_MODEL:4 * D_MODEL]
    sc2 = mod_ref[:, 4 * D_MODEL:5 * D_MODEL]
    n_grp = MERGE_ROW_GROUPS
    grp = x_ref.shape[0] // n_grp

    def dn_out(r):
        rows = slice(r * grp, (r + 1) * grp)
        o = of_ref[rows, :].astype(F32) + ob_ref[rows, :].astype(F32)
        parts = []
        for h in range(DN_HEADS):
            lanes = slice(h * DN_DIM, (h + 1) * DN_DIM)
            parts.append(_rms(o[:, lanes], dn_g) * _silu(dng_ref[rows, lanes].astype(F32)))
        return jnp.concatenate(parts, axis=1).astype(BF16)

    def branches(r, ydn):
        rows = slice(r * grp, (r + 1) * grp)
        m = (_sigmoid(g0_ref[rows, :].astype(F32)) * _dot(ysg_ref[rows, :], wsg_ref[...])
             + _sigmoid(g1_ref[rows, :].astype(F32)) * _dot(ydn, wdn_ref[...])
             + _sigmoid(g2_ref[rows, :].astype(F32)) * _dot(yat_ref[rows, :], wat_ref[...]))
        return m.astype(BF16)

    def out_proj(r, m):
        return _dot(m, wout_ref[...])

    def residual(r, y):
        rows = slice(r * grp, (r + 1) * grp)
        xn = x_ref[rows, :] + gate1 * _rms(y, post_ref[...])
        xo_ref[rows, :] = xn
        h2 = _rms(xn, pre_ref[...]) * (1.0 + sc2) + sh2
        h2_ref[rows, :] = h2
        lg_ref[rows, :] = _dot(h2.astype(BF16), rw_ref[...]) + rb_ref[...]

    stages = (dn_out, branches, out_proj, residual)
    carried = {}
    for t in range(n_grp + len(stages) - 1):
        for k in reversed(range(len(stages))):
            r = t - k
            if 0 <= r < n_grp:
                carried[r] = stages[k](r) if k == 0 else stages[k](r, carried[r])


def _merge(ysg, o_fwd, o_bwd, main, yat, x, mod, mod_row, lw, tm=512):
    n_tok = x.shape[0]
    const = lambda i: (0, 0)
    wspec = pl.BlockSpec((D_MODEL, D_MODEL), const, pipeline_mode=pl.Buffered(1))
    vspec = pl.BlockSpec((1, D_MODEL), const)
    return pl.pallas_call(
        _merge_kernel,
        grid=(n_tok // tm,),
        in_specs=[pl.BlockSpec((tm, D_MODEL), lambda i: (i, 0)),
                  pl.BlockSpec((tm, D_MODEL), lambda i: (i, 0)),
                  pl.BlockSpec((tm, D_MODEL), lambda i: (i, 0)),
                  pl.BlockSpec((tm, D_MODEL), lambda i: (i, COL_DN_G)),
                  pl.BlockSpec((tm, D_MODEL), lambda i: (i, 0)),
                  pl.BlockSpec((tm, D_MODEL), lambda i: (i, COL_GATE0)),
                  pl.BlockSpec((tm, D_MODEL), lambda i: (i, COL_GATE0 + 1)),
                  pl.BlockSpec((tm, D_MODEL), lambda i: (i, COL_GATE0 + 2)),
                  pl.BlockSpec((tm, D_MODEL), lambda i: (i, 0)),
                  pl.BlockSpec((None, 1, 6 * D_MODEL), lambda i: (mod_row(i * tm), 0, 0)),
                  pl.BlockSpec((1, DN_DIM), const), vspec, vspec,
                  wspec, wspec, wspec, wspec,
                  pl.BlockSpec((D_MODEL, LANES), const), pl.BlockSpec((1, LANES), const)],
        out_specs=[pl.BlockSpec((tm, D_MODEL), lambda i: (i, 0)),
                   pl.BlockSpec((tm, D_MODEL), lambda i: (i, 0)),
                   pl.BlockSpec((tm, LANES), lambda i: (i, 0))],
        out_shape=[jax.ShapeDtypeStruct((n_tok, D_MODEL), F32),
                   jax.ShapeDtypeStruct((n_tok, D_MODEL), F32),
                   jax.ShapeDtypeStruct((n_tok, LANES), F32)],
        compiler_params=_cparams(("arbitrary",)),
        name="merge",
    )(ysg, o_fwd, o_bwd, main, yat, main, main, main, x, mod,
      lw["dn_norm_g"], lw["norm_post_mix"], lw["norm_pre_ffn"],
      lw["w_proj_sg"], lw["w_proj_dn"], lw["w_proj_at"], lw["w_out"], lw["router_w"], lw["router_b"])


MOE_TOK = 256
MOE_PIECE = 8
MOE_BUF = MOE_TOK * TOP_K + N_EXPERTS * MOE_PIECE
MOE_META = 256
MOE_META_LEN = {2 * MOE_PIECE: MOE_BUF // (2 * MOE_PIECE), MOE_PIECE: N_EXPERTS}
MOE_META_DOUBLE = 2
MOE_META_SINGLE = MOE_META_DOUBLE + 2 * MOE_META_LEN[2 * MOE_PIECE]
assert MOE_META >= MOE_META_SINGLE + 2 * MOE_META_LEN[MOE_PIECE]


def _route_kernel(lg_ref, gate_ref, lpos_ref, tcnt_ref):
    tm = lg_ref.shape[0]
    l = lg_ref[...]
    lane = lax.broadcasted_iota(I32, l.shape, 1).astype(F32)
    vals, onehots = [], []
    for k in range(TOP_K):
        m = jnp.max(l, axis=-1, keepdims=True)
        ik = jnp.min(jnp.where(l == m, lane, float(LANES)), axis=-1, keepdims=True)
        oh = lane == ik
        vals.append(m)
        onehots.append(oh)
        l = jnp.where(oh, -jnp.inf, l)
    es = [jnp.exp(v - vals[0]) for v in vals]
    den = es[0] + es[1] + es[2] + es[3]
    sel = jnp.zeros(l.shape, F32)
    for k in range(TOP_K):
        gate_ref[:, k:k + 1] = es[k] / den
        sel = sel + onehots[k].astype(F32)
    ri = lax.broadcasted_iota(I32, (tm, tm), 0)
    ci = lax.broadcasted_iota(I32, (tm, tm), 1)
    before = _dot((ri > ci).astype(BF16), sel.astype(BF16))
    tcnt = jnp.sum(sel, axis=0, keepdims=True)
    tcnt_ref[...] = tcnt
    n_piece = jnp.floor((tcnt + (MOE_PIECE - 1)) * (1.0 / MOE_PIECE))
    ei = lax.broadcasted_iota(I32, (LANES, LANES), 0)
    ej = lax.broadcasted_iota(I32, (LANES, LANES), 1)
    run_start = _dot(jnp.broadcast_to(n_piece, (8, LANES)).astype(BF16),
                     (ei < ej).astype(BF16))[0:1] * float(MOE_PIECE)
    pos = before + run_start
    for k in range(TOP_K):
        lpos_ref[:, k:k + 1] = jnp.sum(jnp.where(onehots[k], pos, 0.0), axis=-1,
                                       keepdims=True).astype(I32)


def _route(logits):
    n_tok = logits.shape[0]
    tm = MOE_TOK
    n_t = n_tok // tm
    small = lambda dt: jax.ShapeDtypeStruct((n_tok, TOP_K), dt)
    kspec = pl.BlockSpec((tm, TOP_K), lambda i: (i, 0))
    tspec = pl.BlockSpec((None, 1, LANES), lambda i: (i, 0, 0))
    tshape = jax.ShapeDtypeStruct((n_t, 1, LANES), F32)
    return pl.pallas_call(
        _route_kernel,
        grid=(n_t,),
        in_specs=[pl.BlockSpec((tm, LANES), lambda i: (i, 0))],
        out_specs=[kspec, kspec, tspec],
        out_shape=[small(F32), small(I32), tshape],
        compiler_params=_cparams(("arbitrary",)),
        name="route",
    )(logits)


def _run_copies(meta_ref, base, src_of, dst_of, sem, start):
    def table(first, rows):
        def per_piece(q, carry):
            local = pl.multiple_of(meta_ref[base + first + q], MOE_PIECE)
            slot = pl.multiple_of(meta_ref[base + first + MOE_META_LEN[rows] + q], MOE_PIECE)
            cp = pltpu.make_async_copy(src_of(local, slot, rows), dst_of(local, slot, rows), sem)
            if start:
                cp.start()
            else:
                cp.wait()
            return carry
        return per_piece

    lax.fori_loop(0, meta_ref[base], table(MOE_META_DOUBLE, 2 * MOE_PIECE), 0)
    lax.fori_loop(0, meta_ref[base + 1], table(MOE_META_SINGLE, MOE_PIECE), 0)


def _dispatch_kernel(meta_ref, prev_ref, zmeta_ref, lpos_ref, h_ref, xs_ref, buf_ref, buf1_ref, zero_ref,
                     sem, sem1, *, tm_e):
    tm = MOE_TOK
    step = pl.program_id(0)
    bufs, sems = (buf_ref, buf1_ref), (sem, sem1)
    rows = lambda ref, r, n=MOE_PIECE: ref.at[pl.ds(r, n)]

    @pl.when(step == 0)
    def _():
        zero_ref[...] = jnp.zeros_like(zero_ref)

        def zero_tail(start):
            def per_expert(e, carry):
                z0 = pl.multiple_of(zmeta_ref[e], MOE_PIECE)

                def per_piece(p, c2):
                    cp = pltpu.make_async_copy(zero_ref, rows(xs_ref, z0 + p * MOE_PIECE), sem)
                    if start:
                        cp.start()
                    else:
                        cp.wait()
                    return c2

                return lax.fori_loop(0, zmeta_ref[N_EXPERTS + e], per_piece, carry)

            lax.fori_loop(0, N_EXPERTS, per_expert, 0)

        zero_tail(True)
        zero_tail(False)

        buf_ref[0:tm_e, :] = jnp.zeros((tm_e, D_MODEL // 2), U32)

        def zero_tiles(start):
            def per_tile(p, carry):
                t0 = pl.multiple_of((zmeta_ref[2 * N_EXPERTS] + p) * tm_e, tm_e)
                cp = pltpu.make_async_copy(buf_ref.at[pl.ds(0, tm_e)], xs_ref.at[pl.ds(t0, tm_e)], sem)
                if start:
                    cp.start()
                else:
                    cp.wait()
                return carry

            lax.fori_loop(0, zmeta_ref[2 * N_EXPERTS + 1], per_tile, 0)

        zero_tiles(True)
        zero_tiles(False)

    def group(j):
        s_iota = lax.broadcasted_iota(I32, (MOE_BUF, tm), 0)
        perm = jnp.zeros((MOE_BUF, tm), F32)
        for k in range(TOP_K):
            perm = jnp.where(s_iota == lpos_ref[k:k + 1, j * tm:(j + 1) * tm], 1.0, perm)
        bufs[j][...] = _pack_bf16_pairs(_dot(perm.astype(BF16), h_ref[j * tm:(j + 1) * tm, :].astype(BF16)))

    def copies(mref, j, start):
        _run_copies(mref, j * MOE_META, lambda loc, slot, n: rows(bufs[j], loc, n),
                    lambda loc, slot, n: rows(xs_ref, slot, n), sems[j], start)

    group(0)
    copies(meta_ref, 0, True)

    @pl.when(step > 0)
    def _():
        copies(prev_ref, 1, False)

    group(1)
    copies(meta_ref, 1, True)
    copies(meta_ref, 0, False)

    @pl.when(step == pl.num_programs(0) - 1)
    def _():
        copies(meta_ref, 1, False)


def _dispatch(meta, zmeta, lpos_t, h2, n_slots, tm_e):
    n_tok = h2.shape[0]
    tm = 2 * MOE_TOK
    assert tm_e <= MOE_BUF and n_tok % tm == 0
    return pl.pallas_call(
        functools.partial(_dispatch_kernel, tm_e=tm_e),
        grid=(n_tok // tm,),
        in_specs=[pl.BlockSpec((2 * MOE_META,), lambda i: (i,), memory_space=pltpu.SMEM),
                  pl.BlockSpec((2 * MOE_META,), lambda i: (jnp.maximum(i - 1, 0),), memory_space=pltpu.SMEM),
                  pl.BlockSpec((MOE_META,), lambda i: (0,), memory_space=pltpu.SMEM),
                  pl.BlockSpec((TOP_K, tm), lambda i: (0, i)),
                  pl.BlockSpec((tm, D_MODEL), lambda i: (i, 0))],
        out_specs=pl.BlockSpec(memory_space=pl.ANY),
        out_shape=jax.ShapeDtypeStruct((n_slots, D_MODEL // 2), U32),
        scratch_shapes=[pltpu.VMEM((MOE_BUF, D_MODEL // 2), U32), pltpu.VMEM((MOE_BUF, D_MODEL // 2), U32),
                        pltpu.VMEM((MOE_PIECE, D_MODEL // 2), U32),
                        pltpu.SemaphoreType.DMA, pltpu.SemaphoreType.DMA],
        compiler_params=_cparams(("arbitrary",)),
        name="moe_dispatch",
    )(meta, meta, zmeta, lpos_t, h2)


def _pack_bf16_pairs(x):
    w = x.shape[1] // 2
    xb = x.astype(BF16).astype(F32)
    lo = lax.shift_right_logical(lax.bitcast_convert_type(xb[:, :w], U32), jnp.uint32(16))
    hi = lax.bitcast_convert_type(xb[:, w:], U32) & jnp.uint32(0xFFFF0000)
    return hi | lo


def _unpack_bf16_pairs(p):
    lo = lax.bitcast_convert_type(lax.shift_left(p, jnp.uint32(16)), F32)
    hi = lax.bitcast_convert_type(p & jnp.uint32(0xFFFF0000), F32)
    return jnp.concatenate([lo, hi], axis=1).astype(BF16)


def _expert_kernel(te_ref, first_ref, nu_ref, xs_ref, wgu_ref, bgu_ref, wd_ref, bd_ref, y_ref,
                   wgu_b_ref, wd_b_ref):
    del te_ref
    i = pl.program_id(0)

    @pl.when(first_ref[i] == 1)
    def _():
        wgu_b_ref[...] = wgu_ref[...].astype(BF16)
        wd_b_ref[...] = wd_ref[...].astype(BF16)

    @pl.when(i < nu_ref[0])
    def _():
        gu = _dot(_unpack_bf16_pairs(xs_ref[...]), wgu_b_ref[...]) + bgu_ref[...]
        g = jnp.minimum(gu[:, :D_EXPERT], SWIGLU_LIMIT)
        lin = jnp.clip(gu[:, D_EXPERT:], -SWIGLU_LIMIT, SWIGLU_LIMIT)
        act = g * _sigmoid(SWIGLU_ALPHA * g) * (lin + 1.0)
        y_ref[...] = _pack_bf16_pairs(_dot(act.astype(BF16), wd_b_ref[...]) + bd_ref[...])

    @pl.when(i >= nu_ref[0])
    def _():
        y_ref[...] = jnp.zeros_like(y_ref)


def _experts(tile_expert, n_used, xs, wgu, bgu, wd, bd, layer, tm):
    n_slots = xs.shape[0]
    n_tiles = n_slots // tm
    first = jnp.concatenate([jnp.ones((1,), I32),
                             (tile_expert[1:] != tile_expert[:-1]).astype(I32)])

    def row(i, te, fi, nu):
        return (jnp.minimum(i, nu[0] - 1), 0)

    grid_spec = pltpu.PrefetchScalarGridSpec(
        num_scalar_prefetch=3,
        grid=(n_tiles,),
        in_specs=[pl.BlockSpec((tm, D_MODEL // 2), row),
                  pl.BlockSpec((None, None, D_MODEL, 2 * D_EXPERT), lambda i, te, fi, nu: (layer, te[i], 0, 0)),
                  pl.BlockSpec((None, None, 1, 2 * D_EXPERT), lambda i, te, fi, nu: (layer, te[i], 0, 0)),
                  pl.BlockSpec((None, None, D_EXPERT, D_MODEL), lambda i, te, fi, nu: (layer, te[i], 0, 0)),
                  pl.BlockSpec((None, None, 1, D_MODEL), lambda i, te, fi, nu: (layer, te[i], 0, 0))],
        out_specs=pl.BlockSpec((tm, D_MODEL // 2), lambda i, te, fi, nu: (i, 0)),
        scratch_shapes=[pltpu.VMEM((D_MODEL, 2 * D_EXPERT), BF16), pltpu.VMEM((D_EXPERT, D_MODEL), BF16)],
    )
    return pl.pallas_call(
        _expert_kernel,
        grid_spec=grid_spec,
        out_shape=jax.ShapeDtypeStruct((n_slots, D_MODEL // 2), U32),
        compiler_params=_cparams(("arbitrary",)),
        name="moe_experts",
    )(tile_expert, first, n_used, xs, wgu, bgu, wd, bd)


def _combine_kernel(meta_ref, next_ref, lpos_ref, gate_ref, x_ref, mod_ref, post_ref, y_ref, xo_ref,
                    buf_ref, buf1_ref, sem, sem1):
    tm = MOE_TOK
    step = pl.program_id(0)
    bufs, sems = (buf_ref, buf1_ref), (sem, sem1)
    rows = lambda ref, r, n=MOE_PIECE: ref.at[pl.ds(r, n)]

    def copies(mref, j, base, start):
        _run_copies(mref, base, lambda loc, slot, n: rows(y_ref, slot, n),
                    lambda loc, slot, n: rows(bufs[j], loc, n), sems[j], start)

    def reduce_tile(j):
        tok = slice(j * tm, (j + 1) * tm)
        s_iota = lax.broadcasted_iota(I32, (tm, MOE_BUF), 1)
        sel = jnp.zeros((tm, MOE_BUF), F32)
        for k in range(TOP_K):
            sel = jnp.where(s_iota == lpos_ref[tok, k:k + 1], gate_ref[tok, k:k + 1], sel)
        y = _dot(sel.astype(BF16), _unpack_bf16_pairs(bufs[j][...]))
        gate2 = mod_ref[:, 5 * D_MODEL:6 * D_MODEL]
        xo_ref[tok, :] = x_ref[tok, :] + gate2 * _rms(y, post_ref[...])

    @pl.when(step == 0)
    def _():
        buf_ref[...] = jnp.zeros_like(buf_ref)
        buf1_ref[...] = jnp.zeros_like(buf1_ref)
        copies(meta_ref, 0, 0, True)

    copies(meta_ref, 1, MOE_META, True)
    copies(meta_ref, 0, 0, False)
    reduce_tile(0)

    @pl.when(step < pl.num_programs(0) - 1)
    def _():
        copies(next_ref, 0, 0, True)

    copies(meta_ref, 1, MOE_META, False)
    reduce_tile(1)


def _combine(meta, lpos, gate, x_mid, mod, mod_row, post_g, y):
    n_tok = x_mid.shape[0]
    tm = 2 * MOE_TOK
    n_steps = n_tok // tm
    return pl.pallas_call(
        _combine_kernel,
        grid=(n_steps,),
        in_specs=[pl.BlockSpec((2 * MOE_META,), lambda i: (i,), memory_space=pltpu.SMEM),
                  pl.BlockSpec((2 * MOE_META,), lambda i: (jnp.minimum(i + 1, n_steps - 1),),
                               memory_space=pltpu.SMEM),
                  pl.BlockSpec((tm, TOP_K), lambda i: (i, 0)),
                  pl.BlockSpec((tm, TOP_K), lambda i: (i, 0)),
                  pl.BlockSpec((tm, D_MODEL), lambda i: (i, 0)),
                  pl.BlockSpec((None, 1, 6 * D_MODEL), lambda i: (mod_row(i * tm), 0, 0)),
                  pl.BlockSpec((1, D_MODEL), lambda i: (0, 0)),
                  pl.BlockSpec(memory_space=pl.ANY)],
        out_specs=pl.BlockSpec((tm, D_MODEL), lambda i: (i, 0)),
        out_shape=jax.ShapeDtypeStruct((n_tok, D_MODEL), F32),
        scratch_shapes=[pltpu.VMEM((MOE_BUF, D_MODEL // 2), U32), pltpu.VMEM((MOE_BUF, D_MODEL // 2), U32),
                        pltpu.SemaphoreType.DMA, pltpu.SemaphoreType.DMA],
        compiler_params=_cparams(("arbitrary",)),
        name="moe_combine",
    )(meta, meta, lpos, gate, x_mid, mod, post_g, y)


def _moe(h2, logits, x_mid, mod, mod_row, lw, tm_e=512):
    n_tok = h2.shape[0]
    n_t = n_tok // MOE_TOK
    gate, lpos, tcnt = _route(logits)
    tcnt = tcnt[:, 0, :N_EXPERTS].astype(I32)
    pieces = (tcnt + MOE_PIECE - 1) // MOE_PIECE
    run_end = jnp.cumsum(pieces, axis=0) * MOE_PIECE
    used = run_end[-1]
    padded = (used + tm_e - 1) // tm_e * tm_e
    pad_end = jnp.cumsum(padded)
    offs = pad_end - padded
    n_tiles = (n_tok * TOP_K + n_t * N_EXPERTS * (MOE_PIECE - 1) + tm_e - 1) // tm_e + N_EXPERTS
    tile_start = jnp.arange(n_tiles, dtype=I32) * tm_e
    tile_expert = jnp.minimum(jnp.sum(pad_end[None, :] <= tile_start[:, None], axis=1),
                              N_EXPERTS - 1).astype(I32)
    n_used = (pad_end[-1:] // tm_e).astype(I32)
    lstart = (jnp.cumsum(pieces, axis=1) - pieces) * MOE_PIECE
    slot_start = offs[None, :] + run_end - pieces * MOE_PIECE

    def table(count, rows, first_row, length):
        end = jnp.cumsum(count, axis=1)
        q = jnp.arange(length, dtype=I32)
        passed = q[None, :, None] >= end[:, None, :]
        owner = passed != jnp.concatenate([jnp.ones_like(passed[..., :1]), passed[..., :-1]], axis=-1)
        within = (q[None, :, None] - (end - count)[:, None, :]) * rows + first_row[:, None, :]
        pick = lambda base: jnp.sum(jnp.where(owner, base[:, None, :] + within, 0), axis=-1)
        return end[:, -1:], pick(lstart), pick(slot_start)

    doubles = pieces // 2
    n2, local2, slot2 = table(doubles, 2 * MOE_PIECE, jnp.zeros_like(pieces), MOE_META_LEN[2 * MOE_PIECE])
    n1, local1, slot1 = table(pieces % 2, MOE_PIECE, doubles * 2 * MOE_PIECE, MOE_META_LEN[MOE_PIECE])
    meta = jnp.concatenate([n2, n1, local2, slot2, local1, slot1], axis=1)
    meta = jnp.pad(meta, ((0, 0), (0, MOE_META - meta.shape[1]))).reshape(-1).astype(I32)
    z0 = offs + used
    tail = jnp.stack([n_used[0], n_tiles - n_used[0]])
    zmeta = jnp.concatenate([z0, (pad_end - z0) // MOE_PIECE, tail,
                             jnp.zeros((MOE_META - 2 * N_EXPERTS - 2,), I32)]).astype(I32)
    xs = _dispatch(meta, zmeta, lpos.T, h2, n_tiles * tm_e, tm_e)
    y = _experts(tile_expert, n_used, xs, lw["exp_w_gu"], lw["exp_b_gu"], lw["exp_w_down"],
                 lw["exp_b_down"], lw["layer"], tm_e)
    return _combine(meta, lpos, gate, x_mid, mod, mod_row, lw["norm_post_ffn"], y)


def _split_w_in(w_in):
    offs, o = {}, 0
    for name, width in (("dn_k", 1024), ("dn_v", 1024), ("dn_a", 16), ("dn_b", 16), ("at_k", 128),
                        ("at_v", 128), ("dn_q", 1024), ("dn_g", 1024), ("at_q", 1024),
                        ("sg_u", 1024), ("sg_v", 1024), ("gates", 3072)):
        offs[name] = (o, o + width)
        o += width
    sl = lambda n: w_in[:, offs[n][0]:offs[n][1]]
    w_main = jnp.concatenate([sl(n) for n in ("dn_k", "dn_v", "dn_q", "dn_g", "at_q", "sg_u", "sg_v",
                                              "gates")], axis=1).astype(BF16)
    pad = jnp.zeros((w_in.shape[0], N_SMALL_COLS - 2 * LANES - 4 * DN_HEADS), w_in.dtype)
    w_small = jnp.concatenate([sl("at_k"), sl("at_v"), sl("dn_a"), sl("dn_b"), pad], axis=1).astype(BF16)
    return w_main, w_small


def _dn_gates(small, B, T):
    ab = small[:, 2 * LANES:2 * LANES + 4 * DN_HEADS].reshape(B, T, 2, 2, DN_HEADS)
    col = jnp.transpose(ab, (3, 0, 1, 2, 4)).reshape(2, B, T, 2 * DN_HEADS)
    row = jnp.transpose(col.reshape(2, B, T // DN_CHUNK, DN_CHUNK, 2 * DN_HEADS), (0, 1, 2, 4, 3))
    return col, row


def kernel(x, c, ctx, c_ctx, w_mod, b_mod, norm_pre_mix, norm_post_mix, norm_pre_ffn, norm_post_ffn, w_in, sg_ln_g, sg_ln_b, sg_w, sg_b, dn_conv_w, dn_a_log, dn_dt_bias, dn_norm_g, at_sinks, w_proj_sg, w_proj_dn, w_proj_at, w_out, router_w, router_b, exp_w_gu, exp_b_gu, exp_w_down, exp_b_down):
    B, S, D = x.shape
    L = ctx.shape[1]
    depth = w_mod.shape[0]
    assert D == D_MODEL and S % GRID_W == 0
    n_lat, n_ctx = B * S, B * L

    rows = (B + 1 + 7) // 8 * 8
    cvec = jnp.zeros((rows, D), F32).at[:B].set(c).at[B].set(c_ctx)
    mod_all = _modulation(cvec, w_mod, b_mod)
    tables = _rope_tables(S)

    lat_row = lambda t: t // S
    ctx_row = lambda t: B
    all_row = lambda t: jnp.where(t < n_lat, t // S, B)

    xl = x.reshape(n_lat, D)
    xc = ctx.reshape(n_ctx, D)
    for l in range(depth):
        need_ctx_out = l < depth - 1
        mod = mod_all[l].reshape(rows, 1, 6 * D)
        w_main, w_small = _split_w_in(w_in[l])
        lw = {
            "dn_norm_g": dn_norm_g[l].reshape(1, -1),
            "norm_post_mix": norm_post_mix[l].reshape(1, -1),
            "norm_pre_ffn": norm_pre_ffn[l].reshape(1, -1),
            "norm_post_ffn": norm_post_ffn[l].reshape(1, -1),
            "w_proj_sg": w_proj_sg[l].astype(BF16), "w_proj_dn": w_proj_dn[l].astype(BF16),
            "w_proj_at": w_proj_at[l].astype(BF16), "w_out": w_out[l].astype(BF16),
            "router_w": jnp.pad(router_w[l], ((0, 0), (0, LANES - N_EXPERTS))).astype(BF16),
            "router_b": jnp.pad(router_b[l], (0, LANES - N_EXPERTS),
                                constant_values=NEG_BIG).reshape(1, -1),
            "layer": l,
            "exp_w_gu": exp_w_gu, "exp_b_gu": exp_b_gu.reshape(depth, N_EXPERTS, 1, -1),
            "exp_w_down": exp_w_down, "exp_b_down": exp_b_down.reshape(depth, N_EXPERTS, 1, -1),
        }
        pre_g = norm_pre_mix[l].reshape(1, -1)
        main, small = _inproj(xl, mod, lat_row, pre_g, w_main, w_small, min(1024, S))
        w_main_c = w_main if need_ctx_out else w_main[:, :N_CTX_MAIN_COLS]
        main_c, small_c = _inproj(xc, mod, ctx_row, pre_g, w_main_c, w_small, min(1024, n_ctx))

        sg_args = (sg_ln_g[l].reshape(1, -1), sg_ln_b[l].reshape(1, -1), sg_w[l].astype(BF16),
                   sg_b[l].T)
        ysg = _sgu(main, *sg_args)

        gcol_c, grow_c = _dn_gates(small_c, B, L)
        gcol, grow = _dn_gates(small, B, S)
        s0 = jnp.zeros((2, B, DN_HEADS, DN_DIM, DN_DIM), F32)
        of_c, ob_c, s_ctx = _deltanet(main_c.reshape(B, L, -1), gcol_c, grow_c, dn_conv_w[l], dn_a_log[l],
                                      dn_dt_bias[l], s0, need_ctx_out)
        of_l, ob_l, _ = _deltanet(main.reshape(B, S, -1), gcol, grow, dn_conv_w[l], dn_a_log[l],
                                  dn_dt_bias[l], s_ctx, True)

        sinks = jnp.repeat(at_sinks[l], AT_BLOCK).reshape(-1, 1)
        q_r, k_r, v_r = _rope(main, small, tables, S)
        yat = _attention_local(q_r, k_r, v_r, small_c, sinks, B, S, L)

        x_mid, h2, logits = _merge(ysg, of_l.reshape(n_lat, D), ob_l.reshape(n_lat, D), main, yat, xl, mod,
                                   lat_row, lw)
        if need_ctx_out:
            ysg_c = _sgu(main_c, *sg_args)
            yat_c = _attention_ctx(main_c, small_c, sinks, B, L)
            xc_mid, h2c, logits_c = _merge(ysg_c, of_c.reshape(n_ctx, D), ob_c.reshape(n_ctx, D), main_c,
                                           yat_c, xc, mod, ctx_row, lw)
            x_mid = jnp.concatenate([x_mid, xc_mid], axis=0)
            h2 = jnp.concatenate([h2, h2c], axis=0)
            logits = jnp.concatenate([logits, logits_c], axis=0)
            xo = _moe(h2, logits, x_mid, mod, all_row, lw)
            xl, xc = xo[:n_lat], xo[n_lat:]
        else:
            xl = _moe(h2, logits, x_mid, mod, lat_row, lw)
    return xl.reshape(B, S, D)
```

```python
import functools
import math

import jax
import jax.numpy as jnp
from jax import lax
from jax.experimental import pallas as pl
from jax.experimental.pallas import tpu as pltpu

F32 = jnp.float32
BF16 = jnp.bfloat16
I32 = jnp.int32
U32 = jnp.uint32

EPS = 1e-6
D_MODEL = 1024
GRID_W = 64

SG_CHUNK = 128
SG_GROUPS = 8

DN_HEADS = 8
DN_DIM = 128
DN_CONV = 5
DN_CHUNK = 64
DN_HALO = 16
DN_PREP_CHUNKS = 2
DN_SCAN_CHUNKS = 4

AT_Q_HEADS = 16
AT_KV_HEADS = 2
AT_DIM = 64
AT_BLOCK = 128
MERGE_ROW_GROUPS = 2
AT_STAGE_HEADS = 2
ROPE_BASE = 10000.0

N_EXPERTS = 32
TOP_K = 4
D_EXPERT = 1024
SWIGLU_ALPHA = 1.702
SWIGLU_LIMIT = 7.0
N_BRANCH = 3

LANES = 128
NEG_BIG = -1e30

COL_DN_K, COL_DN_V, COL_DN_Q, COL_DN_G, COL_AT_Q, COL_SG_U, COL_SG_V, COL_GATE0 = range(8)
N_MAIN_COLS = 10 * D_MODEL
N_CTX_MAIN_COLS = 2 * D_MODEL
N_SMALL_COLS = 3 * LANES

VMEM_LIMIT = 52 * 1024 * 1024


def _cparams(sem):
    return pltpu.CompilerParams(dimension_semantics=sem, vmem_limit_bytes=VMEM_LIMIT)


def _dot(a, b):
    return jnp.dot(a, b, preferred_element_type=F32)


def _dot_nt(a, b):
    return lax.dot_general(a, b, (((1,), (1,)), ((), ())), preferred_element_type=F32)


def _dot_tn(a, b):
    return lax.dot_general(a, b, (((0,), (0,)), ((), ())), preferred_element_type=F32)


def _sigmoid(x):
    return 0.5 * (1.0 + jnp.tanh(0.5 * x))


def _silu(x):
    return x * _sigmoid(x)


def _gelu_tanh(x):
    return 0.5 * x * (1.0 + jnp.tanh(math.sqrt(2.0 / math.pi) * (x + 0.044715 * (x * x * x))))


def _softplus(x):
    return jnp.maximum(x, 0.0) + jnp.log(1.0 + jnp.exp(-jnp.abs(x)))


def _rms(x, g):
    return x * lax.rsqrt(jnp.mean(x * x, axis=-1, keepdims=True) + EPS) * g


def _mod_kernel(c_ref, w_ref, b_ref, o_ref):
    s = _silu(c_ref[...])
    o_ref[...] = jnp.dot(s, w_ref[...], preferred_element_type=F32,
                         precision=lax.Precision.HIGHEST) + b_ref[...]


def _modulation(cvec, w_mod, b_mod):
    depth = w_mod.shape[0]
    rows = cvec.shape[0]
    n_col = w_mod.shape[2] // D_MODEL
    return pl.pallas_call(
        _mod_kernel,
        grid=(depth, n_col),
        in_specs=[pl.BlockSpec((rows, D_MODEL), lambda l, j: (0, 0)),
                  pl.BlockSpec((None, D_MODEL, D_MODEL), lambda l, j: (l, 0, j)),
                  pl.BlockSpec((None, 1, D_MODEL), lambda l, j: (l, 0, j))],
        out_specs=pl.BlockSpec((None, rows, D_MODEL), lambda l, j: (l, 0, j)),
        out_shape=jax.ShapeDtypeStruct((depth, rows, w_mod.shape[2]), F32),
        compiler_params=_cparams(("arbitrary", "arbitrary")),
        name="modulation",
    )(cvec, w_mod, b_mod.reshape(depth, 1, -1))


def _inproj_kernel(x_ref, mod_ref, g_ref, wm_ref, ws_ref, main_ref, small_ref, h_ref):
    @pl.when(pl.program_id(1) == 0)
    def _():
        sh = mod_ref[:, 0 * D_MODEL:1 * D_MODEL]
        sc = mod_ref[:, 1 * D_MODEL:2 * D_MODEL]
        h = (_rms(x_ref[...], g_ref[...]) * (1.0 + sc) + sh).astype(BF16)
        h_ref[...] = h
        small_ref[...] = _dot(h, ws_ref[...])

    main_ref[...] = _dot(h_ref[...], wm_ref[...]).astype(BF16)


def _inproj(x, mod, mod_row, norm_g, w_main, w_small, tm, tn=2048):
    n_tok = x.shape[0]
    n_main = w_main.shape[1]
    return pl.pallas_call(
        _inproj_kernel,
        grid=(n_tok // tm, n_main // tn),
        in_specs=[pl.BlockSpec((tm, D_MODEL), lambda i, j: (i, 0)),
                  pl.BlockSpec((None, 1, 6 * D_MODEL), lambda i, j: (mod_row(i * tm), 0, 0)),
                  pl.BlockSpec((1, D_MODEL), lambda i, j: (0, 0)),
                  pl.BlockSpec((D_MODEL, tn), lambda i, j: (0, j)),
                  pl.BlockSpec((D_MODEL, N_SMALL_COLS), lambda i, j: (0, 0))],
        out_specs=[pl.BlockSpec((tm, tn), lambda i, j: (i, j)),
                   pl.BlockSpec((tm, N_SMALL_COLS), lambda i, j: (i, 0))],
        out_shape=[jax.ShapeDtypeStruct((n_tok, n_main), BF16),
                   jax.ShapeDtypeStruct((n_tok, N_SMALL_COLS), F32)],
        scratch_shapes=[pltpu.VMEM((tm, D_MODEL), BF16)],
        compiler_params=_cparams(("arbitrary", "arbitrary")),
        name="inproj",
    )(x, mod, norm_g, w_main, w_small)


def _sgu_kernel(u_ref, v_ref, lng_ref, lnb_ref, ws_ref, bs_ref, o_ref, *, n_chunk):
    u = _gelu_tanh(u_ref[...].astype(F32))
    v = _gelu_tanh(v_ref[...].astype(F32))
    vc = v - jnp.mean(v, axis=-1, keepdims=True)
    var = jnp.mean(vc * vc, axis=-1, keepdims=True)
    vn = (vc * lax.rsqrt(var + EPS) * lng_ref[...] + lnb_ref[...]).astype(BF16)
    for n in range(n_chunk):
        rows = slice(n * SG_CHUNK, (n + 1) * SG_CHUNK)
        for g in range(SG_GROUPS):
            cols = slice(g * LANES, (g + 1) * LANES)
            mixed = _dot(ws_ref[g], vn[rows, cols]) + bs_ref[:, g:g + 1]
            o_ref[rows, cols] = (u[rows, cols] * mixed).astype(BF16)


def _sgu(main, sg_ln_g, sg_ln_b, sg_w, sg_bt, n_chunk=2):
    n_tok = main.shape[0]
    tc = n_chunk * SG_CHUNK
    return pl.pallas_call(
        functools.partial(_sgu_kernel, n_chunk=n_chunk),
        grid=(n_tok // tc,),
        in_specs=[pl.BlockSpec((tc, D_MODEL), lambda i: (i, COL_SG_U)),
                  pl.BlockSpec((tc, D_MODEL), lambda i: (i, COL_SG_V)),
                  pl.BlockSpec((1, D_MODEL), lambda i: (0, 0)),
                  pl.BlockSpec((1, D_MODEL), lambda i: (0, 0)),
                  pl.BlockSpec((SG_GROUPS, SG_CHUNK, SG_CHUNK), lambda i: (0, 0, 0)),
                  pl.BlockSpec((SG_CHUNK, SG_GROUPS), lambda i: (0, 0))],
        out_specs=pl.BlockSpec((tc, D_MODEL), lambda i: (i, 0)),
        out_shape=jax.ShapeDtypeStruct((n_tok, D_MODEL), BF16),
        compiler_params=_cparams(("arbitrary",)),
        name="sgu",
    )(main, main, sg_ln_g, sg_ln_b, sg_w, sg_bt)


def _dn_prep_kernel(*refs, with_q, n_chunks):
    if with_q:
        (qp_ref, qc_ref, qn_ref, kp_ref, kc_ref, kn_ref, vp_ref, vc_ref, vn_ref,
         gcol_ref, grow_ref, cw_ref, alog_r_ref, alog_c_ref, dtb_r_ref, dtb_c_ref,
         w_ref, u0_ref, ke_ref, gt_ref, qs_ref, qk_ref) = refs
    else:
        (kp_ref, kc_ref, kn_ref, vp_ref, vc_ref, vn_ref,
         gcol_ref, grow_ref, cw_ref, alog_r_ref, alog_c_ref, dtb_r_ref, dtb_c_ref,
         w_ref, u0_ref, ke_ref, gt_ref) = refs
    c = pl.program_id(1)
    C = DN_CHUNK
    R = DN_PREP_CHUNKS * C
    has_prev = (c > 0).astype(BF16)
    has_next = (c < n_chunks // DN_PREP_CHUNKS - 1).astype(BF16)

    pad = DN_CONV // 2
    n_sh = DN_CONV - 1
    sr = lax.broadcasted_iota(I32, (n_sh * R, R + 2 * DN_HALO), 0)
    sc = lax.broadcasted_iota(I32, (n_sh * R, R + 2 * DN_HALO), 1)
    blk = sr // R
    off = jnp.where(blk < pad, blk - pad, blk - pad + 1)
    shift_mat = (sc == DN_HALO + (sr - blk * R) + off).astype(BF16)

    def conv_silu(p_ref, c_ref, n_ref, part):
        cur = c_ref[...]
        ext = jnp.concatenate([p_ref[...] * has_prev, cur, n_ref[...] * has_next], axis=0)
        sh = _dot(shift_mat, ext)
        taps = [sh[j * R:(j + 1) * R] for j in range(pad)] + [cur.astype(F32)] + \
               [sh[j * R:(j + 1) * R] for j in range(pad, n_sh)]
        y = None
        for i in range(DN_CONV):
            t = taps[i] * cw_ref[i:i + 1, part * D_MODEL:(part + 1) * D_MODEL]
            y = t if y is None else y + t
        return _silu(y)

    k_all = conv_silu(kp_ref, kc_ref, kn_ref, 1)
    v_all = conv_silu(vp_ref, vc_ref, vn_ref, 2)
    q_all = conv_silu(qp_ref, qc_ref, qn_ref, 0) if with_q else None

    ri = lax.broadcasted_iota(I32, (C, C), 0)
    ci = lax.broadcasted_iota(I32, (C, C), 1)
    eye = (ri == ci).astype(F32)
    CC = range(DN_PREP_CHUNKS)
    H = range(DN_HEADS)
    CH = [(cc, h) for cc in CC for h in H]
    ch = {key: i for i, key in enumerate(CH)}
    rows = [slice(cc * C, (cc + 1) * C) for cc in CC]
    lanes = [slice(h * DN_DIM, (h + 1) * DN_DIM) for h in H]
    kh = [k_all[rows[cc], lanes[h]] for cc, h in CH]
    kh = [k * lax.rsqrt(jnp.sum(k * k, axis=-1, keepdims=True) + EPS) for k in kh]
    kb = [k.astype(BF16) for k in kh]
    vh = [v_all[rows[cc], lanes[h]] for cc, h in CH]
    if with_q:
        qh = [q_all[rows[cc], lanes[h]] for cc, h in CH]
        qh = [q * (lax.rsqrt(jnp.sum(q * q, axis=-1, keepdims=True) + EPS) * DN_DIM ** -0.5) for q in qh]
        gram = [_dot_nt(jnp.concatenate([kb[j], qh[j].astype(BF16)], axis=0), kb[j]) for j in range(len(CH))]
        kk = [g[:C] for g in gram]
        qk_raw = [g[C:] for g in gram]
    else:
        kk = [_dot_nt(k, k) for k in kb]

    D2 = range(2)
    DH = [(cc, d, h) for cc in CC for d in D2 for h in H]
    incl = [(ri >= ci), (ri <= ci)]
    strict = [(ri > ci), (ri < ci)]
    gam_col, gam_row, gam_tot, beta_col = {}, {}, {}, {}
    for cc in CC:
        for d in D2:
            gcol = gcol_ref[d, rows[cc], :]
            ld_col = -jnp.exp(alog_r_ref[d]) * _softplus(gcol[:, 0:DN_HEADS] + dtb_r_ref[d])
            ld_row = -jnp.exp(alog_c_ref[d]) * _softplus(grow_ref[d, cc][0:DN_HEADS, :] + dtb_c_ref[d])
            beta_col[cc, d] = _sigmoid(gcol[:, DN_HEADS:2 * DN_HEADS])
            gam_col[cc, d] = jnp.dot(incl[d].astype(F32), ld_col, preferred_element_type=F32,
                                     precision=lax.Precision.HIGHEST)
            gam_row[cc, d] = jnp.dot(ld_row, incl[1 - d].astype(F32), preferred_element_type=F32,
                                     precision=lax.Precision.HIGHEST)
            tot = jnp.sum(ld_col, axis=0, keepdims=True)
            gam_tot[cc, d] = tot
            gt_ref[d, cc] = tot
    gc = [gam_col[cc, d][:, h:h + 1] for cc, d, h in DH]
    bc = [beta_col[cc, d][:, h:h + 1] for cc, d, h in DH]
    decay = [jnp.exp(jnp.where(incl[d], gc[i] - gam_row[cc, d][h:h + 1, :], NEG_BIG))
             for i, (cc, d, h) in enumerate(DH)]
    x = [-(jnp.where(strict[d], decay[i], 0.0) * bc[i] * kk[ch[cc, h]]) for i, (cc, d, h) in enumerate(DH)]
    N = range(len(DH))
    p = [eye + x[i] for i in N]
    xb = [x[i].astype(BF16) for i in N]
    x = [_dot(xb[i], xb[i]) for i in N]
    n_fac = int(math.log2(C)) - 1
    for j in range(n_fac):
        xb = [x[i].astype(BF16) for i in N]
        if j < n_fac - 1:
            r = [_dot(xb[i], jnp.concatenate([xb[i], p[i].astype(BF16)], axis=1)) for i in N]
            x = [r[i][:, :C] for i in N]
            p = [p[i] + r[i][:, C:] for i in N]
        else:
            p = [p[i] + _dot(xb[i], p[i].astype(BF16)) for i in N]
    rhs = [jnp.concatenate([kh[ch[cc, h]] * (bc[i] * jnp.exp(gc[i])), vh[ch[cc, h]] * bc[i]],
                           axis=1).astype(BF16) for i, (cc, d, h) in enumerate(DH)]
    sol = [_dot(p[i].astype(BF16), rhs[i]) for i in N]
    for i, (cc, d, h) in enumerate(DH):
        j = ch[cc, h]
        w_ref[d, rows[cc], lanes[h]] = sol[i][:, :DN_DIM].astype(BF16)
        u0_ref[d, rows[cc], lanes[h]] = sol[i][:, DN_DIM:].astype(BF16)
        ke_ref[d, rows[cc], lanes[h]] = (kh[j] * jnp.exp(gam_tot[cc, d][:, h:h + 1] - gc[i])).astype(BF16)
        if with_q:
            qs_ref[d, rows[cc], lanes[h]] = (qh[j] * jnp.exp(gc[i])).astype(BF16)
            qk_ref[d, rows[cc], h * C:(h + 1) * C] = (qk_raw[j] * decay[i]).astype(BF16)


def _dn_scan_kernel(*refs, with_q, n_chunks):
    n_state = 2 * DN_HEADS
    s_refs = refs[-n_state:]
    refs = refs[:-n_state]
    if with_q:
        (w0, w1, u0, u1, k0, k1, g0, g1, qs0, qs1, qk0, qk1, s0_ref, o0_ref, o1_ref, sfin_ref) = refs
        qs_r, qk_r, o_r = (qs0, qs1), (qk0, qk1), (o0_ref, o1_ref)
    else:
        (w0, w1, u0, u1, k0, k1, g0, g1, s0_ref, sfin_ref) = refs
    w_r, u_r, k_r, g_r = (w0, w1), (u0, u1), (k0, k1), (g0, g1)
    c = pl.program_id(1)
    C = DN_CHUNK
    DH = [(d, h) for d in range(2) for h in range(DN_HEADS)]
    N = range(len(DH))
    lanes = [slice(h * DN_DIM, (h + 1) * DN_DIM) for h in range(DN_HEADS)]

    @pl.when(c == 0)
    def _():
        for i, (d, h) in enumerate(DH):
            s_refs[i][...] = s0_ref[d, h]

    s = [s_refs[i][...] for i in N]
    for sub in range(DN_SCAN_CHUNKS):
        cix = (sub, DN_SCAN_CHUNKS - 1 - sub)
        rows = [slice(cix[d] * C, (cix[d] + 1) * C) for d in range(2)]
        sb = [s[i].astype(BF16) for i in N]
        w = [w_r[d][rows[d], lanes[h]] for d, h in DH]
        if with_q:
            wq = [jnp.concatenate([w[i], qs_r[d][rows[d], lanes[h]]], axis=0) for i, (d, h) in enumerate(DH)]
            ws = [_dot(wq[i], sb[i]) for i in N]
            ub = [(u_r[d][rows[d], lanes[h]].astype(F32) - ws[i][:C]).astype(BF16)
                  for i, (d, h) in enumerate(DH)]
            qu = [_dot(qk_r[d][rows[d], h * C:(h + 1) * C], ub[i]) for i, (d, h) in enumerate(DH)]
            for i, (d, h) in enumerate(DH):
                o_r[d][rows[d], lanes[h]] = (ws[i][C:] + qu[i]).astype(BF16)
        else:
            ws = [_dot(w[i], sb[i]) for i in N]
            ub = [(u_r[d][rows[d], lanes[h]].astype(F32) - ws[i]).astype(BF16) for i, (d, h) in enumerate(DH)]
        ku = [_dot_tn(k_r[d][rows[d], lanes[h]], ub[i]) for i, (d, h) in enumerate(DH)]
        s = [jnp.exp(g_r[d][cix[d]][:, h:h + 1]) * s[i] + ku[i] for i, (d, h) in enumerate(DH)]
    for i in N:
        s_refs[i][...] = s[i]

    @pl.when(c == n_chunks // DN_SCAN_CHUNKS - 1)
    def _():
        for i, (d, h) in enumerate(DH):
            sfin_ref[d, h] = s_refs[i][...]


def _deltanet(main3, gate_col, gate_row, conv_w, alog, dtb, s0, with_q):
    B, T, _ = main3.shape
    C = DN_CHUNK
    n_chunks = T // C
    NC = DN_PREP_CHUNKS
    R = NC * C
    assert T % R == 0
    hpc = R // DN_HALO
    n_halo = T // DN_HALO

    def trio(col):
        return [pl.BlockSpec((None, DN_HALO, D_MODEL), lambda b, c: (b, jnp.maximum(c * hpc - 1, 0), col)),
                pl.BlockSpec((None, R, D_MODEL), lambda b, c: (b, c, col)),
                pl.BlockSpec((None, DN_HALO, D_MODEL),
                             lambda b, c: (b, jnp.minimum((c + 1) * hpc, n_halo - 1), col))]

    vec = lambda shape: pl.BlockSpec(shape, lambda b, c: (0,) * len(shape))
    in_specs = (trio(COL_DN_Q) if with_q else []) + trio(COL_DN_K) + trio(COL_DN_V) + [
        pl.BlockSpec((2, None, R, 2 * DN_HEADS), lambda b, c: (0, b, c, 0)),
        pl.BlockSpec((2, None, NC, 2 * DN_HEADS, C), lambda b, c: (0, b, c, 0, 0)),
        vec((DN_CONV, 3 * D_MODEL)),
        vec((2, 1, DN_HEADS)), vec((2, DN_HEADS, 1)), vec((2, 1, DN_HEADS)), vec((2, DN_HEADS, 1)),
    ]
    wide = lambda n: (pl.BlockSpec((2, None, R, n), lambda b, c: (0, b, c, 0)),
                      jax.ShapeDtypeStruct((2, B, T, n), BF16))
    outs = [wide(D_MODEL), wide(D_MODEL), wide(D_MODEL),
            (pl.BlockSpec((2, None, NC, 1, DN_HEADS), lambda b, c: (0, b, c, 0, 0)),
             jax.ShapeDtypeStruct((2, B, n_chunks, 1, DN_HEADS), F32))]
    if with_q:
        outs += [wide(D_MODEL), wide(DN_HEADS * C)]
    n_main = 3 if with_q else 2
    prep = pl.pallas_call(
        functools.partial(_dn_prep_kernel, with_q=with_q, n_chunks=n_chunks),
        grid=(B, n_chunks // NC),
        in_specs=in_specs, out_specs=[o[0] for o in outs], out_shape=[o[1] for o in outs],
        compiler_params=_cparams(("arbitrary", "arbitrary")),
        name="dn_prep_q" if with_q else "dn_prep",
    )(*([main3] * (3 * n_main)), gate_col, gate_row, conv_w,
      alog.reshape(2, 1, DN_HEADS), alog.reshape(2, DN_HEADS, 1),
      dtb.reshape(2, 1, DN_HEADS), dtb.reshape(2, DN_HEADS, 1))

    NS = DN_SCAN_CHUNKS
    RS = NS * C
    n_steps = n_chunks // NS
    assert n_chunks % NS == 0

    def both_dirs(arr, n):
        if n is None:
            return [pl.BlockSpec((None, None, NS, 1, DN_HEADS), lambda b, c: (0, b, c, 0, 0)),
                    pl.BlockSpec((None, None, NS, 1, DN_HEADS),
                                 lambda b, c: (1, b, n_steps - 1 - c, 0, 0))], [arr, arr]
        return [pl.BlockSpec((None, None, RS, n), lambda b, c: (0, b, c, 0)),
                pl.BlockSpec((None, None, RS, n), lambda b, c: (1, b, n_steps - 1 - c, 0))], [arr, arr]

    specs, args = [], []
    widths = [D_MODEL, D_MODEL, D_MODEL, None] + ([D_MODEL, DN_HEADS * C] if with_q else [])
    for arr, n in zip(prep, widths):
        sp, ar = both_dirs(arr, n)
        specs += sp
        args += ar
    s_spec = pl.BlockSpec((2, None, DN_HEADS, DN_DIM, DN_DIM), lambda b, c: (0, b, 0, 0, 0))
    s_shape = jax.ShapeDtypeStruct((2, B, DN_HEADS, DN_DIM, DN_DIM), F32)
    if with_q:
        out_specs = [pl.BlockSpec((None, RS, D_MODEL), lambda b, c: (b, c, 0)),
                     pl.BlockSpec((None, RS, D_MODEL), lambda b, c: (b, n_steps - 1 - c, 0)), s_spec]
        out_shape = [jax.ShapeDtypeStruct((B, T, D_MODEL), BF16)] * 2 + [s_shape]
    else:
        out_specs, out_shape = [s_spec], [s_shape]
    out = pl.pallas_call(
        functools.partial(_dn_scan_kernel, with_q=with_q, n_chunks=n_chunks),
        grid=(B, n_steps),
        in_specs=specs + [s_spec], out_specs=out_specs, out_shape=out_shape,
        scratch_shapes=[pltpu.VMEM((DN_DIM, DN_DIM), F32)] * (2 * DN_HEADS),
        compiler_params=_cparams(("arbitrary", "arbitrary")),
        name="dn_scan_q" if with_q else "dn_scan",
    )(*args, s0)
    return (out[0], out[1], out[2]) if with_q else (None, None, out[0])


def _rope_tables(S):
    half = AT_DIM // 2
    nf = half // 2
    inv_freq = ROPE_BASE ** (-jnp.arange(nf, dtype=F32) / nf)
    t = jnp.arange(S, dtype=jnp.int32)
    row = (t // GRID_W).astype(F32)
    col = (t % GRID_W).astype(F32)
    lane = jnp.arange(LANES)
    dd = lane % AT_DIM
    pos = jnp.where((dd < half)[None, :], row[:, None], col[:, None])
    ang = pos * inv_freq[lane % nf][None, :]
    first = ((lane % half) < nf)[None, :]
    sin = jnp.sin(ang)
    return jnp.cos(ang), jnp.where(first, -sin, 0.0), jnp.where(first, 0.0, sin)


def _rope_kernel(q_ref, k_ref, v_ref, cos_ref, sa_ref, sb_ref, qo_ref, ko_ref, vo_ref):
    cos, sa, sb = cos_ref[...], sa_ref[...], sb_ref[...]
    nf = AT_DIM // 4

    def rot(x):
        return x * cos + pltpu.roll(x, LANES - nf, 1) * sa + pltpu.roll(x, nf, 1) * sb

    for j in range(AT_Q_HEADS * AT_DIM // LANES):
        lanes = slice(j * LANES, (j + 1) * LANES)
        qo_ref[:, lanes] = (rot(q_ref[:, lanes].astype(F32)) * AT_DIM ** -0.5).astype(BF16)
    ko_ref[...] = rot(k_ref[...]).astype(BF16)
    vo_ref[...] = v_ref[...].astype(BF16)


def _rope(main, small, tables, S, tm=512):
    n_tok = main.shape[0]
    per_seq = S // tm
    tab_spec = pl.BlockSpec((tm, LANES), lambda i: (i % per_seq, 0))
    return pl.pallas_call(
        _rope_kernel,
        grid=(n_tok // tm,),
        in_specs=[pl.BlockSpec((tm, D_MODEL), lambda i: (i, COL_AT_Q)),
                  pl.BlockSpec((tm, LANES), lambda i: (i, 0)),
                  pl.BlockSpec((tm, LANES), lambda i: (i, 1)),
                  tab_spec, tab_spec, tab_spec],
        out_specs=[pl.BlockSpec((tm, D_MODEL), lambda i: (i, 0)),
                   pl.BlockSpec((tm, LANES), lambda i: (i, 0)),
                   pl.BlockSpec((tm, LANES), lambda i: (i, 0))],
        out_shape=[jax.ShapeDtypeStruct((n_tok, D_MODEL), BF16),
                   jax.ShapeDtypeStruct((n_tok, LANES), BF16),
                   jax.ShapeDtypeStruct((n_tok, LANES), BF16)],
        compiler_params=_cparams(("arbitrary",)),
        name="rope",
    )(main, small, small, *tables)


def _attn_kernel(*refs, local, n_blocks, q_scale):
    if local:
        (q_ref, kp_ref, kc_ref, kn_ref, vp_ref, vc_ref, vn_ref, kx_ref, vx_ref, sink_ref, o_ref) = refs
    else:
        (q_ref, kx_ref, vx_ref, sink_ref, o_ref) = refs
    P = AT_BLOCK
    G = AT_Q_HEADS // AT_KV_HEADS
    kx = kx_ref[...].astype(BF16)
    vx = vx_ref[...].astype(BF16)
    if local:
        i = pl.program_id(1)
        k_all = jnp.concatenate([kp_ref[...], kc_ref[...], kn_ref[...], kx], axis=0)
        v_all = jnp.concatenate([vp_ref[...], vc_ref[...], vn_ref[...], vx], axis=0)
        qi = lax.broadcasted_iota(I32, (P, P), 0)
        kj = lax.broadcasted_iota(I32, (P, P), 1)
        b_prev = jnp.where(kj >= qi, 0.0, NEG_BIG) + jnp.where(i > 0, 0.0, NEG_BIG)
        b_next = jnp.where(kj <= qi, 0.0, NEG_BIG) + jnp.where(i < n_blocks - 1, 0.0, NEG_BIG)
        b_prev2 = jnp.concatenate([b_prev] * AT_STAGE_HEADS, axis=0)
        b_next2 = jnp.concatenate([b_next] * AT_STAGE_HEADS, axis=0)
    else:
        k_all, v_all = kx, vx
    lo = lax.broadcasted_iota(I32, (P, LANES), 1) < AT_DIM
    qf = q_ref[...].astype(F32) * q_scale
    pieces = []
    for qh in range(AT_Q_HEADS):
        blk = qf[:, (qh // 2) * LANES:(qh // 2 + 1) * LANES]
        want_lo = qh // G == 0
        if want_lo != (qh % 2 == 0):
            blk = pltpu.roll(blk, AT_DIM, 1)
        pieces.append(jnp.where(lo if want_lo else ~lo, blk, 0.0).astype(BF16))
    HS = AT_STAGE_HEADS
    n_pair = AT_Q_HEADS // HS

    def logits(j):
        s = _dot_nt(jnp.concatenate(pieces[HS * j:HS * (j + 1)], axis=0), k_all)
        if not local:
            return s
        return jnp.concatenate([s[:, 0:P] + b_prev2, s[:, P:2 * P], s[:, 2 * P:3 * P] + b_next2,
                                s[:, 3 * P:]], axis=1)

    def softmax(j, s):
        sink = sink_ref[HS * j * P:HS * (j + 1) * P, :]
        m = jnp.maximum(jnp.max(s, axis=-1, keepdims=True), sink)
        p = jnp.exp(s - m)
        den = jnp.sum(p, axis=-1, keepdims=True) + jnp.exp(sink - m)
        return p.astype(BF16), den

    def values(j, p, den):
        o = _dot(p, v_all) / den
        for t in range(HS // 2):
            blk = (HS * j) // 2 + t
            a, b = o[2 * t * P:(2 * t + 1) * P], o[(2 * t + 1) * P:(2 * t + 2) * P]
            if (2 * blk) // G == 0:
                out = jnp.where(lo, a, pltpu.roll(b, AT_DIM, 1))
            else:
                out = jnp.where(lo, pltpu.roll(a, AT_DIM, 1), b)
            o_ref[:, blk * LANES:(blk + 1) * LANES] = out.astype(BF16)

    s_next = logits(0)
    prob = None
    for j in range(n_pair):
        s_cur = s_next
        if j + 1 < n_pair:
            s_next = logits(j + 1)
        done = prob
        prob = softmax(j, s_cur)
        if done is not None:
            values(j - 1, *done)
    values(n_pair - 1, *prob)


def _attention_local(q_r, k_r, v_r, small_c, sinks, B, S, L):
    P = AT_BLOCK
    nb = S // P

    def kv_trio():
        return [pl.BlockSpec((P, LANES), lambda b, i: (b * nb + jnp.maximum(i - 1, 0), 0)),
                pl.BlockSpec((P, LANES), lambda b, i: (b * nb + i, 0)),
                pl.BlockSpec((P, LANES), lambda b, i: (b * nb + jnp.minimum(i + 1, nb - 1), 0))]

    return pl.pallas_call(
        functools.partial(_attn_kernel, local=True, n_blocks=nb, q_scale=1.0),
        grid=(B, nb),
        in_specs=[pl.BlockSpec((P, D_MODEL), lambda b, i: (b * nb + i, 0))] + kv_trio() + kv_trio() + [
            pl.BlockSpec((L, LANES), lambda b, i: (b, 0)),
            pl.BlockSpec((L, LANES), lambda b, i: (b, 1)),
            pl.BlockSpec((AT_Q_HEADS * AT_BLOCK, 1), lambda b, i: (0, 0))],
        out_specs=pl.BlockSpec((P, D_MODEL), lambda b, i: (b * nb + i, 0)),
        out_shape=jax.ShapeDtypeStruct((B * S, D_MODEL), BF16),
        compiler_params=_cparams(("arbitrary", "arbitrary")),
        name="attn_local",
    )(q_r, k_r, k_r, k_r, v_r, v_r, v_r, small_c, small_c, sinks)


def _attention_ctx(main_c, small_c, sinks, B, L):
    P = AT_BLOCK
    nb = L // P
    return pl.pallas_call(
        functools.partial(_attn_kernel, local=False, n_blocks=nb, q_scale=AT_DIM ** -0.5),
        grid=(B, nb),
        in_specs=[pl.BlockSpec((P, D_MODEL), lambda b, i: (b * nb + i, COL_AT_Q)),
                  pl.BlockSpec((L, LANES), lambda b, i: (b, 0)),
                  pl.BlockSpec((L, LANES), lambda b, i: (b, 1)),
                  pl.BlockSpec((AT_Q_HEADS * AT_BLOCK, 1), lambda b, i: (0, 0))],
        out_specs=pl.BlockSpec((P, D_MODEL), lambda b, i: (b * nb + i, 0)),
        out_shape=jax.ShapeDtypeStruct((B * L, D_MODEL), BF16),
        compiler_params=_cparams(("arbitrary", "arbitrary")),
        name="attn_ctx",
    )(main_c, small_c, small_c, sinks)


def _merge_kernel(ysg_ref, of_ref, ob_ref, dng_ref, yat_ref, g0_ref, g1_ref, g2_ref, x_ref, mod_ref,
                  dn_norm_ref, post_ref, pre_ref, wsg_ref, wdn_ref, wat_ref, wout_ref, rw_ref, rb_ref,
                  xo_ref, h2_ref, lg_ref):
    dn_g = dn_norm_ref[...]
    gate1 = mod_ref[:, 2 * D_MODEL:3 * D_MODEL]
    sh2 = mod_ref[:, 3 * D_MODEL:4 * D_MODEL]
    sc2 = mod_ref[:, 4 * D_MODEL:5 * D_MODEL]
    n_grp = MERGE_ROW_GROUPS
    grp = x_ref.shape[0] // n_grp

    def dn_out(r):
        rows = slice(r * grp, (r + 1) * grp)
        o = of_ref[rows, :].astype(F32) + ob_ref[rows, :].astype(F32)
        parts = []
        for h in range(DN_HEADS):
            lanes = slice(h * DN_DIM, (h + 1) * DN_DIM)
            parts.append(_rms(o[:, lanes], dn_g) * _silu(dng_ref[rows, lanes].astype(F32)))
        return jnp.concatenate(parts, axis=1).astype(BF16)

    def branches(r, ydn):
        rows = slice(r * grp, (r + 1) * grp)
        m = (_sigmoid(g0_ref[rows, :].astype(F32)) * _dot(ysg_ref[rows, :], wsg_ref[...])
             + _sigmoid(g1_ref[rows, :].astype(F32)) * _dot(ydn, wdn_ref[...])
             + _sigmoid(g2_ref[rows, :].astype(F32)) * _dot(yat_ref[rows, :], wat_ref[...]))
        return m.astype(BF16)

    def out_proj(r, m):
        return _dot(m, wout_ref[...])

    def residual(r, y):
        rows = slice(r * grp, (r + 1) * grp)
        xn = x_ref[rows, :] + gate1 * _rms(y, post_ref[...])
        xo_ref[rows, :] = xn
        h2 = _rms(xn, pre_ref[...]) * (1.0 + sc2) + sh2
        h2_ref[rows, :] = h2
        lg_ref[rows, :] = _dot(h2.astype(BF16), rw_ref[...]) + rb_ref[...]

    stages = (dn_out, branches, out_proj, residual)
    carried = {}
    for t in range(n_grp + len(stages) - 1):
        for k in reversed(range(len(stages))):
            r = t - k
            if 0 <= r < n_grp:
                carried[r] = stages[k](r) if k == 0 else stages[k](r, carried[r])


def _merge(ysg, o_fwd, o_bwd, main, yat, x, mod, mod_row, lw, tm=512):
    n_tok = x.shape[0]
    const = lambda i: (0, 0)
    wspec = pl.BlockSpec((D_MODEL, D_MODEL), const, pipeline_mode=pl.Buffered(1))
    vspec = pl.BlockSpec((1, D_MODEL), const)
    return pl.pallas_call(
        _merge_kernel,
        grid=(n_tok // tm,),
        in_specs=[pl.BlockSpec((tm, D_MODEL), lambda i: (i, 0)),
                  pl.BlockSpec((tm, D_MODEL), lambda i: (i, 0)),
                  pl.BlockSpec((tm, D_MODEL), lambda i: (i, 0)),
                  pl.BlockSpec((tm, D_MODEL), lambda i: (i, COL_DN_G)),
                  pl.BlockSpec((tm, D_MODEL), lambda i: (i, 0)),
                  pl.BlockSpec((tm, D_MODEL), lambda i: (i, COL_GATE0)),
                  pl.BlockSpec((tm, D_MODEL), lambda i: (i, COL_GATE0 + 1)),
                  pl.BlockSpec((tm, D_MODEL), lambda i: (i, COL_GATE0 + 2)),
                  pl.BlockSpec((tm, D_MODEL), lambda i: (i, 0)),
                  pl.BlockSpec((None, 1, 6 * D_MODEL), lambda i: (mod_row(i * tm), 0, 0)),
                  pl.BlockSpec((1, DN_DIM), const), vspec, vspec,
                  wspec, wspec, wspec, wspec,
                  pl.BlockSpec((D_MODEL, LANES), const), pl.BlockSpec((1, LANES), const)],
        out_specs=[pl.BlockSpec((tm, D_MODEL), lambda i: (i, 0)),
                   pl.BlockSpec((tm, D_MODEL), lambda i: (i, 0)),
                   pl.BlockSpec((tm, LANES), lambda i: (i, 0))],
        out_shape=[jax.ShapeDtypeStruct((n_tok, D_MODEL), F32),
                   jax.ShapeDtypeStruct((n_tok, D_MODEL), F32),
                   jax.ShapeDtypeStruct((n_tok, LANES), F32)],
        compiler_params=_cparams(("arbitrary",)),
        name="merge",
    )(ysg, o_fwd, o_bwd, main, yat, main, main, main, x, mod,
      lw["dn_norm_g"], lw["norm_post_mix"], lw["norm_pre_ffn"],
      lw["w_proj_sg"], lw["w_proj_dn"], lw["w_proj_at"], lw["w_out"], lw["router_w"], lw["router_b"])


MOE_TOK = 256
MOE_PIECE = 8
MOE_BUF = MOE_TOK * TOP_K + N_EXPERTS * MOE_PIECE
MOE_META = 256
MOE_DMA_PIECES = (4, 2, 1)
MOE_META_TABLES = []
_off = len(MOE_DMA_PIECES)
for _m in MOE_DMA_PIECES:
    _len = MOE_BUF // (_m * MOE_PIECE) if _m == MOE_DMA_PIECES[0] else N_EXPERTS
    MOE_META_TABLES.append((_m, _len, _off))
    _off += 2 * _len
assert MOE_META >= _off


def _route_kernel(lg_ref, gate_ref, lpos_ref, tcnt_ref):
    tm = lg_ref.shape[0]
    l = lg_ref[...]
    lane = lax.broadcasted_iota(I32, l.shape, 1).astype(F32)
    vals, onehots = [], []
    for k in range(TOP_K):
        m = jnp.max(l, axis=-1, keepdims=True)
        ik = jnp.min(jnp.where(l == m, lane, float(LANES)), axis=-1, keepdims=True)
        oh = lane == ik
        vals.append(m)
        onehots.append(oh)
        l = jnp.where(oh, -jnp.inf, l)
    es = [jnp.exp(v - vals[0]) for v in vals]
    den = es[0] + es[1] + es[2] + es[3]
    sel = jnp.zeros(l.shape, F32)
    for k in range(TOP_K):
        gate_ref[:, k:k + 1] = es[k] / den
        sel = sel + onehots[k].astype(F32)
    ri = lax.broadcasted_iota(I32, (tm, tm), 0)
    ci = lax.broadcasted_iota(I32, (tm, tm), 1)
    before = _dot((ri > ci).astype(BF16), sel.astype(BF16))
    tcnt = jnp.sum(sel, axis=0, keepdims=True)
    tcnt_ref[...] = tcnt
    n_piece = jnp.floor((tcnt + (MOE_PIECE - 1)) * (1.0 / MOE_PIECE))
    ei = lax.broadcasted_iota(I32, (LANES, LANES), 0)
    ej = lax.broadcasted_iota(I32, (LANES, LANES), 1)
    run_start = _dot(jnp.broadcast_to(n_piece, (8, LANES)).astype(BF16),
                     (ei < ej).astype(BF16))[0:1] * float(MOE_PIECE)
    pos = before + run_start
    for k in range(TOP_K):
        lpos_ref[:, k:k + 1] = jnp.sum(jnp.where(onehots[k], pos, 0.0), axis=-1,
                                       keepdims=True).astype(I32)


def _route(logits):
    n_tok = logits.shape[0]
    tm = MOE_TOK
    n_t = n_tok // tm
    small = lambda dt: jax.ShapeDtypeStruct((n_tok, TOP_K), dt)
    kspec = pl.BlockSpec((tm, TOP_K), lambda i: (i, 0))
    tspec = pl.BlockSpec((None, 1, LANES), lambda i: (i, 0, 0))
    tshape = jax.ShapeDtypeStruct((n_t, 1, LANES), F32)
    return pl.pallas_call(
        _route_kernel,
        grid=(n_t,),
        in_specs=[pl.BlockSpec((tm, LANES), lambda i: (i, 0))],
        out_specs=[kspec, kspec, tspec],
        out_shape=[small(F32), small(I32), tshape],
        compiler_params=_cparams(("arbitrary",)),
        name="route",
    )(logits)


def _run_copies(meta_ref, base, src_of, dst_of, sem, start):
    def table(first, length, rows):
        def per_dma(q, carry):
            local = pl.multiple_of(meta_ref[base + first + q], MOE_PIECE)
            slot = pl.multiple_of(meta_ref[base + first + length + q], MOE_PIECE)
            cp = pltpu.make_async_copy(src_of(local, slot, rows), dst_of(local, slot, rows), sem)
            if start:
                cp.start()
            else:
                cp.wait()
            return carry
        return per_dma

    for k, (mult, length, first) in enumerate(MOE_META_TABLES):
        lax.fori_loop(0, meta_ref[base + k], table(first, length, mult * MOE_PIECE), 0)


def _dispatch_kernel(meta_ref, prev_ref, zmeta_ref, lpos_ref, h_ref, xs_ref, buf_ref, buf1_ref, zero_ref,
                     sem, sem1, *, tm_e):
    tm = MOE_TOK
    step = pl.program_id(0)
    bufs, sems = (buf_ref, buf1_ref), (sem, sem1)
    rows = lambda ref, r, n=MOE_PIECE: ref.at[pl.ds(r, n)]

    @pl.when(step == 0)
    def _():
        zero_ref[...] = jnp.zeros_like(zero_ref)

        def zero_tail(start):
            def per_expert(e, carry):
                z0 = pl.multiple_of(zmeta_ref[e], MOE_PIECE)

                def per_piece(p, c2):
                    cp = pltpu.make_async_copy(zero_ref, rows(xs_ref, z0 + p * MOE_PIECE), sem)
                    if start:
                        cp.start()
                    else:
                        cp.wait()
                    return c2

                return lax.fori_loop(0, zmeta_ref[N_EXPERTS + e], per_piece, carry)

            lax.fori_loop(0, N_EXPERTS, per_expert, 0)

        zero_tail(True)
        zero_tail(False)

        buf_ref[0:tm_e, :] = jnp.zeros((tm_e, D_MODEL // 2), U32)

        def zero_tiles(start):
            def per_tile(p, carry):
                t0 = pl.multiple_of((zmeta_ref[2 * N_EXPERTS] + p) * tm_e, tm_e)
                cp = pltpu.make_async_copy(buf_ref.at[pl.ds(0, tm_e)], xs_ref.at[pl.ds(t0, tm_e)], sem)
                if start:
                    cp.start()
                else:
                    cp.wait()
                return carry

            lax.fori_loop(0, zmeta_ref[2 * N_EXPERTS + 1], per_tile, 0)

        zero_tiles(True)
        zero_tiles(False)

    def group(j):
        s_iota = lax.broadcasted_iota(I32, (MOE_BUF, tm), 0)
        perm = jnp.zeros((MOE_BUF, tm), F32)
        for k in range(TOP_K):
            perm = jnp.where(s_iota == lpos_ref[k:k + 1, j * tm:(j + 1) * tm], 1.0, perm)
        bufs[j][...] = _pack_bf16_pairs(_dot(perm.astype(BF16), h_ref[j * tm:(j + 1) * tm, :].astype(BF16)))

    def copies(mref, j, start):
        _run_copies(mref, j * MOE_META, lambda loc, slot, n: rows(bufs[j], loc, n),
                    lambda loc, slot, n: rows(xs_ref, slot, n), sems[j], start)

    group(0)
    copies(meta_ref, 0, True)

    @pl.when(step > 0)
    def _():
        copies(prev_ref, 1, False)

    group(1)
    copies(meta_ref, 1, True)
    copies(meta_ref, 0, False)

    @pl.when(step == pl.num_programs(0) - 1)
    def _():
        copies(meta_ref, 1, False)


def _dispatch(meta, zmeta, lpos_t, h2, n_slots, tm_e):
    n_tok = h2.shape[0]
    tm = 2 * MOE_TOK
    assert tm_e <= MOE_BUF and n_tok % tm == 0
    return pl.pallas_call(
        functools.partial(_dispatch_kernel, tm_e=tm_e),
        grid=(n_tok // tm,),
        in_specs=[pl.BlockSpec((2 * MOE_META,), lambda i: (i,), memory_space=pltpu.SMEM),
                  pl.BlockSpec((2 * MOE_META,), lambda i: (jnp.maximum(i - 1, 0),), memory_space=pltpu.SMEM),
                  pl.BlockSpec((MOE_META,), lambda i: (0,), memory_space=pltpu.SMEM),
                  pl.BlockSpec((TOP_K, tm), lambda i: (0, i)),
                  pl.BlockSpec((tm, D_MODEL), lambda i: (i, 0))],
        out_specs=pl.BlockSpec(memory_space=pl.ANY),
        out_shape=jax.ShapeDtypeStruct((n_slots, D_MODEL // 2), U32),
        scratch_shapes=[pltpu.VMEM((MOE_BUF, D_MODEL // 2), U32), pltpu.VMEM((MOE_BUF, D_MODEL // 2), U32),
                        pltpu.VMEM((MOE_PIECE, D_MODEL // 2), U32),
                        pltpu.SemaphoreType.DMA, pltpu.SemaphoreType.DMA],
        compiler_params=_cparams(("arbitrary",)),
        name="moe_dispatch",
    )(meta, meta, zmeta, lpos_t, h2)


def _pack_bf16_pairs(x):
    w = x.shape[1] // 2
    xb = x.astype(BF16).astype(F32)
    lo = lax.shift_right_logical(lax.bitcast_convert_type(xb[:, :w], U32), jnp.uint32(16))
    hi = lax.bitcast_convert_type(xb[:, w:], U32) & jnp.uint32(0xFFFF0000)
    return hi | lo


def _unpack_bf16_pairs(p):
    lo = lax.bitcast_convert_type(lax.shift_left(p, jnp.uint32(16)), F32)
    hi = lax.bitcast_convert_type(p & jnp.uint32(0xFFFF0000), F32)
    return jnp.concatenate([lo, hi], axis=1).astype(BF16)


def _expert_kernel(te_ref, first_ref, nu_ref, xs_ref, wgu_ref, bgu_ref, wd_ref, bd_ref, y_ref,
                   wgu_b_ref, wd_b_ref):
    del te_ref
    i = pl.program_id(0)

    @pl.when(first_ref[i] == 1)
    def _():
        wgu_b_ref[...] = wgu_ref[...].astype(BF16)
        wd_b_ref[...] = wd_ref[...].astype(BF16)

    @pl.when(i < nu_ref[0])
    def _():
        gu = _dot(_unpack_bf16_pairs(xs_ref[...]), wgu_b_ref[...]) + bgu_ref[...]
        g = jnp.minimum(gu[:, :D_EXPERT], SWIGLU_LIMIT)
        lin = jnp.clip(gu[:, D_EXPERT:], -SWIGLU_LIMIT, SWIGLU_LIMIT)
        act = g * _sigmoid(SWIGLU_ALPHA * g) * (lin + 1.0)
        y_ref[...] = _pack_bf16_pairs(_dot(act.astype(BF16), wd_b_ref[...]) + bd_ref[...])

    @pl.when(i >= nu_ref[0])
    def _():
        y_ref[...] = jnp.zeros_like(y_ref)


def _experts(tile_expert, n_used, xs, wgu, bgu, wd, bd, layer, tm):
    n_slots = xs.shape[0]
    n_tiles = n_slots // tm
    first = jnp.concatenate([jnp.ones((1,), I32),
                             (tile_expert[1:] != tile_expert[:-1]).astype(I32)])

    def row(i, te, fi, nu):
        return (jnp.minimum(i, nu[0] - 1), 0)

    grid_spec = pltpu.PrefetchScalarGridSpec(
        num_scalar_prefetch=3,
        grid=(n_tiles,),
        in_specs=[pl.BlockSpec((tm, D_MODEL // 2), row),
                  pl.BlockSpec((None, None, D_MODEL, 2 * D_EXPERT), lambda i, te, fi, nu: (layer, te[i], 0, 0)),
                  pl.BlockSpec((None, None, 1, 2 * D_EXPERT), lambda i, te, fi, nu: (layer, te[i], 0, 0)),
                  pl.BlockSpec((None, None, D_EXPERT, D_MODEL), lambda i, te, fi, nu: (layer, te[i], 0, 0)),
                  pl.BlockSpec((None, None, 1, D_MODEL), lambda i, te, fi, nu: (layer, te[i], 0, 0))],
        out_specs=pl.BlockSpec((tm, D_MODEL // 2), lambda i, te, fi, nu: (i, 0)),
        scratch_shapes=[pltpu.VMEM((D_MODEL, 2 * D_EXPERT), BF16), pltpu.VMEM((D_EXPERT, D_MODEL), BF16)],
    )
    return pl.pallas_call(
        _expert_kernel,
        grid_spec=grid_spec,
        out_shape=jax.ShapeDtypeStruct((n_slots, D_MODEL // 2), U32),
        compiler_params=_cparams(("arbitrary",)),
        name="moe_experts",
    )(tile_expert, first, n_used, xs, wgu, bgu, wd, bd)


def _combine_kernel(meta_ref, next_ref, lpos_ref, gate_ref, x_ref, mod_ref, post_ref, y_ref, xo_ref,
                    buf_ref, buf1_ref, sem, sem1):
    tm = MOE_TOK
    step = pl.program_id(0)
    bufs, sems = (buf_ref, buf1_ref), (sem, sem1)
    rows = lambda ref, r, n=MOE_PIECE: ref.at[pl.ds(r, n)]

    def copies(mref, j, base, start):
        _run_copies(mref, base, lambda loc, slot, n: rows(y_ref, slot, n),
                    lambda loc, slot, n: rows(bufs[j], loc, n), sems[j], start)

    def reduce_tile(j):
        tok = slice(j * tm, (j + 1) * tm)
        s_iota = lax.broadcasted_iota(I32, (tm, MOE_BUF), 1)
        sel = jnp.zeros((tm, MOE_BUF), F32)
        for k in range(TOP_K):
            sel = jnp.where(s_iota == lpos_ref[tok, k:k + 1], gate_ref[tok, k:k + 1], sel)
        y = _dot(sel.astype(BF16), _unpack_bf16_pairs(bufs[j][...]))
        gate2 = mod_ref[:, 5 * D_MODEL:6 * D_MODEL]
        xo_ref[tok, :] = x_ref[tok, :] + gate2 * _rms(y, post_ref[...])

    @pl.when(step == 0)
    def _():
        buf_ref[...] = jnp.zeros_like(buf_ref)
        buf1_ref[...] = jnp.zeros_like(buf1_ref)
        copies(meta_ref, 0, 0, True)

    copies(meta_ref, 1, MOE_META, True)
    copies(meta_ref, 0, 0, False)
    reduce_tile(0)

    @pl.when(step < pl.num_programs(0) - 1)
    def _():
        copies(next_ref, 0, 0, True)

    copies(meta_ref, 1, MOE_META, False)
    reduce_tile(1)


def _combine(meta, lpos, gate, x_mid, mod, mod_row, post_g, y):
    n_tok = x_mid.shape[0]
    tm = 2 * MOE_TOK
    n_steps = n_tok // tm
    return pl.pallas_call(
        _combine_kernel,
        grid=(n_steps,),
        in_specs=[pl.BlockSpec((2 * MOE_META,), lambda i: (i,), memory_space=pltpu.SMEM),
                  pl.BlockSpec((2 * MOE_META,), lambda i: (jnp.minimum(i + 1, n_steps - 1),),
                               memory_space=pltpu.SMEM),
                  pl.BlockSpec((tm, TOP_K), lambda i: (i, 0)),
                  pl.BlockSpec((tm, TOP_K), lambda i: (i, 0)),
                  pl.BlockSpec((tm, D_MODEL), lambda i: (i, 0)),
                  pl.BlockSpec((None, 1, 6 * D_MODEL), lambda i: (mod_row(i * tm), 0, 0)),
                  pl.BlockSpec((1, D_MODEL), lambda i: (0, 0)),
                  pl.BlockSpec(memory_space=pl.ANY)],
        out_specs=pl.BlockSpec((tm, D_MODEL), lambda i: (i, 0)),
        out_shape=jax.ShapeDtypeStruct((n_tok, D_MODEL), F32),
        scratch_shapes=[pltpu.VMEM((MOE_BUF, D_MODEL // 2), U32), pltpu.VMEM((MOE_BUF, D_MODEL // 2), U32),
                        pltpu.SemaphoreType.DMA, pltpu.SemaphoreType.DMA],
        compiler_params=_cparams(("arbitrary",)),
        name="moe_combine",
    )(meta, meta, lpos, gate, x_mid, mod, post_g, y)


def _moe(h2, logits, x_mid, mod, mod_row, lw, tm_e=512):
    n_tok = h2.shape[0]
    n_t = n_tok // MOE_TOK
    gate, lpos, tcnt = _route(logits)
    tcnt = tcnt[:, 0, :N_EXPERTS].astype(I32)
    pieces = (tcnt + MOE_PIECE - 1) // MOE_PIECE
    run_end = jnp.cumsum(pieces, axis=0) * MOE_PIECE
    used = run_end[-1]
    padded = (used + tm_e - 1) // tm_e * tm_e
    pad_end = jnp.cumsum(padded)
    offs = pad_end - padded
    n_tiles = (n_tok * TOP_K + n_t * N_EXPERTS * (MOE_PIECE - 1) + tm_e - 1) // tm_e + N_EXPERTS
    tile_start = jnp.arange(n_tiles, dtype=I32) * tm_e
    tile_expert = jnp.minimum(jnp.sum(pad_end[None, :] <= tile_start[:, None], axis=1),
                              N_EXPERTS - 1).astype(I32)
    n_used = (pad_end[-1:] // tm_e).astype(I32)
    lstart = (jnp.cumsum(pieces, axis=1) - pieces) * MOE_PIECE
    slot_start = offs[None, :] + run_end - pieces * MOE_PIECE

    def table(count, rows, first_row, length):
        end = jnp.cumsum(count, axis=1)
        q = jnp.arange(length, dtype=I32)
        passed = q[None, :, None] >= end[:, None, :]
        owner = passed != jnp.concatenate([jnp.ones_like(passed[..., :1]), passed[..., :-1]], axis=-1)
        within = (q[None, :, None] - (end - count)[:, None, :]) * rows + first_row[:, None, :]
        pick = lambda base: jnp.sum(jnp.where(owner, base[:, None, :] + within, 0), axis=-1)
        return end[:, -1:], pick(lstart), pick(slot_start)

    counts, tables, left, done_rows = [], [], pieces, jnp.zeros_like(pieces)
    for mult, length, _ in MOE_META_TABLES:
        n, local, slot = table(left // mult, mult * MOE_PIECE, done_rows, length)
        counts.append(n)
        tables += [local, slot]
        done_rows = done_rows + (left // mult) * mult * MOE_PIECE
        left = left % mult
    meta = jnp.concatenate(counts + tables, axis=1)
    meta = jnp.pad(meta, ((0, 0), (0, MOE_META - meta.shape[1]))).reshape(-1).astype(I32)
    z0 = offs + used
    tail = jnp.stack([n_used[0], n_tiles - n_used[0]])
    zmeta = jnp.concatenate([z0, (pad_end - z0) // MOE_PIECE, tail,
                             jnp.zeros((MOE_META - 2 * N_EXPERTS - 2,), I32)]).astype(I32)
    xs = _dispatch(meta, zmeta, lpos.T, h2, n_tiles * tm_e, tm_e)
    y = _experts(tile_expert, n_used, xs, lw["exp_w_gu"], lw["exp_b_gu"], lw["exp_w_down"],
                 lw["exp_b_down"], lw["layer"], tm_e)
    return _combine(meta, lpos, gate, x_mid, mod, mod_row, lw["norm_post_ffn"], y)


def _split_w_in(w_in):
    offs, o = {}, 0
    for name, width in (("dn_k", 1024), ("dn_v", 1024), ("dn_a", 16), ("dn_b", 16), ("at_k", 128),
                        ("at_v", 128), ("dn_q", 1024), ("dn_g", 1024), ("at_q", 1024),
                        ("sg_u", 1024), ("sg_v", 1024), ("gates", 3072)):
        offs[name] = (o, o + width)
        o += width
    sl = lambda n: w_in[:, offs[n][0]:offs[n][1]]
    w_main = jnp.concatenate([sl(n) for n in ("dn_k", "dn_v", "dn_q", "dn_g", "at_q", "sg_u", "sg_v",
                                              "gates")], axis=1).astype(BF16)
    pad = jnp.zeros((w_in.shape[0], N_SMALL_COLS - 2 * LANES - 4 * DN_HEADS), w_in.dtype)
    w_small = jnp.concatenate([sl("at_k"), sl("at_v"), sl("dn_a"), sl("dn_b"), pad], axis=1).astype(BF16)
    return w_main, w_small


def _dn_gates(small, B, T):
    ab = small[:, 2 * LANES:2 * LANES + 4 * DN_HEADS].reshape(B, T, 2, 2, DN_HEADS)
    col = jnp.transpose(ab, (3, 0, 1, 2, 4)).reshape(2, B, T, 2 * DN_HEADS)
    row = jnp.transpose(col.reshape(2, B, T // DN_CHUNK, DN_CHUNK, 2 * DN_HEADS), (0, 1, 2, 4, 3))
    return col, row


def kernel(x, c, ctx, c_ctx, w_mod, b_mod, norm_pre_mix, norm_post_mix, norm_pre_ffn, norm_post_ffn, w_in, sg_ln_g, sg_ln_b, sg_w, sg_b, dn_conv_w, dn_a_log, dn_dt_bias, dn_norm_g, at_sinks, w_proj_sg, w_proj_dn, w_proj_at, w_out, router_w, router_b, exp_w_gu, exp_b_gu, exp_w_down, exp_b_down):
    B, S, D = x.shape
    L = ctx.shape[1]
    depth = w_mod.shape[0]
    assert D == D_MODEL and S % GRID_W == 0
    n_lat, n_ctx = B * S, B * L

    rows = (B + 1 + 7) // 8 * 8
    cvec = jnp.zeros((rows, D), F32).at[:B].set(c).at[B].set(c_ctx)
    mod_all = _modulation(cvec, w_mod, b_mod)
    tables = _rope_tables(S)

    lat_row = lambda t: t // S
    ctx_row = lambda t: B
    all_row = lambda t: jnp.where(t < n_lat, t // S, B)

    xl = x.reshape(n_lat, D)
    xc = ctx.reshape(n_ctx, D)
    for l in range(depth):
        need_ctx_out = l < depth - 1
        mod = mod_all[l].reshape(rows, 1, 6 * D)
        w_main, w_small = _split_w_in(w_in[l])
        lw = {
            "dn_norm_g": dn_norm_g[l].reshape(1, -1),
            "norm_post_mix": norm_post_mix[l].reshape(1, -1),
            "norm_pre_ffn": norm_pre_ffn[l].reshape(1, -1),
            "norm_post_ffn": norm_post_ffn[l].reshape(1, -1),
            "w_proj_sg": w_proj_sg[l].astype(BF16), "w_proj_dn": w_proj_dn[l].astype(BF16),
            "w_proj_at": w_proj_at[l].astype(BF16), "w_out": w_out[l].astype(BF16),
            "router_w": jnp.pad(router_w[l], ((0, 0), (0, LANES - N_EXPERTS))).astype(BF16),
            "router_b": jnp.pad(router_b[l], (0, LANES - N_EXPERTS),
                                constant_values=NEG_BIG).reshape(1, -1),
            "layer": l,
            "exp_w_gu": exp_w_gu, "exp_b_gu": exp_b_gu.reshape(depth, N_EXPERTS, 1, -1),
            "exp_w_down": exp_w_down, "exp_b_down": exp_b_down.reshape(depth, N_EXPERTS, 1, -1),
        }
        pre_g = norm_pre_mix[l].reshape(1, -1)
        main, small = _inproj(xl, mod, lat_row, pre_g, w_main, w_small, min(1024, S))
        w_main_c = w_main if need_ctx_out else w_main[:, :N_CTX_MAIN_COLS]
        main_c, small_c = _inproj(xc, mod, ctx_row, pre_g, w_main_c, w_small, min(1024, n_ctx))

        sg_args = (sg_ln_g[l].reshape(1, -1), sg_ln_b[l].reshape(1, -1), sg_w[l].astype(BF16),
                   sg_b[l].T)
        ysg = _sgu(main, *sg_args)

        gcol_c, grow_c = _dn_gates(small_c, B, L)
        gcol, grow = _dn_gates(small, B, S)
        s0 = jnp.zeros((2, B, DN_HEADS, DN_DIM, DN_DIM), F32)
        of_c, ob_c, s_ctx = _deltanet(main_c.reshape(B, L, -1), gcol_c, grow_c, dn_conv_w[l], dn_a_log[l],
                                      dn_dt_bias[l], s0, need_ctx_out)
        of_l, ob_l, _ = _deltanet(main.reshape(B, S, -1), gcol, grow, dn_conv_w[l], dn_a_log[l],
                                  dn_dt_bias[l], s_ctx, True)

        sinks = jnp.repeat(at_sinks[l], AT_BLOCK).reshape(-1, 1)
        q_r, k_r, v_r = _rope(main, small, tables, S)
        yat = _attention_local(q_r, k_r, v_r, small_c, sinks, B, S, L)

        x_mid, h2, logits = _merge(ysg, of_l.reshape(n_lat, D), ob_l.reshape(n_lat, D), main, yat, xl, mod,
                                   lat_row, lw)
        if need_ctx_out:
            ysg_c = _sgu(main_c, *sg_args)
            yat_c = _attention_ctx(main_c, small_c, sinks, B, L)
            xc_mid, h2c, logits_c = _merge(ysg_c, of_c.reshape(n_ctx, D), ob_c.reshape(n_ctx, D), main_c,
                                           yat_c, xc, mod, ctx_row, lw)
            x_mid = jnp.concatenate([x_mid, xc_mid], axis=0)
            h2 = jnp.concatenate([h2, h2c], axis=0)
            logits = jnp.concatenate([logits, logits_c], axis=0)
            xo = _moe(h2, logits, x_mid, mod, all_row, lw)
            xl, xc = xo[:n_lat], xo[n_lat:]
        else:
            xl = _moe(h2, logits, x_mid, mod, lat_row, lw)
    return xl.reshape(B, S, D)
```

```python
import functools
import math

import jax
import jax.numpy as jnp
from jax import lax
from jax.experimental import pallas as pl
from jax.experimental.pallas import tpu as pltpu

F32 = jnp.float32
BF16 = jnp.bfloat16
I32 = jnp.int32
U32 = jnp.uint32

EPS = 1e-6
D_MODEL = 1024
GRID_W = 64

SG_CHUNK = 128
SG_GROUPS = 8

DN_HEADS = 8
DN_DIM = 128
DN_CONV = 5
DN_CHUNK = 64
DN_HALO = 16
DN_PREP_CHUNKS = 2
DN_SCAN_CHUNKS = 4

AT_Q_HEADS = 16
AT_KV_HEADS = 2
AT_DIM = 64
AT_BLOCK = 128
MERGE_ROW_GROUPS = 2
AT_STAGE_HEADS = 2
ROPE_BASE = 10000.0

N_EXPERTS = 32
TOP_K = 4
D_EXPERT = 1024
SWIGLU_ALPHA = 1.702
SWIGLU_LIMIT = 7.0

LANES = 128
NEG_BIG = -1e30

COL_DN_K, COL_DN_V, COL_DN_Q, COL_DN_G, COL_AT_Q, COL_SG_U, COL_SG_V, COL_GATE0 = range(8)
N_CTX_MAIN_COLS = 2 * D_MODEL
N_SMALL_COLS = 3 * LANES

VMEM_LIMIT = 52 * 1024 * 1024


def _cparams(sem):
    return pltpu.CompilerParams(dimension_semantics=sem, vmem_limit_bytes=VMEM_LIMIT)


def _dot(a, b):
    return jnp.dot(a, b, preferred_element_type=F32)


def _dot_nt(a, b):
    return lax.dot_general(a, b, (((1,), (1,)), ((), ())), preferred_element_type=F32)


def _dot_tn(a, b):
    return lax.dot_general(a, b, (((0,), (0,)), ((), ())), preferred_element_type=F32)


def _sigmoid(x):
    return 0.5 * (1.0 + jnp.tanh(0.5 * x))


def _silu(x):
    return x * _sigmoid(x)


def _gelu_tanh(x):
    return 0.5 * x * (1.0 + jnp.tanh(math.sqrt(2.0 / math.pi) * (x + 0.044715 * (x * x * x))))


def _softplus(x):
    return jnp.maximum(x, 0.0) + jnp.log(1.0 + jnp.exp(-jnp.abs(x)))


def _rms(x, g):
    return x * lax.rsqrt(jnp.mean(x * x, axis=-1, keepdims=True) + EPS) * g


def _mod_kernel(c_ref, w_ref, b_ref, o_ref):
    s = _silu(c_ref[...])
    o_ref[...] = jnp.dot(s, w_ref[...], preferred_element_type=F32,
                         precision=lax.Precision.HIGHEST) + b_ref[...]


def _modulation(cvec, w_mod, b_mod):
    depth = w_mod.shape[0]
    rows = cvec.shape[0]
    n_col = w_mod.shape[2] // D_MODEL
    return pl.pallas_call(
        _mod_kernel,
        grid=(depth, n_col),
        in_specs=[pl.BlockSpec((rows, D_MODEL), lambda l, j: (0, 0)),
                  pl.BlockSpec((None, D_MODEL, D_MODEL), lambda l, j: (l, 0, j)),
                  pl.BlockSpec((None, 1, D_MODEL), lambda l, j: (l, 0, j))],
        out_specs=pl.BlockSpec((None, rows, D_MODEL), lambda l, j: (l, 0, j)),
        out_shape=jax.ShapeDtypeStruct((depth, rows, w_mod.shape[2]), F32),
        compiler_params=_cparams(("arbitrary", "arbitrary")),
        name="modulation",
    )(cvec, w_mod, b_mod.reshape(depth, 1, -1))


def _inproj_kernel(x_ref, mod_ref, g_ref, wm_ref, ws_ref, main_ref, small_ref, h_ref):
    @pl.when(pl.program_id(1) == 0)
    def _():
        sh = mod_ref[:, 0 * D_MODEL:1 * D_MODEL]
        sc = mod_ref[:, 1 * D_MODEL:2 * D_MODEL]
        h = (_rms(x_ref[...], g_ref[...]) * (1.0 + sc) + sh).astype(BF16)
        h_ref[...] = h
        small_ref[...] = _dot(h, ws_ref[...])

    main_ref[...] = _dot(h_ref[...], wm_ref[...]).astype(BF16)


def _inproj(x, mod, mod_row, norm_g, w_main, w_small, tm, tn=2048):
    n_tok = x.shape[0]
    n_main = w_main.shape[1]
    return pl.pallas_call(
        _inproj_kernel,
        grid=(n_tok // tm, n_main // tn),
        in_specs=[pl.BlockSpec((tm, D_MODEL), lambda i, j: (i, 0)),
                  pl.BlockSpec((None, 1, 6 * D_MODEL), lambda i, j: (mod_row(i * tm), 0, 0)),
                  pl.BlockSpec((1, D_MODEL), lambda i, j: (0, 0)),
                  pl.BlockSpec((D_MODEL, tn), lambda i, j: (0, j)),
                  pl.BlockSpec((D_MODEL, N_SMALL_COLS), lambda i, j: (0, 0))],
        out_specs=[pl.BlockSpec((tm, tn), lambda i, j: (i, j)),
                   pl.BlockSpec((tm, N_SMALL_COLS), lambda i, j: (i, 0))],
        out_shape=[jax.ShapeDtypeStruct((n_tok, n_main), BF16),
                   jax.ShapeDtypeStruct((n_tok, N_SMALL_COLS), F32)],
        scratch_shapes=[pltpu.VMEM((tm, D_MODEL), BF16)],
        compiler_params=_cparams(("arbitrary", "arbitrary")),
        name="inproj",
    )(x, mod, norm_g, w_main, w_small)


def _sgu_kernel(u_ref, v_ref, lng_ref, lnb_ref, ws_ref, bs_ref, o_ref, *, n_chunk):
    def activate(n):
        rows = slice(n * SG_CHUNK, (n + 1) * SG_CHUNK)
        u = _gelu_tanh(u_ref[rows, :].astype(F32))
        v = _gelu_tanh(v_ref[rows, :].astype(F32))
        vc = v - jnp.mean(v, axis=-1, keepdims=True)
        var = jnp.mean(vc * vc, axis=-1, keepdims=True)
        return u, (vc * lax.rsqrt(var + EPS) * lng_ref[...] + lnb_ref[...]).astype(BF16)

    def mix(n, u, vn):
        rows = slice(n * SG_CHUNK, (n + 1) * SG_CHUNK)
        for g in range(SG_GROUPS):
            cols = slice(g * LANES, (g + 1) * LANES)
            mixed = _dot(ws_ref[g], vn[:, cols]) + bs_ref[:, g:g + 1]
            o_ref[rows, cols] = (u[:, cols] * mixed).astype(BF16)

    nxt = activate(0)
    for n in range(n_chunk):
        cur = nxt
        if n + 1 < n_chunk:
            nxt = activate(n + 1)
        mix(n, *cur)


def _sgu(main, sg_ln_g, sg_ln_b, sg_w, sg_bt, n_chunk=4):
    n_tok = main.shape[0]
    tc = n_chunk * SG_CHUNK
    return pl.pallas_call(
        functools.partial(_sgu_kernel, n_chunk=n_chunk),
        grid=(n_tok // tc,),
        in_specs=[pl.BlockSpec((tc, D_MODEL), lambda i: (i, COL_SG_U)),
                  pl.BlockSpec((tc, D_MODEL), lambda i: (i, COL_SG_V)),
                  pl.BlockSpec((1, D_MODEL), lambda i: (0, 0)),
                  pl.BlockSpec((1, D_MODEL), lambda i: (0, 0)),
                  pl.BlockSpec((SG_GROUPS, SG_CHUNK, SG_CHUNK), lambda i: (0, 0, 0)),
                  pl.BlockSpec((SG_CHUNK, SG_GROUPS), lambda i: (0, 0))],
        out_specs=pl.BlockSpec((tc, D_MODEL), lambda i: (i, 0)),
        out_shape=jax.ShapeDtypeStruct((n_tok, D_MODEL), BF16),
        compiler_params=_cparams(("arbitrary",)),
        name="sgu",
    )(main, main, sg_ln_g, sg_ln_b, sg_w, sg_bt)


def _dn_prep_kernel(*refs, with_q, n_chunks):
    if with_q:
        (qp_ref, qc_ref, qn_ref, kp_ref, kc_ref, kn_ref, vp_ref, vc_ref, vn_ref,
         gcol_ref, grow_ref, cw_ref, alog_r_ref, alog_c_ref, dtb_r_ref, dtb_c_ref,
         w_ref, u0_ref, ke_ref, gt_ref, qs_ref, qk_ref) = refs
    else:
        (kp_ref, kc_ref, kn_ref, vp_ref, vc_ref, vn_ref,
         gcol_ref, grow_ref, cw_ref, alog_r_ref, alog_c_ref, dtb_r_ref, dtb_c_ref,
         w_ref, u0_ref, ke_ref, gt_ref) = refs
    c = pl.program_id(1)
    C = DN_CHUNK
    R = DN_PREP_CHUNKS * C
    has_prev = (c > 0).astype(BF16)
    has_next = (c < n_chunks // DN_PREP_CHUNKS - 1).astype(BF16)

    pad = DN_CONV // 2
    n_sh = DN_CONV - 1
    sr = lax.broadcasted_iota(I32, (n_sh * R, R + 2 * DN_HALO), 0)
    sc = lax.broadcasted_iota(I32, (n_sh * R, R + 2 * DN_HALO), 1)
    blk = sr // R
    off = jnp.where(blk < pad, blk - pad, blk - pad + 1)
    shift_mat = (sc == DN_HALO + (sr - blk * R) + off).astype(BF16)

    def conv_silu(p_ref, c_ref, n_ref, part):
        cur = c_ref[...]
        ext = jnp.concatenate([p_ref[...] * has_prev, cur, n_ref[...] * has_next], axis=0)
        sh = _dot(shift_mat, ext)
        taps = [sh[j * R:(j + 1) * R] for j in range(pad)] + [cur.astype(F32)] + \
               [sh[j * R:(j + 1) * R] for j in range(pad, n_sh)]
        y = None
        for i in range(DN_CONV):
            t = taps[i] * cw_ref[i:i + 1, part * D_MODEL:(part + 1) * D_MODEL]
            y = t if y is None else y + t
        return _silu(y)

    k_all = conv_silu(kp_ref, kc_ref, kn_ref, 1)
    v_all = conv_silu(vp_ref, vc_ref, vn_ref, 2)
    q_all = conv_silu(qp_ref, qc_ref, qn_ref, 0) if with_q else None

    ri = lax.broadcasted_iota(I32, (C, C), 0)
    ci = lax.broadcasted_iota(I32, (C, C), 1)
    eye = (ri == ci).astype(F32)
    CC = range(DN_PREP_CHUNKS)
    H = range(DN_HEADS)
    CH = [(cc, h) for cc in CC for h in H]
    ch = {key: i for i, key in enumerate(CH)}
    rows = [slice(cc * C, (cc + 1) * C) for cc in CC]
    lanes = [slice(h * DN_DIM, (h + 1) * DN_DIM) for h in H]
    kh = [k_all[rows[cc], lanes[h]] for cc, h in CH]
    kh = [k * lax.rsqrt(jnp.sum(k * k, axis=-1, keepdims=True) + EPS) for k in kh]
    kb = [k.astype(BF16) for k in kh]
    vh = [v_all[rows[cc], lanes[h]] for cc, h in CH]
    if with_q:
        qh = [q_all[rows[cc], lanes[h]] for cc, h in CH]
        qh = [q * (lax.rsqrt(jnp.sum(q * q, axis=-1, keepdims=True) + EPS) * DN_DIM ** -0.5) for q in qh]
        gram = [_dot_nt(jnp.concatenate([kb[j], qh[j].astype(BF16)], axis=0), kb[j]) for j in range(len(CH))]
        kk = [g[:C] for g in gram]
        qk_raw = [g[C:] for g in gram]
    else:
        kk = [_dot_nt(k, k) for k in kb]

    D2 = range(2)
    DH = [(cc, d, h) for cc in CC for d in D2 for h in H]
    incl = [(ri >= ci), (ri <= ci)]
    strict = [(ri > ci), (ri < ci)]
    gam_col, gam_row, gam_tot, beta_col = {}, {}, {}, {}
    for cc in CC:
        for d in D2:
            gcol = gcol_ref[d, rows[cc], :]
            ld_col = -jnp.exp(alog_r_ref[d]) * _softplus(gcol[:, 0:DN_HEADS] + dtb_r_ref[d])
            ld_row = -jnp.exp(alog_c_ref[d]) * _softplus(grow_ref[d, cc][0:DN_HEADS, :] + dtb_c_ref[d])
            beta_col[cc, d] = _sigmoid(gcol[:, DN_HEADS:2 * DN_HEADS])
            gam_col[cc, d] = jnp.dot(incl[d].astype(F32), ld_col, preferred_element_type=F32,
                                     precision=lax.Precision.HIGHEST)
            gam_row[cc, d] = jnp.dot(ld_row, incl[1 - d].astype(F32), preferred_element_type=F32,
                                     precision=lax.Precision.HIGHEST)
            tot = jnp.sum(ld_col, axis=0, keepdims=True)
            gam_tot[cc, d] = tot
            gt_ref[d, cc] = tot
    gc = [gam_col[cc, d][:, h:h + 1] for cc, d, h in DH]
    bc = [beta_col[cc, d][:, h:h + 1] for cc, d, h in DH]
    decay = [jnp.exp(jnp.where(incl[d], gc[i] - gam_row[cc, d][h:h + 1, :], NEG_BIG))
             for i, (cc, d, h) in enumerate(DH)]
    x = [-(jnp.where(strict[d], decay[i], 0.0) * bc[i] * kk[ch[cc, h]]) for i, (cc, d, h) in enumerate(DH)]
    N = range(len(DH))
    p = [eye + x[i] for i in N]
    xb = [x[i].astype(BF16) for i in N]
    x = [_dot(xb[i], xb[i]) for i in N]
    n_fac = int(math.log2(C)) - 1
    for j in range(n_fac):
        xb = [x[i].astype(BF16) for i in N]
        if j < n_fac - 1:
            r = [_dot(xb[i], jnp.concatenate([xb[i], p[i].astype(BF16)], axis=1)) for i in N]
            x = [r[i][:, :C] for i in N]
            p = [p[i] + r[i][:, C:] for i in N]
        else:
            p = [p[i] + _dot(xb[i], p[i].astype(BF16)) for i in N]
    rhs = [jnp.concatenate([kh[ch[cc, h]] * (bc[i] * jnp.exp(gc[i])), vh[ch[cc, h]] * bc[i]],
                           axis=1).astype(BF16) for i, (cc, d, h) in enumerate(DH)]
    sol = [_dot(p[i].astype(BF16), rhs[i]) for i in N]
    for i, (cc, d, h) in enumerate(DH):
        j = ch[cc, h]
        w_ref[d, rows[cc], lanes[h]] = sol[i][:, :DN_DIM].astype(BF16)
        u0_ref[d, rows[cc], lanes[h]] = sol[i][:, DN_DIM:].astype(BF16)
        ke_ref[d, rows[cc], lanes[h]] = (kh[j] * jnp.exp(gam_tot[cc, d][:, h:h + 1] - gc[i])).astype(BF16)
        if with_q:
            qs_ref[d, rows[cc], lanes[h]] = (qh[j] * jnp.exp(gc[i])).astype(BF16)
            qk_ref[d, rows[cc], h * C:(h + 1) * C] = (qk_raw[j] * decay[i]).astype(BF16)


def _dn_scan_kernel(*refs, with_q, n_chunks):
    n_state = 2 * DN_HEADS
    s_refs = refs[-n_state:]
    refs = refs[:-n_state]
    if with_q:
        (w0, w1, u0, u1, k0, k1, g0, g1, qs0, qs1, qk0, qk1, s0_ref, o0_ref, o1_ref, sfin_ref) = refs
        qs_r, qk_r, o_r = (qs0, qs1), (qk0, qk1), (o0_ref, o1_ref)
    else:
        (w0, w1, u0, u1, k0, k1, g0, g1, s0_ref, sfin_ref) = refs
    w_r, u_r, k_r, g_r = (w0, w1), (u0, u1), (k0, k1), (g0, g1)
    c = pl.program_id(1)
    C = DN_CHUNK
    DH = [(d, h) for d in range(2) for h in range(DN_HEADS)]
    N = range(len(DH))
    lanes = [slice(h * DN_DIM, (h + 1) * DN_DIM) for h in range(DN_HEADS)]

    @pl.when(c == 0)
    def _():
        for i, (d, h) in enumerate(DH):
            s_refs[i][...] = s0_ref[d, h]

    s = [s_refs[i][...] for i in N]
    for sub in range(DN_SCAN_CHUNKS):
        cix = (sub, DN_SCAN_CHUNKS - 1 - sub)
        rows = [slice(cix[d] * C, (cix[d] + 1) * C) for d in range(2)]
        sb = [s[i].astype(BF16) for i in N]
        w = [w_r[d][rows[d], lanes[h]] for d, h in DH]
        if with_q:
            wq = [jnp.concatenate([w[i], qs_r[d][rows[d], lanes[h]]], axis=0) for i, (d, h) in enumerate(DH)]
            ws = [_dot(wq[i], sb[i]) for i in N]
            ub = [(u_r[d][rows[d], lanes[h]].astype(F32) - ws[i][:C]).astype(BF16)
                  for i, (d, h) in enumerate(DH)]
            qu = [_dot(qk_r[d][rows[d], h * C:(h + 1) * C], ub[i]) for i, (d, h) in enumerate(DH)]
            for i, (d, h) in enumerate(DH):
                o_r[d][rows[d], lanes[h]] = (ws[i][C:] + qu[i]).astype(BF16)
        else:
            ws = [_dot(w[i], sb[i]) for i in N]
            ub = [(u_r[d][rows[d], lanes[h]].astype(F32) - ws[i]).astype(BF16) for i, (d, h) in enumerate(DH)]
        ku = [_dot_tn(k_r[d][rows[d], lanes[h]], ub[i]) for i, (d, h) in enumerate(DH)]
        s = [jnp.exp(g_r[d][cix[d]][:, h:h + 1]) * s[i] + ku[i] for i, (d, h) in enumerate(DH)]
    for i in N:
        s_refs[i][...] = s[i]

    @pl.when(c == n_chunks // DN_SCAN_CHUNKS - 1)
    def _():
        for i, (d, h) in enumerate(DH):
            sfin_ref[d, h] = s_refs[i][...]


def _deltanet(main3, gate_col, gate_row, conv_w, alog, dtb, s0, with_q):
    B, T, _ = main3.shape
    C = DN_CHUNK
    n_chunks = T // C
    NC = DN_PREP_CHUNKS
    R = NC * C
    assert T % R == 0
    hpc = R // DN_HALO
    n_halo = T // DN_HALO

    def trio(col):
        return [pl.BlockSpec((None, DN_HALO, D_MODEL), lambda b, c: (b, jnp.maximum(c * hpc - 1, 0), col)),
                pl.BlockSpec((None, R, D_MODEL), lambda b, c: (b, c, col)),
                pl.BlockSpec((None, DN_HALO, D_MODEL),
                             lambda b, c: (b, jnp.minimum((c + 1) * hpc, n_halo - 1), col))]

    vec = lambda shape: pl.BlockSpec(shape, lambda b, c: (0,) * len(shape))
    in_specs = (trio(COL_DN_Q) if with_q else []) + trio(COL_DN_K) + trio(COL_DN_V) + [
        pl.BlockSpec((2, None, R, 2 * DN_HEADS), lambda b, c: (0, b, c, 0)),
        pl.BlockSpec((2, None, NC, 2 * DN_HEADS, C), lambda b, c: (0, b, c, 0, 0)),
        vec((DN_CONV, 3 * D_MODEL)),
        vec((2, 1, DN_HEADS)), vec((2, DN_HEADS, 1)), vec((2, 1, DN_HEADS)), vec((2, DN_HEADS, 1)),
    ]
    wide = lambda n: (pl.BlockSpec((2, None, R, n), lambda b, c: (0, b, c, 0)),
                      jax.ShapeDtypeStruct((2, B, T, n), BF16))
    outs = [wide(D_MODEL), wide(D_MODEL), wide(D_MODEL),
            (pl.BlockSpec((2, None, NC, 1, DN_HEADS), lambda b, c: (0, b, c, 0, 0)),
             jax.ShapeDtypeStruct((2, B, n_chunks, 1, DN_HEADS), F32))]
    if with_q:
        outs += [wide(D_MODEL), wide(DN_HEADS * C)]
    n_main = 3 if with_q else 2
    prep = pl.pallas_call(
        functools.partial(_dn_prep_kernel, with_q=with_q, n_chunks=n_chunks),
        grid=(B, n_chunks // NC),
        in_specs=in_specs, out_specs=[o[0] for o in outs], out_shape=[o[1] for o in outs],
        compiler_params=_cparams(("arbitrary", "arbitrary")),
        name="dn_prep_q" if with_q else "dn_prep",
    )(*([main3] * (3 * n_main)), gate_col, gate_row, conv_w,
      alog.reshape(2, 1, DN_HEADS), alog.reshape(2, DN_HEADS, 1),
      dtb.reshape(2, 1, DN_HEADS), dtb.reshape(2, DN_HEADS, 1))

    NS = DN_SCAN_CHUNKS
    RS = NS * C
    n_steps = n_chunks // NS
    assert n_chunks % NS == 0

    def both_dirs(arr, n):
        if n is None:
            return [pl.BlockSpec((None, None, NS, 1, DN_HEADS), lambda b, c: (0, b, c, 0, 0)),
                    pl.BlockSpec((None, None, NS, 1, DN_HEADS),
                                 lambda b, c: (1, b, n_steps - 1 - c, 0, 0))], [arr, arr]
        return [pl.BlockSpec((None, None, RS, n), lambda b, c: (0, b, c, 0)),
                pl.BlockSpec((None, None, RS, n), lambda b, c: (1, b, n_steps - 1 - c, 0))], [arr, arr]

    specs, args = [], []
    widths = [D_MODEL, D_MODEL, D_MODEL, None] + ([D_MODEL, DN_HEADS * C] if with_q else [])
    for arr, n in zip(prep, widths):
        sp, ar = both_dirs(arr, n)
        specs += sp
        args += ar
    s_spec = pl.BlockSpec((2, None, DN_HEADS, DN_DIM, DN_DIM), lambda b, c: (0, b, 0, 0, 0))
    s_shape = jax.ShapeDtypeStruct((2, B, DN_HEADS, DN_DIM, DN_DIM), F32)
    if with_q:
        out_specs = [pl.BlockSpec((None, RS, D_MODEL), lambda b, c: (b, c, 0)),
                     pl.BlockSpec((None, RS, D_MODEL), lambda b, c: (b, n_steps - 1 - c, 0)), s_spec]
        out_shape = [jax.ShapeDtypeStruct((B, T, D_MODEL), BF16)] * 2 + [s_shape]
    else:
        out_specs, out_shape = [s_spec], [s_shape]
    out = pl.pallas_call(
        functools.partial(_dn_scan_kernel, with_q=with_q, n_chunks=n_chunks),
        grid=(B, n_steps),
        in_specs=specs + [s_spec], out_specs=out_specs, out_shape=out_shape,
        scratch_shapes=[pltpu.VMEM((DN_DIM, DN_DIM), F32)] * (2 * DN_HEADS),
        compiler_params=_cparams(("arbitrary", "arbitrary")),
        name="dn_scan_q" if with_q else "dn_scan",
    )(*args, s0)
    return (out[0], out[1], out[2]) if with_q else (None, None, out[0])


def _rope_tables(S):
    half = AT_DIM // 2
    nf = half // 2
    inv_freq = ROPE_BASE ** (-jnp.arange(nf, dtype=F32) / nf)
    t = jnp.arange(S, dtype=jnp.int32)
    row = (t // GRID_W).astype(F32)
    col = (t % GRID_W).astype(F32)
    lane = jnp.arange(LANES)
    dd = lane % AT_DIM
    pos = jnp.where((dd < half)[None, :], row[:, None], col[:, None])
    ang = pos * inv_freq[lane % nf][None, :]
    first = ((lane % half) < nf)[None, :]
    sin = jnp.sin(ang)
    return jnp.cos(ang), jnp.where(first, -sin, 0.0), jnp.where(first, 0.0, sin)


def _rope_kernel(q_ref, k_ref, v_ref, cos_ref, sa_ref, sb_ref, qo_ref, ko_ref, vo_ref):
    cos, sa, sb = cos_ref[...], sa_ref[...], sb_ref[...]
    nf = AT_DIM // 4

    def rot(x):
        return x * cos + pltpu.roll(x, LANES - nf, 1) * sa + pltpu.roll(x, nf, 1) * sb

    for j in range(AT_Q_HEADS * AT_DIM // LANES):
        lanes = slice(j * LANES, (j + 1) * LANES)
        qo_ref[:, lanes] = (rot(q_ref[:, lanes].astype(F32)) * AT_DIM ** -0.5).astype(BF16)
    ko_ref[...] = rot(k_ref[...]).astype(BF16)
    vo_ref[...] = v_ref[...].astype(BF16)


def _rope(main, small, tables, S, tm=512):
    n_tok = main.shape[0]
    per_seq = S // tm
    tab_spec = pl.BlockSpec((tm, LANES), lambda i: (i % per_seq, 0))
    return pl.pallas_call(
        _rope_kernel,
        grid=(n_tok // tm,),
        in_specs=[pl.BlockSpec((tm, D_MODEL), lambda i: (i, COL_AT_Q)),
                  pl.BlockSpec((tm, LANES), lambda i: (i, 0)),
                  pl.BlockSpec((tm, LANES), lambda i: (i, 1)),
                  tab_spec, tab_spec, tab_spec],
        out_specs=[pl.BlockSpec((tm, D_MODEL), lambda i: (i, 0)),
                   pl.BlockSpec((tm, LANES), lambda i: (i, 0)),
                   pl.BlockSpec((tm, LANES), lambda i: (i, 0))],
        out_shape=[jax.ShapeDtypeStruct((n_tok, D_MODEL), BF16),
                   jax.ShapeDtypeStruct((n_tok, LANES), BF16),
                   jax.ShapeDtypeStruct((n_tok, LANES), BF16)],
        compiler_params=_cparams(("arbitrary",)),
        name="rope",
    )(main, small, small, *tables)


def _attn_kernel(*refs, local, n_blocks, q_scale):
    if local:
        (q_ref, kp_ref, kc_ref, kn_ref, vp_ref, vc_ref, vn_ref, kx_ref, vx_ref, sink_ref, o_ref) = refs
    else:
        (q_ref, kx_ref, vx_ref, sink_ref, o_ref) = refs
    P = AT_BLOCK
    G = AT_Q_HEADS // AT_KV_HEADS
    kx = kx_ref[...].astype(BF16)
    vx = vx_ref[...].astype(BF16)
    if local:
        i = pl.program_id(1)
        k_all = jnp.concatenate([kp_ref[...], kc_ref[...], kn_ref[...], kx], axis=0)
        v_all = jnp.concatenate([vp_ref[...], vc_ref[...], vn_ref[...], vx], axis=0)
        qi = lax.broadcasted_iota(I32, (P, P), 0)
        kj = lax.broadcasted_iota(I32, (P, P), 1)
        b_prev = jnp.where(kj >= qi, 0.0, NEG_BIG) + jnp.where(i > 0, 0.0, NEG_BIG)
        b_next = jnp.where(kj <= qi, 0.0, NEG_BIG) + jnp.where(i < n_blocks - 1, 0.0, NEG_BIG)
        b_prev2 = jnp.concatenate([b_prev] * AT_STAGE_HEADS, axis=0)
        b_next2 = jnp.concatenate([b_next] * AT_STAGE_HEADS, axis=0)
    else:
        k_all, v_all = kx, vx
    lo = lax.broadcasted_iota(I32, (P, LANES), 1) < AT_DIM
    qf = q_ref[...].astype(F32) * q_scale
    pieces = []
    for qh in range(AT_Q_HEADS):
        blk = qf[:, (qh // 2) * LANES:(qh // 2 + 1) * LANES]
        want_lo = qh // G == 0
        if want_lo != (qh % 2 == 0):
            blk = pltpu.roll(blk, AT_DIM, 1)
        pieces.append(jnp.where(lo if want_lo else ~lo, blk, 0.0).astype(BF16))
    HS = AT_STAGE_HEADS
    n_pair = AT_Q_HEADS // HS

    def logits(j):
        s = _dot_nt(jnp.concatenate(pieces[HS * j:HS * (j + 1)], axis=0), k_all)
        if not local:
            return s
        return jnp.concatenate([s[:, 0:P] + b_prev2, s[:, P:2 * P], s[:, 2 * P:3 * P] + b_next2,
                                s[:, 3 * P:]], axis=1)

    def softmax(j, s):
        sink = sink_ref[HS * j * P:HS * (j + 1) * P, :]
        m = jnp.maximum(jnp.max(s, axis=-1, keepdims=True), sink)
        p = jnp.exp(s - m)
        den = jnp.sum(p, axis=-1, keepdims=True) + jnp.exp(sink - m)
        return p.astype(BF16), den

    def values(j, p, den):
        o = _dot(p, v_all) / den
        for t in range(HS // 2):
            blk = (HS * j) // 2 + t
            a, b = o[2 * t * P:(2 * t + 1) * P], o[(2 * t + 1) * P:(2 * t + 2) * P]
            if (2 * blk) // G == 0:
                out = jnp.where(lo, a, pltpu.roll(b, AT_DIM, 1))
            else:
                out = jnp.where(lo, pltpu.roll(a, AT_DIM, 1), b)
            o_ref[:, blk * LANES:(blk + 1) * LANES] = out.astype(BF16)

    s_next = logits(0)
    prob = None
    for j in range(n_pair):
        s_cur = s_next
        if j + 1 < n_pair:
            s_next = logits(j + 1)
        done = prob
        prob = softmax(j, s_cur)
        if done is not None:
            values(j - 1, *done)
    values(n_pair - 1, *prob)


def _attention_local(q_r, k_r, v_r, small_c, sinks, B, S, L):
    P = AT_BLOCK
    nb = S // P

    def kv_trio():
        return [pl.BlockSpec((P, LANES), lambda b, i: (b * nb + jnp.maximum(i - 1, 0), 0)),
                pl.BlockSpec((P, LANES), lambda b, i: (b * nb + i, 0)),
                pl.BlockSpec((P, LANES), lambda b, i: (b * nb + jnp.minimum(i + 1, nb - 1), 0))]

    return pl.pallas_call(
        functools.partial(_attn_kernel, local=True, n_blocks=nb, q_scale=1.0),
        grid=(B, nb),
        in_specs=[pl.BlockSpec((P, D_MODEL), lambda b, i: (b * nb + i, 0))] + kv_trio() + kv_trio() + [
            pl.BlockSpec((L, LANES), lambda b, i: (b, 0)),
            pl.BlockSpec((L, LANES), lambda b, i: (b, 1)),
            pl.BlockSpec((AT_Q_HEADS * AT_BLOCK, 1), lambda b, i: (0, 0))],
        out_specs=pl.BlockSpec((P, D_MODEL), lambda b, i: (b * nb + i, 0)),
        out_shape=jax.ShapeDtypeStruct((B * S, D_MODEL), BF16),
        compiler_params=_cparams(("arbitrary", "arbitrary")),
        name="attn_local",
    )(q_r, k_r, k_r, k_r, v_r, v_r, v_r, small_c, small_c, sinks)


def _attention_ctx(main_c, small_c, sinks, B, L):
    P = AT_BLOCK
    nb = L // P
    return pl.pallas_call(
        functools.partial(_attn_kernel, local=False, n_blocks=nb, q_scale=AT_DIM ** -0.5),
        grid=(B, nb),
        in_specs=[pl.BlockSpec((P, D_MODEL), lambda b, i: (b * nb + i, COL_AT_Q)),
                  pl.BlockSpec((L, LANES), lambda b, i: (b, 0)),
                  pl.BlockSpec((L, LANES), lambda b, i: (b, 1)),
                  pl.BlockSpec((AT_Q_HEADS * AT_BLOCK, 1), lambda b, i: (0, 0))],
        out_specs=pl.BlockSpec((P, D_MODEL), lambda b, i: (b * nb + i, 0)),
        out_shape=jax.ShapeDtypeStruct((B * L, D_MODEL), BF16),
        compiler_params=_cparams(("arbitrary", "arbitrary")),
        name="attn_ctx",
    )(main_c, small_c, small_c, sinks)


def _merge_kernel(ysg_ref, of_ref, ob_ref, dng_ref, yat_ref, g0_ref, g1_ref, g2_ref, x_ref, mod_ref,
                  dn_norm_ref, post_ref, pre_ref, wsg_ref, wdn_ref, wat_ref, wout_ref, rw_ref, rb_ref,
                  xo_ref, h2_ref, lg_ref):
    dn_g = dn_norm_ref[...]
    gate1 = mod_ref[:, 2 * D_MODEL:3 * D_MODEL]
    sh2 = mod_ref[:, 3 * D_MODEL:4 * D_MODEL]
    sc2 = mod_ref[:, 4 * D_MODEL:5 * D_MODEL]
    n_grp = MERGE_ROW_GROUPS
    grp = x_ref.shape[0] // n_grp

    def dn_out(r):
        rows = slice(r * grp, (r + 1) * grp)
        o = of_ref[rows, :].astype(F32) + ob_ref[rows, :].astype(F32)
        parts = []
        for h in range(DN_HEADS):
            lanes = slice(h * DN_DIM, (h + 1) * DN_DIM)
            parts.append(_rms(o[:, lanes], dn_g) * _silu(dng_ref[rows, lanes].astype(F32)))
        return jnp.concatenate(parts, axis=1).astype(BF16)

    def branches(r, ydn):
        rows = slice(r * grp, (r + 1) * grp)
        m = (_sigmoid(g0_ref[rows, :].astype(F32)) * _dot(ysg_ref[rows, :], wsg_ref[...])
             + _sigmoid(g1_ref[rows, :].astype(F32)) * _dot(ydn, wdn_ref[...])
             + _sigmoid(g2_ref[rows, :].astype(F32)) * _dot(yat_ref[rows, :], wat_ref[...]))
        return m.astype(BF16)

    def out_proj(r, m):
        return _dot(m, wout_ref[...])

    def residual(r, y):
        rows = slice(r * grp, (r + 1) * grp)
        xn = x_ref[rows, :] + gate1 * _rms(y, post_ref[...])
        xo_ref[rows, :] = xn
        h2 = _rms(xn, pre_ref[...]) * (1.0 + sc2) + sh2
        h2_ref[rows, :] = h2
        lg_ref[rows, :] = _dot(h2.astype(BF16), rw_ref[...]) + rb_ref[...]

    stages = (dn_out, branches, out_proj, residual)
    carried = {}
    for t in range(n_grp + len(stages) - 1):
        for k in reversed(range(len(stages))):
            r = t - k
            if 0 <= r < n_grp:
                carried[r] = stages[k](r) if k == 0 else stages[k](r, carried[r])


def _merge(ysg, o_fwd, o_bwd, main, yat, x, mod, mod_row, lw, tm=512):
    n_tok = x.shape[0]
    const = lambda i: (0, 0)
    wspec = pl.BlockSpec((D_MODEL, D_MODEL), const, pipeline_mode=pl.Buffered(1))
    vspec = pl.BlockSpec((1, D_MODEL), const)
    return pl.pallas_call(
        _merge_kernel,
        grid=(n_tok // tm,),
        in_specs=[pl.BlockSpec((tm, D_MODEL), lambda i: (i, 0)),
                  pl.BlockSpec((tm, D_MODEL), lambda i: (i, 0)),
                  pl.BlockSpec((tm, D_MODEL), lambda i: (i, 0)),
                  pl.BlockSpec((tm, D_MODEL), lambda i: (i, COL_DN_G)),
                  pl.BlockSpec((tm, D_MODEL), lambda i: (i, 0)),
                  pl.BlockSpec((tm, D_MODEL), lambda i: (i, COL_GATE0)),
                  pl.BlockSpec((tm, D_MODEL), lambda i: (i, COL_GATE0 + 1)),
                  pl.BlockSpec((tm, D_MODEL), lambda i: (i, COL_GATE0 + 2)),
                  pl.BlockSpec((tm, D_MODEL), lambda i: (i, 0)),
                  pl.BlockSpec((None, 1, 6 * D_MODEL), lambda i: (mod_row(i * tm), 0, 0)),
                  pl.BlockSpec((1, DN_DIM), const), vspec, vspec,
                  wspec, wspec, wspec, wspec,
                  pl.BlockSpec((D_MODEL, LANES), const), pl.BlockSpec((1, LANES), const)],
        out_specs=[pl.BlockSpec((tm, D_MODEL), lambda i: (i, 0)),
                   pl.BlockSpec((tm, D_MODEL), lambda i: (i, 0)),
                   pl.BlockSpec((tm, LANES), lambda i: (i, 0))],
        out_shape=[jax.ShapeDtypeStruct((n_tok, D_MODEL), F32),
                   jax.ShapeDtypeStruct((n_tok, D_MODEL), F32),
                   jax.ShapeDtypeStruct((n_tok, LANES), F32)],
        compiler_params=_cparams(("arbitrary",)),
        name="merge",
    )(ysg, o_fwd, o_bwd, main, yat, main, main, main, x, mod,
      lw["dn_norm_g"], lw["norm_post_mix"], lw["norm_pre_ffn"],
      lw["w_proj_sg"], lw["w_proj_dn"], lw["w_proj_at"], lw["w_out"], lw["router_w"], lw["router_b"])


MOE_TOK = 256
MOE_PIECE = 8
MOE_BUF = MOE_TOK * TOP_K + N_EXPERTS * MOE_PIECE
MOE_META = 256
MOE_DMA_PIECES = (4, 2, 1)
MOE_META_TABLES = []
_off = len(MOE_DMA_PIECES)
for _m in MOE_DMA_PIECES:
    _len = MOE_BUF // (_m * MOE_PIECE) if _m == MOE_DMA_PIECES[0] else N_EXPERTS
    MOE_META_TABLES.append((_m, _len, _off))
    _off += 2 * _len
assert MOE_META >= _off


def _route_kernel(lg_ref, gate_ref, lpos_ref, tcnt_ref):
    tm = lg_ref.shape[0]
    l = lg_ref[...]
    lane = lax.broadcasted_iota(I32, l.shape, 1).astype(F32)
    vals, onehots = [], []
    for k in range(TOP_K):
        m = jnp.max(l, axis=-1, keepdims=True)
        ik = jnp.min(jnp.where(l == m, lane, float(LANES)), axis=-1, keepdims=True)
        oh = lane == ik
        vals.append(m)
        onehots.append(oh)
        l = jnp.where(oh, -jnp.inf, l)
    es = [jnp.exp(v - vals[0]) for v in vals]
    den = es[0] + es[1] + es[2] + es[3]
    sel = jnp.zeros(l.shape, F32)
    for k in range(TOP_K):
        gate_ref[:, k:k + 1] = es[k] / den
        sel = sel + onehots[k].astype(F32)
    ri = lax.broadcasted_iota(I32, (tm, tm), 0)
    ci = lax.broadcasted_iota(I32, (tm, tm), 1)
    before = _dot((ri > ci).astype(BF16), sel.astype(BF16))
    tcnt = jnp.sum(sel, axis=0, keepdims=True)
    tcnt_ref[...] = tcnt
    n_piece = jnp.floor((tcnt + (MOE_PIECE - 1)) * (1.0 / MOE_PIECE))
    ei = lax.broadcasted_iota(I32, (LANES, LANES), 0)
    ej = lax.broadcasted_iota(I32, (LANES, LANES), 1)
    run_start = _dot(jnp.broadcast_to(n_piece, (8, LANES)).astype(BF16),
                     (ei < ej).astype(BF16))[0:1] * float(MOE_PIECE)
    pos = before + run_start
    for k in range(TOP_K):
        lpos_ref[:, k:k + 1] = jnp.sum(jnp.where(onehots[k], pos, 0.0), axis=-1,
                                       keepdims=True).astype(I32)


def _route(logits):
    n_tok = logits.shape[0]
    tm = MOE_TOK
    n_t = n_tok // tm
    small = lambda dt: jax.ShapeDtypeStruct((n_tok, TOP_K), dt)
    kspec = pl.BlockSpec((tm, TOP_K), lambda i: (i, 0))
    tspec = pl.BlockSpec((None, 1, LANES), lambda i: (i, 0, 0))
    tshape = jax.ShapeDtypeStruct((n_t, 1, LANES), F32)
    return pl.pallas_call(
        _route_kernel,
        grid=(n_t,),
        in_specs=[pl.BlockSpec((tm, LANES), lambda i: (i, 0))],
        out_specs=[kspec, kspec, tspec],
        out_shape=[small(F32), small(I32), tshape],
        compiler_params=_cparams(("arbitrary",)),
        name="route",
    )(logits)


def _run_copies(meta_ref, base, src_of, dst_of, sem, start):
    def table(first, length, rows):
        def per_dma(q, carry):
            local = pl.multiple_of(meta_ref[base + first + q], MOE_PIECE)
            slot = pl.multiple_of(meta_ref[base + first + length + q], MOE_PIECE)
            cp = pltpu.make_async_copy(src_of(local, slot, rows), dst_of(local, slot, rows), sem)
            if start:
                cp.start()
            else:
                cp.wait()
            return carry
        return per_dma

    for k, (mult, length, first) in enumerate(MOE_META_TABLES):
        lax.fori_loop(0, meta_ref[base + k], table(first, length, mult * MOE_PIECE), 0)


def _dispatch_kernel(meta_ref, prev_ref, zmeta_ref, lpos_ref, h_ref, xs_ref, buf_ref, buf1_ref, zero_ref,
                     sem, sem1, *, tm_e):
    tm = MOE_TOK
    step = pl.program_id(0)
    bufs, sems = (buf_ref, buf1_ref), (sem, sem1)
    rows = lambda ref, r, n=MOE_PIECE: ref.at[pl.ds(r, n)]

    @pl.when(step == 0)
    def _():
        zero_ref[...] = jnp.zeros_like(zero_ref)

        def zero_tail(start):
            def per_expert(e, carry):
                z0 = pl.multiple_of(zmeta_ref[e], MOE_PIECE)

                def per_piece(p, c2):
                    cp = pltpu.make_async_copy(zero_ref, rows(xs_ref, z0 + p * MOE_PIECE), sem)
                    if start:
                        cp.start()
                    else:
                        cp.wait()
                    return c2

                return lax.fori_loop(0, zmeta_ref[N_EXPERTS + e], per_piece, carry)

            lax.fori_loop(0, N_EXPERTS, per_expert, 0)

        zero_tail(True)
        zero_tail(False)

        buf_ref[0:tm_e, :] = jnp.zeros((tm_e, D_MODEL // 2), U32)

        def zero_tiles(start):
            def per_tile(p, carry):
                t0 = pl.multiple_of((zmeta_ref[2 * N_EXPERTS] + p) * tm_e, tm_e)
                cp = pltpu.make_async_copy(buf_ref.at[pl.ds(0, tm_e)], xs_ref.at[pl.ds(t0, tm_e)], sem)
                if start:
                    cp.start()
                else:
                    cp.wait()
                return carry

            lax.fori_loop(0, zmeta_ref[2 * N_EXPERTS + 1], per_tile, 0)

        zero_tiles(True)
        zero_tiles(False)

    def group(j):
        s_iota = lax.broadcasted_iota(I32, (MOE_BUF, tm), 0)
        perm = jnp.zeros((MOE_BUF, tm), F32)
        for k in range(TOP_K):
            perm = jnp.where(s_iota == lpos_ref[k:k + 1, j * tm:(j + 1) * tm], 1.0, perm)
        bufs[j][...] = _pack_bf16_pairs(_dot(perm.astype(BF16), h_ref[j * tm:(j + 1) * tm, :].astype(BF16)))

    def copies(mref, j, start):
        _run_copies(mref, j * MOE_META, lambda loc, slot, n: rows(bufs[j], loc, n),
                    lambda loc, slot, n: rows(xs_ref, slot, n), sems[j], start)

    group(0)
    copies(meta_ref, 0, True)

    @pl.when(step > 0)
    def _():
        copies(prev_ref, 1, False)

    group(1)
    copies(meta_ref, 1, True)
    copies(meta_ref, 0, False)

    @pl.when(step == pl.num_programs(0) - 1)
    def _():
        copies(meta_ref, 1, False)


def _dispatch(meta, zmeta, lpos_t, h2, n_slots, tm_e):
    n_tok = h2.shape[0]
    tm = 2 * MOE_TOK
    assert tm_e <= MOE_BUF and n_tok % tm == 0
    return pl.pallas_call(
        functools.partial(_dispatch_kernel, tm_e=tm_e),
        grid=(n_tok // tm,),
        in_specs=[pl.BlockSpec((2 * MOE_META,), lambda i: (i,), memory_space=pltpu.SMEM),
                  pl.BlockSpec((2 * MOE_META,), lambda i: (jnp.maximum(i - 1, 0),), memory_space=pltpu.SMEM),
                  pl.BlockSpec((MOE_META,), lambda i: (0,), memory_space=pltpu.SMEM),
                  pl.BlockSpec((TOP_K, tm), lambda i: (0, i)),
                  pl.BlockSpec((tm, D_MODEL), lambda i: (i, 0))],
        out_specs=pl.BlockSpec(memory_space=pl.ANY),
        out_shape=jax.ShapeDtypeStruct((n_slots, D_MODEL // 2), U32),
        scratch_shapes=[pltpu.VMEM((MOE_BUF, D_MODEL // 2), U32), pltpu.VMEM((MOE_BUF, D_MODEL // 2), U32),
                        pltpu.VMEM((MOE_PIECE, D_MODEL // 2), U32),
                        pltpu.SemaphoreType.DMA, pltpu.SemaphoreType.DMA],
        compiler_params=_cparams(("arbitrary",)),
        name="moe_dispatch",
    )(meta, meta, zmeta, lpos_t, h2)


def _pack_bf16_pairs(x):
    w = x.shape[1] // 2
    xb = x.astype(BF16).astype(F32)
    lo = lax.shift_right_logical(lax.bitcast_convert_type(xb[:, :w], U32), jnp.uint32(16))
    hi = lax.bitcast_convert_type(xb[:, w:], U32) & jnp.uint32(0xFFFF0000)
    return hi | lo


def _unpack_bf16_pairs(p):
    lo = lax.bitcast_convert_type(lax.shift_left(p, jnp.uint32(16)), F32)
    hi = lax.bitcast_convert_type(p & jnp.uint32(0xFFFF0000), F32)
    return jnp.concatenate([lo, hi], axis=1).astype(BF16)


def _expert_kernel(te_ref, first_ref, nu_ref, xs_ref, wgu_ref, bgu_ref, wd_ref, bd_ref, y_ref,
                   wgu_b_ref, wd_b_ref):
    del te_ref
    i = pl.program_id(0)

    @pl.when(first_ref[i] == 1)
    def _():
        wgu_b_ref[...] = wgu_ref[...].astype(BF16)
        wd_b_ref[...] = wd_ref[...].astype(BF16)

    @pl.when(i < nu_ref[0])
    def _():
        gu = _dot(_unpack_bf16_pairs(xs_ref[...]), wgu_b_ref[...]) + bgu_ref[...]
        g = jnp.minimum(gu[:, :D_EXPERT], SWIGLU_LIMIT)
        lin = jnp.clip(gu[:, D_EXPERT:], -SWIGLU_LIMIT, SWIGLU_LIMIT)
        act = g * _sigmoid(SWIGLU_ALPHA * g) * (lin + 1.0)
        y_ref[...] = _pack_bf16_pairs(_dot(act.astype(BF16), wd_b_ref[...]) + bd_ref[...])

    @pl.when(i >= nu_ref[0])
    def _():
        y_ref[...] = jnp.zeros_like(y_ref)


def _experts(tile_expert, n_used, xs, wgu, bgu, wd, bd, layer, tm):
    n_slots = xs.shape[0]
    n_tiles = n_slots // tm
    first = jnp.concatenate([jnp.ones((1,), I32),
                             (tile_expert[1:] != tile_expert[:-1]).astype(I32)])

    def row(i, te, fi, nu):
        return (jnp.minimum(i, nu[0] - 1), 0)

    grid_spec = pltpu.PrefetchScalarGridSpec(
        num_scalar_prefetch=3,
        grid=(n_tiles,),
        in_specs=[pl.BlockSpec((tm, D_MODEL // 2), row),
                  pl.BlockSpec((None, None, D_MODEL, 2 * D_EXPERT), lambda i, te, fi, nu: (layer, te[i], 0, 0)),
                  pl.BlockSpec((None, None, 1, 2 * D_EXPERT), lambda i, te, fi, nu: (layer, te[i], 0, 0)),
                  pl.BlockSpec((None, None, D_EXPERT, D_MODEL), lambda i, te, fi, nu: (layer, te[i], 0, 0)),
                  pl.BlockSpec((None, None, 1, D_MODEL), lambda i, te, fi, nu: (layer, te[i], 0, 0))],
        out_specs=pl.BlockSpec((tm, D_MODEL // 2), lambda i, te, fi, nu: (i, 0)),
        scratch_shapes=[pltpu.VMEM((D_MODEL, 2 * D_EXPERT), BF16), pltpu.VMEM((D_EXPERT, D_MODEL), BF16)],
    )
    return pl.pallas_call(
        _expert_kernel,
        grid_spec=grid_spec,
        out_shape=jax.ShapeDtypeStruct((n_slots, D_MODEL // 2), U32),
        compiler_params=_cparams(("arbitrary",)),
        name="moe_experts",
    )(tile_expert, first, n_used, xs, wgu, bgu, wd, bd)


def _combine_kernel(meta_ref, next_ref, lpos_ref, gate_ref, x_ref, mod_ref, post_ref, y_ref, xo_ref,
                    buf_ref, buf1_ref, sem, sem1):
    tm = MOE_TOK
    step = pl.program_id(0)
    bufs, sems = (buf_ref, buf1_ref), (sem, sem1)
    rows = lambda ref, r, n=MOE_PIECE: ref.at[pl.ds(r, n)]

    def copies(mref, j, base, start):
        _run_copies(mref, base, lambda loc, slot, n: rows(y_ref, slot, n),
                    lambda loc, slot, n: rows(bufs[j], loc, n), sems[j], start)

    def reduce_tile(j):
        tok = slice(j * tm, (j + 1) * tm)
        s_iota = lax.broadcasted_iota(I32, (tm, MOE_BUF), 1)
        sel = jnp.zeros((tm, MOE_BUF), F32)
        for k in range(TOP_K):
            sel = jnp.where(s_iota == lpos_ref[tok, k:k + 1], gate_ref[tok, k:k + 1], sel)
        y = _dot(sel.astype(BF16), _unpack_bf16_pairs(bufs[j][...]))
        gate2 = mod_ref[:, 5 * D_MODEL:6 * D_MODEL]
        xo_ref[tok, :] = x_ref[tok, :] + gate2 * _rms(y, post_ref[...])

    @pl.when(step == 0)
    def _():
        buf_ref[...] = jnp.zeros_like(buf_ref)
        buf1_ref[...] = jnp.zeros_like(buf1_ref)
        copies(meta_ref, 0, 0, True)

    copies(meta_ref, 1, MOE_META, True)
    copies(meta_ref, 0, 0, False)
    reduce_tile(0)

    @pl.when(step < pl.num_programs(0) - 1)
    def _():
        copies(next_ref, 0, 0, True)

    copies(meta_ref, 1, MOE_META, False)
    reduce_tile(1)


def _combine(meta, lpos, gate, x_mid, mod, mod_row, post_g, y):
    n_tok = x_mid.shape[0]
    tm = 2 * MOE_TOK
    n_steps = n_tok // tm
    return pl.pallas_call(
        _combine_kernel,
        grid=(n_steps,),
        in_specs=[pl.BlockSpec((2 * MOE_META,), lambda i: (i,), memory_space=pltpu.SMEM),
                  pl.BlockSpec((2 * MOE_META,), lambda i: (jnp.minimum(i + 1, n_steps - 1),),
                               memory_space=pltpu.SMEM),
                  pl.BlockSpec((tm, TOP_K), lambda i: (i, 0)),
                  pl.BlockSpec((tm, TOP_K), lambda i: (i, 0)),
                  pl.BlockSpec((tm, D_MODEL), lambda i: (i, 0)),
                  pl.BlockSpec((None, 1, 6 * D_MODEL), lambda i: (mod_row(i * tm), 0, 0)),
                  pl.BlockSpec((1, D_MODEL), lambda i: (0, 0)),
                  pl.BlockSpec(memory_space=pl.ANY)],
        out_specs=pl.BlockSpec((tm, D_MODEL), lambda i: (i, 0)),
        out_shape=jax.ShapeDtypeStruct((n_tok, D_MODEL), F32),
        scratch_shapes=[pltpu.VMEM((MOE_BUF, D_MODEL // 2), U32), pltpu.VMEM((MOE_BUF, D_MODEL // 2), U32),
                        pltpu.SemaphoreType.DMA, pltpu.SemaphoreType.DMA],
        compiler_params=_cparams(("arbitrary",)),
        name="moe_combine",
    )(meta, meta, lpos, gate, x_mid, mod, post_g, y)


def _moe(h2, logits, x_mid, mod, mod_row, lw, tm_e=512):
    n_tok = h2.shape[0]
    n_t = n_tok // MOE_TOK
    gate, lpos, tcnt = _route(logits)
    tcnt = tcnt[:, 0, :N_EXPERTS].astype(I32)
    pieces = (tcnt + MOE_PIECE - 1) // MOE_PIECE
    run_end = jnp.cumsum(pieces, axis=0) * MOE_PIECE
    used = run_end[-1]
    padded = (used + tm_e - 1) // tm_e * tm_e
    pad_end = jnp.cumsum(padded)
    offs = pad_end - padded
    n_tiles = (n_tok * TOP_K + n_t * N_EXPERTS * (MOE_PIECE - 1) + tm_e - 1) // tm_e + N_EXPERTS
    tile_start = jnp.arange(n_tiles, dtype=I32) * tm_e
    tile_expert = jnp.minimum(jnp.sum(pad_end[None, :] <= tile_start[:, None], axis=1),
                              N_EXPERTS - 1).astype(I32)
    n_used = (pad_end[-1:] // tm_e).astype(I32)
    lstart = (jnp.cumsum(pieces, axis=1) - pieces) * MOE_PIECE
    slot_start = offs[None, :] + run_end - pieces * MOE_PIECE

    def table(count, rows, first_row, length):
        end = jnp.cumsum(count, axis=1)
        q = jnp.arange(length, dtype=I32)
        passed = q[None, :, None] >= end[:, None, :]
        owner = passed != jnp.concatenate([jnp.ones_like(passed[..., :1]), passed[..., :-1]], axis=-1)
        within = (q[None, :, None] - (end - count)[:, None, :]) * rows + first_row[:, None, :]
        pick = lambda base: jnp.sum(jnp.where(owner, base[:, None, :] + within, 0), axis=-1)
        return end[:, -1:], pick(lstart), pick(slot_start)

    counts, tables, left, done_rows = [], [], pieces, jnp.zeros_like(pieces)
    for mult, length, _ in MOE_META_TABLES:
        n, local, slot = table(left // mult, mult * MOE_PIECE, done_rows, length)
        counts.append(n)
        tables += [local, slot]
        done_rows = done_rows + (left // mult) * mult * MOE_PIECE
        left = left % mult
    meta = jnp.concatenate(counts + tables, axis=1)
    meta = jnp.pad(meta, ((0, 0), (0, MOE_META - meta.shape[1]))).reshape(-1).astype(I32)
    z0 = offs + used
    tail = jnp.stack([n_used[0], n_tiles - n_used[0]])
    zmeta = jnp.concatenate([z0, (pad_end - z0) // MOE_PIECE, tail,
                             jnp.zeros((MOE_META - 2 * N_EXPERTS - 2,), I32)]).astype(I32)
    xs = _dispatch(meta, zmeta, lpos.T, h2, n_tiles * tm_e, tm_e)
    y = _experts(tile_expert, n_used, xs, lw["exp_w_gu"], lw["exp_b_gu"], lw["exp_w_down"],
                 lw["exp_b_down"], lw["layer"], tm_e)
    return _combine(meta, lpos, gate, x_mid, mod, mod_row, lw["norm_post_ffn"], y)


def _split_w_in(w_in):
    offs, o = {}, 0
    for name, width in (("dn_k", 1024), ("dn_v", 1024), ("dn_a", 16), ("dn_b", 16), ("at_k", 128),
                        ("at_v", 128), ("dn_q", 1024), ("dn_g", 1024), ("at_q", 1024),
                        ("sg_u", 1024), ("sg_v", 1024), ("gates", 3072)):
        offs[name] = (o, o + width)
        o += width
    sl = lambda n: w_in[:, offs[n][0]:offs[n][1]]
    w_main = jnp.concatenate([sl(n) for n in ("dn_k", "dn_v", "dn_q", "dn_g", "at_q", "sg_u", "sg_v",
                                              "gates")], axis=1).astype(BF16)
    pad = jnp.zeros((w_in.shape[0], N_SMALL_COLS - 2 * LANES - 4 * DN_HEADS), w_in.dtype)
    w_small = jnp.concatenate([sl("at_k"), sl("at_v"), sl("dn_a"), sl("dn_b"), pad], axis=1).astype(BF16)
    return w_main, w_small


def _dn_gates(small, B, T):
    ab = small[:, 2 * LANES:2 * LANES + 4 * DN_HEADS].reshape(B, T, 2, 2, DN_HEADS)
    col = jnp.transpose(ab, (3, 0, 1, 2, 4)).reshape(2, B, T, 2 * DN_HEADS)
    row = jnp.transpose(col.reshape(2, B, T // DN_CHUNK, DN_CHUNK, 2 * DN_HEADS), (0, 1, 2, 4, 3))
    return col, row


def kernel(x, c, ctx, c_ctx, w_mod, b_mod, norm_pre_mix, norm_post_mix, norm_pre_ffn, norm_post_ffn, w_in, sg_ln_g, sg_ln_b, sg_w, sg_b, dn_conv_w, dn_a_log, dn_dt_bias, dn_norm_g, at_sinks, w_proj_sg, w_proj_dn, w_proj_at, w_out, router_w, router_b, exp_w_gu, exp_b_gu, exp_w_down, exp_b_down):
    B, S, D = x.shape
    L = ctx.shape[1]
    depth = w_mod.shape[0]
    assert D == D_MODEL and S % GRID_W == 0
    n_lat, n_ctx = B * S, B * L

    rows = (B + 1 + 7) // 8 * 8
    cvec = jnp.zeros((rows, D), F32).at[:B].set(c).at[B].set(c_ctx)
    mod_all = _modulation(cvec, w_mod, b_mod)
    tables = _rope_tables(S)

    lat_row = lambda t: t // S
    ctx_row = lambda t: B
    all_row = lambda t: jnp.where(t < n_lat, t // S, B)

    xl = x.reshape(n_lat, D)
    xc = ctx.reshape(n_ctx, D)
    for l in range(depth):
        need_ctx_out = l < depth - 1
        mod = mod_all[l].reshape(rows, 1, 6 * D)
        w_main, w_small = _split_w_in(w_in[l])
        lw = {
            "dn_norm_g": dn_norm_g[l].reshape(1, -1),
            "norm_post_mix": norm_post_mix[l].reshape(1, -1),
            "norm_pre_ffn": norm_pre_ffn[l].reshape(1, -1),
            "norm_post_ffn": norm_post_ffn[l].reshape(1, -1),
            "w_proj_sg": w_proj_sg[l].astype(BF16), "w_proj_dn": w_proj_dn[l].astype(BF16),
            "w_proj_at": w_proj_at[l].astype(BF16), "w_out": w_out[l].astype(BF16),
            "router_w": jnp.pad(router_w[l], ((0, 0), (0, LANES - N_EXPERTS))).astype(BF16),
            "router_b": jnp.pad(router_b[l], (0, LANES - N_EXPERTS),
                                constant_values=NEG_BIG).reshape(1, -1),
            "layer": l,
            "exp_w_gu": exp_w_gu, "exp_b_gu": exp_b_gu.reshape(depth, N_EXPERTS, 1, -1),
            "exp_w_down": exp_w_down, "exp_b_down": exp_b_down.reshape(depth, N_EXPERTS, 1, -1),
        }
        pre_g = norm_pre_mix[l].reshape(1, -1)
        main, small = _inproj(xl, mod, lat_row, pre_g, w_main, w_small, min(1024, S))
        w_main_c = w_main if need_ctx_out else w_main[:, :N_CTX_MAIN_COLS]
        main_c, small_c = _inproj(xc, mod, ctx_row, pre_g, w_main_c, w_small, min(1024, n_ctx))

        sg_args = (sg_ln_g[l].reshape(1, -1), sg_ln_b[l].reshape(1, -1), sg_w[l].astype(BF16),
                   sg_b[l].T)
        ysg = _sgu(main, *sg_args)

        gcol_c, grow_c = _dn_gates(small_c, B, L)
        gcol, grow = _dn_gates(small, B, S)
        s0 = jnp.zeros((2, B, DN_HEADS, DN_DIM, DN_DIM), F32)
        of_c, ob_c, s_ctx = _deltanet(main_c.reshape(B, L, -1), gcol_c, grow_c, dn_conv_w[l], dn_a_log[l],
                                      dn_dt_bias[l], s0, need_ctx_out)
        of_l, ob_l, _ = _deltanet(main.reshape(B, S, -1), gcol, grow, dn_conv_w[l], dn_a_log[l],
                                  dn_dt_bias[l], s_ctx, True)

        sinks = jnp.repeat(at_sinks[l], AT_BLOCK).reshape(-1, 1)
        q_r, k_r, v_r = _rope(main, small, tables, S)
        yat = _attention_local(q_r, k_r, v_r, small_c, sinks, B, S, L)

        x_mid, h2, logits = _merge(ysg, of_l.reshape(n_lat, D), ob_l.reshape(n_lat, D), main, yat, xl, mod,
                                   lat_row, lw)
        if need_ctx_out:
            ysg_c = _sgu(main_c, *sg_args)
            yat_c = _attention_ctx(main_c, small_c, sinks, B, L)
            xc_mid, h2c, logits_c = _merge(ysg_c, of_c.reshape(n_ctx, D), ob_c.reshape(n_ctx, D), main_c,
                                           yat_c, xc, mod, ctx_row, lw)
            x_mid = jnp.concatenate([x_mid, xc_mid], axis=0)
            h2 = jnp.concatenate([h2, h2c], axis=0)
            logits = jnp.concatenate([logits, logits_c], axis=0)
            xo = _moe(h2, logits, x_mid, mod, all_row, lw)
            xl, xc = xo[:n_lat], xo[n_lat:]
        else:
            xl = _moe(h2, logits, x_mid, mod, lat_row, lw)
    return xl.reshape(B, S, D)
```

```python
import functools
import math

import jax
import jax.numpy as jnp
from jax import lax
from jax.experimental import pallas as pl
from jax.experimental.pallas import tpu as pltpu

F32 = jnp.float32
BF16 = jnp.bfloat16
I32 = jnp.int32
U32 = jnp.uint32

EPS = 1e-6
D_MODEL = 1024
GRID_W = 64

SG_CHUNK = 128
SG_GROUPS = 8

DN_HEADS = 8
DN_DIM = 128
DN_CONV = 5
DN_CHUNK = 64
DN_HALO = 16
DN_PREP_CHUNKS = 2
DN_SCAN_CHUNKS = 4

AT_Q_HEADS = 16
AT_KV_HEADS = 2
AT_DIM = 64
AT_BLOCK = 128
MERGE_ROW_GROUPS = 2
AT_STAGE_HEADS = 2
ROPE_BASE = 10000.0

N_EXPERTS = 32
TOP_K = 4
D_EXPERT = 1024
SWIGLU_ALPHA = 1.702
SWIGLU_LIMIT = 7.0

LANES = 128
NEG_BIG = -1e30

COL_DN_K, COL_DN_V, COL_DN_Q, COL_DN_G, COL_AT_Q, COL_SG_U, COL_SG_V, COL_GATE0 = range(8)
N_CTX_MAIN_COLS = 2 * D_MODEL
N_SMALL_COLS = 3 * LANES

VMEM_LIMIT = 52 * 1024 * 1024


def _cparams(sem):
    return pltpu.CompilerParams(dimension_semantics=sem, vmem_limit_bytes=VMEM_LIMIT)


def _dot(a, b):
    return jnp.dot(a, b, preferred_element_type=F32)


def _dot_nt(a, b):
    return lax.dot_general(a, b, (((1,), (1,)), ((), ())), preferred_element_type=F32)


def _dot_tn(a, b):
    return lax.dot_general(a, b, (((0,), (0,)), ((), ())), preferred_element_type=F32)


def _sigmoid(x):
    return 0.5 * (1.0 + jnp.tanh(0.5 * x))


def _silu(x):
    return x * _sigmoid(x)


def _gelu_tanh(x):
    return 0.5 * x * (1.0 + jnp.tanh(math.sqrt(2.0 / math.pi) * (x + 0.044715 * (x * x * x))))


def _softplus(x):
    return jnp.maximum(x, 0.0) + jnp.log(1.0 + jnp.exp(-jnp.abs(x)))


def _rms(x, g):
    return x * lax.rsqrt(jnp.mean(x * x, axis=-1, keepdims=True) + EPS) * g


def _mod_kernel(c_ref, w_ref, b_ref, o_ref):
    s = _silu(c_ref[...])
    o_ref[...] = jnp.dot(s, w_ref[...], preferred_element_type=F32,
                         precision=lax.Precision.HIGHEST) + b_ref[...]


def _modulation(cvec, w_mod, b_mod):
    depth = w_mod.shape[0]
    rows = cvec.shape[0]
    n_col = w_mod.shape[2] // D_MODEL
    return pl.pallas_call(
        _mod_kernel,
        grid=(depth, n_col),
        in_specs=[pl.BlockSpec((rows, D_MODEL), lambda l, j: (0, 0)),
                  pl.BlockSpec((None, D_MODEL, D_MODEL), lambda l, j: (l, 0, j)),
                  pl.BlockSpec((None, 1, D_MODEL), lambda l, j: (l, 0, j))],
        out_specs=pl.BlockSpec((None, rows, D_MODEL), lambda l, j: (l, 0, j)),
        out_shape=jax.ShapeDtypeStruct((depth, rows, w_mod.shape[2]), F32),
        compiler_params=_cparams(("arbitrary", "arbitrary")),
        name="modulation",
    )(cvec, w_mod, b_mod.reshape(depth, 1, -1))


def _inproj_kernel(x_ref, mod_ref, g_ref, wm_ref, ws_ref, main_ref, small_ref, h_ref):
    @pl.when(pl.program_id(1) == 0)
    def _():
        sh = mod_ref[:, 0 * D_MODEL:1 * D_MODEL]
        sc = mod_ref[:, 1 * D_MODEL:2 * D_MODEL]
        h = (_rms(x_ref[...], g_ref[...]) * (1.0 + sc) + sh).astype(BF16)
        h_ref[...] = h
        small_ref[...] = _dot(h, ws_ref[...])

    main_ref[...] = _dot(h_ref[...], wm_ref[...]).astype(BF16)


def _inproj(x, mod, mod_row, norm_g, w_main, w_small, tm, tn=2048):
    n_tok = x.shape[0]
    n_main = w_main.shape[1]
    return pl.pallas_call(
        _inproj_kernel,
        grid=(n_tok // tm, n_main // tn),
        in_specs=[pl.BlockSpec((tm, D_MODEL), lambda i, j: (i, 0)),
                  pl.BlockSpec((None, 1, 6 * D_MODEL), lambda i, j: (mod_row(i * tm), 0, 0)),
                  pl.BlockSpec((1, D_MODEL), lambda i, j: (0, 0)),
                  pl.BlockSpec((D_MODEL, tn), lambda i, j: (0, j)),
                  pl.BlockSpec((D_MODEL, N_SMALL_COLS), lambda i, j: (0, 0))],
        out_specs=[pl.BlockSpec((tm, tn), lambda i, j: (i, j)),
                   pl.BlockSpec((tm, N_SMALL_COLS), lambda i, j: (i, 0))],
        out_shape=[jax.ShapeDtypeStruct((n_tok, n_main), BF16),
                   jax.ShapeDtypeStruct((n_tok, N_SMALL_COLS), F32)],
        scratch_shapes=[pltpu.VMEM((tm, D_MODEL), BF16)],
        compiler_params=_cparams(("arbitrary", "arbitrary")),
        name="inproj",
    )(x, mod, norm_g, w_main, w_small)


def _sgu_kernel(u_ref, v_ref, lng_ref, lnb_ref, ws_ref, bs_ref, o_ref, *, n_chunk):
    def activate(n):
        rows = slice(n * SG_CHUNK, (n + 1) * SG_CHUNK)
        u = _gelu_tanh(u_ref[rows, :].astype(F32))
        v = _gelu_tanh(v_ref[rows, :].astype(F32))
        vc = v - jnp.mean(v, axis=-1, keepdims=True)
        var = jnp.mean(vc * vc, axis=-1, keepdims=True)
        return u, (vc * lax.rsqrt(var + EPS) * lng_ref[...] + lnb_ref[...]).astype(BF16)

    def mix(n, u, vn):
        rows = slice(n * SG_CHUNK, (n + 1) * SG_CHUNK)
        for g in range(SG_GROUPS):
            cols = slice(g * LANES, (g + 1) * LANES)
            mixed = _dot(ws_ref[g], vn[:, cols]) + bs_ref[:, g:g + 1]
            o_ref[rows, cols] = (u[:, cols] * mixed).astype(BF16)

    nxt = activate(0)
    for n in range(n_chunk):
        cur = nxt
        if n + 1 < n_chunk:
            nxt = activate(n + 1)
        mix(n, *cur)


def _sgu(main, sg_ln_g, sg_ln_b, sg_w, sg_bt, n_chunk=4):
    n_tok = main.shape[0]
    tc = n_chunk * SG_CHUNK
    return pl.pallas_call(
        functools.partial(_sgu_kernel, n_chunk=n_chunk),
        grid=(n_tok // tc,),
        in_specs=[pl.BlockSpec((tc, D_MODEL), lambda i: (i, COL_SG_U)),
                  pl.BlockSpec((tc, D_MODEL), lambda i: (i, COL_SG_V)),
                  pl.BlockSpec((1, D_MODEL), lambda i: (0, 0)),
                  pl.BlockSpec((1, D_MODEL), lambda i: (0, 0)),
                  pl.BlockSpec((SG_GROUPS, SG_CHUNK, SG_CHUNK), lambda i: (0, 0, 0)),
                  pl.BlockSpec((SG_CHUNK, SG_GROUPS), lambda i: (0, 0))],
        out_specs=pl.BlockSpec((tc, D_MODEL), lambda i: (i, 0)),
        out_shape=jax.ShapeDtypeStruct((n_tok, D_MODEL), BF16),
        compiler_params=_cparams(("arbitrary",)),
        name="sgu",
    )(main, main, sg_ln_g, sg_ln_b, sg_w, sg_bt)


def _dn_prep_kernel(*refs, with_q, n_chunks):
    if with_q:
        (qp_ref, qc_ref, qn_ref, kp_ref, kc_ref, kn_ref, vp_ref, vc_ref, vn_ref,
         gcol_ref, grow_ref, cw_ref, alog_r_ref, alog_c_ref, dtb_r_ref, dtb_c_ref,
         w_ref, u0_ref, ke_ref, gt_ref, qs_ref, qk_ref) = refs
    else:
        (kp_ref, kc_ref, kn_ref, vp_ref, vc_ref, vn_ref,
         gcol_ref, grow_ref, cw_ref, alog_r_ref, alog_c_ref, dtb_r_ref, dtb_c_ref,
         w_ref, u0_ref, ke_ref, gt_ref) = refs
    c = pl.program_id(1)
    C = DN_CHUNK
    R = DN_PREP_CHUNKS * C
    has_prev = (c > 0).astype(BF16)
    has_next = (c < n_chunks // DN_PREP_CHUNKS - 1).astype(BF16)

    pad = DN_CONV // 2
    n_sh = DN_CONV - 1
    sr = lax.broadcasted_iota(I32, (n_sh * R, R + 2 * DN_HALO), 0)
    sc = lax.broadcasted_iota(I32, (n_sh * R, R + 2 * DN_HALO), 1)
    blk = sr // R
    off = jnp.where(blk < pad, blk - pad, blk - pad + 1)
    shift_mat = (sc == DN_HALO + (sr - blk * R) + off).astype(BF16)

    def conv_silu(p_ref, c_ref, n_ref, part):
        cur = c_ref[...]
        ext = jnp.concatenate([p_ref[...] * has_prev, cur, n_ref[...] * has_next], axis=0)
        sh = _dot(shift_mat, ext)
        taps = [sh[j * R:(j + 1) * R] for j in range(pad)] + [cur.astype(F32)] + \
               [sh[j * R:(j + 1) * R] for j in range(pad, n_sh)]
        y = None
        for i in range(DN_CONV):
            t = taps[i] * cw_ref[i:i + 1, part * D_MODEL:(part + 1) * D_MODEL]
            y = t if y is None else y + t
        return _silu(y)

    k_all = conv_silu(kp_ref, kc_ref, kn_ref, 1)
    v_all = conv_silu(vp_ref, vc_ref, vn_ref, 2)
    q_all = conv_silu(qp_ref, qc_ref, qn_ref, 0) if with_q else None

    ri = lax.broadcasted_iota(I32, (C, C), 0)
    ci = lax.broadcasted_iota(I32, (C, C), 1)
    eye = (ri == ci).astype(F32)
    CC = range(DN_PREP_CHUNKS)
    H = range(DN_HEADS)
    CH = [(cc, h) for cc in CC for h in H]
    ch = {key: i for i, key in enumerate(CH)}
    rows = [slice(cc * C, (cc + 1) * C) for cc in CC]
    lanes = [slice(h * DN_DIM, (h + 1) * DN_DIM) for h in H]
    kh = [k_all[rows[cc], lanes[h]] for cc, h in CH]
    kh = [k * lax.rsqrt(jnp.sum(k * k, axis=-1, keepdims=True) + EPS) for k in kh]
    kb = [k.astype(BF16) for k in kh]
    vh = [v_all[rows[cc], lanes[h]] for cc, h in CH]
    if with_q:
        qh = [q_all[rows[cc], lanes[h]] for cc, h in CH]
        qh = [q * (lax.rsqrt(jnp.sum(q * q, axis=-1, keepdims=True) + EPS) * DN_DIM ** -0.5) for q in qh]
        gram = [_dot_nt(jnp.concatenate([kb[j], qh[j].astype(BF16)], axis=0), kb[j]) for j in range(len(CH))]
        kk = [g[:C] for g in gram]
        qk_raw = [g[C:] for g in gram]
    else:
        kk = [_dot_nt(k, k) for k in kb]

    D2 = range(2)
    DH = [(cc, d, h) for cc in CC for d in D2 for h in H]
    incl = [(ri >= ci), (ri <= ci)]
    strict = [(ri > ci), (ri < ci)]
    gam_col, gam_row, gam_tot, beta_col = {}, {}, {}, {}
    for cc in CC:
        for d in D2:
            gcol = gcol_ref[d, rows[cc], :]
            ld_col = -jnp.exp(alog_r_ref[d]) * _softplus(gcol[:, 0:DN_HEADS] + dtb_r_ref[d])
            ld_row = -jnp.exp(alog_c_ref[d]) * _softplus(grow_ref[d, cc][0:DN_HEADS, :] + dtb_c_ref[d])
            beta_col[cc, d] = _sigmoid(gcol[:, DN_HEADS:2 * DN_HEADS])
            gam_col[cc, d] = jnp.dot(incl[d].astype(F32), ld_col, preferred_element_type=F32,
                                     precision=lax.Precision.HIGHEST)
            gam_row[cc, d] = jnp.dot(ld_row, incl[1 - d].astype(F32), preferred_element_type=F32,
                                     precision=lax.Precision.HIGHEST)
            tot = jnp.sum(ld_col, axis=0, keepdims=True)
            gam_tot[cc, d] = tot
            gt_ref[d, cc] = tot
    gc = [gam_col[cc, d][:, h:h + 1] for cc, d, h in DH]
    bc = [beta_col[cc, d][:, h:h + 1] for cc, d, h in DH]
    decay = [jnp.exp(jnp.where(incl[d], gc[i] - gam_row[cc, d][h:h + 1, :], NEG_BIG))
             for i, (cc, d, h) in enumerate(DH)]
    x = [-(jnp.where(strict[d], decay[i], 0.0) * bc[i] * kk[ch[cc, h]]) for i, (cc, d, h) in enumerate(DH)]
    N = range(len(DH))
    p = [eye + x[i] for i in N]
    xb = [x[i].astype(BF16) for i in N]
    x = [_dot(xb[i], xb[i]) for i in N]
    n_fac = int(math.log2(C)) - 1
    for j in range(n_fac):
        xb = [x[i].astype(BF16) for i in N]
        if j < n_fac - 1:
            r = [_dot(xb[i], jnp.concatenate([xb[i], p[i].astype(BF16)], axis=1)) for i in N]
            x = [r[i][:, :C] for i in N]
            p = [p[i] + r[i][:, C:] for i in N]
        else:
            p = [p[i] + _dot(xb[i], p[i].astype(BF16)) for i in N]
    rhs = [jnp.concatenate([kh[ch[cc, h]] * (bc[i] * jnp.exp(gc[i])), vh[ch[cc, h]] * bc[i]],
                           axis=1).astype(BF16) for i, (cc, d, h) in enumerate(DH)]
    sol = [_dot(p[i].astype(BF16), rhs[i]) for i in N]
    for i, (cc, d, h) in enumerate(DH):
        j = ch[cc, h]
        w_ref[d, rows[cc], lanes[h]] = sol[i][:, :DN_DIM].astype(BF16)
        u0_ref[d, rows[cc], lanes[h]] = sol[i][:, DN_DIM:].astype(BF16)
        ke_ref[d, rows[cc], lanes[h]] = (kh[j] * jnp.exp(gam_tot[cc, d][:, h:h + 1] - gc[i])).astype(BF16)
        if with_q:
            qs_ref[d, rows[cc], lanes[h]] = (qh[j] * jnp.exp(gc[i])).astype(BF16)
            qk_ref[d, rows[cc], h * C:(h + 1) * C] = (qk_raw[j] * decay[i]).astype(BF16)


def _dn_scan_kernel(*refs, with_q, n_chunks):
    n_state = 2 * DN_HEADS
    s_refs = refs[-n_state:]
    refs = refs[:-n_state]
    if with_q:
        (w0, w1, u0, u1, k0, k1, g0, g1, qs0, qs1, qk0, qk1, s0_ref, o0_ref, o1_ref, sfin_ref) = refs
        qs_r, qk_r, o_r = (qs0, qs1), (qk0, qk1), (o0_ref, o1_ref)
    else:
        (w0, w1, u0, u1, k0, k1, g0, g1, s0_ref, sfin_ref) = refs
    w_r, u_r, k_r, g_r = (w0, w1), (u0, u1), (k0, k1), (g0, g1)
    c = pl.program_id(1)
    C = DN_CHUNK
    DH = [(d, h) for d in range(2) for h in range(DN_HEADS)]
    N = range(len(DH))
    lanes = [slice(h * DN_DIM, (h + 1) * DN_DIM) for h in range(DN_HEADS)]

    @pl.when(c == 0)
    def _():
        for i, (d, h) in enumerate(DH):
            s_refs[i][...] = s0_ref[d, h]

    s = [s_refs[i][...] for i in N]
    for sub in range(DN_SCAN_CHUNKS):
        cix = (sub, DN_SCAN_CHUNKS - 1 - sub)
        rows = [slice(cix[d] * C, (cix[d] + 1) * C) for d in range(2)]
        sb = [s[i].astype(BF16) for i in N]
        w = [w_r[d][rows[d], lanes[h]] for d, h in DH]
        if with_q:
            wq = [jnp.concatenate([w[i], qs_r[d][rows[d], lanes[h]]], axis=0) for i, (d, h) in enumerate(DH)]
            ws = [_dot(wq[i], sb[i]) for i in N]
            ub = [(u_r[d][rows[d], lanes[h]].astype(F32) - ws[i][:C]).astype(BF16)
                  for i, (d, h) in enumerate(DH)]
            qu = [_dot(qk_r[d][rows[d], h * C:(h + 1) * C], ub[i]) for i, (d, h) in enumerate(DH)]
            for i, (d, h) in enumerate(DH):
                o_r[d][rows[d], lanes[h]] = (ws[i][C:] + qu[i]).astype(BF16)
        else:
            ws = [_dot(w[i], sb[i]) for i in N]
            ub = [(u_r[d][rows[d], lanes[h]].astype(F32) - ws[i]).astype(BF16) for i, (d, h) in enumerate(DH)]
        ku = [_dot_tn(k_r[d][rows[d], lanes[h]], ub[i]) for i, (d, h) in enumerate(DH)]
        s = [jnp.exp(g_r[d][cix[d]][:, h:h + 1]) * s[i] + ku[i] for i, (d, h) in enumerate(DH)]
    for i in N:
        s_refs[i][...] = s[i]

    @pl.when(c == n_chunks // DN_SCAN_CHUNKS - 1)
    def _():
        for i, (d, h) in enumerate(DH):
            sfin_ref[d, h] = s_refs[i][...]


def _deltanet(main3, gate_col, gate_row, conv_w, alog, dtb, s0, with_q):
    B, T, _ = main3.shape
    C = DN_CHUNK
    n_chunks = T // C
    NC = DN_PREP_CHUNKS
    R = NC * C
    assert T % R == 0
    hpc = R // DN_HALO
    n_halo = T // DN_HALO

    def trio(col):
        return [pl.BlockSpec((None, DN_HALO, D_MODEL), lambda b, c: (b, jnp.maximum(c * hpc - 1, 0), col)),
                pl.BlockSpec((None, R, D_MODEL), lambda b, c: (b, c, col)),
                pl.BlockSpec((None, DN_HALO, D_MODEL),
                             lambda b, c: (b, jnp.minimum((c + 1) * hpc, n_halo - 1), col))]

    vec = lambda shape: pl.BlockSpec(shape, lambda b, c: (0,) * len(shape))
    in_specs = (trio(COL_DN_Q) if with_q else []) + trio(COL_DN_K) + trio(COL_DN_V) + [
        pl.BlockSpec((2, None, R, 2 * DN_HEADS), lambda b, c: (0, b, c, 0)),
        pl.BlockSpec((2, None, NC, 2 * DN_HEADS, C), lambda b, c: (0, b, c, 0, 0)),
        vec((DN_CONV, 3 * D_MODEL)),
        vec((2, 1, DN_HEADS)), vec((2, DN_HEADS, 1)), vec((2, 1, DN_HEADS)), vec((2, DN_HEADS, 1)),
    ]
    wide = lambda n: (pl.BlockSpec((2, None, R, n), lambda b, c: (0, b, c, 0)),
                      jax.ShapeDtypeStruct((2, B, T, n), BF16))
    outs = [wide(D_MODEL), wide(D_MODEL), wide(D_MODEL),
            (pl.BlockSpec((2, None, NC, 1, DN_HEADS), lambda b, c: (0, b, c, 0, 0)),
             jax.ShapeDtypeStruct((2, B, n_chunks, 1, DN_HEADS), F32))]
    if with_q:
        outs += [wide(D_MODEL), wide(DN_HEADS * C)]
    n_main = 3 if with_q else 2
    prep = pl.pallas_call(
        functools.partial(_dn_prep_kernel, with_q=with_q, n_chunks=n_chunks),
        grid=(B, n_chunks // NC),
        in_specs=in_specs, out_specs=[o[0] for o in outs], out_shape=[o[1] for o in outs],
        compiler_params=_cparams(("arbitrary", "arbitrary")),
        name="dn_prep_q" if with_q else "dn_prep",
    )(*([main3] * (3 * n_main)), gate_col, gate_row, conv_w,
      alog.reshape(2, 1, DN_HEADS), alog.reshape(2, DN_HEADS, 1),
      dtb.reshape(2, 1, DN_HEADS), dtb.reshape(2, DN_HEADS, 1))

    NS = DN_SCAN_CHUNKS
    RS = NS * C
    n_steps = n_chunks // NS
    assert n_chunks % NS == 0

    def both_dirs(arr, n):
        if n is None:
            return [pl.BlockSpec((None, None, NS, 1, DN_HEADS), lambda b, c: (0, b, c, 0, 0)),
                    pl.BlockSpec((None, None, NS, 1, DN_HEADS),
                                 lambda b, c: (1, b, n_steps - 1 - c, 0, 0))], [arr, arr]
        return [pl.BlockSpec((None, None, RS, n), lambda b, c: (0, b, c, 0)),
                pl.BlockSpec((None, None, RS, n), lambda b, c: (1, b, n_steps - 1 - c, 0))], [arr, arr]

    specs, args = [], []
    widths = [D_MODEL, D_MODEL, D_MODEL, None] + ([D_MODEL, DN_HEADS * C] if with_q else [])
    for arr, n in zip(prep, widths):
        sp, ar = both_dirs(arr, n)
        specs += sp
        args += ar
    s_spec = pl.BlockSpec((2, None, DN_HEADS, DN_DIM, DN_DIM), lambda b, c: (0, b, 0, 0, 0))
    s_shape = jax.ShapeDtypeStruct((2, B, DN_HEADS, DN_DIM, DN_DIM), F32)
    if with_q:
        out_specs = [pl.BlockSpec((None, RS, D_MODEL), lambda b, c: (b, c, 0)),
                     pl.BlockSpec((None, RS, D_MODEL), lambda b, c: (b, n_steps - 1 - c, 0)), s_spec]
        out_shape = [jax.ShapeDtypeStruct((B, T, D_MODEL), BF16)] * 2 + [s_shape]
    else:
        out_specs, out_shape = [s_spec], [s_shape]
    out = pl.pallas_call(
        functools.partial(_dn_scan_kernel, with_q=with_q, n_chunks=n_chunks),
        grid=(B, n_steps),
        in_specs=specs + [s_spec], out_specs=out_specs, out_shape=out_shape,
        scratch_shapes=[pltpu.VMEM((DN_DIM, DN_DIM), F32)] * (2 * DN_HEADS),
        compiler_params=_cparams(("arbitrary", "arbitrary")),
        name="dn_scan_q" if with_q else "dn_scan",
    )(*args, s0)
    return (out[0], out[1], out[2]) if with_q else (None, None, out[0])


def _rope_tables(S):
    half = AT_DIM // 2
    nf = half // 2
    inv_freq = ROPE_BASE ** (-jnp.arange(nf, dtype=F32) / nf)
    t = jnp.arange(S, dtype=jnp.int32)
    row = (t // GRID_W).astype(F32)
    col = (t % GRID_W).astype(F32)
    lane = jnp.arange(LANES)
    dd = lane % AT_DIM
    pos = jnp.where((dd < half)[None, :], row[:, None], col[:, None])
    ang = pos * inv_freq[lane % nf][None, :]
    first = ((lane % half) < nf)[None, :]
    sin = jnp.sin(ang)
    return jnp.cos(ang), jnp.where(first, -sin, 0.0), jnp.where(first, 0.0, sin)


def _rope_kernel(q_ref, k_ref, v_ref, cos_ref, sa_ref, sb_ref, qo_ref, ko_ref, vo_ref):
    cos, sa, sb = cos_ref[...], sa_ref[...], sb_ref[...]
    nf = AT_DIM // 4

    def rot(x):
        return x * cos + pltpu.roll(x, LANES - nf, 1) * sa + pltpu.roll(x, nf, 1) * sb

    for j in range(AT_Q_HEADS * AT_DIM // LANES):
        lanes = slice(j * LANES, (j + 1) * LANES)
        qo_ref[:, lanes] = (rot(q_ref[:, lanes].astype(F32)) * AT_DIM ** -0.5).astype(BF16)
    ko_ref[...] = rot(k_ref[...]).astype(BF16)
    vo_ref[...] = v_ref[...].astype(BF16)


def _rope(main, small, tables, S, tm=512):
    n_tok = main.shape[0]
    per_seq = S // tm
    tab_spec = pl.BlockSpec((tm, LANES), lambda i: (i % per_seq, 0))
    return pl.pallas_call(
        _rope_kernel,
        grid=(n_tok // tm,),
        in_specs=[pl.BlockSpec((tm, D_MODEL), lambda i: (i, COL_AT_Q)),
                  pl.BlockSpec((tm, LANES), lambda i: (i, 0)),
                  pl.BlockSpec((tm, LANES), lambda i: (i, 1)),
                  tab_spec, tab_spec, tab_spec],
        out_specs=[pl.BlockSpec((tm, D_MODEL), lambda i: (i, 0)),
                   pl.BlockSpec((tm, LANES), lambda i: (i, 0)),
                   pl.BlockSpec((tm, LANES), lambda i: (i, 0))],
        out_shape=[jax.ShapeDtypeStruct((n_tok, D_MODEL), BF16),
                   jax.ShapeDtypeStruct((n_tok, LANES), BF16),
                   jax.ShapeDtypeStruct((n_tok, LANES), BF16)],
        compiler_params=_cparams(("arbitrary",)),
        name="rope",
    )(main, small, small, *tables)


def _attn_kernel(*refs, local, n_blocks, q_scale):
    if local:
        (q_ref, kp_ref, kc_ref, kn_ref, vp_ref, vc_ref, vn_ref, kx_ref, vx_ref, sink_ref, o_ref) = refs
    else:
        (q_ref, kx_ref, vx_ref, sink_ref, o_ref) = refs
    P = AT_BLOCK
    G = AT_Q_HEADS // AT_KV_HEADS
    kx = kx_ref[...].astype(BF16)
    vx = vx_ref[...].astype(BF16)
    if local:
        i = pl.program_id(1)
        k_all = jnp.concatenate([kp_ref[...], kc_ref[...], kn_ref[...], kx], axis=0)
        v_all = jnp.concatenate([vp_ref[...], vc_ref[...], vn_ref[...], vx], axis=0)
        qi = lax.broadcasted_iota(I32, (P, P), 0)
        kj = lax.broadcasted_iota(I32, (P, P), 1)
        b_prev = jnp.where(kj >= qi, 0.0, NEG_BIG) + jnp.where(i > 0, 0.0, NEG_BIG)
        b_next = jnp.where(kj <= qi, 0.0, NEG_BIG) + jnp.where(i < n_blocks - 1, 0.0, NEG_BIG)
        b_prev2 = jnp.concatenate([b_prev] * AT_STAGE_HEADS, axis=0)
        b_next2 = jnp.concatenate([b_next] * AT_STAGE_HEADS, axis=0)
    else:
        k_all, v_all = kx, vx
    lo = lax.broadcasted_iota(I32, (P, LANES), 1) < AT_DIM
    qf = q_ref[...].astype(F32) * q_scale
    pieces = []
    for qh in range(AT_Q_HEADS):
        blk = qf[:, (qh // 2) * LANES:(qh // 2 + 1) * LANES]
        want_lo = qh // G == 0
        if want_lo != (qh % 2 == 0):
            blk = pltpu.roll(blk, AT_DIM, 1)
        pieces.append(jnp.where(lo if want_lo else ~lo, blk, 0.0).astype(BF16))
    HS = AT_STAGE_HEADS
    n_pair = AT_Q_HEADS // HS

    def logits(j):
        s = _dot_nt(jnp.concatenate(pieces[HS * j:HS * (j + 1)], axis=0), k_all)
        if not local:
            return s
        return jnp.concatenate([s[:, 0:P] + b_prev2, s[:, P:2 * P], s[:, 2 * P:3 * P] + b_next2,
                                s[:, 3 * P:]], axis=1)

    def softmax(j, s):
        sink = sink_ref[HS * j * P:HS * (j + 1) * P, :]
        m = jnp.maximum(jnp.max(s, axis=-1, keepdims=True), sink)
        p = jnp.exp(s - m)
        den = jnp.sum(p, axis=-1, keepdims=True) + jnp.exp(sink - m)
        return p.astype(BF16), den

    def values(j, p, den):
        o = _dot(p, v_all) / den
        for t in range(HS // 2):
            blk = (HS * j) // 2 + t
            a, b = o[2 * t * P:(2 * t + 1) * P], o[(2 * t + 1) * P:(2 * t + 2) * P]
            if (2 * blk) // G == 0:
                out = jnp.where(lo, a, pltpu.roll(b, AT_DIM, 1))
            else:
                out = jnp.where(lo, pltpu.roll(a, AT_DIM, 1), b)
            o_ref[:, blk * LANES:(blk + 1) * LANES] = out.astype(BF16)

    s_next = logits(0)
    prob = None
    for j in range(n_pair):
        s_cur = s_next
        if j + 1 < n_pair:
            s_next = logits(j + 1)
        done = prob
        prob = softmax(j, s_cur)
        if done is not None:
            values(j - 1, *done)
    values(n_pair - 1, *prob)


def _attention_local(q_r, k_r, v_r, small_c, sinks, B, S, L):
    P = AT_BLOCK
    nb = S // P

    def kv_trio():
        return [pl.BlockSpec((P, LANES), lambda b, i: (b * nb + jnp.maximum(i - 1, 0), 0)),
                pl.BlockSpec((P, LANES), lambda b, i: (b * nb + i, 0)),
                pl.BlockSpec((P, LANES), lambda b, i: (b * nb + jnp.minimum(i + 1, nb - 1), 0))]

    return pl.pallas_call(
        functools.partial(_attn_kernel, local=True, n_blocks=nb, q_scale=1.0),
        grid=(B, nb),
        in_specs=[pl.BlockSpec((P, D_MODEL), lambda b, i: (b * nb + i, 0))] + kv_trio() + kv_trio() + [
            pl.BlockSpec((L, LANES), lambda b, i: (b, 0)),
            pl.BlockSpec((L, LANES), lambda b, i: (b, 1)),
            pl.BlockSpec((AT_Q_HEADS * AT_BLOCK, 1), lambda b, i: (0, 0))],
        out_specs=pl.BlockSpec((P, D_MODEL), lambda b, i: (b * nb + i, 0)),
        out_shape=jax.ShapeDtypeStruct((B * S, D_MODEL), BF16),
        compiler_params=_cparams(("arbitrary", "arbitrary")),
        name="attn_local",
    )(q_r, k_r, k_r, k_r, v_r, v_r, v_r, small_c, small_c, sinks)


def _attention_ctx(main_c, small_c, sinks, B, L):
    P = AT_BLOCK
    nb = L // P
    return pl.pallas_call(
        functools.partial(_attn_kernel, local=False, n_blocks=nb, q_scale=AT_DIM ** -0.5),
        grid=(B, nb),
        in_specs=[pl.BlockSpec((P, D_MODEL), lambda b, i: (b * nb + i, COL_AT_Q)),
                  pl.BlockSpec((L, LANES), lambda b, i: (b, 0)),
                  pl.BlockSpec((L, LANES), lambda b, i: (b, 1)),
                  pl.BlockSpec((AT_Q_HEADS * AT_BLOCK, 1), lambda b, i: (0, 0))],
        out_specs=pl.BlockSpec((P, D_MODEL), lambda b, i: (b * nb + i, 0)),
        out_shape=jax.ShapeDtypeStruct((B * L, D_MODEL), BF16),
        compiler_params=_cparams(("arbitrary", "arbitrary")),
        name="attn_ctx",
    )(main_c, small_c, small_c, sinks)


def _merge_kernel(ysg_ref, of_ref, ob_ref, dng_ref, yat_ref, g0_ref, g1_ref, g2_ref, x_ref, mod_ref,
                  dn_norm_ref, post_ref, pre_ref, wsg_ref, wdn_ref, wat_ref, wout_ref, rw_ref, rb_ref,
                  xo_ref, h2_ref, lg_ref):
    dn_g = dn_norm_ref[...]
    gate1 = mod_ref[:, 2 * D_MODEL:3 * D_MODEL]
    sh2 = mod_ref[:, 3 * D_MODEL:4 * D_MODEL]
    sc2 = mod_ref[:, 4 * D_MODEL:5 * D_MODEL]
    n_grp = MERGE_ROW_GROUPS
    grp = x_ref.shape[0] // n_grp

    def dn_out(r):
        rows = slice(r * grp, (r + 1) * grp)
        o = of_ref[rows, :].astype(F32) + ob_ref[rows, :].astype(F32)
        parts = []
        for h in range(DN_HEADS):
            lanes = slice(h * DN_DIM, (h + 1) * DN_DIM)
            parts.append(_rms(o[:, lanes], dn_g) * _silu(dng_ref[rows, lanes].astype(F32)))
        return jnp.concatenate(parts, axis=1).astype(BF16)

    def branches(r, ydn):
        rows = slice(r * grp, (r + 1) * grp)
        m = (_sigmoid(g0_ref[rows, :].astype(F32)) * _dot(ysg_ref[rows, :], wsg_ref[...])
             + _sigmoid(g1_ref[rows, :].astype(F32)) * _dot(ydn, wdn_ref[...])
             + _sigmoid(g2_ref[rows, :].astype(F32)) * _dot(yat_ref[rows, :], wat_ref[...]))
        return m.astype(BF16)

    def out_proj(r, m):
        return _dot(m, wout_ref[...])

    def residual(r, y):
        rows = slice(r * grp, (r + 1) * grp)
        xn = x_ref[rows, :] + gate1 * _rms(y, post_ref[...])
        xo_ref[rows, :] = xn
        h2 = _rms(xn, pre_ref[...]) * (1.0 + sc2) + sh2
        h2_ref[rows, :] = h2
        lg_ref[rows, :] = _dot(h2.astype(BF16), rw_ref[...]) + rb_ref[...]

    stages = (dn_out, branches, out_proj, residual)
    carried = {}
    for t in range(n_grp + len(stages) - 1):
        for k in reversed(range(len(stages))):
            r = t - k
            if 0 <= r < n_grp:
                carried[r] = stages[k](r) if k == 0 else stages[k](r, carried[r])


def _merge(ysg, o_fwd, o_bwd, main, yat, x, mod, mod_row, lw, tm=512):
    n_tok = x.shape[0]
    const = lambda i: (0, 0)
    wspec = pl.BlockSpec((D_MODEL, D_MODEL), const, pipeline_mode=pl.Buffered(1))
    vspec = pl.BlockSpec((1, D_MODEL), const)
    return pl.pallas_call(
        _merge_kernel,
        grid=(n_tok // tm,),
        in_specs=[pl.BlockSpec((tm, D_MODEL), lambda i: (i, 0)),
                  pl.BlockSpec((tm, D_MODEL), lambda i: (i, 0)),
                  pl.BlockSpec((tm, D_MODEL), lambda i: (i, 0)),
                  pl.BlockSpec((tm, D_MODEL), lambda i: (i, COL_DN_G)),
                  pl.BlockSpec((tm, D_MODEL), lambda i: (i, 0)),
                  pl.BlockSpec((tm, D_MODEL), lambda i: (i, COL_GATE0)),
                  pl.BlockSpec((tm, D_MODEL), lambda i: (i, COL_GATE0 + 1)),
                  pl.BlockSpec((tm, D_MODEL), lambda i: (i, COL_GATE0 + 2)),
                  pl.BlockSpec((tm, D_MODEL), lambda i: (i, 0)),
                  pl.BlockSpec((None, 1, 6 * D_MODEL), lambda i: (mod_row(i * tm), 0, 0)),
                  pl.BlockSpec((1, DN_DIM), const), vspec, vspec,
                  wspec, wspec, wspec, wspec,
                  pl.BlockSpec((D_MODEL, LANES), const), pl.BlockSpec((1, LANES), const)],
        out_specs=[pl.BlockSpec((tm, D_MODEL), lambda i: (i, 0)),
                   pl.BlockSpec((tm, D_MODEL), lambda i: (i, 0)),
                   pl.BlockSpec((tm, LANES), lambda i: (i, 0))],
        out_shape=[jax.ShapeDtypeStruct((n_tok, D_MODEL), F32),
                   jax.ShapeDtypeStruct((n_tok, D_MODEL), F32),
                   jax.ShapeDtypeStruct((n_tok, LANES), F32)],
        compiler_params=_cparams(("arbitrary",)),
        name="merge",
    )(ysg, o_fwd, o_bwd, main, yat, main, main, main, x, mod,
      lw["dn_norm_g"], lw["norm_post_mix"], lw["norm_pre_ffn"],
      lw["w_proj_sg"], lw["w_proj_dn"], lw["w_proj_at"], lw["w_out"], lw["router_w"], lw["router_b"])


MOE_TOK = 256
MOE_PIECE = 8
MOE_BUF = MOE_TOK * TOP_K + N_EXPERTS * MOE_PIECE
MOE_META = 256
MOE_DMA_PIECES = (4, 2, 1)
MOE_META_TABLES = []
_off = len(MOE_DMA_PIECES)
for _m in MOE_DMA_PIECES:
    _len = MOE_BUF // (_m * MOE_PIECE) if _m == MOE_DMA_PIECES[0] else N_EXPERTS
    MOE_META_TABLES.append((_m, _len, _off))
    _off += 2 * _len
assert MOE_META >= _off


def _route_kernel(lg_ref, gate_ref, lpos_ref, tcnt_ref):
    tm = lg_ref.shape[0]
    l = lg_ref[...]
    lane = lax.broadcasted_iota(I32, l.shape, 1).astype(F32)
    vals, onehots = [], []
    for k in range(TOP_K):
        m = jnp.max(l, axis=-1, keepdims=True)
        ik = jnp.min(jnp.where(l == m, lane, float(LANES)), axis=-1, keepdims=True)
        oh = lane == ik
        vals.append(m)
        onehots.append(oh)
        l = jnp.where(oh, -jnp.inf, l)
    es = [jnp.exp(v - vals[0]) for v in vals]
    den = es[0] + es[1] + es[2] + es[3]
    sel = jnp.zeros(l.shape, F32)
    for k in range(TOP_K):
        gate_ref[:, k:k + 1] = es[k] / den
        sel = sel + onehots[k].astype(F32)
    ri = lax.broadcasted_iota(I32, (tm, tm), 0)
    ci = lax.broadcasted_iota(I32, (tm, tm), 1)
    before = _dot((ri > ci).astype(BF16), sel.astype(BF16))
    tcnt = jnp.sum(sel, axis=0, keepdims=True)
    tcnt_ref[...] = tcnt
    n_piece = jnp.floor((tcnt + (MOE_PIECE - 1)) * (1.0 / MOE_PIECE))
    ei = lax.broadcasted_iota(I32, (LANES, LANES), 0)
    ej = lax.broadcasted_iota(I32, (LANES, LANES), 1)
    run_start = _dot(jnp.broadcast_to(n_piece, (8, LANES)).astype(BF16),
                     (ei < ej).astype(BF16))[0:1] * float(MOE_PIECE)
    pos = before + run_start
    for k in range(TOP_K):
        lpos_ref[:, k:k + 1] = jnp.sum(jnp.where(onehots[k], pos, 0.0), axis=-1,
                                       keepdims=True).astype(I32)


def _route(logits):
    n_tok = logits.shape[0]
    tm = MOE_TOK
    n_t = n_tok // tm
    small = lambda dt: jax.ShapeDtypeStruct((n_tok, TOP_K), dt)
    kspec = pl.BlockSpec((tm, TOP_K), lambda i: (i, 0))
    tspec = pl.BlockSpec((None, 1, LANES), lambda i: (i, 0, 0))
    tshape = jax.ShapeDtypeStruct((n_t, 1, LANES), F32)
    return pl.pallas_call(
        _route_kernel,
        grid=(n_t,),
        in_specs=[pl.BlockSpec((tm, LANES), lambda i: (i, 0))],
        out_specs=[kspec, kspec, tspec],
        out_shape=[small(F32), small(I32), tshape],
        compiler_params=_cparams(("arbitrary",)),
        name="route",
    )(logits)


def _run_copies(meta_ref, base, src_of, dst_of, sem, start):
    def table(first, length, rows, priority):
        def per_dma(q, carry):
            local = pl.multiple_of(meta_ref[base + first + q], MOE_PIECE)
            slot = pl.multiple_of(meta_ref[base + first + length + q], MOE_PIECE)
            cp = pltpu.make_async_copy(src_of(local, slot, rows), dst_of(local, slot, rows), sem)
            if start:
                cp.start(priority=priority)
            else:
                cp.wait()
            return carry
        return per_dma

    for k, (mult, length, first) in enumerate(MOE_META_TABLES):
        lax.fori_loop(0, meta_ref[base + k], table(first, length, mult * MOE_PIECE, min(k, 1)), 0)


def _dispatch_kernel(meta_ref, prev_ref, zmeta_ref, lpos_ref, h_ref, xs_ref, buf_ref, buf1_ref, zero_ref,
                     sem, sem1, *, tm_e):
    tm = MOE_TOK
    step = pl.program_id(0)
    bufs, sems = (buf_ref, buf1_ref), (sem, sem1)
    rows = lambda ref, r, n=MOE_PIECE: ref.at[pl.ds(r, n)]

    @pl.when(step == 0)
    def _():
        zero_ref[...] = jnp.zeros_like(zero_ref)

        def zero_tail(start):
            def per_expert(e, carry):
                z0 = pl.multiple_of(zmeta_ref[e], MOE_PIECE)

                def per_piece(p, c2):
                    cp = pltpu.make_async_copy(zero_ref, rows(xs_ref, z0 + p * MOE_PIECE), sem)
                    if start:
                        cp.start()
                    else:
                        cp.wait()
                    return c2

                return lax.fori_loop(0, zmeta_ref[N_EXPERTS + e], per_piece, carry)

            lax.fori_loop(0, N_EXPERTS, per_expert, 0)

        zero_tail(True)
        zero_tail(False)

        buf_ref[0:tm_e, :] = jnp.zeros((tm_e, D_MODEL // 2), U32)

        def zero_tiles(start):
            def per_tile(p, carry):
                t0 = pl.multiple_of((zmeta_ref[2 * N_EXPERTS] + p) * tm_e, tm_e)
                cp = pltpu.make_async_copy(buf_ref.at[pl.ds(0, tm_e)], xs_ref.at[pl.ds(t0, tm_e)], sem)
                if start:
                    cp.start()
                else:
                    cp.wait()
                return carry

            lax.fori_loop(0, zmeta_ref[2 * N_EXPERTS + 1], per_tile, 0)

        zero_tiles(True)
        zero_tiles(False)

    def group(j):
        s_iota = lax.broadcasted_iota(I32, (MOE_BUF, tm), 0)
        perm = jnp.zeros((MOE_BUF, tm), F32)
        for k in range(TOP_K):
            perm = jnp.where(s_iota == lpos_ref[k:k + 1, j * tm:(j + 1) * tm], 1.0, perm)
        bufs[j][...] = _pack_bf16_pairs(_dot(perm.astype(BF16), h_ref[j * tm:(j + 1) * tm, :].astype(BF16)))

    def copies(mref, j, start):
        _run_copies(mref, j * MOE_META, lambda loc, slot, n: rows(bufs[j], loc, n),
                    lambda loc, slot, n: rows(xs_ref, slot, n), sems[j], start)

    group(0)
    copies(meta_ref, 0, True)

    @pl.when(step > 0)
    def _():
        copies(prev_ref, 1, False)

    group(1)
    copies(meta_ref, 1, True)
    copies(meta_ref, 0, False)

    @pl.when(step == pl.num_programs(0) - 1)
    def _():
        copies(meta_ref, 1, False)


def _dispatch(meta, zmeta, lpos_t, h2, n_slots, tm_e):
    n_tok = h2.shape[0]
    tm = 2 * MOE_TOK
    assert tm_e <= MOE_BUF and n_tok % tm == 0
    return pl.pallas_call(
        functools.partial(_dispatch_kernel, tm_e=tm_e),
        grid=(n_tok // tm,),
        in_specs=[pl.BlockSpec((2 * MOE_META,), lambda i: (i,), memory_space=pltpu.SMEM),
                  pl.BlockSpec((2 * MOE_META,), lambda i: (jnp.maximum(i - 1, 0),), memory_space=pltpu.SMEM),
                  pl.BlockSpec((MOE_META,), lambda i: (0,), memory_space=pltpu.SMEM),
                  pl.BlockSpec((TOP_K, tm), lambda i: (0, i)),
                  pl.BlockSpec((tm, D_MODEL), lambda i: (i, 0))],
        out_specs=pl.BlockSpec(memory_space=pl.ANY),
        out_shape=jax.ShapeDtypeStruct((n_slots, D_MODEL // 2), U32),
        scratch_shapes=[pltpu.VMEM((MOE_BUF, D_MODEL // 2), U32), pltpu.VMEM((MOE_BUF, D_MODEL // 2), U32),
                        pltpu.VMEM((MOE_PIECE, D_MODEL // 2), U32),
                        pltpu.SemaphoreType.DMA, pltpu.SemaphoreType.DMA],
        compiler_params=_cparams(("arbitrary",)),
        name="moe_dispatch",
    )(meta, meta, zmeta, lpos_t, h2)


def _pack_bf16_pairs(x):
    w = x.shape[1] // 2
    xb = x.astype(BF16).astype(F32)
    lo = lax.shift_right_logical(lax.bitcast_convert_type(xb[:, :w], U32), jnp.uint32(16))
    hi = lax.bitcast_convert_type(xb[:, w:], U32) & jnp.uint32(0xFFFF0000)
    return hi | lo


def _unpack_bf16_pairs(p):
    lo = lax.bitcast_convert_type(lax.shift_left(p, jnp.uint32(16)), F32)
    hi = lax.bitcast_convert_type(p & jnp.uint32(0xFFFF0000), F32)
    return jnp.concatenate([lo, hi], axis=1).astype(BF16)


def _expert_kernel(te_ref, first_ref, nu_ref, xs_ref, wgu_ref, bgu_ref, wd_ref, bd_ref, y_ref,
                   wgu_b_ref, wd_b_ref):
    del te_ref
    i = pl.program_id(0)

    @pl.when(first_ref[i] == 1)
    def _():
        wgu_b_ref[...] = wgu_ref[...].astype(BF16)
        wd_b_ref[...] = wd_ref[...].astype(BF16)

    @pl.when(i < nu_ref[0])
    def _():
        gu = _dot(_unpack_bf16_pairs(xs_ref[...]), wgu_b_ref[...]) + bgu_ref[...]
        g = jnp.minimum(gu[:, :D_EXPERT], SWIGLU_LIMIT)
        lin = jnp.clip(gu[:, D_EXPERT:], -SWIGLU_LIMIT, SWIGLU_LIMIT)
        act = g * _sigmoid(SWIGLU_ALPHA * g) * (lin + 1.0)
        y_ref[...] = _pack_bf16_pairs(_dot(act.astype(BF16), wd_b_ref[...]) + bd_ref[...])

    @pl.when(i >= nu_ref[0])
    def _():
        y_ref[...] = jnp.zeros_like(y_ref)


def _experts(tile_expert, n_used, xs, wgu, bgu, wd, bd, layer, tm):
    n_slots = xs.shape[0]
    n_tiles = n_slots // tm
    first = jnp.concatenate([jnp.ones((1,), I32),
                             (tile_expert[1:] != tile_expert[:-1]).astype(I32)])

    def row(i, te, fi, nu):
        return (jnp.minimum(i, nu[0] - 1), 0)

    grid_spec = pltpu.PrefetchScalarGridSpec(
        num_scalar_prefetch=3,
        grid=(n_tiles,),
        in_specs=[pl.BlockSpec((tm, D_MODEL // 2), row),
                  pl.BlockSpec((None, None, D_MODEL, 2 * D_EXPERT), lambda i, te, fi, nu: (layer, te[i], 0, 0)),
                  pl.BlockSpec((None, None, 1, 2 * D_EXPERT), lambda i, te, fi, nu: (layer, te[i], 0, 0)),
                  pl.BlockSpec((None, None, D_EXPERT, D_MODEL), lambda i, te, fi, nu: (layer, te[i], 0, 0)),
                  pl.BlockSpec((None, None, 1, D_MODEL), lambda i, te, fi, nu: (layer, te[i], 0, 0))],
        out_specs=pl.BlockSpec((tm, D_MODEL // 2), lambda i, te, fi, nu: (i, 0)),
        scratch_shapes=[pltpu.VMEM((D_MODEL, 2 * D_EXPERT), BF16), pltpu.VMEM((D_EXPERT, D_MODEL), BF16)],
    )
    return pl.pallas_call(
        _expert_kernel,
        grid_spec=grid_spec,
        out_shape=jax.ShapeDtypeStruct((n_slots, D_MODEL // 2), U32),
        compiler_params=_cparams(("arbitrary",)),
        name="moe_experts",
    )(tile_expert, first, n_used, xs, wgu, bgu, wd, bd)


def _combine_kernel(meta_ref, next_ref, lpos_ref, gate_ref, x_ref, mod_ref, post_ref, y_ref, xo_ref,
                    buf_ref, buf1_ref, sem, sem1):
    tm = MOE_TOK
    step = pl.program_id(0)
    bufs, sems = (buf_ref, buf1_ref), (sem, sem1)
    rows = lambda ref, r, n=MOE_PIECE: ref.at[pl.ds(r, n)]

    def copies(mref, j, base, start):
        _run_copies(mref, base, lambda loc, slot, n: rows(y_ref, slot, n),
                    lambda loc, slot, n: rows(bufs[j], loc, n), sems[j], start)

    def reduce_tile(j):
        tok = slice(j * tm, (j + 1) * tm)
        s_iota = lax.broadcasted_iota(I32, (tm, MOE_BUF), 1)
        sel = jnp.zeros((tm, MOE_BUF), F32)
        for k in range(TOP_K):
            sel = jnp.where(s_iota == lpos_ref[tok, k:k + 1], gate_ref[tok, k:k + 1], sel)
        y = _dot(sel.astype(BF16), _unpack_bf16_pairs(bufs[j][...]))
        gate2 = mod_ref[:, 5 * D_MODEL:6 * D_MODEL]
        xo_ref[tok, :] = x_ref[tok, :] + gate2 * _rms(y, post_ref[...])

    @pl.when(step == 0)
    def _():
        buf_ref[...] = jnp.zeros_like(buf_ref)
        buf1_ref[...] = jnp.zeros_like(buf1_ref)
        copies(meta_ref, 0, 0, True)

    copies(meta_ref, 1, MOE_META, True)
    copies(meta_ref, 0, 0, False)
    reduce_tile(0)

    @pl.when(step < pl.num_programs(0) - 1)
    def _():
        copies(next_ref, 0, 0, True)

    copies(meta_ref, 1, MOE_META, False)
    reduce_tile(1)


def _combine(meta, lpos, gate, x_mid, mod, mod_row, post_g, y):
    n_tok = x_mid.shape[0]
    tm = 2 * MOE_TOK
    n_steps = n_tok // tm
    return pl.pallas_call(
        _combine_kernel,
        grid=(n_steps,),
        in_specs=[pl.BlockSpec((2 * MOE_META,), lambda i: (i,), memory_space=pltpu.SMEM),
                  pl.BlockSpec((2 * MOE_META,), lambda i: (jnp.minimum(i + 1, n_steps - 1),),
                               memory_space=pltpu.SMEM),
                  pl.BlockSpec((tm, TOP_K), lambda i: (i, 0)),
                  pl.BlockSpec((tm, TOP_K), lambda i: (i, 0)),
                  pl.BlockSpec((tm, D_MODEL), lambda i: (i, 0)),
                  pl.BlockSpec((None, 1, 6 * D_MODEL), lambda i: (mod_row(i * tm), 0, 0)),
                  pl.BlockSpec((1, D_MODEL), lambda i: (0, 0)),
                  pl.BlockSpec(memory_space=pl.ANY)],
        out_specs=pl.BlockSpec((tm, D_MODEL), lambda i: (i, 0)),
        out_shape=jax.ShapeDtypeStruct((n_tok, D_MODEL), F32),
        scratch_shapes=[pltpu.VMEM((MOE_BUF, D_MODEL // 2), U32), pltpu.VMEM((MOE_BUF, D_MODEL // 2), U32),
                        pltpu.SemaphoreType.DMA, pltpu.SemaphoreType.DMA],
        compiler_params=_cparams(("arbitrary",)),
        name="moe_combine",
    )(meta, meta, lpos, gate, x_mid, mod, post_g, y)


def _moe(h2, logits, x_mid, mod, mod_row, lw, tm_e=512):
    n_tok = h2.shape[0]
    n_t = n_tok // MOE_TOK
    gate, lpos, tcnt = _route(logits)
    tcnt = tcnt[:, 0, :N_EXPERTS].astype(I32)
    pieces = (tcnt + MOE_PIECE - 1) // MOE_PIECE
    run_end = jnp.cumsum(pieces, axis=0) * MOE_PIECE
    used = run_end[-1]
    padded = (used + tm_e - 1) // tm_e * tm_e
    pad_end = jnp.cumsum(padded)
    offs = pad_end - padded
    n_tiles = (n_tok * TOP_K + n_t * N_EXPERTS * (MOE_PIECE - 1) + tm_e - 1) // tm_e + N_EXPERTS
    tile_start = jnp.arange(n_tiles, dtype=I32) * tm_e
    tile_expert = jnp.minimum(jnp.sum(pad_end[None, :] <= tile_start[:, None], axis=1),
                              N_EXPERTS - 1).astype(I32)
    n_used = (pad_end[-1:] // tm_e).astype(I32)
    lstart = (jnp.cumsum(pieces, axis=1) - pieces) * MOE_PIECE
    slot_start = offs[None, :] + run_end - pieces * MOE_PIECE

    def table(count, rows, first_row, length):
        end = jnp.cumsum(count, axis=1)
        q = jnp.arange(length, dtype=I32)
        passed = q[None, :, None] >= end[:, None, :]
        owner = passed != jnp.concatenate([jnp.ones_like(passed[..., :1]), passed[..., :-1]], axis=-1)
        within = (q[None, :, None] - (end - count)[:, None, :]) * rows + first_row[:, None, :]
        pick = lambda base: jnp.sum(jnp.where(owner, base[:, None, :] + within, 0), axis=-1)
        return end[:, -1:], pick(lstart), pick(slot_start)

    counts, tables, left, done_rows = [], [], pieces, jnp.zeros_like(pieces)
    for mult, length, _ in MOE_META_TABLES:
        n, local, slot = table(left // mult, mult * MOE_PIECE, done_rows, length)
        counts.append(n)
        tables += [local, slot]
        done_rows = done_rows + (left // mult) * mult * MOE_PIECE
        left = left % mult
    meta = jnp.concatenate(counts + tables, axis=1)
    meta = jnp.pad(meta, ((0, 0), (0, MOE_META - meta.shape[1]))).reshape(-1).astype(I32)
    z0 = offs + used
    tail = jnp.stack([n_used[0], n_tiles - n_used[0]])
    zmeta = jnp.concatenate([z0, (pad_end - z0) // MOE_PIECE, tail,
                             jnp.zeros((MOE_META - 2 * N_EXPERTS - 2,), I32)]).astype(I32)
    xs = _dispatch(meta, zmeta, lpos.T, h2, n_tiles * tm_e, tm_e)
    y = _experts(tile_expert, n_used, xs, lw["exp_w_gu"], lw["exp_b_gu"], lw["exp_w_down"],
                 lw["exp_b_down"], lw["layer"], tm_e)
    return _combine(meta, lpos, gate, x_mid, mod, mod_row, lw["norm_post_ffn"], y)


def _split_w_in(w_in):
    offs, o = {}, 0
    for name, width in (("dn_k", 1024), ("dn_v", 1024), ("dn_a", 16), ("dn_b", 16), ("at_k", 128),
                        ("at_v", 128), ("dn_q", 1024), ("dn_g", 1024), ("at_q", 1024),
                        ("sg_u", 1024), ("sg_v", 1024), ("gates", 3072)):
        offs[name] = (o, o + width)
        o += width
    sl = lambda n: w_in[:, offs[n][0]:offs[n][1]]
    w_main = jnp.concatenate([sl(n) for n in ("dn_k", "dn_v", "dn_q", "dn_g", "at_q", "sg_u", "sg_v",
                                              "gates")], axis=1).astype(BF16)
    pad = jnp.zeros((w_in.shape[0], N_SMALL_COLS - 2 * LANES - 4 * DN_HEADS), w_in.dtype)
    w_small = jnp.concatenate([sl("at_k"), sl("at_v"), sl("dn_a"), sl("dn_b"), pad], axis=1).astype(BF16)
    return w_main, w_small


def _dn_gates(small, B, T):
    ab = small[:, 2 * LANES:2 * LANES + 4 * DN_HEADS].reshape(B, T, 2, 2, DN_HEADS)
    col = jnp.transpose(ab, (3, 0, 1, 2, 4)).reshape(2, B, T, 2 * DN_HEADS)
    row = jnp.transpose(col.reshape(2, B, T // DN_CHUNK, DN_CHUNK, 2 * DN_HEADS), (0, 1, 2, 4, 3))
    return col, row


def kernel(x, c, ctx, c_ctx, w_mod, b_mod, norm_pre_mix, norm_post_mix, norm_pre_ffn, norm_post_ffn, w_in, sg_ln_g, sg_ln_b, sg_w, sg_b, dn_conv_w, dn_a_log, dn_dt_bias, dn_norm_g, at_sinks, w_proj_sg, w_proj_dn, w_proj_at, w_out, router_w, router_b, exp_w_gu, exp_b_gu, exp_w_down, exp_b_down):
    B, S, D = x.shape
    L = ctx.shape[1]
    depth = w_mod.shape[0]
    assert D == D_MODEL and S % GRID_W == 0
    n_lat, n_ctx = B * S, B * L

    rows = (B + 1 + 7) // 8 * 8
    cvec = jnp.zeros((rows, D), F32).at[:B].set(c).at[B].set(c_ctx)
    mod_all = _modulation(cvec, w_mod, b_mod)
    tables = _rope_tables(S)

    lat_row = lambda t: t // S
    ctx_row = lambda t: B
    all_row = lambda t: jnp.where(t < n_lat, t // S, B)

    xl = x.reshape(n_lat, D)
    xc = ctx.reshape(n_ctx, D)
    for l in range(depth):
        need_ctx_out = l < depth - 1
        mod = mod_all[l].reshape(rows, 1, 6 * D)
        w_main, w_small = _split_w_in(w_in[l])
        lw = {
            "dn_norm_g": dn_norm_g[l].reshape(1, -1),
            "norm_post_mix": norm_post_mix[l].reshape(1, -1),
            "norm_pre_ffn": norm_pre_ffn[l].reshape(1, -1),
            "norm_post_ffn": norm_post_ffn[l].reshape(1, -1),
            "w_proj_sg": w_proj_sg[l].astype(BF16), "w_proj_dn": w_proj_dn[l].astype(BF16),
            "w_proj_at": w_proj_at[l].astype(BF16), "w_out": w_out[l].astype(BF16),
            "router_w": jnp.pad(router_w[l], ((0, 0), (0, LANES - N_EXPERTS))).astype(BF16),
            "router_b": jnp.pad(router_b[l], (0, LANES - N_EXPERTS),
                                constant_values=NEG_BIG).reshape(1, -1),
            "layer": l,
            "exp_w_gu": exp_w_gu, "exp_b_gu": exp_b_gu.reshape(depth, N_EXPERTS, 1, -1),
            "exp_w_down": exp_w_down, "exp_b_down": exp_b_down.reshape(depth, N_EXPERTS, 1, -1),
        }
        pre_g = norm_pre_mix[l].reshape(1, -1)
        main, small = _inproj(xl, mod, lat_row, pre_g, w_main, w_small, min(1024, S))
        w_main_c = w_main if need_ctx_out else w_main[:, :N_CTX_MAIN_COLS]
        main_c, small_c = _inproj(xc, mod, ctx_row, pre_g, w_main_c, w_small, min(1024, n_ctx))

        sg_args = (sg_ln_g[l].reshape(1, -1), sg_ln_b[l].reshape(1, -1), sg_w[l].astype(BF16),
                   sg_b[l].T)
        ysg = _sgu(main, *sg_args)

        gcol_c, grow_c = _dn_gates(small_c, B, L)
        gcol, grow = _dn_gates(small, B, S)
        s0 = jnp.zeros((2, B, DN_HEADS, DN_DIM, DN_DIM), F32)
        of_c, ob_c, s_ctx = _deltanet(main_c.reshape(B, L, -1), gcol_c, grow_c, dn_conv_w[l], dn_a_log[l],
                                      dn_dt_bias[l], s0, need_ctx_out)
        of_l, ob_l, _ = _deltanet(main.reshape(B, S, -1), gcol, grow, dn_conv_w[l], dn_a_log[l],
                                  dn_dt_bias[l], s_ctx, True)

        sinks = jnp.repeat(at_sinks[l], AT_BLOCK).reshape(-1, 1)
        q_r, k_r, v_r = _rope(main, small, tables, S)
        yat = _attention_local(q_r, k_r, v_r, small_c, sinks, B, S, L)

        x_mid, h2, logits = _merge(ysg, of_l.reshape(n_lat, D), ob_l.reshape(n_lat, D), main, yat, xl, mod,
                                   lat_row, lw)
        if need_ctx_out:
            ysg_c = _sgu(main_c, *sg_args)
            yat_c = _attention_ctx(main_c, small_c, sinks, B, L)
            xc_mid, h2c, logits_c = _merge(ysg_c, of_c.reshape(n_ctx, D), ob_c.reshape(n_ctx, D), main_c,
                                           yat_c, xc, mod, ctx_row, lw)
            x_mid = jnp.concatenate([x_mid, xc_mid], axis=0)
            h2 = jnp.concatenate([h2, h2c], axis=0)
            logits = jnp.concatenate([logits, logits_c], axis=0)
            xo = _moe(h2, logits, x_mid, mod, all_row, lw)
            xl, xc = xo[:n_lat], xo[n_lat:]
        else:
            xl = _moe(h2, logits, x_mid, mod, lat_row, lw)
    return xl.reshape(B, S, D)
```
